```python
import math
import jax, jax.numpy as jnp
from jax import lax
import numpy as np

D_MODEL = 1024
BATCH = 4
SEQ = 4096
DEPTH = 2
DEC_BATCH = 128
DEC_SEQ = 1
PAST_LEN = 2048
PAGE_SIZE = 128

HEAD_DIM = 64
H_RET = 8
H_NSA = 8
N_KV = 2
R_GRP = H_NSA // N_KV
RET_W = H_RET * HEAD_DIM
NSA_QW = H_NSA * HEAD_DIM
MIX_WIDTH = RET_W + NSA_QW
KV_W = 2 * N_KV * HEAD_DIM
N_IN = 4 * RET_W + NSA_QW + 3 * KV_W + 3 * H_NSA
SPLITS = [int(s) for s in np.cumsum([RET_W, RET_W, RET_W, RET_W, NSA_QW, KV_W, KV_W, KV_W])]
RET_CHUNK = 128
CMP_STRIDE = 16
L_CMP = 2 * CMP_STRIDE
CMP_HIDDEN = 2 * HEAD_DIM
SEL_BLOCK = 64
N_SELECT = 16
WINDOW = 512
Q_BLOCK = 128
N_BUCKETS = 32
MAX_DISTANCE = 128
D_FF = 2816
N_EXPERTS = 8
TOP_K = 2
D_FF_EXPERT = 3584
P_DIM = 256
ROPE_BASE = 10000.0
EPS = 1e-6
FORCED_SCORE = 1e4
N_DENSE = (DEPTH + 1) // 2
N_MOE = DEPTH // 2

kernel_name = 'hymba_retnet_nsa_decoder_step'


def rmsnorm(x, g):
    xf = x.astype(jnp.float32)
    y = xf * lax.rsqrt(jnp.mean(xf * xf, axis=-1, keepdims=True) + EPS)
    return (y * g.astype(jnp.float32)).astype(x.dtype)


def rotary(x, pos):
    half = x.shape[-1] // 2
    inv = ROPE_BASE ** (-jnp.arange(half, dtype=jnp.float32) / half)
    ang = pos.astype(jnp.float32)[:, None] * inv[None, :]
    cos, sin = jnp.cos(ang)[:, None, :], jnp.sin(ang)[:, None, :]
    xf = x.astype(jnp.float32)
    x1, x2 = xf[..., :half], xf[..., half:]
    return jnp.concatenate([x1 * cos - x2 * sin, x1 * sin + x2 * cos], axis=-1).astype(x.dtype)


def t5_bucket(dist):
    n = jnp.maximum(dist, 0)
    max_exact = N_BUCKETS // 2
    nf = jnp.maximum(n, 1).astype(jnp.float32)
    large = max_exact + (jnp.log(nf / max_exact) / math.log(MAX_DISTANCE / max_exact)
                         * (N_BUCKETS - max_exact)).astype(jnp.int32)
    large = jnp.minimum(large, N_BUCKETS - 1)
    return jnp.where(n < max_exact, n, large)


def masked_softmax(s, mask):
    s = jnp.where(mask, s.astype(jnp.float32), -1e30)
    m = jnp.max(s, axis=-1, keepdims=True)
    e = jnp.where(mask, jnp.exp(s - m), 0.0)
    return e / jnp.maximum(jnp.sum(e, axis=-1, keepdims=True), jnp.finfo(jnp.float32).tiny)


def retention_chunkwise(q, k, v, state0):
    B, T, H, _ = q.shape
    c = RET_CHUNK if T % RET_CHUNK == 0 else T
    n = T // c
    log_g = jnp.log1p(-jnp.exp2(-5.0 - jnp.arange(H, dtype=jnp.float32)))
    j = jnp.arange(c, dtype=jnp.float32)
    diff = j[:, None] - j[None, :]
    intra = jnp.where(diff >= 0, jnp.exp(jnp.maximum(diff, 0.0)[None] * log_g[:, None, None]), 0.0)
    q_decay = jnp.exp((j[:, None] + 1.0) * log_g[None, :])
    k_decay = jnp.exp((c - 1.0 - j[:, None]) * log_g[None, :])
    c_decay = jnp.exp(c * log_g)

    def to_chunks(a):
        return jnp.moveaxis(a.astype(jnp.float32).reshape(B, n, c, H, a.shape[-1]), 1, 0)

    def step(S, xs):
        qc, kc, vc = xs
        inner = jnp.einsum('bihd,bjhd->bhij', qc, kc) * intra
        o = (jnp.einsum('bhij,bjhe->bihe', inner, vc)
             + jnp.einsum('bihd,bhde->bihe', qc, S) * q_decay[None, :, :, None])
        S = S * c_decay[None, :, None, None] + jnp.einsum('bjhd,bjhe->bhde', kc * k_decay[None, :, :, None], vc)
        return S, o

    S, o = lax.scan(step, state0.astype(jnp.float32), (to_chunks(q), to_chunks(k), to_chunks(v)))
    return jnp.moveaxis(o, 0, 1).reshape(B, T, H, v.shape[-1]), S


def compress_blocks(rows, pe, w1, w2):
    B, L, G, dh = rows.shape
    n_sub = L // CMP_STRIDE
    sub = rows[:, :n_sub * CMP_STRIDE].reshape(B, n_sub, CMP_STRIDE, G, dh)
    blocks = jnp.concatenate([sub[:, :-1], sub[:, 1:]], axis=2) + pe[None, None, :, None, :]
    flat = blocks.transpose(0, 1, 3, 2, 4).reshape(B, n_sub - 1, G, L_CMP * dh)
    return jax.nn.gelu(flat @ w1) @ w2


def nsa_query_block(q, gate, qpos, kc, vc, c_end, ks_blk, vs_blk, kw, vw, kpos_w, table):
    B, Tq, G, R, dh = q.shape
    scale = dh ** -0.5
    bias_c = table[t5_bucket(qpos[:, None] - c_end[None, :])].transpose(2, 3, 0, 1)
    s_c = jnp.einsum('bqgrd,bngd->bgrqn', q, kc).astype(jnp.float32) * scale + bias_c
    p_c = masked_softmax(s_c, c_end[None, :] <= qpos[:, None])
    o_c = jnp.einsum('bgrqn,bngd->bqgrd', p_c.astype(vc.dtype), vc)
    n_sel = ks_blk.shape[2]
    sub = SEL_BLOCK // CMP_STRIDE
    imp = jnp.sum(p_c, axis=2)
    imp = jnp.pad(imp, ((0, 0), (0, 0), (0, 0), (0, n_sel * sub - imp.shape[-1])))
    imp = imp.reshape(B, G, Tq, n_sel, sub).sum(-1)
    blk = jnp.arange(n_sel)
    cur = qpos // SEL_BLOCK
    forced = (blk[None, :] == 0) | (blk[None, :] == cur[:, None]) | (blk[None, :] == cur[:, None] - 1)
    valid = blk[None, :] * SEL_BLOCK <= qpos[:, None]
    score = jnp.where(forced, FORCED_SCORE, jnp.where(valid, imp, -1.0))
    n_top = min(N_SELECT, n_sel)
    _, idx = lax.top_k(score, n_top)
    gidx = idx.reshape(B, G, Tq * n_top, 1)
    ks = jnp.take_along_axis(ks_blk, gidx, axis=2).reshape(B, G, Tq, n_top, SEL_BLOCK, dh)
    vs = jnp.take_along_axis(vs_blk, gidx, axis=2).reshape(B, G, Tq, n_top, SEL_BLOCK, dh)
    kpos_s = idx[..., None] * SEL_BLOCK + jnp.arange(SEL_BLOCK)
    bucket_s = t5_bucket(qpos[None, None, :, None, None] - kpos_s)
    bias_s = jax.vmap(lambda tab, bk: tab[bk], in_axes=(0, 1), out_axes=1)(table.transpose(1, 0, 2), bucket_s)
    s_s = jnp.einsum('bqgrd,bgqkjd->bgrqkj', q, ks).astype(jnp.float32) * scale + jnp.moveaxis(bias_s, -1, 2)
    m_s = (kpos_s <= qpos[None, None, :, None, None])[:, :, None]
    p_s = masked_softmax(s_s.reshape(B, G, R, Tq, -1), m_s.reshape(B, G, 1, Tq, -1)).reshape(s_s.shape)
    o_s = jnp.einsum('bgrqkj,bgqkjd->bqgrd', p_s.astype(vs.dtype), vs)
    dist = qpos[:, None] - kpos_w[None, :]
    bias_w = table[t5_bucket(dist)].transpose(2, 3, 0, 1)
    s_w = jnp.einsum('bqgrd,bsgd->bgrqs', q, kw).astype(jnp.float32) * scale + bias_w
    p_w = masked_softmax(s_w, (dist >= 0) & (dist < WINDOW) & (kpos_w[None, :] >= 0))
    o_w = jnp.einsum('bgrqs,bsgd->bqgrd', p_w.astype(vw.dtype), vw)
    return gate[..., 0:1] * o_c + gate[..., 1:2] * o_s + gate[..., 2:3] * o_w


def token_mixer(hn, w_in_l, ret_gain_l, cmp_pe_l, cmp_w1_l, cmp_w2_l, table,
                ret_state0, win_past, cmp_past, sel_past):
    B, T, _ = hn.shape
    past = cmp_past.shape[1]
    pos = past + jnp.arange(T)
    z = hn @ w_in_l
    rq, rk, rv, rg, nq, kvc, kvs, kvw, ng = jnp.split(z, SPLITS, axis=-1)
    rq = rotary(rq.reshape(B, T, H_RET, HEAD_DIM), pos)
    rk = rotary(rk.reshape(B, T, H_RET, HEAD_DIM), pos) * (HEAD_DIM ** -0.5)
    ret_o, ret_state = retention_chunkwise(rq, rk, rv.reshape(B, T, H_RET, HEAD_DIM), ret_state0)
    ret_y = rmsnorm(ret_o, ret_gain_l).astype(hn.dtype) * jax.nn.silu(rg.reshape(B, T, H_RET, HEAD_DIM))
    kv_shape = (B, T, 2, N_KV, HEAD_DIM)
    kvc, kvs, kvw = kvc.reshape(kv_shape), kvs.reshape(kv_shape), kvw.reshape(kv_shape)
    cmp_all = jnp.concatenate([cmp_past.astype(kvc.dtype), kvc], axis=1)
    sel_all = jnp.concatenate([sel_past.astype(kvs.dtype), kvs], axis=1)
    win_all = jnp.concatenate([win_past.astype(kvw.dtype), kvw], axis=1)
    L = past + T
    kc = compress_blocks(cmp_all[:, :, 0], cmp_pe_l[0], cmp_w1_l[0], cmp_w2_l[0])
    vc = compress_blocks(cmp_all[:, :, 1], cmp_pe_l[1], cmp_w1_l[1], cmp_w2_l[1])
    c_end = jnp.arange(kc.shape[1]) * CMP_STRIDE + L_CMP - 1
    n_sel = -(-L // SEL_BLOCK)
    sel_pad = jnp.pad(sel_all, ((0, 0), (0, n_sel * SEL_BLOCK - L), (0, 0), (0, 0), (0, 0)))
    sel_blk = sel_pad.reshape(B, n_sel, SEL_BLOCK, 2, N_KV, HEAD_DIM).transpose(3, 0, 4, 1, 2, 5)
    sel_blk = sel_blk.reshape(2, B, N_KV, n_sel, SEL_BLOCK * HEAD_DIM)
    q = nq.reshape(B, T, N_KV, R_GRP, HEAD_DIM)
    gate = jax.nn.sigmoid(ng.astype(jnp.float32)).reshape(B, T, N_KV, R_GRP, 3).astype(hn.dtype)
    off = win_all.shape[1] - T
    win_pad = jnp.pad(win_all, ((0, 0), (WINDOW, 0), (0, 0), (0, 0), (0, 0)))
    qb = Q_BLOCK if T % Q_BLOCK == 0 else T

    def body(bi):
        qs = bi * qb
        start = off + qs
        kw_blk = lax.dynamic_slice_in_dim(win_pad, start, WINDOW + qb, axis=1)
        kpos_w = past - off - WINDOW + start + jnp.arange(WINDOW + qb)
        return nsa_query_block(lax.dynamic_slice_in_dim(q, qs, qb, axis=1),
                               lax.dynamic_slice_in_dim(gate, qs, qb, axis=1),
                               past + qs + jnp.arange(qb), kc, vc, c_end, sel_blk[0], sel_blk[1],
                               kw_blk[:, :, 0], kw_blk[:, :, 1], kpos_w, table)

    o = lax.map(body, jnp.arange(T // qb))
    nsa_y = jnp.moveaxis(o, 0, 1).reshape(B, T, NSA_QW)
    mix = jnp.concatenate([ret_y.reshape(B, T, RET_W), nsa_y.astype(hn.dtype)], axis=-1)
    return mix, ret_state, win_all, kvc, kvs


def swiglu(x, w1, w3, w2):
    return (jax.nn.silu(x @ w1) * (x @ w3)) @ w2


def moe_swiglu(x, router_l, w1, w3, w2):
    logits = (x @ router_l).astype(jnp.float32)
    top_v, top_i = lax.top_k(logits, TOP_K)
    top_w = jax.nn.softmax(top_v, axis=-1)
    gates = jnp.sum(jax.nn.one_hot(top_i, N_EXPERTS, dtype=jnp.float32) * top_w[..., None], axis=-2).astype(x.dtype)
    y = jnp.zeros_like(x)
    for e in range(N_EXPERTS):
        y = y + gates[..., e:e + 1] * swiglu(x, w1[e], w3[e], w2[e])
    return y


def decoder_layer(h, p_i, ret_state0, win_past, cmp_past, sel_past, table, norm_mix_l, w_in_l, ret_gain_l,
                  cmp_pe_l, cmp_w1_l, cmp_w2_l, w_out_l, norm_ffn_l, ffn_fn, ple_norm_l, ple_gate_l, ple_proj_l):
    mix, ret_state, win_all, kvc, kvs = token_mixer(rmsnorm(h, norm_mix_l), w_in_l, ret_gain_l, cmp_pe_l, cmp_w1_l,
                                                    cmp_w2_l, table, ret_state0, win_past, cmp_past, sel_past)
    h = h + mix @ w_out_l
    h = h + ffn_fn(rmsnorm(h, norm_ffn_l))
    g = jax.nn.sigmoid((rmsnorm(h, ple_norm_l) @ ple_gate_l).astype(jnp.float32)).astype(h.dtype)
    h = h + g * (p_i.astype(h.dtype) @ ple_proj_l)
    return h, ret_state, win_all, kvc, kvs


def window_state(win_all, w_buf):
    pad = max(w_buf - win_all.shape[1], 0)
    if pad:
        win_all = jnp.pad(win_all, ((0, 0), (pad, 0), (0, 0), (0, 0), (0, 0)))
    return win_all[:, win_all.shape[1] - w_buf:]


def setup_inputs(seed: int = 0) -> dict:
    key = jax.random.key(seed)
    ks = jax.random.split(key, 32)
    f32 = jnp.float32

    def nrm(k, shape, scale):
        return jax.random.normal(k, shape, f32) * scale

    def gain(k, shape):
        return 1.0 + 0.1 * jax.random.normal(k, shape, f32)

    n_pages = PAST_LEN // PAGE_SIZE
    n_pool = (DEC_BATCH * n_pages * 5) // 4
    w_buf = min(WINDOW, PAST_LEN)
    page_table = jax.random.permutation(ks[0], n_pool)[:DEC_BATCH * n_pages].reshape(DEC_BATCH, n_pages).astype(jnp.int32)
    row = (2, N_KV, HEAD_DIM)
    return {
        'x_prompt': nrm(ks[1], (BATCH, SEQ, D_MODEL), 1.0),
        'x_sample': nrm(ks[2], (DEC_BATCH, DEC_SEQ, D_MODEL), 1.0),
        'p_prompt': nrm(ks[3], (DEPTH, BATCH, SEQ, P_DIM), 1.0),
        'p_sample': nrm(ks[4], (DEPTH, DEC_BATCH, DEC_SEQ, P_DIM), 1.0),
        'state_ret': nrm(ks[5], (DEPTH, DEC_BATCH, H_RET, HEAD_DIM, HEAD_DIM), 1.0),
        'cache_win_kv': nrm(ks[6], (DEPTH, DEC_BATCH, w_buf) + row, 1.0),
        'cache_cmp_kv': nrm(ks[7], (DEPTH, n_pool, PAGE_SIZE) + row, 1.0),
        'cache_sel_kv': nrm(ks[8], (DEPTH, n_pool, PAGE_SIZE) + row, 1.0),
        'page_table': page_table,
        'rel_bias': nrm(ks[9], (N_BUCKETS, H_NSA), 0.5),
        'norm_mix': gain(ks[10], (DEPTH, D_MODEL)),
        'w_in': nrm(ks[11], (DEPTH, D_MODEL, N_IN), D_MODEL ** -0.5),
        'ret_gain': gain(ks[12], (DEPTH, H_RET, HEAD_DIM)),
        'cmp_pe': nrm(ks[13], (DEPTH, 2, L_CMP, HEAD_DIM), 0.1),
        'cmp_w1': nrm(ks[14], (DEPTH, 2, L_CMP * HEAD_DIM, CMP_HIDDEN), (L_CMP * HEAD_DIM) ** -0.5),
        'cmp_w2': nrm(ks[15], (DEPTH, 2, CMP_HIDDEN, HEAD_DIM), CMP_HIDDEN ** -0.5),
        'w_out': nrm(ks[16], (DEPTH, MIX_WIDTH, D_MODEL), MIX_WIDTH ** -0.5),
        'norm_ffn': gain(ks[17], (DEPTH, D_MODEL)),
        'ffn_w1': nrm(ks[18], (N_DENSE, D_MODEL, D_FF), D_MODEL ** -0.5),
        'ffn_w3': nrm(ks[19], (N_DENSE, D_MODEL, D_FF), D_MODEL ** -0.5),
        'ffn_w2': nrm(ks[20], (N_DENSE, D_FF, D_MODEL), D_FF ** -0.5),
        'router': nrm(ks[21], (N_MOE, D_MODEL, N_EXPERTS), D_MODEL ** -0.5),
        'moe_w1': nrm(ks[22], (N_MOE, N_EXPERTS, D_MODEL, D_FF_EXPERT), D_MODEL ** -0.5),
        'moe_w3': nrm(ks[23], (N_MOE, N_EXPERTS, D_MODEL, D_FF_EXPERT), D_MODEL ** -0.5),
        'moe_w2': nrm(ks[24], (N_MOE, N_EXPERTS, D_FF_EXPERT, D_MODEL), D_FF_EXPERT ** -0.5),
        'ple_norm': gain(ks[25], (DEPTH, D_MODEL)),
        'ple_gate': nrm(ks[26], (DEPTH, D_MODEL, D_MODEL), D_MODEL ** -0.5),
        'ple_proj': nrm(ks[27], (DEPTH, P_DIM, D_MODEL), P_DIM ** -0.5),
        'norm_final': gain(ks[28], (D_MODEL,)),
    }


def reference(x_prompt, x_sample, p_prompt, p_sample, state_ret, cache_win_kv, cache_cmp_kv, cache_sel_kv,
              page_table, rel_bias, norm_mix, w_in, ret_gain, cmp_pe, cmp_w1, cmp_w2, w_out, norm_ffn,
              ffn_w1, ffn_w3, ffn_w2, router, moe_w1, moe_w3, moe_w2, ple_norm, ple_gate, ple_proj, norm_final):
    b_p = x_prompt.shape[0]
    b_s = x_sample.shape[0]
    past = page_table.shape[1] * PAGE_SIZE
    w_buf = cache_win_kv.shape[2]
    row = (2, N_KV, HEAD_DIM)
    table = rel_bias.reshape(N_BUCKETS, N_KV, R_GRP)
    h_p, h_s = x_prompt, x_sample
    rp, rs, wp, ws, cp, cs, sp, ss = [], [], [], [], [], [], [], []
    for i in range(DEPTH):
        j = i // 2
        if i % 2 == 0:
            ffn = lambda u, j=j: swiglu(u, ffn_w1[j], ffn_w3[j], ffn_w2[j])
        else:
            ffn = lambda u, j=j: moe_swiglu(u, router[j], moe_w1[j], moe_w3[j], moe_w2[j])
        lw = (table, norm_mix[i], w_in[i], ret_gain[i], cmp_pe[i], cmp_w1[i], cmp_w2[i], w_out[i], norm_ffn[i],
              ffn, ple_norm[i], ple_gate[i], ple_proj[i])
        empty = jnp.zeros((b_p, 0) + row, x_prompt.dtype)
        h_p, r_new, win_all, kvc, kvs = decoder_layer(
            h_p, p_prompt[i], jnp.zeros((b_p, H_RET, HEAD_DIM, HEAD_DIM), jnp.float32), empty, empty, empty, *lw)
        rp.append(r_new); wp.append(window_state(win_all, w_buf)); cp.append(kvc); sp.append(kvs)
        cmp_past = cache_cmp_kv[i][page_table].reshape((b_s, past) + row)
        sel_past = cache_sel_kv[i][page_table].reshape((b_s, past) + row)
        h_s, r_new_s, win_all_s, kvc_s, kvs_s = decoder_layer(
            h_s, p_sample[i], state_ret[i], cache_win_kv[i], cmp_past, sel_past, *lw)
        rs.append(r_new_s); ws.append(window_state(win_all_s, w_buf)); cs.append(kvc_s); ss.append(kvs_s)
    y_prompt = rmsnorm(h_p, norm_final)
    y_sample = rmsnorm(h_s, norm_final)
    return (y_prompt, y_sample, jnp.stack(rp), jnp.stack(rs), jnp.stack(wp), jnp.stack(ws),
            jnp.stack(cp), jnp.stack(cs), jnp.stack(sp), jnp.stack(ss))
```

```python
import functools
import math

import numpy as np
import jax
import jax.numpy as jnp
from jax import lax
from jax.experimental import pallas as pl
from jax.experimental.pallas import tpu as pltpu

F32 = jnp.float32
BF16 = jnp.bfloat16
I32 = jnp.int32

LANES = 128
SUBLANES = 8
VMEM_LIMIT_BYTES = 56 * 1024 * 1024

D_MODEL = 1024
HEAD_DIM = 64
H_RET = 8
H_NSA = 8
N_KV = 2
R_GRP = H_NSA // N_KV
RET_W = H_RET * HEAD_DIM
NSA_QW = H_NSA * HEAD_DIM
KV_W = 2 * N_KV * HEAD_DIM
N_IN = 4 * RET_W + NSA_QW + 3 * KV_W + 3 * H_NSA
N_IN_PAD = 3584
COL_RQ, COL_RK, COL_RV, COL_RG = 0, RET_W, 2 * RET_W, 3 * RET_W
COL_NQ = 4 * RET_W
COL_KVC = COL_NQ + NSA_QW
COL_KVS = COL_KVC + KV_W
COL_KVW = COL_KVS + KV_W
COL_NG = COL_KVW + KV_W
RET_CHUNK = 128
CMP_STRIDE = 16
L_CMP = 2 * CMP_STRIDE
CMP_HIDDEN = 2 * HEAD_DIM
SEL_BLOCK = 64
N_SELECT = 16
WINDOW = 512
Q_BLOCK = 128
N_BUCKETS = 32
MAX_DISTANCE = 128
N_EXPERTS = 8
TOP_K = 2
P_DIM = 256
PAGE_SIZE = 128
ROPE_BASE = 10000.0
EPS = 1e-6
FORCED_SCORE = 1e4
NEG = -1e30
TINY = float(np.finfo(np.float32).tiny)
SEL_LANES = 64
FAR_REL = 2
KV_CHUNK = 128


def _cparams(n_grid):
    return pltpu.CompilerParams(dimension_semantics=("arbitrary",) * n_grid,
                                vmem_limit_bytes=VMEM_LIMIT_BYTES)


def _dot(a, b):
    return jnp.dot(a, b, preferred_element_type=F32)


def _dot_nt(a, b):
    return lax.dot_general(a, b, (((1,), (1,)), ((), ())), preferred_element_type=F32)


def _split3(x):
    hi = x.astype(BF16)
    r = x - hi.astype(F32)
    mid = r.astype(BF16)
    lo = (r - mid.astype(F32)).astype(BF16)
    return hi, mid, lo


def _dot3(x, b01):
    hi, mid, lo = _split3(x)
    return _dot(hi, b01) + _dot(mid, b01) + _dot(lo, b01)


def _dot6(x, w):
    xh, xm, xl = _split3(x)
    wh, wm, wl = _split3(w)
    return (_dot(xh, wh) + _dot(xh, wm) + _dot(xm, wh)
            + _dot(xm, wm) + _dot(xh, wl) + _dot(xl, wh))


def _sigmoid(x):
    return 1.0 / (1.0 + jnp.exp(-x))


def _rms(x, g):
    return x * lax.rsqrt(jnp.mean(x * x, axis=-1, keepdims=True) + EPS) * g


def _row_tile(m):
    for t in (512, 256, 128, 64, 32, 16, 8):
        if m % t == 0:
            return t
    raise ValueError(f"row count {m} not a multiple of 8")


def _rms_matmul_kernel(x_ref, g_ref, w_ref, o_ref):
    o_ref[...] = _dot(_rms(x_ref[...], g_ref[...]).astype(BF16), w_ref[...])


def rms_matmul(x, g, w):
    m, d = x.shape
    n = w.shape[1]
    tm = min(_row_tile(m), 256)
    return pl.pallas_call(
        _rms_matmul_kernel,
        grid=(m // tm,),
        in_specs=[pl.BlockSpec((tm, d), lambda i: (i, 0)),
                  pl.BlockSpec((1, d), lambda i: (0, 0)),
                  pl.BlockSpec((d, n), lambda i: (0, 0))],
        out_specs=pl.BlockSpec((tm, n), lambda i: (i, 0)),
        out_shape=jax.ShapeDtypeStruct((m, n), F32),
        compiler_params=_cparams(1),
    )(x, g.reshape(1, d), w)


def _router_kernel(x_ref, g_ref, w_ref, o_ref):
    o_ref[...] = _dot6(_rms(x_ref[...], g_ref[...]), w_ref[...])


def router_logits(x, g, w):
    m, d = x.shape
    n = w.shape[1]
    tm = min(_row_tile(m), 256)
    return pl.pallas_call(
        _router_kernel,
        grid=(m // tm,),
        in_specs=[pl.BlockSpec((tm, d), lambda i: (i, 0)),
                  pl.BlockSpec((1, d), lambda i: (0, 0)),
                  pl.BlockSpec((d, n), lambda i: (0, 0))],
        out_specs=pl.BlockSpec((tm, n), lambda i: (i, 0)),
        out_shape=jax.ShapeDtypeStruct((m, n), F32),
        compiler_params=_cparams(1),
    )(x, g.reshape(1, d), w)


def _outproj_kernel(h_ref, a_ref, b_ref, w_ref, o_ref):
    ka = a_ref.shape[1]
    o_ref[...] = (h_ref[...] + _dot(a_ref[...].astype(BF16), w_ref[0:ka, :])
                  + _dot(b_ref[...].astype(BF16), w_ref[ka:, :]))


def out_proj(h, a, b, w):
    m, d = h.shape
    ka, kb = a.shape[1], b.shape[1]
    tm = _row_tile(m)
    return pl.pallas_call(
        _outproj_kernel,
        grid=(m // tm,),
        in_specs=[pl.BlockSpec((tm, d), lambda i: (i, 0)),
                  pl.BlockSpec((tm, ka), lambda i: (i, 0)),
                  pl.BlockSpec((tm, kb), lambda i: (i, 0)),
                  pl.BlockSpec((ka + kb, d), lambda i: (0, 0))],
        out_specs=pl.BlockSpec((tm, d), lambda i: (i, 0)),
        out_shape=jax.ShapeDtypeStruct((m, d), F32),
        compiler_params=_cparams(1),
    )(h, a, b, w)


def _ffn_kernel(te_ref, tv_ref, x_ref, g_ref, gate_ref, w1_ref, w3_ref, w2_ref, o_ref,
                xn_ref, acc_ref, *, residual):
    i = pl.program_id(0)
    j = pl.program_id(1)

    @pl.when(j == 0)
    def _():
        xn_ref[...] = _rms(x_ref[...], g_ref[...]).astype(BF16)
        acc_ref[...] = jnp.zeros_like(acc_ref)

    @pl.when(tv_ref[i] != 0)
    def _():
        xn = xn_ref[...]
        a = _dot(xn, w1_ref[...])
        b = _dot(xn, w3_ref[...])
        hid = (a * _sigmoid(a)) * b
        acc_ref[...] += _dot(hid.astype(BF16), w2_ref[...])

    @pl.when(j == pl.num_programs(1) - 1)
    def _():
        y = acc_ref[...] * gate_ref[...]
        if residual:
            y = y + x_ref[...]
        o_ref[...] = y


def grouped_ffn(x, g, row_gate, tile_expert, tile_valid, w1, w3, w2, *, tm, tf, residual):
    m, d = x.shape
    f = w1.shape[2]
    grid_spec = pltpu.PrefetchScalarGridSpec(
        num_scalar_prefetch=2,
        grid=(m // tm, f // tf),
        in_specs=[pl.BlockSpec((tm, d), lambda i, j, te, tv: (i, 0)),
                  pl.BlockSpec((1, d), lambda i, j, te, tv: (0, 0)),
                  pl.BlockSpec((tm, 1), lambda i, j, te, tv: (i, 0)),
                  pl.BlockSpec((None, d, tf), lambda i, j, te, tv: (te[i], 0, j)),
                  pl.BlockSpec((None, d, tf), lambda i, j, te, tv: (te[i], 0, j)),
                  pl.BlockSpec((None, tf, d), lambda i, j, te, tv: (te[i], j, 0))],
        out_specs=pl.BlockSpec((tm, d), lambda i, j, te, tv: (i, 0)),
        scratch_shapes=[pltpu.VMEM((tm, d), BF16), pltpu.VMEM((tm, d), F32)],
    )
    return pl.pallas_call(
        functools.partial(_ffn_kernel, residual=residual),
        grid_spec=grid_spec,
        out_shape=jax.ShapeDtypeStruct((m, d), F32),
        compiler_params=_cparams(2),
    )(tile_expert, tile_valid, x, g.reshape(1, d), row_gate, w1, w3, w2)


def _ple_kernel(h_ref, g_ref, wg_ref, p_ref, wp_ref, gf_ref, o_ref, y_ref):
    h = h_ref[...]
    gate = _sigmoid(_dot(_rms(h, g_ref[...]).astype(BF16), wg_ref[...]))
    proj = _dot(p_ref[...].astype(BF16), wp_ref[...])
    h2 = h + gate * proj
    o_ref[...] = h2
    y_ref[...] = _rms(h2, gf_ref[...])


def ple(h, g, wg, p, wp, gf):
    m, d = h.shape
    pd = p.shape[1]
    tm = _row_tile(m)
    row = lambda i: (i, 0)
    fixed = lambda i: (0, 0)
    return pl.pallas_call(
        _ple_kernel,
        grid=(m // tm,),
        in_specs=[pl.BlockSpec((tm, d), row), pl.BlockSpec((1, d), fixed),
                  pl.BlockSpec((d, d), fixed), pl.BlockSpec((tm, pd), row),
                  pl.BlockSpec((pd, d), fixed), pl.BlockSpec((1, d), fixed)],
        out_specs=[pl.BlockSpec((tm, d), row), pl.BlockSpec((tm, d), row)],
        out_shape=[jax.ShapeDtypeStruct((m, d), F32), jax.ShapeDtypeStruct((m, d), F32)],
        compiler_params=_cparams(1),
    )(h, g.reshape(1, d), wg, p, wp, gf.reshape(1, d))


def _bias_kernel(tab_ref, d_ref, o_ref):
    n = jnp.maximum(d_ref[...], 0)
    max_exact = N_BUCKETS // 2
    nf = jnp.maximum(n, 1).astype(F32)
    large = max_exact + (jnp.log(nf / max_exact) / math.log(MAX_DISTANCE / max_exact)
                         * (N_BUCKETS - max_exact)).astype(I32)
    large = jnp.minimum(large, N_BUCKETS - 1)
    bucket = jnp.where(n < max_exact, n, large)
    for h in range(H_NSA):
        acc = jnp.zeros(n.shape, F32)
        for k in range(N_BUCKETS):
            acc = jnp.where(bucket == k, tab_ref[k, h], acc)
        o_ref[h] = acc


def bias_lookup(table, dist):
    r, c = dist.shape
    tr = min(r, 128)
    return pl.pallas_call(
        _bias_kernel,
        grid=(r // tr,),
        in_specs=[pl.BlockSpec(memory_space=pltpu.SMEM),
                  pl.BlockSpec((tr, c), lambda i: (i, 0))],
        out_specs=pl.BlockSpec((H_NSA, tr, c), lambda i: (0, i, 0)),
        out_shape=jax.ShapeDtypeStruct((H_NSA, r, c), F32),
        compiler_params=_cparams(1),
    )(table, dist)


def _swap_halves(x):
    lane = lax.broadcasted_iota(I32, x.shape, 1)
    return jnp.where(lane % HEAD_DIM < HEAD_DIM // 2,
                     pltpu.roll(x, LANES - HEAD_DIM // 2, 1), pltpu.roll(x, HEAD_DIM // 2, 1))


def _retention_prompt_kernel(q_ref, k_ref, v_ref, g_ref, cos_ref, sin_ref, gain_ref, intra_ref,
                             qdec_ref, kdec_ref, cdec_ref, y_ref, s_out_ref, s_ref):
    c = pl.program_id(1)

    @pl.when(c == 0)
    def _():
        s_ref[...] = jnp.zeros_like(s_ref)

    cos = cos_ref[...]
    sin = sin_ref[...]
    lane = lax.broadcasted_iota(I32, (RET_CHUNK, LANES), 1)
    left = lane < HEAD_DIM
    row = lax.broadcasted_iota(I32, (LANES, LANES), 0)
    col = lax.broadcasted_iota(I32, (LANES, LANES), 1)
    same_head = (row < HEAD_DIM) == (col < HEAD_DIM)
    ones_bd = jnp.where(same_head, 1.0, 0.0).astype(BF16)
    for p in range(H_RET // 2):
        sl = slice(p * LANES, (p + 1) * LANES)
        q = q_ref[:, sl]
        k = k_ref[:, sl]
        v = v_ref[:, sl].astype(BF16)
        qr = q * cos + _swap_halves(q) * sin
        kr = (k * cos + _swap_halves(k) * sin) * (HEAD_DIM ** -0.5)
        qb = qr.astype(BF16)
        q2 = jnp.concatenate([jnp.where(left, qr, 0.0), jnp.where(left, 0.0, qr)], axis=0).astype(BF16)
        inner = _dot_nt(q2, kr.astype(BF16)) * intra_ref[p]
        o2 = _dot(inner.astype(BF16), v)
        s_old = s_ref[p]
        cross = _dot(qb, s_old.astype(BF16)) * qdec_ref[:, sl]
        o = jnp.where(left, o2[:RET_CHUNK], o2[RET_CHUNK:]) + cross
        kd = (kr * kdec_ref[:, sl]).T.astype(BF16)
        s_ref[p] = s_old * cdec_ref[p] + jnp.where(same_head, _dot(kd, v), 0.0)
        ms = _dot3(o * o, ones_bd) * (1.0 / HEAD_DIM)
        gate = g_ref[:, sl]
        y = o * lax.rsqrt(ms + EPS) * gain_ref[:, sl] * (gate * _sigmoid(gate))
        y_ref[:, sl] = y.astype(y_ref.dtype)

    @pl.when(c == pl.num_programs(1) - 1)
    def _():
        for p in range(H_RET // 2):
            s = s_ref[p]
            s_out_ref[2 * p] = s[:HEAD_DIM, :HEAD_DIM]
            s_out_ref[2 * p + 1] = s[HEAD_DIM:, HEAD_DIM:]


def _retention_consts(c):
    log_g = jnp.log1p(-jnp.exp2(-5.0 - jnp.arange(H_RET, dtype=F32)))
    j = jnp.arange(c, dtype=F32)
    diff = j[:, None] - j[None, :]
    intra = jnp.where(diff >= 0, jnp.exp(jnp.maximum(diff, 0.0)[None] * log_g[:, None, None]), 0.0)
    q_decay = jnp.exp((j[:, None] + 1.0) * log_g[None, :])
    k_decay = jnp.exp((c - 1.0 - j[:, None]) * log_g[None, :])
    c_decay = jnp.exp(c * log_g)
    intra2 = intra.reshape(H_RET // 2, 2 * c, c)
    qdec = jnp.repeat(q_decay, HEAD_DIM, axis=1)
    kdec = jnp.repeat(k_decay, HEAD_DIM, axis=1)
    cdec = jnp.repeat(c_decay.reshape(H_RET // 2, 2), HEAD_DIM, axis=1)[:, :, None] * jnp.ones((1, 1, LANES), F32)
    return intra2, qdec, kdec, cdec


def _rotary_tables(pos):
    half = HEAD_DIM // 2
    inv = ROPE_BASE ** (-jnp.arange(half, dtype=F32) / half)
    ang = pos.astype(F32)[:, None] * inv[None, :]
    cos, sin = jnp.cos(ang), jnp.sin(ang)
    cos_t = jnp.concatenate([cos, cos, cos, cos], axis=1)
    sin_t = jnp.concatenate([-sin, sin, -sin, sin], axis=1)
    return cos_t, sin_t


def retention_prompt(z3, gain):
    b, t, _ = z3.shape
    c = RET_CHUNK
    cos_t, sin_t = _rotary_tables(jnp.arange(t))
    intra2, qdec, kdec, cdec = _retention_consts(c)
    blk = lambda col: pl.BlockSpec((None, c, RET_W), lambda bi, ci, col=col: (bi, ci, col))
    fixed2 = lambda shape: pl.BlockSpec(shape, lambda bi, ci: (0,) * len(shape))
    return pl.pallas_call(
        _retention_prompt_kernel,
        grid=(b, t // c),
        in_specs=[blk(COL_RQ // RET_W), blk(COL_RK // RET_W), blk(COL_RV // RET_W), blk(COL_RG // RET_W),
                  pl.BlockSpec((c, LANES), lambda bi, ci: (ci, 0)),
                  pl.BlockSpec((c, LANES), lambda bi, ci: (ci, 0)),
                  fixed2((1, RET_W)), fixed2((H_RET // 2, 2 * c, c)), fixed2((c, RET_W)),
                  fixed2((c, RET_W)), fixed2((H_RET // 2, LANES, LANES))],
        out_specs=[pl.BlockSpec((None, c, RET_W), lambda bi, ci: (bi, ci, 0)),
                   pl.BlockSpec((None, H_RET, HEAD_DIM, HEAD_DIM), lambda bi, ci: (bi, 0, 0, 0))],
        out_shape=[jax.ShapeDtypeStruct((b, t, RET_W), BF16),
                   jax.ShapeDtypeStruct((b, H_RET, HEAD_DIM, HEAD_DIM), F32)],
        scratch_shapes=[pltpu.VMEM((H_RET // 2, LANES, LANES), F32)],
        compiler_params=_cparams(2),
    )(z3, z3, z3, z3, cos_t, sin_t, gain.reshape(1, RET_W), intra2, qdec, kdec, cdec)


def _retention_step_kernel(q_ref, k_ref, v_ref, g_ref, s_ref, cos_ref, sin_ref, gamma_ref, gain_ref,
                           swap_ref, exp_ref, tile_ref, red_ref, y_ref, s_out_ref):
    cos = cos_ref[...]
    sin = sin_ref[...]
    q = q_ref[...]
    k = k_ref[...]
    qr = q * cos + _dot3(q, swap_ref[...]) * sin
    kr = (k * cos + _dot3(k, swap_ref[...]) * sin) * (HEAD_DIM ** -0.5)
    kx = _dot3(kr, exp_ref[...])
    vx = _dot3(v_ref[...], tile_ref[...])
    s_new = s_ref[...] * gamma_ref[...] + kx * vx
    s_out_ref[...] = s_new
    o = _dot3(_dot3(qr, exp_ref[...]) * s_new, red_ref[...])
    gate = g_ref[...]
    y_ref[...] = _rms(o, gain_ref[...]) * (gate * _sigmoid(gate))


def retention_step(q, k, v, g, state, layer, pos, gain):
    n = q.shape[0]
    dd = HEAD_DIM * HEAD_DIM
    rb = min(n, 256)
    cos_t, sin_t = _rotary_tables(jnp.full((1,), pos))
    cos_t, sin_t = cos_t[:, :HEAD_DIM], sin_t[:, :HEAD_DIM]
    gamma = 1.0 - jnp.exp2(-5.0 - jnp.arange(H_RET, dtype=F32))
    gamma_rows = jnp.tile(gamma, n // H_RET).reshape(n, 1)
    gain_rows = jnp.tile(gain, (n // H_RET, 1))
    d = np.arange(HEAD_DIM)
    flat = np.arange(dd)
    swap = (d[:, None] == (d[None, :] + HEAD_DIM // 2) % HEAD_DIM).astype(np.float32)
    expand = (d[:, None] == flat[None, :] // HEAD_DIM).astype(np.float32)
    tile = (d[:, None] == flat[None, :] % HEAD_DIM).astype(np.float32)
    row = lambda i: (i, 0)
    fixed = lambda i: (0, 0)
    nblk = n // rb
    return pl.pallas_call(
        _retention_step_kernel,
        grid=(nblk,),
        in_specs=[pl.BlockSpec((rb, HEAD_DIM), row)] * 4
        + [pl.BlockSpec((rb, dd), lambda i: (layer * nblk + i, 0)),
           pl.BlockSpec((1, HEAD_DIM), fixed), pl.BlockSpec((1, HEAD_DIM), fixed),
           pl.BlockSpec((rb, 1), row), pl.BlockSpec((rb, HEAD_DIM), row),
           pl.BlockSpec((HEAD_DIM, HEAD_DIM), fixed), pl.BlockSpec((HEAD_DIM, dd), fixed),
           pl.BlockSpec((HEAD_DIM, dd), fixed), pl.BlockSpec((dd, HEAD_DIM), fixed)],
        out_specs=[pl.BlockSpec((rb, HEAD_DIM), row), pl.BlockSpec((rb, dd), row)],
        out_shape=[jax.ShapeDtypeStruct((n, HEAD_DIM), F32), jax.ShapeDtypeStruct((n, dd), F32)],
        compiler_params=_cparams(1),
    )(q, k, v, g, state, cos_t, sin_t, gamma_rows, gain_rows,
      jnp.asarray(swap, BF16), jnp.asarray(expand, BF16), jnp.asarray(tile, BF16), jnp.asarray(tile.T, BF16))


def _gelu_tanh(x):
    return 0.5 * x * (1.0 + jnp.tanh(math.sqrt(2.0 / math.pi) * (x + 0.044715 * (x * x * x))))


def _compress_kernel(*refs, n_pieces):
    x_refs = refs[1:1 + n_pieces]
    pe_ref, w1_ref, w2_ref, kc_ref, vc_ref = refs[1 + n_pieces:]
    x =jnp.concatenate([r[...] for r in x_refs], axis=0) if n_pieces > 1 else x_refs[0][...]
    n = x.shape[0]
    for kv, o_ref in enumerate((kc_ref, vc_ref)):
        xk = jnp.concatenate([x[:, j * KV_W + kv * LANES: j * KV_W + (kv + 1) * LANES]
                              for j in range(CMP_STRIDE)], axis=1)
        top = _dot((xk + pe_ref[kv, 0]).astype(BF16), w1_ref[kv, 0])
        bot = _dot((xk + pe_ref[kv, 1]).astype(BF16), w1_ref[kv, 1])
        pre = top + pltpu.roll(bot, n - 1, 0)
        hid = _gelu_tanh(pre).astype(BF16)
        o_ref[...] = jnp.concatenate(
            [_dot(hid[:, g * CMP_HIDDEN:(g + 1) * CMP_HIDDEN], w2_ref[kv]) for g in range(N_KV)], axis=1)


def _compress_weights(pe, w1, w2):
    pe_r = pe.reshape(2, 2, CMP_STRIDE, 1, HEAD_DIM)
    pe_t = jnp.broadcast_to(pe_r, (2, 2, CMP_STRIDE, N_KV, HEAD_DIM)).reshape(2, 2, 1, CMP_STRIDE * LANES)
    w1r = w1.reshape(2, 2, CMP_STRIDE, HEAD_DIM, CMP_HIDDEN)
    eye = jnp.eye(N_KV, dtype=w1.dtype)
    wbig = jnp.einsum('khjdc,gG->khjgdGc', w1r, eye)
    wbig = wbig.reshape(2, 2, CMP_STRIDE * LANES, N_KV * CMP_HIDDEN)
    return pe_t.astype(F32), wbig.astype(BF16), w2.astype(BF16)


def compress(src, idx, pe_t, w1b, w2b):
    bsz, n_pieces = idx.shape
    rows, width = src.shape[1], src.shape[2]
    n = n_pieces * rows
    piece = lambda p: pl.BlockSpec((None, rows, width), lambda b, ix, p=p: (ix[b, p], 0, 0))
    fixed = lambda shape: pl.BlockSpec(shape, lambda b, ix: (0,) * len(shape))
    out = pl.BlockSpec((None, n, LANES), lambda b, ix: (b, 0, 0))
    grid_spec = pltpu.PrefetchScalarGridSpec(
        num_scalar_prefetch=1,
        grid=(bsz,),
        in_specs=[piece(p) for p in range(n_pieces)]
        + [fixed(pe_t.shape), fixed(w1b.shape), fixed(w2b.shape)],
        out_specs=[out, out],
    )
    return pl.pallas_call(
        functools.partial(_compress_kernel, n_pieces=n_pieces),
        grid_spec=grid_spec,
        out_shape=[jax.ShapeDtypeStruct((bsz, n, LANES), F32)] * 2,
        compiler_params=_cparams(1),
    )(idx, *([src] * n_pieces), pe_t, w1b, w2b)


def _masked_softmax(s, mask):
    s = jnp.where(mask, s, NEG)
    m = jnp.max(s, axis=-1, keepdims=True)
    e = jnp.where(mask, jnp.exp(s - m), 0.0)
    return e / jnp.maximum(jnp.sum(e, axis=-1, keepdims=True), TINY)


def _block_scores(imp, qpos, n_sel):
    blk = lax.broadcasted_iota(I32, imp.shape, 1)
    cur = qpos // SEL_BLOCK
    forced = (blk == 0) | (blk == cur) | (blk == cur - 1)
    valid = blk * SEL_BLOCK <= qpos
    score = jnp.where(forced, FORCED_SCORE, jnp.where(valid, imp, -1.0))
    return jnp.where(blk < n_sel, score, -2.0)


def _select_top(score, n_top):
    lane = lax.broadcasted_iota(I32, score.shape, 1)
    rank = jnp.zeros(score.shape, F32)
    for i in range(score.shape[1]):
        col = score[:, i:i + 1]
        beats = (col > score) | ((col == score) & (lane > i))
        rank = rank + jnp.where(beats, 1.0, 0.0)
    return jnp.where(rank < n_top, 1.0, 0.0)


def _attend(qbd, k, v, bias, mask, carry):
    m, l, acc = carry
    s = jnp.where(mask, _dot_nt(qbd, k) + bias, NEG)
    m_new = jnp.maximum(m, jnp.max(s, axis=-1, keepdims=True))
    alpha = jnp.exp(m - m_new)
    p = jnp.where(mask, jnp.exp(s - m_new), 0.0)
    l = alpha * l + jnp.sum(p, axis=-1, keepdims=True)
    acc = alpha * acc + _dot(p.astype(BF16), v)
    return m_new, l, acc


def _nsa_prompt_kernel(q_ref, gate_ref, kc_ref, vc_ref, ks_ref, kw_ref, bc_ref, bt_ref, pool_ref, exp_ref,
                       o_ref, sel_ref, *, n_cmp, n_sel):
    i = pl.program_id(1)
    qs = i * Q_BLOCK
    rows = H_NSA * Q_BLOCK
    row_i = lax.broadcasted_iota(I32, (Q_BLOCK, LANES), 0)
    lane_i = lax.broadcasted_iota(I32, (Q_BLOCK, LANES), 1)
    left = lane_i < HEAD_DIM
    q = q_ref[...]
    parts = []
    for h in range(H_NSA):
        slab = q[:, (h // 2) * LANES:(h // 2 + 1) * LANES]
        g = h // R_GRP
        if h % 2 != g:
            slab = pltpu.roll(slab, HEAD_DIM, 1)
        parts.append(jnp.where(left, slab, 0.0) if g == 0 else jnp.where(left, 0.0, slab))
    qbd = (jnp.concatenate(parts, axis=0) * (HEAD_DIM ** -0.5)).astype(BF16)

    ncp = kc_ref.shape[0]
    s_c = _dot_nt(qbd, kc_ref[...].astype(BF16)) + bc_ref[...].reshape(rows, ncp)
    t_c = lax.broadcasted_iota(I32, (Q_BLOCK, ncp), 0)
    n_c = lax.broadcasted_iota(I32, (Q_BLOCK, ncp), 1)
    ok_c = jnp.where((n_c * CMP_STRIDE + (L_CMP - 1) <= qs + t_c) & (n_c < n_cmp), 1.0, 0.0)
    p_c = _masked_softmax(s_c, jnp.concatenate([ok_c] * H_NSA, axis=0) > 0.5)
    o_c = _dot(p_c.astype(BF16), vc_ref[...].astype(BF16))
    pcs = []
    for g in range(N_KV):
        acc = p_c[(g * R_GRP) * Q_BLOCK:(g * R_GRP + 1) * Q_BLOCK]
        for r in range(1, R_GRP):
            acc = acc + p_c[(g * R_GRP + r) * Q_BLOCK:(g * R_GRP + r + 1) * Q_BLOCK]
        pcs.append(acc)
    imp = _dot3(jnp.concatenate(pcs, axis=0), pool_ref[...])
    qpos = qs + lax.broadcasted_iota(I32, imp.shape, 0) % Q_BLOCK
    sel = _select_top(_block_scores(imp, qpos, n_sel), min(N_SELECT, n_sel))
    sel_ref[...] = _dot(sel.astype(BF16), exp_ref[...])

    dmat = row_i - lane_i
    init = (jnp.full((rows, 1), NEG, F32), jnp.zeros((rows, 1), F32), jnp.zeros((rows, LANES), F32))

    def chunk(ref, off):
        return (ref[pl.ds(off, KV_CHUNK), 0:LANES].astype(BF16),
                ref[pl.ds(off, KV_CHUNK), LANES:2 * LANES].astype(BF16))

    def sel_body(j, carry):
        rel = i - j
        off = pl.multiple_of(j * KV_CHUNK, KV_CHUNK)
        k, v = chunk(ks_ref, off)
        causal = jnp.where(rel * KV_CHUNK + dmat >= 0, 1.0, 0.0)
        picked = sel_ref[:, pl.ds(off, KV_CHUNK)]
        m0 = picked[:Q_BLOCK] * causal
        m1 = picked[Q_BLOCK:] * causal
        mask = jnp.concatenate([m0] * R_GRP + [m1] * R_GRP, axis=0) > 0.5
        bias = bt_ref[jnp.minimum(rel, FAR_REL)].reshape(rows, LANES)
        return _attend(qbd, k, v, bias, mask, carry)

    _, l_s, acc_s = lax.fori_loop(0, i + 1, sel_body, init)
    o_s = acc_s / jnp.maximum(l_s, TINY)

    n_back = WINDOW // KV_CHUNK

    def win_body(c, carry):
        j = i - n_back + c
        rel = n_back - c
        off = pl.multiple_of(jnp.maximum(j, 0) * KV_CHUNK, KV_CHUNK)
        k, v = chunk(kw_ref, off)
        dist = rel * KV_CHUNK + dmat + jnp.where(j < 0, WINDOW, 0)
        ok = jnp.where((dist >= 0) & (dist < WINDOW), 1.0, 0.0)
        mask = jnp.concatenate([ok] * H_NSA, axis=0) > 0.5
        bias = bt_ref[jnp.minimum(rel, FAR_REL)].reshape(rows, LANES)
        return _attend(qbd, k, v, bias, mask, carry)

    _, l_w, acc_w = lax.fori_loop(0, n_back + 1, win_body, init)
    o_w = acc_w / jnp.maximum(l_w, TINY)

    sig = _sigmoid(gate_ref[...])
    slabs = []
    for k2 in range(H_NSA // 2):
        pair = []
        for h in (2 * k2, 2 * k2 + 1):
            rs = slice(h * Q_BLOCK, (h + 1) * Q_BLOCK)
            oh = (sig[:, 3 * h:3 * h + 1] * o_c[rs] + sig[:, 3 * h + 1:3 * h + 2] * o_s[rs]
                  + sig[:, 3 * h + 2:3 * h + 3] * o_w[rs])
            if h % 2 != h // R_GRP:
                oh = pltpu.roll(oh, HEAD_DIM, 1)
            pair.append(oh)
        slabs.append(jnp.where(left, pair[0], pair[1]))
    o_ref[...] = jnp.concatenate(slabs, axis=1).astype(o_ref.dtype)


def _selection_consts(n_rows_cmp, n_keys):
    n = np.arange(n_rows_cmp)
    blk = np.arange(SEL_LANES)
    pool = (n[:, None] // (SEL_BLOCK // CMP_STRIDE) == blk[None, :]).astype(np.float32)
    key = np.arange(n_keys)
    expand = (blk[:, None] == key[None, :] // SEL_BLOCK).astype(np.float32)
    return jnp.asarray(pool, BF16), jnp.asarray(expand, BF16)


def nsa_prompt(z3, kc, vc, bias_c, bias_t):
    b, t, _ = z3.shape
    n_sub = t // CMP_STRIDE
    n_sel = -(-t // SEL_BLOCK)
    assert n_sel <= SEL_LANES and kc.shape[1] == n_sub
    pool, expand = _selection_consts(n_sub, t)
    fixed = lambda shape: pl.BlockSpec(shape, lambda bi, i: (0,) * len(shape))
    return pl.pallas_call(
        functools.partial(_nsa_prompt_kernel, n_cmp=n_sub - 1, n_sel=n_sel),
        grid=(b, t // Q_BLOCK),
        in_specs=[pl.BlockSpec((None, Q_BLOCK, NSA_QW), lambda bi, i: (bi, i, COL_NQ // NSA_QW)),
                  pl.BlockSpec((None, Q_BLOCK, LANES), lambda bi, i: (bi, i, COL_NG // LANES)),
                  pl.BlockSpec((None, n_sub, LANES), lambda bi, i: (bi, 0, 0)),
                  pl.BlockSpec((None, n_sub, LANES), lambda bi, i: (bi, 0, 0)),
                  pl.BlockSpec((None, t, KV_W), lambda bi, i: (bi, 0, COL_KVS // KV_W)),
                  pl.BlockSpec((None, t, KV_W), lambda bi, i: (bi, 0, COL_KVW // KV_W)),
                  pl.BlockSpec((H_NSA, Q_BLOCK, n_sub), lambda bi, i: (0, i, 0)),
                  fixed(bias_t.shape), fixed(pool.shape), fixed(expand.shape)],
        out_specs=pl.BlockSpec((None, Q_BLOCK, NSA_QW), lambda bi, i: (bi, i, 0)),
        out_shape=jax.ShapeDtypeStruct((b, t, NSA_QW), BF16),
        scratch_shapes=[pltpu.VMEM((N_KV * Q_BLOCK, t), F32)],
        compiler_params=_cparams(2),
    )(z3, z3, kc, vc, z3, z3, bias_c, bias_t, pool, expand)


def _nsa_step_kernel(pt_ref, z_ref, kc_ref, vc_ref, *rest, n_pages, past, wbuf, n_cmp, n_sel):
    del pt_ref
    pages = rest[:n_pages]
    win_ref, bcs_ref, bss_ref, bws_ref, pool_ref, exp_ref, o_ref = rest[n_pages:]
    z = z_ref[...]
    row8 = lax.broadcasted_iota(I32, (H_NSA, LANES), 0)
    lane8 = lax.broadcasted_iota(I32, (H_NSA, LANES), 1)
    left8 = lane8 < HEAD_DIM
    qbd = jnp.zeros((H_NSA, LANES), F32)
    for h in range(H_NSA):
        slab = jnp.broadcast_to(z[:, COL_NQ + (h // 2) * LANES:COL_NQ + (h // 2 + 1) * LANES], (H_NSA, LANES))
        g = h // R_GRP
        if h % 2 != g:
            slab = pltpu.roll(slab, HEAD_DIM, 1)
        keep = (row8 == h) & (left8 if g == 0 else jnp.logical_not(left8))
        qbd = jnp.where(keep, slab, qbd)
    qbd = (qbd * (HEAD_DIM ** -0.5)).astype(BF16)

    ncp = kc_ref.shape[0]
    n_c = lax.broadcasted_iota(I32, (H_NSA, ncp), 1)
    s_c = _dot_nt(qbd, kc_ref[...].astype(BF16)) + bcs_ref[...]
    p_c = _masked_softmax(s_c, (n_c * CMP_STRIDE + (L_CMP - 1) <= past) & (n_c < n_cmp))
    o_c = _dot(p_c.astype(BF16), vc_ref[...].astype(BF16))
    pc0 = jnp.sum(p_c[0:R_GRP], axis=0, keepdims=True)
    pc1 = jnp.sum(p_c[R_GRP:], axis=0, keepdims=True)
    rowc = lax.broadcasted_iota(I32, (H_NSA, ncp), 0)
    pcs = jnp.where(rowc < R_GRP, jnp.broadcast_to(pc0, (H_NSA, ncp)), jnp.broadcast_to(pc1, (H_NSA, ncp)))
    imp = _dot3(pcs, pool_ref[...])
    sel = _select_top(_block_scores(imp, jnp.full(imp.shape, past, I32), n_sel), min(N_SELECT, n_sel))
    picked = _dot(sel.astype(BF16), exp_ref[...])

    r128 = lax.broadcasted_iota(I32, (KV_CHUNK, LANES), 0)

    def new_row(col):
        return jnp.where(r128 == 0, jnp.broadcast_to(z[:, col:col + LANES], (KV_CHUNK, LANES)), 0.0).astype(BF16)

    ks = [pg[:, 0:LANES].astype(BF16) for pg in pages] + [new_row(COL_KVS)]
    vs = [pg[:, LANES:2 * LANES].astype(BF16) for pg in pages] + [new_row(COL_KVS + LANES)]
    s_s = jnp.concatenate([_dot_nt(qbd, k) for k in ks], axis=1) + bss_ref[...]
    key = lax.broadcasted_iota(I32, s_s.shape, 1)
    p_s = _masked_softmax(s_s, (picked > 0.5) & (key <= past)).astype(BF16)
    o_s = _dot(p_s[:, 0:KV_CHUNK], vs[0])
    for c in range(1, len(vs)):
        o_s = o_s + _dot(p_s[:, c * KV_CHUNK:(c + 1) * KV_CHUNK], vs[c])

    n_wc = wbuf // KV_CHUNK
    kw = [win_ref[c * KV_CHUNK:(c + 1) * KV_CHUNK, 0:LANES].astype(BF16) for c in range(n_wc)] + [new_row(COL_KVW)]
    vw = [win_ref[c * KV_CHUNK:(c + 1) * KV_CHUNK, LANES:2 * LANES].astype(BF16) for c in range(n_wc)]
    vw = vw + [new_row(COL_KVW + LANES)]
    s_w = jnp.concatenate([_dot_nt(qbd, k) for k in kw], axis=1) + bws_ref[...]
    colw = lax.broadcasted_iota(I32, s_w.shape, 1)
    p_w = _masked_softmax(s_w, (colw <= wbuf) & (wbuf - colw < WINDOW)).astype(BF16)
    o_w = _dot(p_w[:, 0:KV_CHUNK], vw[0])
    for c in range(1, len(vw)):
        o_w = o_w + _dot(p_w[:, c * KV_CHUNK:(c + 1) * KV_CHUNK], vw[c])

    sig = jnp.broadcast_to(_sigmoid(z[:, COL_NG:COL_NG + LANES]), (H_NSA, LANES))

    def gate_col(c):
        return jnp.sum(jnp.where(lane8 == row8 * 3 + c, sig, 0.0), axis=1, keepdims=True)

    o = gate_col(0) * o_c + gate_col(1) * o_s + gate_col(2) * o_w
    fix = jnp.where((row8 % 2) != (row8 // R_GRP), pltpu.roll(o, HEAD_DIM, 1), o)
    left1 = left8[0:1]
    o_ref[...] = jnp.concatenate(
        [jnp.where(left1, fix[2 * k2:2 * k2 + 1], fix[2 * k2 + 1:2 * k2 + 2]) for k2 in range(H_NSA // 2)], axis=1)


def nsa_step(zs3, kc, vc, sel_cache, win_cache, page_idx, win_row0, bias_cs, bias_ss, bias_ws, past):
    bs = zs3.shape[0]
    n_pages = page_idx.shape[1]
    wbuf = win_cache.shape[1]
    n_sub = (past + 1) // CMP_STRIDE
    n_sel = -(-(past + 1) // SEL_BLOCK)
    n_keys = (n_pages + 1) * PAGE_SIZE
    assert n_sel <= SEL_LANES and PAGE_SIZE == KV_CHUNK and wbuf % KV_CHUNK == 0 and kc.shape[1] == n_sub
    pool, expand = _selection_consts(n_sub, n_keys)
    fixed = lambda shape: pl.BlockSpec(shape, lambda b, pt: (0,) * len(shape))
    page = lambda p: pl.BlockSpec((None, PAGE_SIZE, KV_W), lambda b, pt, p=p: (pt[b, p], 0, 0))
    grid_spec = pltpu.PrefetchScalarGridSpec(
        num_scalar_prefetch=1,
        grid=(bs,),
        in_specs=[pl.BlockSpec((None, 1, N_IN_PAD), lambda b, pt: (b, 0, 0)),
                  pl.BlockSpec((None, n_sub, LANES), lambda b, pt: (b, 0, 0)),
                  pl.BlockSpec((None, n_sub, LANES), lambda b, pt: (b, 0, 0))]
        + [page(p) for p in range(n_pages)]
        + [pl.BlockSpec((None, wbuf, KV_W), lambda b, pt: (win_row0 + b, 0, 0)),
           fixed(bias_cs.shape), fixed(bias_ss.shape), fixed(bias_ws.shape), fixed(pool.shape), fixed(expand.shape)],
        out_specs=pl.BlockSpec((None, 1, NSA_QW), lambda b, pt: (b, 0, 0)),
    )
    return pl.pallas_call(
        functools.partial(_nsa_step_kernel, n_pages=n_pages, past=past, wbuf=wbuf, n_cmp=n_sub - 1, n_sel=n_sel),
        grid_spec=grid_spec,
        out_shape=jax.ShapeDtypeStruct((bs, 1, NSA_QW), F32),
        compiler_params=_cparams(1),
    )(page_idx, zs3, kc, vc, *([sel_cache] * n_pages), win_cache, bias_cs, bias_ss, bias_ws, pool, expand)


def _route(logits, tm):
    m = logits.shape[0]
    top_v, top_i = lax.top_k(logits, TOP_K)
    top_w = jax.nn.softmax(top_v, axis=-1)
    slot_e = top_i.reshape(-1)
    onehot = (slot_e[:, None] == jnp.arange(N_EXPERTS)[None, :]).astype(I32)
    before = jnp.cumsum(onehot, axis=0) - onehot
    rank = jnp.sum(before * onehot, axis=1)
    count = jnp.sum(onehot, axis=0)
    padded = -(-count // tm) * tm
    start = jnp.cumsum(padded) - padded
    pos = start[slot_e] + rank
    n_rows = (-(-(m * TOP_K) // tm) + N_EXPERTS) * tm
    row_token = jnp.zeros((n_rows,), I32).at[pos].set(jnp.arange(m * TOP_K, dtype=I32) // TOP_K)
    row_gate = jnp.zeros((n_rows,), F32).at[pos].set(top_w.reshape(-1))
    tile_start = jnp.arange(n_rows // tm, dtype=I32) * tm
    tile_expert = jnp.clip(jnp.searchsorted(jnp.cumsum(padded), tile_start, side='right'), 0, N_EXPERTS - 1)
    tile_valid = (tile_start < jnp.sum(padded)).astype(I32)
    return pos.reshape(m, TOP_K), row_token, row_gate.reshape(n_rows, 1), tile_expert.astype(I32), tile_valid


def _dense_ffn(h, g, w1, w3, w2):
    m = h.shape[0]
    tm = _row_tile(m)
    ones = jnp.ones((m // tm,), I32)
    return grouped_ffn(h, g, jnp.ones((m, 1), F32), jnp.zeros((m // tm,), I32), ones,
                       w1[None], w3[None], w2[None], tm=tm, tf=w1.shape[1] // 2, residual=True)


def kernel(x_prompt, x_sample, p_prompt, p_sample, state_ret, cache_win_kv, cache_cmp_kv, cache_sel_kv, page_table, rel_bias, norm_mix, w_in, ret_gain, cmp_pe, cmp_w1, cmp_w2, w_out, norm_ffn, ffn_w1, ffn_w3, ffn_w2, router, moe_w1, moe_w3, moe_w2, ple_norm, ple_gate, ple_proj, norm_final):
    b_p, t_p, d = x_prompt.shape
    b_s = x_sample.shape[0]
    depth = w_in.shape[0]
    n_pages = page_table.shape[1]
    past = n_pages * PAGE_SIZE
    n_pool = cache_cmp_kv.shape[1]
    w_buf = cache_win_kv.shape[2]
    assert x_sample.shape[1] == 1 and t_p % Q_BLOCK == 0 and t_p >= w_buf and past % CMP_STRIDE == 0
    row = (2, N_KV, HEAD_DIM)
    m_p = b_p * t_p

    ar = lambda n: jnp.arange(n, dtype=I32)
    n_sub_p = t_p // CMP_STRIDE
    bias_c = bias_lookup(rel_bias, ar(t_p)[:, None] - (ar(n_sub_p)[None, :] * CMP_STRIDE + L_CMP - 1))
    rel_t = (ar(FAR_REL + 1)[:, None, None] * KV_CHUNK + ar(Q_BLOCK)[None, :, None] - ar(KV_CHUNK)[None, None, :])
    bias_t = bias_lookup(rel_bias, rel_t.reshape((FAR_REL + 1) * Q_BLOCK, KV_CHUNK))
    bias_t = bias_t.reshape(H_NSA, FAR_REL + 1, Q_BLOCK, KV_CHUNK).transpose(1, 0, 2, 3)
    n_sub_s = (past + 1) // CMP_STRIDE
    rows8 = jnp.zeros((SUBLANES, 1), I32)
    bias_cs = bias_lookup(rel_bias, rows8 + (past - (ar(n_sub_s)[None, :] * CMP_STRIDE + L_CMP - 1)))[:, 0]
    bias_ss = bias_lookup(rel_bias, rows8 + (past - ar((n_pages + 1) * PAGE_SIZE)[None, :]))[:, 0]
    bias_ws = bias_lookup(rel_bias, rows8 + (w_buf - ar(w_buf + KV_CHUNK)[None, :]))[:, 0]

    cmp_cache = cache_cmp_kv.reshape(depth * n_pool, PAGE_SIZE // CMP_STRIDE, CMP_STRIDE * KV_W)
    sel_cache = cache_sel_kv.reshape(depth * n_pool, PAGE_SIZE, KV_W)
    win_cache = cache_win_kv.reshape(depth * b_s, w_buf, KV_W)
    state_flat = state_ret.reshape(depth * b_s * H_RET, HEAD_DIM * HEAD_DIM)

    h_p = x_prompt.reshape(m_p, d)
    h_s = x_sample.reshape(b_s, d)
    rp, rs, wp, ws, cp, cs, sp, ss = [], [], [], [], [], [], [], []
    y_p = y_s = None
    for i in range(depth):
        w_in_b = jnp.pad(w_in[i], ((0, 0), (0, N_IN_PAD - N_IN))).astype(BF16)
        w_out_b = w_out[i].astype(BF16)
        pe_t, cw1, cw2 = _compress_weights(cmp_pe[i], cmp_w1[i], cmp_w2[i])

        z3 = rms_matmul(h_p, norm_mix[i], w_in_b).reshape(b_p, t_p, N_IN_PAD)
        kvc = z3[:, :, COL_KVC:COL_KVC + KV_W]
        kvs = z3[:, :, COL_KVS:COL_KVS + KV_W]
        kvw = z3[:, :, COL_KVW:COL_KVW + KV_W]
        ret_y, ret_state = retention_prompt(z3, ret_gain[i])
        kc, vc = compress(kvc.reshape(b_p, n_sub_p, CMP_STRIDE * KV_W), ar(b_p)[:, None], pe_t, cw1, cw2)
        nsa_y = nsa_prompt(z3, kc, vc, bias_c, bias_t)
        h_p = out_proj(h_p, ret_y.reshape(m_p, RET_W), nsa_y.reshape(m_p, NSA_QW), w_out_b)
        rp.append(ret_state)
        wp.append(kvw[:, t_p - w_buf:].reshape((b_p, w_buf) + row))
        cp.append(kvc.reshape((b_p, t_p) + row))
        sp.append(kvs.reshape((b_p, t_p) + row))

        zs = rms_matmul(h_s, norm_mix[i], w_in_b)
        heads = lambda col: zs[:, col:col + RET_W].reshape(b_s * H_RET, HEAD_DIM)
        ret_ys, state_s = retention_step(heads(COL_RQ), heads(COL_RK), heads(COL_RV), heads(COL_RG),
                                         state_flat, i, past, ret_gain[i])
        page_idx = page_table + i * n_pool
        kc_s, vc_s = compress(cmp_cache, page_idx, pe_t, cw1, cw2)
        nsa_ys = nsa_step(zs.reshape(b_s, 1, N_IN_PAD), kc_s, vc_s, sel_cache, win_cache, page_idx, i * b_s,
                          bias_cs, bias_ss, bias_ws, past)
        h_s = out_proj(h_s, ret_ys.reshape(b_s, RET_W), nsa_ys.reshape(b_s, NSA_QW), w_out_b)
        rs.append(state_s.reshape(b_s, H_RET, HEAD_DIM, HEAD_DIM))
        kvw_s = zs[:, None, COL_KVW:COL_KVW + KV_W]
        ws.append(jnp.concatenate([win_cache[i * b_s:(i + 1) * b_s, 1:], kvw_s], axis=1).reshape((b_s, w_buf) + row))
        cs.append(zs[:, COL_KVC:COL_KVC + KV_W].reshape((b_s, 1) + row))
        ss.append(zs[:, COL_KVS:COL_KVS + KV_W].reshape((b_s, 1) + row))

        j = i // 2
        if i % 2 == 0:
            w1, w3, w2 = ffn_w1[j].astype(BF16), ffn_w3[j].astype(BF16), ffn_w2[j].astype(BF16)
            h_p = _dense_ffn(h_p, norm_ffn[i], w1, w3, w2)
            h_s = _dense_ffn(h_s, norm_ffn[i], w1, w3, w2)
        else:
            h_all = jnp.concatenate([h_p, h_s], axis=0)
            m_all = h_all.shape[0]
            m_pad = -(-m_all // 256) * 256
            router_w = jnp.pad(router[j], ((0, 0), (0, LANES - N_EXPERTS)))
            logits = router_logits(jnp.pad(h_all, ((0, m_pad - m_all), (0, 0))), norm_ffn[i], router_w)
            tm = 512
            pos, row_token, row_gate, tile_expert, tile_valid = _route(logits[:m_all, :N_EXPERTS], tm)
            y_rows = grouped_ffn(h_all[row_token], norm_ffn[i], row_gate, tile_expert, tile_valid,
                                 moe_w1[j].astype(BF16), moe_w3[j].astype(BF16), moe_w2[j].astype(BF16),
                                 tm=tm, tf=moe_w1.shape[3] // 4, residual=False)
            h_all = h_all + y_rows[pos[:, 0]] + y_rows[pos[:, 1]]
            h_p, h_s = h_all[:m_p], h_all[m_p:]

        wg, wpj = ple_gate[i].astype(BF16), ple_proj[i].astype(BF16)
        h_p, y_p = ple(h_p, ple_norm[i], wg, p_prompt[i].reshape(m_p, P_DIM), wpj, norm_final)
        h_s, y_s = ple(h_s, ple_norm[i], wg, p_sample[i].reshape(b_s, P_DIM), wpj, norm_final)

    return (y_p.reshape(b_p, t_p, d), y_s.reshape(b_s, 1, d), jnp.stack(rp), jnp.stack(rs), jnp.stack(wp),
            jnp.stack(ws), jnp.stack(cp), jnp.stack(cs), jnp.stack(sp), jnp.stack(ss))
```

```python
import functools
import math

import numpy as np
import jax
import jax.numpy as jnp
from jax import lax
from jax.experimental import pallas as pl
from jax.experimental.pallas import tpu as pltpu

F32 = jnp.float32
BF16 = jnp.bfloat16
I32 = jnp.int32

LANES = 128
SUBLANES = 8
VMEM_LIMIT_BYTES = 56 * 1024 * 1024

D_MODEL = 1024
HEAD_DIM = 64
H_RET = 8
H_NSA = 8
N_KV = 2
R_GRP = H_NSA // N_KV
RET_W = H_RET * HEAD_DIM
NSA_QW = H_NSA * HEAD_DIM
KV_W = 2 * N_KV * HEAD_DIM
N_IN = 4 * RET_W + NSA_QW + 3 * KV_W + 3 * H_NSA
N_IN_PAD = 3584
COL_RQ, COL_RK, COL_RV, COL_RG = 0, RET_W, 2 * RET_W, 3 * RET_W
COL_NQ = 4 * RET_W
COL_KVC = COL_NQ + NSA_QW
COL_KVS = COL_KVC + KV_W
COL_KVW = COL_KVS + KV_W
COL_NG = COL_KVW + KV_W
RET_CHUNK = 128
CMP_STRIDE = 16
L_CMP = 2 * CMP_STRIDE
CMP_HIDDEN = 2 * HEAD_DIM
SEL_BLOCK = 64
N_SELECT = 16
WINDOW = 512
Q_BLOCK = 128
N_BUCKETS = 32
MAX_DISTANCE = 128
N_EXPERTS = 8
TOP_K = 2
P_DIM = 256
PAGE_SIZE = 128
ROPE_BASE = 10000.0
EPS = 1e-6
FORCED_SCORE = 1e4
NEG = -1e30
TINY = float(np.finfo(np.float32).tiny)
SEL_LANES = 64
FAR_REL = 2
KV_CHUNK = 128


def _cparams(n_grid):
    return pltpu.CompilerParams(dimension_semantics=("arbitrary",) * n_grid,
                                vmem_limit_bytes=VMEM_LIMIT_BYTES)


def _dot(a, b):
    return jnp.dot(a, b, preferred_element_type=F32)


def _dot_nt(a, b):
    return lax.dot_general(a, b, (((1,), (1,)), ((), ())), preferred_element_type=F32)


def _split3(x):
    hi = x.astype(BF16)
    r = x - hi.astype(F32)
    mid = r.astype(BF16)
    lo = (r - mid.astype(F32)).astype(BF16)
    return hi, mid, lo


def _dot3(x, b01):
    hi, mid, lo = _split3(x)
    return _dot(hi, b01) + _dot(mid, b01) + _dot(lo, b01)


def _dot6(x, w):
    xh, xm, xl = _split3(x)
    wh, wm, wl = _split3(w)
    return (_dot(xh, wh) + _dot(xh, wm) + _dot(xm, wh)
            + _dot(xm, wm) + _dot(xh, wl) + _dot(xl, wh))


def _sigmoid(x):
    return 1.0 / (1.0 + jnp.exp(-x))


def _rms(x, g):
    return x * lax.rsqrt(jnp.mean(x * x, axis=-1, keepdims=True) + EPS) * g


def _row_tile(m):
    for t in (512, 256, 128, 64, 32, 16, 8):
        if m % t == 0:
            return t
    raise ValueError(f"row count {m} not a multiple of 8")


def _rms_matmul_kernel(x_ref, g_ref, w_ref, o_ref):
    o_ref[...] = _dot(_rms(x_ref[...], g_ref[...]).astype(BF16), w_ref[...])


def rms_matmul(x, g, w):
    m, d = x.shape
    n = w.shape[1]
    tm = min(_row_tile(m), 256)
    return pl.pallas_call(
        _rms_matmul_kernel,
        grid=(m // tm,),
        in_specs=[pl.BlockSpec((tm, d), lambda i: (i, 0)),
                  pl.BlockSpec((1, d), lambda i: (0, 0)),
                  pl.BlockSpec((d, n), lambda i: (0, 0))],
        out_specs=pl.BlockSpec((tm, n), lambda i: (i, 0)),
        out_shape=jax.ShapeDtypeStruct((m, n), F32),
        compiler_params=_cparams(1),
    )(x, g.reshape(1, d), w)


def _rms_matmul_kvt_kernel(x_ref, g_ref, w_ref, wt_ref, o_ref, ot_ref):
    xn = _rms(x_ref[...], g_ref[...]).astype(BF16)
    o_ref[...] = _dot(xn, w_ref[...])
    ot_ref[...] = _dot_nt(wt_ref[...], xn)


def rms_matmul_kvt(x, g, w, wt, bsz):
    m, d = x.shape
    n = w.shape[1]
    nt = wt.shape[0]
    t = m // bsz
    tm = min(_row_tile(t), 256)
    per = t // tm
    return pl.pallas_call(
        _rms_matmul_kvt_kernel,
        grid=(m // tm,),
        in_specs=[pl.BlockSpec((tm, d), lambda i: (i, 0)),
                  pl.BlockSpec((1, d), lambda i: (0, 0)),
                  pl.BlockSpec((d, n), lambda i: (0, 0)),
                  pl.BlockSpec((nt, d), lambda i: (0, 0))],
        out_specs=[pl.BlockSpec((tm, n), lambda i: (i, 0)),
                   pl.BlockSpec((None, nt, tm), lambda i: (i // per, 0, i % per))],
        out_shape=[jax.ShapeDtypeStruct((m, n), F32), jax.ShapeDtypeStruct((bsz, nt, t), F32)],
        compiler_params=_cparams(1),
    )(x, g.reshape(1, d), w, wt)


def _router_kernel(x_ref, g_ref, w_ref, o_ref):
    o_ref[...] = _dot6(_rms(x_ref[...], g_ref[...]), w_ref[...])


def router_logits(x, g, w):
    m, d = x.shape
    n = w.shape[1]
    tm = min(_row_tile(m), 256)
    return pl.pallas_call(
        _router_kernel,
        grid=(m // tm,),
        in_specs=[pl.BlockSpec((tm, d), lambda i: (i, 0)),
                  pl.BlockSpec((1, d), lambda i: (0, 0)),
                  pl.BlockSpec((d, n), lambda i: (0, 0))],
        out_specs=pl.BlockSpec((tm, n), lambda i: (i, 0)),
        out_shape=jax.ShapeDtypeStruct((m, n), F32),
        compiler_params=_cparams(1),
    )(x, g.reshape(1, d), w)


def _outproj_kernel(h_ref, a_ref, b_ref, w_ref, o_ref):
    ka = a_ref.shape[1]
    o_ref[...] = (h_ref[...] + _dot(a_ref[...].astype(BF16), w_ref[0:ka, :])
                  + _dot(b_ref[...].astype(BF16), w_ref[ka:, :]))


def out_proj(h, a, b, w):
    m, d = h.shape
    ka, kb = a.shape[1], b.shape[1]
    tm = _row_tile(m)
    return pl.pallas_call(
        _outproj_kernel,
        grid=(m // tm,),
        in_specs=[pl.BlockSpec((tm, d), lambda i: (i, 0)),
                  pl.BlockSpec((tm, ka), lambda i: (i, 0)),
                  pl.BlockSpec((tm, kb), lambda i: (i, 0)),
                  pl.BlockSpec((ka + kb, d), lambda i: (0, 0))],
        out_specs=pl.BlockSpec((tm, d), lambda i: (i, 0)),
        out_shape=jax.ShapeDtypeStruct((m, d), F32),
        compiler_params=_cparams(1),
    )(h, a, b, w)


def _ffn_kernel(te_ref, tv_ref, x_ref, g_ref, gate_ref, w1_ref, w3_ref, w2_ref, o_ref,
                xn_ref, acc_ref, *, residual):
    i = pl.program_id(0)
    j = pl.program_id(1)

    @pl.when(j == 0)
    def _():
        xn_ref[...] = _rms(x_ref[...], g_ref[...]).astype(BF16)
        acc_ref[...] = jnp.zeros_like(acc_ref)

    @pl.when(tv_ref[i] != 0)
    def _():
        xn = xn_ref[...]
        a = _dot(xn, w1_ref[...])
        b = _dot(xn, w3_ref[...])
        hid = (a * _sigmoid(a)) * b
        acc_ref[...] += _dot(hid.astype(BF16), w2_ref[...])

    @pl.when(j == pl.num_programs(1) - 1)
    def _():
        y = acc_ref[...] * gate_ref[...]
        if residual:
            y = y + x_ref[...]
        o_ref[...] = y


def grouped_ffn(x, g, row_gate, tile_expert, tile_valid, w1, w3, w2, *, tm, tf, residual):
    m, d = x.shape
    f = w1.shape[2]
    grid_spec = pltpu.PrefetchScalarGridSpec(
        num_scalar_prefetch=2,
        grid=(m // tm, f // tf),
        in_specs=[pl.BlockSpec((tm, d), lambda i, j, te, tv: (i, 0)),
                  pl.BlockSpec((1, d), lambda i, j, te, tv: (0, 0)),
                  pl.BlockSpec((tm, 1), lambda i, j, te, tv: (i, 0)),
                  pl.BlockSpec((None, d, tf), lambda i, j, te, tv: (te[i], 0, j)),
                  pl.BlockSpec((None, d, tf), lambda i, j, te, tv: (te[i], 0, j)),
                  pl.BlockSpec((None, tf, d), lambda i, j, te, tv: (te[i], j, 0))],
        out_specs=pl.BlockSpec((tm, d), lambda i, j, te, tv: (i, 0)),
        scratch_shapes=[pltpu.VMEM((tm, d), BF16), pltpu.VMEM((tm, d), F32)],
    )
    return pl.pallas_call(
        functools.partial(_ffn_kernel, residual=residual),
        grid_spec=grid_spec,
        out_shape=jax.ShapeDtypeStruct((m, d), F32),
        compiler_params=_cparams(2),
    )(tile_expert, tile_valid, x, g.reshape(1, d), row_gate, w1, w3, w2)


def _ple_kernel(h_ref, g_ref, wg_ref, p_ref, wp_ref, gf_ref, o_ref, y_ref):
    h = h_ref[...]
    gate = _sigmoid(_dot(_rms(h, g_ref[...]).astype(BF16), wg_ref[...]))
    proj = _dot(p_ref[...].astype(BF16), wp_ref[...])
    h2 = h + gate * proj
    o_ref[...] = h2
    y_ref[...] = _rms(h2, gf_ref[...])


def ple(h, g, wg, p, wp, gf):
    m, d = h.shape
    pd = p.shape[1]
    tm = _row_tile(m)
    row = lambda i: (i, 0)
    fixed = lambda i: (0, 0)
    return pl.pallas_call(
        _ple_kernel,
        grid=(m // tm,),
        in_specs=[pl.BlockSpec((tm, d), row), pl.BlockSpec((1, d), fixed),
                  pl.BlockSpec((d, d), fixed), pl.BlockSpec((tm, pd), row),
                  pl.BlockSpec((pd, d), fixed), pl.BlockSpec((1, d), fixed)],
        out_specs=[pl.BlockSpec((tm, d), row), pl.BlockSpec((tm, d), row)],
        out_shape=[jax.ShapeDtypeStruct((m, d), F32), jax.ShapeDtypeStruct((m, d), F32)],
        compiler_params=_cparams(1),
    )(h, g.reshape(1, d), wg, p, wp, gf.reshape(1, d))


def _bias_kernel(tab_ref, d_ref, o_ref):
    n = jnp.maximum(d_ref[...], 0)
    max_exact = N_BUCKETS // 2
    nf = jnp.maximum(n, 1).astype(F32)
    large = max_exact + (jnp.log(nf / max_exact) / math.log(MAX_DISTANCE / max_exact)
                         * (N_BUCKETS - max_exact)).astype(I32)
    large = jnp.minimum(large, N_BUCKETS - 1)
    bucket = jnp.where(n < max_exact, n, large)
    for h in range(H_NSA):
        acc = jnp.zeros(n.shape, F32)
        for k in range(N_BUCKETS):
            acc = jnp.where(bucket == k, tab_ref[k, h], acc)
        o_ref[h] = acc


def bias_lookup(table, dist):
    r, c = dist.shape
    tr = min(r, 128)
    return pl.pallas_call(
        _bias_kernel,
        grid=(r // tr,),
        in_specs=[pl.BlockSpec(memory_space=pltpu.SMEM),
                  pl.BlockSpec((tr, c), lambda i: (i, 0))],
        out_specs=pl.BlockSpec((H_NSA, tr, c), lambda i: (0, i, 0)),
        out_shape=jax.ShapeDtypeStruct((H_NSA, r, c), F32),
        compiler_params=_cparams(1),
    )(table, dist)


def _swap_halves(x):
    lane = lax.broadcasted_iota(I32, x.shape, 1)
    return jnp.where(lane % HEAD_DIM < HEAD_DIM // 2,
                     pltpu.roll(x, LANES - HEAD_DIM // 2, 1), pltpu.roll(x, HEAD_DIM // 2, 1))


def _retention_prompt_kernel(q_ref, k_ref, v_ref, g_ref, cos_ref, sin_ref, gain_ref, intra_ref,
                             qdec_ref, kdec_ref, cdec_ref, y_ref, s_out_ref, s_ref):
    c = pl.program_id(1)

    @pl.when(c == 0)
    def _():
        s_ref[...] = jnp.zeros_like(s_ref)

    cos = cos_ref[...]
    sin = sin_ref[...]
    lane = lax.broadcasted_iota(I32, (RET_CHUNK, LANES), 1)
    left = lane < HEAD_DIM
    row = lax.broadcasted_iota(I32, (LANES, LANES), 0)
    col = lax.broadcasted_iota(I32, (LANES, LANES), 1)
    same_head = (row < HEAD_DIM) == (col < HEAD_DIM)
    ones_bd = jnp.where(same_head, 1.0, 0.0).astype(BF16)
    for p in range(H_RET // 2):
        sl = slice(p * LANES, (p + 1) * LANES)
        q = q_ref[:, sl]
        k = k_ref[:, sl]
        v = v_ref[:, sl].astype(BF16)
        qr = q * cos + _swap_halves(q) * sin
        kr = (k * cos + _swap_halves(k) * sin) * (HEAD_DIM ** -0.5)
        qb = qr.astype(BF16)
        q2 = jnp.concatenate([jnp.where(left, qr, 0.0), jnp.where(left, 0.0, qr)], axis=0).astype(BF16)
        inner = _dot_nt(q2, kr.astype(BF16)) * intra_ref[p]
        o2 = _dot(inner.astype(BF16), v)
        s_old = s_ref[p]
        cross = _dot(qb, s_old.astype(BF16)) * qdec_ref[:, sl]
        o = jnp.where(left, o2[:RET_CHUNK], o2[RET_CHUNK:]) + cross
        kd = (kr * kdec_ref[:, sl]).T.astype(BF16)
        s_ref[p] = s_old * cdec_ref[p] + jnp.where(same_head, _dot(kd, v), 0.0)
        ms = _dot3(o * o, ones_bd) * (1.0 / HEAD_DIM)
        gate = g_ref[:, sl]
        y = o * lax.rsqrt(ms + EPS) * gain_ref[:, sl] * (gate * _sigmoid(gate))
        y_ref[:, sl] = y.astype(y_ref.dtype)

    @pl.when(c == pl.num_programs(1) - 1)
    def _():
        for p in range(H_RET // 2):
            s = s_ref[p]
            s_out_ref[2 * p] = s[:HEAD_DIM, :HEAD_DIM]
            s_out_ref[2 * p + 1] = s[HEAD_DIM:, HEAD_DIM:]


def _retention_consts(c):
    log_g = jnp.log1p(-jnp.exp2(-5.0 - jnp.arange(H_RET, dtype=F32)))
    j = jnp.arange(c, dtype=F32)
    diff = j[:, None] - j[None, :]
    intra = jnp.where(diff >= 0, jnp.exp(jnp.maximum(diff, 0.0)[None] * log_g[:, None, None]), 0.0)
    q_decay = jnp.exp((j[:, None] + 1.0) * log_g[None, :])
    k_decay = jnp.exp((c - 1.0 - j[:, None]) * log_g[None, :])
    c_decay = jnp.exp(c * log_g)
    intra2 = intra.reshape(H_RET // 2, 2 * c, c)
    qdec = jnp.repeat(q_decay, HEAD_DIM, axis=1)
    kdec = jnp.repeat(k_decay, HEAD_DIM, axis=1)
    cdec = jnp.repeat(c_decay.reshape(H_RET // 2, 2), HEAD_DIM, axis=1)[:, :, None] * jnp.ones((1, 1, LANES), F32)
    return intra2, qdec, kdec, cdec


def _rotary_tables(pos):
    half = HEAD_DIM // 2
    inv = ROPE_BASE ** (-jnp.arange(half, dtype=F32) / half)
    ang = pos.astype(F32)[:, None] * inv[None, :]
    cos, sin = jnp.cos(ang), jnp.sin(ang)
    cos_t = jnp.concatenate([cos, cos, cos, cos], axis=1)
    sin_t = jnp.concatenate([-sin, sin, -sin, sin], axis=1)
    return cos_t, sin_t


def retention_prompt(z3, gain):
    b, t, _ = z3.shape
    c = RET_CHUNK
    cos_t, sin_t = _rotary_tables(jnp.arange(t))
    intra2, qdec, kdec, cdec = _retention_consts(c)
    blk = lambda col: pl.BlockSpec((None, c, RET_W), lambda bi, ci, col=col: (bi, ci, col))
    fixed2 = lambda shape: pl.BlockSpec(shape, lambda bi, ci: (0,) * len(shape))
    return pl.pallas_call(
        _retention_prompt_kernel,
        grid=(b, t // c),
        in_specs=[blk(COL_RQ // RET_W), blk(COL_RK // RET_W), blk(COL_RV // RET_W), blk(COL_RG // RET_W),
                  pl.BlockSpec((c, LANES), lambda bi, ci: (ci, 0)),
                  pl.BlockSpec((c, LANES), lambda bi, ci: (ci, 0)),
                  fixed2((1, RET_W)), fixed2((H_RET // 2, 2 * c, c)), fixed2((c, RET_W)),
                  fixed2((c, RET_W)), fixed2((H_RET // 2, LANES, LANES))],
        out_specs=[pl.BlockSpec((None, c, RET_W), lambda bi, ci: (bi, ci, 0)),
                   pl.BlockSpec((None, H_RET, HEAD_DIM, HEAD_DIM), lambda bi, ci: (bi, 0, 0, 0))],
        out_shape=[jax.ShapeDtypeStruct((b, t, RET_W), BF16),
                   jax.ShapeDtypeStruct((b, H_RET, HEAD_DIM, HEAD_DIM), F32)],
        scratch_shapes=[pltpu.VMEM((H_RET // 2, LANES, LANES), F32)],
        compiler_params=_cparams(2),
    )(z3, z3, z3, z3, cos_t, sin_t, gain.reshape(1, RET_W), intra2, qdec, kdec, cdec)


def _retention_step_kernel(q_ref, k_ref, v_ref, g_ref, s_ref, cos_ref, sin_ref, gamma_ref, gain_ref,
                           y_ref, s_out_ref):
    half = HEAD_DIM // 2
    cos = cos_ref[...]
    sin = sin_ref[...]

    def rot(x):
        x1, x2 = x[:half], x[half:]
        return jnp.concatenate([x1 * cos - x2 * sin, x1 * sin + x2 * cos], axis=0)

    qr = rot(q_ref[...])
    kr = rot(k_ref[...]) * (HEAD_DIM ** -0.5)
    v = v_ref[...]
    gamma = gamma_ref[...]
    o = jnp.zeros_like(v)
    for d in range(HEAD_DIM):
        s_new = s_ref[d] * gamma + kr[d:d + 1] * v
        s_out_ref[d] = s_new
        o = o + qr[d:d + 1] * s_new
    ms = jnp.mean(o * o, axis=0, keepdims=True)
    gate = g_ref[...]
    y_ref[...] = o * lax.rsqrt(ms + EPS) * gain_ref[...] * (gate * _sigmoid(gate))


def retention_step(zt, state, layer, pos, gain):
    bs = zt.shape[1]
    half = HEAD_DIM // 2
    inv = ROPE_BASE ** (-jnp.arange(half, dtype=F32) / half)
    ang = jnp.full((1,), pos).astype(F32)[:, None] * inv[None, :]
    cos = jnp.broadcast_to(jnp.cos(ang).reshape(half, 1), (half, bs))
    sin = jnp.broadcast_to(jnp.sin(ang).reshape(half, 1), (half, bs))
    gamma = 1.0 - jnp.exp2(-5.0 - jnp.arange(H_RET, dtype=F32))
    gamma = jnp.broadcast_to(gamma.reshape(H_RET, 1, 1), (H_RET, 1, bs))
    gain_b = jnp.broadcast_to(gain.reshape(H_RET, HEAD_DIM, 1), (H_RET, HEAD_DIM, bs))
    head = lambda col: pl.BlockSpec((HEAD_DIM, bs), lambda h, col=col: (col // HEAD_DIM + h, 0))
    fixed = pl.BlockSpec((half, bs), lambda h: (0, 0))
    return pl.pallas_call(
        _retention_step_kernel,
        grid=(H_RET,),
        in_specs=[head(COL_RQ), head(COL_RK), head(COL_RV), head(COL_RG),
                  pl.BlockSpec((None, HEAD_DIM, HEAD_DIM, bs), lambda h: (layer * H_RET + h, 0, 0, 0)),
                  fixed, fixed,
                  pl.BlockSpec((None, 1, bs), lambda h: (h, 0, 0)),
                  pl.BlockSpec((None, HEAD_DIM, bs), lambda h: (h, 0, 0))],
        out_specs=[pl.BlockSpec((HEAD_DIM, bs), lambda h: (h, 0)),
                   pl.BlockSpec((None, HEAD_DIM, HEAD_DIM, bs), lambda h: (h, 0, 0, 0))],
        out_shape=[jax.ShapeDtypeStruct((RET_W, bs), F32),
                   jax.ShapeDtypeStruct((H_RET, HEAD_DIM, HEAD_DIM, bs), F32)],
        compiler_params=_cparams(1),
    )(zt, zt, zt, zt, state, cos, sin, gamma, gain_b)


def _gelu_tanh(x):
    return 0.5 * x * (1.0 + jnp.tanh(math.sqrt(2.0 / math.pi) * (x + 0.044715 * (x * x * x))))


def _compress_kernel(*refs, n_pieces):
    x_refs = refs[1:1 + n_pieces]
    pe_ref, w1_ref, w2_ref, kc_ref, vc_ref, rows_ref = refs[1 + n_pieces:]
    width = x_refs[0].shape[1]
    n = n_pieces * width // CMP_STRIDE
    for kv, o_ref in enumerate((kc_ref, vc_ref)):
        for p, x_ref in enumerate(x_refs):
            for c in range(width // LANES):
                rows_ref[pl.ds(p * width + c * LANES, LANES), :] = (
                    x_ref[kv * LANES:(kv + 1) * LANES, c * LANES:(c + 1) * LANES].T)
        xk = jnp.concatenate([rows_ref[pl.ds(j, n, stride=CMP_STRIDE), :] for j in range(CMP_STRIDE)], axis=1)
        top = _dot((xk + pe_ref[kv, 0]).astype(BF16), w1_ref[kv, 0])
        bot = _dot((xk + pe_ref[kv, 1]).astype(BF16), w1_ref[kv, 1])
        pre = top + pltpu.roll(bot, n - 1, 0)
        hid = _gelu_tanh(pre).astype(BF16)
        o_ref[...] = jnp.concatenate(
            [_dot(hid[:, g * CMP_HIDDEN:(g + 1) * CMP_HIDDEN], w2_ref[kv]) for g in range(N_KV)], axis=1)


def _compress_weights(pe, w1, w2):
    pe_r = pe.reshape(2, 2, CMP_STRIDE, 1, HEAD_DIM)
    pe_t = jnp.broadcast_to(pe_r, (2, 2, CMP_STRIDE, N_KV, HEAD_DIM)).reshape(2, 2, 1, CMP_STRIDE * LANES)
    w1r = w1.reshape(2, 2, CMP_STRIDE, HEAD_DIM, CMP_HIDDEN)
    eye = jnp.eye(N_KV, dtype=w1.dtype)
    wbig = jnp.einsum('khjdc,gG->khjgdGc', w1r, eye)
    wbig = wbig.reshape(2, 2, CMP_STRIDE * LANES, N_KV * CMP_HIDDEN)
    return pe_t.astype(F32), wbig.astype(BF16), w2.astype(BF16)


def compress(src, idx, pe_t, w1b, w2b):
    bsz, n_pieces = idx.shape
    width = src.shape[2]
    n = n_pieces * width // CMP_STRIDE
    piece = lambda p: pl.BlockSpec((None, KV_W, width), lambda b, ix, p=p: (ix[b, p], 0, 0))
    fixed = lambda shape: pl.BlockSpec(shape, lambda b, ix: (0,) * len(shape))
    out = pl.BlockSpec((None, n, LANES), lambda b, ix: (b, 0, 0))
    grid_spec = pltpu.PrefetchScalarGridSpec(
        num_scalar_prefetch=1,
        grid=(bsz,),
        in_specs=[piece(p) for p in range(n_pieces)]
        + [fixed(pe_t.shape), fixed(w1b.shape), fixed(w2b.shape)],
        out_specs=[out, out],
        scratch_shapes=[pltpu.VMEM((n_pieces * width, LANES), F32)],
    )
    return pl.pallas_call(
        functools.partial(_compress_kernel, n_pieces=n_pieces),
        grid_spec=grid_spec,
        out_shape=[jax.ShapeDtypeStruct((bsz, n, LANES), F32)] * 2,
        compiler_params=_cparams(1),
    )(idx, *([src] * n_pieces), pe_t, w1b, w2b)


def _masked_softmax(s, mask):
    s = jnp.where(mask, s, NEG)
    m = jnp.max(s, axis=-1, keepdims=True)
    e = jnp.where(mask, jnp.exp(s - m), 0.0)
    return e / jnp.maximum(jnp.sum(e, axis=-1, keepdims=True), TINY)


def _block_scores(imp, qpos, n_sel):
    blk = lax.broadcasted_iota(I32, imp.shape, 1)
    cur = qpos // SEL_BLOCK
    forced = (blk == 0) | (blk == cur) | (blk == cur - 1)
    valid = blk * SEL_BLOCK <= qpos
    score = jnp.where(forced, FORCED_SCORE, jnp.where(valid, imp, -1.0))
    return jnp.where(blk < n_sel, score, -2.0)


def _select_top(score, n_top):
    lane = lax.broadcasted_iota(I32, score.shape, 1)
    rank = jnp.zeros(score.shape, F32)
    for i in range(score.shape[1]):
        col = score[:, i:i + 1]
        beats = (col > score) | ((col == score) & (lane > i))
        rank = rank + jnp.where(beats, 1.0, 0.0)
    return jnp.where(rank < n_top, 1.0, 0.0)


def _attend(qbd, k, v, bias, mask, carry):
    m, l, acc = carry
    s = jnp.where(mask, _dot_nt(qbd, k) + bias, NEG)
    m_new = jnp.maximum(m, jnp.max(s, axis=-1, keepdims=True))
    alpha = jnp.exp(m - m_new)
    p = jnp.where(mask, jnp.exp(s - m_new), 0.0)
    l = alpha * l + jnp.sum(p, axis=-1, keepdims=True)
    acc = alpha * acc + _dot(p.astype(BF16), v)
    return m_new, l, acc


def _nsa_prompt_kernel(q_ref, gate_ref, kc_ref, vc_ref, ks_ref, kw_ref, bc_ref, bt_ref, pool_ref, exp_ref,
                       o_ref, sel_ref, *, n_cmp, n_sel):
    i = pl.program_id(1)
    qs = i * Q_BLOCK
    rows = H_NSA * Q_BLOCK
    row_i = lax.broadcasted_iota(I32, (Q_BLOCK, LANES), 0)
    lane_i = lax.broadcasted_iota(I32, (Q_BLOCK, LANES), 1)
    left = lane_i < HEAD_DIM
    q = q_ref[...]
    parts = []
    for h in range(H_NSA):
        slab = q[:, (h // 2) * LANES:(h // 2 + 1) * LANES]
        g = h // R_GRP
        if h % 2 != g:
            slab = pltpu.roll(slab, HEAD_DIM, 1)
        parts.append(jnp.where(left, slab, 0.0) if g == 0 else jnp.where(left, 0.0, slab))
    qbd = (jnp.concatenate(parts, axis=0) * (HEAD_DIM ** -0.5)).astype(BF16)

    ncp = kc_ref.shape[0]
    s_c = _dot_nt(qbd, kc_ref[...].astype(BF16)) + bc_ref[...].reshape(rows, ncp)
    t_c = lax.broadcasted_iota(I32, (Q_BLOCK, ncp), 0)
    n_c = lax.broadcasted_iota(I32, (Q_BLOCK, ncp), 1)
    ok_c = jnp.where((n_c * CMP_STRIDE + (L_CMP - 1) <= qs + t_c) & (n_c < n_cmp), 1.0, 0.0)
    p_c = _masked_softmax(s_c, jnp.concatenate([ok_c] * H_NSA, axis=0) > 0.5)
    o_c = _dot(p_c.astype(BF16), vc_ref[...].astype(BF16))
    pcs = []
    for g in range(N_KV):
        acc = p_c[(g * R_GRP) * Q_BLOCK:(g * R_GRP + 1) * Q_BLOCK]
        for r in range(1, R_GRP):
            acc = acc + p_c[(g * R_GRP + r) * Q_BLOCK:(g * R_GRP + r + 1) * Q_BLOCK]
        pcs.append(acc)
    imp = _dot3(jnp.concatenate(pcs, axis=0), pool_ref[...])
    qpos = qs + lax.broadcasted_iota(I32, imp.shape, 0) % Q_BLOCK
    sel = _select_top(_block_scores(imp, qpos, n_sel), min(N_SELECT, n_sel))
    sel_ref[...] = _dot(sel.astype(BF16), exp_ref[...])

    dmat = row_i - lane_i
    init = (jnp.full((rows, 1), NEG, F32), jnp.zeros((rows, 1), F32), jnp.zeros((rows, LANES), F32))

    def chunk(ref, off):
        return (ref[pl.ds(off, KV_CHUNK), 0:LANES].astype(BF16),
                ref[pl.ds(off, KV_CHUNK), LANES:2 * LANES].astype(BF16))

    def sel_body(j, carry):
        rel = i - j
        off = pl.multiple_of(j * KV_CHUNK, KV_CHUNK)
        k, v = chunk(ks_ref, off)
        causal = jnp.where(rel * KV_CHUNK + dmat >= 0, 1.0, 0.0)
        picked = sel_ref[:, pl.ds(off, KV_CHUNK)]
        m0 = picked[:Q_BLOCK] * causal
        m1 = picked[Q_BLOCK:] * causal
        mask = jnp.concatenate([m0] * R_GRP + [m1] * R_GRP, axis=0) > 0.5
        bias = bt_ref[jnp.minimum(rel, FAR_REL)].reshape(rows, LANES)
        return _attend(qbd, k, v, bias, mask, carry)

    _, l_s, acc_s = lax.fori_loop(0, i + 1, sel_body, init)
    o_s = acc_s / jnp.maximum(l_s, TINY)

    n_back = WINDOW // KV_CHUNK

    def win_body(c, carry):
        j = i - n_back + c
        rel = n_back - c
        off = pl.multiple_of(jnp.maximum(j, 0) * KV_CHUNK, KV_CHUNK)
        k, v = chunk(kw_ref, off)
        dist = rel * KV_CHUNK + dmat + jnp.where(j < 0, WINDOW, 0)
        ok = jnp.where((dist >= 0) & (dist < WINDOW), 1.0, 0.0)
        mask = jnp.concatenate([ok] * H_NSA, axis=0) > 0.5
        bias = bt_ref[jnp.minimum(rel, FAR_REL)].reshape(rows, LANES)
        return _attend(qbd, k, v, bias, mask, carry)

    _, l_w, acc_w = lax.fori_loop(0, n_back + 1, win_body, init)
    o_w = acc_w / jnp.maximum(l_w, TINY)

    sig = _sigmoid(gate_ref[...])
    slabs = []
    for k2 in range(H_NSA // 2):
        pair = []
        for h in (2 * k2, 2 * k2 + 1):
            rs = slice(h * Q_BLOCK, (h + 1) * Q_BLOCK)
            oh = (sig[:, 3 * h:3 * h + 1] * o_c[rs] + sig[:, 3 * h + 1:3 * h + 2] * o_s[rs]
                  + sig[:, 3 * h + 2:3 * h + 3] * o_w[rs])
            if h % 2 != h // R_GRP:
                oh = pltpu.roll(oh, HEAD_DIM, 1)
            pair.append(oh)
        slabs.append(jnp.where(left, pair[0], pair[1]))
    o_ref[...] = jnp.concatenate(slabs, axis=1).astype(o_ref.dtype)


def _selection_consts(n_rows_cmp, n_keys):
    n = np.arange(n_rows_cmp)
    blk = np.arange(SEL_LANES)
    pool = (n[:, None] // (SEL_BLOCK // CMP_STRIDE) == blk[None, :]).astype(np.float32)
    key = np.arange(n_keys)
    expand = (blk[:, None] == key[None, :] // SEL_BLOCK).astype(np.float32)
    return jnp.asarray(pool, BF16), jnp.asarray(expand, BF16)


def nsa_prompt(z3, kc, vc, bias_c, bias_t):
    b, t, _ = z3.shape
    n_sub = t // CMP_STRIDE
    n_sel = -(-t // SEL_BLOCK)
    assert n_sel <= SEL_LANES and kc.shape[1] == n_sub
    pool, expand = _selection_consts(n_sub, t)
    fixed = lambda shape: pl.BlockSpec(shape, lambda bi, i: (0,) * len(shape))
    return pl.pallas_call(
        functools.partial(_nsa_prompt_kernel, n_cmp=n_sub - 1, n_sel=n_sel),
        grid=(b, t // Q_BLOCK),
        in_specs=[pl.BlockSpec((None, Q_BLOCK, NSA_QW), lambda bi, i: (bi, i, COL_NQ // NSA_QW)),
                  pl.BlockSpec((None, Q_BLOCK, LANES), lambda bi, i: (bi, i, COL_NG // LANES)),
                  pl.BlockSpec((None, n_sub, LANES), lambda bi, i: (bi, 0, 0)),
                  pl.BlockSpec((None, n_sub, LANES), lambda bi, i: (bi, 0, 0)),
                  pl.BlockSpec((None, t, KV_W), lambda bi, i: (bi, 0, COL_KVS // KV_W)),
                  pl.BlockSpec((None, t, KV_W), lambda bi, i: (bi, 0, COL_KVW // KV_W)),
                  pl.BlockSpec((H_NSA, Q_BLOCK, n_sub), lambda bi, i: (0, i, 0)),
                  fixed(bias_t.shape), fixed(pool.shape), fixed(expand.shape)],
        out_specs=pl.BlockSpec((None, Q_BLOCK, NSA_QW), lambda bi, i: (bi, i, 0)),
        out_shape=jax.ShapeDtypeStruct((b, t, NSA_QW), BF16),
        scratch_shapes=[pltpu.VMEM((N_KV * Q_BLOCK, t), F32)],
        compiler_params=_cparams(2),
    )(z3, z3, kc, vc, z3, z3, bias_c, bias_t, pool, expand)


def _nsa_step_kernel(pt_ref, z_ref, kc_ref, vc_ref, *rest, n_pages, past, wbuf, n_cmp, n_sel):
    del pt_ref
    pages = rest[:n_pages]
    win_ref, bcs_ref, bss_ref, bws_ref, pool_ref, exp_ref, o_ref, wo_ref = rest[n_pages:]
    z = z_ref[...]
    row8 = lax.broadcasted_iota(I32, (H_NSA, LANES), 0)
    lane8 = lax.broadcasted_iota(I32, (H_NSA, LANES), 1)
    left8 = lane8 < HEAD_DIM
    qbd = jnp.zeros((H_NSA, LANES), F32)
    for h in range(H_NSA):
        slab = jnp.broadcast_to(z[:, COL_NQ + (h // 2) * LANES:COL_NQ + (h // 2 + 1) * LANES], (H_NSA, LANES))
        g = h // R_GRP
        if h % 2 != g:
            slab = pltpu.roll(slab, HEAD_DIM, 1)
        keep = (row8 == h) & (left8 if g == 0 else jnp.logical_not(left8))
        qbd = jnp.where(keep, slab, qbd)
    qbd = (qbd * (HEAD_DIM ** -0.5)).astype(BF16)

    ncp = kc_ref.shape[0]
    n_c = lax.broadcasted_iota(I32, (H_NSA, ncp), 1)
    s_c = _dot_nt(qbd, kc_ref[...].astype(BF16)) + bcs_ref[...]
    p_c = _masked_softmax(s_c, (n_c * CMP_STRIDE + (L_CMP - 1) <= past) & (n_c < n_cmp))
    o_c = _dot(p_c.astype(BF16), vc_ref[...].astype(BF16))
    pc0 = jnp.sum(p_c[0:R_GRP], axis=0, keepdims=True)
    pc1 = jnp.sum(p_c[R_GRP:], axis=0, keepdims=True)
    rowc = lax.broadcasted_iota(I32, (H_NSA, ncp), 0)
    pcs = jnp.where(rowc < R_GRP, jnp.broadcast_to(pc0, (H_NSA, ncp)), jnp.broadcast_to(pc1, (H_NSA, ncp)))
    imp = _dot3(pcs, pool_ref[...])
    sel = _select_top(_block_scores(imp, jnp.full(imp.shape, past, I32), n_sel), min(N_SELECT, n_sel))
    picked = _dot(sel.astype(BF16), exp_ref[...])

    c128 = lax.broadcasted_iota(I32, (LANES, KV_CHUNK), 1)

    def new_col(col):
        return jnp.broadcast_to(z[:, col:col + LANES], (KV_CHUNK, LANES)).T

    def first_col(tile):
        return jnp.where(c128 == 0, tile, 0.0).astype(BF16)

    ks = [pg[0:LANES, :].astype(BF16) for pg in pages] + [first_col(new_col(COL_KVS))]
    vs = [pg[LANES:2 * LANES, :].astype(BF16) for pg in pages] + [first_col(new_col(COL_KVS + LANES))]
    s_s = jnp.concatenate([_dot(qbd, k) for k in ks], axis=1) + bss_ref[...]
    key = lax.broadcasted_iota(I32, s_s.shape, 1)
    p_s = _masked_softmax(s_s, (picked > 0.5) & (key <= past)).astype(BF16)
    o_s = _dot_nt(p_s[:, 0:KV_CHUNK], vs[0])
    for c in range(1, len(vs)):
        o_s = o_s + _dot_nt(p_s[:, c * KV_CHUNK:(c + 1) * KV_CHUNK], vs[c])

    kw_new, vw_new = new_col(COL_KVW), new_col(COL_KVW + LANES)
    win = win_ref[...]
    s_w = jnp.concatenate([_dot(qbd, win[0:LANES].astype(BF16)), _dot(qbd, first_col(kw_new))], axis=1) + bws_ref[...]
    colw = lax.broadcasted_iota(I32, s_w.shape, 1)
    p_w = _masked_softmax(s_w, (colw <= wbuf) & (wbuf - colw < WINDOW)).astype(BF16)
    o_w = _dot_nt(p_w[:, 0:wbuf], win[LANES:].astype(BF16)) + _dot_nt(p_w[:, wbuf:], first_col(vw_new))

    shifted = pltpu.roll(win, wbuf - 1, 1)
    new_kv = jnp.concatenate([kw_new, vw_new], axis=0)
    last = jnp.where(lax.broadcasted_iota(I32, (KV_W, LANES), 1) == LANES - 1, new_kv, shifted[:, wbuf - LANES:])
    wo_ref[:, 0:wbuf - LANES] = shifted[:, 0:wbuf - LANES]
    wo_ref[:, wbuf - LANES:] = last

    sig = jnp.broadcast_to(_sigmoid(z[:, COL_NG:COL_NG + LANES]), (H_NSA, LANES))

    def gate_col(c):
        return jnp.sum(jnp.where(lane8 == row8 * 3 + c, sig, 0.0), axis=1, keepdims=True)

    o = gate_col(0) * o_c + gate_col(1) * o_s + gate_col(2) * o_w
    fix = jnp.where((row8 % 2) != (row8 // R_GRP), pltpu.roll(o, HEAD_DIM, 1), o)
    left1 = left8[0:1]
    o_ref[...] = jnp.concatenate(
        [jnp.where(left1, fix[2 * k2:2 * k2 + 1], fix[2 * k2 + 1:2 * k2 + 2]) for k2 in range(H_NSA // 2)], axis=1)


def nsa_step(zs3, kc, vc, sel_cache, win_cache, page_idx, win_row0, bias_cs, bias_ss, bias_ws, past):
    bs = zs3.shape[0]
    n_pages = page_idx.shape[1]
    wbuf = win_cache.shape[2]
    n_sub = (past + 1) // CMP_STRIDE
    n_sel = -(-(past + 1) // SEL_BLOCK)
    n_keys = (n_pages + 1) * PAGE_SIZE
    assert n_sel <= SEL_LANES and PAGE_SIZE == KV_CHUNK and wbuf % KV_CHUNK == 0 and kc.shape[1] == n_sub
    pool, expand = _selection_consts(n_sub, n_keys)
    fixed = lambda shape: pl.BlockSpec(shape, lambda b, pt: (0,) * len(shape))
    page = lambda p: pl.BlockSpec((None, KV_W, PAGE_SIZE), lambda b, pt, p=p: (pt[b, p], 0, 0))
    grid_spec = pltpu.PrefetchScalarGridSpec(
        num_scalar_prefetch=1,
        grid=(bs,),
        in_specs=[pl.BlockSpec((None, 1, N_IN_PAD), lambda b, pt: (b, 0, 0)),
                  pl.BlockSpec((None, n_sub, LANES), lambda b, pt: (b, 0, 0)),
                  pl.BlockSpec((None, n_sub, LANES), lambda b, pt: (b, 0, 0))]
        + [page(p) for p in range(n_pages)]
        + [pl.BlockSpec((None, KV_W, wbuf), lambda b, pt: (win_row0 + b, 0, 0)),
           fixed(bias_cs.shape), fixed(bias_ss.shape), fixed(bias_ws.shape), fixed(pool.shape), fixed(expand.shape)],
        out_specs=[pl.BlockSpec((None, 1, NSA_QW), lambda b, pt: (b, 0, 0)),
                   pl.BlockSpec((None, KV_W, wbuf), lambda b, pt: (b, 0, 0))],
    )
    return pl.pallas_call(
        functools.partial(_nsa_step_kernel, n_pages=n_pages, past=past, wbuf=wbuf, n_cmp=n_sub - 1, n_sel=n_sel),
        grid_spec=grid_spec,
        out_shape=[jax.ShapeDtypeStruct((bs, 1, NSA_QW), F32), jax.ShapeDtypeStruct((bs, KV_W, wbuf), F32)],
        compiler_params=_cparams(1),
    )(page_idx, zs3, kc, vc, *([sel_cache] * n_pages), win_cache, bias_cs, bias_ss, bias_ws, pool, expand)


def _route(logits, tm):
    m = logits.shape[0]
    top_v, top_i = lax.top_k(logits, TOP_K)
    top_w = jax.nn.softmax(top_v, axis=-1)
    slot_e = top_i.reshape(-1)
    onehot = (slot_e[:, None] == jnp.arange(N_EXPERTS)[None, :]).astype(I32)
    before = jnp.cumsum(onehot, axis=0) - onehot
    rank = jnp.sum(before * onehot, axis=1)
    count = jnp.sum(onehot, axis=0)
    padded = -(-count // tm) * tm
    start = jnp.cumsum(padded) - padded
    pos = start[slot_e] + rank
    n_rows = (-(-(m * TOP_K) // tm) + N_EXPERTS) * tm
    row_token = jnp.zeros((n_rows,), I32).at[pos].set(jnp.arange(m * TOP_K, dtype=I32) // TOP_K)
    row_gate = jnp.zeros((n_rows,), F32).at[pos].set(top_w.reshape(-1))
    tile_start = jnp.arange(n_rows // tm, dtype=I32) * tm
    tile_expert = jnp.clip(jnp.searchsorted(jnp.cumsum(padded), tile_start, side='right'), 0, N_EXPERTS - 1)
    tile_valid = (tile_start < jnp.sum(padded)).astype(I32)
    return pos.reshape(m, TOP_K), row_token, row_gate.reshape(n_rows, 1), tile_expert.astype(I32), tile_valid


def _dense_ffn(h, g, w1, w3, w2):
    m = h.shape[0]
    tm = _row_tile(m)
    ones = jnp.ones((m // tm,), I32)
    return grouped_ffn(h, g, jnp.ones((m, 1), F32), jnp.zeros((m // tm,), I32), ones,
                       w1[None], w3[None], w2[None], tm=tm, tf=w1.shape[1] // 2, residual=True)


def kernel(x_prompt, x_sample, p_prompt, p_sample, state_ret, cache_win_kv, cache_cmp_kv, cache_sel_kv, page_table, rel_bias, norm_mix, w_in, ret_gain, cmp_pe, cmp_w1, cmp_w2, w_out, norm_ffn, ffn_w1, ffn_w3, ffn_w2, router, moe_w1, moe_w3, moe_w2, ple_norm, ple_gate, ple_proj, norm_final):
    b_p, t_p, d = x_prompt.shape
    b_s = x_sample.shape[0]
    depth = w_in.shape[0]
    n_pages = page_table.shape[1]
    past = n_pages * PAGE_SIZE
    n_pool = cache_cmp_kv.shape[1]
    w_buf = cache_win_kv.shape[2]
    assert x_sample.shape[1] == 1 and t_p % Q_BLOCK == 0 and t_p >= w_buf and past % CMP_STRIDE == 0
    row = (2, N_KV, HEAD_DIM)
    m_p = b_p * t_p

    ar = lambda n: jnp.arange(n, dtype=I32)
    n_sub_p = t_p // CMP_STRIDE
    bias_c = bias_lookup(rel_bias, ar(t_p)[:, None] - (ar(n_sub_p)[None, :] * CMP_STRIDE + L_CMP - 1))
    rel_t = (ar(FAR_REL + 1)[:, None, None] * KV_CHUNK + ar(Q_BLOCK)[None, :, None] - ar(KV_CHUNK)[None, None, :])
    bias_t = bias_lookup(rel_bias, rel_t.reshape((FAR_REL + 1) * Q_BLOCK, KV_CHUNK))
    bias_t = bias_t.reshape(H_NSA, FAR_REL + 1, Q_BLOCK, KV_CHUNK).transpose(1, 0, 2, 3)
    n_sub_s = (past + 1) // CMP_STRIDE
    rows8 = jnp.zeros((SUBLANES, 1), I32)
    bias_cs = bias_lookup(rel_bias, rows8 + (past - (ar(n_sub_s)[None, :] * CMP_STRIDE + L_CMP - 1)))[:, 0]
    bias_ss = bias_lookup(rel_bias, rows8 + (past - ar((n_pages + 1) * PAGE_SIZE)[None, :]))[:, 0]
    bias_ws = bias_lookup(rel_bias, rows8 + (w_buf - ar(w_buf + KV_CHUNK)[None, :]))[:, 0]

    def rows_last(x):
        return jnp.moveaxis(x, 2, -1).reshape(x.shape[0] * x.shape[1], KV_W, x.shape[2])

    def rows_first(xt, lead):
        return jnp.moveaxis(xt.reshape((lead,) + row + (xt.shape[-1],)), -1, 1)

    cmp_cache = rows_last(cache_cmp_kv)
    sel_cache = rows_last(cache_sel_kv)
    win_cache = rows_last(cache_win_kv)
    state_t = jnp.moveaxis(state_ret, 1, -1).reshape(depth * H_RET, HEAD_DIM, HEAD_DIM, b_s)

    h_p = x_prompt.reshape(m_p, d)
    h_s = x_sample.reshape(b_s, d)
    rp, rs, wp, ws, cp, cs, sp, ss = [], [], [], [], [], [], [], []
    y_p = y_s = None
    for i in range(depth):
        w_in_b = jnp.pad(w_in[i], ((0, 0), (0, N_IN_PAD - N_IN))).astype(BF16)
        w_out_b = w_out[i].astype(BF16)
        pe_t, cw1, cw2 = _compress_weights(cmp_pe[i], cmp_w1[i], cmp_w2[i])

        w_kvt = w_in[i][:, COL_KVC:COL_KVC + 3 * KV_W].T.astype(BF16)
        z, kvt = rms_matmul_kvt(h_p, norm_mix[i], w_in_b, w_kvt, b_p)
        z3 = z.reshape(b_p, t_p, N_IN_PAD)
        ret_y, ret_state = retention_prompt(z3, ret_gain[i])
        kc, vc = compress(kvt, ar(b_p)[:, None], pe_t, cw1, cw2)
        nsa_y = nsa_prompt(z3, kc, vc, bias_c, bias_t)
        h_p = out_proj(h_p, ret_y.reshape(m_p, RET_W), nsa_y.reshape(m_p, NSA_QW), w_out_b)
        rp.append(ret_state)
        cp.append(rows_first(kvt[:, 0:KV_W], b_p))
        sp.append(rows_first(kvt[:, KV_W:2 * KV_W], b_p))
        wp.append(rows_first(kvt[:, 2 * KV_W:, t_p - w_buf:], b_p))

        zs = rms_matmul(h_s, norm_mix[i], w_in_b)
        zst = zs.T
        ret_yst, state_s = retention_step(zst, state_t, i, past, ret_gain[i])
        page_idx = page_table + i * n_pool
        kc_s, vc_s = compress(cmp_cache, page_idx, pe_t, cw1, cw2)
        nsa_ys, win_s = nsa_step(zs.reshape(b_s, 1, N_IN_PAD), kc_s, vc_s, sel_cache, win_cache, page_idx, i * b_s,
                                 bias_cs, bias_ss, bias_ws, past)
        h_s = out_proj(h_s, ret_yst.T, nsa_ys.reshape(b_s, NSA_QW), w_out_b)
        rs.append(jnp.moveaxis(state_s, -1, 0))
        ws.append(rows_first(win_s, b_s))
        cs.append(rows_first(zst[None, COL_KVC:COL_KVC + KV_W], 1).reshape((b_s, 1) + row))
        ss.append(rows_first(zst[None, COL_KVS:COL_KVS + KV_W], 1).reshape((b_s, 1) + row))

        j = i // 2
        if i % 2 == 0:
            w1, w3, w2 = ffn_w1[j].astype(BF16), ffn_w3[j].astype(BF16), ffn_w2[j].astype(BF16)
            h_p = _dense_ffn(h_p, norm_ffn[i], w1, w3, w2)
            h_s = _dense_ffn(h_s, norm_ffn[i], w1, w3, w2)
        else:
            h_all = jnp.concatenate([h_p, h_s], axis=0)
            m_all = h_all.shape[0]
            m_pad = -(-m_all // 256) * 256
            router_w = jnp.pad(router[j], ((0, 0), (0, LANES - N_EXPERTS)))
            logits = router_logits(jnp.pad(h_all, ((0, m_pad - m_all), (0, 0))), norm_ffn[i], router_w)
            tm = 512
            pos, row_token, row_gate, tile_expert, tile_valid = _route(logits[:m_all, :N_EXPERTS], tm)
            y_rows = grouped_ffn(h_all[row_token], norm_ffn[i], row_gate, tile_expert, tile_valid,
                                 moe_w1[j].astype(BF16), moe_w3[j].astype(BF16), moe_w2[j].astype(BF16),
                                 tm=tm, tf=moe_w1.shape[3] // 4, residual=False)
            h_all = h_all + y_rows[pos[:, 0]] + y_rows[pos[:, 1]]
            h_p, h_s = h_all[:m_p], h_all[m_p:]

        wg, wpj = ple_gate[i].astype(BF16), ple_proj[i].astype(BF16)
        h_p, y_p = ple(h_p, ple_norm[i], wg, p_prompt[i].reshape(m_p, P_DIM), wpj, norm_final)
        h_s, y_s = ple(h_s, ple_norm[i], wg, p_sample[i].reshape(b_s, P_DIM), wpj, norm_final)

    return (y_p.reshape(b_p, t_p, d), y_s.reshape(b_s, 1, d), jnp.stack(rp), jnp.stack(rs), jnp.stack(wp),
            jnp.stack(ws), jnp.stack(cp), jnp.stack(cs), jnp.stack(sp), jnp.stack(ss))
```

```python
import functools
import math

import numpy as np
import jax
import jax.numpy as jnp
from jax import lax
from jax.experimental import pallas as pl
from jax.experimental.pallas import tpu as pltpu

F32 = jnp.float32
BF16 = jnp.bfloat16
I32 = jnp.int32

LANES = 128
SUBLANES = 8
VMEM_LIMIT_BYTES = 56 * 1024 * 1024

D_MODEL = 1024
HEAD_DIM = 64
H_RET = 8
H_NSA = 8
N_KV = 2
R_GRP = H_NSA // N_KV
RET_W = H_RET * HEAD_DIM
NSA_QW = H_NSA * HEAD_DIM
KV_W = 2 * N_KV * HEAD_DIM
N_IN = 4 * RET_W + NSA_QW + 3 * KV_W + 3 * H_NSA
N_IN_PAD = 3584
COL_RQ, COL_RK, COL_RV, COL_RG = 0, RET_W, 2 * RET_W, 3 * RET_W
COL_NQ = 4 * RET_W
COL_KVC = COL_NQ + NSA_QW
COL_KVS = COL_KVC + KV_W
COL_KVW = COL_KVS + KV_W
COL_NG = COL_KVW + KV_W
RET_CHUNK = 128
CMP_STRIDE = 16
L_CMP = 2 * CMP_STRIDE
CMP_HIDDEN = 2 * HEAD_DIM
SEL_BLOCK = 64
N_SELECT = 16
WINDOW = 512
Q_BLOCK = 128
N_BUCKETS = 32
MAX_DISTANCE = 128
N_EXPERTS = 8
TOP_K = 2
P_DIM = 256
PAGE_SIZE = 128
ROPE_BASE = 10000.0
EPS = 1e-6
FORCED_SCORE = 1e4
NEG = -1e30
TINY = float(np.finfo(np.float32).tiny)
SEL_LANES = 64
FAR_REL = 2
KV_CHUNK = 128


def _cparams(n_grid):
    return pltpu.CompilerParams(dimension_semantics=("arbitrary",) * n_grid,
                                vmem_limit_bytes=VMEM_LIMIT_BYTES)


def _dot(a, b):
    return jnp.dot(a, b, preferred_element_type=F32)


def _dot_nt(a, b):
    return lax.dot_general(a, b, (((1,), (1,)), ((), ())), preferred_element_type=F32)


def _split3(x):
    hi = x.astype(BF16)
    r = x - hi.astype(F32)
    mid = r.astype(BF16)
    lo = (r - mid.astype(F32)).astype(BF16)
    return hi, mid, lo


def _dot3(x, b01):
    hi, mid, lo = _split3(x)
    return _dot(hi, b01) + _dot(mid, b01) + _dot(lo, b01)


def _dot6(x, w):
    xh, xm, xl = _split3(x)
    wh, wm, wl = _split3(w)
    return (_dot(xh, wh) + _dot(xh, wm) + _dot(xm, wh)
            + _dot(xm, wm) + _dot(xh, wl) + _dot(xl, wh))


def _sigmoid(x):
    return 1.0 / (1.0 + jnp.exp(-x))


def _rms(x, g):
    return x * lax.rsqrt(jnp.mean(x * x, axis=-1, keepdims=True) + EPS) * g


def _row_tile(m):
    for t in (512, 256, 128, 64, 32, 16, 8):
        if m % t == 0:
            return t
    raise ValueError(f"row count {m} not a multiple of 8")


def _rms_matmul_kernel(x_ref, g_ref, w_ref, o_ref):
    o_ref[...] = _dot(_rms(x_ref[...], g_ref[...]).astype(BF16), w_ref[...])


def rms_matmul(x, g, w):
    m, d = x.shape
    n = w.shape[1]
    tm = min(_row_tile(m), 256)
    return pl.pallas_call(
        _rms_matmul_kernel,
        grid=(m // tm,),
        in_specs=[pl.BlockSpec((tm, d), lambda i: (i, 0)),
                  pl.BlockSpec((1, d), lambda i: (0, 0)),
                  pl.BlockSpec((d, n), lambda i: (0, 0))],
        out_specs=pl.BlockSpec((tm, n), lambda i: (i, 0)),
        out_shape=jax.ShapeDtypeStruct((m, n), F32),
        compiler_params=_cparams(1),
    )(x, g.reshape(1, d), w)


def _rms_matmul_kvt_kernel(x_ref, g_ref, w_ref, wt_ref, o_ref, ot_ref):
    xn = _rms(x_ref[...], g_ref[...]).astype(BF16)
    o_ref[...] = _dot(xn, w_ref[...])
    ot_ref[...] = _dot_nt(wt_ref[...], xn)


def rms_matmul_kvt(x, g, w, wt, bsz):
    m, d = x.shape
    n = w.shape[1]
    nt = wt.shape[0]
    t = m // bsz
    tm = min(_row_tile(t), 256)
    per = t // tm
    return pl.pallas_call(
        _rms_matmul_kvt_kernel,
        grid=(m // tm,),
        in_specs=[pl.BlockSpec((tm, d), lambda i: (i, 0)),
                  pl.BlockSpec((1, d), lambda i: (0, 0)),
                  pl.BlockSpec((d, n), lambda i: (0, 0)),
                  pl.BlockSpec((nt, d), lambda i: (0, 0))],
        out_specs=[pl.BlockSpec((tm, n), lambda i: (i, 0)),
                   pl.BlockSpec((None, nt, tm), lambda i: (i // per, 0, i % per))],
        out_shape=[jax.ShapeDtypeStruct((m, n), F32), jax.ShapeDtypeStruct((bsz, nt, t), F32)],
        compiler_params=_cparams(1),
    )(x, g.reshape(1, d), w, wt)


def _router_kernel(x_ref, g_ref, w_ref, o_ref):
    o_ref[...] = _dot6(_rms(x_ref[...], g_ref[...]), w_ref[...])


def router_logits(x, g, w):
    m, d = x.shape
    n = w.shape[1]
    tm = min(_row_tile(m), 256)
    return pl.pallas_call(
        _router_kernel,
        grid=(m // tm,),
        in_specs=[pl.BlockSpec((tm, d), lambda i: (i, 0)),
                  pl.BlockSpec((1, d), lambda i: (0, 0)),
                  pl.BlockSpec((d, n), lambda i: (0, 0))],
        out_specs=pl.BlockSpec((tm, n), lambda i: (i, 0)),
        out_shape=jax.ShapeDtypeStruct((m, n), F32),
        compiler_params=_cparams(1),
    )(x, g.reshape(1, d), w)


def _outproj_kernel(h_ref, a_ref, b_ref, w_ref, o_ref):
    ka = a_ref.shape[1]
    o_ref[...] = (h_ref[...] + _dot(a_ref[...].astype(BF16), w_ref[0:ka, :])
                  + _dot(b_ref[...].astype(BF16), w_ref[ka:, :]))


def out_proj(h, a, b, w):
    m, d = h.shape
    ka, kb = a.shape[1], b.shape[1]
    tm = _row_tile(m)
    return pl.pallas_call(
        _outproj_kernel,
        grid=(m // tm,),
        in_specs=[pl.BlockSpec((tm, d), lambda i: (i, 0)),
                  pl.BlockSpec((tm, ka), lambda i: (i, 0)),
                  pl.BlockSpec((tm, kb), lambda i: (i, 0)),
                  pl.BlockSpec((ka + kb, d), lambda i: (0, 0))],
        out_specs=pl.BlockSpec((tm, d), lambda i: (i, 0)),
        out_shape=jax.ShapeDtypeStruct((m, d), F32),
        compiler_params=_cparams(1),
    )(h, a, b, w)


def _ffn_kernel(te_ref, tv_ref, x_ref, g_ref, gate_ref, w1_ref, w3_ref, w2_ref, o_ref,
                xn_ref, acc_ref, *, residual):
    i = pl.program_id(0)
    j = pl.program_id(1)

    @pl.when(j == 0)
    def _():
        xn_ref[...] = _rms(x_ref[...], g_ref[...]).astype(BF16)
        acc_ref[...] = jnp.zeros_like(acc_ref)

    @pl.when(tv_ref[i] != 0)
    def _():
        xn = xn_ref[...]
        a = _dot(xn, w1_ref[...])
        b = _dot(xn, w3_ref[...])
        hid = (a * _sigmoid(a)) * b
        acc_ref[...] += _dot(hid.astype(BF16), w2_ref[...])

    @pl.when(j == pl.num_programs(1) - 1)
    def _():
        y = acc_ref[...] * gate_ref[...]
        if residual:
            y = y + x_ref[...]
        o_ref[...] = y


def grouped_ffn(x, g, row_gate, tile_expert, tile_valid, w1, w3, w2, *, tm, tf, residual):
    m, d = x.shape
    f = w1.shape[2]
    grid_spec = pltpu.PrefetchScalarGridSpec(
        num_scalar_prefetch=2,
        grid=(m // tm, f // tf),
        in_specs=[pl.BlockSpec((tm, d), lambda i, j, te, tv: (i, 0)),
                  pl.BlockSpec((1, d), lambda i, j, te, tv: (0, 0)),
                  pl.BlockSpec((tm, 1), lambda i, j, te, tv: (i, 0)),
                  pl.BlockSpec((None, d, tf), lambda i, j, te, tv: (te[i], 0, j)),
                  pl.BlockSpec((None, d, tf), lambda i, j, te, tv: (te[i], 0, j)),
                  pl.BlockSpec((None, tf, d), lambda i, j, te, tv: (te[i], j, 0))],
        out_specs=pl.BlockSpec((tm, d), lambda i, j, te, tv: (i, 0)),
        scratch_shapes=[pltpu.VMEM((tm, d), BF16), pltpu.VMEM((tm, d), F32)],
    )
    return pl.pallas_call(
        functools.partial(_ffn_kernel, residual=residual),
        grid_spec=grid_spec,
        out_shape=jax.ShapeDtypeStruct((m, d), F32),
        compiler_params=_cparams(2),
    )(tile_expert, tile_valid, x, g.reshape(1, d), row_gate, w1, w3, w2)


def _ple_kernel(h_ref, g_ref, wg_ref, p_ref, wp_ref, gf_ref, o_ref, y_ref):
    h = h_ref[...]
    gate = _sigmoid(_dot(_rms(h, g_ref[...]).astype(BF16), wg_ref[...]))
    proj = _dot(p_ref[...].astype(BF16), wp_ref[...])
    h2 = h + gate * proj
    o_ref[...] = h2
    y_ref[...] = _rms(h2, gf_ref[...])


def ple(h, g, wg, p, wp, gf):
    m, d = h.shape
    pd = p.shape[1]
    tm = _row_tile(m)
    row = lambda i: (i, 0)
    fixed = lambda i: (0, 0)
    return pl.pallas_call(
        _ple_kernel,
        grid=(m // tm,),
        in_specs=[pl.BlockSpec((tm, d), row), pl.BlockSpec((1, d), fixed),
                  pl.BlockSpec((d, d), fixed), pl.BlockSpec((tm, pd), row),
                  pl.BlockSpec((pd, d), fixed), pl.BlockSpec((1, d), fixed)],
        out_specs=[pl.BlockSpec((tm, d), row), pl.BlockSpec((tm, d), row)],
        out_shape=[jax.ShapeDtypeStruct((m, d), F32), jax.ShapeDtypeStruct((m, d), F32)],
        compiler_params=_cparams(1),
    )(h, g.reshape(1, d), wg, p, wp, gf.reshape(1, d))


def _bias_kernel(tab_ref, d_ref, o_ref):
    n = jnp.maximum(d_ref[...], 0)
    max_exact = N_BUCKETS // 2
    nf = jnp.maximum(n, 1).astype(F32)
    large = max_exact + (jnp.log(nf / max_exact) / math.log(MAX_DISTANCE / max_exact)
                         * (N_BUCKETS - max_exact)).astype(I32)
    large = jnp.minimum(large, N_BUCKETS - 1)
    bucket = jnp.where(n < max_exact, n, large)
    for h in range(H_NSA):
        acc = jnp.zeros(n.shape, F32)
        for k in range(N_BUCKETS):
            acc = jnp.where(bucket == k, tab_ref[k, h], acc)
        o_ref[h] = acc


def bias_lookup(table, dist):
    r, c = dist.shape
    tr = min(r, 128)
    tc = 512 if c % 512 == 0 else c
    return pl.pallas_call(
        _bias_kernel,
        grid=(r // tr, c // tc),
        in_specs=[pl.BlockSpec(memory_space=pltpu.SMEM),
                  pl.BlockSpec((tr, tc), lambda i, j: (i, j))],
        out_specs=pl.BlockSpec((H_NSA, tr, tc), lambda i, j: (0, i, j)),
        out_shape=jax.ShapeDtypeStruct((H_NSA, r, c), F32),
        compiler_params=_cparams(2),
        name="bias_lookup",
    )(table, dist)


def _swap_halves(x):
    lane = lax.broadcasted_iota(I32, x.shape, 1)
    return jnp.where(lane % HEAD_DIM < HEAD_DIM // 2,
                     pltpu.roll(x, LANES - HEAD_DIM // 2, 1), pltpu.roll(x, HEAD_DIM // 2, 1))


def _retention_prompt_kernel(q_ref, k_ref, v_ref, g_ref, cos_ref, sin_ref, gain_ref, intra_ref,
                             qdec_ref, kdec_ref, cdec_ref, y_ref, s_out_ref, s_ref):
    c = pl.program_id(1)

    @pl.when(c == 0)
    def _():
        s_ref[...] = jnp.zeros_like(s_ref)

    cos = cos_ref[...]
    sin = sin_ref[...]
    lane = lax.broadcasted_iota(I32, (RET_CHUNK, LANES), 1)
    left = lane < HEAD_DIM
    row = lax.broadcasted_iota(I32, (LANES, LANES), 0)
    col = lax.broadcasted_iota(I32, (LANES, LANES), 1)
    same_head = (row < HEAD_DIM) == (col < HEAD_DIM)
    ones_bd = jnp.where(same_head, 1.0, 0.0).astype(BF16)
    for p in range(H_RET // 2):
        sl = slice(p * LANES, (p + 1) * LANES)
        q = q_ref[:, sl]
        k = k_ref[:, sl]
        v = v_ref[:, sl].astype(BF16)
        qr = q * cos + _swap_halves(q) * sin
        kr = (k * cos + _swap_halves(k) * sin) * (HEAD_DIM ** -0.5)
        qb = qr.astype(BF16)
        q2 = jnp.concatenate([jnp.where(left, qr, 0.0), jnp.where(left, 0.0, qr)], axis=0).astype(BF16)
        inner = _dot_nt(q2, kr.astype(BF16)) * intra_ref[p]
        o2 = _dot(inner.astype(BF16), v)
        s_old = s_ref[p]
        cross = _dot(qb, s_old.astype(BF16)) * qdec_ref[:, sl]
        o = jnp.where(left, o2[:RET_CHUNK], o2[RET_CHUNK:]) + cross
        kd = (kr * kdec_ref[:, sl]).T.astype(BF16)
        s_ref[p] = s_old * cdec_ref[p] + jnp.where(same_head, _dot(kd, v), 0.0)
        ms = _dot3(o * o, ones_bd) * (1.0 / HEAD_DIM)
        gate = g_ref[:, sl]
        y = o * lax.rsqrt(ms + EPS) * gain_ref[:, sl] * (gate * _sigmoid(gate))
        y_ref[:, sl] = y.astype(y_ref.dtype)

    @pl.when(c == pl.num_programs(1) - 1)
    def _():
        for p in range(H_RET // 2):
            s = s_ref[p]
            s_out_ref[2 * p] = s[:HEAD_DIM, :HEAD_DIM]
            s_out_ref[2 * p + 1] = s[HEAD_DIM:, HEAD_DIM:]


def _retention_consts(c):
    log_g = jnp.log1p(-jnp.exp2(-5.0 - jnp.arange(H_RET, dtype=F32)))
    j = jnp.arange(c, dtype=F32)
    diff = j[:, None] - j[None, :]
    intra = jnp.where(diff >= 0, jnp.exp(jnp.maximum(diff, 0.0)[None] * log_g[:, None, None]), 0.0)
    q_decay = jnp.exp((j[:, None] + 1.0) * log_g[None, :])
    k_decay = jnp.exp((c - 1.0 - j[:, None]) * log_g[None, :])
    c_decay = jnp.exp(c * log_g)
    intra2 = intra.reshape(H_RET // 2, 2 * c, c)
    qdec = jnp.repeat(q_decay, HEAD_DIM, axis=1)
    kdec = jnp.repeat(k_decay, HEAD_DIM, axis=1)
    cdec = jnp.repeat(c_decay.reshape(H_RET // 2, 2), HEAD_DIM, axis=1)[:, :, None] * jnp.ones((1, 1, LANES), F32)
    return intra2, qdec, kdec, cdec


def _rotary_tables(pos):
    half = HEAD_DIM // 2
    inv = ROPE_BASE ** (-jnp.arange(half, dtype=F32) / half)
    ang = pos.astype(F32)[:, None] * inv[None, :]
    cos, sin = jnp.cos(ang), jnp.sin(ang)
    cos_t = jnp.concatenate([cos, cos, cos, cos], axis=1)
    sin_t = jnp.concatenate([-sin, sin, -sin, sin], axis=1)
    return cos_t, sin_t


def retention_prompt(z3, gain):
    b, t, _ = z3.shape
    c = RET_CHUNK
    cos_t, sin_t = _rotary_tables(jnp.arange(t))
    intra2, qdec, kdec, cdec = _retention_consts(c)
    blk = lambda col: pl.BlockSpec((None, c, RET_W), lambda bi, ci, col=col: (bi, ci, col))
    fixed2 = lambda shape: pl.BlockSpec(shape, lambda bi, ci: (0,) * len(shape))
    return pl.pallas_call(
        _retention_prompt_kernel,
        grid=(b, t // c),
        in_specs=[blk(COL_RQ // RET_W), blk(COL_RK // RET_W), blk(COL_RV // RET_W), blk(COL_RG // RET_W),
                  pl.BlockSpec((c, LANES), lambda bi, ci: (ci, 0)),
                  pl.BlockSpec((c, LANES), lambda bi, ci: (ci, 0)),
                  fixed2((1, RET_W)), fixed2((H_RET // 2, 2 * c, c)), fixed2((c, RET_W)),
                  fixed2((c, RET_W)), fixed2((H_RET // 2, LANES, LANES))],
        out_specs=[pl.BlockSpec((None, c, RET_W), lambda bi, ci: (bi, ci, 0)),
                   pl.BlockSpec((None, H_RET, HEAD_DIM, HEAD_DIM), lambda bi, ci: (bi, 0, 0, 0))],
        out_shape=[jax.ShapeDtypeStruct((b, t, RET_W), BF16),
                   jax.ShapeDtypeStruct((b, H_RET, HEAD_DIM, HEAD_DIM), F32)],
        scratch_shapes=[pltpu.VMEM((H_RET // 2, LANES, LANES), F32)],
        compiler_params=_cparams(2),
    )(z3, z3, z3, z3, cos_t, sin_t, gain.reshape(1, RET_W), intra2, qdec, kdec, cdec)


def _retention_step_kernel(q_ref, k_ref, v_ref, g_ref, s_ref, cos_ref, sin_ref, gamma_ref, gain_ref,
                           y_ref, s_out_ref):
    half = HEAD_DIM // 2
    cos = cos_ref[...]
    sin = sin_ref[...]

    def rot(x):
        x1, x2 = x[:half], x[half:]
        return jnp.concatenate([x1 * cos - x2 * sin, x1 * sin + x2 * cos], axis=0)

    qr = rot(q_ref[...])
    kr = rot(k_ref[...]) * (HEAD_DIM ** -0.5)
    v = v_ref[...]
    gamma = gamma_ref[...]
    o = jnp.zeros_like(v)
    for d in range(HEAD_DIM):
        s_new = s_ref[d] * gamma + kr[d:d + 1] * v
        s_out_ref[d] = s_new
        o = o + qr[d:d + 1] * s_new
    ms = jnp.mean(o * o, axis=0, keepdims=True)
    gate = g_ref[...]
    y_ref[...] = o * lax.rsqrt(ms + EPS) * gain_ref[...] * (gate * _sigmoid(gate))


def retention_step(zt, state, layer, pos, gain):
    bs = zt.shape[1]
    half = HEAD_DIM // 2
    inv = ROPE_BASE ** (-jnp.arange(half, dtype=F32) / half)
    ang = jnp.full((1,), pos).astype(F32)[:, None] * inv[None, :]
    cos = jnp.broadcast_to(jnp.cos(ang).reshape(half, 1), (half, bs))
    sin = jnp.broadcast_to(jnp.sin(ang).reshape(half, 1), (half, bs))
    gamma = 1.0 - jnp.exp2(-5.0 - jnp.arange(H_RET, dtype=F32))
    gamma = jnp.broadcast_to(gamma.reshape(H_RET, 1, 1), (H_RET, 1, bs))
    gain_b = jnp.broadcast_to(gain.reshape(H_RET, HEAD_DIM, 1), (H_RET, HEAD_DIM, bs))
    head = lambda col: pl.BlockSpec((HEAD_DIM, bs), lambda h, col=col: (col // HEAD_DIM + h, 0))
    fixed = pl.BlockSpec((half, bs), lambda h: (0, 0))
    return pl.pallas_call(
        _retention_step_kernel,
        grid=(H_RET,),
        in_specs=[head(COL_RQ), head(COL_RK), head(COL_RV), head(COL_RG),
                  pl.BlockSpec((None, HEAD_DIM, HEAD_DIM, bs), lambda h: (layer * H_RET + h, 0, 0, 0)),
                  fixed, fixed,
                  pl.BlockSpec((None, 1, bs), lambda h: (h, 0, 0)),
                  pl.BlockSpec((None, HEAD_DIM, bs), lambda h: (h, 0, 0))],
        out_specs=[pl.BlockSpec((HEAD_DIM, bs), lambda h: (h, 0)),
                   pl.BlockSpec((None, HEAD_DIM, HEAD_DIM, bs), lambda h: (h, 0, 0, 0))],
        out_shape=[jax.ShapeDtypeStruct((RET_W, bs), F32),
                   jax.ShapeDtypeStruct((H_RET, HEAD_DIM, HEAD_DIM, bs), F32)],
        compiler_params=_cparams(1),
    )(zt, zt, zt, zt, state, cos, sin, gamma, gain_b)


def _gelu_tanh(x):
    return 0.5 * x * (1.0 + jnp.tanh(math.sqrt(2.0 / math.pi) * (x + 0.044715 * (x * x * x))))


def _compress_kernel(*refs, n_pieces):
    x_refs = refs[1:1 + n_pieces]
    pe_ref, w1_ref, w2_ref, kc_ref, vc_ref, rows_ref = refs[1 + n_pieces:]
    width = x_refs[0].shape[1]
    n = n_pieces * width // CMP_STRIDE
    for kv, o_ref in enumerate((kc_ref, vc_ref)):
        for p, x_ref in enumerate(x_refs):
            for c in range(width // LANES):
                rows_ref[pl.ds(p * width + c * LANES, LANES), :] = (
                    x_ref[kv * LANES:(kv + 1) * LANES, c * LANES:(c + 1) * LANES].T)
        xk = jnp.concatenate([rows_ref[pl.ds(j, n, stride=CMP_STRIDE), :] for j in range(CMP_STRIDE)], axis=1)
        top = _dot((xk + pe_ref[kv, 0]).astype(BF16), w1_ref[kv, 0])
        bot = _dot((xk + pe_ref[kv, 1]).astype(BF16), w1_ref[kv, 1])
        pre = top + pltpu.roll(bot, n - 1, 0)
        hid = _gelu_tanh(pre).astype(BF16)
        o_ref[...] = jnp.concatenate(
            [_dot(hid[:, g * CMP_HIDDEN:(g + 1) * CMP_HIDDEN], w2_ref[kv]) for g in range(N_KV)], axis=1)


def _compress_weights(pe, w1, w2):
    pe_r = pe.reshape(2, 2, CMP_STRIDE, 1, HEAD_DIM)
    pe_t = jnp.broadcast_to(pe_r, (2, 2, CMP_STRIDE, N_KV, HEAD_DIM)).reshape(2, 2, 1, CMP_STRIDE * LANES)
    w1r = w1.reshape(2, 2, CMP_STRIDE, HEAD_DIM, CMP_HIDDEN)
    eye = jnp.eye(N_KV, dtype=w1.dtype)
    wbig = jnp.einsum('khjdc,gG->khjgdGc', w1r, eye)
    wbig = wbig.reshape(2, 2, CMP_STRIDE * LANES, N_KV * CMP_HIDDEN)
    return pe_t.astype(F32), wbig.astype(BF16), w2.astype(BF16)


def compress(src, idx, pe_t, w1b, w2b):
    bsz, n_pieces = idx.shape
    width = src.shape[2]
    n = n_pieces * width // CMP_STRIDE
    piece = lambda p: pl.BlockSpec((None, KV_W, width), lambda b, ix, p=p: (ix[b, p], 0, 0))
    fixed = lambda shape: pl.BlockSpec(shape, lambda b, ix: (0,) * len(shape))
    out = pl.BlockSpec((None, n, LANES), lambda b, ix: (b, 0, 0))
    grid_spec = pltpu.PrefetchScalarGridSpec(
        num_scalar_prefetch=1,
        grid=(bsz,),
        in_specs=[piece(p) for p in range(n_pieces)]
        + [fixed(pe_t.shape), fixed(w1b.shape), fixed(w2b.shape)],
        out_specs=[out, out],
        scratch_shapes=[pltpu.VMEM((n_pieces * width, LANES), F32)],
    )
    return pl.pallas_call(
        functools.partial(_compress_kernel, n_pieces=n_pieces),
        grid_spec=grid_spec,
        out_shape=[jax.ShapeDtypeStruct((bsz, n, LANES), F32)] * 2,
        compiler_params=_cparams(1),
    )(idx, *([src] * n_pieces), pe_t, w1b, w2b)


def _masked_softmax(s, mask):
    s = jnp.where(mask, s, NEG)
    m = jnp.max(s, axis=-1, keepdims=True)
    e = jnp.where(mask, jnp.exp(s - m), 0.0)
    return e / jnp.maximum(jnp.sum(e, axis=-1, keepdims=True), TINY)


def _block_scores(imp, qpos, n_sel, axis=1):
    blk = lax.broadcasted_iota(I32, imp.shape, axis)
    cur = qpos // SEL_BLOCK
    forced = (blk == 0) | (blk == cur) | (blk == cur - 1)
    valid = blk * SEL_BLOCK <= qpos
    score = jnp.where(forced, FORCED_SCORE, jnp.where(valid, imp, -1.0))
    return jnp.where(blk < n_sel, score, -2.0)


def _select_top(score, n_top, axis=1):
    blk = lax.broadcasted_iota(I32, score.shape, axis)
    rank = jnp.zeros(score.shape, F32)
    for i in range(score.shape[axis]):
        one = score[:, i:i + 1] if axis == 1 else score[i:i + 1, :]
        beats = (one > score) | ((one == score) & (blk > i))
        rank = rank + jnp.where(beats, 1.0, 0.0)
    return jnp.where(rank < n_top, 1.0, 0.0)


def _attend_t(s, ok, m_prev, l_prev):
    s = jnp.where(ok, s, NEG)
    m_new = jnp.maximum(m_prev, jnp.max(s, axis=0, keepdims=True))
    alpha = jnp.exp(m_prev - m_new)
    p = jnp.exp(s - m_new)
    l_new = alpha * l_prev + jnp.sum(p, axis=0, keepdims=True)
    return m_new, l_new, alpha, p


def _nsa_prompt_kernel(q_ref, gate_ref, kc_ref, vct_ref, ks_ref, vst_ref, kw_ref, vwt_ref, bc_ref, bt_ref,
                       o_ref, qt_ref, pcs_ref, pick_ref, acc_ref, os_ref, *, n_cmp, n_sel):
    i = pl.program_id(1)
    qs = i * Q_BLOCK
    key_i = lax.broadcasted_iota(I32, (KV_CHUNK, Q_BLOCK), 0)
    tok_i = lax.broadcasted_iota(I32, (KV_CHUNK, Q_BLOCK), 1)
    row8 = lax.broadcasted_iota(I32, (H_NSA, Q_BLOCK), 0)
    zeros_half = jnp.zeros((HEAD_DIM, Q_BLOCK), F32)

    for k2 in range(H_NSA // 2):
        slab_t = q_ref[:, k2 * LANES:(k2 + 1) * LANES].T * (HEAD_DIM ** -0.5)
        for h in (2 * k2, 2 * k2 + 1):
            qh = slab_t[(h % 2) * HEAD_DIM:(h % 2 + 1) * HEAD_DIM]
            both = [qh, zeros_half] if h // R_GRP == 0 else [zeros_half, qh]
            qt_ref[:, h * Q_BLOCK:(h + 1) * Q_BLOCK] = jnp.concatenate(both, axis=0).astype(BF16)

    sig_t = _sigmoid(gate_ref[...]).T

    ncp = kc_ref.shape[0]
    kc = kc_ref[...].astype(BF16)
    n_c = lax.broadcasted_iota(I32, (ncp, Q_BLOCK), 0)
    t_c = lax.broadcasted_iota(I32, (ncp, Q_BLOCK), 1)
    ok_c = (n_c * CMP_STRIDE + (L_CMP - 1) <= qs + t_c) & (n_c < n_cmp)
    blk_pos = qs + lax.broadcasted_iota(I32, (pick_ref.shape[1], Q_BLOCK), 1)
    o_c = []
    for g in range(N_KV):
        vct = vct_ref[g * HEAD_DIM:(g + 1) * HEAD_DIM, :].astype(BF16)
        pcs = jnp.zeros((ncp, Q_BLOCK), F32)
        for r in range(R_GRP):
            h = g * R_GRP + r
            s = _dot(kc, qt_ref[:, h * Q_BLOCK:(h + 1) * Q_BLOCK]) + bc_ref[h]
            s = jnp.where(ok_c, s, NEG)
            e = jnp.where(ok_c, jnp.exp(s - jnp.max(s, axis=0, keepdims=True)), 0.0)
            p = e * (1.0 / jnp.maximum(jnp.sum(e, axis=0, keepdims=True), TINY))
            o_c.append(_dot(vct, p.astype(BF16)))
            pcs = pcs + p
        pcs_ref[...] = pcs
        per = SEL_BLOCK // CMP_STRIDE
        imp = pcs_ref[pl.ds(0, ncp // per, stride=per), :]
        for c in range(1, per):
            imp = imp + pcs_ref[pl.ds(c, ncp // per, stride=per), :]
        pick_ref[g] = _select_top(_block_scores(imp, blk_pos, n_sel, axis=0), min(N_SELECT, n_sel), axis=0)

    def attend_chunk(k_ref, vt_ref, off, rel, ok_g, carry):
        m_all, l_all = carry
        k = k_ref[pl.ds(off, KV_CHUNK), :].astype(BF16)
        bias_idx = jnp.minimum(rel, FAR_REL)
        for g in range(N_KV):
            vt = vt_ref[g * HEAD_DIM:(g + 1) * HEAD_DIM, pl.ds(off, KV_CHUNK)].astype(BF16)
            for r in range(R_GRP):
                h = g * R_GRP + r
                s = _dot(k, qt_ref[:, h * Q_BLOCK:(h + 1) * Q_BLOCK]) + bt_ref[bias_idx, h]
                m_new, l_new, alpha, p = _attend_t(s, ok_g[g], m_all[h:h + 1], l_all[h:h + 1])
                m_all = jnp.where(row8 == h, m_new, m_all)
                l_all = jnp.where(row8 == h, l_new, l_all)
                acc_ref[h] = acc_ref[h] * alpha + _dot(vt, p.astype(BF16))
        return m_all, l_all

    init = (jnp.full((H_NSA, Q_BLOCK), NEG, F32), jnp.zeros((H_NSA, Q_BLOCK), F32))
    dmat = tok_i - key_i

    acc_ref[...] = jnp.zeros_like(acc_ref)

    def sel_body(j, carry):
        rel = i - j
        off = pl.multiple_of(j * KV_CHUNK, KV_CHUNK)
        causal = rel * KV_CHUNK + dmat >= 0
        per = KV_CHUNK // SEL_BLOCK
        ok_g = []
        for g in range(N_KV):
            picked = jnp.concatenate(
                [jnp.broadcast_to(pick_ref[g, pl.ds(j * per + c, 1), :], (SEL_BLOCK, Q_BLOCK)) for c in range(per)],
                axis=0)
            ok_g.append(causal & (picked > 0.5))
        return attend_chunk(ks_ref, vst_ref, off, rel, ok_g, carry)

    m_s, l_s = lax.fori_loop(0, i + 1, sel_body, init)
    for h in range(H_NSA):
        os_ref[h] = jnp.where(m_s[h:h + 1] > 0.5 * NEG, acc_ref[h] / jnp.maximum(l_s[h:h + 1], TINY), 0.0)

    acc_ref[...] = jnp.zeros_like(acc_ref)
    n_back = WINDOW // KV_CHUNK

    def win_body(c, carry):
        j = i - n_back + c
        rel = n_back - c
        off = pl.multiple_of(jnp.maximum(j, 0) * KV_CHUNK, KV_CHUNK)
        dist = rel * KV_CHUNK + dmat + jnp.where(j < 0, WINDOW, 0)
        ok = (dist >= 0) & (dist < WINDOW)
        return attend_chunk(kw_ref, vwt_ref, off, rel, [ok, ok], carry)

    m_w, l_w = lax.fori_loop(0, n_back + 1, win_body, init)

    for k2 in range(H_NSA // 2):
        pair = []
        for h in (2 * k2, 2 * k2 + 1):
            o_w = jnp.where(m_w[h:h + 1] > 0.5 * NEG, acc_ref[h] / jnp.maximum(l_w[h:h + 1], TINY), 0.0)
            pair.append(sig_t[3 * h:3 * h + 1] * o_c[h] + sig_t[3 * h + 1:3 * h + 2] * os_ref[h]
                        + sig_t[3 * h + 2:3 * h + 3] * o_w)
        o_ref[:, k2 * LANES:(k2 + 1) * LANES] = jnp.concatenate(pair, axis=0).T.astype(o_ref.dtype)


def _selection_consts(n_rows_cmp, n_keys):
    n = np.arange(n_rows_cmp)
    blk = np.arange(SEL_LANES)
    pool = (n[:, None] // (SEL_BLOCK // CMP_STRIDE) == blk[None, :]).astype(np.float32)
    key = np.arange(n_keys)
    expand = (blk[:, None] == key[None, :] // SEL_BLOCK).astype(np.float32)
    return jnp.asarray(pool, BF16), jnp.asarray(expand, BF16)


def nsa_prompt(z3, kvt, kc, vct, bias_ct, bias_tt):
    b, t, _ = z3.shape
    n_sub = t // CMP_STRIDE
    n_sel = -(-t // SEL_BLOCK)
    assert kc.shape[1] == n_sub and n_sub == n_sel * (SEL_BLOCK // CMP_STRIDE) and n_sel % SUBLANES == 0
    fixed = lambda shape: pl.BlockSpec(shape, lambda bi, i: (0,) * len(shape))
    v_rows = lambda branch: (branch * KV_W + N_KV * HEAD_DIM) // LANES
    return pl.pallas_call(
        functools.partial(_nsa_prompt_kernel, n_cmp=n_sub - 1, n_sel=n_sel),
        grid=(b, t // Q_BLOCK),
        in_specs=[pl.BlockSpec((None, Q_BLOCK, NSA_QW), lambda bi, i: (bi, i, COL_NQ // NSA_QW)),
                  pl.BlockSpec((None, Q_BLOCK, LANES), lambda bi, i: (bi, i, COL_NG // LANES)),
                  pl.BlockSpec((None, n_sub, LANES), lambda bi, i: (bi, 0, 0)),
                  pl.BlockSpec((None, LANES, n_sub), lambda bi, i: (bi, 0, 0)),
                  pl.BlockSpec((None, t, LANES), lambda bi, i: (bi, 0, COL_KVS // LANES)),
                  pl.BlockSpec((None, LANES, t), lambda bi, i: (bi, v_rows(1), 0)),
                  pl.BlockSpec((None, t, LANES), lambda bi, i: (bi, 0, COL_KVW // LANES)),
                  pl.BlockSpec((None, LANES, t), lambda bi, i: (bi, v_rows(2), 0)),
                  pl.BlockSpec((H_NSA, n_sub, Q_BLOCK), lambda bi, i: (0, 0, i)),
                  fixed(bias_tt.shape)],
        out_specs=pl.BlockSpec((None, Q_BLOCK, NSA_QW), lambda bi, i: (bi, i, 0)),
        out_shape=jax.ShapeDtypeStruct((b, t, NSA_QW), BF16),
        scratch_shapes=[pltpu.VMEM((LANES, H_NSA * Q_BLOCK), BF16),
                        pltpu.VMEM((n_sub, Q_BLOCK), F32),
                        pltpu.VMEM((N_KV, n_sel, Q_BLOCK), F32),
                        pltpu.VMEM((H_NSA, HEAD_DIM, Q_BLOCK), F32),
                        pltpu.VMEM((H_NSA, HEAD_DIM, Q_BLOCK), F32)],
        compiler_params=_cparams(2),
        name="nsa_prompt",
    )(z3, z3, kc, vct, z3, kvt, z3, kvt, bias_ct, bias_tt)


def _nsa_step_kernel(pt_ref, z_ref, kc_ref, vc_ref, *rest, n_pages, past, wbuf, n_cmp, n_sel):
    del pt_ref
    pages = rest[:n_pages]
    win_ref, bcs_ref, bss_ref, bws_ref, pool_ref, exp_ref, o_ref, wo_ref = rest[n_pages:]
    z = z_ref[...]
    row8 = lax.broadcasted_iota(I32, (H_NSA, LANES), 0)
    lane8 = lax.broadcasted_iota(I32, (H_NSA, LANES), 1)
    left8 = lane8 < HEAD_DIM
    qbd = jnp.zeros((H_NSA, LANES), F32)
    for h in range(H_NSA):
        slab = jnp.broadcast_to(z[:, COL_NQ + (h // 2) * LANES:COL_NQ + (h // 2 + 1) * LANES], (H_NSA, LANES))
        g = h // R_GRP
        if h % 2 != g:
            slab = pltpu.roll(slab, HEAD_DIM, 1)
        keep = (row8 == h) & (left8 if g == 0 else jnp.logical_not(left8))
        qbd = jnp.where(keep, slab, qbd)
    qbd = (qbd * (HEAD_DIM ** -0.5)).astype(BF16)

    ncp = kc_ref.shape[0]
    n_c = lax.broadcasted_iota(I32, (H_NSA, ncp), 1)
    s_c = _dot_nt(qbd, kc_ref[...].astype(BF16)) + bcs_ref[...]
    p_c = _masked_softmax(s_c, (n_c * CMP_STRIDE + (L_CMP - 1) <= past) & (n_c < n_cmp))
    o_c = _dot(p_c.astype(BF16), vc_ref[...].astype(BF16))
    pc0 = jnp.sum(p_c[0:R_GRP], axis=0, keepdims=True)
    pc1 = jnp.sum(p_c[R_GRP:], axis=0, keepdims=True)
    rowc = lax.broadcasted_iota(I32, (H_NSA, ncp), 0)
    pcs = jnp.where(rowc < R_GRP, jnp.broadcast_to(pc0, (H_NSA, ncp)), jnp.broadcast_to(pc1, (H_NSA, ncp)))
    imp = _dot3(pcs, pool_ref[...])
    sel = _select_top(_block_scores(imp, jnp.full(imp.shape, past, I32), n_sel), min(N_SELECT, n_sel))
    picked = _dot(sel.astype(BF16), exp_ref[...])

    c128 = lax.broadcasted_iota(I32, (LANES, KV_CHUNK), 1)

    def new_col(col):
        return jnp.broadcast_to(z[:, col:col + LANES], (KV_CHUNK, LANES)).T

    def first_col(tile):
        return jnp.where(c128 == 0, tile, 0.0).astype(BF16)

    ks = [pg[0:LANES, :].astype(BF16) for pg in pages] + [first_col(new_col(COL_KVS))]
    vs = [pg[LANES:2 * LANES, :].astype(BF16) for pg in pages] + [first_col(new_col(COL_KVS + LANES))]
    s_s = jnp.concatenate([_dot(qbd, k) for k in ks], axis=1) + bss_ref[...]
    key = lax.broadcasted_iota(I32, s_s.shape, 1)
    p_s = _masked_softmax(s_s, (picked > 0.5) & (key <= past)).astype(BF16)
    o_s = _dot_nt(p_s[:, 0:KV_CHUNK], vs[0])
    for c in range(1, len(vs)):
        o_s = o_s + _dot_nt(p_s[:, c * KV_CHUNK:(c + 1) * KV_CHUNK], vs[c])

    kw_new, vw_new = new_col(COL_KVW), new_col(COL_KVW + LANES)
    win = win_ref[...]
    s_w = jnp.concatenate([_dot(qbd, win[0:LANES].astype(BF16)), _dot(qbd, first_col(kw_new))], axis=1) + bws_ref[...]
    colw = lax.broadcasted_iota(I32, s_w.shape, 1)
    p_w = _masked_softmax(s_w, (colw <= wbuf) & (wbuf - colw < WINDOW)).astype(BF16)
    o_w = _dot_nt(p_w[:, 0:wbuf], win[LANES:].astype(BF16)) + _dot_nt(p_w[:, wbuf:], first_col(vw_new))

    shifted = pltpu.roll(win, wbuf - 1, 1)
    new_kv = jnp.concatenate([kw_new, vw_new], axis=0)
    last = jnp.where(lax.broadcasted_iota(I32, (KV_W, LANES), 1) == LANES - 1, new_kv, shifted[:, wbuf - LANES:])
    wo_ref[:, 0:wbuf - LANES] = shifted[:, 0:wbuf - LANES]
    wo_ref[:, wbuf - LANES:] = last

    sig = jnp.broadcast_to(_sigmoid(z[:, COL_NG:COL_NG + LANES]), (H_NSA, LANES))

    def gate_col(c):
        return jnp.sum(jnp.where(lane8 == row8 * 3 + c, sig, 0.0), axis=1, keepdims=True)

    o = gate_col(0) * o_c + gate_col(1) * o_s + gate_col(2) * o_w
    fix = jnp.where((row8 % 2) != (row8 // R_GRP), pltpu.roll(o, HEAD_DIM, 1), o)
    left1 = left8[0:1]
    o_ref[...] = jnp.concatenate(
        [jnp.where(left1, fix[2 * k2:2 * k2 + 1], fix[2 * k2 + 1:2 * k2 + 2]) for k2 in range(H_NSA // 2)], axis=1)


def nsa_step(zs3, kc, vc, sel_cache, win_cache, page_idx, win_row0, bias_cs, bias_ss, bias_ws, past):
    bs = zs3.shape[0]
    n_pages = page_idx.shape[1]
    wbuf = win_cache.shape[2]
    n_sub = (past + 1) // CMP_STRIDE
    n_sel = -(-(past + 1) // SEL_BLOCK)
    n_keys = (n_pages + 1) * PAGE_SIZE
    assert n_sel <= SEL_LANES and PAGE_SIZE == KV_CHUNK and wbuf % KV_CHUNK == 0 and kc.shape[1] == n_sub
    pool, expand = _selection_consts(n_sub, n_keys)
    fixed = lambda shape: pl.BlockSpec(shape, lambda b, pt: (0,) * len(shape))
    page = lambda p: pl.BlockSpec((None, KV_W, PAGE_SIZE), lambda b, pt, p=p: (pt[b, p], 0, 0))
    grid_spec = pltpu.PrefetchScalarGridSpec(
        num_scalar_prefetch=1,
        grid=(bs,),
        in_specs=[pl.BlockSpec((None, 1, N_IN_PAD), lambda b, pt: (b, 0, 0)),
                  pl.BlockSpec((None, n_sub, LANES), lambda b, pt: (b, 0, 0)),
                  pl.BlockSpec((None, n_sub, LANES), lambda b, pt: (b, 0, 0))]
        + [page(p) for p in range(n_pages)]
        + [pl.BlockSpec((None, KV_W, wbuf), lambda b, pt: (win_row0 + b, 0, 0)),
           fixed(bias_cs.shape), fixed(bias_ss.shape), fixed(bias_ws.shape), fixed(pool.shape), fixed(expand.shape)],
        out_specs=[pl.BlockSpec((None, 1, NSA_QW), lambda b, pt: (b, 0, 0)),
                   pl.BlockSpec((None, KV_W, wbuf), lambda b, pt: (b, 0, 0))],
    )
    return pl.pallas_call(
        functools.partial(_nsa_step_kernel, n_pages=n_pages, past=past, wbuf=wbuf, n_cmp=n_sub - 1, n_sel=n_sel),
        grid_spec=grid_spec,
        out_shape=[jax.ShapeDtypeStruct((bs, 1, NSA_QW), F32), jax.ShapeDtypeStruct((bs, KV_W, wbuf), F32)],
        compiler_params=_cparams(1),
    )(page_idx, zs3, kc, vc, *([sel_cache] * n_pages), win_cache, bias_cs, bias_ss, bias_ws, pool, expand)


def _route(logits, tm):
    m = logits.shape[0]
    top_v, top_i = lax.top_k(logits, TOP_K)
    top_w = jax.nn.softmax(top_v, axis=-1)
    slot_e = top_i.reshape(-1)
    onehot = (slot_e[:, None] == jnp.arange(N_EXPERTS)[None, :]).astype(I32)
    before = jnp.cumsum(onehot, axis=0) - onehot
    rank = jnp.sum(before * onehot, axis=1)
    count = jnp.sum(onehot, axis=0)
    padded = -(-count // tm) * tm
    start = jnp.cumsum(padded) - padded
    pos = start[slot_e] + rank
    n_rows = (-(-(m * TOP_K) // tm) + N_EXPERTS) * tm
    row_token = jnp.zeros((n_rows,), I32).at[pos].set(jnp.arange(m * TOP_K, dtype=I32) // TOP_K)
    row_gate = jnp.zeros((n_rows,), F32).at[pos].set(top_w.reshape(-1))
    tile_start = jnp.arange(n_rows // tm, dtype=I32) * tm
    tile_expert = jnp.clip(jnp.searchsorted(jnp.cumsum(padded), tile_start, side='right'), 0, N_EXPERTS - 1)
    tile_valid = (tile_start < jnp.sum(padded)).astype(I32)
    return pos.reshape(m, TOP_K), row_token, row_gate.reshape(n_rows, 1), tile_expert.astype(I32), tile_valid


def _dense_ffn(h, g, w1, w3, w2):
    m = h.shape[0]
    tm = _row_tile(m)
    ones = jnp.ones((m // tm,), I32)
    return grouped_ffn(h, g, jnp.ones((m, 1), F32), jnp.zeros((m // tm,), I32), ones,
                       w1[None], w3[None], w2[None], tm=tm, tf=w1.shape[1] // 2, residual=True)


def kernel(x_prompt, x_sample, p_prompt, p_sample, state_ret, cache_win_kv, cache_cmp_kv, cache_sel_kv, page_table, rel_bias, norm_mix, w_in, ret_gain, cmp_pe, cmp_w1, cmp_w2, w_out, norm_ffn, ffn_w1, ffn_w3, ffn_w2, router, moe_w1, moe_w3, moe_w2, ple_norm, ple_gate, ple_proj, norm_final):
    b_p, t_p, d = x_prompt.shape
    b_s = x_sample.shape[0]
    depth = w_in.shape[0]
    n_pages = page_table.shape[1]
    past = n_pages * PAGE_SIZE
    n_pool = cache_cmp_kv.shape[1]
    w_buf = cache_win_kv.shape[2]
    assert x_sample.shape[1] == 1 and t_p % Q_BLOCK == 0 and t_p >= w_buf and past % CMP_STRIDE == 0
    row = (2, N_KV, HEAD_DIM)
    m_p = b_p * t_p

    ar = lambda n: jnp.arange(n, dtype=I32)
    n_sub_p = t_p // CMP_STRIDE
    bias_ct = bias_lookup(rel_bias, ar(t_p)[None, :] - (ar(n_sub_p)[:, None] * CMP_STRIDE + L_CMP - 1))
    rel_t = (ar(FAR_REL + 1)[:, None, None] * KV_CHUNK + ar(Q_BLOCK)[None, None, :] - ar(KV_CHUNK)[None, :, None])
    bias_tt = bias_lookup(rel_bias, rel_t.reshape((FAR_REL + 1) * KV_CHUNK, Q_BLOCK))
    bias_tt = bias_tt.reshape(H_NSA, FAR_REL + 1, KV_CHUNK, Q_BLOCK).transpose(1, 0, 2, 3)
    n_sub_s = (past + 1) // CMP_STRIDE
    rows8 = jnp.zeros((SUBLANES, 1), I32)
    bias_cs = bias_lookup(rel_bias, rows8 + (past - (ar(n_sub_s)[None, :] * CMP_STRIDE + L_CMP - 1)))[:, 0]
    bias_ss = bias_lookup(rel_bias, rows8 + (past - ar((n_pages + 1) * PAGE_SIZE)[None, :]))[:, 0]
    bias_ws = bias_lookup(rel_bias, rows8 + (w_buf - ar(w_buf + KV_CHUNK)[None, :]))[:, 0]

    def rows_last(x):
        return jnp.moveaxis(x, 2, -1).reshape(x.shape[0] * x.shape[1], KV_W, x.shape[2])

    def rows_first(xt, lead):
        return jnp.moveaxis(xt.reshape((lead,) + row + (xt.shape[-1],)), -1, 1)

    cmp_cache = rows_last(cache_cmp_kv)
    sel_cache = rows_last(cache_sel_kv)
    win_cache = rows_last(cache_win_kv)
    state_t = jnp.moveaxis(state_ret, 1, -1).reshape(depth * H_RET, HEAD_DIM, HEAD_DIM, b_s)

    h_p = x_prompt.reshape(m_p, d)
    h_s = x_sample.reshape(b_s, d)
    rp, rs, wp, ws, cp, cs, sp, ss = [], [], [], [], [], [], [], []
    y_p = y_s = None
    for i in range(depth):
        w_in_b = jnp.pad(w_in[i], ((0, 0), (0, N_IN_PAD - N_IN))).astype(BF16)
        w_out_b = w_out[i].astype(BF16)
        pe_t, cw1, cw2 = _compress_weights(cmp_pe[i], cmp_w1[i], cmp_w2[i])

        w_kvt = w_in[i][:, COL_KVC:COL_KVC + 3 * KV_W].T.astype(BF16)
        z, kvt = rms_matmul_kvt(h_p, norm_mix[i], w_in_b, w_kvt, b_p)
        z3 = z.reshape(b_p, t_p, N_IN_PAD)
        ret_y, ret_state = retention_prompt(z3, ret_gain[i])
        kc, vc = compress(kvt, ar(b_p)[:, None], pe_t, cw1, cw2)
        nsa_y = nsa_prompt(z3, kvt, kc, jnp.swapaxes(vc, 1, 2), bias_ct, bias_tt)
        h_p = out_proj(h_p, ret_y.reshape(m_p, RET_W), nsa_y.reshape(m_p, NSA_QW), w_out_b)
        rp.append(ret_state)
        cp.append(rows_first(kvt[:, 0:KV_W], b_p))
        sp.append(rows_first(kvt[:, KV_W:2 * KV_W], b_p))
        wp.append(rows_first(kvt[:, 2 * KV_W:, t_p - w_buf:], b_p))

        zs = rms_matmul(h_s, norm_mix[i], w_in_b)
        zst = zs.T
        ret_yst, state_s = retention_step(zst, state_t, i, past, ret_gain[i])
        page_idx = page_table + i * n_pool
        kc_s, vc_s = compress(cmp_cache, page_idx, pe_t, cw1, cw2)
        nsa_ys, win_s = nsa_step(zs.reshape(b_s, 1, N_IN_PAD), kc_s, vc_s, sel_cache, win_cache, page_idx, i * b_s,
                                 bias_cs, bias_ss, bias_ws, past)
        h_s = out_proj(h_s, ret_yst.T, nsa_ys.reshape(b_s, NSA_QW), w_out_b)
        rs.append(jnp.moveaxis(state_s, -1, 0))
        ws.append(rows_first(win_s, b_s))
        cs.append(rows_first(zst[None, COL_KVC:COL_KVC + KV_W], 1).reshape((b_s, 1) + row))
        ss.append(rows_first(zst[None, COL_KVS:COL_KVS + KV_W], 1).reshape((b_s, 1) + row))

        j = i // 2
        if i % 2 == 0:
            w1, w3, w2 = ffn_w1[j].astype(BF16), ffn_w3[j].astype(BF16), ffn_w2[j].astype(BF16)
            h_p = _dense_ffn(h_p, norm_ffn[i], w1, w3, w2)
            h_s = _dense_ffn(h_s, norm_ffn[i], w1, w3, w2)
        else:
            h_all = jnp.concatenate([h_p, h_s], axis=0)
            m_all = h_all.shape[0]
            m_pad = -(-m_all // 256) * 256
            router_w = jnp.pad(router[j], ((0, 0), (0, LANES - N_EXPERTS)))
            logits = router_logits(jnp.pad(h_all, ((0, m_pad - m_all), (0, 0))), norm_ffn[i], router_w)
            tm = 512
            pos, row_token, row_gate, tile_expert, tile_valid = _route(logits[:m_all, :N_EXPERTS], tm)
            y_rows = grouped_ffn(h_all[row_token], norm_ffn[i], row_gate, tile_expert, tile_valid,
                                 moe_w1[j].astype(BF16), moe_w3[j].astype(BF16), moe_w2[j].astype(BF16),
                                 tm=tm, tf=moe_w1.shape[3] // 4, residual=False)
            h_all = h_all + y_rows[pos[:, 0]] + y_rows[pos[:, 1]]
            h_p, h_s = h_all[:m_p], h_all[m_p:]

        wg, wpj = ple_gate[i].astype(BF16), ple_proj[i].astype(BF16)
        h_p, y_p = ple(h_p, ple_norm[i], wg, p_prompt[i].reshape(m_p, P_DIM), wpj, norm_final)
        h_s, y_s = ple(h_s, ple_norm[i], wg, p_sample[i].reshape(b_s, P_DIM), wpj, norm_final)

    return (y_p.reshape(b_p, t_p, d), y_s.reshape(b_s, 1, d), jnp.stack(rp), jnp.stack(rs), jnp.stack(wp),
            jnp.stack(ws), jnp.stack(cp), jnp.stack(cs), jnp.stack(sp), jnp.stack(ss))
```

```python
import functools
import math

import numpy as np
import jax
import jax.numpy as jnp
from jax import lax
from jax.experimental import pallas as pl
from jax.experimental.pallas import tpu as pltpu

F32 = jnp.float32
BF16 = jnp.bfloat16
I32 = jnp.int32

LANES = 128
SUBLANES = 8
VMEM_LIMIT_BYTES = 56 * 1024 * 1024

D_MODEL = 1024
HEAD_DIM = 64
H_RET = 8
H_NSA = 8
N_KV = 2
R_GRP = H_NSA // N_KV
RET_W = H_RET * HEAD_DIM
NSA_QW = H_NSA * HEAD_DIM
KV_W = 2 * N_KV * HEAD_DIM
N_IN = 4 * RET_W + NSA_QW + 3 * KV_W + 3 * H_NSA
N_IN_PAD = 3584
COL_RQ, COL_RK, COL_RV, COL_RG = 0, RET_W, 2 * RET_W, 3 * RET_W
COL_NQ = 4 * RET_W
COL_KVC = COL_NQ + NSA_QW
COL_KVS = COL_KVC + KV_W
COL_KVW = COL_KVS + KV_W
COL_NG = COL_KVW + KV_W
RET_CHUNK = 128
CMP_STRIDE = 16
L_CMP = 2 * CMP_STRIDE
CMP_HIDDEN = 2 * HEAD_DIM
SEL_BLOCK = 64
N_SELECT = 16
WINDOW = 512
Q_BLOCK = 128
N_BUCKETS = 32
MAX_DISTANCE = 128
N_EXPERTS = 8
TOP_K = 2
P_DIM = 256
PAGE_SIZE = 128
ROPE_BASE = 10000.0
EPS = 1e-6
FORCED_SCORE = 1e4
NEG = -1e30
TINY = float(np.finfo(np.float32).tiny)
SEL_LANES = 64
FAR_REL = 2
KV_CHUNK = 128


def _cparams(n_grid):
    return pltpu.CompilerParams(dimension_semantics=("arbitrary",) * n_grid,
                                vmem_limit_bytes=VMEM_LIMIT_BYTES)


def _dot(a, b):
    return jnp.dot(a, b, preferred_element_type=F32)


def _dot_nt(a, b):
    return lax.dot_general(a, b, (((1,), (1,)), ((), ())), preferred_element_type=F32)


def _split3(x):
    hi = x.astype(BF16)
    r = x - hi.astype(F32)
    mid = r.astype(BF16)
    lo = (r - mid.astype(F32)).astype(BF16)
    return hi, mid, lo


def _dot3(x, b01):
    hi, mid, lo = _split3(x)
    return _dot(hi, b01) + _dot(mid, b01) + _dot(lo, b01)


def _dot6(x, w):
    xh, xm, xl = _split3(x)
    wh, wm, wl = _split3(w)
    return (_dot(xh, wh) + _dot(xh, wm) + _dot(xm, wh)
            + _dot(xm, wm) + _dot(xh, wl) + _dot(xl, wh))


def _sigmoid(x):
    return 1.0 / (1.0 + jnp.exp(-x))


def _rms(x, g):
    return x * lax.rsqrt(jnp.mean(x * x, axis=-1, keepdims=True) + EPS) * g


def _row_tile(m):
    for t in (512, 256, 128, 64, 32, 16, 8):
        if m % t == 0:
            return t
    raise ValueError(f"row count {m} not a multiple of 8")


def _rms_matmul_kernel(x_ref, g_ref, w_ref, o_ref):
    o_ref[...] = _dot(_rms(x_ref[...], g_ref[...]).astype(BF16), w_ref[...])


def rms_matmul(x, g, w):
    m, d = x.shape
    n = w.shape[1]
    tm = min(_row_tile(m), 256)
    return pl.pallas_call(
        _rms_matmul_kernel,
        grid=(m // tm,),
        in_specs=[pl.BlockSpec((tm, d), lambda i: (i, 0)),
                  pl.BlockSpec((1, d), lambda i: (0, 0)),
                  pl.BlockSpec((d, n), lambda i: (0, 0))],
        out_specs=pl.BlockSpec((tm, n), lambda i: (i, 0)),
        out_shape=jax.ShapeDtypeStruct((m, n), F32),
        compiler_params=_cparams(1),
    )(x, g.reshape(1, d), w)


def _rms_matmul_kvt_kernel(x_ref, g_ref, w_ref, wt_ref, o_ref, ot_ref):
    xn = _rms(x_ref[...], g_ref[...]).astype(BF16)
    o_ref[...] = _dot(xn, w_ref[...])
    ot_ref[...] = _dot_nt(wt_ref[...], xn)


def rms_matmul_kvt(x, g, w, wt, bsz):
    m, d = x.shape
    n = w.shape[1]
    nt = wt.shape[0]
    t = m // bsz
    tm = min(_row_tile(t), 256)
    per = t // tm
    return pl.pallas_call(
        _rms_matmul_kvt_kernel,
        grid=(m // tm,),
        in_specs=[pl.BlockSpec((tm, d), lambda i: (i, 0)),
                  pl.BlockSpec((1, d), lambda i: (0, 0)),
                  pl.BlockSpec((d, n), lambda i: (0, 0)),
                  pl.BlockSpec((nt, d), lambda i: (0, 0))],
        out_specs=[pl.BlockSpec((tm, n), lambda i: (i, 0)),
                   pl.BlockSpec((None, nt, tm), lambda i: (i // per, 0, i % per))],
        out_shape=[jax.ShapeDtypeStruct((m, n), F32), jax.ShapeDtypeStruct((bsz, nt, t), F32)],
        compiler_params=_cparams(1),
    )(x, g.reshape(1, d), w, wt)


def _router_kernel(x_ref, g_ref, w_ref, o_ref):
    o_ref[...] = _dot6(_rms(x_ref[...], g_ref[...]), w_ref[...])


def router_logits(x, g, w):
    m, d = x.shape
    n = w.shape[1]
    tm = min(_row_tile(m), 256)
    return pl.pallas_call(
        _router_kernel,
        grid=(m // tm,),
        in_specs=[pl.BlockSpec((tm, d), lambda i: (i, 0)),
                  pl.BlockSpec((1, d), lambda i: (0, 0)),
                  pl.BlockSpec((d, n), lambda i: (0, 0))],
        out_specs=pl.BlockSpec((tm, n), lambda i: (i, 0)),
        out_shape=jax.ShapeDtypeStruct((m, n), F32),
        compiler_params=_cparams(1),
    )(x, g.reshape(1, d), w)


def _outproj_kernel(h_ref, a_ref, b_ref, w_ref, o_ref):
    ka = a_ref.shape[1]
    o_ref[...] = (h_ref[...] + _dot(a_ref[...].astype(BF16), w_ref[0:ka, :])
                  + _dot(b_ref[...].astype(BF16), w_ref[ka:, :]))


def out_proj(h, a, b, w):
    m, d = h.shape
    ka, kb = a.shape[1], b.shape[1]
    tm = _row_tile(m)
    return pl.pallas_call(
        _outproj_kernel,
        grid=(m // tm,),
        in_specs=[pl.BlockSpec((tm, d), lambda i: (i, 0)),
                  pl.BlockSpec((tm, ka), lambda i: (i, 0)),
                  pl.BlockSpec((tm, kb), lambda i: (i, 0)),
                  pl.BlockSpec((ka + kb, d), lambda i: (0, 0))],
        out_specs=pl.BlockSpec((tm, d), lambda i: (i, 0)),
        out_shape=jax.ShapeDtypeStruct((m, d), F32),
        compiler_params=_cparams(1),
    )(h, a, b, w)


def _ffn_kernel(te_ref, tv_ref, x_ref, g_ref, gate_ref, w1_ref, w3_ref, w2_ref, o_ref,
                xn_ref, acc_ref, *, residual):
    i = pl.program_id(0)
    j = pl.program_id(1)

    @pl.when(j == 0)
    def _():
        xn_ref[...] = _rms(x_ref[...], g_ref[...]).astype(BF16)
        acc_ref[...] = jnp.zeros_like(acc_ref)

    @pl.when(tv_ref[i] != 0)
    def _():
        xn = xn_ref[...]
        a = _dot(xn, w1_ref[...])
        b = _dot(xn, w3_ref[...])
        hid = (a * _sigmoid(a)) * b
        acc_ref[...] += _dot(hid.astype(BF16), w2_ref[...])

    @pl.when(j == pl.num_programs(1) - 1)
    def _():
        y = acc_ref[...] * gate_ref[...]
        if residual:
            y = y + x_ref[...]
        o_ref[...] = y


def grouped_ffn(x, g, row_gate, tile_expert, tile_valid, w1, w3, w2, *, tm, tf, residual):
    m, d = x.shape
    f = w1.shape[2]
    grid_spec = pltpu.PrefetchScalarGridSpec(
        num_scalar_prefetch=2,
        grid=(m // tm, f // tf),
        in_specs=[pl.BlockSpec((tm, d), lambda i, j, te, tv: (i, 0)),
                  pl.BlockSpec((1, d), lambda i, j, te, tv: (0, 0)),
                  pl.BlockSpec((tm, 1), lambda i, j, te, tv: (i, 0)),
                  pl.BlockSpec((None, d, tf), lambda i, j, te, tv: (te[i], 0, j)),
                  pl.BlockSpec((None, d, tf), lambda i, j, te, tv: (te[i], 0, j)),
                  pl.BlockSpec((None, tf, d), lambda i, j, te, tv: (te[i], j, 0))],
        out_specs=pl.BlockSpec((tm, d), lambda i, j, te, tv: (i, 0)),
        scratch_shapes=[pltpu.VMEM((tm, d), BF16), pltpu.VMEM((tm, d), F32)],
    )
    return pl.pallas_call(
        functools.partial(_ffn_kernel, residual=residual),
        grid_spec=grid_spec,
        out_shape=jax.ShapeDtypeStruct((m, d), F32),
        compiler_params=_cparams(2),
    )(tile_expert, tile_valid, x, g.reshape(1, d), row_gate, w1, w3, w2)


def _ple_kernel(h_ref, g_ref, wg_ref, p_ref, wp_ref, gf_ref, o_ref, y_ref):
    h = h_ref[...]
    gate = _sigmoid(_dot(_rms(h, g_ref[...]).astype(BF16), wg_ref[...]))
    proj = _dot(p_ref[...].astype(BF16), wp_ref[...])
    h2 = h + gate * proj
    o_ref[...] = h2
    y_ref[...] = _rms(h2, gf_ref[...])


def ple(h, g, wg, p, wp, gf):
    m, d = h.shape
    pd = p.shape[1]
    tm = _row_tile(m)
    row = lambda i: (i, 0)
    fixed = lambda i: (0, 0)
    return pl.pallas_call(
        _ple_kernel,
        grid=(m // tm,),
        in_specs=[pl.BlockSpec((tm, d), row), pl.BlockSpec((1, d), fixed),
                  pl.BlockSpec((d, d), fixed), pl.BlockSpec((tm, pd), row),
                  pl.BlockSpec((pd, d), fixed), pl.BlockSpec((1, d), fixed)],
        out_specs=[pl.BlockSpec((tm, d), row), pl.BlockSpec((tm, d), row)],
        out_shape=[jax.ShapeDtypeStruct((m, d), F32), jax.ShapeDtypeStruct((m, d), F32)],
        compiler_params=_cparams(1),
    )(h, g.reshape(1, d), wg, p, wp, gf.reshape(1, d))


def _bias_kernel(tab_ref, d_ref, o_ref):
    n = jnp.maximum(d_ref[...], 0)
    max_exact = N_BUCKETS // 2
    nf = jnp.maximum(n, 1).astype(F32)
    large = max_exact + (jnp.log(nf / max_exact) / math.log(MAX_DISTANCE / max_exact)
                         * (N_BUCKETS - max_exact)).astype(I32)
    large = jnp.minimum(large, N_BUCKETS - 1)
    bucket = jnp.where(n < max_exact, n, large)
    for h in range(H_NSA):
        acc = jnp.zeros(n.shape, F32)
        for k in range(N_BUCKETS):
            acc = jnp.where(bucket == k, tab_ref[k, h], acc)
        o_ref[h] = acc


def bias_lookup(table, dist):
    r, c = dist.shape
    tr = min(r, 128)
    tc = 512 if c % 512 == 0 else c
    return pl.pallas_call(
        _bias_kernel,
        grid=(r // tr, c // tc),
        in_specs=[pl.BlockSpec(memory_space=pltpu.SMEM),
                  pl.BlockSpec((tr, tc), lambda i, j: (i, j))],
        out_specs=pl.BlockSpec((H_NSA, tr, tc), lambda i, j: (0, i, j)),
        out_shape=jax.ShapeDtypeStruct((H_NSA, r, c), F32),
        compiler_params=_cparams(2),
        name="bias_lookup",
    )(table, dist)


def _swap_halves(x):
    lane = lax.broadcasted_iota(I32, x.shape, 1)
    return jnp.where(lane % HEAD_DIM < HEAD_DIM // 2,
                     pltpu.roll(x, LANES - HEAD_DIM // 2, 1), pltpu.roll(x, HEAD_DIM // 2, 1))


def _retention_prompt_kernel(q_ref, k_ref, v_ref, g_ref, cos_ref, sin_ref, gain_ref, intra_ref,
                             qdec_ref, kdec_ref, cdec_ref, y_ref, s_out_ref, s_ref):
    c = pl.program_id(1)

    @pl.when(c == 0)
    def _():
        s_ref[...] = jnp.zeros_like(s_ref)

    cos = cos_ref[...]
    sin = sin_ref[...]
    lane = lax.broadcasted_iota(I32, (RET_CHUNK, LANES), 1)
    left = lane < HEAD_DIM
    row = lax.broadcasted_iota(I32, (LANES, LANES), 0)
    col = lax.broadcasted_iota(I32, (LANES, LANES), 1)
    same_head = (row < HEAD_DIM) == (col < HEAD_DIM)
    ones_bd = jnp.where(same_head, 1.0, 0.0).astype(BF16)
    for p in range(H_RET // 2):
        sl = slice(p * LANES, (p + 1) * LANES)
        q = q_ref[:, sl]
        k = k_ref[:, sl]
        v = v_ref[:, sl].astype(BF16)
        qr = q * cos + _swap_halves(q) * sin
        kr = (k * cos + _swap_halves(k) * sin) * (HEAD_DIM ** -0.5)
        qb = qr.astype(BF16)
        q2 = jnp.concatenate([jnp.where(left, qr, 0.0), jnp.where(left, 0.0, qr)], axis=0).astype(BF16)
        inner = _dot_nt(q2, kr.astype(BF16)) * intra_ref[p]
        o2 = _dot(inner.astype(BF16), v)
        s_old = s_ref[p]
        cross = _dot(qb, s_old.astype(BF16)) * qdec_ref[:, sl]
        o = jnp.where(left, o2[:RET_CHUNK], o2[RET_CHUNK:]) + cross
        kd = (kr * kdec_ref[:, sl]).T.astype(BF16)
        s_ref[p] = s_old * cdec_ref[p] + jnp.where(same_head, _dot(kd, v), 0.0)
        ms = _dot3(o * o, ones_bd) * (1.0 / HEAD_DIM)
        gate = g_ref[:, sl]
        y = o * lax.rsqrt(ms + EPS) * gain_ref[:, sl] * (gate * _sigmoid(gate))
        y_ref[:, sl] = y.astype(y_ref.dtype)

    @pl.when(c == pl.num_programs(1) - 1)
    def _():
        for p in range(H_RET // 2):
            s = s_ref[p]
            s_out_ref[2 * p] = s[:HEAD_DIM, :HEAD_DIM]
            s_out_ref[2 * p + 1] = s[HEAD_DIM:, HEAD_DIM:]


def _retention_consts(c):
    log_g = jnp.log1p(-jnp.exp2(-5.0 - jnp.arange(H_RET, dtype=F32)))
    j = jnp.arange(c, dtype=F32)
    diff = j[:, None] - j[None, :]
    intra = jnp.where(diff >= 0, jnp.exp(jnp.maximum(diff, 0.0)[None] * log_g[:, None, None]), 0.0)
    q_decay = jnp.exp((j[:, None] + 1.0) * log_g[None, :])
    k_decay = jnp.exp((c - 1.0 - j[:, None]) * log_g[None, :])
    c_decay = jnp.exp(c * log_g)
    intra2 = intra.reshape(H_RET // 2, 2 * c, c)
    qdec = jnp.repeat(q_decay, HEAD_DIM, axis=1)
    kdec = jnp.repeat(k_decay, HEAD_DIM, axis=1)
    cdec = jnp.repeat(c_decay.reshape(H_RET // 2, 2), HEAD_DIM, axis=1)[:, :, None] * jnp.ones((1, 1, LANES), F32)
    return intra2, qdec, kdec, cdec


def _rotary_tables(pos):
    half = HEAD_DIM // 2
    inv = ROPE_BASE ** (-jnp.arange(half, dtype=F32) / half)
    ang = pos.astype(F32)[:, None] * inv[None, :]
    cos, sin = jnp.cos(ang), jnp.sin(ang)
    cos_t = jnp.concatenate([cos, cos, cos, cos], axis=1)
    sin_t = jnp.concatenate([-sin, sin, -sin, sin], axis=1)
    return cos_t, sin_t


def retention_prompt(z3, gain):
    b, t, _ = z3.shape
    c = RET_CHUNK
    cos_t, sin_t = _rotary_tables(jnp.arange(t))
    intra2, qdec, kdec, cdec = _retention_consts(c)
    blk = lambda col: pl.BlockSpec((None, c, RET_W), lambda bi, ci, col=col: (bi, ci, col))
    fixed2 = lambda shape: pl.BlockSpec(shape, lambda bi, ci: (0,) * len(shape))
    return pl.pallas_call(
        _retention_prompt_kernel,
        grid=(b, t // c),
        in_specs=[blk(COL_RQ // RET_W), blk(COL_RK // RET_W), blk(COL_RV // RET_W), blk(COL_RG // RET_W),
                  pl.BlockSpec((c, LANES), lambda bi, ci: (ci, 0)),
                  pl.BlockSpec((c, LANES), lambda bi, ci: (ci, 0)),
                  fixed2((1, RET_W)), fixed2((H_RET // 2, 2 * c, c)), fixed2((c, RET_W)),
                  fixed2((c, RET_W)), fixed2((H_RET // 2, LANES, LANES))],
        out_specs=[pl.BlockSpec((None, c, RET_W), lambda bi, ci: (bi, ci, 0)),
                   pl.BlockSpec((None, H_RET, HEAD_DIM, HEAD_DIM), lambda bi, ci: (bi, 0, 0, 0))],
        out_shape=[jax.ShapeDtypeStruct((b, t, RET_W), BF16),
                   jax.ShapeDtypeStruct((b, H_RET, HEAD_DIM, HEAD_DIM), F32)],
        scratch_shapes=[pltpu.VMEM((H_RET // 2, LANES, LANES), F32)],
        compiler_params=_cparams(2),
    )(z3, z3, z3, z3, cos_t, sin_t, gain.reshape(1, RET_W), intra2, qdec, kdec, cdec)


def _retention_step_kernel(q_ref, k_ref, v_ref, g_ref, s_ref, cos_ref, sin_ref, gamma_ref, gain_ref,
                           y_ref, s_out_ref):
    half = HEAD_DIM // 2
    cos = cos_ref[...]
    sin = sin_ref[...]

    def rot(x):
        x1, x2 = x[:half], x[half:]
        return jnp.concatenate([x1 * cos - x2 * sin, x1 * sin + x2 * cos], axis=0)

    qr = rot(q_ref[...])
    kr = rot(k_ref[...]) * (HEAD_DIM ** -0.5)
    v = v_ref[...]
    gamma = gamma_ref[...]
    o = jnp.zeros_like(v)
    for d in range(HEAD_DIM):
        s_new = s_ref[d] * gamma + kr[d:d + 1] * v
        s_out_ref[d] = s_new
        o = o + qr[d:d + 1] * s_new
    ms = jnp.mean(o * o, axis=0, keepdims=True)
    gate = g_ref[...]
    y_ref[...] = o * lax.rsqrt(ms + EPS) * gain_ref[...] * (gate * _sigmoid(gate))


def retention_step(zt, state, layer, pos, gain):
    bs = zt.shape[1]
    half = HEAD_DIM // 2
    inv = ROPE_BASE ** (-jnp.arange(half, dtype=F32) / half)
    ang = jnp.full((1,), pos).astype(F32)[:, None] * inv[None, :]
    cos = jnp.broadcast_to(jnp.cos(ang).reshape(half, 1), (half, bs))
    sin = jnp.broadcast_to(jnp.sin(ang).reshape(half, 1), (half, bs))
    gamma = 1.0 - jnp.exp2(-5.0 - jnp.arange(H_RET, dtype=F32))
    gamma = jnp.broadcast_to(gamma.reshape(H_RET, 1, 1), (H_RET, 1, bs))
    gain_b = jnp.broadcast_to(gain.reshape(H_RET, HEAD_DIM, 1), (H_RET, HEAD_DIM, bs))
    head = lambda col: pl.BlockSpec((HEAD_DIM, bs), lambda h, col=col: (col // HEAD_DIM + h, 0))
    fixed = pl.BlockSpec((half, bs), lambda h: (0, 0))
    return pl.pallas_call(
        _retention_step_kernel,
        grid=(H_RET,),
        in_specs=[head(COL_RQ), head(COL_RK), head(COL_RV), head(COL_RG),
                  pl.BlockSpec((None, HEAD_DIM, HEAD_DIM, bs), lambda h: (layer * H_RET + h, 0, 0, 0)),
                  fixed, fixed,
                  pl.BlockSpec((None, 1, bs), lambda h: (h, 0, 0)),
                  pl.BlockSpec((None, HEAD_DIM, bs), lambda h: (h, 0, 0))],
        out_specs=[pl.BlockSpec((HEAD_DIM, bs), lambda h: (h, 0)),
                   pl.BlockSpec((None, HEAD_DIM, HEAD_DIM, bs), lambda h: (h, 0, 0, 0))],
        out_shape=[jax.ShapeDtypeStruct((RET_W, bs), F32),
                   jax.ShapeDtypeStruct((H_RET, HEAD_DIM, HEAD_DIM, bs), F32)],
        compiler_params=_cparams(1),
    )(zt, zt, zt, zt, state, cos, sin, gamma, gain_b)


def _gelu_tanh(x):
    return 0.5 * x * (1.0 + jnp.tanh(math.sqrt(2.0 / math.pi) * (x + 0.044715 * (x * x * x))))


def _compress_kernel(*refs, n_pieces):
    x_refs = refs[1:1 + n_pieces]
    pe_ref, w1_ref, w2_ref, kc_ref, vc_ref, rows_ref = refs[1 + n_pieces:]
    width = x_refs[0].shape[1]
    n = n_pieces * width // CMP_STRIDE
    for kv, o_ref in enumerate((kc_ref, vc_ref)):
        for p, x_ref in enumerate(x_refs):
            for c in range(width // LANES):
                rows_ref[pl.ds(p * width + c * LANES, LANES), :] = (
                    x_ref[kv * LANES:(kv + 1) * LANES, c * LANES:(c + 1) * LANES].T)
        xk = jnp.concatenate([rows_ref[pl.ds(j, n, stride=CMP_STRIDE), :] for j in range(CMP_STRIDE)], axis=1)
        top = _dot((xk + pe_ref[kv, 0]).astype(BF16), w1_ref[kv, 0])
        bot = _dot((xk + pe_ref[kv, 1]).astype(BF16), w1_ref[kv, 1])
        pre = top + pltpu.roll(bot, n - 1, 0)
        hid = _gelu_tanh(pre).astype(BF16)
        o_ref[...] = jnp.concatenate(
            [_dot(hid[:, g * CMP_HIDDEN:(g + 1) * CMP_HIDDEN], w2_ref[kv]) for g in range(N_KV)], axis=1)


def _compress_weights(pe, w1, w2):
    pe_r = pe.reshape(2, 2, CMP_STRIDE, 1, HEAD_DIM)
    pe_t = jnp.broadcast_to(pe_r, (2, 2, CMP_STRIDE, N_KV, HEAD_DIM)).reshape(2, 2, 1, CMP_STRIDE * LANES)
    w1r = w1.reshape(2, 2, CMP_STRIDE, HEAD_DIM, CMP_HIDDEN)
    eye = jnp.eye(N_KV, dtype=w1.dtype)
    wbig = jnp.einsum('khjdc,gG->khjgdGc', w1r, eye)
    wbig = wbig.reshape(2, 2, CMP_STRIDE * LANES, N_KV * CMP_HIDDEN)
    return pe_t.astype(F32), wbig.astype(BF16), w2.astype(BF16)


def compress(src, idx, pe_t, w1b, w2b):
    bsz, n_pieces = idx.shape
    width = src.shape[2]
    n = n_pieces * width // CMP_STRIDE
    piece = lambda p: pl.BlockSpec((None, KV_W, width), lambda b, ix, p=p: (ix[b, p], 0, 0))
    fixed = lambda shape: pl.BlockSpec(shape, lambda b, ix: (0,) * len(shape))
    out = pl.BlockSpec((None, n, LANES), lambda b, ix: (b, 0, 0))
    grid_spec = pltpu.PrefetchScalarGridSpec(
        num_scalar_prefetch=1,
        grid=(bsz,),
        in_specs=[piece(p) for p in range(n_pieces)]
        + [fixed(pe_t.shape), fixed(w1b.shape), fixed(w2b.shape)],
        out_specs=[out, out],
        scratch_shapes=[pltpu.VMEM((n_pieces * width, LANES), F32)],
    )
    return pl.pallas_call(
        functools.partial(_compress_kernel, n_pieces=n_pieces),
        grid_spec=grid_spec,
        out_shape=[jax.ShapeDtypeStruct((bsz, n, LANES), F32)] * 2,
        compiler_params=_cparams(1),
    )(idx, *([src] * n_pieces), pe_t, w1b, w2b)


def _masked_softmax(s, mask):
    s = jnp.where(mask, s, NEG)
    m = jnp.max(s, axis=-1, keepdims=True)
    e = jnp.where(mask, jnp.exp(s - m), 0.0)
    return e / jnp.maximum(jnp.sum(e, axis=-1, keepdims=True), TINY)


def _block_scores(imp, qpos, n_sel, axis=1):
    blk = lax.broadcasted_iota(I32, imp.shape, axis)
    cur = qpos // SEL_BLOCK
    forced = (blk == 0) | (blk == cur) | (blk == cur - 1)
    valid = blk * SEL_BLOCK <= qpos
    score = jnp.where(forced, FORCED_SCORE, jnp.where(valid, imp, -1.0))
    return jnp.where(blk < n_sel, score, -2.0)


def _select_top(score, n_top, axis=1):
    blk = lax.broadcasted_iota(I32, score.shape, axis)
    rank = jnp.zeros(score.shape, F32)
    for i in range(score.shape[axis]):
        one = score[:, i:i + 1] if axis == 1 else score[i:i + 1, :]
        beats = (one > score) | ((one == score) & (blk > i))
        rank = rank + jnp.where(beats, 1.0, 0.0)
    return jnp.where(rank < n_top, 1.0, 0.0)


def _attend_t(s, ok, m_prev, l_prev):
    s = jnp.where(ok, s, NEG)
    m_new = jnp.maximum(m_prev, jnp.max(s, axis=0, keepdims=True))
    alpha = jnp.exp(m_prev - m_new)
    p = jnp.exp(s - m_new)
    l_new = alpha * l_prev + jnp.sum(p, axis=0, keepdims=True)
    return m_new, l_new, alpha, p


def _nsa_prompt_kernel(q_ref, gate_ref, kc_ref, vct_ref, ks_ref, vst_ref, kw_ref, vwt_ref, bc_ref, bt_ref,
                       o_ref, qt_ref, pcs_ref, pick_ref, acc_ref, os_ref, s_ref, p_ref, *, n_cmp, n_sel):
    i = pl.program_id(1)
    qs = i * Q_BLOCK
    key_i = lax.broadcasted_iota(I32, (KV_CHUNK, Q_BLOCK), 0)
    tok_i = lax.broadcasted_iota(I32, (KV_CHUNK, Q_BLOCK), 1)
    row8 = lax.broadcasted_iota(I32, (H_NSA, Q_BLOCK), 0)
    zeros_half = jnp.zeros((HEAD_DIM, Q_BLOCK), F32)

    for k2 in range(H_NSA // 2):
        slab_t = q_ref[:, k2 * LANES:(k2 + 1) * LANES].T * (HEAD_DIM ** -0.5)
        for h in (2 * k2, 2 * k2 + 1):
            qh = slab_t[(h % 2) * HEAD_DIM:(h % 2 + 1) * HEAD_DIM]
            both = [qh, zeros_half] if h // R_GRP == 0 else [zeros_half, qh]
            qt_ref[:, h * Q_BLOCK:(h + 1) * Q_BLOCK] = jnp.concatenate(both, axis=0).astype(BF16)

    sig_t = _sigmoid(gate_ref[...]).T

    ncp = kc_ref.shape[0]
    kc = kc_ref[...].astype(BF16)
    n_c = lax.broadcasted_iota(I32, (ncp, Q_BLOCK), 0)
    t_c = lax.broadcasted_iota(I32, (ncp, Q_BLOCK), 1)
    ok_c = (n_c * CMP_STRIDE + (L_CMP - 1) <= qs + t_c) & (n_c < n_cmp)
    blk_pos = qs + lax.broadcasted_iota(I32, (pick_ref.shape[1], Q_BLOCK), 1)
    o_c = []
    for g in range(N_KV):
        vct = vct_ref[g * HEAD_DIM:(g + 1) * HEAD_DIM, :].astype(BF16)
        pcs = jnp.zeros((ncp, Q_BLOCK), F32)
        for r in range(R_GRP):
            h = g * R_GRP + r
            s = _dot(kc, qt_ref[:, h * Q_BLOCK:(h + 1) * Q_BLOCK]) + bc_ref[h]
            s = jnp.where(ok_c, s, NEG)
            e = jnp.where(ok_c, jnp.exp(s - jnp.max(s, axis=0, keepdims=True)), 0.0)
            p = e * (1.0 / jnp.maximum(jnp.sum(e, axis=0, keepdims=True), TINY))
            o_c.append(_dot(vct, p.astype(BF16)))
            pcs = pcs + p
        pcs_ref[...] = pcs
        per = SEL_BLOCK // CMP_STRIDE
        imp = pcs_ref[pl.ds(0, ncp // per, stride=per), :]
        for c in range(1, per):
            imp = imp + pcs_ref[pl.ds(c, ncp // per, stride=per), :]
        pick_ref[g] = _select_top(_block_scores(imp, blk_pos, n_sel, axis=0), min(N_SELECT, n_sel), axis=0)

    dmat = tok_i - key_i

    def stream(k_ref, vt_ref, n_chunks, chunk_of, mask_of):
        def scores(c, slot):
            off, _ = chunk_of(c)
            k = k_ref[pl.ds(off, KV_CHUNK), :].astype(BF16)
            for h in range(H_NSA):
                s_ref[slot, h] = _dot(k, qt_ref[:, h * Q_BLOCK:(h + 1) * Q_BLOCK])

        def values(c, slot, alpha_all):
            off, _ = chunk_of(c)
            for g in range(N_KV):
                vt = vt_ref[g * HEAD_DIM:(g + 1) * HEAD_DIM, pl.ds(off, KV_CHUNK)].astype(BF16)
                for h in range(g * R_GRP, (g + 1) * R_GRP):
                    acc_ref[h] = acc_ref[h] * alpha_all[h:h + 1] + _dot(vt, p_ref[slot, h])

        def softmax(c, slot, m_all, l_all):
            _, rel = chunk_of(c)
            ok_g = mask_of(c, rel)
            bias_idx = min(max(rel, 0), FAR_REL) if isinstance(rel, int) else jnp.clip(rel, 0, FAR_REL)
            alpha_all = jnp.zeros((H_NSA, Q_BLOCK), F32)
            for h in range(H_NSA):
                s = s_ref[slot, h] + bt_ref[bias_idx, h]
                m_new, l_new, alpha, p = _attend_t(s, ok_g[h // R_GRP], m_all[h:h + 1], l_all[h:h + 1])
                p_ref[slot, h] = p.astype(BF16)
                m_all = jnp.where(row8 == h, m_new, m_all)
                l_all = jnp.where(row8 == h, l_new, l_all)
                alpha_all = jnp.where(row8 == h, alpha, alpha_all)
            return m_all, l_all, alpha_all

        def stage(c, slot, carry, prefetch=True):
            m_all, l_all, alpha_prev = carry
            if prefetch:
                scores(c + 1, 1 - slot)
            values(c - 1, 1 - slot, alpha_prev)
            return softmax(c, slot, m_all, l_all)

        acc_ref[...] = jnp.zeros_like(acc_ref)
        p_ref[1] = jnp.zeros(p_ref.shape[1:], BF16)
        scores(0, 0)
        carry = (jnp.full((H_NSA, Q_BLOCK), NEG, F32), jnp.zeros((H_NSA, Q_BLOCK), F32),
                 jnp.ones((H_NSA, Q_BLOCK), F32))
        if isinstance(n_chunks, int):
            for c in range(n_chunks):
                carry = stage(c, c % 2, carry, prefetch=c + 1 < n_chunks)
            last = n_chunks - 1
        else:
            pairs = (n_chunks + 1) // 2
            carry = lax.fori_loop(0, pairs, lambda t, cr: stage(2 * t + 1, 1, stage(2 * t, 0, cr)), carry)
            last = 2 * pairs - 1
        m_all, l_all, alpha_last = carry
        values(last, last % 2 if isinstance(last, int) else 1, alpha_last)
        return m_all, l_all

    def sel_chunk(c):
        return pl.multiple_of(jnp.clip(c, 0, i) * KV_CHUNK, KV_CHUNK), i - c

    def sel_mask(c, rel):
        causal = rel * KV_CHUNK + dmat >= 0
        per = KV_CHUNK // SEL_BLOCK
        first = jnp.minimum(c, i) * per
        ok_g = []
        for g in range(N_KV):
            picked = jnp.concatenate(
                [jnp.broadcast_to(pick_ref[g, pl.ds(first + b, 1), :], (SEL_BLOCK, Q_BLOCK)) for b in range(per)],
                axis=0)
            ok_g.append(causal & (picked > 0.5))
        return ok_g

    m_s, l_s = stream(ks_ref, vst_ref, i + 1, sel_chunk, sel_mask)
    for h in range(H_NSA):
        os_ref[h] = jnp.where(m_s[h:h + 1] > 0.5 * NEG, acc_ref[h] / jnp.maximum(l_s[h:h + 1], TINY), 0.0)

    n_back = WINDOW // KV_CHUNK

    def win_chunk(c):
        return pl.multiple_of(jnp.clip(i - n_back + c, 0, i) * KV_CHUNK, KV_CHUNK), n_back - c

    def win_mask(c, rel):
        dist = rel * KV_CHUNK + dmat + jnp.where(i - n_back + c < 0, WINDOW, 0)
        ok = (dist >= 0) & (dist < WINDOW)
        return [ok, ok]

    m_w, l_w = stream(kw_ref, vwt_ref, n_back + 1, win_chunk, win_mask)

    for k2 in range(H_NSA // 2):
        pair = []
        for h in (2 * k2, 2 * k2 + 1):
            o_w = jnp.where(m_w[h:h + 1] > 0.5 * NEG, acc_ref[h] / jnp.maximum(l_w[h:h + 1], TINY), 0.0)
            pair.append(sig_t[3 * h:3 * h + 1] * o_c[h] + sig_t[3 * h + 1:3 * h + 2] * os_ref[h]
                        + sig_t[3 * h + 2:3 * h + 3] * o_w)
        o_ref[:, k2 * LANES:(k2 + 1) * LANES] = jnp.concatenate(pair, axis=0).T.astype(o_ref.dtype)


def _selection_consts(n_rows_cmp, n_keys):
    n = np.arange(n_rows_cmp)
    blk = np.arange(SEL_LANES)
    pool = (n[:, None] // (SEL_BLOCK // CMP_STRIDE) == blk[None, :]).astype(np.float32)
    key = np.arange(n_keys)
    expand = (blk[:, None] == key[None, :] // SEL_BLOCK).astype(np.float32)
    return jnp.asarray(pool, BF16), jnp.asarray(expand, BF16)


def nsa_prompt(z3, kvt, kc, vct, bias_ct, bias_tt):
    b, t, _ = z3.shape
    n_sub = t // CMP_STRIDE
    n_sel = -(-t // SEL_BLOCK)
    assert kc.shape[1] == n_sub and n_sub == n_sel * (SEL_BLOCK // CMP_STRIDE) and n_sel % SUBLANES == 0
    fixed = lambda shape: pl.BlockSpec(shape, lambda bi, i: (0,) * len(shape))
    v_rows = lambda branch: (branch * KV_W + N_KV * HEAD_DIM) // LANES
    return pl.pallas_call(
        functools.partial(_nsa_prompt_kernel, n_cmp=n_sub - 1, n_sel=n_sel),
        grid=(b, t // Q_BLOCK),
        in_specs=[pl.BlockSpec((None, Q_BLOCK, NSA_QW), lambda bi, i: (bi, i, COL_NQ // NSA_QW)),
                  pl.BlockSpec((None, Q_BLOCK, LANES), lambda bi, i: (bi, i, COL_NG // LANES)),
                  pl.BlockSpec((None, n_sub, LANES), lambda bi, i: (bi, 0, 0)),
                  pl.BlockSpec((None, LANES, n_sub), lambda bi, i: (bi, 0, 0)),
                  pl.BlockSpec((None, t, LANES), lambda bi, i: (bi, 0, COL_KVS // LANES)),
                  pl.BlockSpec((None, LANES, t), lambda bi, i: (bi, v_rows(1), 0)),
                  pl.BlockSpec((None, t, LANES), lambda bi, i: (bi, 0, COL_KVW // LANES)),
                  pl.BlockSpec((None, LANES, t), lambda bi, i: (bi, v_rows(2), 0)),
                  pl.BlockSpec((H_NSA, n_sub, Q_BLOCK), lambda bi, i: (0, 0, i)),
                  fixed(bias_tt.shape)],
        out_specs=pl.BlockSpec((None, Q_BLOCK, NSA_QW), lambda bi, i: (bi, i, 0)),
        out_shape=jax.ShapeDtypeStruct((b, t, NSA_QW), BF16),
        scratch_shapes=[pltpu.VMEM((LANES, H_NSA * Q_BLOCK), BF16),
                        pltpu.VMEM((n_sub, Q_BLOCK), F32),
                        pltpu.VMEM((N_KV, n_sel, Q_BLOCK), F32),
                        pltpu.VMEM((H_NSA, HEAD_DIM, Q_BLOCK), F32),
                        pltpu.VMEM((H_NSA, HEAD_DIM, Q_BLOCK), F32),
                        pltpu.VMEM((2, H_NSA, KV_CHUNK, Q_BLOCK), F32),
                        pltpu.VMEM((2, H_NSA, KV_CHUNK, Q_BLOCK), BF16)],
        compiler_params=_cparams(2),
        name="nsa_prompt",
    )(z3, z3, kc, vct, z3, kvt, z3, kvt, bias_ct, bias_tt)


def _nsa_step_kernel(pt_ref, z_ref, kc_ref, vc_ref, *rest, n_pages, past, wbuf, n_cmp, n_sel):
    del pt_ref
    pages = rest[:n_pages]
    win_ref, bcs_ref, bss_ref, bws_ref, pool_ref, exp_ref, o_ref, wo_ref = rest[n_pages:]
    z = z_ref[...]
    row8 = lax.broadcasted_iota(I32, (H_NSA, LANES), 0)
    lane8 = lax.broadcasted_iota(I32, (H_NSA, LANES), 1)
    left8 = lane8 < HEAD_DIM
    qbd = jnp.zeros((H_NSA, LANES), F32)
    for h in range(H_NSA):
        slab = jnp.broadcast_to(z[:, COL_NQ + (h // 2) * LANES:COL_NQ + (h // 2 + 1) * LANES], (H_NSA, LANES))
        g = h // R_GRP
        if h % 2 != g:
            slab = pltpu.roll(slab, HEAD_DIM, 1)
        keep = (row8 == h) & (left8 if g == 0 else jnp.logical_not(left8))
        qbd = jnp.where(keep, slab, qbd)
    qbd = (qbd * (HEAD_DIM ** -0.5)).astype(BF16)

    ncp = kc_ref.shape[0]
    n_c = lax.broadcasted_iota(I32, (H_NSA, ncp), 1)
    s_c = _dot_nt(qbd, kc_ref[...].astype(BF16)) + bcs_ref[...]
    p_c = _masked_softmax(s_c, (n_c * CMP_STRIDE + (L_CMP - 1) <= past) & (n_c < n_cmp))
    o_c = _dot(p_c.astype(BF16), vc_ref[...].astype(BF16))
    pc0 = jnp.sum(p_c[0:R_GRP], axis=0, keepdims=True)
    pc1 = jnp.sum(p_c[R_GRP:], axis=0, keepdims=True)
    rowc = lax.broadcasted_iota(I32, (H_NSA, ncp), 0)
    pcs = jnp.where(rowc < R_GRP, jnp.broadcast_to(pc0, (H_NSA, ncp)), jnp.broadcast_to(pc1, (H_NSA, ncp)))
    imp = _dot3(pcs, pool_ref[...])
    sel = _select_top(_block_scores(imp, jnp.full(imp.shape, past, I32), n_sel), min(N_SELECT, n_sel))
    picked = _dot(sel.astype(BF16), exp_ref[...])

    c128 = lax.broadcasted_iota(I32, (LANES, KV_CHUNK), 1)

    def new_col(col):
        return jnp.broadcast_to(z[:, col:col + LANES], (KV_CHUNK, LANES)).T

    def first_col(tile):
        return jnp.where(c128 == 0, tile, 0.0).astype(BF16)

    ks = [pg[0:LANES, :].astype(BF16) for pg in pages] + [first_col(new_col(COL_KVS))]
    vs = [pg[LANES:2 * LANES, :].astype(BF16) for pg in pages] + [first_col(new_col(COL_KVS + LANES))]
    s_s = jnp.concatenate([_dot(qbd, k) for k in ks], axis=1) + bss_ref[...]
    key = lax.broadcasted_iota(I32, s_s.shape, 1)
    p_s = _masked_softmax(s_s, (picked > 0.5) & (key <= past)).astype(BF16)
    o_s = _dot_nt(p_s[:, 0:KV_CHUNK], vs[0])
    for c in range(1, len(vs)):
        o_s = o_s + _dot_nt(p_s[:, c * KV_CHUNK:(c + 1) * KV_CHUNK], vs[c])

    kw_new, vw_new = new_col(COL_KVW), new_col(COL_KVW + LANES)
    win = win_ref[...]
    s_w = jnp.concatenate([_dot(qbd, win[0:LANES].astype(BF16)), _dot(qbd, first_col(kw_new))], axis=1) + bws_ref[...]
    colw = lax.broadcasted_iota(I32, s_w.shape, 1)
    p_w = _masked_softmax(s_w, (colw <= wbuf) & (wbuf - colw < WINDOW)).astype(BF16)
    o_w = _dot_nt(p_w[:, 0:wbuf], win[LANES:].astype(BF16)) + _dot_nt(p_w[:, wbuf:], first_col(vw_new))

    shifted = pltpu.roll(win, wbuf - 1, 1)
    new_kv = jnp.concatenate([kw_new, vw_new], axis=0)
    last = jnp.where(lax.broadcasted_iota(I32, (KV_W, LANES), 1) == LANES - 1, new_kv, shifted[:, wbuf - LANES:])
    wo_ref[:, 0:wbuf - LANES] = shifted[:, 0:wbuf - LANES]
    wo_ref[:, wbuf - LANES:] = last

    sig = jnp.broadcast_to(_sigmoid(z[:, COL_NG:COL_NG + LANES]), (H_NSA, LANES))

    def gate_col(c):
        return jnp.sum(jnp.where(lane8 == row8 * 3 + c, sig, 0.0), axis=1, keepdims=True)

    o = gate_col(0) * o_c + gate_col(1) * o_s + gate_col(2) * o_w
    fix = jnp.where((row8 % 2) != (row8 // R_GRP), pltpu.roll(o, HEAD_DIM, 1), o)
    left1 = left8[0:1]
    o_ref[...] = jnp.concatenate(
        [jnp.where(left1, fix[2 * k2:2 * k2 + 1], fix[2 * k2 + 1:2 * k2 + 2]) for k2 in range(H_NSA // 2)], axis=1)


def nsa_step(zs3, kc, vc, sel_cache, win_cache, page_idx, win_row0, bias_cs, bias_ss, bias_ws, past):
    bs = zs3.shape[0]
    n_pages = page_idx.shape[1]
    wbuf = win_cache.shape[2]
    n_sub = (past + 1) // CMP_STRIDE
    n_sel = -(-(past + 1) // SEL_BLOCK)
    n_keys = (n_pages + 1) * PAGE_SIZE
    assert n_sel <= SEL_LANES and PAGE_SIZE == KV_CHUNK and wbuf % KV_CHUNK == 0 and kc.shape[1] == n_sub
    pool, expand = _selection_consts(n_sub, n_keys)
    fixed = lambda shape: pl.BlockSpec(shape, lambda b, pt: (0,) * len(shape))
    page = lambda p: pl.BlockSpec((None, KV_W, PAGE_SIZE), lambda b, pt, p=p: (pt[b, p], 0, 0))
    grid_spec = pltpu.PrefetchScalarGridSpec(
        num_scalar_prefetch=1,
        grid=(bs,),
        in_specs=[pl.BlockSpec((None, 1, N_IN_PAD), lambda b, pt: (b, 0, 0)),
                  pl.BlockSpec((None, n_sub, LANES), lambda b, pt: (b, 0, 0)),
                  pl.BlockSpec((None, n_sub, LANES), lambda b, pt: (b, 0, 0))]
        + [page(p) for p in range(n_pages)]
        + [pl.BlockSpec((None, KV_W, wbuf), lambda b, pt: (win_row0 + b, 0, 0)),
           fixed(bias_cs.shape), fixed(bias_ss.shape), fixed(bias_ws.shape), fixed(pool.shape), fixed(expand.shape)],
        out_specs=[pl.BlockSpec((None, 1, NSA_QW), lambda b, pt: (b, 0, 0)),
                   pl.BlockSpec((None, KV_W, wbuf), lambda b, pt: (b, 0, 0))],
    )
    return pl.pallas_call(
        functools.partial(_nsa_step_kernel, n_pages=n_pages, past=past, wbuf=wbuf, n_cmp=n_sub - 1, n_sel=n_sel),
        grid_spec=grid_spec,
        out_shape=[jax.ShapeDtypeStruct((bs, 1, NSA_QW), F32), jax.ShapeDtypeStruct((bs, KV_W, wbuf), F32)],
        compiler_params=_cparams(1),
    )(page_idx, zs3, kc, vc, *([sel_cache] * n_pages), win_cache, bias_cs, bias_ss, bias_ws, pool, expand)


def _route(logits, tm):
    m = logits.shape[0]
    top_v, top_i = lax.top_k(logits, TOP_K)
    top_w = jax.nn.softmax(top_v, axis=-1)
    slot_e = top_i.reshape(-1)
    onehot = (slot_e[:, None] == jnp.arange(N_EXPERTS)[None, :]).astype(I32)
    before = jnp.cumsum(onehot, axis=0) - onehot
    rank = jnp.sum(before * onehot, axis=1)
    count = jnp.sum(onehot, axis=0)
    padded = -(-count // tm) * tm
    start = jnp.cumsum(padded) - padded
    pos = start[slot_e] + rank
    n_rows = (-(-(m * TOP_K) // tm) + N_EXPERTS) * tm
    row_token = jnp.zeros((n_rows,), I32).at[pos].set(jnp.arange(m * TOP_K, dtype=I32) // TOP_K)
    row_gate = jnp.zeros((n_rows,), F32).at[pos].set(top_w.reshape(-1))
    tile_start = jnp.arange(n_rows // tm, dtype=I32) * tm
    tile_expert = jnp.clip(jnp.searchsorted(jnp.cumsum(padded), tile_start, side='right'), 0, N_EXPERTS - 1)
    tile_valid = (tile_start < jnp.sum(padded)).astype(I32)
    return pos.reshape(m, TOP_K), row_token, row_gate.reshape(n_rows, 1), tile_expert.astype(I32), tile_valid


def _dense_ffn(h, g, w1, w3, w2):
    m = h.shape[0]
    tm = _row_tile(m)
    ones = jnp.ones((m // tm,), I32)
    return grouped_ffn(h, g, jnp.ones((m, 1), F32), jnp.zeros((m // tm,), I32), ones,
                       w1[None], w3[None], w2[None], tm=tm, tf=w1.shape[1] // 2, residual=True)


def kernel(x_prompt, x_sample, p_prompt, p_sample, state_ret, cache_win_kv, cache_cmp_kv, cache_sel_kv, page_table, rel_bias, norm_mix, w_in, ret_gain, cmp_pe, cmp_w1, cmp_w2, w_out, norm_ffn, ffn_w1, ffn_w3, ffn_w2, router, moe_w1, moe_w3, moe_w2, ple_norm, ple_gate, ple_proj, norm_final):
    b_p, t_p, d = x_prompt.shape
    b_s = x_sample.shape[0]
    depth = w_in.shape[0]
    n_pages = page_table.shape[1]
    past = n_pages * PAGE_SIZE
    n_pool = cache_cmp_kv.shape[1]
    w_buf = cache_win_kv.shape[2]
    assert x_sample.shape[1] == 1 and t_p % Q_BLOCK == 0 and t_p >= w_buf and past % CMP_STRIDE == 0
    row = (2, N_KV, HEAD_DIM)
    m_p = b_p * t_p

    ar = lambda n: jnp.arange(n, dtype=I32)
    n_sub_p = t_p // CMP_STRIDE
    bias_ct = bias_lookup(rel_bias, ar(t_p)[None, :] - (ar(n_sub_p)[:, None] * CMP_STRIDE + L_CMP - 1))
    rel_t = (ar(FAR_REL + 1)[:, None, None] * KV_CHUNK + ar(Q_BLOCK)[None, None, :] - ar(KV_CHUNK)[None, :, None])
    bias_tt = bias_lookup(rel_bias, rel_t.reshape((FAR_REL + 1) * KV_CHUNK, Q_BLOCK))
    bias_tt = bias_tt.reshape(H_NSA, FAR_REL + 1, KV_CHUNK, Q_BLOCK).transpose(1, 0, 2, 3)
    n_sub_s = (past + 1) // CMP_STRIDE
    rows8 = jnp.zeros((SUBLANES, 1), I32)
    bias_cs = bias_lookup(rel_bias, rows8 + (past - (ar(n_sub_s)[None, :] * CMP_STRIDE + L_CMP - 1)))[:, 0]
    bias_ss = bias_lookup(rel_bias, rows8 + (past - ar((n_pages + 1) * PAGE_SIZE)[None, :]))[:, 0]
    bias_ws = bias_lookup(rel_bias, rows8 + (w_buf - ar(w_buf + KV_CHUNK)[None, :]))[:, 0]

    def rows_last(x):
        return jnp.moveaxis(x, 2, -1).reshape(x.shape[0] * x.shape[1], KV_W, x.shape[2])

    def rows_first(xt, lead):
        return jnp.moveaxis(xt.reshape((lead,) + row + (xt.shape[-1],)), -1, 1)

    cmp_cache = rows_last(cache_cmp_kv)
    sel_cache = rows_last(cache_sel_kv)
    win_cache = rows_last(cache_win_kv)
    state_t = jnp.moveaxis(state_ret, 1, -1).reshape(depth * H_RET, HEAD_DIM, HEAD_DIM, b_s)

    h_p = x_prompt.reshape(m_p, d)
    h_s = x_sample.reshape(b_s, d)
    rp, rs, wp, ws, cp, cs, sp, ss = [], [], [], [], [], [], [], []
    y_p = y_s = None
    for i in range(depth):
        w_in_b = jnp.pad(w_in[i], ((0, 0), (0, N_IN_PAD - N_IN))).astype(BF16)
        w_out_b = w_out[i].astype(BF16)
        pe_t, cw1, cw2 = _compress_weights(cmp_pe[i], cmp_w1[i], cmp_w2[i])

        w_kvt = w_in[i][:, COL_KVC:COL_KVC + 3 * KV_W].T.astype(BF16)
        z, kvt = rms_matmul_kvt(h_p, norm_mix[i], w_in_b, w_kvt, b_p)
        z3 = z.reshape(b_p, t_p, N_IN_PAD)
        ret_y, ret_state = retention_prompt(z3, ret_gain[i])
        kc, vc = compress(kvt, ar(b_p)[:, None], pe_t, cw1, cw2)
        nsa_y = nsa_prompt(z3, kvt, kc, jnp.swapaxes(vc, 1, 2), bias_ct, bias_tt)
        h_p = out_proj(h_p, ret_y.reshape(m_p, RET_W), nsa_y.reshape(m_p, NSA_QW), w_out_b)
        rp.append(ret_state)
        cp.append(rows_first(kvt[:, 0:KV_W], b_p))
        sp.append(rows_first(kvt[:, KV_W:2 * KV_W], b_p))
        wp.append(rows_first(kvt[:, 2 * KV_W:, t_p - w_buf:], b_p))

        zs = rms_matmul(h_s, norm_mix[i], w_in_b)
        zst = zs.T
        ret_yst, state_s = retention_step(zst, state_t, i, past, ret_gain[i])
        page_idx = page_table + i * n_pool
        kc_s, vc_s = compress(cmp_cache, page_idx, pe_t, cw1, cw2)
        nsa_ys, win_s = nsa_step(zs.reshape(b_s, 1, N_IN_PAD), kc_s, vc_s, sel_cache, win_cache, page_idx, i * b_s,
                                 bias_cs, bias_ss, bias_ws, past)
        h_s = out_proj(h_s, ret_yst.T, nsa_ys.reshape(b_s, NSA_QW), w_out_b)
        rs.append(jnp.moveaxis(state_s, -1, 0))
        ws.append(rows_first(win_s, b_s))
        cs.append(rows_first(zst[None, COL_KVC:COL_KVC + KV_W], 1).reshape((b_s, 1) + row))
        ss.append(rows_first(zst[None, COL_KVS:COL_KVS + KV_W], 1).reshape((b_s, 1) + row))

        j = i // 2
        if i % 2 == 0:
            w1, w3, w2 = ffn_w1[j].astype(BF16), ffn_w3[j].astype(BF16), ffn_w2[j].astype(BF16)
            h_p = _dense_ffn(h_p, norm_ffn[i], w1, w3, w2)
            h_s = _dense_ffn(h_s, norm_ffn[i], w1, w3, w2)
        else:
            h_all = jnp.concatenate([h_p, h_s], axis=0)
            m_all = h_all.shape[0]
            m_pad = -(-m_all // 256) * 256
            router_w = jnp.pad(router[j], ((0, 0), (0, LANES - N_EXPERTS)))
            logits = router_logits(jnp.pad(h_all, ((0, m_pad - m_all), (0, 0))), norm_ffn[i], router_w)
            tm = 512
            pos, row_token, row_gate, tile_expert, tile_valid = _route(logits[:m_all, :N_EXPERTS], tm)
            y_rows = grouped_ffn(h_all[row_token], norm_ffn[i], row_gate, tile_expert, tile_valid,
                                 moe_w1[j].astype(BF16), moe_w3[j].astype(BF16), moe_w2[j].astype(BF16),
                                 tm=tm, tf=moe_w1.shape[3] // 4, residual=False)
            h_all = h_all + y_rows[pos[:, 0]] + y_rows[pos[:, 1]]
            h_p, h_s = h_all[:m_p], h_all[m_p:]

        wg, wpj = ple_gate[i].astype(BF16), ple_proj[i].astype(BF16)
        h_p, y_p = ple(h_p, ple_norm[i], wg, p_prompt[i].reshape(m_p, P_DIM), wpj, norm_final)
        h_s, y_s = ple(h_s, ple_norm[i], wg, p_sample[i].reshape(b_s, P_DIM), wpj, norm_final)

    return (y_p.reshape(b_p, t_p, d), y_s.reshape(b_s, 1, d), jnp.stack(rp), jnp.stack(rs), jnp.stack(wp),
            jnp.stack(ws), jnp.stack(cp), jnp.stack(cs), jnp.stack(sp), jnp.stack(ss))
```

```python
import functools
import math

import numpy as np
import jax
import jax.numpy as jnp
from jax import lax
from jax.experimental import pallas as pl
from jax.experimental.pallas import tpu as pltpu

F32 = jnp.float32
BF16 = jnp.bfloat16
I32 = jnp.int32

LANES = 128
SUBLANES = 8
VMEM_LIMIT_BYTES = 56 * 1024 * 1024

D_MODEL = 1024
HEAD_DIM = 64
H_RET = 8
H_NSA = 8
N_KV = 2
R_GRP = H_NSA // N_KV
RET_W = H_RET * HEAD_DIM
NSA_QW = H_NSA * HEAD_DIM
KV_W = 2 * N_KV * HEAD_DIM
N_IN = 4 * RET_W + NSA_QW + 3 * KV_W + 3 * H_NSA
N_IN_PAD = 3584
COL_RQ, COL_RK, COL_RV, COL_RG = 0, RET_W, 2 * RET_W, 3 * RET_W
COL_NQ = 4 * RET_W
COL_KVC = COL_NQ + NSA_QW
COL_KVS = COL_KVC + KV_W
COL_KVW = COL_KVS + KV_W
COL_NG = COL_KVW + KV_W
RET_CHUNK = 128
CMP_STRIDE = 16
L_CMP = 2 * CMP_STRIDE
CMP_HIDDEN = 2 * HEAD_DIM
SEL_BLOCK = 64
N_SELECT = 16
WINDOW = 512
Q_BLOCK = 128
N_BUCKETS = 32
MAX_DISTANCE = 128
N_EXPERTS = 8
TOP_K = 2
P_DIM = 256
PAGE_SIZE = 128
ROPE_BASE = 10000.0
EPS = 1e-6
FORCED_SCORE = 1e4
NEG = -1e30
TINY = float(np.finfo(np.float32).tiny)
SEL_LANES = 64
FAR_REL = 2
KV_CHUNK = 128


def _cparams(n_grid):
    return pltpu.CompilerParams(dimension_semantics=("arbitrary",) * n_grid,
                                vmem_limit_bytes=VMEM_LIMIT_BYTES)


def _dot(a, b):
    return jnp.dot(a, b, preferred_element_type=F32)


def _dot_nt(a, b):
    return lax.dot_general(a, b, (((1,), (1,)), ((), ())), preferred_element_type=F32)


def _split3(x):
    hi = x.astype(BF16)
    r = x - hi.astype(F32)
    mid = r.astype(BF16)
    lo = (r - mid.astype(F32)).astype(BF16)
    return hi, mid, lo


def _dot3(x, b01):
    hi, mid, lo = _split3(x)
    return _dot(hi, b01) + _dot(mid, b01) + _dot(lo, b01)


def _dot6(x, w):
    xh, xm, xl = _split3(x)
    wh, wm, wl = _split3(w)
    return (_dot(xh, wh) + _dot(xh, wm) + _dot(xm, wh)
            + _dot(xm, wm) + _dot(xh, wl) + _dot(xl, wh))


def _sigmoid(x):
    return 1.0 / (1.0 + jnp.exp(-x))


def _rms(x, g):
    return x * lax.rsqrt(jnp.mean(x * x, axis=-1, keepdims=True) + EPS) * g


def _row_tile(m):
    for t in (512, 256, 128, 64, 32, 16, 8):
        if m % t == 0:
            return t
    raise ValueError(f"row count {m} not a multiple of 8")


def _rms_matmul_kernel(x_ref, g_ref, w_ref, o_ref):
    o_ref[...] = _dot(_rms(x_ref[...], g_ref[...]).astype(BF16), w_ref[...])


def rms_matmul(x, g, w):
    m, d = x.shape
    n = w.shape[1]
    tm = min(_row_tile(m), 256)
    return pl.pallas_call(
        _rms_matmul_kernel,
        grid=(m // tm,),
        in_specs=[pl.BlockSpec((tm, d), lambda i: (i, 0)),
                  pl.BlockSpec((1, d), lambda i: (0, 0)),
                  pl.BlockSpec((d, n), lambda i: (0, 0))],
        out_specs=pl.BlockSpec((tm, n), lambda i: (i, 0)),
        out_shape=jax.ShapeDtypeStruct((m, n), F32),
        compiler_params=_cparams(1),
    )(x, g.reshape(1, d), w)


def _rms_matmul_kvt_kernel(x_ref, g_ref, w_ref, wt_ref, o_ref, ot_ref):
    xn = _rms(x_ref[...], g_ref[...]).astype(BF16)
    o_ref[...] = _dot(xn, w_ref[...])
    ot_ref[...] = _dot_nt(wt_ref[...], xn)


def rms_matmul_kvt(x, g, w, wt, bsz):
    m, d = x.shape
    n = w.shape[1]
    nt = wt.shape[0]
    t = m // bsz
    tm = min(_row_tile(t), 256)
    per = t // tm
    return pl.pallas_call(
        _rms_matmul_kvt_kernel,
        grid=(m // tm,),
        in_specs=[pl.BlockSpec((tm, d), lambda i: (i, 0)),
                  pl.BlockSpec((1, d), lambda i: (0, 0)),
                  pl.BlockSpec((d, n), lambda i: (0, 0)),
                  pl.BlockSpec((nt, d), lambda i: (0, 0))],
        out_specs=[pl.BlockSpec((tm, n), lambda i: (i, 0)),
                   pl.BlockSpec((None, nt, tm), lambda i: (i // per, 0, i % per))],
        out_shape=[jax.ShapeDtypeStruct((m, n), F32), jax.ShapeDtypeStruct((bsz, nt, t), F32)],
        compiler_params=_cparams(1),
        name="rms_matmul_kvt",
    )(x, g.reshape(1, d), w, wt)


def _router_kernel(x_ref, g_ref, w_ref, o_ref):
    o_ref[...] = _dot6(_rms(x_ref[...], g_ref[...]), w_ref[...])


def router_logits(x, g, w):
    m, d = x.shape
    n = w.shape[1]
    tm = min(_row_tile(m), 256)
    return pl.pallas_call(
        _router_kernel,
        grid=(m // tm,),
        in_specs=[pl.BlockSpec((tm, d), lambda i: (i, 0)),
                  pl.BlockSpec((1, d), lambda i: (0, 0)),
                  pl.BlockSpec((d, n), lambda i: (0, 0))],
        out_specs=pl.BlockSpec((tm, n), lambda i: (i, 0)),
        out_shape=jax.ShapeDtypeStruct((m, n), F32),
        compiler_params=_cparams(1),
    )(x, g.reshape(1, d), w)


def _outproj_kernel(h_ref, a_ref, b_ref, w_ref, o_ref):
    ka = a_ref.shape[1]
    o_ref[...] = (h_ref[...] + _dot(a_ref[...].astype(BF16), w_ref[0:ka, :])
                  + _dot(b_ref[...].astype(BF16), w_ref[ka:, :]))


def out_proj(h, a, b, w):
    m, d = h.shape
    ka, kb = a.shape[1], b.shape[1]
    tm = _row_tile(m)
    return pl.pallas_call(
        _outproj_kernel,
        grid=(m // tm,),
        in_specs=[pl.BlockSpec((tm, d), lambda i: (i, 0)),
                  pl.BlockSpec((tm, ka), lambda i: (i, 0)),
                  pl.BlockSpec((tm, kb), lambda i: (i, 0)),
                  pl.BlockSpec((ka + kb, d), lambda i: (0, 0))],
        out_specs=pl.BlockSpec((tm, d), lambda i: (i, 0)),
        out_shape=jax.ShapeDtypeStruct((m, d), F32),
        compiler_params=_cparams(1),
        name="out_proj",
    )(h, a, b, w)


def _ffn_kernel(te_ref, tv_ref, x_ref, g_ref, gate_ref, w1_ref, w3_ref, w2_ref, o_ref,
                xn_ref, acc_ref, *, residual):
    i = pl.program_id(0)
    j = pl.program_id(1)

    @pl.when(j == 0)
    def _():
        xn_ref[...] = _rms(x_ref[...], g_ref[...]).astype(BF16)
        acc_ref[...] = jnp.zeros_like(acc_ref)

    @pl.when(tv_ref[i] != 0)
    def _():
        xn = xn_ref[...]
        a = _dot(xn, w1_ref[...].astype(BF16))
        b = _dot(xn, w3_ref[...].astype(BF16))
        hid = (a * _sigmoid(a)) * b
        acc_ref[...] += _dot(hid.astype(BF16), w2_ref[...].astype(BF16))

    @pl.when(j == pl.num_programs(1) - 1)
    def _():
        y = acc_ref[...] * gate_ref[...]
        if residual:
            y = y + x_ref[...]
        o_ref[...] = y


def grouped_ffn(x, g, row_gate, tile_expert, tile_valid, w1, w3, w2, *, tm, tf, residual):
    m, d = x.shape
    f = w1.shape[2]
    grid_spec = pltpu.PrefetchScalarGridSpec(
        num_scalar_prefetch=2,
        grid=(m // tm, f // tf),
        in_specs=[pl.BlockSpec((tm, d), lambda i, j, te, tv: (i, 0)),
                  pl.BlockSpec((1, d), lambda i, j, te, tv: (0, 0)),
                  pl.BlockSpec((tm, 1), lambda i, j, te, tv: (i, 0)),
                  pl.BlockSpec((None, d, tf), lambda i, j, te, tv: (te[i], 0, j)),
                  pl.BlockSpec((None, d, tf), lambda i, j, te, tv: (te[i], 0, j)),
                  pl.BlockSpec((None, tf, d), lambda i, j, te, tv: (te[i], j, 0))],
        out_specs=pl.BlockSpec((tm, d), lambda i, j, te, tv: (i, 0)),
        scratch_shapes=[pltpu.VMEM((tm, d), BF16), pltpu.VMEM((tm, d), F32)],
    )
    return pl.pallas_call(
        functools.partial(_ffn_kernel, residual=residual),
        grid_spec=grid_spec,
        out_shape=jax.ShapeDtypeStruct((m, d), F32),
        compiler_params=_cparams(2),
        name="grouped_ffn",
    )(tile_expert, tile_valid, x, g.reshape(1, d), row_gate, w1, w3, w2)


def _ple_kernel(h_ref, g_ref, wg_ref, p_ref, wp_ref, gf_ref, o_ref, *, final):
    h = h_ref[...]
    gate = _sigmoid(_dot(_rms(h, g_ref[...]).astype(BF16), wg_ref[...]))
    proj = _dot(p_ref[...].astype(BF16), wp_ref[...])
    h2 = h + gate * proj
    o_ref[...] = _rms(h2, gf_ref[...]) if final else h2


def ple(h, g, wg, p, wp, gf, final):
    m, d = h.shape
    pd = p.shape[1]
    tm = _row_tile(m)
    row = lambda i: (i, 0)
    fixed = lambda i: (0, 0)
    return pl.pallas_call(
        functools.partial(_ple_kernel, final=final),
        grid=(m // tm,),
        in_specs=[pl.BlockSpec((tm, d), row), pl.BlockSpec((1, d), fixed),
                  pl.BlockSpec((d, d), fixed), pl.BlockSpec((tm, pd), row),
                  pl.BlockSpec((pd, d), fixed), pl.BlockSpec((1, d), fixed)],
        out_specs=pl.BlockSpec((tm, d), row),
        out_shape=jax.ShapeDtypeStruct((m, d), F32),
        compiler_params=_cparams(1),
        name="ple",
    )(h, g.reshape(1, d), wg, p, wp, gf.reshape(1, d))


def _bias_kernel(tab_ref, d_ref, o_ref):
    n = jnp.maximum(d_ref[...], 0)
    max_exact = N_BUCKETS // 2
    nf = jnp.maximum(n, 1).astype(F32)
    large = max_exact + (jnp.log(nf / max_exact) / math.log(MAX_DISTANCE / max_exact)
                         * (N_BUCKETS - max_exact)).astype(I32)
    large = jnp.minimum(large, N_BUCKETS - 1)
    bucket = jnp.where(n < max_exact, n, large)
    for h in range(H_NSA):
        acc = jnp.zeros(n.shape, F32)
        for k in range(N_BUCKETS):
            acc = jnp.where(bucket == k, tab_ref[k, h], acc)
        o_ref[h] = acc


def bias_lookup(table, dist):
    r, c = dist.shape
    tr = min(r, 128)
    tc = 512 if c % 512 == 0 else c
    return pl.pallas_call(
        _bias_kernel,
        grid=(r // tr, c // tc),
        in_specs=[pl.BlockSpec(memory_space=pltpu.SMEM),
                  pl.BlockSpec((tr, tc), lambda i, j: (i, j))],
        out_specs=pl.BlockSpec((H_NSA, tr, tc), lambda i, j: (0, i, j)),
        out_shape=jax.ShapeDtypeStruct((H_NSA, r, c), F32),
        compiler_params=_cparams(2),
        name="bias_lookup",
    )(table, dist)


def _swap_halves(x):
    lane = lax.broadcasted_iota(I32, x.shape, 1)
    return jnp.where(lane % HEAD_DIM < HEAD_DIM // 2,
                     pltpu.roll(x, LANES - HEAD_DIM // 2, 1), pltpu.roll(x, HEAD_DIM // 2, 1))


def _retention_prompt_kernel(q_ref, k_ref, v_ref, g_ref, cos_ref, sin_ref, gain_ref, intra_ref,
                             qdec_ref, kdec_ref, cdec_ref, y_ref, s_out_ref, s_ref):
    c = pl.program_id(1)

    @pl.when(c == 0)
    def _():
        s_ref[...] = jnp.zeros_like(s_ref)

    cos = cos_ref[...]
    sin = sin_ref[...]
    lane = lax.broadcasted_iota(I32, (RET_CHUNK, LANES), 1)
    left = lane < HEAD_DIM
    row = lax.broadcasted_iota(I32, (LANES, LANES), 0)
    col = lax.broadcasted_iota(I32, (LANES, LANES), 1)
    same_head = (row < HEAD_DIM) == (col < HEAD_DIM)
    ones_bd = jnp.where(same_head, 1.0, 0.0).astype(BF16)
    for p in range(H_RET // 2):
        sl = slice(p * LANES, (p + 1) * LANES)
        q = q_ref[:, sl]
        k = k_ref[:, sl]
        v = v_ref[:, sl].astype(BF16)
        qr = q * cos + _swap_halves(q) * sin
        kr = (k * cos + _swap_halves(k) * sin) * (HEAD_DIM ** -0.5)
        qb = qr.astype(BF16)
        q2 = jnp.concatenate([jnp.where(left, qr, 0.0), jnp.where(left, 0.0, qr)], axis=0).astype(BF16)
        inner = _dot_nt(q2, kr.astype(BF16)) * intra_ref[p]
        o2 = _dot(inner.astype(BF16), v)
        s_old = s_ref[p]
        cross = _dot(qb, s_old.astype(BF16)) * qdec_ref[:, sl]
        o = jnp.where(left, o2[:RET_CHUNK], o2[RET_CHUNK:]) + cross
        kd = (kr * kdec_ref[:, sl]).T.astype(BF16)
        s_ref[p] = s_old * cdec_ref[p] + jnp.where(same_head, _dot(kd, v), 0.0)
        ms = _dot3(o * o, ones_bd) * (1.0 / HEAD_DIM)
        gate = g_ref[:, sl]
        y = o * lax.rsqrt(ms + EPS) * gain_ref[:, sl] * (gate * _sigmoid(gate))
        y_ref[:, sl] = y.astype(y_ref.dtype)

    @pl.when(c == pl.num_programs(1) - 1)
    def _():
        for p in range(H_RET // 2):
            s = s_ref[p]
            s_out_ref[2 * p] = s[:HEAD_DIM, :HEAD_DIM]
            s_out_ref[2 * p + 1] = s[HEAD_DIM:, HEAD_DIM:]


def _retention_consts(c):
    log_g = jnp.log1p(-jnp.exp2(-5.0 - jnp.arange(H_RET, dtype=F32)))
    j = jnp.arange(c, dtype=F32)
    diff = j[:, None] - j[None, :]
    intra = jnp.where(diff >= 0, jnp.exp(jnp.maximum(diff, 0.0)[None] * log_g[:, None, None]), 0.0)
    q_decay = jnp.exp((j[:, None] + 1.0) * log_g[None, :])
    k_decay = jnp.exp((c - 1.0 - j[:, None]) * log_g[None, :])
    c_decay = jnp.exp(c * log_g)
    intra2 = intra.reshape(H_RET // 2, 2 * c, c)
    qdec = jnp.repeat(q_decay, HEAD_DIM, axis=1)
    kdec = jnp.repeat(k_decay, HEAD_DIM, axis=1)
    cdec = jnp.repeat(c_decay.reshape(H_RET // 2, 2), HEAD_DIM, axis=1)[:, :, None] * jnp.ones((1, 1, LANES), F32)
    return intra2, qdec, kdec, cdec


def _rotary_tables(pos):
    half = HEAD_DIM // 2
    inv = ROPE_BASE ** (-jnp.arange(half, dtype=F32) / half)
    ang = pos.astype(F32)[:, None] * inv[None, :]
    cos, sin = jnp.cos(ang), jnp.sin(ang)
    cos_t = jnp.concatenate([cos, cos, cos, cos], axis=1)
    sin_t = jnp.concatenate([-sin, sin, -sin, sin], axis=1)
    return cos_t, sin_t


def retention_prompt(z3, gain):
    b, t, _ = z3.shape
    c = RET_CHUNK
    cos_t, sin_t = _rotary_tables(jnp.arange(t))
    intra2, qdec, kdec, cdec = _retention_consts(c)
    blk = lambda col: pl.BlockSpec((None, c, RET_W), lambda bi, ci, col=col: (bi, ci, col))
    fixed2 = lambda shape: pl.BlockSpec(shape, lambda bi, ci: (0,) * len(shape))
    return pl.pallas_call(
        _retention_prompt_kernel,
        grid=(b, t // c),
        in_specs=[blk(COL_RQ // RET_W), blk(COL_RK // RET_W), blk(COL_RV // RET_W), blk(COL_RG // RET_W),
                  pl.BlockSpec((c, LANES), lambda bi, ci: (ci, 0)),
                  pl.BlockSpec((c, LANES), lambda bi, ci: (ci, 0)),
                  fixed2((1, RET_W)), fixed2((H_RET // 2, 2 * c, c)), fixed2((c, RET_W)),
                  fixed2((c, RET_W)), fixed2((H_RET // 2, LANES, LANES))],
        out_specs=[pl.BlockSpec((None, c, RET_W), lambda bi, ci: (bi, ci, 0)),
                   pl.BlockSpec((None, H_RET, HEAD_DIM, HEAD_DIM), lambda bi, ci: (bi, 0, 0, 0))],
        out_shape=[jax.ShapeDtypeStruct((b, t, RET_W), BF16),
                   jax.ShapeDtypeStruct((b, H_RET, HEAD_DIM, HEAD_DIM), F32)],
        scratch_shapes=[pltpu.VMEM((H_RET // 2, LANES, LANES), F32)],
        compiler_params=_cparams(2),
        name="retention_prompt",
    )(z3, z3, z3, z3, cos_t, sin_t, gain.reshape(1, RET_W), intra2, qdec, kdec, cdec)


def _retention_step_kernel(q_ref, k_ref, v_ref, g_ref, s_ref, cos_ref, sin_ref, gamma_ref, gain_ref,
                           y_ref, s_out_ref):
    half = HEAD_DIM // 2
    cos = cos_ref[...]
    sin = sin_ref[...]

    def rot(x):
        x1, x2 = x[:half], x[half:]
        return jnp.concatenate([x1 * cos - x2 * sin, x1 * sin + x2 * cos], axis=0)

    qr = rot(q_ref[...])
    kr = rot(k_ref[...]) * (HEAD_DIM ** -0.5)
    v = v_ref[...]
    gamma = gamma_ref[...]
    o = jnp.zeros_like(v)
    for d in range(HEAD_DIM):
        s_new = s_ref[d] * gamma + kr[d:d + 1] * v
        s_out_ref[d] = s_new
        o = o + qr[d:d + 1] * s_new
    ms = jnp.mean(o * o, axis=0, keepdims=True)
    gate = g_ref[...]
    y_ref[...] = o * lax.rsqrt(ms + EPS) * gain_ref[...] * (gate * _sigmoid(gate))


def retention_step(zt, state, layer, pos, gain):
    bs = zt.shape[1]
    half = HEAD_DIM // 2
    inv = ROPE_BASE ** (-jnp.arange(half, dtype=F32) / half)
    ang = jnp.full((1,), pos).astype(F32)[:, None] * inv[None, :]
    cos = jnp.broadcast_to(jnp.cos(ang).reshape(half, 1), (half, bs))
    sin = jnp.broadcast_to(jnp.sin(ang).reshape(half, 1), (half, bs))
    gamma = 1.0 - jnp.exp2(-5.0 - jnp.arange(H_RET, dtype=F32))
    gamma = jnp.broadcast_to(gamma.reshape(H_RET, 1, 1), (H_RET, 1, bs))
    gain_b = jnp.broadcast_to(gain.reshape(H_RET, HEAD_DIM, 1), (H_RET, HEAD_DIM, bs))
    head = lambda col: pl.BlockSpec((HEAD_DIM, bs), lambda h, col=col: (col // HEAD_DIM + h, 0))
    fixed = pl.BlockSpec((half, bs), lambda h: (0, 0))
    return pl.pallas_call(
        _retention_step_kernel,
        grid=(H_RET,),
        in_specs=[head(COL_RQ), head(COL_RK), head(COL_RV), head(COL_RG),
                  pl.BlockSpec((None, HEAD_DIM, HEAD_DIM, bs), lambda h: (layer * H_RET + h, 0, 0, 0)),
                  fixed, fixed,
                  pl.BlockSpec((None, 1, bs), lambda h: (h, 0, 0)),
                  pl.BlockSpec((None, HEAD_DIM, bs), lambda h: (h, 0, 0))],
        out_specs=[pl.BlockSpec((HEAD_DIM, bs), lambda h: (h, 0)),
                   pl.BlockSpec((None, HEAD_DIM, HEAD_DIM, bs), lambda h: (h, 0, 0, 0))],
        out_shape=[jax.ShapeDtypeStruct((RET_W, bs), F32),
                   jax.ShapeDtypeStruct((H_RET, HEAD_DIM, HEAD_DIM, bs), F32)],
        compiler_params=_cparams(1),
        name="retention_step",
    )(zt, zt, zt, zt, state, cos, sin, gamma, gain_b)


def _gelu_tanh(x):
    return 0.5 * x * (1.0 + jnp.tanh(math.sqrt(2.0 / math.pi) * (x + 0.044715 * (x * x * x))))


def _compress_kernel(*refs, n_pieces):
    x_refs = refs[1:1 + n_pieces]
    pe_ref, w1_ref, w2_ref, kc_ref, vc_ref, rows_ref = refs[1 + n_pieces:]
    width = x_refs[0].shape[1]
    n = n_pieces * width // CMP_STRIDE
    for kv, o_ref in enumerate((kc_ref, vc_ref)):
        for p, x_ref in enumerate(x_refs):
            for c in range(width // LANES):
                rows_ref[pl.ds(p * width + c * LANES, LANES), :] = (
                    x_ref[kv * LANES:(kv + 1) * LANES, c * LANES:(c + 1) * LANES].T)
        xk = jnp.concatenate([rows_ref[pl.ds(j, n, stride=CMP_STRIDE), :] for j in range(CMP_STRIDE)], axis=1)
        top = _dot((xk + pe_ref[kv, 0]).astype(BF16), w1_ref[kv, 0])
        bot = _dot((xk + pe_ref[kv, 1]).astype(BF16), w1_ref[kv, 1])
        pre = top + pltpu.roll(bot, n - 1, 0)
        hid = _gelu_tanh(pre).astype(BF16)
        o_ref[...] = jnp.concatenate(
            [_dot(hid[:, g * CMP_HIDDEN:(g + 1) * CMP_HIDDEN], w2_ref[kv]) for g in range(N_KV)], axis=1)


def _compress_weights(pe, w1, w2):
    pe_r = pe.reshape(2, 2, CMP_STRIDE, 1, HEAD_DIM)
    pe_t = jnp.broadcast_to(pe_r, (2, 2, CMP_STRIDE, N_KV, HEAD_DIM)).reshape(2, 2, 1, CMP_STRIDE * LANES)
    w1r = w1.reshape(2, 2, CMP_STRIDE, HEAD_DIM, CMP_HIDDEN)
    eye = jnp.eye(N_KV, dtype=w1.dtype)
    wbig = jnp.einsum('khjdc,gG->khjgdGc', w1r, eye)
    wbig = wbig.reshape(2, 2, CMP_STRIDE * LANES, N_KV * CMP_HIDDEN)
    return pe_t.astype(F32), wbig.astype(BF16), w2.astype(BF16)


def compress(src, idx, pe_t, w1b, w2b):
    bsz, n_pieces = idx.shape
    width = src.shape[2]
    n = n_pieces * width // CMP_STRIDE
    piece = lambda p: pl.BlockSpec((None, KV_W, width), lambda b, ix, p=p: (ix[b, p], 0, 0))
    fixed = lambda shape: pl.BlockSpec(shape, lambda b, ix: (0,) * len(shape))
    out = pl.BlockSpec((None, n, LANES), lambda b, ix: (b, 0, 0))
    grid_spec = pltpu.PrefetchScalarGridSpec(
        num_scalar_prefetch=1,
        grid=(bsz,),
        in_specs=[piece(p) for p in range(n_pieces)]
        + [fixed(pe_t.shape), fixed(w1b.shape), fixed(w2b.shape)],
        out_specs=[out, out],
        scratch_shapes=[pltpu.VMEM((n_pieces * width, LANES), F32)],
    )
    return pl.pallas_call(
        functools.partial(_compress_kernel, n_pieces=n_pieces),
        grid_spec=grid_spec,
        out_shape=[jax.ShapeDtypeStruct((bsz, n, LANES), F32)] * 2,
        compiler_params=_cparams(1),
        name="compress",
    )(idx, *([src] * n_pieces), pe_t, w1b, w2b)


def _masked_softmax(s, mask):
    s = jnp.where(mask, s, NEG)
    m = jnp.max(s, axis=-1, keepdims=True)
    e = jnp.where(mask, jnp.exp(s - m), 0.0)
    return e / jnp.maximum(jnp.sum(e, axis=-1, keepdims=True), TINY)


def _block_scores(imp, qpos, n_sel, axis=1):
    blk = lax.broadcasted_iota(I32, imp.shape, axis)
    cur = qpos // SEL_BLOCK
    forced = (blk == 0) | (blk == cur) | (blk == cur - 1)
    valid = blk * SEL_BLOCK <= qpos
    score = jnp.where(forced, FORCED_SCORE, jnp.where(valid, imp, -1.0))
    return jnp.where(blk < n_sel, score, -2.0)


def _select_top(score, n_top, axis=1):
    blk = lax.broadcasted_iota(I32, score.shape, axis)
    rank = jnp.zeros(score.shape, F32)
    for i in range(score.shape[axis]):
        one = score[:, i:i + 1] if axis == 1 else score[i:i + 1, :]
        beats = (one > score) | ((one == score) & (blk > i))
        rank = rank + jnp.where(beats, 1.0, 0.0)
    return jnp.where(rank < n_top, 1.0, 0.0)


def _attend_t(s, ok, m_prev, l_prev):
    s = jnp.where(ok, s, NEG)
    m_new = jnp.maximum(m_prev, jnp.max(s, axis=0, keepdims=True))
    alpha = jnp.exp(m_prev - m_new)
    p = jnp.exp(s - m_new)
    l_new = alpha * l_prev + jnp.sum(p, axis=0, keepdims=True)
    return m_new, l_new, alpha, p


def _nsa_prompt_kernel(q_ref, gate_ref, kc_ref, vct_ref, ks_ref, vst_ref, kw_ref, vwt_ref, bc_ref, bt_ref,
                       o_ref, qt_ref, pcs_ref, pick_ref, acc_ref, os_ref, s_ref, p_ref, *, n_cmp, n_sel):
    i = pl.program_id(1)
    qs = i * Q_BLOCK
    key_i = lax.broadcasted_iota(I32, (KV_CHUNK, Q_BLOCK), 0)
    tok_i = lax.broadcasted_iota(I32, (KV_CHUNK, Q_BLOCK), 1)
    row8 = lax.broadcasted_iota(I32, (H_NSA, Q_BLOCK), 0)
    zeros_half = jnp.zeros((HEAD_DIM, Q_BLOCK), F32)

    for k2 in range(H_NSA // 2):
        slab_t = q_ref[:, k2 * LANES:(k2 + 1) * LANES].T * (HEAD_DIM ** -0.5)
        for h in (2 * k2, 2 * k2 + 1):
            qh = slab_t[(h % 2) * HEAD_DIM:(h % 2 + 1) * HEAD_DIM]
            both = [qh, zeros_half] if h // R_GRP == 0 else [zeros_half, qh]
            qt_ref[:, h * Q_BLOCK:(h + 1) * Q_BLOCK] = jnp.concatenate(both, axis=0).astype(BF16)

    sig_t = _sigmoid(gate_ref[...]).T

    ncp = kc_ref.shape[0]
    kc = kc_ref[...].astype(BF16)
    n_c = lax.broadcasted_iota(I32, (ncp, Q_BLOCK), 0)
    t_c = lax.broadcasted_iota(I32, (ncp, Q_BLOCK), 1)
    ok_c = (n_c * CMP_STRIDE + (L_CMP - 1) <= qs + t_c) & (n_c < n_cmp)
    blk_pos = qs + lax.broadcasted_iota(I32, (pick_ref.shape[1], Q_BLOCK), 1)
    o_c = []
    for g in range(N_KV):
        vct = vct_ref[g * HEAD_DIM:(g + 1) * HEAD_DIM, :].astype(BF16)
        pcs = jnp.zeros((ncp, Q_BLOCK), F32)
        for r in range(R_GRP):
            h = g * R_GRP + r
            s = _dot(kc, qt_ref[:, h * Q_BLOCK:(h + 1) * Q_BLOCK]) + bc_ref[h]
            s = jnp.where(ok_c, s, NEG)
            e = jnp.where(ok_c, jnp.exp(s - jnp.max(s, axis=0, keepdims=True)), 0.0)
            p = e * (1.0 / jnp.maximum(jnp.sum(e, axis=0, keepdims=True), TINY))
            o_c.append(_dot(vct, p.astype(BF16)))
            pcs = pcs + p
        pcs_ref[...] = pcs
        per = SEL_BLOCK // CMP_STRIDE
        imp = pcs_ref[pl.ds(0, ncp // per, stride=per), :]
        for c in range(1, per):
            imp = imp + pcs_ref[pl.ds(c, ncp // per, stride=per), :]
        pick_ref[g] = _select_top(_block_scores(imp, blk_pos, n_sel, axis=0), min(N_SELECT, n_sel), axis=0)

    dmat = tok_i - key_i

    def stream(k_ref, vt_ref, n_chunks, chunk_of, mask_of):
        def scores(c, slot):
            off, _ = chunk_of(c)
            k = k_ref[pl.ds(off, KV_CHUNK), :].astype(BF16)
            for h in range(H_NSA):
                s_ref[slot, h] = _dot(k, qt_ref[:, h * Q_BLOCK:(h + 1) * Q_BLOCK])

        def values(c, slot, alpha_all):
            off, _ = chunk_of(c)
            for g in range(N_KV):
                vt = vt_ref[g * HEAD_DIM:(g + 1) * HEAD_DIM, pl.ds(off, KV_CHUNK)].astype(BF16)
                for h in range(g * R_GRP, (g + 1) * R_GRP):
                    acc_ref[h] = acc_ref[h] * alpha_all[h:h + 1] + _dot(vt, p_ref[slot, h])

        def softmax(c, slot, m_all, l_all):
            _, rel = chunk_of(c)
            ok_g = mask_of(c, rel)
            bias_idx = min(max(rel, 0), FAR_REL) if isinstance(rel, int) else jnp.clip(rel, 0, FAR_REL)
            alpha_all = jnp.zeros((H_NSA, Q_BLOCK), F32)
            for h in range(H_NSA):
                s = s_ref[slot, h] + bt_ref[bias_idx, h]
                m_new, l_new, alpha, p = _attend_t(s, ok_g[h // R_GRP], m_all[h:h + 1], l_all[h:h + 1])
                p_ref[slot, h] = p.astype(BF16)
                m_all = jnp.where(row8 == h, m_new, m_all)
                l_all = jnp.where(row8 == h, l_new, l_all)
                alpha_all = jnp.where(row8 == h, alpha, alpha_all)
            return m_all, l_all, alpha_all

        def stage(c, slot, carry, prefetch=True):
            m_all, l_all, alpha_prev = carry
            if prefetch:
                scores(c + 1, 1 - slot)
            values(c - 1, 1 - slot, alpha_prev)
            return softmax(c, slot, m_all, l_all)

        acc_ref[...] = jnp.zeros_like(acc_ref)
        p_ref[1] = jnp.zeros(p_ref.shape[1:], BF16)
        scores(0, 0)
        carry = (jnp.full((H_NSA, Q_BLOCK), NEG, F32), jnp.zeros((H_NSA, Q_BLOCK), F32),
                 jnp.ones((H_NSA, Q_BLOCK), F32))
        if isinstance(n_chunks, int):
            for c in range(n_chunks):
                carry = stage(c, c % 2, carry, prefetch=c + 1 < n_chunks)
            last = n_chunks - 1
        else:
            pairs = (n_chunks + 1) // 2
            carry = lax.fori_loop(0, pairs, lambda t, cr: stage(2 * t + 1, 1, stage(2 * t, 0, cr)), carry)
            last = 2 * pairs - 1
        m_all, l_all, alpha_last = carry
        values(last, last % 2 if isinstance(last, int) else 1, alpha_last)
        return m_all, l_all

    def sel_chunk(c):
        return pl.multiple_of(jnp.clip(c, 0, i) * KV_CHUNK, KV_CHUNK), i - c

    def sel_mask(c, rel):
        causal = rel * KV_CHUNK + dmat >= 0
        per = KV_CHUNK // SEL_BLOCK
        first = jnp.minimum(c, i) * per
        ok_g = []
        for g in range(N_KV):
            picked = jnp.concatenate(
                [jnp.broadcast_to(pick_ref[g, pl.ds(first + b, 1), :], (SEL_BLOCK, Q_BLOCK)) for b in range(per)],
                axis=0)
            ok_g.append(causal & (picked > 0.5))
        return ok_g

    m_s, l_s = stream(ks_ref, vst_ref, i + 1, sel_chunk, sel_mask)
    for h in range(H_NSA):
        os_ref[h] = jnp.where(m_s[h:h + 1] > 0.5 * NEG, acc_ref[h] / jnp.maximum(l_s[h:h + 1], TINY), 0.0)

    n_back = WINDOW // KV_CHUNK

    def win_chunk(c):
        return pl.multiple_of(jnp.clip(i - n_back + c, 0, i) * KV_CHUNK, KV_CHUNK), n_back - c

    def win_mask(c, rel):
        dist = rel * KV_CHUNK + dmat + jnp.where(i - n_back + c < 0, WINDOW, 0)
        ok = (dist >= 0) & (dist < WINDOW)
        return [ok, ok]

    m_w, l_w = stream(kw_ref, vwt_ref, n_back + 1, win_chunk, win_mask)

    for k2 in range(H_NSA // 2):
        pair = []
        for h in (2 * k2, 2 * k2 + 1):
            o_w = jnp.where(m_w[h:h + 1] > 0.5 * NEG, acc_ref[h] / jnp.maximum(l_w[h:h + 1], TINY), 0.0)
            pair.append(sig_t[3 * h:3 * h + 1] * o_c[h] + sig_t[3 * h + 1:3 * h + 2] * os_ref[h]
                        + sig_t[3 * h + 2:3 * h + 3] * o_w)
        o_ref[:, k2 * LANES:(k2 + 1) * LANES] = jnp.concatenate(pair, axis=0).T.astype(o_ref.dtype)


def _selection_consts(n_rows_cmp, n_keys):
    n = np.arange(n_rows_cmp)
    blk = np.arange(SEL_LANES)
    pool = (n[:, None] // (SEL_BLOCK // CMP_STRIDE) == blk[None, :]).astype(np.float32)
    key = np.arange(n_keys)
    expand = (blk[:, None] == key[None, :] // SEL_BLOCK).astype(np.float32)
    return jnp.asarray(pool, BF16), jnp.asarray(expand, BF16)


def nsa_prompt(z3, kvt, kc, vct, bias_ct, bias_tt):
    b, t, _ = z3.shape
    n_sub = t // CMP_STRIDE
    n_sel = -(-t // SEL_BLOCK)
    assert kc.shape[1] == n_sub and n_sub == n_sel * (SEL_BLOCK // CMP_STRIDE) and n_sel % SUBLANES == 0
    fixed = lambda shape: pl.BlockSpec(shape, lambda bi, i: (0,) * len(shape))
    v_rows = lambda branch: (branch * KV_W + N_KV * HEAD_DIM) // LANES
    return pl.pallas_call(
        functools.partial(_nsa_prompt_kernel, n_cmp=n_sub - 1, n_sel=n_sel),
        grid=(b, t // Q_BLOCK),
        in_specs=[pl.BlockSpec((None, Q_BLOCK, NSA_QW), lambda bi, i: (bi, i, COL_NQ // NSA_QW)),
                  pl.BlockSpec((None, Q_BLOCK, LANES), lambda bi, i: (bi, i, COL_NG // LANES)),
                  pl.BlockSpec((None, n_sub, LANES), lambda bi, i: (bi, 0, 0)),
                  pl.BlockSpec((None, LANES, n_sub), lambda bi, i: (bi, 0, 0)),
                  pl.BlockSpec((None, t, LANES), lambda bi, i: (bi, 0, COL_KVS // LANES)),
                  pl.BlockSpec((None, LANES, t), lambda bi, i: (bi, v_rows(1), 0)),
                  pl.BlockSpec((None, t, LANES), lambda bi, i: (bi, 0, COL_KVW // LANES)),
                  pl.BlockSpec((None, LANES, t), lambda bi, i: (bi, v_rows(2), 0)),
                  pl.BlockSpec((H_NSA, n_sub, Q_BLOCK), lambda bi, i: (0, 0, i)),
                  fixed(bias_tt.shape)],
        out_specs=pl.BlockSpec((None, Q_BLOCK, NSA_QW), lambda bi, i: (bi, i, 0)),
        out_shape=jax.ShapeDtypeStruct((b, t, NSA_QW), BF16),
        scratch_shapes=[pltpu.VMEM((LANES, H_NSA * Q_BLOCK), BF16),
                        pltpu.VMEM((n_sub, Q_BLOCK), F32),
                        pltpu.VMEM((N_KV, n_sel, Q_BLOCK), F32),
                        pltpu.VMEM((H_NSA, HEAD_DIM, Q_BLOCK), F32),
                        pltpu.VMEM((H_NSA, HEAD_DIM, Q_BLOCK), F32),
                        pltpu.VMEM((2, H_NSA, KV_CHUNK, Q_BLOCK), F32),
                        pltpu.VMEM((2, H_NSA, KV_CHUNK, Q_BLOCK), BF16)],
        compiler_params=_cparams(2),
        name="nsa_prompt",
    )(z3, z3, kc, vct, z3, kvt, z3, kvt, bias_ct, bias_tt)


def _nsa_step_kernel(pt_ref, z_ref, kc_ref, vc_ref, *rest, n_pages, past, wbuf, n_cmp, n_sel):
    del pt_ref
    pages = rest[:n_pages]
    win_ref, bcs_ref, bss_ref, bws_ref, pool_ref, exp_ref, o_ref, wo_ref = rest[n_pages:]
    z = z_ref[...]
    row8 = lax.broadcasted_iota(I32, (H_NSA, LANES), 0)
    lane8 = lax.broadcasted_iota(I32, (H_NSA, LANES), 1)
    left8 = lane8 < HEAD_DIM
    qbd = jnp.zeros((H_NSA, LANES), F32)
    for h in range(H_NSA):
        slab = jnp.broadcast_to(z[:, COL_NQ + (h // 2) * LANES:COL_NQ + (h // 2 + 1) * LANES], (H_NSA, LANES))
        g = h // R_GRP
        if h % 2 != g:
            slab = pltpu.roll(slab, HEAD_DIM, 1)
        keep = (row8 == h) & (left8 if g == 0 else jnp.logical_not(left8))
        qbd = jnp.where(keep, slab, qbd)
    qbd = (qbd * (HEAD_DIM ** -0.5)).astype(BF16)

    ncp = kc_ref.shape[0]
    n_c = lax.broadcasted_iota(I32, (H_NSA, ncp), 1)
    s_c = _dot_nt(qbd, kc_ref[...].astype(BF16)) + bcs_ref[...]
    p_c = _masked_softmax(s_c, (n_c * CMP_STRIDE + (L_CMP - 1) <= past) & (n_c < n_cmp))
    o_c = _dot(p_c.astype(BF16), vc_ref[...].astype(BF16))
    pc0 = jnp.sum(p_c[0:R_GRP], axis=0, keepdims=True)
    pc1 = jnp.sum(p_c[R_GRP:], axis=0, keepdims=True)
    rowc = lax.broadcasted_iota(I32, (H_NSA, ncp), 0)
    pcs = jnp.where(rowc < R_GRP, jnp.broadcast_to(pc0, (H_NSA, ncp)), jnp.broadcast_to(pc1, (H_NSA, ncp)))
    imp = _dot3(pcs, pool_ref[...])
    sel = _select_top(_block_scores(imp, jnp.full(imp.shape, past, I32), n_sel), min(N_SELECT, n_sel))
    picked = _dot(sel.astype(BF16), exp_ref[...])

    c128 = lax.broadcasted_iota(I32, (LANES, KV_CHUNK), 1)

    def new_col(col):
        return jnp.broadcast_to(z[:, col:col + LANES], (KV_CHUNK, LANES)).T

    def first_col(tile):
        return jnp.where(c128 == 0, tile, 0.0).astype(BF16)

    ks = [pg[0:LANES, :].astype(BF16) for pg in pages] + [first_col(new_col(COL_KVS))]
    vs = [pg[LANES:2 * LANES, :].astype(BF16) for pg in pages] + [first_col(new_col(COL_KVS + LANES))]
    s_s = jnp.concatenate([_dot(qbd, k) for k in ks], axis=1) + bss_ref[...]
    key = lax.broadcasted_iota(I32, s_s.shape, 1)
    p_s = _masked_softmax(s_s, (picked > 0.5) & (key <= past)).astype(BF16)
    o_s = _dot_nt(p_s[:, 0:KV_CHUNK], vs[0])
    for c in range(1, len(vs)):
        o_s = o_s + _dot_nt(p_s[:, c * KV_CHUNK:(c + 1) * KV_CHUNK], vs[c])

    kw_new, vw_new = new_col(COL_KVW), new_col(COL_KVW + LANES)
    win = win_ref[...]
    s_w = jnp.concatenate([_dot(qbd, win[0:LANES].astype(BF16)), _dot(qbd, first_col(kw_new))], axis=1) + bws_ref[...]
    colw = lax.broadcasted_iota(I32, s_w.shape, 1)
    p_w = _masked_softmax(s_w, (colw <= wbuf) & (wbuf - colw < WINDOW)).astype(BF16)
    o_w = _dot_nt(p_w[:, 0:wbuf], win[LANES:].astype(BF16)) + _dot_nt(p_w[:, wbuf:], first_col(vw_new))

    shifted = pltpu.roll(win, wbuf - 1, 1)
    new_kv = jnp.concatenate([kw_new, vw_new], axis=0)
    last = jnp.where(lax.broadcasted_iota(I32, (KV_W, LANES), 1) == LANES - 1, new_kv, shifted[:, wbuf - LANES:])
    wo_ref[:, 0:wbuf - LANES] = shifted[:, 0:wbuf - LANES]
    wo_ref[:, wbuf - LANES:] = last

    sig = jnp.broadcast_to(_sigmoid(z[:, COL_NG:COL_NG + LANES]), (H_NSA, LANES))

    def gate_col(c):
        return jnp.sum(jnp.where(lane8 == row8 * 3 + c, sig, 0.0), axis=1, keepdims=True)

    o = gate_col(0) * o_c + gate_col(1) * o_s + gate_col(2) * o_w
    fix = jnp.where((row8 % 2) != (row8 // R_GRP), pltpu.roll(o, HEAD_DIM, 1), o)
    left1 = left8[0:1]
    o_ref[...] = jnp.concatenate(
        [jnp.where(left1, fix[2 * k2:2 * k2 + 1], fix[2 * k2 + 1:2 * k2 + 2]) for k2 in range(H_NSA // 2)], axis=1)


def nsa_step(zs3, kc, vc, sel_cache, win_cache, page_idx, win_row0, bias_cs, bias_ss, bias_ws, past):
    bs = zs3.shape[0]
    n_pages = page_idx.shape[1]
    wbuf = win_cache.shape[2]
    n_sub = (past + 1) // CMP_STRIDE
    n_sel = -(-(past + 1) // SEL_BLOCK)
    n_keys = (n_pages + 1) * PAGE_SIZE
    assert n_sel <= SEL_LANES and PAGE_SIZE == KV_CHUNK and wbuf % KV_CHUNK == 0 and kc.shape[1] == n_sub
    pool, expand = _selection_consts(n_sub, n_keys)
    fixed = lambda shape: pl.BlockSpec(shape, lambda b, pt: (0,) * len(shape))
    page = lambda p: pl.BlockSpec((None, KV_W, PAGE_SIZE), lambda b, pt, p=p: (pt[b, p], 0, 0))
    grid_spec = pltpu.PrefetchScalarGridSpec(
        num_scalar_prefetch=1,
        grid=(bs,),
        in_specs=[pl.BlockSpec((None, 1, N_IN_PAD), lambda b, pt: (b, 0, 0)),
                  pl.BlockSpec((None, n_sub, LANES), lambda b, pt: (b, 0, 0)),
                  pl.BlockSpec((None, n_sub, LANES), lambda b, pt: (b, 0, 0))]
        + [page(p) for p in range(n_pages)]
        + [pl.BlockSpec((None, KV_W, wbuf), lambda b, pt: (win_row0 + b, 0, 0)),
           fixed(bias_cs.shape), fixed(bias_ss.shape), fixed(bias_ws.shape), fixed(pool.shape), fixed(expand.shape)],
        out_specs=[pl.BlockSpec((None, 1, NSA_QW), lambda b, pt: (b, 0, 0)),
                   pl.BlockSpec((None, KV_W, wbuf), lambda b, pt: (b, 0, 0))],
    )
    return pl.pallas_call(
        functools.partial(_nsa_step_kernel, n_pages=n_pages, past=past, wbuf=wbuf, n_cmp=n_sub - 1, n_sel=n_sel),
        grid_spec=grid_spec,
        out_shape=[jax.ShapeDtypeStruct((bs, 1, NSA_QW), F32), jax.ShapeDtypeStruct((bs, KV_W, wbuf), F32)],
        compiler_params=_cparams(1),
        name="nsa_step",
    )(page_idx, zs3, kc, vc, *([sel_cache] * n_pages), win_cache, bias_cs, bias_ss, bias_ws, pool, expand)


def _route(logits, tm):
    m = logits.shape[0]
    top_v, top_i = lax.top_k(logits, TOP_K)
    top_w = jax.nn.softmax(top_v, axis=-1)
    slot_e = top_i.reshape(-1)
    onehot = (slot_e[:, None] == jnp.arange(N_EXPERTS)[None, :]).astype(I32)
    before = jnp.cumsum(onehot, axis=0) - onehot
    rank = jnp.sum(before * onehot, axis=1)
    count = jnp.sum(onehot, axis=0)
    padded = -(-count // tm) * tm
    start = jnp.cumsum(padded) - padded
    pos = start[slot_e] + rank
    n_rows = (-(-(m * TOP_K) // tm) + N_EXPERTS) * tm
    row_token = jnp.zeros((n_rows,), I32).at[pos].set(jnp.arange(m * TOP_K, dtype=I32) // TOP_K)
    row_gate = jnp.zeros((n_rows,), F32).at[pos].set(top_w.reshape(-1))
    tile_start = jnp.arange(n_rows // tm, dtype=I32) * tm
    tile_expert = jnp.clip(jnp.searchsorted(jnp.cumsum(padded), tile_start, side='right'), 0, N_EXPERTS - 1)
    tile_valid = (tile_start < jnp.sum(padded)).astype(I32)
    return pos.reshape(m, TOP_K), row_token, row_gate.reshape(n_rows, 1), tile_expert.astype(I32), tile_valid


def _dense_ffn(h, g, w1, w3, w2):
    m = h.shape[0]
    tm = _row_tile(m)
    ones = jnp.ones((m // tm,), I32)
    return grouped_ffn(h, g, jnp.ones((m, 1), F32), jnp.zeros((m // tm,), I32), ones,
                       w1[None], w3[None], w2[None], tm=tm, tf=w1.shape[1] // 2, residual=True)


def kernel(x_prompt, x_sample, p_prompt, p_sample, state_ret, cache_win_kv, cache_cmp_kv, cache_sel_kv, page_table, rel_bias, norm_mix, w_in, ret_gain, cmp_pe, cmp_w1, cmp_w2, w_out, norm_ffn, ffn_w1, ffn_w3, ffn_w2, router, moe_w1, moe_w3, moe_w2, ple_norm, ple_gate, ple_proj, norm_final):
    b_p, t_p, d = x_prompt.shape
    b_s = x_sample.shape[0]
    depth = w_in.shape[0]
    n_pages = page_table.shape[1]
    past = n_pages * PAGE_SIZE
    n_pool = cache_cmp_kv.shape[1]
    w_buf = cache_win_kv.shape[2]
    assert x_sample.shape[1] == 1 and t_p % Q_BLOCK == 0 and t_p >= w_buf and past % CMP_STRIDE == 0
    row = (2, N_KV, HEAD_DIM)
    m_p = b_p * t_p

    ar = lambda n: jnp.arange(n, dtype=I32)
    n_sub_p = t_p // CMP_STRIDE
    bias_ct = bias_lookup(rel_bias, ar(t_p)[None, :] - (ar(n_sub_p)[:, None] * CMP_STRIDE + L_CMP - 1))
    rel_t = (ar(FAR_REL + 1)[:, None, None] * KV_CHUNK + ar(Q_BLOCK)[None, None, :] - ar(KV_CHUNK)[None, :, None])
    bias_tt = bias_lookup(rel_bias, rel_t.reshape((FAR_REL + 1) * KV_CHUNK, Q_BLOCK))
    bias_tt = bias_tt.reshape(H_NSA, FAR_REL + 1, KV_CHUNK, Q_BLOCK).transpose(1, 0, 2, 3)
    n_sub_s = (past + 1) // CMP_STRIDE
    rows8 = jnp.zeros((SUBLANES, 1), I32)
    bias_cs = bias_lookup(rel_bias, rows8 + (past - (ar(n_sub_s)[None, :] * CMP_STRIDE + L_CMP - 1)))[:, 0]
    bias_ss = bias_lookup(rel_bias, rows8 + (past - ar((n_pages + 1) * PAGE_SIZE)[None, :]))[:, 0]
    bias_ws = bias_lookup(rel_bias, rows8 + (w_buf - ar(w_buf + KV_CHUNK)[None, :]))[:, 0]

    def rows_last(x):
        return jnp.moveaxis(x, 2, -1).reshape(x.shape[0] * x.shape[1], KV_W, x.shape[2])

    def rows_first(xt, lead):
        return jnp.moveaxis(xt.reshape((lead,) + row + (xt.shape[-1],)), -1, 1)

    cmp_cache = rows_last(cache_cmp_kv)
    sel_cache = rows_last(cache_sel_kv)
    win_cache = rows_last(cache_win_kv)
    state_t = jnp.moveaxis(state_ret, 1, -1).reshape(depth * H_RET, HEAD_DIM, HEAD_DIM, b_s)

    h_p = x_prompt.reshape(m_p, d)
    h_s = x_sample.reshape(b_s, d)
    rp, rs, wp, ws, cp, cs, sp, ss = [], [], [], [], [], [], [], []
    for i in range(depth):
        w_in_b = jnp.pad(w_in[i], ((0, 0), (0, N_IN_PAD - N_IN))).astype(BF16)
        w_out_b = w_out[i].astype(BF16)
        pe_t, cw1, cw2 = _compress_weights(cmp_pe[i], cmp_w1[i], cmp_w2[i])

        w_kvt = w_in[i][:, COL_KVC:COL_KVC + 3 * KV_W].T.astype(BF16)
        z, kvt = rms_matmul_kvt(h_p, norm_mix[i], w_in_b, w_kvt, b_p)
        z3 = z.reshape(b_p, t_p, N_IN_PAD)
        ret_y, ret_state = retention_prompt(z3, ret_gain[i])
        kc, vc = compress(kvt, ar(b_p)[:, None], pe_t, cw1, cw2)
        nsa_y = nsa_prompt(z3, kvt, kc, jnp.swapaxes(vc, 1, 2), bias_ct, bias_tt)
        h_p = out_proj(h_p, ret_y.reshape(m_p, RET_W), nsa_y.reshape(m_p, NSA_QW), w_out_b)
        rp.append(ret_state)
        cp.append(rows_first(kvt[:, 0:KV_W], b_p))
        sp.append(rows_first(kvt[:, KV_W:2 * KV_W], b_p))
        wp.append(rows_first(kvt[:, 2 * KV_W:, t_p - w_buf:], b_p))

        zs = rms_matmul(h_s, norm_mix[i], w_in_b)
        zst = zs.T
        ret_yst, state_s = retention_step(zst, state_t, i, past, ret_gain[i])
        page_idx = page_table + i * n_pool
        kc_s, vc_s = compress(cmp_cache, page_idx, pe_t, cw1, cw2)
        nsa_ys, win_s = nsa_step(zs.reshape(b_s, 1, N_IN_PAD), kc_s, vc_s, sel_cache, win_cache, page_idx, i * b_s,
                                 bias_cs, bias_ss, bias_ws, past)
        h_s = out_proj(h_s, ret_yst.T, nsa_ys.reshape(b_s, NSA_QW), w_out_b)
        rs.append(jnp.moveaxis(state_s, -1, 0))
        ws.append(rows_first(win_s, b_s))
        cs.append(rows_first(zst[None, COL_KVC:COL_KVC + KV_W], 1).reshape((b_s, 1) + row))
        ss.append(rows_first(zst[None, COL_KVS:COL_KVS + KV_W], 1).reshape((b_s, 1) + row))

        j = i // 2
        if i % 2 == 0:
            w1, w3, w2 = ffn_w1[j].astype(BF16), ffn_w3[j].astype(BF16), ffn_w2[j].astype(BF16)
            h_p = _dense_ffn(h_p, norm_ffn[i], w1, w3, w2)
            h_s = _dense_ffn(h_s, norm_ffn[i], w1, w3, w2)
        else:
            h_all = jnp.concatenate([h_p, h_s], axis=0)
            router_w = jnp.pad(router[j], ((0, 0), (0, LANES - N_EXPERTS)))
            logits = jnp.concatenate([router_logits(h_p, norm_ffn[i], router_w),
                                      router_logits(h_s, norm_ffn[i], router_w)], axis=0)
            tm = 1024
            pos, row_token, row_gate, tile_expert, tile_valid = _route(logits[:, :N_EXPERTS], tm)
            y_rows = grouped_ffn(h_all[row_token], norm_ffn[i], row_gate, tile_expert, tile_valid,
                                 moe_w1[j], moe_w3[j], moe_w2[j], tm=tm, tf=moe_w1.shape[3] // 7, residual=False)
            h_all = h_all + y_rows[pos[:, 0]] + y_rows[pos[:, 1]]
            h_p, h_s = h_all[:m_p], h_all[m_p:]

        wg, wpj = ple_gate[i].astype(BF16), ple_proj[i].astype(BF16)
        h_p = ple(h_p, ple_norm[i], wg, p_prompt[i].reshape(m_p, P_DIM), wpj, norm_final, i == depth - 1)
        h_s = ple(h_s, ple_norm[i], wg, p_sample[i].reshape(b_s, P_DIM), wpj, norm_final, i == depth - 1)

    return (h_p.reshape(b_p, t_p, d), h_s.reshape(b_s, 1, d), jnp.stack(rp), jnp.stack(rs), jnp.stack(wp),
            jnp.stack(ws), jnp.stack(cp), jnp.stack(cs), jnp.stack(sp), jnp.stack(ss))
```

```python
import functools
import math

import numpy as np
import jax
import jax.numpy as jnp
from jax import lax
from jax.experimental import pallas as pl
from jax.experimental.pallas import tpu as pltpu

F32 = jnp.float32
BF16 = jnp.bfloat16
I32 = jnp.int32

LANES = 128
SUBLANES = 8
VMEM_LIMIT_BYTES = 56 * 1024 * 1024

D_MODEL = 1024
HEAD_DIM = 64
H_RET = 8
H_NSA = 8
N_KV = 2
R_GRP = H_NSA // N_KV
RET_W = H_RET * HEAD_DIM
NSA_QW = H_NSA * HEAD_DIM
KV_W = 2 * N_KV * HEAD_DIM
N_IN = 4 * RET_W + NSA_QW + 3 * KV_W + 3 * H_NSA
N_IN_PAD = 3584
COL_RQ, COL_RK, COL_RV, COL_RG = 0, RET_W, 2 * RET_W, 3 * RET_W
COL_NQ = 4 * RET_W
COL_KVC = COL_NQ + NSA_QW
COL_KVS = COL_KVC + KV_W
COL_KVW = COL_KVS + KV_W
COL_NG = COL_KVW + KV_W
RET_CHUNK = 128
CMP_STRIDE = 16
L_CMP = 2 * CMP_STRIDE
CMP_HIDDEN = 2 * HEAD_DIM
SEL_BLOCK = 64
N_SELECT = 16
WINDOW = 512
Q_BLOCK = 128
N_BUCKETS = 32
MAX_DISTANCE = 128
N_EXPERTS = 8
TOP_K = 2
P_DIM = 256
PAGE_SIZE = 128
ROPE_BASE = 10000.0
EPS = 1e-6
FORCED_SCORE = 1e4
NEG = -1e30
TINY = float(np.finfo(np.float32).tiny)
SEL_LANES = 64
FAR_REL = 2
KV_CHUNK = 128


def _cparams(n_grid):
    return pltpu.CompilerParams(dimension_semantics=("arbitrary",) * n_grid,
                                vmem_limit_bytes=VMEM_LIMIT_BYTES)


def _dot(a, b):
    return jnp.dot(a, b, preferred_element_type=F32)


def _dot_nt(a, b):
    return lax.dot_general(a, b, (((1,), (1,)), ((), ())), preferred_element_type=F32)


def _split3(x):
    hi = x.astype(BF16)
    r = x - hi.astype(F32)
    mid = r.astype(BF16)
    lo = (r - mid.astype(F32)).astype(BF16)
    return hi, mid, lo


def _dot3(x, b01):
    hi, mid, lo = _split3(x)
    return _dot(hi, b01) + _dot(mid, b01) + _dot(lo, b01)


def _dot6(x, w):
    xh, xm, xl = _split3(x)
    wh, wm, wl = _split3(w)
    return (_dot(xh, wh) + _dot(xh, wm) + _dot(xm, wh)
            + _dot(xm, wm) + _dot(xh, wl) + _dot(xl, wh))


def _sigmoid(x):
    return 1.0 / (1.0 + jnp.exp(-x))


def _rms(x, g):
    return x * lax.rsqrt(jnp.mean(x * x, axis=-1, keepdims=True) + EPS) * g


def _row_tile(m):
    for t in (512, 256, 128, 64, 32, 16, 8):
        if m % t == 0:
            return t
    raise ValueError(f"row count {m} not a multiple of 8")


def _rms_matmul_kernel(x_ref, g_ref, w_ref, o_ref):
    o_ref[...] = _dot(_rms(x_ref[...], g_ref[...]).astype(BF16), w_ref[...])


def rms_matmul(x, g, w):
    m, d = x.shape
    n = w.shape[1]
    tm = min(_row_tile(m), 256)
    return pl.pallas_call(
        _rms_matmul_kernel,
        grid=(m // tm,),
        in_specs=[pl.BlockSpec((tm, d), lambda i: (i, 0)),
                  pl.BlockSpec((1, d), lambda i: (0, 0)),
                  pl.BlockSpec((d, n), lambda i: (0, 0))],
        out_specs=pl.BlockSpec((tm, n), lambda i: (i, 0)),
        out_shape=jax.ShapeDtypeStruct((m, n), F32),
        compiler_params=_cparams(1),
    )(x, g.reshape(1, d), w)


def _rms_matmul_kvt_kernel(x_ref, g_ref, w_ref, wt_ref, o_ref, ot_ref):
    xn = _rms(x_ref[...], g_ref[...]).astype(BF16)
    o_ref[...] = _dot(xn, w_ref[...])
    ot_ref[...] = _dot_nt(wt_ref[...], xn)


def rms_matmul_kvt(x, g, w, wt, bsz):
    m, d = x.shape
    n = w.shape[1]
    nt = wt.shape[0]
    t = m // bsz
    tm = min(_row_tile(t), 256)
    per = t // tm
    return pl.pallas_call(
        _rms_matmul_kvt_kernel,
        grid=(m // tm,),
        in_specs=[pl.BlockSpec((tm, d), lambda i: (i, 0)),
                  pl.BlockSpec((1, d), lambda i: (0, 0)),
                  pl.BlockSpec((d, n), lambda i: (0, 0)),
                  pl.BlockSpec((nt, d), lambda i: (0, 0))],
        out_specs=[pl.BlockSpec((tm, n), lambda i: (i, 0)),
                   pl.BlockSpec((None, nt, tm), lambda i: (i // per, 0, i % per))],
        out_shape=[jax.ShapeDtypeStruct((m, n), F32), jax.ShapeDtypeStruct((bsz, nt, t), F32)],
        compiler_params=_cparams(1),
        name="rms_matmul_kvt",
    )(x, g.reshape(1, d), w, wt)


def _router_kernel(x_ref, g_ref, w_ref, o_ref, xn_ref):
    xn = _rms(x_ref[...], g_ref[...])
    o_ref[...] = _dot6(xn, w_ref[...])
    xn_ref[...] = xn.astype(BF16)


def router_logits(x, g, w):
    m, d = x.shape
    n = w.shape[1]
    tm = min(_row_tile(m), 256)
    return pl.pallas_call(
        _router_kernel,
        grid=(m // tm,),
        in_specs=[pl.BlockSpec((tm, d), lambda i: (i, 0)),
                  pl.BlockSpec((1, d), lambda i: (0, 0)),
                  pl.BlockSpec((d, n), lambda i: (0, 0))],
        out_specs=[pl.BlockSpec((tm, n), lambda i: (i, 0)), pl.BlockSpec((tm, d), lambda i: (i, 0))],
        out_shape=[jax.ShapeDtypeStruct((m, n), F32), jax.ShapeDtypeStruct((m, d), BF16)],
        compiler_params=_cparams(1),
        name="router",
    )(x, g.reshape(1, d), w)


def _outproj_kernel(h_ref, a_ref, b_ref, w_ref, o_ref):
    ka = a_ref.shape[1]
    o_ref[...] = (h_ref[...] + _dot(a_ref[...].astype(BF16), w_ref[0:ka, :])
                  + _dot(b_ref[...].astype(BF16), w_ref[ka:, :]))


def out_proj(h, a, b, w):
    m, d = h.shape
    ka, kb = a.shape[1], b.shape[1]
    tm = _row_tile(m)
    return pl.pallas_call(
        _outproj_kernel,
        grid=(m // tm,),
        in_specs=[pl.BlockSpec((tm, d), lambda i: (i, 0)),
                  pl.BlockSpec((tm, ka), lambda i: (i, 0)),
                  pl.BlockSpec((tm, kb), lambda i: (i, 0)),
                  pl.BlockSpec((ka + kb, d), lambda i: (0, 0))],
        out_specs=pl.BlockSpec((tm, d), lambda i: (i, 0)),
        out_shape=jax.ShapeDtypeStruct((m, d), F32),
        compiler_params=_cparams(1),
        name="out_proj",
    )(h, a, b, w)


def _ffn_kernel(te_ref, tv_ref, x_ref, g_ref, gate_ref, w1_ref, w3_ref, w2_ref, o_ref,
                xn_ref, acc_ref, *, residual):
    i = pl.program_id(0)
    j = pl.program_id(1)

    @pl.when(j == 0)
    def _():
        xn_ref[...] = x_ref[...] if x_ref.dtype == BF16 else _rms(x_ref[...], g_ref[...]).astype(BF16)
        acc_ref[...] = jnp.zeros_like(acc_ref)

    @pl.when(tv_ref[i] != 0)
    def _():
        xn = xn_ref[...]
        a = _dot(xn, w1_ref[...].astype(BF16))
        b = _dot(xn, w3_ref[...].astype(BF16))
        hid = (a * _sigmoid(a)) * b
        acc_ref[...] += _dot(hid.astype(BF16), w2_ref[...].astype(BF16))

    @pl.when(j == pl.num_programs(1) - 1)
    def _():
        y = acc_ref[...] * gate_ref[...]
        if residual:
            y = y + x_ref[...]
        o_ref[...] = y


def grouped_ffn(x, g, row_gate, tile_expert, tile_valid, w1, w3, w2, *, tm, tf, residual):
    m, d = x.shape
    f = w1.shape[2]
    assert not (residual and x.dtype == BF16)
    grid_spec = pltpu.PrefetchScalarGridSpec(
        num_scalar_prefetch=2,
        grid=(m // tm, f // tf),
        in_specs=[pl.BlockSpec((tm, d), lambda i, j, te, tv: (i, 0)),
                  pl.BlockSpec((1, d), lambda i, j, te, tv: (0, 0)),
                  pl.BlockSpec((tm, 1), lambda i, j, te, tv: (i, 0)),
                  pl.BlockSpec((None, d, tf), lambda i, j, te, tv: (te[i], 0, j)),
                  pl.BlockSpec((None, d, tf), lambda i, j, te, tv: (te[i], 0, j)),
                  pl.BlockSpec((None, tf, d), lambda i, j, te, tv: (te[i], j, 0))],
        out_specs=pl.BlockSpec((tm, d), lambda i, j, te, tv: (i, 0)),
        scratch_shapes=[pltpu.VMEM((tm, d), BF16), pltpu.VMEM((tm, d), F32)],
    )
    return pl.pallas_call(
        functools.partial(_ffn_kernel, residual=residual),
        grid_spec=grid_spec,
        out_shape=jax.ShapeDtypeStruct((m, d), F32),
        compiler_params=_cparams(2),
        name="grouped_ffn",
    )(tile_expert, tile_valid, x, g.reshape(1, d), row_gate, w1, w3, w2)


def _ple_kernel(h_ref, g_ref, wg_ref, p_ref, wp_ref, gf_ref, o_ref, *, final):
    h = h_ref[...]
    gate = _sigmoid(_dot(_rms(h, g_ref[...]).astype(BF16), wg_ref[...]))
    proj = _dot(p_ref[...].astype(BF16), wp_ref[...])
    h2 = h + gate * proj
    o_ref[...] = _rms(h2, gf_ref[...]) if final else h2


def ple(h, g, wg, p, wp, gf, final):
    m, d = h.shape
    pd = p.shape[1]
    tm = _row_tile(m)
    row = lambda i: (i, 0)
    fixed = lambda i: (0, 0)
    return pl.pallas_call(
        functools.partial(_ple_kernel, final=final),
        grid=(m // tm,),
        in_specs=[pl.BlockSpec((tm, d), row), pl.BlockSpec((1, d), fixed),
                  pl.BlockSpec((d, d), fixed), pl.BlockSpec((tm, pd), row),
                  pl.BlockSpec((pd, d), fixed), pl.BlockSpec((1, d), fixed)],
        out_specs=pl.BlockSpec((tm, d), row),
        out_shape=jax.ShapeDtypeStruct((m, d), F32),
        compiler_params=_cparams(1),
        name="ple",
    )(h, g.reshape(1, d), wg, p, wp, gf.reshape(1, d))


def _bias_kernel(tab_ref, d_ref, o_ref):
    n = jnp.maximum(d_ref[...], 0)
    max_exact = N_BUCKETS // 2
    nf = jnp.maximum(n, 1).astype(F32)
    large = max_exact + (jnp.log(nf / max_exact) / math.log(MAX_DISTANCE / max_exact)
                         * (N_BUCKETS - max_exact)).astype(I32)
    large = jnp.minimum(large, N_BUCKETS - 1)
    bucket = jnp.where(n < max_exact, n, large)
    for h in range(H_NSA):
        acc = jnp.zeros(n.shape, F32)
        for k in range(N_BUCKETS):
            acc = jnp.where(bucket == k, tab_ref[k, h], acc)
        o_ref[h] = acc


def bias_lookup(table, dist):
    r, c = dist.shape
    tr = min(r, 128)
    tc = 512 if c % 512 == 0 else c
    return pl.pallas_call(
        _bias_kernel,
        grid=(r // tr, c // tc),
        in_specs=[pl.BlockSpec(memory_space=pltpu.SMEM),
                  pl.BlockSpec((tr, tc), lambda i, j: (i, j))],
        out_specs=pl.BlockSpec((H_NSA, tr, tc), lambda i, j: (0, i, j)),
        out_shape=jax.ShapeDtypeStruct((H_NSA, r, c), F32),
        compiler_params=_cparams(2),
        name="bias_lookup",
    )(table, dist)


def _swap_halves(x):
    lane = lax.broadcasted_iota(I32, x.shape, 1)
    return jnp.where(lane % HEAD_DIM < HEAD_DIM // 2,
                     pltpu.roll(x, LANES - HEAD_DIM // 2, 1), pltpu.roll(x, HEAD_DIM // 2, 1))


def _retention_prompt_kernel(q_ref, k_ref, v_ref, g_ref, cos_ref, sin_ref, gain_ref, intra_ref,
                             qdec_ref, kdec_ref, cdec_ref, y_ref, s_out_ref, s_ref):
    c = pl.program_id(1)

    @pl.when(c == 0)
    def _():
        s_ref[...] = jnp.zeros_like(s_ref)

    cos = cos_ref[...]
    sin = sin_ref[...]
    lane = lax.broadcasted_iota(I32, (RET_CHUNK, LANES), 1)
    left = lane < HEAD_DIM
    row = lax.broadcasted_iota(I32, (LANES, LANES), 0)
    col = lax.broadcasted_iota(I32, (LANES, LANES), 1)
    same_head = (row < HEAD_DIM) == (col < HEAD_DIM)
    ones_bd = jnp.where(same_head, 1.0, 0.0).astype(BF16)
    for p in range(H_RET // 2):
        sl = slice(p * LANES, (p + 1) * LANES)
        q = q_ref[:, sl]
        k = k_ref[:, sl]
        v = v_ref[:, sl].astype(BF16)
        qr = q * cos + _swap_halves(q) * sin
        kr = (k * cos + _swap_halves(k) * sin) * (HEAD_DIM ** -0.5)
        qb = qr.astype(BF16)
        q2 = jnp.concatenate([jnp.where(left, qr, 0.0), jnp.where(left, 0.0, qr)], axis=0).astype(BF16)
        inner = _dot_nt(q2, kr.astype(BF16)) * intra_ref[p]
        o2 = _dot(inner.astype(BF16), v)
        s_old = s_ref[p]
        cross = _dot(qb, s_old.astype(BF16)) * qdec_ref[:, sl]
        o = jnp.where(left, o2[:RET_CHUNK], o2[RET_CHUNK:]) + cross
        kd = (kr * kdec_ref[:, sl]).T.astype(BF16)
        s_ref[p] = s_old * cdec_ref[p] + jnp.where(same_head, _dot(kd, v), 0.0)
        ms = _dot3(o * o, ones_bd) * (1.0 / HEAD_DIM)
        gate = g_ref[:, sl]
        y = o * lax.rsqrt(ms + EPS) * gain_ref[:, sl] * (gate * _sigmoid(gate))
        y_ref[:, sl] = y.astype(y_ref.dtype)

    @pl.when(c == pl.num_programs(1) - 1)
    def _():
        for p in range(H_RET // 2):
            s = s_ref[p]
            s_out_ref[2 * p] = s[:HEAD_DIM, :HEAD_DIM]
            s_out_ref[2 * p + 1] = s[HEAD_DIM:, HEAD_DIM:]


def _retention_consts(c):
    log_g = jnp.log1p(-jnp.exp2(-5.0 - jnp.arange(H_RET, dtype=F32)))
    j = jnp.arange(c, dtype=F32)
    diff = j[:, None] - j[None, :]
    intra = jnp.where(diff >= 0, jnp.exp(jnp.maximum(diff, 0.0)[None] * log_g[:, None, None]), 0.0)
    q_decay = jnp.exp((j[:, None] + 1.0) * log_g[None, :])
    k_decay = jnp.exp((c - 1.0 - j[:, None]) * log_g[None, :])
    c_decay = jnp.exp(c * log_g)
    intra2 = intra.reshape(H_RET // 2, 2 * c, c)
    qdec = jnp.repeat(q_decay, HEAD_DIM, axis=1)
    kdec = jnp.repeat(k_decay, HEAD_DIM, axis=1)
    cdec = jnp.repeat(c_decay.reshape(H_RET // 2, 2), HEAD_DIM, axis=1)[:, :, None] * jnp.ones((1, 1, LANES), F32)
    return intra2, qdec, kdec, cdec


def _rotary_tables(pos):
    half = HEAD_DIM // 2
    inv = ROPE_BASE ** (-jnp.arange(half, dtype=F32) / half)
    ang = pos.astype(F32)[:, None] * inv[None, :]
    cos, sin = jnp.cos(ang), jnp.sin(ang)
    cos_t = jnp.concatenate([cos, cos, cos, cos], axis=1)
    sin_t = jnp.concatenate([-sin, sin, -sin, sin], axis=1)
    return cos_t, sin_t


def retention_prompt(z3, gain):
    b, t, _ = z3.shape
    c = RET_CHUNK
    cos_t, sin_t = _rotary_tables(jnp.arange(t))
    intra2, qdec, kdec, cdec = _retention_consts(c)
    blk = lambda col: pl.BlockSpec((None, c, RET_W), lambda bi, ci, col=col: (bi, ci, col))
    fixed2 = lambda shape: pl.BlockSpec(shape, lambda bi, ci: (0,) * len(shape))
    return pl.pallas_call(
        _retention_prompt_kernel,
        grid=(b, t // c),
        in_specs=[blk(COL_RQ // RET_W), blk(COL_RK // RET_W), blk(COL_RV // RET_W), blk(COL_RG // RET_W),
                  pl.BlockSpec((c, LANES), lambda bi, ci: (ci, 0)),
                  pl.BlockSpec((c, LANES), lambda bi, ci: (ci, 0)),
                  fixed2((1, RET_W)), fixed2((H_RET // 2, 2 * c, c)), fixed2((c, RET_W)),
                  fixed2((c, RET_W)), fixed2((H_RET // 2, LANES, LANES))],
        out_specs=[pl.BlockSpec((None, c, RET_W), lambda bi, ci: (bi, ci, 0)),
                   pl.BlockSpec((None, H_RET, HEAD_DIM, HEAD_DIM), lambda bi, ci: (bi, 0, 0, 0))],
        out_shape=[jax.ShapeDtypeStruct((b, t, RET_W), BF16),
                   jax.ShapeDtypeStruct((b, H_RET, HEAD_DIM, HEAD_DIM), F32)],
        scratch_shapes=[pltpu.VMEM((H_RET // 2, LANES, LANES), F32)],
        compiler_params=_cparams(2),
        name="retention_prompt",
    )(z3, z3, z3, z3, cos_t, sin_t, gain.reshape(1, RET_W), intra2, qdec, kdec, cdec)


def _retention_step_kernel(q_ref, k_ref, v_ref, g_ref, s_ref, cos_ref, sin_ref, gamma_ref, gain_ref,
                           y_ref, s_out_ref):
    half = HEAD_DIM // 2
    cos = cos_ref[...]
    sin = sin_ref[...]

    def rot(x):
        x1, x2 = x[:half], x[half:]
        return jnp.concatenate([x1 * cos - x2 * sin, x1 * sin + x2 * cos], axis=0)

    qr = rot(q_ref[...])
    kr = rot(k_ref[...]) * (HEAD_DIM ** -0.5)
    v = v_ref[...]
    gamma = gamma_ref[...]
    o = jnp.zeros_like(v)
    for d in range(HEAD_DIM):
        s_new = s_ref[d] * gamma + kr[d:d + 1] * v
        s_out_ref[d] = s_new
        o = o + qr[d:d + 1] * s_new
    ms = jnp.mean(o * o, axis=0, keepdims=True)
    gate = g_ref[...]
    y_ref[...] = o * lax.rsqrt(ms + EPS) * gain_ref[...] * (gate * _sigmoid(gate))


def retention_step(zt, state, layer, pos, gain):
    bs = zt.shape[1]
    half = HEAD_DIM // 2
    inv = ROPE_BASE ** (-jnp.arange(half, dtype=F32) / half)
    ang = jnp.full((1,), pos).astype(F32)[:, None] * inv[None, :]
    cos = jnp.broadcast_to(jnp.cos(ang).reshape(half, 1), (half, bs))
    sin = jnp.broadcast_to(jnp.sin(ang).reshape(half, 1), (half, bs))
    gamma = 1.0 - jnp.exp2(-5.0 - jnp.arange(H_RET, dtype=F32))
    gamma = jnp.broadcast_to(gamma.reshape(H_RET, 1, 1), (H_RET, 1, bs))
    gain_b = jnp.broadcast_to(gain.reshape(H_RET, HEAD_DIM, 1), (H_RET, HEAD_DIM, bs))
    head = lambda col: pl.BlockSpec((HEAD_DIM, bs), lambda h, col=col: (col // HEAD_DIM + h, 0))
    fixed = pl.BlockSpec((half, bs), lambda h: (0, 0))
    return pl.pallas_call(
        _retention_step_kernel,
        grid=(H_RET,),
        in_specs=[head(COL_RQ), head(COL_RK), head(COL_RV), head(COL_RG),
                  pl.BlockSpec((None, HEAD_DIM, HEAD_DIM, bs), lambda h: (layer * H_RET + h, 0, 0, 0)),
                  fixed, fixed,
                  pl.BlockSpec((None, 1, bs), lambda h: (h, 0, 0)),
                  pl.BlockSpec((None, HEAD_DIM, bs), lambda h: (h, 0, 0))],
        out_specs=[pl.BlockSpec((HEAD_DIM, bs), lambda h: (h, 0)),
                   pl.BlockSpec((None, HEAD_DIM, HEAD_DIM, bs), lambda h: (h, 0, 0, 0))],
        out_shape=[jax.ShapeDtypeStruct((RET_W, bs), F32),
                   jax.ShapeDtypeStruct((H_RET, HEAD_DIM, HEAD_DIM, bs), F32)],
        compiler_params=_cparams(1),
        name="retention_step",
    )(zt, zt, zt, zt, state, cos, sin, gamma, gain_b)


def _gelu_tanh(x):
    return 0.5 * x * (1.0 + jnp.tanh(math.sqrt(2.0 / math.pi) * (x + 0.044715 * (x * x * x))))


def _compress_kernel(*refs, n_pieces):
    x_refs = refs[1:1 + n_pieces]
    pe_ref, w1_ref, w2_ref, kc_ref, vc_ref, rows_ref = refs[1 + n_pieces:]
    width = x_refs[0].shape[1]
    n = n_pieces * width // CMP_STRIDE
    for kv, o_ref in enumerate((kc_ref, vc_ref)):
        for p, x_ref in enumerate(x_refs):
            for c in range(width // LANES):
                rows_ref[pl.ds(p * width + c * LANES, LANES), :] = (
                    x_ref[kv * LANES:(kv + 1) * LANES, c * LANES:(c + 1) * LANES].T)
        xk = jnp.concatenate([rows_ref[pl.ds(j, n, stride=CMP_STRIDE), :] for j in range(CMP_STRIDE)], axis=1)
        top = _dot((xk + pe_ref[kv, 0]).astype(BF16), w1_ref[kv, 0])
        bot = _dot((xk + pe_ref[kv, 1]).astype(BF16), w1_ref[kv, 1])
        pre = top + pltpu.roll(bot, n - 1, 0)
        hid = _gelu_tanh(pre).astype(BF16)
        o_ref[...] = jnp.concatenate(
            [_dot(hid[:, g * CMP_HIDDEN:(g + 1) * CMP_HIDDEN], w2_ref[kv]) for g in range(N_KV)], axis=1)


def _compress_weights(pe, w1, w2):
    pe_r = pe.reshape(2, 2, CMP_STRIDE, 1, HEAD_DIM)
    pe_t = jnp.broadcast_to(pe_r, (2, 2, CMP_STRIDE, N_KV, HEAD_DIM)).reshape(2, 2, 1, CMP_STRIDE * LANES)
    w1r = w1.reshape(2, 2, CMP_STRIDE, HEAD_DIM, CMP_HIDDEN)
    eye = jnp.eye(N_KV, dtype=w1.dtype)
    wbig = jnp.einsum('khjdc,gG->khjgdGc', w1r, eye)
    wbig = wbig.reshape(2, 2, CMP_STRIDE * LANES, N_KV * CMP_HIDDEN)
    return pe_t.astype(F32), wbig.astype(BF16), w2.astype(BF16)


def compress(src, idx, pe_t, w1b, w2b):
    bsz, n_pieces = idx.shape
    width = src.shape[2]
    n = n_pieces * width // CMP_STRIDE
    piece = lambda p: pl.BlockSpec((None, KV_W, width), lambda b, ix, p=p: (ix[b, p], 0, 0))
    fixed = lambda shape: pl.BlockSpec(shape, lambda b, ix: (0,) * len(shape))
    out = pl.BlockSpec((None, n, LANES), lambda b, ix: (b, 0, 0))
    grid_spec = pltpu.PrefetchScalarGridSpec(
        num_scalar_prefetch=1,
        grid=(bsz,),
        in_specs=[piece(p) for p in range(n_pieces)]
        + [fixed(pe_t.shape), fixed(w1b.shape), fixed(w2b.shape)],
        out_specs=[out, out],
        scratch_shapes=[pltpu.VMEM((n_pieces * width, LANES), F32)],
    )
    return pl.pallas_call(
        functools.partial(_compress_kernel, n_pieces=n_pieces),
        grid_spec=grid_spec,
        out_shape=[jax.ShapeDtypeStruct((bsz, n, LANES), F32)] * 2,
        compiler_params=_cparams(1),
        name="compress",
    )(idx, *([src] * n_pieces), pe_t, w1b, w2b)


def _masked_softmax(s, mask):
    s = jnp.where(mask, s, NEG)
    m = jnp.max(s, axis=-1, keepdims=True)
    e = jnp.where(mask, jnp.exp(s - m), 0.0)
    return e / jnp.maximum(jnp.sum(e, axis=-1, keepdims=True), TINY)


def _block_scores(imp, qpos, n_sel, axis=1):
    blk = lax.broadcasted_iota(I32, imp.shape, axis)
    cur = qpos // SEL_BLOCK
    forced = (blk == 0) | (blk == cur) | (blk == cur - 1)
    valid = blk * SEL_BLOCK <= qpos
    score = jnp.where(forced, FORCED_SCORE, jnp.where(valid, imp, -1.0))
    return jnp.where(blk < n_sel, score, -2.0)


def _select_top(score, n_top, axis=1):
    blk = lax.broadcasted_iota(I32, score.shape, axis)
    rank = jnp.zeros(score.shape, F32)
    for i in range(score.shape[axis]):
        one = score[:, i:i + 1] if axis == 1 else score[i:i + 1, :]
        beats = (one > score) | ((one == score) & (blk > i))
        rank = rank + jnp.where(beats, 1.0, 0.0)
    return jnp.where(rank < n_top, 1.0, 0.0)


def _attend_t(s, ok, m_prev, l_prev):
    s = jnp.where(ok, s, NEG)
    m_new = jnp.maximum(m_prev, jnp.max(s, axis=0, keepdims=True))
    alpha = jnp.exp(m_prev - m_new)
    p = jnp.exp(s - m_new)
    l_new = alpha * l_prev + jnp.sum(p, axis=0, keepdims=True)
    return m_new, l_new, alpha, p


def _nsa_prompt_kernel(q_ref, gate_ref, kc_ref, vct_ref, ks_ref, vst_ref, kw_ref, vwt_ref, bc_ref, bt_ref,
                       o_ref, qt_ref, pcs_ref, pick_ref, acc_ref, os_ref, s_ref, p_ref, *, n_cmp, n_sel):
    i = pl.program_id(1)
    qs = i * Q_BLOCK
    key_i = lax.broadcasted_iota(I32, (KV_CHUNK, Q_BLOCK), 0)
    tok_i = lax.broadcasted_iota(I32, (KV_CHUNK, Q_BLOCK), 1)
    row8 = lax.broadcasted_iota(I32, (H_NSA, Q_BLOCK), 0)
    zeros_half = jnp.zeros((HEAD_DIM, Q_BLOCK), F32)

    for k2 in range(H_NSA // 2):
        slab_t = q_ref[:, k2 * LANES:(k2 + 1) * LANES].T * (HEAD_DIM ** -0.5)
        for h in (2 * k2, 2 * k2 + 1):
            qh = slab_t[(h % 2) * HEAD_DIM:(h % 2 + 1) * HEAD_DIM]
            both = [qh, zeros_half] if h // R_GRP == 0 else [zeros_half, qh]
            qt_ref[:, h * Q_BLOCK:(h + 1) * Q_BLOCK] = jnp.concatenate(both, axis=0).astype(BF16)

    sig_t = _sigmoid(gate_ref[...]).T

    ncp = kc_ref.shape[0]
    kc = kc_ref[...].astype(BF16)
    n_c = lax.broadcasted_iota(I32, (ncp, Q_BLOCK), 0)
    t_c = lax.broadcasted_iota(I32, (ncp, Q_BLOCK), 1)
    ok_c = (n_c * CMP_STRIDE + (L_CMP - 1) <= qs + t_c) & (n_c < n_cmp)
    blk_pos = qs + lax.broadcasted_iota(I32, (pick_ref.shape[1], Q_BLOCK), 1)
    o_c = []
    for g in range(N_KV):
        vct = vct_ref[g * HEAD_DIM:(g + 1) * HEAD_DIM, :].astype(BF16)
        pcs = jnp.zeros((ncp, Q_BLOCK), F32)
        for r in range(R_GRP):
            h = g * R_GRP + r
            s = _dot(kc, qt_ref[:, h * Q_BLOCK:(h + 1) * Q_BLOCK]) + bc_ref[h]
            s = jnp.where(ok_c, s, NEG)
            e = jnp.where(ok_c, jnp.exp(s - jnp.max(s, axis=0, keepdims=True)), 0.0)
            p = e * (1.0 / jnp.maximum(jnp.sum(e, axis=0, keepdims=True), TINY))
            o_c.append(_dot(vct, p.astype(BF16)))
            pcs = pcs + p
        pcs_ref[...] = pcs
        per = SEL_BLOCK // CMP_STRIDE
        imp = pcs_ref[pl.ds(0, ncp // per, stride=per), :]
        for c in range(1, per):
            imp = imp + pcs_ref[pl.ds(c, ncp // per, stride=per), :]
        pick_ref[g] = _select_top(_block_scores(imp, blk_pos, n_sel, axis=0), min(N_SELECT, n_sel), axis=0)

    dmat = tok_i - key_i

    def stream(k_ref, vt_ref, n_chunks, chunk_of, mask_of):
        def scores(c, slot):
            off, _ = chunk_of(c)
            k = k_ref[pl.ds(off, KV_CHUNK), :].astype(BF16)
            for h in range(H_NSA):
                s_ref[slot, h] = _dot(k, qt_ref[:, h * Q_BLOCK:(h + 1) * Q_BLOCK])

        def values(c, slot, alpha_all):
            off, _ = chunk_of(c)
            for g in range(N_KV):
                vt = vt_ref[g * HEAD_DIM:(g + 1) * HEAD_DIM, pl.ds(off, KV_CHUNK)].astype(BF16)
                for h in range(g * R_GRP, (g + 1) * R_GRP):
                    acc_ref[h] = acc_ref[h] * alpha_all[h:h + 1] + _dot(vt, p_ref[slot, h])

        def softmax(c, slot, m_all, l_all):
            _, rel = chunk_of(c)
            ok_g = mask_of(c, rel)
            bias_idx = min(max(rel, 0), FAR_REL) if isinstance(rel, int) else jnp.clip(rel, 0, FAR_REL)
            alpha_all = jnp.zeros((H_NSA, Q_BLOCK), F32)
            for h in range(H_NSA):
                s = s_ref[slot, h] + bt_ref[bias_idx, h]
                m_new, l_new, alpha, p = _attend_t(s, ok_g[h // R_GRP], m_all[h:h + 1], l_all[h:h + 1])
                p_ref[slot, h] = p.astype(BF16)
                m_all = jnp.where(row8 == h, m_new, m_all)
                l_all = jnp.where(row8 == h, l_new, l_all)
                alpha_all = jnp.where(row8 == h, alpha, alpha_all)
            return m_all, l_all, alpha_all

        def stage(c, slot, carry, prefetch=True):
            m_all, l_all, alpha_prev = carry
            if prefetch:
                scores(c + 1, 1 - slot)
            values(c - 1, 1 - slot, alpha_prev)
            return softmax(c, slot, m_all, l_all)

        acc_ref[...] = jnp.zeros_like(acc_ref)
        p_ref[1] = jnp.zeros(p_ref.shape[1:], BF16)
        scores(0, 0)
        carry = (jnp.full((H_NSA, Q_BLOCK), NEG, F32), jnp.zeros((H_NSA, Q_BLOCK), F32),
                 jnp.ones((H_NSA, Q_BLOCK), F32))
        if isinstance(n_chunks, int):
            for c in range(n_chunks):
                carry = stage(c, c % 2, carry, prefetch=c + 1 < n_chunks)
            last = n_chunks - 1
        else:
            pairs = (n_chunks + 1) // 2
            carry = lax.fori_loop(0, pairs, lambda t, cr: stage(2 * t + 1, 1, stage(2 * t, 0, cr)), carry)
            last = 2 * pairs - 1
        m_all, l_all, alpha_last = carry
        values(last, last % 2 if isinstance(last, int) else 1, alpha_last)
        return m_all, l_all

    def sel_chunk(c):
        return pl.multiple_of(jnp.clip(c, 0, i) * KV_CHUNK, KV_CHUNK), i - c

    def sel_mask(c, rel):
        causal = rel * KV_CHUNK + dmat >= 0
        per = KV_CHUNK // SEL_BLOCK
        first = jnp.minimum(c, i) * per
        ok_g = []
        for g in range(N_KV):
            picked = jnp.concatenate(
                [jnp.broadcast_to(pick_ref[g, pl.ds(first + b, 1), :], (SEL_BLOCK, Q_BLOCK)) for b in range(per)],
                axis=0)
            ok_g.append(causal & (picked > 0.5))
        return ok_g

    m_s, l_s = stream(ks_ref, vst_ref, i + 1, sel_chunk, sel_mask)
    for h in range(H_NSA):
        os_ref[h] = jnp.where(m_s[h:h + 1] > 0.5 * NEG, acc_ref[h] / jnp.maximum(l_s[h:h + 1], TINY), 0.0)

    n_back = WINDOW // KV_CHUNK

    def win_chunk(c):
        return pl.multiple_of(jnp.clip(i - n_back + c, 0, i) * KV_CHUNK, KV_CHUNK), n_back - c

    def win_mask(c, rel):
        dist = rel * KV_CHUNK + dmat + jnp.where(i - n_back + c < 0, WINDOW, 0)
        ok = (dist >= 0) & (dist < WINDOW)
        return [ok, ok]

    m_w, l_w = stream(kw_ref, vwt_ref, n_back + 1, win_chunk, win_mask)

    for k2 in range(H_NSA // 2):
        pair = []
        for h in (2 * k2, 2 * k2 + 1):
            o_w = jnp.where(m_w[h:h + 1] > 0.5 * NEG, acc_ref[h] / jnp.maximum(l_w[h:h + 1], TINY), 0.0)
            pair.append(sig_t[3 * h:3 * h + 1] * o_c[h] + sig_t[3 * h + 1:3 * h + 2] * os_ref[h]
                        + sig_t[3 * h + 2:3 * h + 3] * o_w)
        o_ref[:, k2 * LANES:(k2 + 1) * LANES] = jnp.concatenate(pair, axis=0).T.astype(o_ref.dtype)


def _selection_consts(n_rows_cmp, n_keys):
    n = np.arange(n_rows_cmp)
    blk = np.arange(SEL_LANES)
    pool = (n[:, None] // (SEL_BLOCK // CMP_STRIDE) == blk[None, :]).astype(np.float32)
    key = np.arange(n_keys)
    expand = (blk[:, None] == key[None, :] // SEL_BLOCK).astype(np.float32)
    return jnp.asarray(pool, BF16), jnp.asarray(expand, BF16)


def nsa_prompt(z3, kvt, kc, vct, bias_ct, bias_tt):
    b, t, _ = z3.shape
    n_sub = t // CMP_STRIDE
    n_sel = -(-t // SEL_BLOCK)
    assert kc.shape[1] == n_sub and n_sub == n_sel * (SEL_BLOCK // CMP_STRIDE) and n_sel % SUBLANES == 0
    fixed = lambda shape: pl.BlockSpec(shape, lambda bi, i: (0,) * len(shape))
    v_rows = lambda branch: (branch * KV_W + N_KV * HEAD_DIM) // LANES
    return pl.pallas_call(
        functools.partial(_nsa_prompt_kernel, n_cmp=n_sub - 1, n_sel=n_sel),
        grid=(b, t // Q_BLOCK),
        in_specs=[pl.BlockSpec((None, Q_BLOCK, NSA_QW), lambda bi, i: (bi, i, COL_NQ // NSA_QW)),
                  pl.BlockSpec((None, Q_BLOCK, LANES), lambda bi, i: (bi, i, COL_NG // LANES)),
                  pl.BlockSpec((None, n_sub, LANES), lambda bi, i: (bi, 0, 0)),
                  pl.BlockSpec((None, LANES, n_sub), lambda bi, i: (bi, 0, 0)),
                  pl.BlockSpec((None, t, LANES), lambda bi, i: (bi, 0, COL_KVS // LANES)),
                  pl.BlockSpec((None, LANES, t), lambda bi, i: (bi, v_rows(1), 0)),
                  pl.BlockSpec((None, t, LANES), lambda bi, i: (bi, 0, COL_KVW // LANES)),
                  pl.BlockSpec((None, LANES, t), lambda bi, i: (bi, v_rows(2), 0)),
                  pl.BlockSpec((H_NSA, n_sub, Q_BLOCK), lambda bi, i: (0, 0, i)),
                  fixed(bias_tt.shape)],
        out_specs=pl.BlockSpec((None, Q_BLOCK, NSA_QW), lambda bi, i: (bi, i, 0)),
        out_shape=jax.ShapeDtypeStruct((b, t, NSA_QW), BF16),
        scratch_shapes=[pltpu.VMEM((LANES, H_NSA * Q_BLOCK), BF16),
                        pltpu.VMEM((n_sub, Q_BLOCK), F32),
                        pltpu.VMEM((N_KV, n_sel, Q_BLOCK), F32),
                        pltpu.VMEM((H_NSA, HEAD_DIM, Q_BLOCK), F32),
                        pltpu.VMEM((H_NSA, HEAD_DIM, Q_BLOCK), F32),
                        pltpu.VMEM((2, H_NSA, KV_CHUNK, Q_BLOCK), F32),
                        pltpu.VMEM((2, H_NSA, KV_CHUNK, Q_BLOCK), BF16)],
        compiler_params=_cparams(2),
        name="nsa_prompt",
    )(z3, z3, kc, vct, z3, kvt, z3, kvt, bias_ct, bias_tt)


def _nsa_step_kernel(pt_ref, z_ref, kc_ref, vc_ref, *rest, n_pages, past, wbuf, n_cmp, n_sel):
    del pt_ref
    pages = rest[:n_pages]
    win_ref, bcs_ref, bss_ref, bws_ref, pool_ref, exp_ref, o_ref, wo_ref = rest[n_pages:]
    z = z_ref[...]
    row8 = lax.broadcasted_iota(I32, (H_NSA, LANES), 0)
    lane8 = lax.broadcasted_iota(I32, (H_NSA, LANES), 1)
    left8 = lane8 < HEAD_DIM
    qbd = jnp.zeros((H_NSA, LANES), F32)
    for h in range(H_NSA):
        slab = jnp.broadcast_to(z[:, COL_NQ + (h // 2) * LANES:COL_NQ + (h // 2 + 1) * LANES], (H_NSA, LANES))
        g = h // R_GRP
        if h % 2 != g:
            slab = pltpu.roll(slab, HEAD_DIM, 1)
        keep = (row8 == h) & (left8 if g == 0 else jnp.logical_not(left8))
        qbd = jnp.where(keep, slab, qbd)
    qbd = (qbd * (HEAD_DIM ** -0.5)).astype(BF16)

    ncp = kc_ref.shape[0]
    n_c = lax.broadcasted_iota(I32, (H_NSA, ncp), 1)
    s_c = _dot_nt(qbd, kc_ref[...].astype(BF16)) + bcs_ref[...]
    p_c = _masked_softmax(s_c, (n_c * CMP_STRIDE + (L_CMP - 1) <= past) & (n_c < n_cmp))
    o_c = _dot(p_c.astype(BF16), vc_ref[...].astype(BF16))
    pc0 = jnp.sum(p_c[0:R_GRP], axis=0, keepdims=True)
    pc1 = jnp.sum(p_c[R_GRP:], axis=0, keepdims=True)
    rowc = lax.broadcasted_iota(I32, (H_NSA, ncp), 0)
    pcs = jnp.where(rowc < R_GRP, jnp.broadcast_to(pc0, (H_NSA, ncp)), jnp.broadcast_to(pc1, (H_NSA, ncp)))
    imp = _dot3(pcs, pool_ref[...])
    sel = _select_top(_block_scores(imp, jnp.full(imp.shape, past, I32), n_sel), min(N_SELECT, n_sel))
    picked = _dot(sel.astype(BF16), exp_ref[...])

    c128 = lax.broadcasted_iota(I32, (LANES, KV_CHUNK), 1)

    def new_col(col):
        return jnp.broadcast_to(z[:, col:col + LANES], (KV_CHUNK, LANES)).T

    def first_col(tile):
        return jnp.where(c128 == 0, tile, 0.0).astype(BF16)

    ks = [pg[0:LANES, :].astype(BF16) for pg in pages] + [first_col(new_col(COL_KVS))]
    vs = [pg[LANES:2 * LANES, :].astype(BF16) for pg in pages] + [first_col(new_col(COL_KVS + LANES))]
    s_s = jnp.concatenate([_dot(qbd, k) for k in ks], axis=1) + bss_ref[...]
    key = lax.broadcasted_iota(I32, s_s.shape, 1)
    p_s = _masked_softmax(s_s, (picked > 0.5) & (key <= past)).astype(BF16)
    o_s = _dot_nt(p_s[:, 0:KV_CHUNK], vs[0])
    for c in range(1, len(vs)):
        o_s = o_s + _dot_nt(p_s[:, c * KV_CHUNK:(c + 1) * KV_CHUNK], vs[c])

    kw_new, vw_new = new_col(COL_KVW), new_col(COL_KVW + LANES)
    win = win_ref[...]
    s_w = jnp.concatenate([_dot(qbd, win[0:LANES].astype(BF16)), _dot(qbd, first_col(kw_new))], axis=1) + bws_ref[...]
    colw = lax.broadcasted_iota(I32, s_w.shape, 1)
    p_w = _masked_softmax(s_w, (colw <= wbuf) & (wbuf - colw < WINDOW)).astype(BF16)
    o_w = _dot_nt(p_w[:, 0:wbuf], win[LANES:].astype(BF16)) + _dot_nt(p_w[:, wbuf:], first_col(vw_new))

    shifted = pltpu.roll(win, wbuf - 1, 1)
    new_kv = jnp.concatenate([kw_new, vw_new], axis=0)
    last = jnp.where(lax.broadcasted_iota(I32, (KV_W, LANES), 1) == LANES - 1, new_kv, shifted[:, wbuf - LANES:])
    wo_ref[:, 0:wbuf - LANES] = shifted[:, 0:wbuf - LANES]
    wo_ref[:, wbuf - LANES:] = last

    sig = jnp.broadcast_to(_sigmoid(z[:, COL_NG:COL_NG + LANES]), (H_NSA, LANES))

    def gate_col(c):
        return jnp.sum(jnp.where(lane8 == row8 * 3 + c, sig, 0.0), axis=1, keepdims=True)

    o = gate_col(0) * o_c + gate_col(1) * o_s + gate_col(2) * o_w
    fix = jnp.where((row8 % 2) != (row8 // R_GRP), pltpu.roll(o, HEAD_DIM, 1), o)
    left1 = left8[0:1]
    o_ref[...] = jnp.concatenate(
        [jnp.where(left1, fix[2 * k2:2 * k2 + 1], fix[2 * k2 + 1:2 * k2 + 2]) for k2 in range(H_NSA // 2)], axis=1)


def nsa_step(zs3, kc, vc, sel_cache, win_cache, page_idx, win_row0, bias_cs, bias_ss, bias_ws, past):
    bs = zs3.shape[0]
    n_pages = page_idx.shape[1]
    wbuf = win_cache.shape[2]
    n_sub = (past + 1) // CMP_STRIDE
    n_sel = -(-(past + 1) // SEL_BLOCK)
    n_keys = (n_pages + 1) * PAGE_SIZE
    assert n_sel <= SEL_LANES and PAGE_SIZE == KV_CHUNK and wbuf % KV_CHUNK == 0 and kc.shape[1] == n_sub
    pool, expand = _selection_consts(n_sub, n_keys)
    fixed = lambda shape: pl.BlockSpec(shape, lambda b, pt: (0,) * len(shape))
    page = lambda p: pl.BlockSpec((None, KV_W, PAGE_SIZE), lambda b, pt, p=p: (pt[b, p], 0, 0))
    grid_spec = pltpu.PrefetchScalarGridSpec(
        num_scalar_prefetch=1,
        grid=(bs,),
        in_specs=[pl.BlockSpec((None, 1, N_IN_PAD), lambda b, pt: (b, 0, 0)),
                  pl.BlockSpec((None, n_sub, LANES), lambda b, pt: (b, 0, 0)),
                  pl.BlockSpec((None, n_sub, LANES), lambda b, pt: (b, 0, 0))]
        + [page(p) for p in range(n_pages)]
        + [pl.BlockSpec((None, KV_W, wbuf), lambda b, pt: (win_row0 + b, 0, 0)),
           fixed(bias_cs.shape), fixed(bias_ss.shape), fixed(bias_ws.shape), fixed(pool.shape), fixed(expand.shape)],
        out_specs=[pl.BlockSpec((None, 1, NSA_QW), lambda b, pt: (b, 0, 0)),
                   pl.BlockSpec((None, KV_W, wbuf), lambda b, pt: (b, 0, 0))],
    )
    return pl.pallas_call(
        functools.partial(_nsa_step_kernel, n_pages=n_pages, past=past, wbuf=wbuf, n_cmp=n_sub - 1, n_sel=n_sel),
        grid_spec=grid_spec,
        out_shape=[jax.ShapeDtypeStruct((bs, 1, NSA_QW), F32), jax.ShapeDtypeStruct((bs, KV_W, wbuf), F32)],
        compiler_params=_cparams(1),
        name="nsa_step",
    )(page_idx, zs3, kc, vc, *([sel_cache] * n_pages), win_cache, bias_cs, bias_ss, bias_ws, pool, expand)


def _route(logits, tm):
    m = logits.shape[0]
    top_v, top_i = lax.top_k(logits, TOP_K)
    top_w = jax.nn.softmax(top_v, axis=-1)
    slot_e = top_i.reshape(-1)
    onehot = (slot_e[:, None] == jnp.arange(N_EXPERTS)[None, :]).astype(I32)
    before = jnp.cumsum(onehot, axis=0) - onehot
    rank = jnp.sum(before * onehot, axis=1)
    count = jnp.sum(onehot, axis=0)
    padded = -(-count // tm) * tm
    start = jnp.cumsum(padded) - padded
    pos = start[slot_e] + rank
    n_rows = (-(-(m * TOP_K) // tm) + N_EXPERTS) * tm
    row_token = jnp.zeros((n_rows,), I32).at[pos].set(jnp.arange(m * TOP_K, dtype=I32) // TOP_K)
    row_gate = jnp.zeros((n_rows,), F32).at[pos].set(top_w.reshape(-1))
    tile_start = jnp.arange(n_rows // tm, dtype=I32) * tm
    tile_expert = jnp.clip(jnp.searchsorted(jnp.cumsum(padded), tile_start, side='right'), 0, N_EXPERTS - 1)
    tile_valid = (tile_start < jnp.sum(padded)).astype(I32)
    return pos.reshape(m, TOP_K), row_token, row_gate.reshape(n_rows, 1), tile_expert.astype(I32), tile_valid


def _dense_ffn(h, g, w1, w3, w2):
    m = h.shape[0]
    tm = _row_tile(m)
    ones = jnp.ones((m // tm,), I32)
    return grouped_ffn(h, g, jnp.ones((m, 1), F32), jnp.zeros((m // tm,), I32), ones,
                       w1[None], w3[None], w2[None], tm=tm, tf=w1.shape[1] // 2, residual=True)


def kernel(x_prompt, x_sample, p_prompt, p_sample, state_ret, cache_win_kv, cache_cmp_kv, cache_sel_kv, page_table, rel_bias, norm_mix, w_in, ret_gain, cmp_pe, cmp_w1, cmp_w2, w_out, norm_ffn, ffn_w1, ffn_w3, ffn_w2, router, moe_w1, moe_w3, moe_w2, ple_norm, ple_gate, ple_proj, norm_final):
    b_p, t_p, d = x_prompt.shape
    b_s = x_sample.shape[0]
    depth = w_in.shape[0]
    n_pages = page_table.shape[1]
    past = n_pages * PAGE_SIZE
    n_pool = cache_cmp_kv.shape[1]
    w_buf = cache_win_kv.shape[2]
    assert x_sample.shape[1] == 1 and t_p % Q_BLOCK == 0 and t_p >= w_buf and past % CMP_STRIDE == 0
    row = (2, N_KV, HEAD_DIM)
    m_p = b_p * t_p

    ar = lambda n: jnp.arange(n, dtype=I32)
    n_sub_p = t_p // CMP_STRIDE
    bias_ct = bias_lookup(rel_bias, ar(t_p)[None, :] - (ar(n_sub_p)[:, None] * CMP_STRIDE + L_CMP - 1))
    rel_t = (ar(FAR_REL + 1)[:, None, None] * KV_CHUNK + ar(Q_BLOCK)[None, None, :] - ar(KV_CHUNK)[None, :, None])
    bias_tt = bias_lookup(rel_bias, rel_t.reshape((FAR_REL + 1) * KV_CHUNK, Q_BLOCK))
    bias_tt = bias_tt.reshape(H_NSA, FAR_REL + 1, KV_CHUNK, Q_BLOCK).transpose(1, 0, 2, 3)
    n_sub_s = (past + 1) // CMP_STRIDE
    rows8 = jnp.zeros((SUBLANES, 1), I32)
    bias_cs = bias_lookup(rel_bias, rows8 + (past - (ar(n_sub_s)[None, :] * CMP_STRIDE + L_CMP - 1)))[:, 0]
    bias_ss = bias_lookup(rel_bias, rows8 + (past - ar((n_pages + 1) * PAGE_SIZE)[None, :]))[:, 0]
    bias_ws = bias_lookup(rel_bias, rows8 + (w_buf - ar(w_buf + KV_CHUNK)[None, :]))[:, 0]

    def rows_last(x):
        return jnp.moveaxis(x, 2, -1).reshape(x.shape[0] * x.shape[1], KV_W, x.shape[2])

    def rows_first(xt, lead):
        return jnp.moveaxis(xt.reshape((lead,) + row + (xt.shape[-1],)), -1, 1)

    cmp_cache = rows_last(cache_cmp_kv)
    sel_cache = rows_last(cache_sel_kv)
    win_cache = rows_last(cache_win_kv)
    state_t = jnp.moveaxis(state_ret, 1, -1).reshape(depth * H_RET, HEAD_DIM, HEAD_DIM, b_s)

    h_p = x_prompt.reshape(m_p, d)
    h_s = x_sample.reshape(b_s, d)
    rp, rs, wp, ws, cp, cs, sp, ss = [], [], [], [], [], [], [], []
    for i in range(depth):
        w_in_b = jnp.pad(w_in[i], ((0, 0), (0, N_IN_PAD - N_IN))).astype(BF16)
        w_out_b = w_out[i].astype(BF16)
        pe_t, cw1, cw2 = _compress_weights(cmp_pe[i], cmp_w1[i], cmp_w2[i])

        w_kvt = w_in[i][:, COL_KVC:COL_KVC + 3 * KV_W].T.astype(BF16)
        z, kvt = rms_matmul_kvt(h_p, norm_mix[i], w_in_b, w_kvt, b_p)
        z3 = z.reshape(b_p, t_p, N_IN_PAD)
        ret_y, ret_state = retention_prompt(z3, ret_gain[i])
        kc, vc = compress(kvt, ar(b_p)[:, None], pe_t, cw1, cw2)
        nsa_y = nsa_prompt(z3, kvt, kc, jnp.swapaxes(vc, 1, 2), bias_ct, bias_tt)
        h_p = out_proj(h_p, ret_y.reshape(m_p, RET_W), nsa_y.reshape(m_p, NSA_QW), w_out_b)
        rp.append(ret_state)
        cp.append(rows_first(kvt[:, 0:KV_W], b_p))
        sp.append(rows_first(kvt[:, KV_W:2 * KV_W], b_p))
        wp.append(rows_first(kvt[:, 2 * KV_W:, t_p - w_buf:], b_p))

        zs = rms_matmul(h_s, norm_mix[i], w_in_b)
        zst = zs.T
        ret_yst, state_s = retention_step(zst, state_t, i, past, ret_gain[i])
        page_idx = page_table + i * n_pool
        grp = 4 if b_s % 4 == 0 else 1
        kc_s, vc_s = compress(cmp_cache, page_idx.reshape(b_s // grp, grp * n_pages), pe_t, cw1, cw2)
        kc_s, vc_s = kc_s.reshape(b_s, n_sub_s, LANES), vc_s.reshape(b_s, n_sub_s, LANES)
        nsa_ys, win_s = nsa_step(zs.reshape(b_s, 1, N_IN_PAD), kc_s, vc_s, sel_cache, win_cache, page_idx, i * b_s,
                                 bias_cs, bias_ss, bias_ws, past)
        h_s = out_proj(h_s, ret_yst.T, nsa_ys.reshape(b_s, NSA_QW), w_out_b)
        rs.append(jnp.moveaxis(state_s, -1, 0))
        ws.append(rows_first(win_s, b_s))
        cs.append(rows_first(zst[None, COL_KVC:COL_KVC + KV_W], 1).reshape((b_s, 1) + row))
        ss.append(rows_first(zst[None, COL_KVS:COL_KVS + KV_W], 1).reshape((b_s, 1) + row))

        j = i // 2
        if i % 2 == 0:
            w1, w3, w2 = ffn_w1[j].astype(BF16), ffn_w3[j].astype(BF16), ffn_w2[j].astype(BF16)
            h_p = _dense_ffn(h_p, norm_ffn[i], w1, w3, w2)
            h_s = _dense_ffn(h_s, norm_ffn[i], w1, w3, w2)
        else:
            router_w = jnp.pad(router[j], ((0, 0), (0, LANES - N_EXPERTS)))
            tf = moe_w1.shape[3] // 7
            logits_p, xn_p = router_logits(h_p, norm_ffn[i], router_w)
            tm = 1024
            pos, row_token, row_gate, tile_expert, tile_valid = _route(logits_p[:, :N_EXPERTS], tm)
            y_rows = grouped_ffn(xn_p[row_token], norm_ffn[i], row_gate, tile_expert, tile_valid,
                                 moe_w1[j], moe_w3[j], moe_w2[j], tm=tm, tf=tf, residual=False)
            h_p = h_p + y_rows[pos[:, 0]] + y_rows[pos[:, 1]]
            logits_s, xn_s = router_logits(h_s, norm_ffn[i], router_w)
            top_v, top_i = lax.top_k(logits_s[:, :N_EXPERTS], TOP_K)
            top_w = jax.nn.softmax(top_v, axis=-1)
            gates = jnp.sum(jax.nn.one_hot(top_i, N_EXPERTS, dtype=F32) * top_w[..., None], axis=-2)
            y_e = grouped_ffn(jnp.tile(xn_s, (N_EXPERTS, 1)), norm_ffn[i], gates.T.reshape(N_EXPERTS * b_s, 1),
                              jnp.arange(N_EXPERTS, dtype=I32), jnp.ones((N_EXPERTS,), I32),
                              moe_w1[j], moe_w3[j], moe_w2[j], tm=b_s, tf=tf, residual=False)
            h_s = h_s + jnp.sum(y_e.reshape(N_EXPERTS, b_s, d), axis=0)

        wg, wpj = ple_gate[i].astype(BF16), ple_proj[i].astype(BF16)
        h_p = ple(h_p, ple_norm[i], wg, p_prompt[i].reshape(m_p, P_DIM), wpj, norm_final, i == depth - 1)
        h_s = ple(h_s, ple_norm[i], wg, p_sample[i].reshape(b_s, P_DIM), wpj, norm_final, i == depth - 1)

    return (h_p.reshape(b_p, t_p, d), h_s.reshape(b_s, 1, d), jnp.stack(rp), jnp.stack(rs), jnp.stack(wp),
            jnp.stack(ws), jnp.stack(cp), jnp.stack(cs), jnp.stack(sp), jnp.stack(ss))
```

```python
import functools
import math

import numpy as np
import jax
import jax.numpy as jnp
from jax import lax
from jax.experimental import pallas as pl
from jax.experimental.pallas import tpu as pltpu

F32 = jnp.float32
BF16 = jnp.bfloat16
I32 = jnp.int32

LANES = 128
SUBLANES = 8
VMEM_LIMIT_BYTES = 56 * 1024 * 1024

D_MODEL = 1024
HEAD_DIM = 64
H_RET = 8
H_NSA = 8
N_KV = 2
R_GRP = H_NSA // N_KV
RET_W = H_RET * HEAD_DIM
NSA_QW = H_NSA * HEAD_DIM
KV_W = 2 * N_KV * HEAD_DIM
N_IN = 4 * RET_W + NSA_QW + 3 * KV_W + 3 * H_NSA
N_IN_PAD = 3584
COL_RQ, COL_RK, COL_RV, COL_RG = 0, RET_W, 2 * RET_W, 3 * RET_W
COL_NQ = 4 * RET_W
COL_KVC = COL_NQ + NSA_QW
COL_KVS = COL_KVC + KV_W
COL_KVW = COL_KVS + KV_W
COL_NG = COL_KVW + KV_W
RET_CHUNK = 128
CMP_STRIDE = 16
L_CMP = 2 * CMP_STRIDE
CMP_HIDDEN = 2 * HEAD_DIM
SEL_BLOCK = 64
N_SELECT = 16
WINDOW = 512
Q_BLOCK = 128
N_BUCKETS = 32
MAX_DISTANCE = 128
N_EXPERTS = 8
TOP_K = 2
P_DIM = 256
PAGE_SIZE = 128
ROPE_BASE = 10000.0
EPS = 1e-6
FORCED_SCORE = 1e4
NEG = -1e30
TINY = float(np.finfo(np.float32).tiny)
SEL_LANES = 64
FAR_REL = 2
KV_CHUNK = 128


def _cparams(n_grid):
    return pltpu.CompilerParams(dimension_semantics=("arbitrary",) * n_grid,
                                vmem_limit_bytes=VMEM_LIMIT_BYTES)


def _dot(a, b):
    return jnp.dot(a, b, preferred_element_type=F32)


def _dot_nt(a, b):
    return lax.dot_general(a, b, (((1,), (1,)), ((), ())), preferred_element_type=F32)


def _split3(x):
    hi = x.astype(BF16)
    r = x - hi.astype(F32)
    mid = r.astype(BF16)
    lo = (r - mid.astype(F32)).astype(BF16)
    return hi, mid, lo


def _dot3(x, b01):
    hi, mid, lo = _split3(x)
    return _dot(hi, b01) + _dot(mid, b01) + _dot(lo, b01)


def _dot6(x, w):
    xh, xm, xl = _split3(x)
    wh, wm, wl = _split3(w)
    return (_dot(xh, wh) + _dot(xh, wm) + _dot(xm, wh)
            + _dot(xm, wm) + _dot(xh, wl) + _dot(xl, wh))


def _sigmoid(x):
    return 1.0 / (1.0 + jnp.exp(-x))


def _rms(x, g):
    return x * lax.rsqrt(jnp.mean(x * x, axis=-1, keepdims=True) + EPS) * g


def _row_tile(m):
    for t in (512, 256, 128, 64, 32, 16, 8):
        if m % t == 0:
            return t
    raise ValueError(f"row count {m} not a multiple of 8")


def _rms_matmul_kernel(x_ref, g_ref, w_ref, o_ref):
    o_ref[...] = _dot(_rms(x_ref[...], g_ref[...]).astype(BF16), w_ref[...])


def rms_matmul(x, g, w):
    m, d = x.shape
    n = w.shape[1]
    tm = min(_row_tile(m), 256)
    return pl.pallas_call(
        _rms_matmul_kernel,
        grid=(m // tm,),
        in_specs=[pl.BlockSpec((tm, d), lambda i: (i, 0)),
                  pl.BlockSpec((1, d), lambda i: (0, 0)),
                  pl.BlockSpec((d, n), lambda i: (0, 0))],
        out_specs=pl.BlockSpec((tm, n), lambda i: (i, 0)),
        out_shape=jax.ShapeDtypeStruct((m, n), F32),
        compiler_params=_cparams(1),
    )(x, g.reshape(1, d), w)


def _rms_matmul_kvt_kernel(x_ref, g_ref, w_ref, wt_ref, o_ref, ot_ref):
    xn = _rms(x_ref[...], g_ref[...]).astype(BF16)
    o_ref[...] = _dot(xn, w_ref[...])
    ot_ref[...] = _dot_nt(wt_ref[...], xn)


def rms_matmul_kvt(x, g, w, wt, bsz):
    m, d = x.shape
    n = w.shape[1]
    nt = wt.shape[0]
    t = m // bsz
    tm = min(_row_tile(t), 256)
    per = t // tm
    return pl.pallas_call(
        _rms_matmul_kvt_kernel,
        grid=(m // tm,),
        in_specs=[pl.BlockSpec((tm, d), lambda i: (i, 0)),
                  pl.BlockSpec((1, d), lambda i: (0, 0)),
                  pl.BlockSpec((d, n), lambda i: (0, 0)),
                  pl.BlockSpec((nt, d), lambda i: (0, 0))],
        out_specs=[pl.BlockSpec((tm, n), lambda i: (i, 0)),
                   pl.BlockSpec((None, nt, tm), lambda i: (i // per, 0, i % per))],
        out_shape=[jax.ShapeDtypeStruct((m, n), F32), jax.ShapeDtypeStruct((bsz, nt, t), F32)],
        compiler_params=_cparams(1),
        name="rms_matmul_kvt",
    )(x, g.reshape(1, d), w, wt)


def _router_kernel(x_ref, g_ref, w_ref, o_ref):
    o_ref[...] = _dot6(_rms(x_ref[...], g_ref[...]), w_ref[...])


def router_logits(x, g, w):
    m, d = x.shape
    n = w.shape[1]
    tm = min(_row_tile(m), 256)
    return pl.pallas_call(
        _router_kernel,
        grid=(m // tm,),
        in_specs=[pl.BlockSpec((tm, d), lambda i: (i, 0)),
                  pl.BlockSpec((1, d), lambda i: (0, 0)),
                  pl.BlockSpec((d, n), lambda i: (0, 0))],
        out_specs=pl.BlockSpec((tm, n), lambda i: (i, 0)),
        out_shape=jax.ShapeDtypeStruct((m, n), F32),
        compiler_params=_cparams(1),
        name="router",
    )(x, g.reshape(1, d), w)


def _outproj_kernel(h_ref, a_ref, b_ref, w_ref, o_ref):
    ka = a_ref.shape[1]
    o_ref[...] = (h_ref[...] + _dot(a_ref[...].astype(BF16), w_ref[0:ka, :])
                  + _dot(b_ref[...].astype(BF16), w_ref[ka:, :]))


def out_proj(h, a, b, w):
    m, d = h.shape
    ka, kb = a.shape[1], b.shape[1]
    tm = _row_tile(m)
    return pl.pallas_call(
        _outproj_kernel,
        grid=(m // tm,),
        in_specs=[pl.BlockSpec((tm, d), lambda i: (i, 0)),
                  pl.BlockSpec((tm, ka), lambda i: (i, 0)),
                  pl.BlockSpec((tm, kb), lambda i: (i, 0)),
                  pl.BlockSpec((ka + kb, d), lambda i: (0, 0))],
        out_specs=pl.BlockSpec((tm, d), lambda i: (i, 0)),
        out_shape=jax.ShapeDtypeStruct((m, d), F32),
        compiler_params=_cparams(1),
        name="out_proj",
    )(h, a, b, w)


def _ffn_kernel(te_ref, tv_ref, x_ref, g_ref, gate_ref, w1_ref, w3_ref, w2_ref, o_ref,
                xn_ref, acc_ref, *, residual):
    i = pl.program_id(0)
    j = pl.program_id(1)

    @pl.when(j == 0)
    def _():
        xn_ref[...] = x_ref[...] if x_ref.dtype == BF16 else _rms(x_ref[...], g_ref[...]).astype(BF16)
        acc_ref[...] = jnp.zeros_like(acc_ref)

    @pl.when(tv_ref[i] != 0)
    def _():
        xn = xn_ref[...]
        a = _dot(xn, w1_ref[...].astype(BF16))
        b = _dot(xn, w3_ref[...].astype(BF16))
        hid = (a * _sigmoid(a)) * b
        acc_ref[...] += _dot(hid.astype(BF16), w2_ref[...].astype(BF16))

    @pl.when(j == pl.num_programs(1) - 1)
    def _():
        y = acc_ref[...] * gate_ref[...]
        if residual:
            y = y + x_ref[...]
        o_ref[...] = y


def grouped_ffn(x, g, row_gate, tile_expert, tile_valid, w1, w3, w2, *, tm, tf, residual):
    m, d = x.shape
    f = w1.shape[2]
    assert not (residual and x.dtype == BF16)
    grid_spec = pltpu.PrefetchScalarGridSpec(
        num_scalar_prefetch=2,
        grid=(m // tm, f // tf),
        in_specs=[pl.BlockSpec((tm, d), lambda i, j, te, tv: (i, 0)),
                  pl.BlockSpec((1, d), lambda i, j, te, tv: (0, 0)),
                  pl.BlockSpec((tm, 1), lambda i, j, te, tv: (i, 0)),
                  pl.BlockSpec((None, d, tf), lambda i, j, te, tv: (te[i], 0, j)),
                  pl.BlockSpec((None, d, tf), lambda i, j, te, tv: (te[i], 0, j)),
                  pl.BlockSpec((None, tf, d), lambda i, j, te, tv: (te[i], j, 0))],
        out_specs=pl.BlockSpec((tm, d), lambda i, j, te, tv: (i, 0)),
        scratch_shapes=[pltpu.VMEM((tm, d), BF16), pltpu.VMEM((tm, d), F32)],
    )
    return pl.pallas_call(
        functools.partial(_ffn_kernel, residual=residual),
        grid_spec=grid_spec,
        out_shape=jax.ShapeDtypeStruct((m, d), F32),
        compiler_params=_cparams(2),
        name="grouped_ffn",
    )(tile_expert, tile_valid, x, g.reshape(1, d), row_gate, w1, w3, w2)


def _gather_rows_kernel(idx_ref, x_hbm, o_ref, sem):
    base = pl.program_id(0) * o_ref.shape[0]

    def row_copy(k):
        return pltpu.make_async_copy(x_hbm.at[pl.ds(idx_ref[base + k], 1)], o_ref.at[pl.ds(k, 1)], sem)

    def start(k, carry):
        row_copy(k).start()
        return carry

    def wait(k, carry):
        row_copy(k).wait()
        return carry

    lax.fori_loop(0, o_ref.shape[0], start, 0)
    lax.fori_loop(0, o_ref.shape[0], wait, 0)


def gather_rows(x, idx, rows_per_step=512):
    n = idx.shape[0]
    d = x.shape[1]
    assert n % rows_per_step == 0 and x.dtype == F32
    grid_spec = pltpu.PrefetchScalarGridSpec(
        num_scalar_prefetch=1,
        grid=(n // rows_per_step,),
        in_specs=[pl.BlockSpec(memory_space=pl.ANY)],
        out_specs=pl.BlockSpec((rows_per_step, d), lambda i, ix: (i, 0)),
        scratch_shapes=[pltpu.SemaphoreType.DMA],
    )
    return pl.pallas_call(
        _gather_rows_kernel,
        grid_spec=grid_spec,
        out_shape=jax.ShapeDtypeStruct((n, d), x.dtype),
        compiler_params=_cparams(1),
        name="gather_rows",
    )(idx, x)


def _ple_kernel(h_ref, g_ref, wg_ref, p_ref, wp_ref, gf_ref, o_ref, *, final):
    h = h_ref[...]
    gate = _sigmoid(_dot(_rms(h, g_ref[...]).astype(BF16), wg_ref[...]))
    proj = _dot(p_ref[...].astype(BF16), wp_ref[...])
    h2 = h + gate * proj
    o_ref[...] = _rms(h2, gf_ref[...]) if final else h2


def ple(h, g, wg, p, wp, gf, final):
    m, d = h.shape
    pd = p.shape[1]
    tm = _row_tile(m)
    row = lambda i: (i, 0)
    fixed = lambda i: (0, 0)
    return pl.pallas_call(
        functools.partial(_ple_kernel, final=final),
        grid=(m // tm,),
        in_specs=[pl.BlockSpec((tm, d), row), pl.BlockSpec((1, d), fixed),
                  pl.BlockSpec((d, d), fixed), pl.BlockSpec((tm, pd), row),
                  pl.BlockSpec((pd, d), fixed), pl.BlockSpec((1, d), fixed)],
        out_specs=pl.BlockSpec((tm, d), row),
        out_shape=jax.ShapeDtypeStruct((m, d), F32),
        compiler_params=_cparams(1),
        name="ple",
    )(h, g.reshape(1, d), wg, p, wp, gf.reshape(1, d))


def _bias_kernel(tab_ref, d_ref, o_ref):
    n = jnp.maximum(d_ref[...], 0)
    max_exact = N_BUCKETS // 2
    nf = jnp.maximum(n, 1).astype(F32)
    large = max_exact + (jnp.log(nf / max_exact) / math.log(MAX_DISTANCE / max_exact)
                         * (N_BUCKETS - max_exact)).astype(I32)
    large = jnp.minimum(large, N_BUCKETS - 1)
    bucket = jnp.where(n < max_exact, n, large)
    for h in range(H_NSA):
        acc = jnp.zeros(n.shape, F32)
        for k in range(N_BUCKETS):
            acc = jnp.where(bucket == k, tab_ref[k, h], acc)
        o_ref[h] = acc


def bias_lookup(table, dist):
    r, c = dist.shape
    tr = min(r, 128)
    tc = 512 if c % 512 == 0 else c
    return pl.pallas_call(
        _bias_kernel,
        grid=(r // tr, c // tc),
        in_specs=[pl.BlockSpec(memory_space=pltpu.SMEM),
                  pl.BlockSpec((tr, tc), lambda i, j: (i, j))],
        out_specs=pl.BlockSpec((H_NSA, tr, tc), lambda i, j: (0, i, j)),
        out_shape=jax.ShapeDtypeStruct((H_NSA, r, c), F32),
        compiler_params=_cparams(2),
        name="bias_lookup",
    )(table, dist)


def _swap_halves(x):
    lane = lax.broadcasted_iota(I32, x.shape, 1)
    return jnp.where(lane % HEAD_DIM < HEAD_DIM // 2,
                     pltpu.roll(x, LANES - HEAD_DIM // 2, 1), pltpu.roll(x, HEAD_DIM // 2, 1))


def _retention_prompt_kernel(q_ref, k_ref, v_ref, g_ref, cos_ref, sin_ref, gain_ref, intra_ref,
                             qdec_ref, kdec_ref, cdec_ref, y_ref, s_out_ref, s_ref):
    c = pl.program_id(1)

    @pl.when(c == 0)
    def _():
        s_ref[...] = jnp.zeros_like(s_ref)

    cos = cos_ref[...]
    sin = sin_ref[...]
    lane = lax.broadcasted_iota(I32, (RET_CHUNK, LANES), 1)
    left = lane < HEAD_DIM
    row = lax.broadcasted_iota(I32, (LANES, LANES), 0)
    col = lax.broadcasted_iota(I32, (LANES, LANES), 1)
    same_head = (row < HEAD_DIM) == (col < HEAD_DIM)
    ones_bd = jnp.where(same_head, 1.0, 0.0).astype(BF16)
    for p in range(H_RET // 2):
        sl = slice(p * LANES, (p + 1) * LANES)
        q = q_ref[:, sl]
        k = k_ref[:, sl]
        v = v_ref[:, sl].astype(BF16)
        qr = q * cos + _swap_halves(q) * sin
        kr = (k * cos + _swap_halves(k) * sin) * (HEAD_DIM ** -0.5)
        qb = qr.astype(BF16)
        q2 = jnp.concatenate([jnp.where(left, qr, 0.0), jnp.where(left, 0.0, qr)], axis=0).astype(BF16)
        inner = _dot_nt(q2, kr.astype(BF16)) * intra_ref[p]
        o2 = _dot(inner.astype(BF16), v)
        s_old = s_ref[p]
        cross = _dot(qb, s_old.astype(BF16)) * qdec_ref[:, sl]
        o = jnp.where(left, o2[:RET_CHUNK], o2[RET_CHUNK:]) + cross
        kd = (kr * kdec_ref[:, sl]).T.astype(BF16)
        s_ref[p] = s_old * cdec_ref[p] + jnp.where(same_head, _dot(kd, v), 0.0)
        ms = _dot3(o * o, ones_bd) * (1.0 / HEAD_DIM)
        gate = g_ref[:, sl]
        y = o * lax.rsqrt(ms + EPS) * gain_ref[:, sl] * (gate * _sigmoid(gate))
        y_ref[:, sl] = y.astype(y_ref.dtype)

    @pl.when(c == pl.num_programs(1) - 1)
    def _():
        for p in range(H_RET // 2):
            s = s_ref[p]
            s_out_ref[2 * p] = s[:HEAD_DIM, :HEAD_DIM]
            s_out_ref[2 * p + 1] = s[HEAD_DIM:, HEAD_DIM:]


def _retention_consts(c):
    log_g = jnp.log1p(-jnp.exp2(-5.0 - jnp.arange(H_RET, dtype=F32)))
    j = jnp.arange(c, dtype=F32)
    diff = j[:, None] - j[None, :]
    intra = jnp.where(diff >= 0, jnp.exp(jnp.maximum(diff, 0.0)[None] * log_g[:, None, None]), 0.0)
    q_decay = jnp.exp((j[:, None] + 1.0) * log_g[None, :])
    k_decay = jnp.exp((c - 1.0 - j[:, None]) * log_g[None, :])
    c_decay = jnp.exp(c * log_g)
    intra2 = intra.reshape(H_RET // 2, 2 * c, c)
    qdec = jnp.repeat(q_decay, HEAD_DIM, axis=1)
    kdec = jnp.repeat(k_decay, HEAD_DIM, axis=1)
    cdec = jnp.repeat(c_decay.reshape(H_RET // 2, 2), HEAD_DIM, axis=1)[:, :, None] * jnp.ones((1, 1, LANES), F32)
    return intra2, qdec, kdec, cdec


def _rotary_tables(pos):
    half = HEAD_DIM // 2
    inv = ROPE_BASE ** (-jnp.arange(half, dtype=F32) / half)
    ang = pos.astype(F32)[:, None] * inv[None, :]
    cos, sin = jnp.cos(ang), jnp.sin(ang)
    cos_t = jnp.concatenate([cos, cos, cos, cos], axis=1)
    sin_t = jnp.concatenate([-sin, sin, -sin, sin], axis=1)
    return cos_t, sin_t


def retention_prompt(z3, gain):
    b, t, _ = z3.shape
    c = RET_CHUNK
    cos_t, sin_t = _rotary_tables(jnp.arange(t))
    intra2, qdec, kdec, cdec = _retention_consts(c)
    blk = lambda col: pl.BlockSpec((None, c, RET_W), lambda bi, ci, col=col: (bi, ci, col))
    fixed2 = lambda shape: pl.BlockSpec(shape, lambda bi, ci: (0,) * len(shape))
    return pl.pallas_call(
        _retention_prompt_kernel,
        grid=(b, t // c),
        in_specs=[blk(COL_RQ // RET_W), blk(COL_RK // RET_W), blk(COL_RV // RET_W), blk(COL_RG // RET_W),
                  pl.BlockSpec((c, LANES), lambda bi, ci: (ci, 0)),
                  pl.BlockSpec((c, LANES), lambda bi, ci: (ci, 0)),
                  fixed2((1, RET_W)), fixed2((H_RET // 2, 2 * c, c)), fixed2((c, RET_W)),
                  fixed2((c, RET_W)), fixed2((H_RET // 2, LANES, LANES))],
        out_specs=[pl.BlockSpec((None, c, RET_W), lambda bi, ci: (bi, ci, 0)),
                   pl.BlockSpec((None, H_RET, HEAD_DIM, HEAD_DIM), lambda bi, ci: (bi, 0, 0, 0))],
        out_shape=[jax.ShapeDtypeStruct((b, t, RET_W), BF16),
                   jax.ShapeDtypeStruct((b, H_RET, HEAD_DIM, HEAD_DIM), F32)],
        scratch_shapes=[pltpu.VMEM((H_RET // 2, LANES, LANES), F32)],
        compiler_params=_cparams(2),
        name="retention_prompt",
    )(z3, z3, z3, z3, cos_t, sin_t, gain.reshape(1, RET_W), intra2, qdec, kdec, cdec)


def _retention_step_kernel(q_ref, k_ref, v_ref, g_ref, s_ref, cos_ref, sin_ref, gamma_ref, gain_ref,
                           y_ref, s_out_ref):
    half = HEAD_DIM // 2
    cos = cos_ref[...]
    sin = sin_ref[...]

    def rot(x):
        x1, x2 = x[:half], x[half:]
        return jnp.concatenate([x1 * cos - x2 * sin, x1 * sin + x2 * cos], axis=0)

    qr = rot(q_ref[...])
    kr = rot(k_ref[...]) * (HEAD_DIM ** -0.5)
    v = v_ref[...]
    gamma = gamma_ref[...]
    o = jnp.zeros_like(v)
    for d in range(HEAD_DIM):
        s_new = s_ref[d] * gamma + kr[d:d + 1] * v
        s_out_ref[d] = s_new
        o = o + qr[d:d + 1] * s_new
    ms = jnp.mean(o * o, axis=0, keepdims=True)
    gate = g_ref[...]
    y_ref[...] = o * lax.rsqrt(ms + EPS) * gain_ref[...] * (gate * _sigmoid(gate))


def retention_step(zt, state, layer, pos, gain):
    bs = zt.shape[1]
    half = HEAD_DIM // 2
    inv = ROPE_BASE ** (-jnp.arange(half, dtype=F32) / half)
    ang = jnp.full((1,), pos).astype(F32)[:, None] * inv[None, :]
    cos = jnp.broadcast_to(jnp.cos(ang).reshape(half, 1), (half, bs))
    sin = jnp.broadcast_to(jnp.sin(ang).reshape(half, 1), (half, bs))
    gamma = 1.0 - jnp.exp2(-5.0 - jnp.arange(H_RET, dtype=F32))
    gamma = jnp.broadcast_to(gamma.reshape(H_RET, 1, 1), (H_RET, 1, bs))
    gain_b = jnp.broadcast_to(gain.reshape(H_RET, HEAD_DIM, 1), (H_RET, HEAD_DIM, bs))
    head = lambda col: pl.BlockSpec((HEAD_DIM, bs), lambda h, col=col: (col // HEAD_DIM + h, 0))
    fixed = pl.BlockSpec((half, bs), lambda h: (0, 0))
    return pl.pallas_call(
        _retention_step_kernel,
        grid=(H_RET,),
        in_specs=[head(COL_RQ), head(COL_RK), head(COL_RV), head(COL_RG),
                  pl.BlockSpec((None, HEAD_DIM, HEAD_DIM, bs), lambda h: (layer * H_RET + h, 0, 0, 0)),
                  fixed, fixed,
                  pl.BlockSpec((None, 1, bs), lambda h: (h, 0, 0)),
                  pl.BlockSpec((None, HEAD_DIM, bs), lambda h: (h, 0, 0))],
        out_specs=[pl.BlockSpec((HEAD_DIM, bs), lambda h: (h, 0)),
                   pl.BlockSpec((None, HEAD_DIM, HEAD_DIM, bs), lambda h: (h, 0, 0, 0))],
        out_shape=[jax.ShapeDtypeStruct((RET_W, bs), F32),
                   jax.ShapeDtypeStruct((H_RET, HEAD_DIM, HEAD_DIM, bs), F32)],
        compiler_params=_cparams(1),
        name="retention_step",
    )(zt, zt, zt, zt, state, cos, sin, gamma, gain_b)


def _gelu_tanh(x):
    return 0.5 * x * (1.0 + jnp.tanh(math.sqrt(2.0 / math.pi) * (x + 0.044715 * (x * x * x))))


def _compress_kernel(*refs, n_pieces):
    x_refs = refs[1:1 + n_pieces]
    pe_ref, w1_ref, w2_ref, kc_ref, vc_ref, rows_ref = refs[1 + n_pieces:]
    width = x_refs[0].shape[1]
    n = n_pieces * width // CMP_STRIDE
    for kv, o_ref in enumerate((kc_ref, vc_ref)):
        for p, x_ref in enumerate(x_refs):
            for c in range(width // LANES):
                rows_ref[pl.ds(p * width + c * LANES, LANES), :] = (
                    x_ref[kv * LANES:(kv + 1) * LANES, c * LANES:(c + 1) * LANES].T)
        xk = jnp.concatenate([rows_ref[pl.ds(j, n, stride=CMP_STRIDE), :] for j in range(CMP_STRIDE)], axis=1)
        top = _dot((xk + pe_ref[kv, 0]).astype(BF16), w1_ref[kv, 0])
        bot = _dot((xk + pe_ref[kv, 1]).astype(BF16), w1_ref[kv, 1])
        pre = top + pltpu.roll(bot, n - 1, 0)
        hid = _gelu_tanh(pre).astype(BF16)
        o_ref[...] = jnp.concatenate(
            [_dot(hid[:, g * CMP_HIDDEN:(g + 1) * CMP_HIDDEN], w2_ref[kv]) for g in range(N_KV)], axis=1)


def _compress_weights(pe, w1, w2):
    pe_r = pe.reshape(2, 2, CMP_STRIDE, 1, HEAD_DIM)
    pe_t = jnp.broadcast_to(pe_r, (2, 2, CMP_STRIDE, N_KV, HEAD_DIM)).reshape(2, 2, 1, CMP_STRIDE * LANES)
    w1r = w1.reshape(2, 2, CMP_STRIDE, HEAD_DIM, CMP_HIDDEN)
    eye = jnp.eye(N_KV, dtype=w1.dtype)
    wbig = jnp.einsum('khjdc,gG->khjgdGc', w1r, eye)
    wbig = wbig.reshape(2, 2, CMP_STRIDE * LANES, N_KV * CMP_HIDDEN)
    return pe_t.astype(F32), wbig.astype(BF16), w2.astype(BF16)


def compress(src, idx, pe_t, w1b, w2b):
    bsz, n_pieces = idx.shape
    width = src.shape[2]
    n = n_pieces * width // CMP_STRIDE
    piece = lambda p: pl.BlockSpec((None, KV_W, width), lambda b, ix, p=p: (ix[b, p], 0, 0))
    fixed = lambda shape: pl.BlockSpec(shape, lambda b, ix: (0,) * len(shape))
    out = pl.BlockSpec((None, n, LANES), lambda b, ix: (b, 0, 0))
    grid_spec = pltpu.PrefetchScalarGridSpec(
        num_scalar_prefetch=1,
        grid=(bsz,),
        in_specs=[piece(p) for p in range(n_pieces)]
        + [fixed(pe_t.shape), fixed(w1b.shape), fixed(w2b.shape)],
        out_specs=[out, out],
        scratch_shapes=[pltpu.VMEM((n_pieces * width, LANES), F32)],
    )
    return pl.pallas_call(
        functools.partial(_compress_kernel, n_pieces=n_pieces),
        grid_spec=grid_spec,
        out_shape=[jax.ShapeDtypeStruct((bsz, n, LANES), F32)] * 2,
        compiler_params=_cparams(1),
        name="compress",
    )(idx, *([src] * n_pieces), pe_t, w1b, w2b)


def _masked_softmax(s, mask):
    s = jnp.where(mask, s, NEG)
    m = jnp.max(s, axis=-1, keepdims=True)
    e = jnp.where(mask, jnp.exp(s - m), 0.0)
    return e / jnp.maximum(jnp.sum(e, axis=-1, keepdims=True), TINY)


def _block_scores(imp, qpos, n_sel, axis=1):
    blk = lax.broadcasted_iota(I32, imp.shape, axis)
    cur = qpos // SEL_BLOCK
    forced = (blk == 0) | (blk == cur) | (blk == cur - 1)
    valid = blk * SEL_BLOCK <= qpos
    score = jnp.where(forced, FORCED_SCORE, jnp.where(valid, imp, -1.0))
    return jnp.where(blk < n_sel, score, -2.0)


def _select_top(score, n_top, axis=1):
    blk = lax.broadcasted_iota(I32, score.shape, axis)
    rank = jnp.zeros(score.shape, F32)
    for i in range(score.shape[axis]):
        one = score[:, i:i + 1] if axis == 1 else score[i:i + 1, :]
        beats = (one > score) | ((one == score) & (blk > i))
        rank = rank + jnp.where(beats, 1.0, 0.0)
    return jnp.where(rank < n_top, 1.0, 0.0)


def _attend_t(s, ok, m_prev, l_prev):
    s = jnp.where(ok, s, NEG)
    m_new = jnp.maximum(m_prev, jnp.max(s, axis=0, keepdims=True))
    alpha = jnp.exp(m_prev - m_new)
    p = jnp.exp(s - m_new)
    l_new = alpha * l_prev + jnp.sum(p, axis=0, keepdims=True)
    return m_new, l_new, alpha, p


def _nsa_prompt_kernel(q_ref, gate_ref, kc_ref, vct_ref, ks_ref, vst_ref, kw_ref, vwt_ref, bc_ref, bt_ref,
                       o_ref, qt_ref, pcs_ref, pick_ref, acc_ref, os_ref, s_ref, p_ref, *, n_cmp, n_sel):
    i = pl.program_id(1)
    qs = i * Q_BLOCK
    key_i = lax.broadcasted_iota(I32, (KV_CHUNK, Q_BLOCK), 0)
    tok_i = lax.broadcasted_iota(I32, (KV_CHUNK, Q_BLOCK), 1)
    row8 = lax.broadcasted_iota(I32, (H_NSA, Q_BLOCK), 0)
    zeros_half = jnp.zeros((HEAD_DIM, Q_BLOCK), F32)

    for k2 in range(H_NSA // 2):
        slab_t = q_ref[:, k2 * LANES:(k2 + 1) * LANES].T * (HEAD_DIM ** -0.5)
        for h in (2 * k2, 2 * k2 + 1):
            qh = slab_t[(h % 2) * HEAD_DIM:(h % 2 + 1) * HEAD_DIM]
            both = [qh, zeros_half] if h // R_GRP == 0 else [zeros_half, qh]
            qt_ref[:, h * Q_BLOCK:(h + 1) * Q_BLOCK] = jnp.concatenate(both, axis=0).astype(BF16)

    sig_t = _sigmoid(gate_ref[...]).T

    ncp = kc_ref.shape[0]
    kc = kc_ref[...].astype(BF16)
    n_c = lax.broadcasted_iota(I32, (ncp, Q_BLOCK), 0)
    t_c = lax.broadcasted_iota(I32, (ncp, Q_BLOCK), 1)
    ok_c = (n_c * CMP_STRIDE + (L_CMP - 1) <= qs + t_c) & (n_c < n_cmp)
    blk_pos = qs + lax.broadcasted_iota(I32, (pick_ref.shape[1], Q_BLOCK), 1)
    o_c = []
    for g in range(N_KV):
        vct = vct_ref[g * HEAD_DIM:(g + 1) * HEAD_DIM, :].astype(BF16)
        pcs = jnp.zeros((ncp, Q_BLOCK), F32)
        for r in range(R_GRP):
            h = g * R_GRP + r
            s = _dot(kc, qt_ref[:, h * Q_BLOCK:(h + 1) * Q_BLOCK]) + bc_ref[h]
            s = jnp.where(ok_c, s, NEG)
            e = jnp.where(ok_c, jnp.exp(s - jnp.max(s, axis=0, keepdims=True)), 0.0)
            p = e * (1.0 / jnp.maximum(jnp.sum(e, axis=0, keepdims=True), TINY))
            o_c.append(_dot(vct, p.astype(BF16)))
            pcs = pcs + p
        pcs_ref[...] = pcs
        per = SEL_BLOCK // CMP_STRIDE
        imp = pcs_ref[pl.ds(0, ncp // per, stride=per), :]
        for c in range(1, per):
            imp = imp + pcs_ref[pl.ds(c, ncp // per, stride=per), :]
        pick_ref[g] = _select_top(_block_scores(imp, blk_pos, n_sel, axis=0), min(N_SELECT, n_sel), axis=0)

    dmat = tok_i - key_i

    def stream(k_ref, vt_ref, n_chunks, chunk_of, mask_of):
        def scores(c, slot):
            off, _ = chunk_of(c)
            k = k_ref[pl.ds(off, KV_CHUNK), :].astype(BF16)
            for h in range(H_NSA):
                s_ref[slot, h] = _dot(k, qt_ref[:, h * Q_BLOCK:(h + 1) * Q_BLOCK])

        def values(c, slot, alpha_all):
            off, _ = chunk_of(c)
            for g in range(N_KV):
                vt = vt_ref[g * HEAD_DIM:(g + 1) * HEAD_DIM, pl.ds(off, KV_CHUNK)].astype(BF16)
                for h in range(g * R_GRP, (g + 1) * R_GRP):
                    acc_ref[h] = acc_ref[h] * alpha_all[h:h + 1] + _dot(vt, p_ref[slot, h])

        def softmax(c, slot, m_all, l_all):
            _, rel = chunk_of(c)
            ok_g = mask_of(c, rel)
            bias_idx = min(max(rel, 0), FAR_REL) if isinstance(rel, int) else jnp.clip(rel, 0, FAR_REL)
            alpha_all = jnp.zeros((H_NSA, Q_BLOCK), F32)
            for h in range(H_NSA):
                s = s_ref[slot, h] + bt_ref[bias_idx, h]
                m_new, l_new, alpha, p = _attend_t(s, ok_g[h // R_GRP], m_all[h:h + 1], l_all[h:h + 1])
                p_ref[slot, h] = p.astype(BF16)
                m_all = jnp.where(row8 == h, m_new, m_all)
                l_all = jnp.where(row8 == h, l_new, l_all)
                alpha_all = jnp.where(row8 == h, alpha, alpha_all)
            return m_all, l_all, alpha_all

        def stage(c, slot, carry, prefetch=True):
            m_all, l_all, alpha_prev = carry
            if prefetch:
                scores(c + 1, 1 - slot)
            values(c - 1, 1 - slot, alpha_prev)
            return softmax(c, slot, m_all, l_all)

        acc_ref[...] = jnp.zeros_like(acc_ref)
        p_ref[1] = jnp.zeros(p_ref.shape[1:], BF16)
        scores(0, 0)
        carry = (jnp.full((H_NSA, Q_BLOCK), NEG, F32), jnp.zeros((H_NSA, Q_BLOCK), F32),
                 jnp.ones((H_NSA, Q_BLOCK), F32))
        if isinstance(n_chunks, int):
            for c in range(n_chunks):
                carry = stage(c, c % 2, carry, prefetch=c + 1 < n_chunks)
            last = n_chunks - 1
        else:
            pairs = (n_chunks + 1) // 2
            carry = lax.fori_loop(0, pairs, lambda t, cr: stage(2 * t + 1, 1, stage(2 * t, 0, cr)), carry)
            last = 2 * pairs - 1
        m_all, l_all, alpha_last = carry
        values(last, last % 2 if isinstance(last, int) else 1, alpha_last)
        return m_all, l_all

    def sel_chunk(c):
        return pl.multiple_of(jnp.clip(c, 0, i) * KV_CHUNK, KV_CHUNK), i - c

    def sel_mask(c, rel):
        causal = rel * KV_CHUNK + dmat >= 0
        per = KV_CHUNK // SEL_BLOCK
        first = jnp.minimum(c, i) * per
        ok_g = []
        for g in range(N_KV):
            picked = jnp.concatenate(
                [jnp.broadcast_to(pick_ref[g, pl.ds(first + b, 1), :], (SEL_BLOCK, Q_BLOCK)) for b in range(per)],
                axis=0)
            ok_g.append(causal & (picked > 0.5))
        return ok_g

    m_s, l_s = stream(ks_ref, vst_ref, i + 1, sel_chunk, sel_mask)
    for h in range(H_NSA):
        os_ref[h] = jnp.where(m_s[h:h + 1] > 0.5 * NEG, acc_ref[h] / jnp.maximum(l_s[h:h + 1], TINY), 0.0)

    n_back = WINDOW // KV_CHUNK

    def win_chunk(c):
        return pl.multiple_of(jnp.clip(i - n_back + c, 0, i) * KV_CHUNK, KV_CHUNK), n_back - c

    def win_mask(c, rel):
        dist = rel * KV_CHUNK + dmat + jnp.where(i - n_back + c < 0, WINDOW, 0)
        ok = (dist >= 0) & (dist < WINDOW)
        return [ok, ok]

    m_w, l_w = stream(kw_ref, vwt_ref, n_back + 1, win_chunk, win_mask)

    for k2 in range(H_NSA // 2):
        pair = []
        for h in (2 * k2, 2 * k2 + 1):
            o_w = jnp.where(m_w[h:h + 1] > 0.5 * NEG, acc_ref[h] / jnp.maximum(l_w[h:h + 1], TINY), 0.0)
            pair.append(sig_t[3 * h:3 * h + 1] * o_c[h] + sig_t[3 * h + 1:3 * h + 2] * os_ref[h]
                        + sig_t[3 * h + 2:3 * h + 3] * o_w)
        o_ref[:, k2 * LANES:(k2 + 1) * LANES] = jnp.concatenate(pair, axis=0).T.astype(o_ref.dtype)


def _selection_consts(n_rows_cmp, n_keys):
    n = np.arange(n_rows_cmp)
    blk = np.arange(SEL_LANES)
    pool = (n[:, None] // (SEL_BLOCK // CMP_STRIDE) == blk[None, :]).astype(np.float32)
    key = np.arange(n_keys)
    expand = (blk[:, None] == key[None, :] // SEL_BLOCK).astype(np.float32)
    return jnp.asarray(pool, BF16), jnp.asarray(expand, BF16)


def nsa_prompt(z3, kvt, kc, vct, bias_ct, bias_tt):
    b, t, _ = z3.shape
    n_sub = t // CMP_STRIDE
    n_sel = -(-t // SEL_BLOCK)
    assert kc.shape[1] == n_sub and n_sub == n_sel * (SEL_BLOCK // CMP_STRIDE) and n_sel % SUBLANES == 0
    fixed = lambda shape: pl.BlockSpec(shape, lambda bi, i: (0,) * len(shape))
    v_rows = lambda branch: (branch * KV_W + N_KV * HEAD_DIM) // LANES
    return pl.pallas_call(
        functools.partial(_nsa_prompt_kernel, n_cmp=n_sub - 1, n_sel=n_sel),
        grid=(b, t // Q_BLOCK),
        in_specs=[pl.BlockSpec((None, Q_BLOCK, NSA_QW), lambda bi, i: (bi, i, COL_NQ // NSA_QW)),
                  pl.BlockSpec((None, Q_BLOCK, LANES), lambda bi, i: (bi, i, COL_NG // LANES)),
                  pl.BlockSpec((None, n_sub, LANES), lambda bi, i: (bi, 0, 0)),
                  pl.BlockSpec((None, LANES, n_sub), lambda bi, i: (bi, 0, 0)),
                  pl.BlockSpec((None, t, LANES), lambda bi, i: (bi, 0, COL_KVS // LANES)),
                  pl.BlockSpec((None, LANES, t), lambda bi, i: (bi, v_rows(1), 0)),
                  pl.BlockSpec((None, t, LANES), lambda bi, i: (bi, 0, COL_KVW // LANES)),
                  pl.BlockSpec((None, LANES, t), lambda bi, i: (bi, v_rows(2), 0)),
                  pl.BlockSpec((H_NSA, n_sub, Q_BLOCK), lambda bi, i: (0, 0, i)),
                  fixed(bias_tt.shape)],
        out_specs=pl.BlockSpec((None, Q_BLOCK, NSA_QW), lambda bi, i: (bi, i, 0)),
        out_shape=jax.ShapeDtypeStruct((b, t, NSA_QW), BF16),
        scratch_shapes=[pltpu.VMEM((LANES, H_NSA * Q_BLOCK), BF16),
                        pltpu.VMEM((n_sub, Q_BLOCK), F32),
                        pltpu.VMEM((N_KV, n_sel, Q_BLOCK), F32),
                        pltpu.VMEM((H_NSA, HEAD_DIM, Q_BLOCK), F32),
                        pltpu.VMEM((H_NSA, HEAD_DIM, Q_BLOCK), F32),
                        pltpu.VMEM((2, H_NSA, KV_CHUNK, Q_BLOCK), F32),
                        pltpu.VMEM((2, H_NSA, KV_CHUNK, Q_BLOCK), BF16)],
        compiler_params=_cparams(2),
        name="nsa_prompt",
    )(z3, z3, kc, vct, z3, kvt, z3, kvt, bias_ct, bias_tt)


def _nsa_step_kernel(pt_ref, z_ref, kc_ref, vc_ref, *rest, n_pages, past, wbuf, n_cmp, n_sel):
    del pt_ref
    pages = rest[:n_pages]
    win_ref, bcs_ref, bss_ref, bws_ref, pool_ref, exp_ref, o_ref, wo_ref = rest[n_pages:]
    z = z_ref[...]
    row8 = lax.broadcasted_iota(I32, (H_NSA, LANES), 0)
    lane8 = lax.broadcasted_iota(I32, (H_NSA, LANES), 1)
    left8 = lane8 < HEAD_DIM
    qbd = jnp.zeros((H_NSA, LANES), F32)
    for h in range(H_NSA):
        slab = jnp.broadcast_to(z[:, COL_NQ + (h // 2) * LANES:COL_NQ + (h // 2 + 1) * LANES], (H_NSA, LANES))
        g = h // R_GRP
        if h % 2 != g:
            slab = pltpu.roll(slab, HEAD_DIM, 1)
        keep = (row8 == h) & (left8 if g == 0 else jnp.logical_not(left8))
        qbd = jnp.where(keep, slab, qbd)
    qbd = (qbd * (HEAD_DIM ** -0.5)).astype(BF16)

    ncp = kc_ref.shape[0]
    n_c = lax.broadcasted_iota(I32, (H_NSA, ncp), 1)
    s_c = _dot_nt(qbd, kc_ref[...].astype(BF16)) + bcs_ref[...]
    p_c = _masked_softmax(s_c, (n_c * CMP_STRIDE + (L_CMP - 1) <= past) & (n_c < n_cmp))
    o_c = _dot(p_c.astype(BF16), vc_ref[...].astype(BF16))
    pc0 = jnp.sum(p_c[0:R_GRP], axis=0, keepdims=True)
    pc1 = jnp.sum(p_c[R_GRP:], axis=0, keepdims=True)
    rowc = lax.broadcasted_iota(I32, (H_NSA, ncp), 0)
    pcs = jnp.where(rowc < R_GRP, jnp.broadcast_to(pc0, (H_NSA, ncp)), jnp.broadcast_to(pc1, (H_NSA, ncp)))
    imp = _dot3(pcs, pool_ref[...])
    sel = _select_top(_block_scores(imp, jnp.full(imp.shape, past, I32), n_sel), min(N_SELECT, n_sel))
    picked = _dot(sel.astype(BF16), exp_ref[...])

    c128 = lax.broadcasted_iota(I32, (LANES, KV_CHUNK), 1)

    def new_col(col):
        return jnp.broadcast_to(z[:, col:col + LANES], (KV_CHUNK, LANES)).T

    def first_col(tile):
        return jnp.where(c128 == 0, tile, 0.0).astype(BF16)

    ks = [pg[0:LANES, :].astype(BF16) for pg in pages] + [first_col(new_col(COL_KVS))]
    vs = [pg[LANES:2 * LANES, :].astype(BF16) for pg in pages] + [first_col(new_col(COL_KVS + LANES))]
    s_s = jnp.concatenate([_dot(qbd, k) for k in ks], axis=1) + bss_ref[...]
    key = lax.broadcasted_iota(I32, s_s.shape, 1)
    p_s = _masked_softmax(s_s, (picked > 0.5) & (key <= past)).astype(BF16)
    o_s = _dot_nt(p_s[:, 0:KV_CHUNK], vs[0])
    for c in range(1, len(vs)):
        o_s = o_s + _dot_nt(p_s[:, c * KV_CHUNK:(c + 1) * KV_CHUNK], vs[c])

    kw_new, vw_new = new_col(COL_KVW), new_col(COL_KVW + LANES)
    win = win_ref[...]
    s_w = jnp.concatenate([_dot(qbd, win[0:LANES].astype(BF16)), _dot(qbd, first_col(kw_new))], axis=1) + bws_ref[...]
    colw = lax.broadcasted_iota(I32, s_w.shape, 1)
    p_w = _masked_softmax(s_w, (colw <= wbuf) & (wbuf - colw < WINDOW)).astype(BF16)
    o_w = _dot_nt(p_w[:, 0:wbuf], win[LANES:].astype(BF16)) + _dot_nt(p_w[:, wbuf:], first_col(vw_new))

    shifted = pltpu.roll(win, wbuf - 1, 1)
    new_kv = jnp.concatenate([kw_new, vw_new], axis=0)
    last = jnp.where(lax.broadcasted_iota(I32, (KV_W, LANES), 1) == LANES - 1, new_kv, shifted[:, wbuf - LANES:])
    wo_ref[:, 0:wbuf - LANES] = shifted[:, 0:wbuf - LANES]
    wo_ref[:, wbuf - LANES:] = last

    sig = jnp.broadcast_to(_sigmoid(z[:, COL_NG:COL_NG + LANES]), (H_NSA, LANES))

    def gate_col(c):
        return jnp.sum(jnp.where(lane8 == row8 * 3 + c, sig, 0.0), axis=1, keepdims=True)

    o = gate_col(0) * o_c + gate_col(1) * o_s + gate_col(2) * o_w
    fix = jnp.where((row8 % 2) != (row8 // R_GRP), pltpu.roll(o, HEAD_DIM, 1), o)
    left1 = left8[0:1]
    o_ref[...] = jnp.concatenate(
        [jnp.where(left1, fix[2 * k2:2 * k2 + 1], fix[2 * k2 + 1:2 * k2 + 2]) for k2 in range(H_NSA // 2)], axis=1)


def nsa_step(zs3, kc, vc, sel_cache, win_cache, page_idx, win_row0, bias_cs, bias_ss, bias_ws, past):
    bs = zs3.shape[0]
    n_pages = page_idx.shape[1]
    wbuf = win_cache.shape[2]
    n_sub = (past + 1) // CMP_STRIDE
    n_sel = -(-(past + 1) // SEL_BLOCK)
    n_keys = (n_pages + 1) * PAGE_SIZE
    assert n_sel <= SEL_LANES and PAGE_SIZE == KV_CHUNK and wbuf % KV_CHUNK == 0 and kc.shape[1] == n_sub
    pool, expand = _selection_consts(n_sub, n_keys)
    fixed = lambda shape: pl.BlockSpec(shape, lambda b, pt: (0,) * len(shape))
    page = lambda p: pl.BlockSpec((None, KV_W, PAGE_SIZE), lambda b, pt, p=p: (pt[b, p], 0, 0))
    grid_spec = pltpu.PrefetchScalarGridSpec(
        num_scalar_prefetch=1,
        grid=(bs,),
        in_specs=[pl.BlockSpec((None, 1, N_IN_PAD), lambda b, pt: (b, 0, 0)),
                  pl.BlockSpec((None, n_sub, LANES), lambda b, pt: (b, 0, 0)),
                  pl.BlockSpec((None, n_sub, LANES), lambda b, pt: (b, 0, 0))]
        + [page(p) for p in range(n_pages)]
        + [pl.BlockSpec((None, KV_W, wbuf), lambda b, pt: (win_row0 + b, 0, 0)),
           fixed(bias_cs.shape), fixed(bias_ss.shape), fixed(bias_ws.shape), fixed(pool.shape), fixed(expand.shape)],
        out_specs=[pl.BlockSpec((None, 1, NSA_QW), lambda b, pt: (b, 0, 0)),
                   pl.BlockSpec((None, KV_W, wbuf), lambda b, pt: (b, 0, 0))],
    )
    return pl.pallas_call(
        functools.partial(_nsa_step_kernel, n_pages=n_pages, past=past, wbuf=wbuf, n_cmp=n_sub - 1, n_sel=n_sel),
        grid_spec=grid_spec,
        out_shape=[jax.ShapeDtypeStruct((bs, 1, NSA_QW), F32), jax.ShapeDtypeStruct((bs, KV_W, wbuf), F32)],
        compiler_params=_cparams(1),
        name="nsa_step",
    )(page_idx, zs3, kc, vc, *([sel_cache] * n_pages), win_cache, bias_cs, bias_ss, bias_ws, pool, expand)


def _route(logits, tm):
    m = logits.shape[0]
    top_v, top_i = lax.top_k(logits, TOP_K)
    top_w = jax.nn.softmax(top_v, axis=-1)
    slot_e = top_i.reshape(-1)
    onehot = (slot_e[:, None] == jnp.arange(N_EXPERTS)[None, :]).astype(I32)
    before = jnp.cumsum(onehot, axis=0) - onehot
    rank = jnp.sum(before * onehot, axis=1)
    count = jnp.sum(onehot, axis=0)
    padded = -(-count // tm) * tm
    start = jnp.cumsum(padded) - padded
    pos = start[slot_e] + rank
    n_rows = (-(-(m * TOP_K) // tm) + N_EXPERTS) * tm
    row_token = jnp.zeros((n_rows,), I32).at[pos].set(jnp.arange(m * TOP_K, dtype=I32) // TOP_K)
    row_gate = jnp.zeros((n_rows,), F32).at[pos].set(top_w.reshape(-1))
    tile_start = jnp.arange(n_rows // tm, dtype=I32) * tm
    tile_expert = jnp.clip(jnp.searchsorted(jnp.cumsum(padded), tile_start, side='right'), 0, N_EXPERTS - 1)
    tile_valid = (tile_start < jnp.sum(padded)).astype(I32)
    return pos.reshape(m, TOP_K), row_token, row_gate.reshape(n_rows, 1), tile_expert.astype(I32), tile_valid


def _dense_ffn(h, g, w1, w3, w2):
    m = h.shape[0]
    tm = _row_tile(m)
    ones = jnp.ones((m // tm,), I32)
    return grouped_ffn(h, g, jnp.ones((m, 1), F32), jnp.zeros((m // tm,), I32), ones,
                       w1[None], w3[None], w2[None], tm=tm, tf=w1.shape[1] // 2, residual=True)


def kernel(x_prompt, x_sample, p_prompt, p_sample, state_ret, cache_win_kv, cache_cmp_kv, cache_sel_kv, page_table, rel_bias, norm_mix, w_in, ret_gain, cmp_pe, cmp_w1, cmp_w2, w_out, norm_ffn, ffn_w1, ffn_w3, ffn_w2, router, moe_w1, moe_w3, moe_w2, ple_norm, ple_gate, ple_proj, norm_final):
    b_p, t_p, d = x_prompt.shape
    b_s = x_sample.shape[0]
    depth = w_in.shape[0]
    n_pages = page_table.shape[1]
    past = n_pages * PAGE_SIZE
    n_pool = cache_cmp_kv.shape[1]
    w_buf = cache_win_kv.shape[2]
    assert x_sample.shape[1] == 1 and t_p % Q_BLOCK == 0 and t_p >= w_buf and past % CMP_STRIDE == 0
    row = (2, N_KV, HEAD_DIM)
    m_p = b_p * t_p

    ar = lambda n: jnp.arange(n, dtype=I32)
    n_sub_p = t_p // CMP_STRIDE
    bias_ct = bias_lookup(rel_bias, ar(t_p)[None, :] - (ar(n_sub_p)[:, None] * CMP_STRIDE + L_CMP - 1))
    rel_t = (ar(FAR_REL + 1)[:, None, None] * KV_CHUNK + ar(Q_BLOCK)[None, None, :] - ar(KV_CHUNK)[None, :, None])
    bias_tt = bias_lookup(rel_bias, rel_t.reshape((FAR_REL + 1) * KV_CHUNK, Q_BLOCK))
    bias_tt = bias_tt.reshape(H_NSA, FAR_REL + 1, KV_CHUNK, Q_BLOCK).transpose(1, 0, 2, 3)
    n_sub_s = (past + 1) // CMP_STRIDE
    rows8 = jnp.zeros((SUBLANES, 1), I32)
    bias_cs = bias_lookup(rel_bias, rows8 + (past - (ar(n_sub_s)[None, :] * CMP_STRIDE + L_CMP - 1)))[:, 0]
    bias_ss = bias_lookup(rel_bias, rows8 + (past - ar((n_pages + 1) * PAGE_SIZE)[None, :]))[:, 0]
    bias_ws = bias_lookup(rel_bias, rows8 + (w_buf - ar(w_buf + KV_CHUNK)[None, :]))[:, 0]

    def rows_last(x):
        return jnp.moveaxis(x, 2, -1).reshape(x.shape[0] * x.shape[1], KV_W, x.shape[2])

    def rows_first(xt, lead):
        return jnp.moveaxis(xt.reshape((lead,) + row + (xt.shape[-1],)), -1, 1)

    cmp_cache = rows_last(cache_cmp_kv)
    sel_cache = rows_last(cache_sel_kv)
    win_cache = rows_last(cache_win_kv)
    state_t = jnp.moveaxis(state_ret, 1, -1).reshape(depth * H_RET, HEAD_DIM, HEAD_DIM, b_s)

    h_p = x_prompt.reshape(m_p, d)
    h_s = x_sample.reshape(b_s, d)
    rp, rs, wp, ws, cp, cs, sp, ss = [], [], [], [], [], [], [], []
    for i in range(depth):
        w_in_b = jnp.pad(w_in[i], ((0, 0), (0, N_IN_PAD - N_IN))).astype(BF16)
        w_out_b = w_out[i].astype(BF16)
        pe_t, cw1, cw2 = _compress_weights(cmp_pe[i], cmp_w1[i], cmp_w2[i])

        w_kvt = w_in[i][:, COL_KVC:COL_KVC + 3 * KV_W].T.astype(BF16)
        z, kvt = rms_matmul_kvt(h_p, norm_mix[i], w_in_b, w_kvt, b_p)
        z3 = z.reshape(b_p, t_p, N_IN_PAD)
        ret_y, ret_state = retention_prompt(z3, ret_gain[i])
        kc, vc = compress(kvt, ar(b_p)[:, None], pe_t, cw1, cw2)
        nsa_y = nsa_prompt(z3, kvt, kc, jnp.swapaxes(vc, 1, 2), bias_ct, bias_tt)
        h_p = out_proj(h_p, ret_y.reshape(m_p, RET_W), nsa_y.reshape(m_p, NSA_QW), w_out_b)
        rp.append(ret_state)
        cp.append(rows_first(kvt[:, 0:KV_W], b_p))
        sp.append(rows_first(kvt[:, KV_W:2 * KV_W], b_p))
        wp.append(rows_first(kvt[:, 2 * KV_W:, t_p - w_buf:], b_p))

        zs = rms_matmul(h_s, norm_mix[i], w_in_b)
        zst = zs.T
        ret_yst, state_s = retention_step(zst, state_t, i, past, ret_gain[i])
        page_idx = page_table + i * n_pool
        grp = 4 if b_s % 4 == 0 else 1
        kc_s, vc_s = compress(cmp_cache, page_idx.reshape(b_s // grp, grp * n_pages), pe_t, cw1, cw2)
        kc_s, vc_s = kc_s.reshape(b_s, n_sub_s, LANES), vc_s.reshape(b_s, n_sub_s, LANES)
        nsa_ys, win_s = nsa_step(zs.reshape(b_s, 1, N_IN_PAD), kc_s, vc_s, sel_cache, win_cache, page_idx, i * b_s,
                                 bias_cs, bias_ss, bias_ws, past)
        h_s = out_proj(h_s, ret_yst.T, nsa_ys.reshape(b_s, NSA_QW), w_out_b)
        rs.append(jnp.moveaxis(state_s, -1, 0))
        ws.append(rows_first(win_s, b_s))
        cs.append(rows_first(zst[None, COL_KVC:COL_KVC + KV_W], 1).reshape((b_s, 1) + row))
        ss.append(rows_first(zst[None, COL_KVS:COL_KVS + KV_W], 1).reshape((b_s, 1) + row))

        j = i // 2
        if i % 2 == 0:
            w1, w3, w2 = ffn_w1[j].astype(BF16), ffn_w3[j].astype(BF16), ffn_w2[j].astype(BF16)
            h_p = _dense_ffn(h_p, norm_ffn[i], w1, w3, w2)
            h_s = _dense_ffn(h_s, norm_ffn[i], w1, w3, w2)
        else:
            router_w = jnp.pad(router[j], ((0, 0), (0, LANES - N_EXPERTS)))
            tf = moe_w1.shape[3] // 7
            logits_p = router_logits(h_p, norm_ffn[i], router_w)
            tm = 1024
            pos, row_token, row_gate, tile_expert, tile_valid = _route(logits_p[:, :N_EXPERTS], tm)
            y_rows = grouped_ffn(gather_rows(h_p, row_token), norm_ffn[i], row_gate, tile_expert, tile_valid,
                                 moe_w1[j], moe_w3[j], moe_w2[j], tm=tm, tf=tf, residual=False)
            h_p = h_p + y_rows[pos[:, 0]] + y_rows[pos[:, 1]]
            logits_s = router_logits(h_s, norm_ffn[i], router_w)
            top_v, top_i = lax.top_k(logits_s[:, :N_EXPERTS], TOP_K)
            top_w = jax.nn.softmax(top_v, axis=-1)
            gates = jnp.sum(jax.nn.one_hot(top_i, N_EXPERTS, dtype=F32) * top_w[..., None], axis=-2)
            y_e = grouped_ffn(jnp.tile(h_s, (N_EXPERTS, 1)), norm_ffn[i], gates.T.reshape(N_EXPERTS * b_s, 1),
                              jnp.arange(N_EXPERTS, dtype=I32), jnp.ones((N_EXPERTS,), I32),
                              moe_w1[j], moe_w3[j], moe_w2[j], tm=b_s, tf=tf, residual=False)
            h_s = h_s + jnp.sum(y_e.reshape(N_EXPERTS, b_s, d), axis=0)

        wg, wpj = ple_gate[i].astype(BF16), ple_proj[i].astype(BF16)
        h_p = ple(h_p, ple_norm[i], wg, p_prompt[i].reshape(m_p, P_DIM), wpj, norm_final, i == depth - 1)
        h_s = ple(h_s, ple_norm[i], wg, p_sample[i].reshape(b_s, P_DIM), wpj, norm_final, i == depth - 1)

    return (h_p.reshape(b_p, t_p, d), h_s.reshape(b_s, 1, d), jnp.stack(rp), jnp.stack(rs), jnp.stack(wp),
            jnp.stack(ws), jnp.stack(cp), jnp.stack(cs), jnp.stack(sp), jnp.stack(ss))
```

```python
import functools
import math

import numpy as np
import jax
import jax.numpy as jnp
from jax import lax
from jax.experimental import pallas as pl
from jax.experimental.pallas import tpu as pltpu

F32 = jnp.float32
BF16 = jnp.bfloat16
I32 = jnp.int32

LANES = 128
SUBLANES = 8
VMEM_LIMIT_BYTES = 56 * 1024 * 1024

D_MODEL = 1024
HEAD_DIM = 64
H_RET = 8
H_NSA = 8
N_KV = 2
R_GRP = H_NSA // N_KV
RET_W = H_RET * HEAD_DIM
NSA_QW = H_NSA * HEAD_DIM
KV_W = 2 * N_KV * HEAD_DIM
N_IN = 4 * RET_W + NSA_QW + 3 * KV_W + 3 * H_NSA
N_IN_PAD = 3584
COL_RQ, COL_RK, COL_RV, COL_RG = 0, RET_W, 2 * RET_W, 3 * RET_W
COL_NQ = 4 * RET_W
COL_KVC = COL_NQ + NSA_QW
COL_KVS = COL_KVC + KV_W
COL_KVW = COL_KVS + KV_W
COL_NG = COL_KVW + KV_W
RET_CHUNK = 128
CMP_STRIDE = 16
L_CMP = 2 * CMP_STRIDE
CMP_HIDDEN = 2 * HEAD_DIM
SEL_BLOCK = 64
N_SELECT = 16
WINDOW = 512
Q_BLOCK = 128
N_BUCKETS = 32
MAX_DISTANCE = 128
N_EXPERTS = 8
TOP_K = 2
P_DIM = 256
PAGE_SIZE = 128
ROPE_BASE = 10000.0
EPS = 1e-6
FORCED_SCORE = 1e4
NEG = -1e30
TINY = float(np.finfo(np.float32).tiny)
SEL_LANES = 64
FAR_REL = 2
KV_CHUNK = 128


def _cparams(n_grid):
    return pltpu.CompilerParams(dimension_semantics=("arbitrary",) * n_grid,
                                vmem_limit_bytes=VMEM_LIMIT_BYTES)


def _dot(a, b):
    return jnp.dot(a, b, preferred_element_type=F32)


def _dot_nt(a, b):
    return lax.dot_general(a, b, (((1,), (1,)), ((), ())), preferred_element_type=F32)


def _split3(x):
    hi = x.astype(BF16)
    r = x - hi.astype(F32)
    mid = r.astype(BF16)
    lo = (r - mid.astype(F32)).astype(BF16)
    return hi, mid, lo


def _dot3(x, b01):
    hi, mid, lo = _split3(x)
    return _dot(hi, b01) + _dot(mid, b01) + _dot(lo, b01)


def _dot6(x, w):
    xh, xm, xl = _split3(x)
    wh, wm, wl = _split3(w)
    return (_dot(xh, wh) + _dot(xh, wm) + _dot(xm, wh)
            + _dot(xm, wm) + _dot(xh, wl) + _dot(xl, wh))


def _sigmoid(x):
    return 1.0 / (1.0 + jnp.exp(-x))


def _rms(x, g):
    return x * lax.rsqrt(jnp.mean(x * x, axis=-1, keepdims=True) + EPS) * g


def _row_tile(m):
    for t in (512, 256, 128, 64, 32, 16, 8):
        if m % t == 0:
            return t
    raise ValueError(f"row count {m} not a multiple of 8")


def _rms_matmul_kernel(x_ref, g_ref, w_ref, o_ref):
    o_ref[...] = _dot(_rms(x_ref[...], g_ref[...]).astype(BF16), w_ref[...])


def rms_matmul(x, g, w):
    m, d = x.shape
    n = w.shape[1]
    tm = min(_row_tile(m), 256)
    return pl.pallas_call(
        _rms_matmul_kernel,
        grid=(m // tm,),
        in_specs=[pl.BlockSpec((tm, d), lambda i: (i, 0)),
                  pl.BlockSpec((1, d), lambda i: (0, 0)),
                  pl.BlockSpec((d, n), lambda i: (0, 0))],
        out_specs=pl.BlockSpec((tm, n), lambda i: (i, 0)),
        out_shape=jax.ShapeDtypeStruct((m, n), F32),
        compiler_params=_cparams(1),
    )(x, g.reshape(1, d), w)


def _rms_matmul_kvt_kernel(x_ref, g_ref, w_ref, wt_ref, o_ref, ot_ref):
    xn = _rms(x_ref[...], g_ref[...]).astype(BF16)
    o_ref[...] = _dot(xn, w_ref[...])
    ot_ref[...] = _dot_nt(wt_ref[...], xn)


def rms_matmul_kvt(x, g, w, wt, bsz):
    m, d = x.shape
    n = w.shape[1]
    nt = wt.shape[0]
    t = m // bsz
    tm = min(_row_tile(t), 256)
    per = t // tm
    return pl.pallas_call(
        _rms_matmul_kvt_kernel,
        grid=(m // tm,),
        in_specs=[pl.BlockSpec((tm, d), lambda i: (i, 0)),
                  pl.BlockSpec((1, d), lambda i: (0, 0)),
                  pl.BlockSpec((d, n), lambda i: (0, 0)),
                  pl.BlockSpec((nt, d), lambda i: (0, 0))],
        out_specs=[pl.BlockSpec((tm, n), lambda i: (i, 0)),
                   pl.BlockSpec((None, nt, tm), lambda i: (i // per, 0, i % per))],
        out_shape=[jax.ShapeDtypeStruct((m, n), F32), jax.ShapeDtypeStruct((bsz, nt, t), F32)],
        compiler_params=_cparams(1),
        name="rms_matmul_kvt",
    )(x, g.reshape(1, d), w, wt)


def _router_kernel(x_ref, g_ref, w_ref, o_ref):
    o_ref[...] = _dot6(_rms(x_ref[...], g_ref[...]), w_ref[...])


def router_logits(x, g, w):
    m, d = x.shape
    n = w.shape[1]
    tm = min(_row_tile(m), 256)
    return pl.pallas_call(
        _router_kernel,
        grid=(m // tm,),
        in_specs=[pl.BlockSpec((tm, d), lambda i: (i, 0)),
                  pl.BlockSpec((1, d), lambda i: (0, 0)),
                  pl.BlockSpec((d, n), lambda i: (0, 0))],
        out_specs=pl.BlockSpec((tm, n), lambda i: (i, 0)),
        out_shape=jax.ShapeDtypeStruct((m, n), F32),
        compiler_params=_cparams(1),
        name="router",
    )(x, g.reshape(1, d), w)


def _outproj_kernel(h_ref, a_ref, b_ref, w_ref, o_ref):
    ka = a_ref.shape[1]
    o_ref[...] = (h_ref[...] + _dot(a_ref[...].astype(BF16), w_ref[0:ka, :])
                  + _dot(b_ref[...].astype(BF16), w_ref[ka:, :]))


def out_proj(h, a, b, w):
    m, d = h.shape
    ka, kb = a.shape[1], b.shape[1]
    tm = _row_tile(m)
    return pl.pallas_call(
        _outproj_kernel,
        grid=(m // tm,),
        in_specs=[pl.BlockSpec((tm, d), lambda i: (i, 0)),
                  pl.BlockSpec((tm, ka), lambda i: (i, 0)),
                  pl.BlockSpec((tm, kb), lambda i: (i, 0)),
                  pl.BlockSpec((ka + kb, d), lambda i: (0, 0))],
        out_specs=pl.BlockSpec((tm, d), lambda i: (i, 0)),
        out_shape=jax.ShapeDtypeStruct((m, d), F32),
        compiler_params=_cparams(1),
        name="out_proj",
    )(h, a, b, w)


def _ffn_kernel(te_ref, tv_ref, x_ref, g_ref, gate_ref, w1_ref, w3_ref, w2_ref, o_ref,
                xn_ref, acc_ref, *, residual):
    i = pl.program_id(0)
    j = pl.program_id(1)

    @pl.when(j == 0)
    def _():
        if len(x_ref.shape) == 3:
            x = jnp.concatenate([x_ref[:, s, :] for s in range(x_ref.shape[1])], axis=1)
        else:
            x = x_ref[...]
        xn_ref[...] = _rms(x, g_ref[...]).astype(BF16)
        acc_ref[...] = jnp.zeros_like(acc_ref)

    @pl.when(tv_ref[i] != 0)
    def _():
        xn = xn_ref[...]
        a = _dot(xn, w1_ref[...].astype(BF16))
        b = _dot(xn, w3_ref[...].astype(BF16))
        hid = (a * _sigmoid(a)) * b
        acc_ref[...] += _dot(hid.astype(BF16), w2_ref[...].astype(BF16))

    @pl.when(j == pl.num_programs(1) - 1)
    def _():
        y = acc_ref[...] * gate_ref[...]
        if residual:
            y = y + x_ref[...]
        o_ref[...] = y


def grouped_ffn(x, g, row_gate, tile_expert, tile_valid, w1, w3, w2, *, tm, tf, residual):
    m = x.shape[0]
    d = w1.shape[1]
    f = w1.shape[2]
    assert not (residual and x.ndim == 3)
    x_spec = (pl.BlockSpec((tm,) + x.shape[1:], lambda i, j, te, tv: (i, 0, 0)) if x.ndim == 3
              else pl.BlockSpec((tm, d), lambda i, j, te, tv: (i, 0)))
    grid_spec = pltpu.PrefetchScalarGridSpec(
        num_scalar_prefetch=2,
        grid=(m // tm, f // tf),
        in_specs=[x_spec,
                  pl.BlockSpec((1, d), lambda i, j, te, tv: (0, 0)),
                  pl.BlockSpec((tm, 1), lambda i, j, te, tv: (i, 0)),
                  pl.BlockSpec((None, d, tf), lambda i, j, te, tv: (te[i], 0, j)),
                  pl.BlockSpec((None, d, tf), lambda i, j, te, tv: (te[i], 0, j)),
                  pl.BlockSpec((None, tf, d), lambda i, j, te, tv: (te[i], j, 0))],
        out_specs=pl.BlockSpec((tm, d), lambda i, j, te, tv: (i, 0)),
        scratch_shapes=[pltpu.VMEM((tm, d), BF16), pltpu.VMEM((tm, d), F32)],
    )
    return pl.pallas_call(
        functools.partial(_ffn_kernel, residual=residual),
        grid_spec=grid_spec,
        out_shape=jax.ShapeDtypeStruct((m, d), F32),
        compiler_params=_cparams(2),
        name="grouped_ffn",
    )(tile_expert, tile_valid, x, g.reshape(1, d), row_gate, w1, w3, w2)


def _gather_rows_kernel(idx_ref, x_hbm, o_ref, sem):
    base = pl.program_id(0) * o_ref.shape[0]

    def row_copy(k):
        return pltpu.make_async_copy(x_hbm.at[idx_ref[base + k]], o_ref.at[k], sem)

    def start(k, carry):
        row_copy(k).start()
        return carry

    def wait(k, carry):
        row_copy(k).wait()
        return carry

    lax.fori_loop(0, o_ref.shape[0], start, 0)
    lax.fori_loop(0, o_ref.shape[0], wait, 0)


def gather_rows(x, idx, rows_per_step=512):
    n = idx.shape[0]
    assert n % rows_per_step == 0 and x.dtype == F32 and x.shape[1:] == (SUBLANES, LANES)
    grid_spec = pltpu.PrefetchScalarGridSpec(
        num_scalar_prefetch=1,
        grid=(n // rows_per_step,),
        in_specs=[pl.BlockSpec(memory_space=pl.ANY)],
        out_specs=pl.BlockSpec((rows_per_step, SUBLANES, LANES), lambda i, ix: (i, 0, 0)),
        scratch_shapes=[pltpu.SemaphoreType.DMA],
    )
    return pl.pallas_call(
        _gather_rows_kernel,
        grid_spec=grid_spec,
        out_shape=jax.ShapeDtypeStruct((n, SUBLANES, LANES), x.dtype),
        compiler_params=_cparams(1),
        name="gather_rows",
    )(idx, x)


def _ple_kernel(h_ref, g_ref, wg_ref, p_ref, wp_ref, gf_ref, o_ref, *, final):
    h = h_ref[...]
    gate = _sigmoid(_dot(_rms(h, g_ref[...]).astype(BF16), wg_ref[...]))
    proj = _dot(p_ref[...].astype(BF16), wp_ref[...])
    h2 = h + gate * proj
    o_ref[...] = _rms(h2, gf_ref[...]) if final else h2


def ple(h, g, wg, p, wp, gf, final):
    m, d = h.shape
    pd = p.shape[1]
    tm = _row_tile(m)
    row = lambda i: (i, 0)
    fixed = lambda i: (0, 0)
    return pl.pallas_call(
        functools.partial(_ple_kernel, final=final),
        grid=(m // tm,),
        in_specs=[pl.BlockSpec((tm, d), row), pl.BlockSpec((1, d), fixed),
                  pl.BlockSpec((d, d), fixed), pl.BlockSpec((tm, pd), row),
                  pl.BlockSpec((pd, d), fixed), pl.BlockSpec((1, d), fixed)],
        out_specs=pl.BlockSpec((tm, d), row),
        out_shape=jax.ShapeDtypeStruct((m, d), F32),
        compiler_params=_cparams(1),
        name="ple",
    )(h, g.reshape(1, d), wg, p, wp, gf.reshape(1, d))


def _bias_kernel(tab_ref, d_ref, o_ref):
    n = jnp.maximum(d_ref[...], 0)
    max_exact = N_BUCKETS // 2
    nf = jnp.maximum(n, 1).astype(F32)
    large = max_exact + (jnp.log(nf / max_exact) / math.log(MAX_DISTANCE / max_exact)
                         * (N_BUCKETS - max_exact)).astype(I32)
    large = jnp.minimum(large, N_BUCKETS - 1)
    bucket = jnp.where(n < max_exact, n, large)
    for h in range(H_NSA):
        acc = jnp.zeros(n.shape, F32)
        for k in range(N_BUCKETS):
            acc = jnp.where(bucket == k, tab_ref[k, h], acc)
        o_ref[h] = acc


def bias_lookup(table, dist):
    r, c = dist.shape
    tr = min(r, 128)
    tc = 512 if c % 512 == 0 else c
    return pl.pallas_call(
        _bias_kernel,
        grid=(r // tr, c // tc),
        in_specs=[pl.BlockSpec(memory_space=pltpu.SMEM),
                  pl.BlockSpec((tr, tc), lambda i, j: (i, j))],
        out_specs=pl.BlockSpec((H_NSA, tr, tc), lambda i, j: (0, i, j)),
        out_shape=jax.ShapeDtypeStruct((H_NSA, r, c), F32),
        compiler_params=_cparams(2),
        name="bias_lookup",
    )(table, dist)


def _swap_halves(x):
    lane = lax.broadcasted_iota(I32, x.shape, 1)
    return jnp.where(lane % HEAD_DIM < HEAD_DIM // 2,
                     pltpu.roll(x, LANES - HEAD_DIM // 2, 1), pltpu.roll(x, HEAD_DIM // 2, 1))


def _retention_prompt_kernel(q_ref, k_ref, v_ref, g_ref, cos_ref, sin_ref, gain_ref, intra_ref,
                             qdec_ref, kdec_ref, cdec_ref, y_ref, s_out_ref, s_ref):
    c = pl.program_id(1)

    @pl.when(c == 0)
    def _():
        s_ref[...] = jnp.zeros_like(s_ref)

    cos = cos_ref[...]
    sin = sin_ref[...]
    lane = lax.broadcasted_iota(I32, (RET_CHUNK, LANES), 1)
    left = lane < HEAD_DIM
    row = lax.broadcasted_iota(I32, (LANES, LANES), 0)
    col = lax.broadcasted_iota(I32, (LANES, LANES), 1)
    same_head = (row < HEAD_DIM) == (col < HEAD_DIM)
    ones_bd = jnp.where(same_head, 1.0, 0.0).astype(BF16)
    for p in range(H_RET // 2):
        sl = slice(p * LANES, (p + 1) * LANES)
        q = q_ref[:, sl]
        k = k_ref[:, sl]
        v = v_ref[:, sl].astype(BF16)
        qr = q * cos + _swap_halves(q) * sin
        kr = (k * cos + _swap_halves(k) * sin) * (HEAD_DIM ** -0.5)
        qb = qr.astype(BF16)
        q2 = jnp.concatenate([jnp.where(left, qr, 0.0), jnp.where(left, 0.0, qr)], axis=0).astype(BF16)
        inner = _dot_nt(q2, kr.astype(BF16)) * intra_ref[p]
        o2 = _dot(inner.astype(BF16), v)
        s_old = s_ref[p]
        cross = _dot(qb, s_old.astype(BF16)) * qdec_ref[:, sl]
        o = jnp.where(left, o2[:RET_CHUNK], o2[RET_CHUNK:]) + cross
        kd = (kr * kdec_ref[:, sl]).T.astype(BF16)
        s_ref[p] = s_old * cdec_ref[p] + jnp.where(same_head, _dot(kd, v), 0.0)
        ms = _dot3(o * o, ones_bd) * (1.0 / HEAD_DIM)
        gate = g_ref[:, sl]
        y = o * lax.rsqrt(ms + EPS) * gain_ref[:, sl] * (gate * _sigmoid(gate))
        y_ref[:, sl] = y.astype(y_ref.dtype)

    @pl.when(c == pl.num_programs(1) - 1)
    def _():
        for p in range(H_RET // 2):
            s = s_ref[p]
            s_out_ref[2 * p] = s[:HEAD_DIM, :HEAD_DIM]
            s_out_ref[2 * p + 1] = s[HEAD_DIM:, HEAD_DIM:]


def _retention_consts(c):
    log_g = jnp.log1p(-jnp.exp2(-5.0 - jnp.arange(H_RET, dtype=F32)))
    j = jnp.arange(c, dtype=F32)
    diff = j[:, None] - j[None, :]
    intra = jnp.where(diff >= 0, jnp.exp(jnp.maximum(diff, 0.0)[None] * log_g[:, None, None]), 0.0)
    q_decay = jnp.exp((j[:, None] + 1.0) * log_g[None, :])
    k_decay = jnp.exp((c - 1.0 - j[:, None]) * log_g[None, :])
    c_decay = jnp.exp(c * log_g)
    intra2 = intra.reshape(H_RET // 2, 2 * c, c)
    qdec = jnp.repeat(q_decay, HEAD_DIM, axis=1)
    kdec = jnp.repeat(k_decay, HEAD_DIM, axis=1)
    cdec = jnp.repeat(c_decay.reshape(H_RET // 2, 2), HEAD_DIM, axis=1)[:, :, None] * jnp.ones((1, 1, LANES), F32)
    return intra2, qdec, kdec, cdec


def _rotary_tables(pos):
    half = HEAD_DIM // 2
    inv = ROPE_BASE ** (-jnp.arange(half, dtype=F32) / half)
    ang = pos.astype(F32)[:, None] * inv[None, :]
    cos, sin = jnp.cos(ang), jnp.sin(ang)
    cos_t = jnp.concatenate([cos, cos, cos, cos], axis=1)
    sin_t = jnp.concatenate([-sin, sin, -sin, sin], axis=1)
    return cos_t, sin_t


def retention_prompt(z3, gain):
    b, t, _ = z3.shape
    c = RET_CHUNK
    cos_t, sin_t = _rotary_tables(jnp.arange(t))
    intra2, qdec, kdec, cdec = _retention_consts(c)
    blk = lambda col: pl.BlockSpec((None, c, RET_W), lambda bi, ci, col=col: (bi, ci, col))
    fixed2 = lambda shape: pl.BlockSpec(shape, lambda bi, ci: (0,) * len(shape))
    return pl.pallas_call(
        _retention_prompt_kernel,
        grid=(b, t // c),
        in_specs=[blk(COL_RQ // RET_W), blk(COL_RK // RET_W), blk(COL_RV // RET_W), blk(COL_RG // RET_W),
                  pl.BlockSpec((c, LANES), lambda bi, ci: (ci, 0)),
                  pl.BlockSpec((c, LANES), lambda bi, ci: (ci, 0)),
                  fixed2((1, RET_W)), fixed2((H_RET // 2, 2 * c, c)), fixed2((c, RET_W)),
                  fixed2((c, RET_W)), fixed2((H_RET // 2, LANES, LANES))],
        out_specs=[pl.BlockSpec((None, c, RET_W), lambda bi, ci: (bi, ci, 0)),
                   pl.BlockSpec((None, H_RET, HEAD_DIM, HEAD_DIM), lambda bi, ci: (bi, 0, 0, 0))],
        out_shape=[jax.ShapeDtypeStruct((b, t, RET_W), BF16),
                   jax.ShapeDtypeStruct((b, H_RET, HEAD_DIM, HEAD_DIM), F32)],
        scratch_shapes=[pltpu.VMEM((H_RET // 2, LANES, LANES), F32)],
        compiler_params=_cparams(2),
        name="retention_prompt",
    )(z3, z3, z3, z3, cos_t, sin_t, gain.reshape(1, RET_W), intra2, qdec, kdec, cdec)


def _retention_step_kernel(q_ref, k_ref, v_ref, g_ref, s_ref, cos_ref, sin_ref, gamma_ref, gain_ref,
                           y_ref, s_out_ref):
    half = HEAD_DIM // 2
    cos = cos_ref[...]
    sin = sin_ref[...]

    def rot(x):
        x1, x2 = x[:half], x[half:]
        return jnp.concatenate([x1 * cos - x2 * sin, x1 * sin + x2 * cos], axis=0)

    qr = rot(q_ref[...])
    kr = rot(k_ref[...]) * (HEAD_DIM ** -0.5)
    v = v_ref[...]
    gamma = gamma_ref[...]
    o = jnp.zeros_like(v)
    for d in range(HEAD_DIM):
        s_new = s_ref[d] * gamma + kr[d:d + 1] * v
        s_out_ref[d] = s_new
        o = o + qr[d:d + 1] * s_new
    ms = jnp.mean(o * o, axis=0, keepdims=True)
    gate = g_ref[...]
    y_ref[...] = o * lax.rsqrt(ms + EPS) * gain_ref[...] * (gate * _sigmoid(gate))


def retention_step(zt, state, layer, pos, gain):
    bs = zt.shape[1]
    half = HEAD_DIM // 2
    inv = ROPE_BASE ** (-jnp.arange(half, dtype=F32) / half)
    ang = jnp.full((1,), pos).astype(F32)[:, None] * inv[None, :]
    cos = jnp.broadcast_to(jnp.cos(ang).reshape(half, 1), (half, bs))
    sin = jnp.broadcast_to(jnp.sin(ang).reshape(half, 1), (half, bs))
    gamma = 1.0 - jnp.exp2(-5.0 - jnp.arange(H_RET, dtype=F32))
    gamma = jnp.broadcast_to(gamma.reshape(H_RET, 1, 1), (H_RET, 1, bs))
    gain_b = jnp.broadcast_to(gain.reshape(H_RET, HEAD_DIM, 1), (H_RET, HEAD_DIM, bs))
    head = lambda col: pl.BlockSpec((HEAD_DIM, bs), lambda h, col=col: (col // HEAD_DIM + h, 0))
    fixed = pl.BlockSpec((half, bs), lambda h: (0, 0))
    return pl.pallas_call(
        _retention_step_kernel,
        grid=(H_RET,),
        in_specs=[head(COL_RQ), head(COL_RK), head(COL_RV), head(COL_RG),
                  pl.BlockSpec((None, HEAD_DIM, HEAD_DIM, bs), lambda h: (layer * H_RET + h, 0, 0, 0)),
                  fixed, fixed,
                  pl.BlockSpec((None, 1, bs), lambda h: (h, 0, 0)),
                  pl.BlockSpec((None, HEAD_DIM, bs), lambda h: (h, 0, 0))],
        out_specs=[pl.BlockSpec((HEAD_DIM, bs), lambda h: (h, 0)),
                   pl.BlockSpec((None, HEAD_DIM, HEAD_DIM, bs), lambda h: (h, 0, 0, 0))],
        out_shape=[jax.ShapeDtypeStruct((RET_W, bs), F32),
                   jax.ShapeDtypeStruct((H_RET, HEAD_DIM, HEAD_DIM, bs), F32)],
        compiler_params=_cparams(1),
        name="retention_step",
    )(zt, zt, zt, zt, state, cos, sin, gamma, gain_b)


def _gelu_tanh(x):
    return 0.5 * x * (1.0 + jnp.tanh(math.sqrt(2.0 / math.pi) * (x + 0.044715 * (x * x * x))))


def _compress_kernel(*refs, n_pieces):
    x_refs = refs[1:1 + n_pieces]
    pe_ref, w1_ref, w2_ref, kc_ref, vc_ref, rows_ref = refs[1 + n_pieces:]
    width = x_refs[0].shape[1]
    n = n_pieces * width // CMP_STRIDE
    for kv, o_ref in enumerate((kc_ref, vc_ref)):
        for p, x_ref in enumerate(x_refs):
            for c in range(width // LANES):
                rows_ref[pl.ds(p * width + c * LANES, LANES), :] = (
                    x_ref[kv * LANES:(kv + 1) * LANES, c * LANES:(c + 1) * LANES].T)
        xk = jnp.concatenate([rows_ref[pl.ds(j, n, stride=CMP_STRIDE), :] for j in range(CMP_STRIDE)], axis=1)
        top = _dot((xk + pe_ref[kv, 0]).astype(BF16), w1_ref[kv, 0])
        bot = _dot((xk + pe_ref[kv, 1]).astype(BF16), w1_ref[kv, 1])
        pre = top + pltpu.roll(bot, n - 1, 0)
        hid = _gelu_tanh(pre).astype(BF16)
        o_ref[...] = jnp.concatenate(
            [_dot(hid[:, g * CMP_HIDDEN:(g + 1) * CMP_HIDDEN], w2_ref[kv]) for g in range(N_KV)], axis=1)


def _compress_weights(pe, w1, w2):
    pe_r = pe.reshape(2, 2, CMP_STRIDE, 1, HEAD_DIM)
    pe_t = jnp.broadcast_to(pe_r, (2, 2, CMP_STRIDE, N_KV, HEAD_DIM)).reshape(2, 2, 1, CMP_STRIDE * LANES)
    w1r = w1.reshape(2, 2, CMP_STRIDE, HEAD_DIM, CMP_HIDDEN)
    eye = jnp.eye(N_KV, dtype=w1.dtype)
    wbig = jnp.einsum('khjdc,gG->khjgdGc', w1r, eye)
    wbig = wbig.reshape(2, 2, CMP_STRIDE * LANES, N_KV * CMP_HIDDEN)
    return pe_t.astype(F32), wbig.astype(BF16), w2.astype(BF16)


def compress(src, idx, pe_t, w1b, w2b):
    bsz, n_pieces = idx.shape
    width = src.shape[2]
    n = n_pieces * width // CMP_STRIDE
    piece = lambda p: pl.BlockSpec((None, KV_W, width), lambda b, ix, p=p: (ix[b, p], 0, 0))
    fixed = lambda shape: pl.BlockSpec(shape, lambda b, ix: (0,) * len(shape))
    out = pl.BlockSpec((None, n, LANES), lambda b, ix: (b, 0, 0))
    grid_spec = pltpu.PrefetchScalarGridSpec(
        num_scalar_prefetch=1,
        grid=(bsz,),
        in_specs=[piece(p) for p in range(n_pieces)]
        + [fixed(pe_t.shape), fixed(w1b.shape), fixed(w2b.shape)],
        out_specs=[out, out],
        scratch_shapes=[pltpu.VMEM((n_pieces * width, LANES), F32)],
    )
    return pl.pallas_call(
        functools.partial(_compress_kernel, n_pieces=n_pieces),
        grid_spec=grid_spec,
        out_shape=[jax.ShapeDtypeStruct((bsz, n, LANES), F32)] * 2,
        compiler_params=_cparams(1),
        name="compress",
    )(idx, *([src] * n_pieces), pe_t, w1b, w2b)


def _masked_softmax(s, mask):
    s = jnp.where(mask, s, NEG)
    m = jnp.max(s, axis=-1, keepdims=True)
    e = jnp.where(mask, jnp.exp(s - m), 0.0)
    return e / jnp.maximum(jnp.sum(e, axis=-1, keepdims=True), TINY)


def _block_scores(imp, qpos, n_sel, axis=1):
    blk = lax.broadcasted_iota(I32, imp.shape, axis)
    cur = qpos // SEL_BLOCK
    forced = (blk == 0) | (blk == cur) | (blk == cur - 1)
    valid = blk * SEL_BLOCK <= qpos
    score = jnp.where(forced, FORCED_SCORE, jnp.where(valid, imp, -1.0))
    return jnp.where(blk < n_sel, score, -2.0)


def _select_top(score, n_top, axis=1):
    blk = lax.broadcasted_iota(I32, score.shape, axis)
    rank = jnp.zeros(score.shape, F32)
    for i in range(score.shape[axis]):
        one = score[:, i:i + 1] if axis == 1 else score[i:i + 1, :]
        beats = (one > score) | ((one == score) & (blk > i))
        rank = rank + jnp.where(beats, 1.0, 0.0)
    return jnp.where(rank < n_top, 1.0, 0.0)


def _attend_t(s, ok, m_prev, l_prev):
    s = jnp.where(ok, s, NEG)
    m_new = jnp.maximum(m_prev, jnp.max(s, axis=0, keepdims=True))
    alpha = jnp.exp(m_prev - m_new)
    p = jnp.exp(s - m_new)
    l_new = alpha * l_prev + jnp.sum(p, axis=0, keepdims=True)
    return m_new, l_new, alpha, p


def _nsa_prompt_kernel(q_ref, gate_ref, kc_ref, vct_ref, ks_ref, vst_ref, kw_ref, vwt_ref, bc_ref, bt_ref,
                       o_ref, qt_ref, pcs_ref, pick_ref, acc_ref, os_ref, s_ref, p_ref, *, n_cmp, n_sel):
    i = pl.program_id(1)
    qs = i * Q_BLOCK
    key_i = lax.broadcasted_iota(I32, (KV_CHUNK, Q_BLOCK), 0)
    tok_i = lax.broadcasted_iota(I32, (KV_CHUNK, Q_BLOCK), 1)
    row8 = lax.broadcasted_iota(I32, (H_NSA, Q_BLOCK), 0)
    zeros_half = jnp.zeros((HEAD_DIM, Q_BLOCK), F32)

    for k2 in range(H_NSA // 2):
        slab_t = q_ref[:, k2 * LANES:(k2 + 1) * LANES].T * (HEAD_DIM ** -0.5)
        for h in (2 * k2, 2 * k2 + 1):
            qh = slab_t[(h % 2) * HEAD_DIM:(h % 2 + 1) * HEAD_DIM]
            both = [qh, zeros_half] if h // R_GRP == 0 else [zeros_half, qh]
            qt_ref[:, h * Q_BLOCK:(h + 1) * Q_BLOCK] = jnp.concatenate(both, axis=0).astype(BF16)

    sig_t = _sigmoid(gate_ref[...]).T

    ncp = kc_ref.shape[0]
    kc = kc_ref[...].astype(BF16)
    n_c = lax.broadcasted_iota(I32, (ncp, Q_BLOCK), 0)
    t_c = lax.broadcasted_iota(I32, (ncp, Q_BLOCK), 1)
    ok_c = (n_c * CMP_STRIDE + (L_CMP - 1) <= qs + t_c) & (n_c < n_cmp)
    blk_pos = qs + lax.broadcasted_iota(I32, (pick_ref.shape[1], Q_BLOCK), 1)
    o_c = []
    for g in range(N_KV):
        vct = vct_ref[g * HEAD_DIM:(g + 1) * HEAD_DIM, :].astype(BF16)
        pcs = jnp.zeros((ncp, Q_BLOCK), F32)
        for r in range(R_GRP):
            h = g * R_GRP + r
            s = _dot(kc, qt_ref[:, h * Q_BLOCK:(h + 1) * Q_BLOCK]) + bc_ref[h]
            s = jnp.where(ok_c, s, NEG)
            e = jnp.where(ok_c, jnp.exp(s - jnp.max(s, axis=0, keepdims=True)), 0.0)
            p = e * (1.0 / jnp.maximum(jnp.sum(e, axis=0, keepdims=True), TINY))
            o_c.append(_dot(vct, p.astype(BF16)))
            pcs = pcs + p
        pcs_ref[...] = pcs
        per = SEL_BLOCK // CMP_STRIDE
        imp = pcs_ref[pl.ds(0, ncp // per, stride=per), :]
        for c in range(1, per):
            imp = imp + pcs_ref[pl.ds(c, ncp // per, stride=per), :]
        pick_ref[g] = _select_top(_block_scores(imp, blk_pos, n_sel, axis=0), min(N_SELECT, n_sel), axis=0)

    dmat = tok_i - key_i

    def stream(k_ref, vt_ref, n_chunks, chunk_of, mask_of):
        def scores(c, slot):
            off, _ = chunk_of(c)
            k = k_ref[pl.ds(off, KV_CHUNK), :].astype(BF16)
            for h in range(H_NSA):
                s_ref[slot, h] = _dot(k, qt_ref[:, h * Q_BLOCK:(h + 1) * Q_BLOCK])

        def values(c, slot, alpha_all):
            off, _ = chunk_of(c)
            for g in range(N_KV):
                vt = vt_ref[g * HEAD_DIM:(g + 1) * HEAD_DIM, pl.ds(off, KV_CHUNK)].astype(BF16)
                for h in range(g * R_GRP, (g + 1) * R_GRP):
                    acc_ref[h] = acc_ref[h] * alpha_all[h:h + 1] + _dot(vt, p_ref[slot, h])

        def softmax(c, slot, m_all, l_all):
            _, rel = chunk_of(c)
            ok_g = mask_of(c, rel)
            bias_idx = min(max(rel, 0), FAR_REL) if isinstance(rel, int) else jnp.clip(rel, 0, FAR_REL)
            alpha_all = jnp.zeros((H_NSA, Q_BLOCK), F32)
            for h in range(H_NSA):
                s = s_ref[slot, h] + bt_ref[bias_idx, h]
                m_new, l_new, alpha, p = _attend_t(s, ok_g[h // R_GRP], m_all[h:h + 1], l_all[h:h + 1])
                p_ref[slot, h] = p.astype(BF16)
                m_all = jnp.where(row8 == h, m_new, m_all)
                l_all = jnp.where(row8 == h, l_new, l_all)
                alpha_all = jnp.where(row8 == h, alpha, alpha_all)
            return m_all, l_all, alpha_all

        def stage(c, slot, carry, prefetch=True):
            m_all, l_all, alpha_prev = carry
            if prefetch:
                scores(c + 1, 1 - slot)
            values(c - 1, 1 - slot, alpha_prev)
            return softmax(c, slot, m_all, l_all)

        acc_ref[...] = jnp.zeros_like(acc_ref)
        p_ref[1] = jnp.zeros(p_ref.shape[1:], BF16)
        scores(0, 0)
        carry = (jnp.full((H_NSA, Q_BLOCK), NEG, F32), jnp.zeros((H_NSA, Q_BLOCK), F32),
                 jnp.ones((H_NSA, Q_BLOCK), F32))
        if isinstance(n_chunks, int):
            for c in range(n_chunks):
                carry = stage(c, c % 2, carry, prefetch=c + 1 < n_chunks)
            last = n_chunks - 1
        else:
            pairs = (n_chunks + 1) // 2
            carry = lax.fori_loop(0, pairs, lambda t, cr: stage(2 * t + 1, 1, stage(2 * t, 0, cr)), carry)
            last = 2 * pairs - 1
        m_all, l_all, alpha_last = carry
        values(last, last % 2 if isinstance(last, int) else 1, alpha_last)
        return m_all, l_all

    def sel_chunk(c):
        return pl.multiple_of(jnp.clip(c, 0, i) * KV_CHUNK, KV_CHUNK), i - c

    def sel_mask(c, rel):
        causal = rel * KV_CHUNK + dmat >= 0
        per = KV_CHUNK // SEL_BLOCK
        first = jnp.minimum(c, i) * per
        ok_g = []
        for g in range(N_KV):
            picked = jnp.concatenate(
                [jnp.broadcast_to(pick_ref[g, pl.ds(first + b, 1), :], (SEL_BLOCK, Q_BLOCK)) for b in range(per)],
                axis=0)
            ok_g.append(causal & (picked > 0.5))
        return ok_g

    m_s, l_s = stream(ks_ref, vst_ref, i + 1, sel_chunk, sel_mask)
    for h in range(H_NSA):
        os_ref[h] = jnp.where(m_s[h:h + 1] > 0.5 * NEG, acc_ref[h] / jnp.maximum(l_s[h:h + 1], TINY), 0.0)

    n_back = WINDOW // KV_CHUNK

    def win_chunk(c):
        return pl.multiple_of(jnp.clip(i - n_back + c, 0, i) * KV_CHUNK, KV_CHUNK), n_back - c

    def win_mask(c, rel):
        dist = rel * KV_CHUNK + dmat + jnp.where(i - n_back + c < 0, WINDOW, 0)
        ok = (dist >= 0) & (dist < WINDOW)
        return [ok, ok]

    m_w, l_w = stream(kw_ref, vwt_ref, n_back + 1, win_chunk, win_mask)

    for k2 in range(H_NSA // 2):
        pair = []
        for h in (2 * k2, 2 * k2 + 1):
            o_w = jnp.where(m_w[h:h + 1] > 0.5 * NEG, acc_ref[h] / jnp.maximum(l_w[h:h + 1], TINY), 0.0)
            pair.append(sig_t[3 * h:3 * h + 1] * o_c[h] + sig_t[3 * h + 1:3 * h + 2] * os_ref[h]
                        + sig_t[3 * h + 2:3 * h + 3] * o_w)
        o_ref[:, k2 * LANES:(k2 + 1) * LANES] = jnp.concatenate(pair, axis=0).T.astype(o_ref.dtype)


def _selection_consts(n_rows_cmp, n_keys):
    n = np.arange(n_rows_cmp)
    blk = np.arange(SEL_LANES)
    pool = (n[:, None] // (SEL_BLOCK // CMP_STRIDE) == blk[None, :]).astype(np.float32)
    key = np.arange(n_keys)
    expand = (blk[:, None] == key[None, :] // SEL_BLOCK).astype(np.float32)
    return jnp.asarray(pool, BF16), jnp.asarray(expand, BF16)


def nsa_prompt(z3, kvt, kc, vct, bias_ct, bias_tt):
    b, t, _ = z3.shape
    n_sub = t // CMP_STRIDE
    n_sel = -(-t // SEL_BLOCK)
    assert kc.shape[1] == n_sub and n_sub == n_sel * (SEL_BLOCK // CMP_STRIDE) and n_sel % SUBLANES == 0
    fixed = lambda shape: pl.BlockSpec(shape, lambda bi, i: (0,) * len(shape))
    v_rows = lambda branch: (branch * KV_W + N_KV * HEAD_DIM) // LANES
    return pl.pallas_call(
        functools.partial(_nsa_prompt_kernel, n_cmp=n_sub - 1, n_sel=n_sel),
        grid=(b, t // Q_BLOCK),
        in_specs=[pl.BlockSpec((None, Q_BLOCK, NSA_QW), lambda bi, i: (bi, i, COL_NQ // NSA_QW)),
                  pl.BlockSpec((None, Q_BLOCK, LANES), lambda bi, i: (bi, i, COL_NG // LANES)),
                  pl.BlockSpec((None, n_sub, LANES), lambda bi, i: (bi, 0, 0)),
                  pl.BlockSpec((None, LANES, n_sub), lambda bi, i: (bi, 0, 0)),
                  pl.BlockSpec((None, t, LANES), lambda bi, i: (bi, 0, COL_KVS // LANES)),
                  pl.BlockSpec((None, LANES, t), lambda bi, i: (bi, v_rows(1), 0)),
                  pl.BlockSpec((None, t, LANES), lambda bi, i: (bi, 0, COL_KVW // LANES)),
                  pl.BlockSpec((None, LANES, t), lambda bi, i: (bi, v_rows(2), 0)),
                  pl.BlockSpec((H_NSA, n_sub, Q_BLOCK), lambda bi, i: (0, 0, i)),
                  fixed(bias_tt.shape)],
        out_specs=pl.BlockSpec((None, Q_BLOCK, NSA_QW), lambda bi, i: (bi, i, 0)),
        out_shape=jax.ShapeDtypeStruct((b, t, NSA_QW), BF16),
        scratch_shapes=[pltpu.VMEM((LANES, H_NSA * Q_BLOCK), BF16),
                        pltpu.VMEM((n_sub, Q_BLOCK), F32),
                        pltpu.VMEM((N_KV, n_sel, Q_BLOCK), F32),
                        pltpu.VMEM((H_NSA, HEAD_DIM, Q_BLOCK), F32),
                        pltpu.VMEM((H_NSA, HEAD_DIM, Q_BLOCK), F32),
                        pltpu.VMEM((2, H_NSA, KV_CHUNK, Q_BLOCK), F32),
                        pltpu.VMEM((2, H_NSA, KV_CHUNK, Q_BLOCK), BF16)],
        compiler_params=_cparams(2),
        name="nsa_prompt",
    )(z3, z3, kc, vct, z3, kvt, z3, kvt, bias_ct, bias_tt)


def _nsa_step_kernel(pt_ref, z_ref, kc_ref, vc_ref, *rest, n_pages, past, wbuf, n_cmp, n_sel):
    del pt_ref
    pages = rest[:n_pages]
    win_ref, bcs_ref, bss_ref, bws_ref, pool_ref, exp_ref, o_ref, wo_ref = rest[n_pages:]
    z = z_ref[...]
    row8 = lax.broadcasted_iota(I32, (H_NSA, LANES), 0)
    lane8 = lax.broadcasted_iota(I32, (H_NSA, LANES), 1)
    left8 = lane8 < HEAD_DIM
    qbd = jnp.zeros((H_NSA, LANES), F32)
    for h in range(H_NSA):
        slab = jnp.broadcast_to(z[:, COL_NQ + (h // 2) * LANES:COL_NQ + (h // 2 + 1) * LANES], (H_NSA, LANES))
        g = h // R_GRP
        if h % 2 != g:
            slab = pltpu.roll(slab, HEAD_DIM, 1)
        keep = (row8 == h) & (left8 if g == 0 else jnp.logical_not(left8))
        qbd = jnp.where(keep, slab, qbd)
    qbd = (qbd * (HEAD_DIM ** -0.5)).astype(BF16)

    ncp = kc_ref.shape[0]
    n_c = lax.broadcasted_iota(I32, (H_NSA, ncp), 1)
    s_c = _dot_nt(qbd, kc_ref[...].astype(BF16)) + bcs_ref[...]
    p_c = _masked_softmax(s_c, (n_c * CMP_STRIDE + (L_CMP - 1) <= past) & (n_c < n_cmp))
    o_c = _dot(p_c.astype(BF16), vc_ref[...].astype(BF16))
    pc0 = jnp.sum(p_c[0:R_GRP], axis=0, keepdims=True)
    pc1 = jnp.sum(p_c[R_GRP:], axis=0, keepdims=True)
    rowc = lax.broadcasted_iota(I32, (H_NSA, ncp), 0)
    pcs = jnp.where(rowc < R_GRP, jnp.broadcast_to(pc0, (H_NSA, ncp)), jnp.broadcast_to(pc1, (H_NSA, ncp)))
    imp = _dot3(pcs, pool_ref[...])
    sel = _select_top(_block_scores(imp, jnp.full(imp.shape, past, I32), n_sel), min(N_SELECT, n_sel))
    picked = _dot(sel.astype(BF16), exp_ref[...])

    c128 = lax.broadcasted_iota(I32, (LANES, KV_CHUNK), 1)

    def new_col(col):
        return jnp.broadcast_to(z[:, col:col + LANES], (KV_CHUNK, LANES)).T

    def first_col(tile):
        return jnp.where(c128 == 0, tile, 0.0).astype(BF16)

    ks = [pg[0:LANES, :].astype(BF16) for pg in pages] + [first_col(new_col(COL_KVS))]
    vs = [pg[LANES:2 * LANES, :].astype(BF16) for pg in pages] + [first_col(new_col(COL_KVS + LANES))]
    s_s = jnp.concatenate([_dot(qbd, k) for k in ks], axis=1) + bss_ref[...]
    key = lax.broadcasted_iota(I32, s_s.shape, 1)
    p_s = _masked_softmax(s_s, (picked > 0.5) & (key <= past)).astype(BF16)
    o_s = _dot_nt(p_s[:, 0:KV_CHUNK], vs[0])
    for c in range(1, len(vs)):
        o_s = o_s + _dot_nt(p_s[:, c * KV_CHUNK:(c + 1) * KV_CHUNK], vs[c])

    kw_new, vw_new = new_col(COL_KVW), new_col(COL_KVW + LANES)
    win = win_ref[...]
    s_w = jnp.concatenate([_dot(qbd, win[0:LANES].astype(BF16)), _dot(qbd, first_col(kw_new))], axis=1) + bws_ref[...]
    colw = lax.broadcasted_iota(I32, s_w.shape, 1)
    p_w = _masked_softmax(s_w, (colw <= wbuf) & (wbuf - colw < WINDOW)).astype(BF16)
    o_w = _dot_nt(p_w[:, 0:wbuf], win[LANES:].astype(BF16)) + _dot_nt(p_w[:, wbuf:], first_col(vw_new))

    shifted = pltpu.roll(win, wbuf - 1, 1)
    new_kv = jnp.concatenate([kw_new, vw_new], axis=0)
    last = jnp.where(lax.broadcasted_iota(I32, (KV_W, LANES), 1) == LANES - 1, new_kv, shifted[:, wbuf - LANES:])
    wo_ref[:, 0:wbuf - LANES] = shifted[:, 0:wbuf - LANES]
    wo_ref[:, wbuf - LANES:] = last

    sig = jnp.broadcast_to(_sigmoid(z[:, COL_NG:COL_NG + LANES]), (H_NSA, LANES))

    def gate_col(c):
        return jnp.sum(jnp.where(lane8 == row8 * 3 + c, sig, 0.0), axis=1, keepdims=True)

    o = gate_col(0) * o_c + gate_col(1) * o_s + gate_col(2) * o_w
    fix = jnp.where((row8 % 2) != (row8 // R_GRP), pltpu.roll(o, HEAD_DIM, 1), o)
    left1 = left8[0:1]
    o_ref[...] = jnp.concatenate(
        [jnp.where(left1, fix[2 * k2:2 * k2 + 1], fix[2 * k2 + 1:2 * k2 + 2]) for k2 in range(H_NSA // 2)], axis=1)


def nsa_step(zs3, kc, vc, sel_cache, win_cache, page_idx, win_row0, bias_cs, bias_ss, bias_ws, past):
    bs = zs3.shape[0]
    n_pages = page_idx.shape[1]
    wbuf = win_cache.shape[2]
    n_sub = (past + 1) // CMP_STRIDE
    n_sel = -(-(past + 1) // SEL_BLOCK)
    n_keys = (n_pages + 1) * PAGE_SIZE
    assert n_sel <= SEL_LANES and PAGE_SIZE == KV_CHUNK and wbuf % KV_CHUNK == 0 and kc.shape[1] == n_sub
    pool, expand = _selection_consts(n_sub, n_keys)
    fixed = lambda shape: pl.BlockSpec(shape, lambda b, pt: (0,) * len(shape))
    page = lambda p: pl.BlockSpec((None, KV_W, PAGE_SIZE), lambda b, pt, p=p: (pt[b, p], 0, 0))
    grid_spec = pltpu.PrefetchScalarGridSpec(
        num_scalar_prefetch=1,
        grid=(bs,),
        in_specs=[pl.BlockSpec((None, 1, N_IN_PAD), lambda b, pt: (b, 0, 0)),
                  pl.BlockSpec((None, n_sub, LANES), lambda b, pt: (b, 0, 0)),
                  pl.BlockSpec((None, n_sub, LANES), lambda b, pt: (b, 0, 0))]
        + [page(p) for p in range(n_pages)]
        + [pl.BlockSpec((None, KV_W, wbuf), lambda b, pt: (win_row0 + b, 0, 0)),
           fixed(bias_cs.shape), fixed(bias_ss.shape), fixed(bias_ws.shape), fixed(pool.shape), fixed(expand.shape)],
        out_specs=[pl.BlockSpec((None, 1, NSA_QW), lambda b, pt: (b, 0, 0)),
                   pl.BlockSpec((None, KV_W, wbuf), lambda b, pt: (b, 0, 0))],
    )
    return pl.pallas_call(
        functools.partial(_nsa_step_kernel, n_pages=n_pages, past=past, wbuf=wbuf, n_cmp=n_sub - 1, n_sel=n_sel),
        grid_spec=grid_spec,
        out_shape=[jax.ShapeDtypeStruct((bs, 1, NSA_QW), F32), jax.ShapeDtypeStruct((bs, KV_W, wbuf), F32)],
        compiler_params=_cparams(1),
        name="nsa_step",
    )(page_idx, zs3, kc, vc, *([sel_cache] * n_pages), win_cache, bias_cs, bias_ss, bias_ws, pool, expand)


def _route(logits, tm):
    m = logits.shape[0]
    top_v, top_i = lax.top_k(logits, TOP_K)
    top_w = jax.nn.softmax(top_v, axis=-1)
    slot_e = top_i.reshape(-1)
    onehot = (slot_e[:, None] == jnp.arange(N_EXPERTS)[None, :]).astype(I32)
    before = jnp.cumsum(onehot, axis=0) - onehot
    rank = jnp.sum(before * onehot, axis=1)
    count = jnp.sum(onehot, axis=0)
    padded = -(-count // tm) * tm
    start = jnp.cumsum(padded) - padded
    pos = start[slot_e] + rank
    n_rows = (-(-(m * TOP_K) // tm) + N_EXPERTS) * tm
    row_token = jnp.zeros((n_rows,), I32).at[pos].set(jnp.arange(m * TOP_K, dtype=I32) // TOP_K)
    row_gate = jnp.zeros((n_rows,), F32).at[pos].set(top_w.reshape(-1))
    tile_start = jnp.arange(n_rows // tm, dtype=I32) * tm
    tile_expert = jnp.clip(jnp.searchsorted(jnp.cumsum(padded), tile_start, side='right'), 0, N_EXPERTS - 1)
    tile_valid = (tile_start < jnp.sum(padded)).astype(I32)
    return pos.reshape(m, TOP_K), row_token, row_gate.reshape(n_rows, 1), tile_expert.astype(I32), tile_valid


def _dense_ffn(h, g, w1, w3, w2):
    m = h.shape[0]
    tm = _row_tile(m)
    ones = jnp.ones((m // tm,), I32)
    return grouped_ffn(h, g, jnp.ones((m, 1), F32), jnp.zeros((m // tm,), I32), ones,
                       w1[None], w3[None], w2[None], tm=tm, tf=w1.shape[1] // 2, residual=True)


def kernel(x_prompt, x_sample, p_prompt, p_sample, state_ret, cache_win_kv, cache_cmp_kv, cache_sel_kv, page_table, rel_bias, norm_mix, w_in, ret_gain, cmp_pe, cmp_w1, cmp_w2, w_out, norm_ffn, ffn_w1, ffn_w3, ffn_w2, router, moe_w1, moe_w3, moe_w2, ple_norm, ple_gate, ple_proj, norm_final):
    b_p, t_p, d = x_prompt.shape
    b_s = x_sample.shape[0]
    depth = w_in.shape[0]
    n_pages = page_table.shape[1]
    past = n_pages * PAGE_SIZE
    n_pool = cache_cmp_kv.shape[1]
    w_buf = cache_win_kv.shape[2]
    assert x_sample.shape[1] == 1 and t_p % Q_BLOCK == 0 and t_p >= w_buf and past % CMP_STRIDE == 0
    row = (2, N_KV, HEAD_DIM)
    m_p = b_p * t_p

    ar = lambda n: jnp.arange(n, dtype=I32)
    n_sub_p = t_p // CMP_STRIDE
    bias_ct = bias_lookup(rel_bias, ar(t_p)[None, :] - (ar(n_sub_p)[:, None] * CMP_STRIDE + L_CMP - 1))
    rel_t = (ar(FAR_REL + 1)[:, None, None] * KV_CHUNK + ar(Q_BLOCK)[None, None, :] - ar(KV_CHUNK)[None, :, None])
    bias_tt = bias_lookup(rel_bias, rel_t.reshape((FAR_REL + 1) * KV_CHUNK, Q_BLOCK))
    bias_tt = bias_tt.reshape(H_NSA, FAR_REL + 1, KV_CHUNK, Q_BLOCK).transpose(1, 0, 2, 3)
    n_sub_s = (past + 1) // CMP_STRIDE
    rows8 = jnp.zeros((SUBLANES, 1), I32)
    bias_cs = bias_lookup(rel_bias, rows8 + (past - (ar(n_sub_s)[None, :] * CMP_STRIDE + L_CMP - 1)))[:, 0]
    bias_ss = bias_lookup(rel_bias, rows8 + (past - ar((n_pages + 1) * PAGE_SIZE)[None, :]))[:, 0]
    bias_ws = bias_lookup(rel_bias, rows8 + (w_buf - ar(w_buf + KV_CHUNK)[None, :]))[:, 0]

    def rows_last(x):
        return jnp.moveaxis(x, 2, -1).reshape(x.shape[0] * x.shape[1], KV_W, x.shape[2])

    def rows_first(xt, lead):
        return jnp.moveaxis(xt.reshape((lead,) + row + (xt.shape[-1],)), -1, 1)

    cmp_cache = rows_last(cache_cmp_kv)
    sel_cache = rows_last(cache_sel_kv)
    win_cache = rows_last(cache_win_kv)
    state_t = jnp.moveaxis(state_ret, 1, -1).reshape(depth * H_RET, HEAD_DIM, HEAD_DIM, b_s)

    h_p = x_prompt.reshape(m_p, d)
    h_s = x_sample.reshape(b_s, d)
    rp, rs, wp, ws, cp, cs, sp, ss = [], [], [], [], [], [], [], []
    for i in range(depth):
        w_in_b = jnp.pad(w_in[i], ((0, 0), (0, N_IN_PAD - N_IN))).astype(BF16)
        w_out_b = w_out[i].astype(BF16)
        pe_t, cw1, cw2 = _compress_weights(cmp_pe[i], cmp_w1[i], cmp_w2[i])

        w_kvt = w_in[i][:, COL_KVC:COL_KVC + 3 * KV_W].T.astype(BF16)
        z, kvt = rms_matmul_kvt(h_p, norm_mix[i], w_in_b, w_kvt, b_p)
        z3 = z.reshape(b_p, t_p, N_IN_PAD)
        ret_y, ret_state = retention_prompt(z3, ret_gain[i])
        kc, vc = compress(kvt, ar(b_p)[:, None], pe_t, cw1, cw2)
        nsa_y = nsa_prompt(z3, kvt, kc, jnp.swapaxes(vc, 1, 2), bias_ct, bias_tt)
        h_p = out_proj(h_p, ret_y.reshape(m_p, RET_W), nsa_y.reshape(m_p, NSA_QW), w_out_b)
        rp.append(ret_state)
        cp.append(rows_first(kvt[:, 0:KV_W], b_p))
        sp.append(rows_first(kvt[:, KV_W:2 * KV_W], b_p))
        wp.append(rows_first(kvt[:, 2 * KV_W:, t_p - w_buf:], b_p))

        zs = rms_matmul(h_s, norm_mix[i], w_in_b)
        zst = zs.T
        ret_yst, state_s = retention_step(zst, state_t, i, past, ret_gain[i])
        page_idx = page_table + i * n_pool
        grp = 4 if b_s % 4 == 0 else 1
        kc_s, vc_s = compress(cmp_cache, page_idx.reshape(b_s // grp, grp * n_pages), pe_t, cw1, cw2)
        kc_s, vc_s = kc_s.reshape(b_s, n_sub_s, LANES), vc_s.reshape(b_s, n_sub_s, LANES)
        nsa_ys, win_s = nsa_step(zs.reshape(b_s, 1, N_IN_PAD), kc_s, vc_s, sel_cache, win_cache, page_idx, i * b_s,
                                 bias_cs, bias_ss, bias_ws, past)
        h_s = out_proj(h_s, ret_yst.T, nsa_ys.reshape(b_s, NSA_QW), w_out_b)
        rs.append(jnp.moveaxis(state_s, -1, 0))
        ws.append(rows_first(win_s, b_s))
        cs.append(rows_first(zst[None, COL_KVC:COL_KVC + KV_W], 1).reshape((b_s, 1) + row))
        ss.append(rows_first(zst[None, COL_KVS:COL_KVS + KV_W], 1).reshape((b_s, 1) + row))

        j = i // 2
        if i % 2 == 0:
            w1, w3, w2 = ffn_w1[j].astype(BF16), ffn_w3[j].astype(BF16), ffn_w2[j].astype(BF16)
            h_p = _dense_ffn(h_p, norm_ffn[i], w1, w3, w2)
            h_s = _dense_ffn(h_s, norm_ffn[i], w1, w3, w2)
        else:
            router_w = jnp.pad(router[j], ((0, 0), (0, LANES - N_EXPERTS)))
            tf = moe_w1.shape[3] // 7
            logits_p = router_logits(h_p, norm_ffn[i], router_w)
            tm = 1024
            pos, row_token, row_gate, tile_expert, tile_valid = _route(logits_p[:, :N_EXPERTS], tm)
            x_rows = gather_rows(h_p.reshape(m_p, SUBLANES, d // SUBLANES), row_token)
            y_rows = grouped_ffn(x_rows, norm_ffn[i], row_gate, tile_expert, tile_valid,
                                 moe_w1[j], moe_w3[j], moe_w2[j], tm=tm, tf=tf, residual=False)
            h_p = h_p + y_rows[pos[:, 0]] + y_rows[pos[:, 1]]
            logits_s = router_logits(h_s, norm_ffn[i], router_w)
            top_v, top_i = lax.top_k(logits_s[:, :N_EXPERTS], TOP_K)
            top_w = jax.nn.softmax(top_v, axis=-1)
            gates = jnp.sum(jax.nn.one_hot(top_i, N_EXPERTS, dtype=F32) * top_w[..., None], axis=-2)
            y_e = grouped_ffn(jnp.tile(h_s, (N_EXPERTS, 1)), norm_ffn[i], gates.T.reshape(N_EXPERTS * b_s, 1),
                              jnp.arange(N_EXPERTS, dtype=I32), jnp.ones((N_EXPERTS,), I32),
                              moe_w1[j], moe_w3[j], moe_w2[j], tm=b_s, tf=tf, residual=False)
            h_s = h_s + jnp.sum(y_e.reshape(N_EXPERTS, b_s, d), axis=0)

        wg, wpj = ple_gate[i].astype(BF16), ple_proj[i].astype(BF16)
        h_p = ple(h_p, ple_norm[i], wg, p_prompt[i].reshape(m_p, P_DIM), wpj, norm_final, i == depth - 1)
        h_s = ple(h_s, ple_norm[i], wg, p_sample[i].reshape(b_s, P_DIM), wpj, norm_final, i == depth - 1)

    return (h_p.reshape(b_p, t_p, d), h_s.reshape(b_s, 1, d), jnp.stack(rp), jnp.stack(rs), jnp.stack(wp),
            jnp.stack(ws), jnp.stack(cp), jnp.stack(cs), jnp.stack(sp), jnp.stack(ss))
```

```python
import functools
import math

import numpy as np
import jax
import jax.numpy as jnp
from jax import lax
from jax.experimental import pallas as pl
from jax.experimental.pallas import tpu as pltpu

F32 = jnp.float32
BF16 = jnp.bfloat16
I32 = jnp.int32

LANES = 128
SUBLANES = 8
VMEM_LIMIT_BYTES = 56 * 1024 * 1024

D_MODEL = 1024
HEAD_DIM = 64
H_RET = 8
H_NSA = 8
N_KV = 2
R_GRP = H_NSA // N_KV
RET_W = H_RET * HEAD_DIM
NSA_QW = H_NSA * HEAD_DIM
KV_W = 2 * N_KV * HEAD_DIM
N_IN = 4 * RET_W + NSA_QW + 3 * KV_W + 3 * H_NSA
N_IN_PAD = 3584
COL_RQ, COL_RK, COL_RV, COL_RG = 0, RET_W, 2 * RET_W, 3 * RET_W
COL_NQ = 4 * RET_W
COL_KVC = COL_NQ + NSA_QW
COL_KVS = COL_KVC + KV_W
COL_KVW = COL_KVS + KV_W
COL_NG = COL_KVW + KV_W
RET_CHUNK = 128
CMP_STRIDE = 16
L_CMP = 2 * CMP_STRIDE
CMP_HIDDEN = 2 * HEAD_DIM
SEL_BLOCK = 64
N_SELECT = 16
WINDOW = 512
Q_BLOCK = 128
N_BUCKETS = 32
MAX_DISTANCE = 128
N_EXPERTS = 8
TOP_K = 2
P_DIM = 256
PAGE_SIZE = 128
ROPE_BASE = 10000.0
EPS = 1e-6
FORCED_SCORE = 1e4
NEG = -1e30
TINY = float(np.finfo(np.float32).tiny)
SEL_LANES = 64
FAR_REL = 2
KV_CHUNK = 128


def _cparams(n_grid):
    return pltpu.CompilerParams(dimension_semantics=("arbitrary",) * n_grid,
                                vmem_limit_bytes=VMEM_LIMIT_BYTES)


def _dot(a, b):
    return jnp.dot(a, b, preferred_element_type=F32)


def _dot_nt(a, b):
    return lax.dot_general(a, b, (((1,), (1,)), ((), ())), preferred_element_type=F32)


def _split3(x):
    hi = x.astype(BF16)
    r = x - hi.astype(F32)
    mid = r.astype(BF16)
    lo = (r - mid.astype(F32)).astype(BF16)
    return hi, mid, lo


def _dot3(x, b01):
    hi, mid, lo = _split3(x)
    return _dot(hi, b01) + _dot(mid, b01) + _dot(lo, b01)


def _dot6(x, w):
    xh, xm, xl = _split3(x)
    wh, wm, wl = _split3(w)
    return (_dot(xh, wh) + _dot(xh, wm) + _dot(xm, wh)
            + _dot(xm, wm) + _dot(xh, wl) + _dot(xl, wh))


def _sigmoid(x):
    return 1.0 / (1.0 + jnp.exp(-x))


def _rms(x, g):
    return x * lax.rsqrt(jnp.mean(x * x, axis=-1, keepdims=True) + EPS) * g


def _row_tile(m):
    for t in (512, 256, 128, 64, 32, 16, 8):
        if m % t == 0:
            return t
    raise ValueError(f"row count {m} not a multiple of 8")


def _rms_matmul_kernel(x_ref, g_ref, w_ref, o_ref):
    o_ref[...] = _dot(_rms(x_ref[...], g_ref[...]).astype(BF16), w_ref[...])


def rms_matmul(x, g, w):
    m, d = x.shape
    n = w.shape[1]
    tm = min(_row_tile(m), 256)
    return pl.pallas_call(
        _rms_matmul_kernel,
        grid=(m // tm,),
        in_specs=[pl.BlockSpec((tm, d), lambda i: (i, 0)),
                  pl.BlockSpec((1, d), lambda i: (0, 0)),
                  pl.BlockSpec((d, n), lambda i: (0, 0))],
        out_specs=pl.BlockSpec((tm, n), lambda i: (i, 0)),
        out_shape=jax.ShapeDtypeStruct((m, n), F32),
        compiler_params=_cparams(1),
    )(x, g.reshape(1, d), w)


def _rms_matmul_kvt_kernel(x_ref, g_ref, w_ref, wt_ref, o_ref, ot_ref):
    xn = _rms(x_ref[...], g_ref[...]).astype(BF16)
    o_ref[...] = _dot(xn, w_ref[...])
    ot_ref[...] = _dot_nt(wt_ref[...], xn)


def rms_matmul_kvt(x, g, w, wt, bsz):
    m, d = x.shape
    n = w.shape[1]
    nt = wt.shape[0]
    t = m // bsz
    tm = min(_row_tile(t), 256)
    per = t // tm
    return pl.pallas_call(
        _rms_matmul_kvt_kernel,
        grid=(m // tm,),
        in_specs=[pl.BlockSpec((tm, d), lambda i: (i, 0)),
                  pl.BlockSpec((1, d), lambda i: (0, 0)),
                  pl.BlockSpec((d, n), lambda i: (0, 0)),
                  pl.BlockSpec((nt, d), lambda i: (0, 0))],
        out_specs=[pl.BlockSpec((tm, n), lambda i: (i, 0)),
                   pl.BlockSpec((None, nt, tm), lambda i: (i // per, 0, i % per))],
        out_shape=[jax.ShapeDtypeStruct((m, n), F32), jax.ShapeDtypeStruct((bsz, nt, t), F32)],
        compiler_params=_cparams(1),
        name="rms_matmul_kvt",
    )(x, g.reshape(1, d), w, wt)


def _router_kernel(x_ref, g_ref, w_ref, o_ref):
    o_ref[...] = _dot6(_rms(x_ref[...], g_ref[...]), w_ref[...])


def router_logits(x, g, w):
    m, d = x.shape
    n = w.shape[1]
    tm = min(_row_tile(m), 256)
    return pl.pallas_call(
        _router_kernel,
        grid=(m // tm,),
        in_specs=[pl.BlockSpec((tm, d), lambda i: (i, 0)),
                  pl.BlockSpec((1, d), lambda i: (0, 0)),
                  pl.BlockSpec((d, n), lambda i: (0, 0))],
        out_specs=pl.BlockSpec((tm, n), lambda i: (i, 0)),
        out_shape=jax.ShapeDtypeStruct((m, n), F32),
        compiler_params=_cparams(1),
        name="router",
    )(x, g.reshape(1, d), w)


def _outproj_kernel(h_ref, a_ref, b_ref, w_ref, o_ref):
    ka = a_ref.shape[1]
    o_ref[...] = (h_ref[...] + _dot(a_ref[...].astype(BF16), w_ref[0:ka, :])
                  + _dot(b_ref[...].astype(BF16), w_ref[ka:, :]))


def out_proj(h, a, b, w):
    m, d = h.shape
    ka, kb = a.shape[1], b.shape[1]
    tm = _row_tile(m)
    return pl.pallas_call(
        _outproj_kernel,
        grid=(m // tm,),
        in_specs=[pl.BlockSpec((tm, d), lambda i: (i, 0)),
                  pl.BlockSpec((tm, ka), lambda i: (i, 0)),
                  pl.BlockSpec((tm, kb), lambda i: (i, 0)),
                  pl.BlockSpec((ka + kb, d), lambda i: (0, 0))],
        out_specs=pl.BlockSpec((tm, d), lambda i: (i, 0)),
        out_shape=jax.ShapeDtypeStruct((m, d), F32),
        compiler_params=_cparams(1),
        name="out_proj",
    )(h, a, b, w)


def _ffn_kernel(te_ref, tv_ref, x_ref, g_ref, gate_ref, w1_ref, w3_ref, w2_ref, o_ref,
                xn_ref, acc_ref, *, residual):
    i = pl.program_id(0)
    j = pl.program_id(1)

    @pl.when(j == 0)
    def _():
        if len(x_ref.shape) == 3:
            x = jnp.concatenate([x_ref[:, s, :] for s in range(x_ref.shape[1])], axis=1)
        else:
            x = x_ref[...]
        xn_ref[...] = _rms(x, g_ref[...]).astype(BF16)
        acc_ref[...] = jnp.zeros_like(acc_ref)

    @pl.when(tv_ref[i] != 0)
    def _():
        xn = xn_ref[...]
        a = _dot(xn, w1_ref[...].astype(BF16))
        b = _dot(xn, w3_ref[...].astype(BF16))
        hid = (a * _sigmoid(a)) * b
        acc_ref[...] += _dot(hid.astype(BF16), w2_ref[...].astype(BF16))

    @pl.when(j == pl.num_programs(1) - 1)
    def _():
        y = acc_ref[...] * gate_ref[...]
        if residual:
            y = y + x_ref[...]
        o_ref[...] = y


def grouped_ffn(x, g, row_gate, tile_expert, tile_valid, w1, w3, w2, *, tm, tf, residual):
    m = x.shape[0]
    d = w1.shape[1]
    f = w1.shape[2]
    assert not (residual and x.ndim == 3)
    x_spec = (pl.BlockSpec((tm,) + x.shape[1:], lambda i, j, te, tv: (i, 0, 0)) if x.ndim == 3
              else pl.BlockSpec((tm, d), lambda i, j, te, tv: (i, 0)))
    grid_spec = pltpu.PrefetchScalarGridSpec(
        num_scalar_prefetch=2,
        grid=(m // tm, f // tf),
        in_specs=[x_spec,
                  pl.BlockSpec((1, d), lambda i, j, te, tv: (0, 0)),
                  pl.BlockSpec((tm, 1), lambda i, j, te, tv: (i, 0)),
                  pl.BlockSpec((None, d, tf), lambda i, j, te, tv: (te[i], 0, j)),
                  pl.BlockSpec((None, d, tf), lambda i, j, te, tv: (te[i], 0, j)),
                  pl.BlockSpec((None, tf, d), lambda i, j, te, tv: (te[i], j, 0))],
        out_specs=pl.BlockSpec((tm, d), lambda i, j, te, tv: (i, 0)),
        scratch_shapes=[pltpu.VMEM((tm, d), BF16), pltpu.VMEM((tm, d), F32)],
    )
    return pl.pallas_call(
        functools.partial(_ffn_kernel, residual=residual),
        grid_spec=grid_spec,
        out_shape=jax.ShapeDtypeStruct((m, d), F32),
        compiler_params=_cparams(2),
        name="grouped_ffn",
    )(tile_expert, tile_valid, x, g.reshape(1, d), row_gate, w1, w3, w2)


def _gather_rows_kernel(idx_ref, x_hbm, o_ref, sem):
    base = pl.program_id(0) * o_ref.shape[0]

    def row_copy(k):
        return pltpu.make_async_copy(x_hbm.at[idx_ref[base + k]], o_ref.at[k], sem)

    def start(k, carry):
        row_copy(k).start()
        return carry

    def wait(k, carry):
        row_copy(k).wait()
        return carry

    lax.fori_loop(0, o_ref.shape[0], start, 0)
    lax.fori_loop(0, o_ref.shape[0], wait, 0)


def gather_rows(x, idx, rows_per_step=256):
    n = idx.shape[0]
    assert n % rows_per_step == 0 and x.dtype == F32 and x.shape[1:] == (SUBLANES, LANES)
    grid_spec = pltpu.PrefetchScalarGridSpec(
        num_scalar_prefetch=1,
        grid=(n // rows_per_step,),
        in_specs=[pl.BlockSpec(memory_space=pl.ANY)],
        out_specs=pl.BlockSpec((rows_per_step, SUBLANES, LANES), lambda i, ix: (i, 0, 0)),
        scratch_shapes=[pltpu.SemaphoreType.DMA],
    )
    return pl.pallas_call(
        _gather_rows_kernel,
        grid_spec=grid_spec,
        out_shape=jax.ShapeDtypeStruct((n, SUBLANES, LANES), x.dtype),
        compiler_params=_cparams(1),
        name="gather_rows",
    )(idx, x)


def _ple_kernel(h_ref, g_ref, wg_ref, p_ref, wp_ref, gf_ref, o_ref, *, final):
    h = h_ref[...]
    gate = _sigmoid(_dot(_rms(h, g_ref[...]).astype(BF16), wg_ref[...]))
    proj = _dot(p_ref[...].astype(BF16), wp_ref[...])
    h2 = h + gate * proj
    o_ref[...] = _rms(h2, gf_ref[...]) if final else h2


def ple(h, g, wg, p, wp, gf, final):
    m, d = h.shape
    pd = p.shape[1]
    tm = _row_tile(m)
    row = lambda i: (i, 0)
    fixed = lambda i: (0, 0)
    return pl.pallas_call(
        functools.partial(_ple_kernel, final=final),
        grid=(m // tm,),
        in_specs=[pl.BlockSpec((tm, d), row), pl.BlockSpec((1, d), fixed),
                  pl.BlockSpec((d, d), fixed), pl.BlockSpec((tm, pd), row),
                  pl.BlockSpec((pd, d), fixed), pl.BlockSpec((1, d), fixed)],
        out_specs=pl.BlockSpec((tm, d), row),
        out_shape=jax.ShapeDtypeStruct((m, d), F32),
        compiler_params=_cparams(1),
        name="ple",
    )(h, g.reshape(1, d), wg, p, wp, gf.reshape(1, d))


def _bias_kernel(tab_ref, d_ref, o_ref):
    n = jnp.maximum(d_ref[...], 0)
    max_exact = N_BUCKETS // 2
    nf = jnp.maximum(n, 1).astype(F32)
    large = max_exact + (jnp.log(nf / max_exact) / math.log(MAX_DISTANCE / max_exact)
                         * (N_BUCKETS - max_exact)).astype(I32)
    large = jnp.minimum(large, N_BUCKETS - 1)
    bucket = jnp.where(n < max_exact, n, large)
    for h in range(H_NSA):
        acc = jnp.zeros(n.shape, F32)
        for k in range(N_BUCKETS):
            acc = jnp.where(bucket == k, tab_ref[k, h], acc)
        o_ref[h] = acc


def bias_lookup(table, dist):
    r, c = dist.shape
    tr = min(r, 128)
    tc = 512 if c % 512 == 0 else c
    return pl.pallas_call(
        _bias_kernel,
        grid=(r // tr, c // tc),
        in_specs=[pl.BlockSpec(memory_space=pltpu.SMEM),
                  pl.BlockSpec((tr, tc), lambda i, j: (i, j))],
        out_specs=pl.BlockSpec((H_NSA, tr, tc), lambda i, j: (0, i, j)),
        out_shape=jax.ShapeDtypeStruct((H_NSA, r, c), F32),
        compiler_params=_cparams(2),
        name="bias_lookup",
    )(table, dist)


def _swap_halves(x):
    lane = lax.broadcasted_iota(I32, x.shape, 1)
    return jnp.where(lane % HEAD_DIM < HEAD_DIM // 2,
                     pltpu.roll(x, LANES - HEAD_DIM // 2, 1), pltpu.roll(x, HEAD_DIM // 2, 1))


def _retention_prompt_kernel(q_ref, k_ref, v_ref, g_ref, cos_ref, sin_ref, gain_ref, intra_ref,
                             qdec_ref, kdec_ref, cdec_ref, y_ref, s_out_ref, s_ref):
    c = pl.program_id(1)

    @pl.when(c == 0)
    def _():
        s_ref[...] = jnp.zeros_like(s_ref)

    cos = cos_ref[...]
    sin = sin_ref[...]
    lane = lax.broadcasted_iota(I32, (RET_CHUNK, LANES), 1)
    left = lane < HEAD_DIM
    row = lax.broadcasted_iota(I32, (LANES, LANES), 0)
    col = lax.broadcasted_iota(I32, (LANES, LANES), 1)
    same_head = (row < HEAD_DIM) == (col < HEAD_DIM)
    ones_bd = jnp.where(same_head, 1.0, 0.0).astype(BF16)
    for p in range(H_RET // 2):
        sl = slice(p * LANES, (p + 1) * LANES)
        q = q_ref[:, sl]
        k = k_ref[:, sl]
        v = v_ref[:, sl].astype(BF16)
        qr = q * cos + _swap_halves(q) * sin
        kr = (k * cos + _swap_halves(k) * sin) * (HEAD_DIM ** -0.5)
        qb = qr.astype(BF16)
        q2 = jnp.concatenate([jnp.where(left, qr, 0.0), jnp.where(left, 0.0, qr)], axis=0).astype(BF16)
        inner = _dot_nt(q2, kr.astype(BF16)) * intra_ref[p]
        o2 = _dot(inner.astype(BF16), v)
        s_old = s_ref[p]
        cross = _dot(qb, s_old.astype(BF16)) * qdec_ref[:, sl]
        o = jnp.where(left, o2[:RET_CHUNK], o2[RET_CHUNK:]) + cross
        kd = (kr * kdec_ref[:, sl]).T.astype(BF16)
        s_ref[p] = s_old * cdec_ref[p] + jnp.where(same_head, _dot(kd, v), 0.0)
        ms = _dot3(o * o, ones_bd) * (1.0 / HEAD_DIM)
        gate = g_ref[:, sl]
        y = o * lax.rsqrt(ms + EPS) * gain_ref[:, sl] * (gate * _sigmoid(gate))
        y_ref[:, sl] = y.astype(y_ref.dtype)

    @pl.when(c == pl.num_programs(1) - 1)
    def _():
        for p in range(H_RET // 2):
            s = s_ref[p]
            s_out_ref[2 * p] = s[:HEAD_DIM, :HEAD_DIM]
            s_out_ref[2 * p + 1] = s[HEAD_DIM:, HEAD_DIM:]


def _retention_consts(c):
    log_g = jnp.log1p(-jnp.exp2(-5.0 - jnp.arange(H_RET, dtype=F32)))
    j = jnp.arange(c, dtype=F32)
    diff = j[:, None] - j[None, :]
    intra = jnp.where(diff >= 0, jnp.exp(jnp.maximum(diff, 0.0)[None] * log_g[:, None, None]), 0.0)
    q_decay = jnp.exp((j[:, None] + 1.0) * log_g[None, :])
    k_decay = jnp.exp((c - 1.0 - j[:, None]) * log_g[None, :])
    c_decay = jnp.exp(c * log_g)
    intra2 = intra.reshape(H_RET // 2, 2 * c, c)
    qdec = jnp.repeat(q_decay, HEAD_DIM, axis=1)
    kdec = jnp.repeat(k_decay, HEAD_DIM, axis=1)
    cdec = jnp.repeat(c_decay.reshape(H_RET // 2, 2), HEAD_DIM, axis=1)[:, :, None] * jnp.ones((1, 1, LANES), F32)
    return intra2, qdec, kdec, cdec


def _rotary_tables(pos):
    half = HEAD_DIM // 2
    inv = ROPE_BASE ** (-jnp.arange(half, dtype=F32) / half)
    ang = pos.astype(F32)[:, None] * inv[None, :]
    cos, sin = jnp.cos(ang), jnp.sin(ang)
    cos_t = jnp.concatenate([cos, cos, cos, cos], axis=1)
    sin_t = jnp.concatenate([-sin, sin, -sin, sin], axis=1)
    return cos_t, sin_t


def retention_prompt(z3, gain):
    b, t, _ = z3.shape
    c = RET_CHUNK
    cos_t, sin_t = _rotary_tables(jnp.arange(t))
    intra2, qdec, kdec, cdec = _retention_consts(c)
    blk = lambda col: pl.BlockSpec((None, c, RET_W), lambda bi, ci, col=col: (bi, ci, col))
    fixed2 = lambda shape: pl.BlockSpec(shape, lambda bi, ci: (0,) * len(shape))
    return pl.pallas_call(
        _retention_prompt_kernel,
        grid=(b, t // c),
        in_specs=[blk(COL_RQ // RET_W), blk(COL_RK // RET_W), blk(COL_RV // RET_W), blk(COL_RG // RET_W),
                  pl.BlockSpec((c, LANES), lambda bi, ci: (ci, 0)),
                  pl.BlockSpec((c, LANES), lambda bi, ci: (ci, 0)),
                  fixed2((1, RET_W)), fixed2((H_RET // 2, 2 * c, c)), fixed2((c, RET_W)),
                  fixed2((c, RET_W)), fixed2((H_RET // 2, LANES, LANES))],
        out_specs=[pl.BlockSpec((None, c, RET_W), lambda bi, ci: (bi, ci, 0)),
                   pl.BlockSpec((None, H_RET, HEAD_DIM, HEAD_DIM), lambda bi, ci: (bi, 0, 0, 0))],
        out_shape=[jax.ShapeDtypeStruct((b, t, RET_W), BF16),
                   jax.ShapeDtypeStruct((b, H_RET, HEAD_DIM, HEAD_DIM), F32)],
        scratch_shapes=[pltpu.VMEM((H_RET // 2, LANES, LANES), F32)],
        compiler_params=_cparams(2),
        name="retention_prompt",
    )(z3, z3, z3, z3, cos_t, sin_t, gain.reshape(1, RET_W), intra2, qdec, kdec, cdec)


def _retention_step_kernel(q_ref, k_ref, v_ref, g_ref, s_ref, cos_ref, sin_ref, gamma_ref, gain_ref,
                           y_ref, s_out_ref):
    half = HEAD_DIM // 2
    cos = cos_ref[...]
    sin = sin_ref[...]

    def rot(x):
        x1, x2 = x[:half], x[half:]
        return jnp.concatenate([x1 * cos - x2 * sin, x1 * sin + x2 * cos], axis=0)

    qr = rot(q_ref[...])
    kr = rot(k_ref[...]) * (HEAD_DIM ** -0.5)
    v = v_ref[...]
    gamma = gamma_ref[...]
    o = jnp.zeros_like(v)
    for d in range(HEAD_DIM):
        s_new = s_ref[d] * gamma + kr[d:d + 1] * v
        s_out_ref[d] = s_new
        o = o + qr[d:d + 1] * s_new
    ms = jnp.mean(o * o, axis=0, keepdims=True)
    gate = g_ref[...]
    y_ref[...] = o * lax.rsqrt(ms + EPS) * gain_ref[...] * (gate * _sigmoid(gate))


def retention_step(zt, state, layer, pos, gain):
    bs = zt.shape[1]
    half = HEAD_DIM // 2
    inv = ROPE_BASE ** (-jnp.arange(half, dtype=F32) / half)
    ang = jnp.full((1,), pos).astype(F32)[:, None] * inv[None, :]
    cos = jnp.broadcast_to(jnp.cos(ang).reshape(half, 1), (half, bs))
    sin = jnp.broadcast_to(jnp.sin(ang).reshape(half, 1), (half, bs))
    gamma = 1.0 - jnp.exp2(-5.0 - jnp.arange(H_RET, dtype=F32))
    gamma = jnp.broadcast_to(gamma.reshape(H_RET, 1, 1), (H_RET, 1, bs))
    gain_b = jnp.broadcast_to(gain.reshape(H_RET, HEAD_DIM, 1), (H_RET, HEAD_DIM, bs))
    head = lambda col: pl.BlockSpec((HEAD_DIM, bs), lambda h, col=col: (col // HEAD_DIM + h, 0))
    fixed = pl.BlockSpec((half, bs), lambda h: (0, 0))
    return pl.pallas_call(
        _retention_step_kernel,
        grid=(H_RET,),
        in_specs=[head(COL_RQ), head(COL_RK), head(COL_RV), head(COL_RG),
                  pl.BlockSpec((None, HEAD_DIM, HEAD_DIM, bs), lambda h: (layer * H_RET + h, 0, 0, 0)),
                  fixed, fixed,
                  pl.BlockSpec((None, 1, bs), lambda h: (h, 0, 0)),
                  pl.BlockSpec((None, HEAD_DIM, bs), lambda h: (h, 0, 0))],
        out_specs=[pl.BlockSpec((HEAD_DIM, bs), lambda h: (h, 0)),
                   pl.BlockSpec((None, HEAD_DIM, HEAD_DIM, bs), lambda h: (h, 0, 0, 0))],
        out_shape=[jax.ShapeDtypeStruct((RET_W, bs), F32),
                   jax.ShapeDtypeStruct((H_RET, HEAD_DIM, HEAD_DIM, bs), F32)],
        compiler_params=_cparams(1),
        name="retention_step",
    )(zt, zt, zt, zt, state, cos, sin, gamma, gain_b)


def _gelu_tanh(x):
    return 0.5 * x * (1.0 + jnp.tanh(math.sqrt(2.0 / math.pi) * (x + 0.044715 * (x * x * x))))


def _compress_kernel(*refs, n_pieces):
    x_refs = refs[1:1 + n_pieces]
    pe_ref, w1_ref, w2_ref, kc_ref, vc_ref, rows_ref = refs[1 + n_pieces:]
    width = x_refs[0].shape[1]
    n = n_pieces * width // CMP_STRIDE
    for kv, o_ref in enumerate((kc_ref, vc_ref)):
        for p, x_ref in enumerate(x_refs):
            for c in range(width // LANES):
                rows_ref[pl.ds(p * width + c * LANES, LANES), :] = (
                    x_ref[kv * LANES:(kv + 1) * LANES, c * LANES:(c + 1) * LANES].T)
        xk = jnp.concatenate([rows_ref[pl.ds(j, n, stride=CMP_STRIDE), :] for j in range(CMP_STRIDE)], axis=1)
        top = _dot((xk + pe_ref[kv, 0]).astype(BF16), w1_ref[kv, 0])
        bot = _dot((xk + pe_ref[kv, 1]).astype(BF16), w1_ref[kv, 1])
        pre = top + pltpu.roll(bot, n - 1, 0)
        hid = _gelu_tanh(pre).astype(BF16)
        o_ref[...] = jnp.concatenate(
            [_dot(hid[:, g * CMP_HIDDEN:(g + 1) * CMP_HIDDEN], w2_ref[kv]) for g in range(N_KV)], axis=1)


def _compress_weights(pe, w1, w2):
    pe_r = pe.reshape(2, 2, CMP_STRIDE, 1, HEAD_DIM)
    pe_t = jnp.broadcast_to(pe_r, (2, 2, CMP_STRIDE, N_KV, HEAD_DIM)).reshape(2, 2, 1, CMP_STRIDE * LANES)
    w1r = w1.reshape(2, 2, CMP_STRIDE, HEAD_DIM, CMP_HIDDEN)
    eye = jnp.eye(N_KV, dtype=w1.dtype)
    wbig = jnp.einsum('khjdc,gG->khjgdGc', w1r, eye)
    wbig = wbig.reshape(2, 2, CMP_STRIDE * LANES, N_KV * CMP_HIDDEN)
    return pe_t.astype(F32), wbig.astype(BF16), w2.astype(BF16)


def compress(src, idx, pe_t, w1b, w2b):
    bsz, n_pieces = idx.shape
    width = src.shape[2]
    n = n_pieces * width // CMP_STRIDE
    piece = lambda p: pl.BlockSpec((None, KV_W, width), lambda b, ix, p=p: (ix[b, p], 0, 0))
    fixed = lambda shape: pl.BlockSpec(shape, lambda b, ix: (0,) * len(shape))
    out = pl.BlockSpec((None, n, LANES), lambda b, ix: (b, 0, 0))
    grid_spec = pltpu.PrefetchScalarGridSpec(
        num_scalar_prefetch=1,
        grid=(bsz,),
        in_specs=[piece(p) for p in range(n_pieces)]
        + [fixed(pe_t.shape), fixed(w1b.shape), fixed(w2b.shape)],
        out_specs=[out, out],
        scratch_shapes=[pltpu.VMEM((n_pieces * width, LANES), F32)],
    )
    return pl.pallas_call(
        functools.partial(_compress_kernel, n_pieces=n_pieces),
        grid_spec=grid_spec,
        out_shape=[jax.ShapeDtypeStruct((bsz, n, LANES), F32)] * 2,
        compiler_params=_cparams(1),
        name="compress",
    )(idx, *([src] * n_pieces), pe_t, w1b, w2b)


def _masked_softmax(s, mask):
    s = jnp.where(mask, s, NEG)
    m = jnp.max(s, axis=-1, keepdims=True)
    e = jnp.where(mask, jnp.exp(s - m), 0.0)
    return e / jnp.maximum(jnp.sum(e, axis=-1, keepdims=True), TINY)


def _block_scores(imp, qpos, n_sel, axis=1):
    blk = lax.broadcasted_iota(I32, imp.shape, axis)
    cur = qpos // SEL_BLOCK
    forced = (blk == 0) | (blk == cur) | (blk == cur - 1)
    valid = blk * SEL_BLOCK <= qpos
    score = jnp.where(forced, FORCED_SCORE, jnp.where(valid, imp, -1.0))
    return jnp.where(blk < n_sel, score, -2.0)


def _select_top(score, n_top, axis=1):
    blk = lax.broadcasted_iota(I32, score.shape, axis)
    rank = jnp.zeros(score.shape, F32)
    for i in range(score.shape[axis]):
        one = score[:, i:i + 1] if axis == 1 else score[i:i + 1, :]
        beats = (one > score) | ((one == score) & (blk > i))
        rank = rank + jnp.where(beats, 1.0, 0.0)
    return jnp.where(rank < n_top, 1.0, 0.0)


def _attend_t(s, ok, m_prev, l_prev):
    s = jnp.where(ok, s, NEG)
    m_new = jnp.maximum(m_prev, jnp.max(s, axis=0, keepdims=True))
    alpha = jnp.exp(m_prev - m_new)
    p = jnp.exp(s - m_new)
    l_new = alpha * l_prev + jnp.sum(p, axis=0, keepdims=True)
    return m_new, l_new, alpha, p


def _nsa_prompt_kernel(q_ref, gate_ref, kc_ref, vct_ref, ks_ref, vst_ref, kw_ref, vwt_ref, bc_ref, bt_ref,
                       o_ref, qt_ref, pcs_ref, pick_ref, acc_ref, os_ref, s_ref, p_ref, *, n_cmp, n_sel):
    i = pl.program_id(1)
    qs = i * Q_BLOCK
    key_i = lax.broadcasted_iota(I32, (KV_CHUNK, Q_BLOCK), 0)
    tok_i = lax.broadcasted_iota(I32, (KV_CHUNK, Q_BLOCK), 1)
    row8 = lax.broadcasted_iota(I32, (H_NSA, Q_BLOCK), 0)
    zeros_half = jnp.zeros((HEAD_DIM, Q_BLOCK), F32)

    for k2 in range(H_NSA // 2):
        slab_t = q_ref[:, k2 * LANES:(k2 + 1) * LANES].T * (HEAD_DIM ** -0.5)
        for h in (2 * k2, 2 * k2 + 1):
            qh = slab_t[(h % 2) * HEAD_DIM:(h % 2 + 1) * HEAD_DIM]
            both = [qh, zeros_half] if h // R_GRP == 0 else [zeros_half, qh]
            qt_ref[:, h * Q_BLOCK:(h + 1) * Q_BLOCK] = jnp.concatenate(both, axis=0).astype(BF16)

    sig_t = _sigmoid(gate_ref[...]).T

    ncp = kc_ref.shape[0]
    kc = kc_ref[...].astype(BF16)
    n_c = lax.broadcasted_iota(I32, (ncp, Q_BLOCK), 0)
    t_c = lax.broadcasted_iota(I32, (ncp, Q_BLOCK), 1)
    ok_c = (n_c * CMP_STRIDE + (L_CMP - 1) <= qs + t_c) & (n_c < n_cmp)
    blk_pos = qs + lax.broadcasted_iota(I32, (pick_ref.shape[1], Q_BLOCK), 1)
    o_c = []
    for g in range(N_KV):
        vct = vct_ref[g * HEAD_DIM:(g + 1) * HEAD_DIM, :].astype(BF16)
        pcs = jnp.zeros((ncp, Q_BLOCK), F32)
        for r in range(R_GRP):
            h = g * R_GRP + r
            s = _dot(kc, qt_ref[:, h * Q_BLOCK:(h + 1) * Q_BLOCK]) + bc_ref[h]
            s = jnp.where(ok_c, s, NEG)
            e = jnp.where(ok_c, jnp.exp(s - jnp.max(s, axis=0, keepdims=True)), 0.0)
            p = e * (1.0 / jnp.maximum(jnp.sum(e, axis=0, keepdims=True), TINY))
            o_c.append(_dot(vct, p.astype(BF16)))
            pcs = pcs + p
        pcs_ref[...] = pcs
        per = SEL_BLOCK // CMP_STRIDE
        imp = pcs_ref[pl.ds(0, ncp // per, stride=per), :]
        for c in range(1, per):
            imp = imp + pcs_ref[pl.ds(c, ncp // per, stride=per), :]
        pick_ref[g] = _select_top(_block_scores(imp, blk_pos, n_sel, axis=0), min(N_SELECT, n_sel), axis=0)

    dmat = tok_i - key_i

    def stream(k_ref, vt_ref, n_chunks, chunk_of, mask_of):
        def scores(c, slot):
            off, _ = chunk_of(c)
            k = k_ref[pl.ds(off, KV_CHUNK), :].astype(BF16)
            for h in range(H_NSA):
                s_ref[slot, h] = _dot(k, qt_ref[:, h * Q_BLOCK:(h + 1) * Q_BLOCK])

        def values(c, slot, alpha_all):
            off, _ = chunk_of(c)
            for g in range(N_KV):
                vt = vt_ref[g * HEAD_DIM:(g + 1) * HEAD_DIM, pl.ds(off, KV_CHUNK)].astype(BF16)
                for h in range(g * R_GRP, (g + 1) * R_GRP):
                    acc_ref[h] = acc_ref[h] * alpha_all[h:h + 1] + _dot(vt, p_ref[slot, h])

        def softmax(c, slot, m_all, l_all):
            _, rel = chunk_of(c)
            ok_g = mask_of(c, rel)
            bias_idx = min(max(rel, 0), FAR_REL) if isinstance(rel, int) else jnp.clip(rel, 0, FAR_REL)
            alpha_all = jnp.zeros((H_NSA, Q_BLOCK), F32)
            for h in range(H_NSA):
                s = s_ref[slot, h] + bt_ref[bias_idx, h]
                m_new, l_new, alpha, p = _attend_t(s, ok_g[h // R_GRP], m_all[h:h + 1], l_all[h:h + 1])
                p_ref[slot, h] = p.astype(BF16)
                m_all = jnp.where(row8 == h, m_new, m_all)
                l_all = jnp.where(row8 == h, l_new, l_all)
                alpha_all = jnp.where(row8 == h, alpha, alpha_all)
            return m_all, l_all, alpha_all

        def stage(c, slot, carry, prefetch=True):
            m_all, l_all, alpha_prev = carry
            if prefetch:
                scores(c + 1, 1 - slot)
            values(c - 1, 1 - slot, alpha_prev)
            return softmax(c, slot, m_all, l_all)

        acc_ref[...] = jnp.zeros_like(acc_ref)
        p_ref[1] = jnp.zeros(p_ref.shape[1:], BF16)
        scores(0, 0)
        carry = (jnp.full((H_NSA, Q_BLOCK), NEG, F32), jnp.zeros((H_NSA, Q_BLOCK), F32),
                 jnp.ones((H_NSA, Q_BLOCK), F32))
        if isinstance(n_chunks, int):
            for c in range(n_chunks):
                carry = stage(c, c % 2, carry, prefetch=c + 1 < n_chunks)
            last = n_chunks - 1
        else:
            pairs = (n_chunks + 1) // 2
            carry = lax.fori_loop(0, pairs, lambda t, cr: stage(2 * t + 1, 1, stage(2 * t, 0, cr)), carry)
            last = 2 * pairs - 1
        m_all, l_all, alpha_last = carry
        values(last, last % 2 if isinstance(last, int) else 1, alpha_last)
        return m_all, l_all

    def sel_chunk(c):
        return pl.multiple_of(jnp.clip(c, 0, i) * KV_CHUNK, KV_CHUNK), i - c

    def sel_mask(c, rel):
        causal = rel * KV_CHUNK + dmat >= 0
        per = KV_CHUNK // SEL_BLOCK
        first = jnp.minimum(c, i) * per
        ok_g = []
        for g in range(N_KV):
            picked = jnp.concatenate(
                [jnp.broadcast_to(pick_ref[g, pl.ds(first + b, 1), :], (SEL_BLOCK, Q_BLOCK)) for b in range(per)],
                axis=0)
            ok_g.append(causal & (picked > 0.5))
        return ok_g

    m_s, l_s = stream(ks_ref, vst_ref, i + 1, sel_chunk, sel_mask)
    for h in range(H_NSA):
        os_ref[h] = jnp.where(m_s[h:h + 1] > 0.5 * NEG, acc_ref[h] / jnp.maximum(l_s[h:h + 1], TINY), 0.0)

    n_back = WINDOW // KV_CHUNK

    def win_chunk(c):
        return pl.multiple_of(jnp.clip(i - n_back + c, 0, i) * KV_CHUNK, KV_CHUNK), n_back - c

    def win_mask(c, rel):
        dist = rel * KV_CHUNK + dmat + jnp.where(i - n_back + c < 0, WINDOW, 0)
        ok = (dist >= 0) & (dist < WINDOW)
        return [ok, ok]

    m_w, l_w = stream(kw_ref, vwt_ref, n_back + 1, win_chunk, win_mask)

    for k2 in range(H_NSA // 2):
        pair = []
        for h in (2 * k2, 2 * k2 + 1):
            o_w = jnp.where(m_w[h:h + 1] > 0.5 * NEG, acc_ref[h] / jnp.maximum(l_w[h:h + 1], TINY), 0.0)
            pair.append(sig_t[3 * h:3 * h + 1] * o_c[h] + sig_t[3 * h + 1:3 * h + 2] * os_ref[h]
                        + sig_t[3 * h + 2:3 * h + 3] * o_w)
        o_ref[:, k2 * LANES:(k2 + 1) * LANES] = jnp.concatenate(pair, axis=0).T.astype(o_ref.dtype)


def _selection_consts(n_rows_cmp, n_keys):
    n = np.arange(n_rows_cmp)
    blk = np.arange(SEL_LANES)
    pool = (n[:, None] // (SEL_BLOCK // CMP_STRIDE) == blk[None, :]).astype(np.float32)
    key = np.arange(n_keys)
    expand = (blk[:, None] == key[None, :] // SEL_BLOCK).astype(np.float32)
    return jnp.asarray(pool, BF16), jnp.asarray(expand, BF16)


def nsa_prompt(z3, kvt, kc, vct, bias_ct, bias_tt):
    b, t, _ = z3.shape
    n_sub = t // CMP_STRIDE
    n_sel = -(-t // SEL_BLOCK)
    assert kc.shape[1] == n_sub and n_sub == n_sel * (SEL_BLOCK // CMP_STRIDE) and n_sel % SUBLANES == 0
    fixed = lambda shape: pl.BlockSpec(shape, lambda bi, i: (0,) * len(shape))
    v_rows = lambda branch: (branch * KV_W + N_KV * HEAD_DIM) // LANES
    return pl.pallas_call(
        functools.partial(_nsa_prompt_kernel, n_cmp=n_sub - 1, n_sel=n_sel),
        grid=(b, t // Q_BLOCK),
        in_specs=[pl.BlockSpec((None, Q_BLOCK, NSA_QW), lambda bi, i: (bi, i, COL_NQ // NSA_QW)),
                  pl.BlockSpec((None, Q_BLOCK, LANES), lambda bi, i: (bi, i, COL_NG // LANES)),
                  pl.BlockSpec((None, n_sub, LANES), lambda bi, i: (bi, 0, 0)),
                  pl.BlockSpec((None, LANES, n_sub), lambda bi, i: (bi, 0, 0)),
                  pl.BlockSpec((None, t, LANES), lambda bi, i: (bi, 0, COL_KVS // LANES)),
                  pl.BlockSpec((None, LANES, t), lambda bi, i: (bi, v_rows(1), 0)),
                  pl.BlockSpec((None, t, LANES), lambda bi, i: (bi, 0, COL_KVW // LANES)),
                  pl.BlockSpec((None, LANES, t), lambda bi, i: (bi, v_rows(2), 0)),
                  pl.BlockSpec((H_NSA, n_sub, Q_BLOCK), lambda bi, i: (0, 0, i)),
                  fixed(bias_tt.shape)],
        out_specs=pl.BlockSpec((None, Q_BLOCK, NSA_QW), lambda bi, i: (bi, i, 0)),
        out_shape=jax.ShapeDtypeStruct((b, t, NSA_QW), BF16),
        scratch_shapes=[pltpu.VMEM((LANES, H_NSA * Q_BLOCK), BF16),
                        pltpu.VMEM((n_sub, Q_BLOCK), F32),
                        pltpu.VMEM((N_KV, n_sel, Q_BLOCK), F32),
                        pltpu.VMEM((H_NSA, HEAD_DIM, Q_BLOCK), F32),
                        pltpu.VMEM((H_NSA, HEAD_DIM, Q_BLOCK), F32),
                        pltpu.VMEM((2, H_NSA, KV_CHUNK, Q_BLOCK), F32),
                        pltpu.VMEM((2, H_NSA, KV_CHUNK, Q_BLOCK), BF16)],
        compiler_params=_cparams(2),
        name="nsa_prompt",
    )(z3, z3, kc, vct, z3, kvt, z3, kvt, bias_ct, bias_tt)


def _nsa_step_kernel(pt_ref, z_ref, kc_ref, vc_ref, *rest, n_pages, per_step, **static):
    del pt_ref
    pages = rest[:per_step * n_pages]
    win_ref, bcs_ref, bss_ref, bws_ref, pool_ref, exp_ref, o_ref, wo_ref = rest[per_step * n_pages:]
    for u in range(per_step):
        _nsa_step_one(z_ref.at[u], kc_ref.at[u], vc_ref.at[u], pages[u * n_pages:(u + 1) * n_pages], win_ref.at[u],
                      bcs_ref, bss_ref, bws_ref, pool_ref, exp_ref, o_ref.at[u], wo_ref.at[u], **static)


def _nsa_step_one(z_ref, kc_ref, vc_ref, pages, win_ref, bcs_ref, bss_ref, bws_ref, pool_ref, exp_ref, o_ref, wo_ref,
                  *, past, wbuf, n_cmp, n_sel):
    z = z_ref[...]
    row8 = lax.broadcasted_iota(I32, (H_NSA, LANES), 0)
    lane8 = lax.broadcasted_iota(I32, (H_NSA, LANES), 1)
    left8 = lane8 < HEAD_DIM
    qbd = jnp.zeros((H_NSA, LANES), F32)
    for h in range(H_NSA):
        slab = jnp.broadcast_to(z[:, COL_NQ + (h // 2) * LANES:COL_NQ + (h // 2 + 1) * LANES], (H_NSA, LANES))
        g = h // R_GRP
        if h % 2 != g:
            slab = pltpu.roll(slab, HEAD_DIM, 1)
        keep = (row8 == h) & (left8 if g == 0 else jnp.logical_not(left8))
        qbd = jnp.where(keep, slab, qbd)
    qbd = (qbd * (HEAD_DIM ** -0.5)).astype(BF16)

    ncp = kc_ref.shape[0]
    n_c = lax.broadcasted_iota(I32, (H_NSA, ncp), 1)
    s_c = _dot_nt(qbd, kc_ref[...].astype(BF16)) + bcs_ref[...]
    p_c = _masked_softmax(s_c, (n_c * CMP_STRIDE + (L_CMP - 1) <= past) & (n_c < n_cmp))
    o_c = _dot(p_c.astype(BF16), vc_ref[...].astype(BF16))
    pc0 = jnp.sum(p_c[0:R_GRP], axis=0, keepdims=True)
    pc1 = jnp.sum(p_c[R_GRP:], axis=0, keepdims=True)
    rowc = lax.broadcasted_iota(I32, (H_NSA, ncp), 0)
    pcs = jnp.where(rowc < R_GRP, jnp.broadcast_to(pc0, (H_NSA, ncp)), jnp.broadcast_to(pc1, (H_NSA, ncp)))
    imp = _dot3(pcs, pool_ref[...])
    sel = _select_top(_block_scores(imp, jnp.full(imp.shape, past, I32), n_sel), min(N_SELECT, n_sel))
    picked = _dot(sel.astype(BF16), exp_ref[...])

    c128 = lax.broadcasted_iota(I32, (LANES, KV_CHUNK), 1)

    def new_col(col):
        return jnp.broadcast_to(z[:, col:col + LANES], (KV_CHUNK, LANES)).T

    def first_col(tile):
        return jnp.where(c128 == 0, tile, 0.0).astype(BF16)

    ks = [pg[0:LANES, :].astype(BF16) for pg in pages] + [first_col(new_col(COL_KVS))]
    vs = [pg[LANES:2 * LANES, :].astype(BF16) for pg in pages] + [first_col(new_col(COL_KVS + LANES))]
    s_s = jnp.concatenate([_dot(qbd, k) for k in ks], axis=1) + bss_ref[...]
    key = lax.broadcasted_iota(I32, s_s.shape, 1)
    p_s = _masked_softmax(s_s, (picked > 0.5) & (key <= past)).astype(BF16)
    o_s = _dot_nt(p_s[:, 0:KV_CHUNK], vs[0])
    for c in range(1, len(vs)):
        o_s = o_s + _dot_nt(p_s[:, c * KV_CHUNK:(c + 1) * KV_CHUNK], vs[c])

    kw_new, vw_new = new_col(COL_KVW), new_col(COL_KVW + LANES)
    win = win_ref[...]
    s_w = jnp.concatenate([_dot(qbd, win[0:LANES].astype(BF16)), _dot(qbd, first_col(kw_new))], axis=1) + bws_ref[...]
    colw = lax.broadcasted_iota(I32, s_w.shape, 1)
    p_w = _masked_softmax(s_w, (colw <= wbuf) & (wbuf - colw < WINDOW)).astype(BF16)
    o_w = _dot_nt(p_w[:, 0:wbuf], win[LANES:].astype(BF16)) + _dot_nt(p_w[:, wbuf:], first_col(vw_new))

    shifted = pltpu.roll(win, wbuf - 1, 1)
    new_kv = jnp.concatenate([kw_new, vw_new], axis=0)
    last = jnp.where(lax.broadcasted_iota(I32, (KV_W, LANES), 1) == LANES - 1, new_kv, shifted[:, wbuf - LANES:])
    wo_ref[:, 0:wbuf - LANES] = shifted[:, 0:wbuf - LANES]
    wo_ref[:, wbuf - LANES:] = last

    sig = jnp.broadcast_to(_sigmoid(z[:, COL_NG:COL_NG + LANES]), (H_NSA, LANES))

    def gate_col(c):
        return jnp.sum(jnp.where(lane8 == row8 * 3 + c, sig, 0.0), axis=1, keepdims=True)

    o = gate_col(0) * o_c + gate_col(1) * o_s + gate_col(2) * o_w
    fix = jnp.where((row8 % 2) != (row8 // R_GRP), pltpu.roll(o, HEAD_DIM, 1), o)
    left1 = left8[0:1]
    o_ref[...] = jnp.concatenate(
        [jnp.where(left1, fix[2 * k2:2 * k2 + 1], fix[2 * k2 + 1:2 * k2 + 2]) for k2 in range(H_NSA // 2)], axis=1)


def nsa_step(zs3, kc, vc, sel_cache, win_cache, page_idx, win_row0, bias_cs, bias_ss, bias_ws, past):
    bs = zs3.shape[0]
    n_pages = page_idx.shape[1]
    wbuf = win_cache.shape[2]
    n_sub = (past + 1) // CMP_STRIDE
    n_sel = -(-(past + 1) // SEL_BLOCK)
    n_keys = (n_pages + 1) * PAGE_SIZE
    assert n_sel <= SEL_LANES and PAGE_SIZE == KV_CHUNK and wbuf % KV_CHUNK == 0 and kc.shape[1] == n_sub
    pool, expand = _selection_consts(n_sub, n_keys)
    per = 2 if bs % 2 == 0 and win_row0 % 2 == 0 else 1
    fixed = lambda shape: pl.BlockSpec(shape, lambda b, pt: (0,) * len(shape))
    page = lambda u, p: pl.BlockSpec((None, KV_W, PAGE_SIZE), lambda b, pt, u=u, p=p: (pt[per * b + u, p], 0, 0))
    grid_spec = pltpu.PrefetchScalarGridSpec(
        num_scalar_prefetch=1,
        grid=(bs // per,),
        in_specs=[pl.BlockSpec((per, 1, N_IN_PAD), lambda b, pt: (b, 0, 0)),
                  pl.BlockSpec((per, n_sub, LANES), lambda b, pt: (b, 0, 0)),
                  pl.BlockSpec((per, n_sub, LANES), lambda b, pt: (b, 0, 0))]
        + [page(u, p) for u in range(per) for p in range(n_pages)]
        + [pl.BlockSpec((per, KV_W, wbuf), lambda b, pt: (win_row0 // per + b, 0, 0)),
           fixed(bias_cs.shape), fixed(bias_ss.shape), fixed(bias_ws.shape), fixed(pool.shape), fixed(expand.shape)],
        out_specs=[pl.BlockSpec((per, 1, NSA_QW), lambda b, pt: (b, 0, 0)),
                   pl.BlockSpec((per, KV_W, wbuf), lambda b, pt: (b, 0, 0))],
    )
    return pl.pallas_call(
        functools.partial(_nsa_step_kernel, n_pages=n_pages, per_step=per, past=past, wbuf=wbuf, n_cmp=n_sub - 1,
                          n_sel=n_sel),
        grid_spec=grid_spec,
        out_shape=[jax.ShapeDtypeStruct((bs, 1, NSA_QW), F32), jax.ShapeDtypeStruct((bs, KV_W, wbuf), F32)],
        compiler_params=_cparams(1),
        name="nsa_step",
    )(page_idx, zs3, kc, vc, *([sel_cache] * (per * n_pages)), win_cache, bias_cs, bias_ss, bias_ws, pool, expand)


def _route(logits, tm):
    m = logits.shape[0]
    top_v, top_i = lax.top_k(logits, TOP_K)
    top_w = jax.nn.softmax(top_v, axis=-1)
    slot_e = top_i.reshape(-1)
    onehot = (slot_e[:, None] == jnp.arange(N_EXPERTS)[None, :]).astype(I32)
    before = jnp.cumsum(onehot, axis=0) - onehot
    rank = jnp.sum(before * onehot, axis=1)
    count = jnp.sum(onehot, axis=0)
    padded = -(-count // tm) * tm
    start = jnp.cumsum(padded) - padded
    pos = start[slot_e] + rank
    n_rows = (-(-(m * TOP_K) // tm) + N_EXPERTS) * tm
    row_token = jnp.zeros((n_rows,), I32).at[pos].set(jnp.arange(m * TOP_K, dtype=I32) // TOP_K)
    row_gate = jnp.zeros((n_rows,), F32).at[pos].set(top_w.reshape(-1))
    tile_start = jnp.arange(n_rows // tm, dtype=I32) * tm
    tile_expert = jnp.clip(jnp.searchsorted(jnp.cumsum(padded), tile_start, side='right'), 0, N_EXPERTS - 1)
    tile_valid = (tile_start < jnp.sum(padded)).astype(I32)
    return pos.reshape(m, TOP_K), row_token, row_gate.reshape(n_rows, 1), tile_expert.astype(I32), tile_valid


def _dense_ffn(h, g, w1, w3, w2):
    m = h.shape[0]
    tm = _row_tile(m)
    ones = jnp.ones((m // tm,), I32)
    return grouped_ffn(h, g, jnp.ones((m, 1), F32), jnp.zeros((m // tm,), I32), ones,
                       w1[None], w3[None], w2[None], tm=tm, tf=w1.shape[1] // 2, residual=True)


def kernel(x_prompt, x_sample, p_prompt, p_sample, state_ret, cache_win_kv, cache_cmp_kv, cache_sel_kv, page_table, rel_bias, norm_mix, w_in, ret_gain, cmp_pe, cmp_w1, cmp_w2, w_out, norm_ffn, ffn_w1, ffn_w3, ffn_w2, router, moe_w1, moe_w3, moe_w2, ple_norm, ple_gate, ple_proj, norm_final):
    b_p, t_p, d = x_prompt.shape
    b_s = x_sample.shape[0]
    depth = w_in.shape[0]
    n_pages = page_table.shape[1]
    past = n_pages * PAGE_SIZE
    n_pool = cache_cmp_kv.shape[1]
    w_buf = cache_win_kv.shape[2]
    assert x_sample.shape[1] == 1 and t_p % Q_BLOCK == 0 and t_p >= w_buf and past % CMP_STRIDE == 0
    row = (2, N_KV, HEAD_DIM)
    m_p = b_p * t_p

    ar = lambda n: jnp.arange(n, dtype=I32)
    n_sub_p = t_p // CMP_STRIDE
    bias_ct = bias_lookup(rel_bias, ar(t_p)[None, :] - (ar(n_sub_p)[:, None] * CMP_STRIDE + L_CMP - 1))
    rel_t = (ar(FAR_REL + 1)[:, None, None] * KV_CHUNK + ar(Q_BLOCK)[None, None, :] - ar(KV_CHUNK)[None, :, None])
    bias_tt = bias_lookup(rel_bias, rel_t.reshape((FAR_REL + 1) * KV_CHUNK, Q_BLOCK))
    bias_tt = bias_tt.reshape(H_NSA, FAR_REL + 1, KV_CHUNK, Q_BLOCK).transpose(1, 0, 2, 3)
    n_sub_s = (past + 1) // CMP_STRIDE
    rows8 = jnp.zeros((SUBLANES, 1), I32)
    bias_cs = bias_lookup(rel_bias, rows8 + (past - (ar(n_sub_s)[None, :] * CMP_STRIDE + L_CMP - 1)))[:, 0]
    bias_ss = bias_lookup(rel_bias, rows8 + (past - ar((n_pages + 1) * PAGE_SIZE)[None, :]))[:, 0]
    bias_ws = bias_lookup(rel_bias, rows8 + (w_buf - ar(w_buf + KV_CHUNK)[None, :]))[:, 0]

    def rows_last(x):
        return jnp.moveaxis(x, 2, -1).reshape(x.shape[0] * x.shape[1], KV_W, x.shape[2])

    def rows_first(xt, lead):
        return jnp.moveaxis(xt.reshape((lead,) + row + (xt.shape[-1],)), -1, 1)

    cmp_cache = rows_last(cache_cmp_kv)
    sel_cache = rows_last(cache_sel_kv)
    win_cache = rows_last(cache_win_kv)
    state_t = jnp.moveaxis(state_ret, 1, -1).reshape(depth * H_RET, HEAD_DIM, HEAD_DIM, b_s)

    h_p = x_prompt.reshape(m_p, d)
    h_s = x_sample.reshape(b_s, d)
    rp, rs, wp, ws, cp, cs, sp, ss = [], [], [], [], [], [], [], []
    for i in range(depth):
        w_in_b = jnp.pad(w_in[i], ((0, 0), (0, N_IN_PAD - N_IN))).astype(BF16)
        w_out_b = w_out[i].astype(BF16)
        pe_t, cw1, cw2 = _compress_weights(cmp_pe[i], cmp_w1[i], cmp_w2[i])

        w_kvt = w_in[i][:, COL_KVC:COL_KVC + 3 * KV_W].T.astype(BF16)
        z, kvt = rms_matmul_kvt(h_p, norm_mix[i], w_in_b, w_kvt, b_p)
        z3 = z.reshape(b_p, t_p, N_IN_PAD)
        ret_y, ret_state = retention_prompt(z3, ret_gain[i])
        kc, vc = compress(kvt, ar(b_p)[:, None], pe_t, cw1, cw2)
        nsa_y = nsa_prompt(z3, kvt, kc, jnp.swapaxes(vc, 1, 2), bias_ct, bias_tt)
        h_p = out_proj(h_p, ret_y.reshape(m_p, RET_W), nsa_y.reshape(m_p, NSA_QW), w_out_b)
        rp.append(ret_state)
        cp.append(rows_first(kvt[:, 0:KV_W], b_p))
        sp.append(rows_first(kvt[:, KV_W:2 * KV_W], b_p))
        wp.append(rows_first(kvt[:, 2 * KV_W:, t_p - w_buf:], b_p))

        zs = rms_matmul(h_s, norm_mix[i], w_in_b)
        zst = zs.T
        ret_yst, state_s = retention_step(zst, state_t, i, past, ret_gain[i])
        page_idx = page_table + i * n_pool
        grp = 4 if b_s % 4 == 0 else 1
        kc_s, vc_s = compress(cmp_cache, page_idx.reshape(b_s // grp, grp * n_pages), pe_t, cw1, cw2)
        kc_s, vc_s = kc_s.reshape(b_s, n_sub_s, LANES), vc_s.reshape(b_s, n_sub_s, LANES)
        nsa_ys, win_s = nsa_step(zs.reshape(b_s, 1, N_IN_PAD), kc_s, vc_s, sel_cache, win_cache, page_idx, i * b_s,
                                 bias_cs, bias_ss, bias_ws, past)
        h_s = out_proj(h_s, ret_yst.T, nsa_ys.reshape(b_s, NSA_QW), w_out_b)
        rs.append(jnp.moveaxis(state_s, -1, 0))
        ws.append(rows_first(win_s, b_s))
        cs.append(rows_first(zst[None, COL_KVC:COL_KVC + KV_W], 1).reshape((b_s, 1) + row))
        ss.append(rows_first(zst[None, COL_KVS:COL_KVS + KV_W], 1).reshape((b_s, 1) + row))

        j = i // 2
        if i % 2 == 0:
            w1, w3, w2 = ffn_w1[j].astype(BF16), ffn_w3[j].astype(BF16), ffn_w2[j].astype(BF16)
            h_p = _dense_ffn(h_p, norm_ffn[i], w1, w3, w2)
            h_s = _dense_ffn(h_s, norm_ffn[i], w1, w3, w2)
        else:
            router_w = jnp.pad(router[j], ((0, 0), (0, LANES - N_EXPERTS)))
            tf = moe_w1.shape[3] // 7
            logits_p = router_logits(h_p, norm_ffn[i], router_w)
            tm = 1024
            pos, row_token, row_gate, tile_expert, tile_valid = _route(logits_p[:, :N_EXPERTS], tm)
            x_rows = gather_rows(h_p.reshape(m_p, SUBLANES, d // SUBLANES), row_token)
            y_rows = grouped_ffn(x_rows, norm_ffn[i], row_gate, tile_expert, tile_valid,
                                 moe_w1[j], moe_w3[j], moe_w2[j], tm=tm, tf=tf, residual=False)
            h_p = h_p + y_rows[pos[:, 0]] + y_rows[pos[:, 1]]
            logits_s = router_logits(h_s, norm_ffn[i], router_w)
            top_v, top_i = lax.top_k(logits_s[:, :N_EXPERTS], TOP_K)
            top_w = jax.nn.softmax(top_v, axis=-1)
            gates = jnp.sum(jax.nn.one_hot(top_i, N_EXPERTS, dtype=F32) * top_w[..., None], axis=-2)
            y_e = grouped_ffn(jnp.tile(h_s, (N_EXPERTS, 1)), norm_ffn[i], gates.T.reshape(N_EXPERTS * b_s, 1),
                              jnp.arange(N_EXPERTS, dtype=I32), jnp.ones((N_EXPERTS,), I32),
                              moe_w1[j], moe_w3[j], moe_w2[j], tm=b_s, tf=tf, residual=False)
            h_s = h_s + jnp.sum(y_e.reshape(N_EXPERTS, b_s, d), axis=0)

        wg, wpj = ple_gate[i].astype(BF16), ple_proj[i].astype(BF16)
        h_p = ple(h_p, ple_norm[i], wg, p_prompt[i].reshape(m_p, P_DIM), wpj, norm_final, i == depth - 1)
        h_s = ple(h_s, ple_norm[i], wg, p_sample[i].reshape(b_s, P_DIM), wpj, norm_final, i == depth - 1)

    return (h_p.reshape(b_p, t_p, d), h_s.reshape(b_s, 1, d), jnp.stack(rp), jnp.stack(rs), jnp.stack(wp),
            jnp.stack(ws), jnp.stack(cp), jnp.stack(cs), jnp.stack(sp), jnp.stack(ss))
```

```python
import functools
import math

import numpy as np
import jax
import jax.numpy as jnp
from jax import lax
from jax.experimental import pallas as pl
from jax.experimental.pallas import tpu as pltpu

F32 = jnp.float32
BF16 = jnp.bfloat16
I32 = jnp.int32

LANES = 128
SUBLANES = 8
VMEM_LIMIT_BYTES = 56 * 1024 * 1024

D_MODEL = 1024
HEAD_DIM = 64
H_RET = 8
H_NSA = 8
N_KV = 2
R_GRP = H_NSA // N_KV
RET_W = H_RET * HEAD_DIM
NSA_QW = H_NSA * HEAD_DIM
KV_W = 2 * N_KV * HEAD_DIM
N_IN = 4 * RET_W + NSA_QW + 3 * KV_W + 3 * H_NSA
N_IN_PAD = 3584
COL_RQ, COL_RK, COL_RV, COL_RG = 0, RET_W, 2 * RET_W, 3 * RET_W
COL_NQ = 4 * RET_W
COL_KVC = COL_NQ + NSA_QW
COL_KVS = COL_KVC + KV_W
COL_KVW = COL_KVS + KV_W
COL_NG = COL_KVW + KV_W
RET_CHUNK = 128
CMP_STRIDE = 16
L_CMP = 2 * CMP_STRIDE
CMP_HIDDEN = 2 * HEAD_DIM
SEL_BLOCK = 64
N_SELECT = 16
WINDOW = 512
Q_BLOCK = 128
N_BUCKETS = 32
MAX_DISTANCE = 128
N_EXPERTS = 8
TOP_K = 2
P_DIM = 256
PAGE_SIZE = 128
ROPE_BASE = 10000.0
EPS = 1e-6
FORCED_SCORE = 1e4
NEG = -1e30
TINY = float(np.finfo(np.float32).tiny)
SEL_LANES = 64
FAR_REL = 2
KV_CHUNK = 128


def _cparams(n_grid):
    return pltpu.CompilerParams(dimension_semantics=("arbitrary",) * n_grid,
                                vmem_limit_bytes=VMEM_LIMIT_BYTES)


def _dot(a, b):
    return jnp.dot(a, b, preferred_element_type=F32)


def _dot_nt(a, b):
    return lax.dot_general(a, b, (((1,), (1,)), ((), ())), preferred_element_type=F32)


def _split3(x):
    hi = x.astype(BF16)
    r = x - hi.astype(F32)
    mid = r.astype(BF16)
    lo = (r - mid.astype(F32)).astype(BF16)
    return hi, mid, lo


def _dot3(x, b01):
    hi, mid, lo = _split3(x)
    return _dot(hi, b01) + _dot(mid, b01) + _dot(lo, b01)


def _dot6(x, w):
    xh, xm, xl = _split3(x)
    wh, wm, wl = _split3(w)
    return (_dot(xh, wh) + _dot(xh, wm) + _dot(xm, wh)
            + _dot(xm, wm) + _dot(xh, wl) + _dot(xl, wh))


def _sigmoid(x):
    return 1.0 / (1.0 + jnp.exp(-x))


def _rms(x, g):
    return x * lax.rsqrt(jnp.mean(x * x, axis=-1, keepdims=True) + EPS) * g


def _row_tile(m):
    for t in (512, 256, 128, 64, 32, 16, 8):
        if m % t == 0:
            return t
    raise ValueError(f"row count {m} not a multiple of 8")


def _rms_matmul_kernel(x_ref, g_ref, w_ref, o_ref):
    o_ref[...] = _dot(_rms(x_ref[...], g_ref[...]).astype(BF16), w_ref[...])


def rms_matmul(x, g, w):
    m, d = x.shape
    n = w.shape[1]
    tm = min(_row_tile(m), 256)
    return pl.pallas_call(
        _rms_matmul_kernel,
        grid=(m // tm,),
        in_specs=[pl.BlockSpec((tm, d), lambda i: (i, 0)),
                  pl.BlockSpec((1, d), lambda i: (0, 0)),
                  pl.BlockSpec((d, n), lambda i: (0, 0))],
        out_specs=pl.BlockSpec((tm, n), lambda i: (i, 0)),
        out_shape=jax.ShapeDtypeStruct((m, n), F32),
        compiler_params=_cparams(1),
    )(x, g.reshape(1, d), w)


def _rms_matmul_kvt_kernel(x_ref, g_ref, w_ref, wt_ref, o_ref, ot_ref):
    xn = _rms(x_ref[...], g_ref[...]).astype(BF16)
    o_ref[...] = _dot(xn, w_ref[...])
    ot_ref[...] = _dot_nt(wt_ref[...], xn)


def rms_matmul_kvt(x, g, w, wt, bsz):
    m, d = x.shape
    n = w.shape[1]
    nt = wt.shape[0]
    t = m // bsz
    tm = min(_row_tile(t), 256)
    per = t // tm
    return pl.pallas_call(
        _rms_matmul_kvt_kernel,
        grid=(m // tm,),
        in_specs=[pl.BlockSpec((tm, d), lambda i: (i, 0)),
                  pl.BlockSpec((1, d), lambda i: (0, 0)),
                  pl.BlockSpec((d, n), lambda i: (0, 0)),
                  pl.BlockSpec((nt, d), lambda i: (0, 0))],
        out_specs=[pl.BlockSpec((tm, n), lambda i: (i, 0)),
                   pl.BlockSpec((None, nt, tm), lambda i: (i // per, 0, i % per))],
        out_shape=[jax.ShapeDtypeStruct((m, n), F32), jax.ShapeDtypeStruct((bsz, nt, t), F32)],
        compiler_params=_cparams(1),
        name="rms_matmul_kvt",
    )(x, g.reshape(1, d), w, wt)


def _router_kernel(x_ref, g_ref, w_ref, o_ref):
    o_ref[...] = _dot6(_rms(x_ref[...], g_ref[...]), w_ref[...])


def router_logits(x, g, w):
    m, d = x.shape
    n = w.shape[1]
    tm = min(_row_tile(m), 256)
    return pl.pallas_call(
        _router_kernel,
        grid=(m // tm,),
        in_specs=[pl.BlockSpec((tm, d), lambda i: (i, 0)),
                  pl.BlockSpec((1, d), lambda i: (0, 0)),
                  pl.BlockSpec((d, n), lambda i: (0, 0))],
        out_specs=pl.BlockSpec((tm, n), lambda i: (i, 0)),
        out_shape=jax.ShapeDtypeStruct((m, n), F32),
        compiler_params=_cparams(1),
        name="router",
    )(x, g.reshape(1, d), w)


def _outproj_kernel(h_ref, a_ref, b_ref, w_ref, o_ref):
    ka = a_ref.shape[1]
    o_ref[...] = (h_ref[...] + _dot(a_ref[...].astype(BF16), w_ref[0:ka, :])
                  + _dot(b_ref[...].astype(BF16), w_ref[ka:, :]))


def out_proj(h, a, b, w):
    m, d = h.shape
    ka, kb = a.shape[1], b.shape[1]
    tm = _row_tile(m)
    return pl.pallas_call(
        _outproj_kernel,
        grid=(m // tm,),
        in_specs=[pl.BlockSpec((tm, d), lambda i: (i, 0)),
                  pl.BlockSpec((tm, ka), lambda i: (i, 0)),
                  pl.BlockSpec((tm, kb), lambda i: (i, 0)),
                  pl.BlockSpec((ka + kb, d), lambda i: (0, 0))],
        out_specs=pl.BlockSpec((tm, d), lambda i: (i, 0)),
        out_shape=jax.ShapeDtypeStruct((m, d), F32),
        compiler_params=_cparams(1),
        name="out_proj",
    )(h, a, b, w)


def _ffn_kernel(te_ref, tv_ref, x_ref, g_ref, gate_ref, w1_ref, w3_ref, w2_ref, o_ref,
                xn_ref, acc_ref, *, residual):
    i = pl.program_id(0)
    j = pl.program_id(1)

    @pl.when(j == 0)
    def _():
        if len(x_ref.shape) == 3:
            x = jnp.concatenate([x_ref[:, s, :] for s in range(x_ref.shape[1])], axis=1)
        else:
            x = x_ref[...]
        xn_ref[...] = _rms(x, g_ref[...]).astype(BF16)
        acc_ref[...] = jnp.zeros_like(acc_ref)

    @pl.when(tv_ref[i] != 0)
    def _():
        xn = xn_ref[...]
        a = _dot(xn, w1_ref[...].astype(BF16))
        b = _dot(xn, w3_ref[...].astype(BF16))
        hid = (a * _sigmoid(a)) * b
        acc_ref[...] += _dot(hid.astype(BF16), w2_ref[...].astype(BF16))

    @pl.when(j == pl.num_programs(1) - 1)
    def _():
        y = acc_ref[...] * gate_ref[...]
        if residual:
            y = y + x_ref[...]
        o_ref[...] = y


def grouped_ffn(x, g, row_gate, tile_expert, tile_valid, w1, w3, w2, *, tm, tf, residual):
    m = x.shape[0]
    d = w1.shape[1]
    f = w1.shape[2]
    assert not (residual and x.ndim == 3)
    x_spec = (pl.BlockSpec((tm,) + x.shape[1:], lambda i, j, te, tv: (i, 0, 0)) if x.ndim == 3
              else pl.BlockSpec((tm, d), lambda i, j, te, tv: (i, 0)))
    grid_spec = pltpu.PrefetchScalarGridSpec(
        num_scalar_prefetch=2,
        grid=(m // tm, f // tf),
        in_specs=[x_spec,
                  pl.BlockSpec((1, d), lambda i, j, te, tv: (0, 0)),
                  pl.BlockSpec((tm, 1), lambda i, j, te, tv: (i, 0)),
                  pl.BlockSpec((None, d, tf), lambda i, j, te, tv: (te[i], 0, j)),
                  pl.BlockSpec((None, d, tf), lambda i, j, te, tv: (te[i], 0, j)),
                  pl.BlockSpec((None, tf, d), lambda i, j, te, tv: (te[i], j, 0))],
        out_specs=pl.BlockSpec((tm, d), lambda i, j, te, tv: (i, 0)),
        scratch_shapes=[pltpu.VMEM((tm, d), BF16), pltpu.VMEM((tm, d), F32)],
    )
    return pl.pallas_call(
        functools.partial(_ffn_kernel, residual=residual),
        grid_spec=grid_spec,
        out_shape=jax.ShapeDtypeStruct((m, d), F32),
        compiler_params=_cparams(2),
        name="grouped_ffn",
    )(tile_expert, tile_valid, x, g.reshape(1, d), row_gate, w1, w3, w2)


def _gather_rows_kernel(idx_ref, x_hbm, o_ref, sem):
    base = pl.program_id(0) * o_ref.shape[0]

    def row_copy(k):
        return pltpu.make_async_copy(x_hbm.at[idx_ref[base + k]], o_ref.at[k], sem)

    def start(k, carry):
        row_copy(k).start()
        return carry

    lax.fori_loop(0, o_ref.shape[0], start, 0, unroll=8)
    pltpu.make_async_copy(x_hbm.at[pl.ds(0, o_ref.shape[0])], o_ref, sem).wait()


def gather_rows(x, idx, rows_per_step=256):
    n = idx.shape[0]
    assert n % rows_per_step == 0 and x.dtype == F32 and x.shape[1:] == (SUBLANES, LANES)
    grid_spec = pltpu.PrefetchScalarGridSpec(
        num_scalar_prefetch=1,
        grid=(n // rows_per_step,),
        in_specs=[pl.BlockSpec(memory_space=pl.ANY)],
        out_specs=pl.BlockSpec((rows_per_step, SUBLANES, LANES), lambda i, ix: (i, 0, 0)),
        scratch_shapes=[pltpu.SemaphoreType.DMA],
    )
    return pl.pallas_call(
        _gather_rows_kernel,
        grid_spec=grid_spec,
        out_shape=jax.ShapeDtypeStruct((n, SUBLANES, LANES), x.dtype),
        compiler_params=_cparams(1),
        name="gather_rows",
    )(idx, x)


def _ple_kernel(h_ref, g_ref, wg_ref, p_ref, wp_ref, gf_ref, o_ref, *, final):
    h = h_ref[...]
    gate = _sigmoid(_dot(_rms(h, g_ref[...]).astype(BF16), wg_ref[...]))
    proj = _dot(p_ref[...].astype(BF16), wp_ref[...])
    h2 = h + gate * proj
    o_ref[...] = _rms(h2, gf_ref[...]) if final else h2


def ple(h, g, wg, p, wp, gf, final):
    m, d = h.shape
    pd = p.shape[1]
    tm = _row_tile(m)
    row = lambda i: (i, 0)
    fixed = lambda i: (0, 0)
    return pl.pallas_call(
        functools.partial(_ple_kernel, final=final),
        grid=(m // tm,),
        in_specs=[pl.BlockSpec((tm, d), row), pl.BlockSpec((1, d), fixed),
                  pl.BlockSpec((d, d), fixed), pl.BlockSpec((tm, pd), row),
                  pl.BlockSpec((pd, d), fixed), pl.BlockSpec((1, d), fixed)],
        out_specs=pl.BlockSpec((tm, d), row),
        out_shape=jax.ShapeDtypeStruct((m, d), F32),
        compiler_params=_cparams(1),
        name="ple",
    )(h, g.reshape(1, d), wg, p, wp, gf.reshape(1, d))


def _bias_kernel(tab_ref, d_ref, o_ref):
    n = jnp.maximum(d_ref[...], 0)
    max_exact = N_BUCKETS // 2
    nf = jnp.maximum(n, 1).astype(F32)
    large = max_exact + (jnp.log(nf / max_exact) / math.log(MAX_DISTANCE / max_exact)
                         * (N_BUCKETS - max_exact)).astype(I32)
    large = jnp.minimum(large, N_BUCKETS - 1)
    bucket = jnp.where(n < max_exact, n, large)
    for h in range(H_NSA):
        acc = jnp.zeros(n.shape, F32)
        for k in range(N_BUCKETS):
            acc = jnp.where(bucket == k, tab_ref[k, h], acc)
        o_ref[h] = acc


def bias_lookup(table, dist):
    r, c = dist.shape
    tr = min(r, 128)
    tc = 512 if c % 512 == 0 else c
    return pl.pallas_call(
        _bias_kernel,
        grid=(r // tr, c // tc),
        in_specs=[pl.BlockSpec(memory_space=pltpu.SMEM),
                  pl.BlockSpec((tr, tc), lambda i, j: (i, j))],
        out_specs=pl.BlockSpec((H_NSA, tr, tc), lambda i, j: (0, i, j)),
        out_shape=jax.ShapeDtypeStruct((H_NSA, r, c), F32),
        compiler_params=_cparams(2),
        name="bias_lookup",
    )(table, dist)


def _swap_halves(x):
    lane = lax.broadcasted_iota(I32, x.shape, 1)
    return jnp.where(lane % HEAD_DIM < HEAD_DIM // 2,
                     pltpu.roll(x, LANES - HEAD_DIM // 2, 1), pltpu.roll(x, HEAD_DIM // 2, 1))


def _retention_prompt_kernel(q_ref, k_ref, v_ref, g_ref, cos_ref, sin_ref, gain_ref, intra_ref,
                             qdec_ref, kdec_ref, cdec_ref, y_ref, s_out_ref, s_ref):
    c = pl.program_id(1)

    @pl.when(c == 0)
    def _():
        s_ref[...] = jnp.zeros_like(s_ref)

    cos = cos_ref[...]
    sin = sin_ref[...]
    lane = lax.broadcasted_iota(I32, (RET_CHUNK, LANES), 1)
    left = lane < HEAD_DIM
    row = lax.broadcasted_iota(I32, (LANES, LANES), 0)
    col = lax.broadcasted_iota(I32, (LANES, LANES), 1)
    same_head = (row < HEAD_DIM) == (col < HEAD_DIM)
    ones_bd = jnp.where(same_head, 1.0, 0.0).astype(BF16)
    for p in range(H_RET // 2):
        sl = slice(p * LANES, (p + 1) * LANES)
        q = q_ref[:, sl]
        k = k_ref[:, sl]
        v = v_ref[:, sl].astype(BF16)
        qr = q * cos + _swap_halves(q) * sin
        kr = (k * cos + _swap_halves(k) * sin) * (HEAD_DIM ** -0.5)
        qb = qr.astype(BF16)
        q2 = jnp.concatenate([jnp.where(left, qr, 0.0), jnp.where(left, 0.0, qr)], axis=0).astype(BF16)
        inner = _dot_nt(q2, kr.astype(BF16)) * intra_ref[p]
        o2 = _dot(inner.astype(BF16), v)
        s_old = s_ref[p]
        cross = _dot(qb, s_old.astype(BF16)) * qdec_ref[:, sl]
        o = jnp.where(left, o2[:RET_CHUNK], o2[RET_CHUNK:]) + cross
        kd = (kr * kdec_ref[:, sl]).T.astype(BF16)
        s_ref[p] = s_old * cdec_ref[p] + jnp.where(same_head, _dot(kd, v), 0.0)
        ms = _dot3(o * o, ones_bd) * (1.0 / HEAD_DIM)
        gate = g_ref[:, sl]
        y = o * lax.rsqrt(ms + EPS) * gain_ref[:, sl] * (gate * _sigmoid(gate))
        y_ref[:, sl] = y.astype(y_ref.dtype)

    @pl.when(c == pl.num_programs(1) - 1)
    def _():
        for p in range(H_RET // 2):
            s = s_ref[p]
            s_out_ref[2 * p] = s[:HEAD_DIM, :HEAD_DIM]
            s_out_ref[2 * p + 1] = s[HEAD_DIM:, HEAD_DIM:]


def _retention_consts(c):
    log_g = jnp.log1p(-jnp.exp2(-5.0 - jnp.arange(H_RET, dtype=F32)))
    j = jnp.arange(c, dtype=F32)
    diff = j[:, None] - j[None, :]
    intra = jnp.where(diff >= 0, jnp.exp(jnp.maximum(diff, 0.0)[None] * log_g[:, None, None]), 0.0)
    q_decay = jnp.exp((j[:, None] + 1.0) * log_g[None, :])
    k_decay = jnp.exp((c - 1.0 - j[:, None]) * log_g[None, :])
    c_decay = jnp.exp(c * log_g)
    intra2 = intra.reshape(H_RET // 2, 2 * c, c)
    qdec = jnp.repeat(q_decay, HEAD_DIM, axis=1)
    kdec = jnp.repeat(k_decay, HEAD_DIM, axis=1)
    cdec = jnp.repeat(c_decay.reshape(H_RET // 2, 2), HEAD_DIM, axis=1)[:, :, None] * jnp.ones((1, 1, LANES), F32)
    return intra2, qdec, kdec, cdec


def _rotary_tables(pos):
    half = HEAD_DIM // 2
    inv = ROPE_BASE ** (-jnp.arange(half, dtype=F32) / half)
    ang = pos.astype(F32)[:, None] * inv[None, :]
    cos, sin = jnp.cos(ang), jnp.sin(ang)
    cos_t = jnp.concatenate([cos, cos, cos, cos], axis=1)
    sin_t = jnp.concatenate([-sin, sin, -sin, sin], axis=1)
    return cos_t, sin_t


def retention_prompt(z3, gain):
    b, t, _ = z3.shape
    c = RET_CHUNK
    cos_t, sin_t = _rotary_tables(jnp.arange(t))
    intra2, qdec, kdec, cdec = _retention_consts(c)
    blk = lambda col: pl.BlockSpec((None, c, RET_W), lambda bi, ci, col=col: (bi, ci, col))
    fixed2 = lambda shape: pl.BlockSpec(shape, lambda bi, ci: (0,) * len(shape))
    return pl.pallas_call(
        _retention_prompt_kernel,
        grid=(b, t // c),
        in_specs=[blk(COL_RQ // RET_W), blk(COL_RK // RET_W), blk(COL_RV // RET_W), blk(COL_RG // RET_W),
                  pl.BlockSpec((c, LANES), lambda bi, ci: (ci, 0)),
                  pl.BlockSpec((c, LANES), lambda bi, ci: (ci, 0)),
                  fixed2((1, RET_W)), fixed2((H_RET // 2, 2 * c, c)), fixed2((c, RET_W)),
                  fixed2((c, RET_W)), fixed2((H_RET // 2, LANES, LANES))],
        out_specs=[pl.BlockSpec((None, c, RET_W), lambda bi, ci: (bi, ci, 0)),
                   pl.BlockSpec((None, H_RET, HEAD_DIM, HEAD_DIM), lambda bi, ci: (bi, 0, 0, 0))],
        out_shape=[jax.ShapeDtypeStruct((b, t, RET_W), BF16),
                   jax.ShapeDtypeStruct((b, H_RET, HEAD_DIM, HEAD_DIM), F32)],
        scratch_shapes=[pltpu.VMEM((H_RET // 2, LANES, LANES), F32)],
        compiler_params=_cparams(2),
        name="retention_prompt",
    )(z3, z3, z3, z3, cos_t, sin_t, gain.reshape(1, RET_W), intra2, qdec, kdec, cdec)


def _retention_step_kernel(q_ref, k_ref, v_ref, g_ref, s_ref, cos_ref, sin_ref, gamma_ref, gain_ref,
                           y_ref, s_out_ref):
    half = HEAD_DIM // 2
    cos = cos_ref[...]
    sin = sin_ref[...]

    def rot(x):
        x1, x2 = x[:half], x[half:]
        return jnp.concatenate([x1 * cos - x2 * sin, x1 * sin + x2 * cos], axis=0)

    qr = rot(q_ref[...])
    kr = rot(k_ref[...]) * (HEAD_DIM ** -0.5)
    v = v_ref[...]
    gamma = gamma_ref[...]
    o = jnp.zeros_like(v)
    for d in range(HEAD_DIM):
        s_new = s_ref[d] * gamma + kr[d:d + 1] * v
        s_out_ref[d] = s_new
        o = o + qr[d:d + 1] * s_new
    ms = jnp.mean(o * o, axis=0, keepdims=True)
    gate = g_ref[...]
    y_ref[...] = o * lax.rsqrt(ms + EPS) * gain_ref[...] * (gate * _sigmoid(gate))


def retention_step(zt, state, layer, pos, gain):
    bs = zt.shape[1]
    half = HEAD_DIM // 2
    inv = ROPE_BASE ** (-jnp.arange(half, dtype=F32) / half)
    ang = jnp.full((1,), pos).astype(F32)[:, None] * inv[None, :]
    cos = jnp.broadcast_to(jnp.cos(ang).reshape(half, 1), (half, bs))
    sin = jnp.broadcast_to(jnp.sin(ang).reshape(half, 1), (half, bs))
    gamma = 1.0 - jnp.exp2(-5.0 - jnp.arange(H_RET, dtype=F32))
    gamma = jnp.broadcast_to(gamma.reshape(H_RET, 1, 1), (H_RET, 1, bs))
    gain_b = jnp.broadcast_to(gain.reshape(H_RET, HEAD_DIM, 1), (H_RET, HEAD_DIM, bs))
    head = lambda col: pl.BlockSpec((HEAD_DIM, bs), lambda h, col=col: (col // HEAD_DIM + h, 0))
    fixed = pl.BlockSpec((half, bs), lambda h: (0, 0))
    return pl.pallas_call(
        _retention_step_kernel,
        grid=(H_RET,),
        in_specs=[head(COL_RQ), head(COL_RK), head(COL_RV), head(COL_RG),
                  pl.BlockSpec((None, HEAD_DIM, HEAD_DIM, bs), lambda h: (layer * H_RET + h, 0, 0, 0)),
                  fixed, fixed,
                  pl.BlockSpec((None, 1, bs), lambda h: (h, 0, 0)),
                  pl.BlockSpec((None, HEAD_DIM, bs), lambda h: (h, 0, 0))],
        out_specs=[pl.BlockSpec((HEAD_DIM, bs), lambda h: (h, 0)),
                   pl.BlockSpec((None, HEAD_DIM, HEAD_DIM, bs), lambda h: (h, 0, 0, 0))],
        out_shape=[jax.ShapeDtypeStruct((RET_W, bs), F32),
                   jax.ShapeDtypeStruct((H_RET, HEAD_DIM, HEAD_DIM, bs), F32)],
        compiler_params=_cparams(1),
        name="retention_step",
    )(zt, zt, zt, zt, state, cos, sin, gamma, gain_b)


def _gelu_tanh(x):
    return 0.5 * x * (1.0 + jnp.tanh(math.sqrt(2.0 / math.pi) * (x + 0.044715 * (x * x * x))))


def _compress_kernel(*refs, n_pieces):
    x_refs = refs[1:1 + n_pieces]
    pe_ref, w1_ref, w2_ref, kc_ref, vc_ref, rows_ref = refs[1 + n_pieces:]
    width = x_refs[0].shape[1]
    n = n_pieces * width // CMP_STRIDE
    for kv, o_ref in enumerate((kc_ref, vc_ref)):
        for p, x_ref in enumerate(x_refs):
            for c in range(width // LANES):
                rows_ref[pl.ds(p * width + c * LANES, LANES), :] = (
                    x_ref[kv * LANES:(kv + 1) * LANES, c * LANES:(c + 1) * LANES].T)
        xk = jnp.concatenate([rows_ref[pl.ds(j, n, stride=CMP_STRIDE), :] for j in range(CMP_STRIDE)], axis=1)
        top = _dot((xk + pe_ref[kv, 0]).astype(BF16), w1_ref[kv, 0])
        bot = _dot((xk + pe_ref[kv, 1]).astype(BF16), w1_ref[kv, 1])
        pre = top + pltpu.roll(bot, n - 1, 0)
        hid = _gelu_tanh(pre).astype(BF16)
        o_ref[...] = jnp.concatenate(
            [_dot(hid[:, g * CMP_HIDDEN:(g + 1) * CMP_HIDDEN], w2_ref[kv]) for g in range(N_KV)], axis=1)


def _compress_weights(pe, w1, w2):
    pe_r = pe.reshape(2, 2, CMP_STRIDE, 1, HEAD_DIM)
    pe_t = jnp.broadcast_to(pe_r, (2, 2, CMP_STRIDE, N_KV, HEAD_DIM)).reshape(2, 2, 1, CMP_STRIDE * LANES)
    w1r = w1.reshape(2, 2, CMP_STRIDE, HEAD_DIM, CMP_HIDDEN)
    eye = jnp.eye(N_KV, dtype=w1.dtype)
    wbig = jnp.einsum('khjdc,gG->khjgdGc', w1r, eye)
    wbig = wbig.reshape(2, 2, CMP_STRIDE * LANES, N_KV * CMP_HIDDEN)
    return pe_t.astype(F32), wbig.astype(BF16), w2.astype(BF16)


def compress(src, idx, pe_t, w1b, w2b):
    bsz, n_pieces = idx.shape
    width = src.shape[2]
    n = n_pieces * width // CMP_STRIDE
    piece = lambda p: pl.BlockSpec((None, KV_W, width), lambda b, ix, p=p: (ix[b, p], 0, 0))
    fixed = lambda shape: pl.BlockSpec(shape, lambda b, ix: (0,) * len(shape))
    out = pl.BlockSpec((None, n, LANES), lambda b, ix: (b, 0, 0))
    grid_spec = pltpu.PrefetchScalarGridSpec(
        num_scalar_prefetch=1,
        grid=(bsz,),
        in_specs=[piece(p) for p in range(n_pieces)]
        + [fixed(pe_t.shape), fixed(w1b.shape), fixed(w2b.shape)],
        out_specs=[out, out],
        scratch_shapes=[pltpu.VMEM((n_pieces * width, LANES), F32)],
    )
    return pl.pallas_call(
        functools.partial(_compress_kernel, n_pieces=n_pieces),
        grid_spec=grid_spec,
        out_shape=[jax.ShapeDtypeStruct((bsz, n, LANES), F32)] * 2,
        compiler_params=_cparams(1),
        name="compress",
    )(idx, *([src] * n_pieces), pe_t, w1b, w2b)


def _masked_softmax(s, mask):
    s = jnp.where(mask, s, NEG)
    m = jnp.max(s, axis=-1, keepdims=True)
    e = jnp.where(mask, jnp.exp(s - m), 0.0)
    return e / jnp.maximum(jnp.sum(e, axis=-1, keepdims=True), TINY)


def _block_scores(imp, qpos, n_sel, axis=1):
    blk = lax.broadcasted_iota(I32, imp.shape, axis)
    cur = qpos // SEL_BLOCK
    forced = (blk == 0) | (blk == cur) | (blk == cur - 1)
    valid = blk * SEL_BLOCK <= qpos
    score = jnp.where(forced, FORCED_SCORE, jnp.where(valid, imp, -1.0))
    return jnp.where(blk < n_sel, score, -2.0)


def _select_top(score, n_top, axis=1):
    blk = lax.broadcasted_iota(I32, score.shape, axis)
    rank = jnp.zeros(score.shape, F32)
    for i in range(score.shape[axis]):
        one = score[:, i:i + 1] if axis == 1 else score[i:i + 1, :]
        beats = (one > score) | ((one == score) & (blk > i))
        rank = rank + jnp.where(beats, 1.0, 0.0)
    return jnp.where(rank < n_top, 1.0, 0.0)


def _attend_t(s, ok, m_prev, l_prev):
    s = jnp.where(ok, s, NEG)
    m_new = jnp.maximum(m_prev, jnp.max(s, axis=0, keepdims=True))
    alpha = jnp.exp(m_prev - m_new)
    p = jnp.exp(s - m_new)
    l_new = alpha * l_prev + jnp.sum(p, axis=0, keepdims=True)
    return m_new, l_new, alpha, p


def _nsa_prompt_kernel(q_ref, gate_ref, kc_ref, vct_ref, ks_ref, vst_ref, kw_ref, vwt_ref, bc_ref, bt_ref,
                       o_ref, qt_ref, pcs_ref, pick_ref, acc_ref, os_ref, s_ref, p_ref, *, n_cmp, n_sel):
    i = pl.program_id(1)
    qs = i * Q_BLOCK
    key_i = lax.broadcasted_iota(I32, (KV_CHUNK, Q_BLOCK), 0)
    tok_i = lax.broadcasted_iota(I32, (KV_CHUNK, Q_BLOCK), 1)
    row8 = lax.broadcasted_iota(I32, (H_NSA, Q_BLOCK), 0)
    zeros_half = jnp.zeros((HEAD_DIM, Q_BLOCK), F32)

    for k2 in range(H_NSA // 2):
        slab_t = q_ref[:, k2 * LANES:(k2 + 1) * LANES].T * (HEAD_DIM ** -0.5)
        for h in (2 * k2, 2 * k2 + 1):
            qh = slab_t[(h % 2) * HEAD_DIM:(h % 2 + 1) * HEAD_DIM]
            both = [qh, zeros_half] if h // R_GRP == 0 else [zeros_half, qh]
            qt_ref[:, h * Q_BLOCK:(h + 1) * Q_BLOCK] = jnp.concatenate(both, axis=0).astype(BF16)

    sig_t = _sigmoid(gate_ref[...]).T

    ncp = kc_ref.shape[0]
    kc = kc_ref[...].astype(BF16)
    n_c = lax.broadcasted_iota(I32, (ncp, Q_BLOCK), 0)
    t_c = lax.broadcasted_iota(I32, (ncp, Q_BLOCK), 1)
    ok_c = (n_c * CMP_STRIDE + (L_CMP - 1) <= qs + t_c) & (n_c < n_cmp)
    blk_pos = qs + lax.broadcasted_iota(I32, (pick_ref.shape[1], Q_BLOCK), 1)
    o_c = []
    for g in range(N_KV):
        vct = vct_ref[g * HEAD_DIM:(g + 1) * HEAD_DIM, :].astype(BF16)
        pcs = jnp.zeros((ncp, Q_BLOCK), F32)
        for r in range(R_GRP):
            h = g * R_GRP + r
            s = _dot(kc, qt_ref[:, h * Q_BLOCK:(h + 1) * Q_BLOCK]) + bc_ref[h]
            s = jnp.where(ok_c, s, NEG)
            e = jnp.where(ok_c, jnp.exp(s - jnp.max(s, axis=0, keepdims=True)), 0.0)
            p = e * (1.0 / jnp.maximum(jnp.sum(e, axis=0, keepdims=True), TINY))
            o_c.append(_dot(vct, p.astype(BF16)))
            pcs = pcs + p
        pcs_ref[...] = pcs
        per = SEL_BLOCK // CMP_STRIDE
        imp = pcs_ref[pl.ds(0, ncp // per, stride=per), :]
        for c in range(1, per):
            imp = imp + pcs_ref[pl.ds(c, ncp // per, stride=per), :]
        pick_ref[g] = _select_top(_block_scores(imp, blk_pos, n_sel, axis=0), min(N_SELECT, n_sel), axis=0)

    dmat = tok_i - key_i

    def stream(k_ref, vt_ref, n_chunks, chunk_of, mask_of):
        def scores(c, slot):
            off, _ = chunk_of(c)
            k = k_ref[pl.ds(off, KV_CHUNK), :].astype(BF16)
            for h in range(H_NSA):
                s_ref[slot, h] = _dot(k, qt_ref[:, h * Q_BLOCK:(h + 1) * Q_BLOCK])

        def values(c, slot, alpha_all):
            off, _ = chunk_of(c)
            for g in range(N_KV):
                vt = vt_ref[g * HEAD_DIM:(g + 1) * HEAD_DIM, pl.ds(off, KV_CHUNK)].astype(BF16)
                for h in range(g * R_GRP, (g + 1) * R_GRP):
                    acc_ref[h] = acc_ref[h] * alpha_all[h:h + 1] + _dot(vt, p_ref[slot, h])

        def softmax(c, slot, m_all, l_all):
            _, rel = chunk_of(c)
            ok_g = mask_of(c, rel)
            bias_idx = min(max(rel, 0), FAR_REL) if isinstance(rel, int) else jnp.clip(rel, 0, FAR_REL)
            alpha_all = jnp.zeros((H_NSA, Q_BLOCK), F32)
            for h in range(H_NSA):
                s = s_ref[slot, h] + bt_ref[bias_idx, h]
                m_new, l_new, alpha, p = _attend_t(s, ok_g[h // R_GRP], m_all[h:h + 1], l_all[h:h + 1])
                p_ref[slot, h] = p.astype(BF16)
                m_all = jnp.where(row8 == h, m_new, m_all)
                l_all = jnp.where(row8 == h, l_new, l_all)
                alpha_all = jnp.where(row8 == h, alpha, alpha_all)
            return m_all, l_all, alpha_all

        def stage(c, slot, carry, prefetch=True):
            m_all, l_all, alpha_prev = carry
            if prefetch:
                scores(c + 1, 1 - slot)
            values(c - 1, 1 - slot, alpha_prev)
            return softmax(c, slot, m_all, l_all)

        acc_ref[...] = jnp.zeros_like(acc_ref)
        p_ref[1] = jnp.zeros(p_ref.shape[1:], BF16)
        scores(0, 0)
        carry = (jnp.full((H_NSA, Q_BLOCK), NEG, F32), jnp.zeros((H_NSA, Q_BLOCK), F32),
                 jnp.ones((H_NSA, Q_BLOCK), F32))
        if isinstance(n_chunks, int):
            for c in range(n_chunks):
                carry = stage(c, c % 2, carry, prefetch=c + 1 < n_chunks)
            last = n_chunks - 1
        else:
            pairs = (n_chunks + 1) // 2
            carry = lax.fori_loop(0, pairs, lambda t, cr: stage(2 * t + 1, 1, stage(2 * t, 0, cr)), carry)
            last = 2 * pairs - 1
        m_all, l_all, alpha_last = carry
        values(last, last % 2 if isinstance(last, int) else 1, alpha_last)
        return m_all, l_all

    def sel_chunk(c):
        return pl.multiple_of(jnp.clip(c, 0, i) * KV_CHUNK, KV_CHUNK), i - c

    def sel_mask(c, rel):
        causal = rel * KV_CHUNK + dmat >= 0
        per = KV_CHUNK // SEL_BLOCK
        first = jnp.minimum(c, i) * per
        ok_g = []
        for g in range(N_KV):
            picked = jnp.concatenate(
                [jnp.broadcast_to(pick_ref[g, pl.ds(first + b, 1), :], (SEL_BLOCK, Q_BLOCK)) for b in range(per)],
                axis=0)
            ok_g.append(causal & (picked > 0.5))
        return ok_g

    m_s, l_s = stream(ks_ref, vst_ref, i + 1, sel_chunk, sel_mask)
    for h in range(H_NSA):
        os_ref[h] = jnp.where(m_s[h:h + 1] > 0.5 * NEG, acc_ref[h] / jnp.maximum(l_s[h:h + 1], TINY), 0.0)

    n_back = WINDOW // KV_CHUNK

    def win_chunk(c):
        return pl.multiple_of(jnp.clip(i - n_back + c, 0, i) * KV_CHUNK, KV_CHUNK), n_back - c

    def win_mask(c, rel):
        dist = rel * KV_CHUNK + dmat + jnp.where(i - n_back + c < 0, WINDOW, 0)
        ok = (dist >= 0) & (dist < WINDOW)
        return [ok, ok]

    m_w, l_w = stream(kw_ref, vwt_ref, n_back + 1, win_chunk, win_mask)

    for k2 in range(H_NSA // 2):
        pair = []
        for h in (2 * k2, 2 * k2 + 1):
            o_w = jnp.where(m_w[h:h + 1] > 0.5 * NEG, acc_ref[h] / jnp.maximum(l_w[h:h + 1], TINY), 0.0)
            pair.append(sig_t[3 * h:3 * h + 1] * o_c[h] + sig_t[3 * h + 1:3 * h + 2] * os_ref[h]
                        + sig_t[3 * h + 2:3 * h + 3] * o_w)
        o_ref[:, k2 * LANES:(k2 + 1) * LANES] = jnp.concatenate(pair, axis=0).T.astype(o_ref.dtype)


def _selection_consts(n_rows_cmp, n_keys):
    n = np.arange(n_rows_cmp)
    blk = np.arange(SEL_LANES)
    pool = (n[:, None] // (SEL_BLOCK // CMP_STRIDE) == blk[None, :]).astype(np.float32)
    key = np.arange(n_keys)
    expand = (blk[:, None] == key[None, :] // SEL_BLOCK).astype(np.float32)
    return jnp.asarray(pool, BF16), jnp.asarray(expand, BF16)


def nsa_prompt(z3, kvt, kc, vct, bias_ct, bias_tt):
    b, t, _ = z3.shape
    n_sub = t // CMP_STRIDE
    n_sel = -(-t // SEL_BLOCK)
    assert kc.shape[1] == n_sub and n_sub == n_sel * (SEL_BLOCK // CMP_STRIDE) and n_sel % SUBLANES == 0
    fixed = lambda shape: pl.BlockSpec(shape, lambda bi, i: (0,) * len(shape))
    v_rows = lambda branch: (branch * KV_W + N_KV * HEAD_DIM) // LANES
    return pl.pallas_call(
        functools.partial(_nsa_prompt_kernel, n_cmp=n_sub - 1, n_sel=n_sel),
        grid=(b, t // Q_BLOCK),
        in_specs=[pl.BlockSpec((None, Q_BLOCK, NSA_QW), lambda bi, i: (bi, i, COL_NQ // NSA_QW)),
                  pl.BlockSpec((None, Q_BLOCK, LANES), lambda bi, i: (bi, i, COL_NG // LANES)),
                  pl.BlockSpec((None, n_sub, LANES), lambda bi, i: (bi, 0, 0)),
                  pl.BlockSpec((None, LANES, n_sub), lambda bi, i: (bi, 0, 0)),
                  pl.BlockSpec((None, t, LANES), lambda bi, i: (bi, 0, COL_KVS // LANES)),
                  pl.BlockSpec((None, LANES, t), lambda bi, i: (bi, v_rows(1), 0)),
                  pl.BlockSpec((None, t, LANES), lambda bi, i: (bi, 0, COL_KVW // LANES)),
                  pl.BlockSpec((None, LANES, t), lambda bi, i: (bi, v_rows(2), 0)),
                  pl.BlockSpec((H_NSA, n_sub, Q_BLOCK), lambda bi, i: (0, 0, i)),
                  fixed(bias_tt.shape)],
        out_specs=pl.BlockSpec((None, Q_BLOCK, NSA_QW), lambda bi, i: (bi, i, 0)),
        out_shape=jax.ShapeDtypeStruct((b, t, NSA_QW), BF16),
        scratch_shapes=[pltpu.VMEM((LANES, H_NSA * Q_BLOCK), BF16),
                        pltpu.VMEM((n_sub, Q_BLOCK), F32),
                        pltpu.VMEM((N_KV, n_sel, Q_BLOCK), F32),
                        pltpu.VMEM((H_NSA, HEAD_DIM, Q_BLOCK), F32),
                        pltpu.VMEM((H_NSA, HEAD_DIM, Q_BLOCK), F32),
                        pltpu.VMEM((2, H_NSA, KV_CHUNK, Q_BLOCK), F32),
                        pltpu.VMEM((2, H_NSA, KV_CHUNK, Q_BLOCK), BF16)],
        compiler_params=_cparams(2),
        name="nsa_prompt",
    )(z3, z3, kc, vct, z3, kvt, z3, kvt, bias_ct, bias_tt)


def _nsa_step_kernel(pt_ref, z_ref, kc_ref, vc_ref, *rest, n_pages, per_step, **static):
    del pt_ref
    pages = rest[:per_step * n_pages]
    win_ref, bcs_ref, bss_ref, bws_ref, pool_ref, exp_ref, o_ref, wo_ref = rest[per_step * n_pages:]
    for u in range(per_step):
        _nsa_step_one(z_ref.at[u], kc_ref.at[u], vc_ref.at[u], pages[u * n_pages:(u + 1) * n_pages], win_ref.at[u],
                      bcs_ref, bss_ref, bws_ref, pool_ref, exp_ref, o_ref.at[u], wo_ref.at[u], **static)


def _nsa_step_one(z_ref, kc_ref, vc_ref, pages, win_ref, bcs_ref, bss_ref, bws_ref, pool_ref, exp_ref, o_ref, wo_ref,
                  *, past, wbuf, n_cmp, n_sel):
    z = z_ref[...]
    row8 = lax.broadcasted_iota(I32, (H_NSA, LANES), 0)
    lane8 = lax.broadcasted_iota(I32, (H_NSA, LANES), 1)
    left8 = lane8 < HEAD_DIM
    qbd = jnp.zeros((H_NSA, LANES), F32)
    for h in range(H_NSA):
        slab = jnp.broadcast_to(z[:, COL_NQ + (h // 2) * LANES:COL_NQ + (h // 2 + 1) * LANES], (H_NSA, LANES))
        g = h // R_GRP
        if h % 2 != g:
            slab = pltpu.roll(slab, HEAD_DIM, 1)
        keep = (row8 == h) & (left8 if g == 0 else jnp.logical_not(left8))
        qbd = jnp.where(keep, slab, qbd)
    qbd = (qbd * (HEAD_DIM ** -0.5)).astype(BF16)

    ncp = kc_ref.shape[0]
    n_c = lax.broadcasted_iota(I32, (H_NSA, ncp), 1)
    s_c = _dot_nt(qbd, kc_ref[...].astype(BF16)) + bcs_ref[...]
    p_c = _masked_softmax(s_c, (n_c * CMP_STRIDE + (L_CMP - 1) <= past) & (n_c < n_cmp))
    o_c = _dot(p_c.astype(BF16), vc_ref[...].astype(BF16))
    pc0 = jnp.sum(p_c[0:R_GRP], axis=0, keepdims=True)
    pc1 = jnp.sum(p_c[R_GRP:], axis=0, keepdims=True)
    rowc = lax.broadcasted_iota(I32, (H_NSA, ncp), 0)
    pcs = jnp.where(rowc < R_GRP, jnp.broadcast_to(pc0, (H_NSA, ncp)), jnp.broadcast_to(pc1, (H_NSA, ncp)))
    imp = _dot3(pcs, pool_ref[...])
    sel = _select_top(_block_scores(imp, jnp.full(imp.shape, past, I32), n_sel), min(N_SELECT, n_sel))
    picked = _dot(sel.astype(BF16), exp_ref[...])

    c128 = lax.broadcasted_iota(I32, (LANES, KV_CHUNK), 1)

    def new_col(col):
        return jnp.broadcast_to(z[:, col:col + LANES], (KV_CHUNK, LANES)).T

    def first_col(tile):
        return jnp.where(c128 == 0, tile, 0.0).astype(BF16)

    ks = [pg[0:LANES, :].astype(BF16) for pg in pages] + [first_col(new_col(COL_KVS))]
    vs = [pg[LANES:2 * LANES, :].astype(BF16) for pg in pages] + [first_col(new_col(COL_KVS + LANES))]
    s_s = jnp.concatenate([_dot(qbd, k) for k in ks], axis=1) + bss_ref[...]
    key = lax.broadcasted_iota(I32, s_s.shape, 1)
    p_s = _masked_softmax(s_s, (picked > 0.5) & (key <= past)).astype(BF16)
    o_s = _dot_nt(p_s[:, 0:KV_CHUNK], vs[0])
    for c in range(1, len(vs)):
        o_s = o_s + _dot_nt(p_s[:, c * KV_CHUNK:(c + 1) * KV_CHUNK], vs[c])

    kw_new, vw_new = new_col(COL_KVW), new_col(COL_KVW + LANES)
    win = win_ref[...]
    s_w = jnp.concatenate([_dot(qbd, win[0:LANES].astype(BF16)), _dot(qbd, first_col(kw_new))], axis=1) + bws_ref[...]
    colw = lax.broadcasted_iota(I32, s_w.shape, 1)
    p_w = _masked_softmax(s_w, (colw <= wbuf) & (wbuf - colw < WINDOW)).astype(BF16)
    o_w = _dot_nt(p_w[:, 0:wbuf], win[LANES:].astype(BF16)) + _dot_nt(p_w[:, wbuf:], first_col(vw_new))

    shifted = pltpu.roll(win, wbuf - 1, 1)
    new_kv = jnp.concatenate([kw_new, vw_new], axis=0)
    last = jnp.where(lax.broadcasted_iota(I32, (KV_W, LANES), 1) == LANES - 1, new_kv, shifted[:, wbuf - LANES:])
    wo_ref[:, 0:wbuf - LANES] = shifted[:, 0:wbuf - LANES]
    wo_ref[:, wbuf - LANES:] = last

    sig = jnp.broadcast_to(_sigmoid(z[:, COL_NG:COL_NG + LANES]), (H_NSA, LANES))

    def gate_col(c):
        return jnp.sum(jnp.where(lane8 == row8 * 3 + c, sig, 0.0), axis=1, keepdims=True)

    o = gate_col(0) * o_c + gate_col(1) * o_s + gate_col(2) * o_w
    fix = jnp.where((row8 % 2) != (row8 // R_GRP), pltpu.roll(o, HEAD_DIM, 1), o)
    left1 = left8[0:1]
    o_ref[...] = jnp.concatenate(
        [jnp.where(left1, fix[2 * k2:2 * k2 + 1], fix[2 * k2 + 1:2 * k2 + 2]) for k2 in range(H_NSA // 2)], axis=1)


def nsa_step(zs3, kc, vc, sel_cache, win_cache, page_idx, win_row0, bias_cs, bias_ss, bias_ws, past):
    bs = zs3.shape[0]
    n_pages = page_idx.shape[1]
    wbuf = win_cache.shape[2]
    n_sub = (past + 1) // CMP_STRIDE
    n_sel = -(-(past + 1) // SEL_BLOCK)
    n_keys = (n_pages + 1) * PAGE_SIZE
    assert n_sel <= SEL_LANES and PAGE_SIZE == KV_CHUNK and wbuf % KV_CHUNK == 0 and kc.shape[1] == n_sub
    pool, expand = _selection_consts(n_sub, n_keys)
    per = 2 if bs % 2 == 0 and win_row0 % 2 == 0 else 1
    fixed = lambda shape: pl.BlockSpec(shape, lambda b, pt: (0,) * len(shape))
    page = lambda u, p: pl.BlockSpec((None, KV_W, PAGE_SIZE), lambda b, pt, u=u, p=p: (pt[per * b + u, p], 0, 0))
    grid_spec = pltpu.PrefetchScalarGridSpec(
        num_scalar_prefetch=1,
        grid=(bs // per,),
        in_specs=[pl.BlockSpec((per, 1, N_IN_PAD), lambda b, pt: (b, 0, 0)),
                  pl.BlockSpec((per, n_sub, LANES), lambda b, pt: (b, 0, 0)),
                  pl.BlockSpec((per, n_sub, LANES), lambda b, pt: (b, 0, 0))]
        + [page(u, p) for u in range(per) for p in range(n_pages)]
        + [pl.BlockSpec((per, KV_W, wbuf), lambda b, pt: (win_row0 // per + b, 0, 0)),
           fixed(bias_cs.shape), fixed(bias_ss.shape), fixed(bias_ws.shape), fixed(pool.shape), fixed(expand.shape)],
        out_specs=[pl.BlockSpec((per, 1, NSA_QW), lambda b, pt: (b, 0, 0)),
                   pl.BlockSpec((per, KV_W, wbuf), lambda b, pt: (b, 0, 0))],
    )
    return pl.pallas_call(
        functools.partial(_nsa_step_kernel, n_pages=n_pages, per_step=per, past=past, wbuf=wbuf, n_cmp=n_sub - 1,
                          n_sel=n_sel),
        grid_spec=grid_spec,
        out_shape=[jax.ShapeDtypeStruct((bs, 1, NSA_QW), F32), jax.ShapeDtypeStruct((bs, KV_W, wbuf), F32)],
        compiler_params=_cparams(1),
        name="nsa_step",
    )(page_idx, zs3, kc, vc, *([sel_cache] * (per * n_pages)), win_cache, bias_cs, bias_ss, bias_ws, pool, expand)


def _route(logits, tm):
    m = logits.shape[0]
    top_v, top_i = lax.top_k(logits, TOP_K)
    top_w = jax.nn.softmax(top_v, axis=-1)
    slot_e = top_i.reshape(-1)
    onehot = (slot_e[:, None] == jnp.arange(N_EXPERTS)[None, :]).astype(I32)
    before = jnp.cumsum(onehot, axis=0) - onehot
    rank = jnp.sum(before * onehot, axis=1)
    count = jnp.sum(onehot, axis=0)
    padded = -(-count // tm) * tm
    start = jnp.cumsum(padded) - padded
    pos = start[slot_e] + rank
    n_rows = (-(-(m * TOP_K) // tm) + N_EXPERTS) * tm
    row_slot = jnp.full((n_rows,), -1, I32).at[pos].set(jnp.arange(m * TOP_K, dtype=I32))
    slot = jnp.maximum(row_slot, 0)
    row_token = slot // TOP_K
    row_gate = jnp.where(row_slot >= 0, top_w.reshape(-1)[slot], 0.0)
    tile_start = jnp.arange(n_rows // tm, dtype=I32) * tm
    tile_expert = jnp.clip(jnp.searchsorted(jnp.cumsum(padded), tile_start, side='right'), 0, N_EXPERTS - 1)
    tile_valid = (tile_start < jnp.sum(padded)).astype(I32)
    return pos.reshape(m, TOP_K), row_token, row_gate.reshape(n_rows, 1), tile_expert.astype(I32), tile_valid


def _dense_ffn(h, g, w1, w3, w2):
    m = h.shape[0]
    tm = _row_tile(m)
    ones = jnp.ones((m // tm,), I32)
    return grouped_ffn(h, g, jnp.ones((m, 1), F32), jnp.zeros((m // tm,), I32), ones,
                       w1[None], w3[None], w2[None], tm=tm, tf=w1.shape[1] // 2, residual=True)


def kernel(x_prompt, x_sample, p_prompt, p_sample, state_ret, cache_win_kv, cache_cmp_kv, cache_sel_kv, page_table, rel_bias, norm_mix, w_in, ret_gain, cmp_pe, cmp_w1, cmp_w2, w_out, norm_ffn, ffn_w1, ffn_w3, ffn_w2, router, moe_w1, moe_w3, moe_w2, ple_norm, ple_gate, ple_proj, norm_final):
    b_p, t_p, d = x_prompt.shape
    b_s = x_sample.shape[0]
    depth = w_in.shape[0]
    n_pages = page_table.shape[1]
    past = n_pages * PAGE_SIZE
    n_pool = cache_cmp_kv.shape[1]
    w_buf = cache_win_kv.shape[2]
    assert x_sample.shape[1] == 1 and t_p % Q_BLOCK == 0 and t_p >= w_buf and past % CMP_STRIDE == 0
    row = (2, N_KV, HEAD_DIM)
    m_p = b_p * t_p

    ar = lambda n: jnp.arange(n, dtype=I32)
    n_sub_p = t_p // CMP_STRIDE
    bias_ct = bias_lookup(rel_bias, ar(t_p)[None, :] - (ar(n_sub_p)[:, None] * CMP_STRIDE + L_CMP - 1))
    rel_t = (ar(FAR_REL + 1)[:, None, None] * KV_CHUNK + ar(Q_BLOCK)[None, None, :] - ar(KV_CHUNK)[None, :, None])
    bias_tt = bias_lookup(rel_bias, rel_t.reshape((FAR_REL + 1) * KV_CHUNK, Q_BLOCK))
    bias_tt = bias_tt.reshape(H_NSA, FAR_REL + 1, KV_CHUNK, Q_BLOCK).transpose(1, 0, 2, 3)
    n_sub_s = (past + 1) // CMP_STRIDE
    rows8 = jnp.zeros((SUBLANES, 1), I32)
    bias_cs = bias_lookup(rel_bias, rows8 + (past - (ar(n_sub_s)[None, :] * CMP_STRIDE + L_CMP - 1)))[:, 0]
    bias_ss = bias_lookup(rel_bias, rows8 + (past - ar((n_pages + 1) * PAGE_SIZE)[None, :]))[:, 0]
    bias_ws = bias_lookup(rel_bias, rows8 + (w_buf - ar(w_buf + KV_CHUNK)[None, :]))[:, 0]

    def rows_last(x):
        return jnp.moveaxis(x, 2, -1).reshape(x.shape[0] * x.shape[1], KV_W, x.shape[2])

    def rows_first(xt, lead):
        return jnp.moveaxis(xt.reshape((lead,) + row + (xt.shape[-1],)), -1, 1)

    cmp_cache = rows_last(cache_cmp_kv)
    sel_cache = rows_last(cache_sel_kv)
    win_cache = rows_last(cache_win_kv)
    state_t = jnp.moveaxis(state_ret, 1, -1).reshape(depth * H_RET, HEAD_DIM, HEAD_DIM, b_s)

    h_p = x_prompt.reshape(m_p, d)
    h_s = x_sample.reshape(b_s, d)
    rp, rs, wp, ws, cp, cs, sp, ss = [], [], [], [], [], [], [], []
    for i in range(depth):
        w_in_b = jnp.pad(w_in[i], ((0, 0), (0, N_IN_PAD - N_IN))).astype(BF16)
        w_out_b = w_out[i].astype(BF16)
        pe_t, cw1, cw2 = _compress_weights(cmp_pe[i], cmp_w1[i], cmp_w2[i])

        w_kvt = w_in[i][:, COL_KVC:COL_KVC + 3 * KV_W].T.astype(BF16)
        z, kvt = rms_matmul_kvt(h_p, norm_mix[i], w_in_b, w_kvt, b_p)
        z3 = z.reshape(b_p, t_p, N_IN_PAD)
        ret_y, ret_state = retention_prompt(z3, ret_gain[i])
        kc, vc = compress(kvt, ar(b_p)[:, None], pe_t, cw1, cw2)
        nsa_y = nsa_prompt(z3, kvt, kc, jnp.swapaxes(vc, 1, 2), bias_ct, bias_tt)
        h_p = out_proj(h_p, ret_y.reshape(m_p, RET_W), nsa_y.reshape(m_p, NSA_QW), w_out_b)
        rp.append(ret_state)
        cp.append(rows_first(kvt[:, 0:KV_W], b_p))
        sp.append(rows_first(kvt[:, KV_W:2 * KV_W], b_p))
        wp.append(rows_first(kvt[:, 2 * KV_W:, t_p - w_buf:], b_p))

        zs = rms_matmul(h_s, norm_mix[i], w_in_b)
        zst = zs.T
        ret_yst, state_s = retention_step(zst, state_t, i, past, ret_gain[i])
        page_idx = page_table + i * n_pool
        grp = 4 if b_s % 4 == 0 else 1
        kc_s, vc_s = compress(cmp_cache, page_idx.reshape(b_s // grp, grp * n_pages), pe_t, cw1, cw2)
        kc_s, vc_s = kc_s.reshape(b_s, n_sub_s, LANES), vc_s.reshape(b_s, n_sub_s, LANES)
        nsa_ys, win_s = nsa_step(zs.reshape(b_s, 1, N_IN_PAD), kc_s, vc_s, sel_cache, win_cache, page_idx, i * b_s,
                                 bias_cs, bias_ss, bias_ws, past)
        h_s = out_proj(h_s, ret_yst.T, nsa_ys.reshape(b_s, NSA_QW), w_out_b)
        rs.append(jnp.moveaxis(state_s, -1, 0))
        ws.append(rows_first(win_s, b_s))
        cs.append(rows_first(zst[None, COL_KVC:COL_KVC + KV_W], 1).reshape((b_s, 1) + row))
        ss.append(rows_first(zst[None, COL_KVS:COL_KVS + KV_W], 1).reshape((b_s, 1) + row))

        j = i // 2
        if i % 2 == 0:
            w1, w3, w2 = ffn_w1[j].astype(BF16), ffn_w3[j].astype(BF16), ffn_w2[j].astype(BF16)
            h_p = _dense_ffn(h_p, norm_ffn[i], w1, w3, w2)
            h_s = _dense_ffn(h_s, norm_ffn[i], w1, w3, w2)
        else:
            router_w = jnp.pad(router[j], ((0, 0), (0, LANES - N_EXPERTS)))
            tf = moe_w1.shape[3] // 7
            logits_p = router_logits(h_p, norm_ffn[i], router_w)
            tm = 1024
            pos, row_token, row_gate, tile_expert, tile_valid = _route(logits_p[:, :N_EXPERTS], tm)
            x_rows = gather_rows(h_p.reshape(m_p, SUBLANES, d // SUBLANES), row_token)
            y_rows = grouped_ffn(x_rows, norm_ffn[i], row_gate, tile_expert, tile_valid,
                                 moe_w1[j], moe_w3[j], moe_w2[j], tm=tm, tf=tf, residual=False)
            h_p = h_p + y_rows[pos[:, 0]] + y_rows[pos[:, 1]]
            logits_s = router_logits(h_s, norm_ffn[i], router_w)
            top_v, top_i = lax.top_k(logits_s[:, :N_EXPERTS], TOP_K)
            top_w = jax.nn.softmax(top_v, axis=-1)
            gates = jnp.sum(jax.nn.one_hot(top_i, N_EXPERTS, dtype=F32) * top_w[..., None], axis=-2)
            y_e = grouped_ffn(jnp.tile(h_s, (N_EXPERTS, 1)), norm_ffn[i], gates.T.reshape(N_EXPERTS * b_s, 1),
                              jnp.arange(N_EXPERTS, dtype=I32), jnp.ones((N_EXPERTS,), I32),
                              moe_w1[j], moe_w3[j], moe_w2[j], tm=b_s, tf=tf, residual=False)
            h_s = h_s + jnp.sum(y_e.reshape(N_EXPERTS, b_s, d), axis=0)

        wg, wpj = ple_gate[i].astype(BF16), ple_proj[i].astype(BF16)
        h_p = ple(h_p, ple_norm[i], wg, p_prompt[i].reshape(m_p, P_DIM), wpj, norm_final, i == depth - 1)
        h_s = ple(h_s, ple_norm[i], wg, p_sample[i].reshape(b_s, P_DIM), wpj, norm_final, i == depth - 1)

    return (h_p.reshape(b_p, t_p, d), h_s.reshape(b_s, 1, d), jnp.stack(rp), jnp.stack(rs), jnp.stack(wp),
            jnp.stack(ws), jnp.stack(cp), jnp.stack(cs), jnp.stack(sp), jnp.stack(ss))
```

```python
import functools
import math

import numpy as np
import jax
import jax.numpy as jnp
from jax import lax
from jax.experimental import pallas as pl
from jax.experimental.pallas import tpu as pltpu

F32 = jnp.float32
BF16 = jnp.bfloat16
I32 = jnp.int32

LANES = 128
SUBLANES = 8
VMEM_LIMIT_BYTES = 56 * 1024 * 1024

D_MODEL = 1024
HEAD_DIM = 64
H_RET = 8
H_NSA = 8
N_KV = 2
R_GRP = H_NSA // N_KV
RET_W = H_RET * HEAD_DIM
NSA_QW = H_NSA * HEAD_DIM
KV_W = 2 * N_KV * HEAD_DIM
N_IN = 4 * RET_W + NSA_QW + 3 * KV_W + 3 * H_NSA
N_IN_PAD = 3584
COL_RQ, COL_RK, COL_RV, COL_RG = 0, RET_W, 2 * RET_W, 3 * RET_W
COL_NQ = 4 * RET_W
COL_KVC = COL_NQ + NSA_QW
COL_KVS = COL_KVC + KV_W
COL_KVW = COL_KVS + KV_W
COL_NG = COL_KVW + KV_W
RET_CHUNK = 128
CMP_STRIDE = 16
L_CMP = 2 * CMP_STRIDE
CMP_HIDDEN = 2 * HEAD_DIM
SEL_BLOCK = 64
N_SELECT = 16
WINDOW = 512
Q_BLOCK = 128
N_BUCKETS = 32
MAX_DISTANCE = 128
N_EXPERTS = 8
TOP_K = 2
P_DIM = 256
PAGE_SIZE = 128
ROPE_BASE = 10000.0
EPS = 1e-6
FORCED_SCORE = 1e4
NEG = -1e30
TINY = float(np.finfo(np.float32).tiny)
SEL_LANES = 64
FAR_REL = 2
KV_CHUNK = 128


def _cparams(n_grid):
    return pltpu.CompilerParams(dimension_semantics=("arbitrary",) * n_grid,
                                vmem_limit_bytes=VMEM_LIMIT_BYTES)


def _dot(a, b):
    return jnp.dot(a, b, preferred_element_type=F32)


def _dot_nt(a, b):
    return lax.dot_general(a, b, (((1,), (1,)), ((), ())), preferred_element_type=F32)


def _split3(x):
    hi = x.astype(BF16)
    r = x - hi.astype(F32)
    mid = r.astype(BF16)
    lo = (r - mid.astype(F32)).astype(BF16)
    return hi, mid, lo


def _dot3(x, b01):
    hi, mid, lo = _split3(x)
    return _dot(hi, b01) + _dot(mid, b01) + _dot(lo, b01)


def _dot6(x, w):
    xh, xm, xl = _split3(x)
    wh, wm, wl = _split3(w)
    return (_dot(xh, wh) + _dot(xh, wm) + _dot(xm, wh)
            + _dot(xm, wm) + _dot(xh, wl) + _dot(xl, wh))


def _sigmoid(x):
    return 1.0 / (1.0 + jnp.exp(-x))


def _rms(x, g):
    return x * lax.rsqrt(jnp.mean(x * x, axis=-1, keepdims=True) + EPS) * g


def _row_tile(m):
    for t in (512, 256, 128, 64, 32, 16, 8):
        if m % t == 0:
            return t
    raise ValueError(f"row count {m} not a multiple of 8")


def _rms_matmul_kernel(x_ref, g_ref, w_ref, o_ref):
    o_ref[...] = _dot(_rms(x_ref[...], g_ref[...]).astype(BF16), w_ref[...])


def rms_matmul(x, g, w):
    m, d = x.shape
    n = w.shape[1]
    tm = min(_row_tile(m), 256)
    return pl.pallas_call(
        _rms_matmul_kernel,
        grid=(m // tm,),
        in_specs=[pl.BlockSpec((tm, d), lambda i: (i, 0)),
                  pl.BlockSpec((1, d), lambda i: (0, 0)),
                  pl.BlockSpec((d, n), lambda i: (0, 0))],
        out_specs=pl.BlockSpec((tm, n), lambda i: (i, 0)),
        out_shape=jax.ShapeDtypeStruct((m, n), F32),
        compiler_params=_cparams(1),
    )(x, g.reshape(1, d), w)


def _rms_matmul_kvt_kernel(x_ref, g_ref, w_ref, wt_ref, o_ref, ot_ref):
    xn = _rms(x_ref[...], g_ref[...]).astype(BF16)
    o_ref[...] = _dot(xn, w_ref[...])
    ot_ref[...] = _dot_nt(wt_ref[...], xn)


def rms_matmul_kvt(x, g, w, wt, bsz):
    m, d = x.shape
    n = w.shape[1]
    nt = wt.shape[0]
    t = m // bsz
    tm = min(_row_tile(t), 256)
    per = t // tm
    return pl.pallas_call(
        _rms_matmul_kvt_kernel,
        grid=(m // tm,),
        in_specs=[pl.BlockSpec((tm, d), lambda i: (i, 0)),
                  pl.BlockSpec((1, d), lambda i: (0, 0)),
                  pl.BlockSpec((d, n), lambda i: (0, 0)),
                  pl.BlockSpec((nt, d), lambda i: (0, 0))],
        out_specs=[pl.BlockSpec((tm, n), lambda i: (i, 0)),
                   pl.BlockSpec((None, nt, tm), lambda i: (i // per, 0, i % per))],
        out_shape=[jax.ShapeDtypeStruct((m, n), F32), jax.ShapeDtypeStruct((bsz, nt, t), F32)],
        compiler_params=_cparams(1),
        name="rms_matmul_kvt",
    )(x, g.reshape(1, d), w, wt)


def _router_kernel(x_ref, g_ref, w_ref, o_ref):
    o_ref[...] = _dot6(_rms(x_ref[...], g_ref[...]), w_ref[...])


def router_logits(x, g, w):
    m, d = x.shape
    n = w.shape[1]
    tm = min(_row_tile(m), 256)
    return pl.pallas_call(
        _router_kernel,
        grid=(m // tm,),
        in_specs=[pl.BlockSpec((tm, d), lambda i: (i, 0)),
                  pl.BlockSpec((1, d), lambda i: (0, 0)),
                  pl.BlockSpec((d, n), lambda i: (0, 0))],
        out_specs=pl.BlockSpec((tm, n), lambda i: (i, 0)),
        out_shape=jax.ShapeDtypeStruct((m, n), F32),
        compiler_params=_cparams(1),
        name="router",
    )(x, g.reshape(1, d), w)


def _outproj_kernel(h_ref, a_ref, b_ref, w_ref, o_ref):
    ka = a_ref.shape[1]
    o_ref[...] = (h_ref[...] + _dot(a_ref[...].astype(BF16), w_ref[0:ka, :])
                  + _dot(b_ref[...].astype(BF16), w_ref[ka:, :]))


def out_proj(h, a, b, w):
    m, d = h.shape
    ka, kb = a.shape[1], b.shape[1]
    tm = _row_tile(m)
    return pl.pallas_call(
        _outproj_kernel,
        grid=(m // tm,),
        in_specs=[pl.BlockSpec((tm, d), lambda i: (i, 0)),
                  pl.BlockSpec((tm, ka), lambda i: (i, 0)),
                  pl.BlockSpec((tm, kb), lambda i: (i, 0)),
                  pl.BlockSpec((ka + kb, d), lambda i: (0, 0))],
        out_specs=pl.BlockSpec((tm, d), lambda i: (i, 0)),
        out_shape=jax.ShapeDtypeStruct((m, d), F32),
        compiler_params=_cparams(1),
        name="out_proj",
    )(h, a, b, w)


def _ffn_kernel(te_ref, tv_ref, idx_ref, x_ref, g_ref, gate_ref, w1_ref, w3_ref, w2_ref, o_ref,
                xn_ref, acc_ref, *gather_scratch, residual):
    i = pl.program_id(0)
    j = pl.program_id(1)

    if gather_scratch:
        xbuf, sem = gather_scratch
        tm = xbuf.shape[1]

        def issue(tile, slot):
            def start(k, carry):
                pltpu.make_async_copy(x_ref.at[idx_ref[tile * tm + k]], xbuf.at[slot, k], sem.at[slot]).start()
                return carry
            lax.fori_loop(0, tm, start, 0, unroll=8)

        @pl.when((j == 0) & (i == 0))
        def _():
            issue(0, 0)

        @pl.when(j == 0)
        def _():
            slot = i % 2
            pltpu.make_async_copy(x_ref.at[pl.ds(0, tm)], xbuf.at[slot], sem.at[slot]).wait()

            @pl.when(i + 1 < pl.num_programs(0))
            def _():
                issue(i + 1, 1 - slot)

            x = jnp.concatenate([xbuf[slot, :, s, :] for s in range(xbuf.shape[2])], axis=1)
            xn_ref[...] = _rms(x, g_ref[...]).astype(BF16)
            acc_ref[...] = jnp.zeros_like(acc_ref)
    else:
        @pl.when(j == 0)
        def _():
            xn_ref[...] = _rms(x_ref[...], g_ref[...]).astype(BF16)
            acc_ref[...] = jnp.zeros_like(acc_ref)

    @pl.when(tv_ref[i] != 0)
    def _():
        xn = xn_ref[...]
        a = _dot(xn, w1_ref[...].astype(BF16))
        b = _dot(xn, w3_ref[...].astype(BF16))
        hid = (a * _sigmoid(a)) * b
        acc_ref[...] += _dot(hid.astype(BF16), w2_ref[...].astype(BF16))

    @pl.when(j == pl.num_programs(1) - 1)
    def _():
        y = acc_ref[...] * gate_ref[...]
        if residual:
            y = y + x_ref[...]
        o_ref[...] = y


def grouped_ffn(x, g, row_gate, tile_expert, tile_valid, w1, w3, w2, *, tm, tf, residual, row_idx=None):
    gather = row_idx is not None
    m = row_gate.shape[0]
    d = w1.shape[1]
    f = w1.shape[2]
    assert not (residual and gather)
    if gather:
        assert x.shape[1:] == (SUBLANES, d // SUBLANES)
        x_spec = pl.BlockSpec(memory_space=pl.ANY)
        scratch = [pltpu.VMEM((2, tm) + x.shape[1:], F32), pltpu.SemaphoreType.DMA((2,))]
    else:
        row_idx = jnp.zeros((1,), I32)
        x_spec = pl.BlockSpec((tm, d), lambda i, j, *_: (i, 0))
        scratch = []
    grid_spec = pltpu.PrefetchScalarGridSpec(
        num_scalar_prefetch=3,
        grid=(m // tm, f // tf),
        in_specs=[x_spec,
                  pl.BlockSpec((1, d), lambda i, j, *_: (0, 0)),
                  pl.BlockSpec((tm, 1), lambda i, j, *_: (i, 0)),
                  pl.BlockSpec((None, d, tf), lambda i, j, te, *_: (te[i], 0, j)),
                  pl.BlockSpec((None, d, tf), lambda i, j, te, *_: (te[i], 0, j)),
                  pl.BlockSpec((None, tf, d), lambda i, j, te, *_: (te[i], j, 0))],
        out_specs=pl.BlockSpec((tm, d), lambda i, j, *_: (i, 0)),
        scratch_shapes=[pltpu.VMEM((tm, d), BF16), pltpu.VMEM((tm, d), F32)] + scratch,
    )
    return pl.pallas_call(
        functools.partial(_ffn_kernel, residual=residual),
        grid_spec=grid_spec,
        out_shape=jax.ShapeDtypeStruct((m, d), F32),
        compiler_params=_cparams(2),
        name="grouped_ffn",
    )(tile_expert, tile_valid, row_idx, x, g.reshape(1, d), row_gate, w1, w3, w2)


def _gather_rows_kernel(idx_ref, x_hbm, o_ref, sem):
    base = pl.program_id(0) * o_ref.shape[0]

    def row_copy(k):
        return pltpu.make_async_copy(x_hbm.at[idx_ref[base + k]], o_ref.at[k], sem)

    def start(k, carry):
        row_copy(k).start()
        return carry

    lax.fori_loop(0, o_ref.shape[0], start, 0, unroll=8)
    pltpu.make_async_copy(x_hbm.at[pl.ds(0, o_ref.shape[0])], o_ref, sem).wait()


def gather_rows(x, idx, rows_per_step=256):
    n = idx.shape[0]
    assert n % rows_per_step == 0 and x.dtype == F32 and x.shape[1:] == (SUBLANES, LANES)
    grid_spec = pltpu.PrefetchScalarGridSpec(
        num_scalar_prefetch=1,
        grid=(n // rows_per_step,),
        in_specs=[pl.BlockSpec(memory_space=pl.ANY)],
        out_specs=pl.BlockSpec((rows_per_step, SUBLANES, LANES), lambda i, ix: (i, 0, 0)),
        scratch_shapes=[pltpu.SemaphoreType.DMA],
    )
    return pl.pallas_call(
        _gather_rows_kernel,
        grid_spec=grid_spec,
        out_shape=jax.ShapeDtypeStruct((n, SUBLANES, LANES), x.dtype),
        compiler_params=_cparams(1),
        name="gather_rows",
    )(idx, x)


def _ple_kernel(h_ref, g_ref, wg_ref, p_ref, wp_ref, gf_ref, o_ref, *, final):
    h = h_ref[...]
    gate = _sigmoid(_dot(_rms(h, g_ref[...]).astype(BF16), wg_ref[...]))
    proj = _dot(p_ref[...].astype(BF16), wp_ref[...])
    h2 = h + gate * proj
    o_ref[...] = _rms(h2, gf_ref[...]) if final else h2


def ple(h, g, wg, p, wp, gf, final):
    m, d = h.shape
    pd = p.shape[1]
    tm = _row_tile(m)
    row = lambda i: (i, 0)
    fixed = lambda i: (0, 0)
    return pl.pallas_call(
        functools.partial(_ple_kernel, final=final),
        grid=(m // tm,),
        in_specs=[pl.BlockSpec((tm, d), row), pl.BlockSpec((1, d), fixed),
                  pl.BlockSpec((d, d), fixed), pl.BlockSpec((tm, pd), row),
                  pl.BlockSpec((pd, d), fixed), pl.BlockSpec((1, d), fixed)],
        out_specs=pl.BlockSpec((tm, d), row),
        out_shape=jax.ShapeDtypeStruct((m, d), F32),
        compiler_params=_cparams(1),
        name="ple",
    )(h, g.reshape(1, d), wg, p, wp, gf.reshape(1, d))


def _bias_kernel(tab_ref, d_ref, o_ref):
    n = jnp.maximum(d_ref[...], 0)
    max_exact = N_BUCKETS // 2
    nf = jnp.maximum(n, 1).astype(F32)
    large = max_exact + (jnp.log(nf / max_exact) / math.log(MAX_DISTANCE / max_exact)
                         * (N_BUCKETS - max_exact)).astype(I32)
    large = jnp.minimum(large, N_BUCKETS - 1)
    bucket = jnp.where(n < max_exact, n, large)
    for h in range(H_NSA):
        acc = jnp.zeros(n.shape, F32)
        for k in range(N_BUCKETS):
            acc = jnp.where(bucket == k, tab_ref[k, h], acc)
        o_ref[h] = acc


def bias_lookup(table, dist):
    r, c = dist.shape
    tr = min(r, 128)
    tc = 512 if c % 512 == 0 else c
    return pl.pallas_call(
        _bias_kernel,
        grid=(r // tr, c // tc),
        in_specs=[pl.BlockSpec(memory_space=pltpu.SMEM),
                  pl.BlockSpec((tr, tc), lambda i, j: (i, j))],
        out_specs=pl.BlockSpec((H_NSA, tr, tc), lambda i, j: (0, i, j)),
        out_shape=jax.ShapeDtypeStruct((H_NSA, r, c), F32),
        compiler_params=_cparams(2),
        name="bias_lookup",
    )(table, dist)


def _swap_halves(x):
    lane = lax.broadcasted_iota(I32, x.shape, 1)
    return jnp.where(lane % HEAD_DIM < HEAD_DIM // 2,
                     pltpu.roll(x, LANES - HEAD_DIM // 2, 1), pltpu.roll(x, HEAD_DIM // 2, 1))


def _retention_prompt_kernel(q_ref, k_ref, v_ref, g_ref, cos_ref, sin_ref, gain_ref, intra_ref,
                             qdec_ref, kdec_ref, cdec_ref, y_ref, s_out_ref, s_ref):
    c = pl.program_id(1)

    @pl.when(c == 0)
    def _():
        s_ref[...] = jnp.zeros_like(s_ref)

    cos = cos_ref[...]
    sin = sin_ref[...]
    lane = lax.broadcasted_iota(I32, (RET_CHUNK, LANES), 1)
    left = lane < HEAD_DIM
    row = lax.broadcasted_iota(I32, (LANES, LANES), 0)
    col = lax.broadcasted_iota(I32, (LANES, LANES), 1)
    same_head = (row < HEAD_DIM) == (col < HEAD_DIM)
    ones_bd = jnp.where(same_head, 1.0, 0.0).astype(BF16)
    for p in range(H_RET // 2):
        sl = slice(p * LANES, (p + 1) * LANES)
        q = q_ref[:, sl]
        k = k_ref[:, sl]
        v = v_ref[:, sl].astype(BF16)
        qr = q * cos + _swap_halves(q) * sin
        kr = (k * cos + _swap_halves(k) * sin) * (HEAD_DIM ** -0.5)
        qb = qr.astype(BF16)
        q2 = jnp.concatenate([jnp.where(left, qr, 0.0), jnp.where(left, 0.0, qr)], axis=0).astype(BF16)
        inner = _dot_nt(q2, kr.astype(BF16)) * intra_ref[p]
        o2 = _dot(inner.astype(BF16), v)
        s_old = s_ref[p]
        cross = _dot(qb, s_old.astype(BF16)) * qdec_ref[:, sl]
        o = jnp.where(left, o2[:RET_CHUNK], o2[RET_CHUNK:]) + cross
        kd = (kr * kdec_ref[:, sl]).T.astype(BF16)
        s_ref[p] = s_old * cdec_ref[p] + jnp.where(same_head, _dot(kd, v), 0.0)
        ms = _dot3(o * o, ones_bd) * (1.0 / HEAD_DIM)
        gate = g_ref[:, sl]
        y = o * lax.rsqrt(ms + EPS) * gain_ref[:, sl] * (gate * _sigmoid(gate))
        y_ref[:, sl] = y.astype(y_ref.dtype)

    @pl.when(c == pl.num_programs(1) - 1)
    def _():
        for p in range(H_RET // 2):
            s = s_ref[p]
            s_out_ref[2 * p] = s[:HEAD_DIM, :HEAD_DIM]
            s_out_ref[2 * p + 1] = s[HEAD_DIM:, HEAD_DIM:]


def _retention_consts(c):
    log_g = jnp.log1p(-jnp.exp2(-5.0 - jnp.arange(H_RET, dtype=F32)))
    j = jnp.arange(c, dtype=F32)
    diff = j[:, None] - j[None, :]
    intra = jnp.where(diff >= 0, jnp.exp(jnp.maximum(diff, 0.0)[None] * log_g[:, None, None]), 0.0)
    q_decay = jnp.exp((j[:, None] + 1.0) * log_g[None, :])
    k_decay = jnp.exp((c - 1.0 - j[:, None]) * log_g[None, :])
    c_decay = jnp.exp(c * log_g)
    intra2 = intra.reshape(H_RET // 2, 2 * c, c)
    qdec = jnp.repeat(q_decay, HEAD_DIM, axis=1)
    kdec = jnp.repeat(k_decay, HEAD_DIM, axis=1)
    cdec = jnp.repeat(c_decay.reshape(H_RET // 2, 2), HEAD_DIM, axis=1)[:, :, None] * jnp.ones((1, 1, LANES), F32)
    return intra2, qdec, kdec, cdec


def _rotary_tables(pos):
    half = HEAD_DIM // 2
    inv = ROPE_BASE ** (-jnp.arange(half, dtype=F32) / half)
    ang = pos.astype(F32)[:, None] * inv[None, :]
    cos, sin = jnp.cos(ang), jnp.sin(ang)
    cos_t = jnp.concatenate([cos, cos, cos, cos], axis=1)
    sin_t = jnp.concatenate([-sin, sin, -sin, sin], axis=1)
    return cos_t, sin_t


def retention_prompt(z3, gain):
    b, t, _ = z3.shape
    c = RET_CHUNK
    cos_t, sin_t = _rotary_tables(jnp.arange(t))
    intra2, qdec, kdec, cdec = _retention_consts(c)
    blk = lambda col: pl.BlockSpec((None, c, RET_W), lambda bi, ci, col=col: (bi, ci, col))
    fixed2 = lambda shape: pl.BlockSpec(shape, lambda bi, ci: (0,) * len(shape))
    return pl.pallas_call(
        _retention_prompt_kernel,
        grid=(b, t // c),
        in_specs=[blk(COL_RQ // RET_W), blk(COL_RK // RET_W), blk(COL_RV // RET_W), blk(COL_RG // RET_W),
                  pl.BlockSpec((c, LANES), lambda bi, ci: (ci, 0)),
                  pl.BlockSpec((c, LANES), lambda bi, ci: (ci, 0)),
                  fixed2((1, RET_W)), fixed2((H_RET // 2, 2 * c, c)), fixed2((c, RET_W)),
                  fixed2((c, RET_W)), fixed2((H_RET // 2, LANES, LANES))],
        out_specs=[pl.BlockSpec((None, c, RET_W), lambda bi, ci: (bi, ci, 0)),
                   pl.BlockSpec((None, H_RET, HEAD_DIM, HEAD_DIM), lambda bi, ci: (bi, 0, 0, 0))],
        out_shape=[jax.ShapeDtypeStruct((b, t, RET_W), BF16),
                   jax.ShapeDtypeStruct((b, H_RET, HEAD_DIM, HEAD_DIM), F32)],
        scratch_shapes=[pltpu.VMEM((H_RET // 2, LANES, LANES), F32)],
        compiler_params=_cparams(2),
        name="retention_prompt",
    )(z3, z3, z3, z3, cos_t, sin_t, gain.reshape(1, RET_W), intra2, qdec, kdec, cdec)


def _retention_step_kernel(q_ref, k_ref, v_ref, g_ref, s_ref, cos_ref, sin_ref, gamma_ref, gain_ref,
                           y_ref, s_out_ref):
    half = HEAD_DIM // 2
    cos = cos_ref[...]
    sin = sin_ref[...]

    def rot(x):
        x1, x2 = x[:half], x[half:]
        return jnp.concatenate([x1 * cos - x2 * sin, x1 * sin + x2 * cos], axis=0)

    qr = rot(q_ref[...])
    kr = rot(k_ref[...]) * (HEAD_DIM ** -0.5)
    v = v_ref[...]
    gamma = gamma_ref[...]
    o = jnp.zeros_like(v)
    for d in range(HEAD_DIM):
        s_new = s_ref[d] * gamma + kr[d:d + 1] * v
        s_out_ref[d] = s_new
        o = o + qr[d:d + 1] * s_new
    ms = jnp.mean(o * o, axis=0, keepdims=True)
    gate = g_ref[...]
    y_ref[...] = o * lax.rsqrt(ms + EPS) * gain_ref[...] * (gate * _sigmoid(gate))


def retention_step(zt, state, layer, pos, gain):
    bs = zt.shape[1]
    half = HEAD_DIM // 2
    inv = ROPE_BASE ** (-jnp.arange(half, dtype=F32) / half)
    ang = jnp.full((1,), pos).astype(F32)[:, None] * inv[None, :]
    cos = jnp.broadcast_to(jnp.cos(ang).reshape(half, 1), (half, bs))
    sin = jnp.broadcast_to(jnp.sin(ang).reshape(half, 1), (half, bs))
    gamma = 1.0 - jnp.exp2(-5.0 - jnp.arange(H_RET, dtype=F32))
    gamma = jnp.broadcast_to(gamma.reshape(H_RET, 1, 1), (H_RET, 1, bs))
    gain_b = jnp.broadcast_to(gain.reshape(H_RET, HEAD_DIM, 1), (H_RET, HEAD_DIM, bs))
    head = lambda col: pl.BlockSpec((HEAD_DIM, bs), lambda h, col=col: (col // HEAD_DIM + h, 0))
    fixed = pl.BlockSpec((half, bs), lambda h: (0, 0))
    return pl.pallas_call(
        _retention_step_kernel,
        grid=(H_RET,),
        in_specs=[head(COL_RQ), head(COL_RK), head(COL_RV), head(COL_RG),
                  pl.BlockSpec((None, HEAD_DIM, HEAD_DIM, bs), lambda h: (layer * H_RET + h, 0, 0, 0)),
                  fixed, fixed,
                  pl.BlockSpec((None, 1, bs), lambda h: (h, 0, 0)),
                  pl.BlockSpec((None, HEAD_DIM, bs), lambda h: (h, 0, 0))],
        out_specs=[pl.BlockSpec((HEAD_DIM, bs), lambda h: (h, 0)),
                   pl.BlockSpec((None, HEAD_DIM, HEAD_DIM, bs), lambda h: (h, 0, 0, 0))],
        out_shape=[jax.ShapeDtypeStruct((RET_W, bs), F32),
                   jax.ShapeDtypeStruct((H_RET, HEAD_DIM, HEAD_DIM, bs), F32)],
        compiler_params=_cparams(1),
        name="retention_step",
    )(zt, zt, zt, zt, state, cos, sin, gamma, gain_b)


def _gelu_tanh(x):
    return 0.5 * x * (1.0 + jnp.tanh(math.sqrt(2.0 / math.pi) * (x + 0.044715 * (x * x * x))))


def _compress_kernel(*refs, n_pieces):
    x_refs = refs[1:1 + n_pieces]
    pe_ref, w1_ref, w2_ref, kc_ref, vc_ref, rows_ref = refs[1 + n_pieces:]
    width = x_refs[0].shape[1]
    n = n_pieces * width // CMP_STRIDE
    for kv, o_ref in enumerate((kc_ref, vc_ref)):
        for p, x_ref in enumerate(x_refs):
            for c in range(width // LANES):
                rows_ref[pl.ds(p * width + c * LANES, LANES), :] = (
                    x_ref[kv * LANES:(kv + 1) * LANES, c * LANES:(c + 1) * LANES].T)
        xk = jnp.concatenate([rows_ref[pl.ds(j, n, stride=CMP_STRIDE), :] for j in range(CMP_STRIDE)], axis=1)
        top = _dot((xk + pe_ref[kv, 0]).astype(BF16), w1_ref[kv, 0])
        bot = _dot((xk + pe_ref[kv, 1]).astype(BF16), w1_ref[kv, 1])
        pre = top + pltpu.roll(bot, n - 1, 0)
        hid = _gelu_tanh(pre).astype(BF16)
        o_ref[...] = jnp.concatenate(
            [_dot(hid[:, g * CMP_HIDDEN:(g + 1) * CMP_HIDDEN], w2_ref[kv]) for g in range(N_KV)], axis=1)


def _compress_weights(pe, w1, w2):
    pe_r = pe.reshape(2, 2, CMP_STRIDE, 1, HEAD_DIM)
    pe_t = jnp.broadcast_to(pe_r, (2, 2, CMP_STRIDE, N_KV, HEAD_DIM)).reshape(2, 2, 1, CMP_STRIDE * LANES)
    w1r = w1.reshape(2, 2, CMP_STRIDE, HEAD_DIM, CMP_HIDDEN)
    eye = jnp.eye(N_KV, dtype=w1.dtype)
    wbig = jnp.einsum('khjdc,gG->khjgdGc', w1r, eye)
    wbig = wbig.reshape(2, 2, CMP_STRIDE * LANES, N_KV * CMP_HIDDEN)
    return pe_t.astype(F32), wbig.astype(BF16), w2.astype(BF16)


def compress(src, idx, pe_t, w1b, w2b):
    bsz, n_pieces = idx.shape
    width = src.shape[2]
    n = n_pieces * width // CMP_STRIDE
    piece = lambda p: pl.BlockSpec((None, KV_W, width), lambda b, ix, p=p: (ix[b, p], 0, 0))
    fixed = lambda shape: pl.BlockSpec(shape, lambda b, ix: (0,) * len(shape))
    out = pl.BlockSpec((None, n, LANES), lambda b, ix: (b, 0, 0))
    grid_spec = pltpu.PrefetchScalarGridSpec(
        num_scalar_prefetch=1,
        grid=(bsz,),
        in_specs=[piece(p) for p in range(n_pieces)]
        + [fixed(pe_t.shape), fixed(w1b.shape), fixed(w2b.shape)],
        out_specs=[out, out],
        scratch_shapes=[pltpu.VMEM((n_pieces * width, LANES), F32)],
    )
    return pl.pallas_call(
        functools.partial(_compress_kernel, n_pieces=n_pieces),
        grid_spec=grid_spec,
        out_shape=[jax.ShapeDtypeStruct((bsz, n, LANES), F32)] * 2,
        compiler_params=_cparams(1),
        name="compress",
    )(idx, *([src] * n_pieces), pe_t, w1b, w2b)


def _masked_softmax(s, mask):
    s = jnp.where(mask, s, NEG)
    m = jnp.max(s, axis=-1, keepdims=True)
    e = jnp.where(mask, jnp.exp(s - m), 0.0)
    return e / jnp.maximum(jnp.sum(e, axis=-1, keepdims=True), TINY)


def _block_scores(imp, qpos, n_sel, axis=1):
    blk = lax.broadcasted_iota(I32, imp.shape, axis)
    cur = qpos // SEL_BLOCK
    forced = (blk == 0) | (blk == cur) | (blk == cur - 1)
    valid = blk * SEL_BLOCK <= qpos
    score = jnp.where(forced, FORCED_SCORE, jnp.where(valid, imp, -1.0))
    return jnp.where(blk < n_sel, score, -2.0)


def _select_top(score, n_top, axis=1):
    blk = lax.broadcasted_iota(I32, score.shape, axis)
    rank = jnp.zeros(score.shape, F32)
    for i in range(score.shape[axis]):
        one = score[:, i:i + 1] if axis == 1 else score[i:i + 1, :]
        beats = (one > score) | ((one == score) & (blk > i))
        rank = rank + jnp.where(beats, 1.0, 0.0)
    return jnp.where(rank < n_top, 1.0, 0.0)


def _attend_t(s, ok, m_prev, l_prev):
    s = jnp.where(ok, s, NEG)
    m_new = jnp.maximum(m_prev, jnp.max(s, axis=0, keepdims=True))
    alpha = jnp.exp(m_prev - m_new)
    p = jnp.exp(s - m_new)
    l_new = alpha * l_prev + jnp.sum(p, axis=0, keepdims=True)
    return m_new, l_new, alpha, p


def _nsa_prompt_kernel(q_ref, gate_ref, kc_ref, vct_ref, ks_ref, vst_ref, kw_ref, vwt_ref, bc_ref, bt_ref,
                       o_ref, qt_ref, pcs_ref, pick_ref, acc_ref, os_ref, s_ref, p_ref, *, n_cmp, n_sel):
    i = pl.program_id(1)
    qs = i * Q_BLOCK
    key_i = lax.broadcasted_iota(I32, (KV_CHUNK, Q_BLOCK), 0)
    tok_i = lax.broadcasted_iota(I32, (KV_CHUNK, Q_BLOCK), 1)
    row8 = lax.broadcasted_iota(I32, (H_NSA, Q_BLOCK), 0)
    zeros_half = jnp.zeros((HEAD_DIM, Q_BLOCK), F32)

    for k2 in range(H_NSA // 2):
        slab_t = q_ref[:, k2 * LANES:(k2 + 1) * LANES].T * (HEAD_DIM ** -0.5)
        for h in (2 * k2, 2 * k2 + 1):
            qh = slab_t[(h % 2) * HEAD_DIM:(h % 2 + 1) * HEAD_DIM]
            both = [qh, zeros_half] if h // R_GRP == 0 else [zeros_half, qh]
            qt_ref[:, h * Q_BLOCK:(h + 1) * Q_BLOCK] = jnp.concatenate(both, axis=0).astype(BF16)

    sig_t = _sigmoid(gate_ref[...]).T

    ncp = kc_ref.shape[0]
    kc = kc_ref[...].astype(BF16)
    n_c = lax.broadcasted_iota(I32, (ncp, Q_BLOCK), 0)
    t_c = lax.broadcasted_iota(I32, (ncp, Q_BLOCK), 1)
    ok_c = (n_c * CMP_STRIDE + (L_CMP - 1) <= qs + t_c) & (n_c < n_cmp)
    blk_pos = qs + lax.broadcasted_iota(I32, (pick_ref.shape[1], Q_BLOCK), 1)
    o_c = []
    for g in range(N_KV):
        vct = vct_ref[g * HEAD_DIM:(g + 1) * HEAD_DIM, :].astype(BF16)
        pcs = jnp.zeros((ncp, Q_BLOCK), F32)
        for r in range(R_GRP):
            h = g * R_GRP + r
            s = _dot(kc, qt_ref[:, h * Q_BLOCK:(h + 1) * Q_BLOCK]) + bc_ref[h]
            s = jnp.where(ok_c, s, NEG)
            e = jnp.where(ok_c, jnp.exp(s - jnp.max(s, axis=0, keepdims=True)), 0.0)
            p = e * (1.0 / jnp.maximum(jnp.sum(e, axis=0, keepdims=True), TINY))
            o_c.append(_dot(vct, p.astype(BF16)))
            pcs = pcs + p
        pcs_ref[...] = pcs
        per = SEL_BLOCK // CMP_STRIDE
        imp = pcs_ref[pl.ds(0, ncp // per, stride=per), :]
        for c in range(1, per):
            imp = imp + pcs_ref[pl.ds(c, ncp // per, stride=per), :]
        pick_ref[g] = _select_top(_block_scores(imp, blk_pos, n_sel, axis=0), min(N_SELECT, n_sel), axis=0)

    dmat = tok_i - key_i

    def stream(k_ref, vt_ref, n_chunks, chunk_of, mask_of):
        def scores(c, slot):
            off, _ = chunk_of(c)
            k = k_ref[pl.ds(off, KV_CHUNK), :].astype(BF16)
            for h in range(H_NSA):
                s_ref[slot, h] = _dot(k, qt_ref[:, h * Q_BLOCK:(h + 1) * Q_BLOCK])

        def values(c, slot, alpha_all):
            off, _ = chunk_of(c)
            for g in range(N_KV):
                vt = vt_ref[g * HEAD_DIM:(g + 1) * HEAD_DIM, pl.ds(off, KV_CHUNK)].astype(BF16)
                for h in range(g * R_GRP, (g + 1) * R_GRP):
                    acc_ref[h] = acc_ref[h] * alpha_all[h:h + 1] + _dot(vt, p_ref[slot, h])

        def softmax(c, slot, m_all, l_all):
            _, rel = chunk_of(c)
            ok_g = mask_of(c, rel)
            bias_idx = min(max(rel, 0), FAR_REL) if isinstance(rel, int) else jnp.clip(rel, 0, FAR_REL)
            alpha_all = jnp.zeros((H_NSA, Q_BLOCK), F32)
            for h in range(H_NSA):
                s = s_ref[slot, h] + bt_ref[bias_idx, h]
                m_new, l_new, alpha, p = _attend_t(s, ok_g[h // R_GRP], m_all[h:h + 1], l_all[h:h + 1])
                p_ref[slot, h] = p.astype(BF16)
                m_all = jnp.where(row8 == h, m_new, m_all)
                l_all = jnp.where(row8 == h, l_new, l_all)
                alpha_all = jnp.where(row8 == h, alpha, alpha_all)
            return m_all, l_all, alpha_all

        def stage(c, slot, carry, prefetch=True):
            m_all, l_all, alpha_prev = carry
            if prefetch:
                scores(c + 1, 1 - slot)
            values(c - 1, 1 - slot, alpha_prev)
            return softmax(c, slot, m_all, l_all)

        acc_ref[...] = jnp.zeros_like(acc_ref)
        p_ref[1] = jnp.zeros(p_ref.shape[1:], BF16)
        scores(0, 0)
        carry = (jnp.full((H_NSA, Q_BLOCK), NEG, F32), jnp.zeros((H_NSA, Q_BLOCK), F32),
                 jnp.ones((H_NSA, Q_BLOCK), F32))
        if isinstance(n_chunks, int):
            for c in range(n_chunks):
                carry = stage(c, c % 2, carry, prefetch=c + 1 < n_chunks)
            last = n_chunks - 1
        else:
            pairs = (n_chunks + 1) // 2
            carry = lax.fori_loop(0, pairs, lambda t, cr: stage(2 * t + 1, 1, stage(2 * t, 0, cr)), carry)
            last = 2 * pairs - 1
        m_all, l_all, alpha_last = carry
        values(last, last % 2 if isinstance(last, int) else 1, alpha_last)
        return m_all, l_all

    def sel_chunk(c):
        return pl.multiple_of(jnp.clip(c, 0, i) * KV_CHUNK, KV_CHUNK), i - c

    def sel_mask(c, rel):
        causal = rel * KV_CHUNK + dmat >= 0
        per = KV_CHUNK // SEL_BLOCK
        first = jnp.minimum(c, i) * per
        ok_g = []
        for g in range(N_KV):
            picked = jnp.concatenate(
                [jnp.broadcast_to(pick_ref[g, pl.ds(first + b, 1), :], (SEL_BLOCK, Q_BLOCK)) for b in range(per)],
                axis=0)
            ok_g.append(causal & (picked > 0.5))
        return ok_g

    m_s, l_s = stream(ks_ref, vst_ref, i + 1, sel_chunk, sel_mask)
    for h in range(H_NSA):
        os_ref[h] = jnp.where(m_s[h:h + 1] > 0.5 * NEG, acc_ref[h] / jnp.maximum(l_s[h:h + 1], TINY), 0.0)

    n_back = WINDOW // KV_CHUNK

    def win_chunk(c):
        return pl.multiple_of(jnp.clip(i - n_back + c, 0, i) * KV_CHUNK, KV_CHUNK), n_back - c

    def win_mask(c, rel):
        dist = rel * KV_CHUNK + dmat + jnp.where(i - n_back + c < 0, WINDOW, 0)
        ok = (dist >= 0) & (dist < WINDOW)
        return [ok, ok]

    m_w, l_w = stream(kw_ref, vwt_ref, n_back + 1, win_chunk, win_mask)

    for k2 in range(H_NSA // 2):
        pair = []
        for h in (2 * k2, 2 * k2 + 1):
            o_w = jnp.where(m_w[h:h + 1] > 0.5 * NEG, acc_ref[h] / jnp.maximum(l_w[h:h + 1], TINY), 0.0)
            pair.append(sig_t[3 * h:3 * h + 1] * o_c[h] + sig_t[3 * h + 1:3 * h + 2] * os_ref[h]
                        + sig_t[3 * h + 2:3 * h + 3] * o_w)
        o_ref[:, k2 * LANES:(k2 + 1) * LANES] = jnp.concatenate(pair, axis=0).T.astype(o_ref.dtype)


def _selection_consts(n_rows_cmp, n_keys):
    n = np.arange(n_rows_cmp)
    blk = np.arange(SEL_LANES)
    pool = (n[:, None] // (SEL_BLOCK // CMP_STRIDE) == blk[None, :]).astype(np.float32)
    key = np.arange(n_keys)
    expand = (blk[:, None] == key[None, :] // SEL_BLOCK).astype(np.float32)
    return jnp.asarray(pool, BF16), jnp.asarray(expand, BF16)


def nsa_prompt(z3, kvt, kc, vct, bias_ct, bias_tt):
    b, t, _ = z3.shape
    n_sub = t // CMP_STRIDE
    n_sel = -(-t // SEL_BLOCK)
    assert kc.shape[1] == n_sub and n_sub == n_sel * (SEL_BLOCK // CMP_STRIDE) and n_sel % SUBLANES == 0
    fixed = lambda shape: pl.BlockSpec(shape, lambda bi, i: (0,) * len(shape))
    v_rows = lambda branch: (branch * KV_W + N_KV * HEAD_DIM) // LANES
    return pl.pallas_call(
        functools.partial(_nsa_prompt_kernel, n_cmp=n_sub - 1, n_sel=n_sel),
        grid=(b, t // Q_BLOCK),
        in_specs=[pl.BlockSpec((None, Q_BLOCK, NSA_QW), lambda bi, i: (bi, i, COL_NQ // NSA_QW)),
                  pl.BlockSpec((None, Q_BLOCK, LANES), lambda bi, i: (bi, i, COL_NG // LANES)),
                  pl.BlockSpec((None, n_sub, LANES), lambda bi, i: (bi, 0, 0)),
                  pl.BlockSpec((None, LANES, n_sub), lambda bi, i: (bi, 0, 0)),
                  pl.BlockSpec((None, t, LANES), lambda bi, i: (bi, 0, COL_KVS // LANES)),
                  pl.BlockSpec((None, LANES, t), lambda bi, i: (bi, v_rows(1), 0)),
                  pl.BlockSpec((None, t, LANES), lambda bi, i: (bi, 0, COL_KVW // LANES)),
                  pl.BlockSpec((None, LANES, t), lambda bi, i: (bi, v_rows(2), 0)),
                  pl.BlockSpec((H_NSA, n_sub, Q_BLOCK), lambda bi, i: (0, 0, i)),
                  fixed(bias_tt.shape)],
        out_specs=pl.BlockSpec((None, Q_BLOCK, NSA_QW), lambda bi, i: (bi, i, 0)),
        out_shape=jax.ShapeDtypeStruct((b, t, NSA_QW), BF16),
        scratch_shapes=[pltpu.VMEM((LANES, H_NSA * Q_BLOCK), BF16),
                        pltpu.VMEM((n_sub, Q_BLOCK), F32),
                        pltpu.VMEM((N_KV, n_sel, Q_BLOCK), F32),
                        pltpu.VMEM((H_NSA, HEAD_DIM, Q_BLOCK), F32),
                        pltpu.VMEM((H_NSA, HEAD_DIM, Q_BLOCK), F32),
                        pltpu.VMEM((2, H_NSA, KV_CHUNK, Q_BLOCK), F32),
                        pltpu.VMEM((2, H_NSA, KV_CHUNK, Q_BLOCK), BF16)],
        compiler_params=_cparams(2),
        name="nsa_prompt",
    )(z3, z3, kc, vct, z3, kvt, z3, kvt, bias_ct, bias_tt)


def _nsa_step_kernel(pt_ref, z_ref, kc_ref, vc_ref, *rest, n_pages, per_step, **static):
    del pt_ref
    pages = rest[:per_step * n_pages]
    win_ref, bcs_ref, bss_ref, bws_ref, pool_ref, exp_ref, o_ref, wo_ref = rest[per_step * n_pages:]
    for u in range(per_step):
        _nsa_step_one(z_ref.at[u], kc_ref.at[u], vc_ref.at[u], pages[u * n_pages:(u + 1) * n_pages], win_ref.at[u],
                      bcs_ref, bss_ref, bws_ref, pool_ref, exp_ref, o_ref.at[u], wo_ref.at[u], **static)


def _nsa_step_one(z_ref, kc_ref, vc_ref, pages, win_ref, bcs_ref, bss_ref, bws_ref, pool_ref, exp_ref, o_ref, wo_ref,
                  *, past, wbuf, n_cmp, n_sel):
    z = z_ref[...]
    row8 = lax.broadcasted_iota(I32, (H_NSA, LANES), 0)
    lane8 = lax.broadcasted_iota(I32, (H_NSA, LANES), 1)
    left8 = lane8 < HEAD_DIM
    qbd = jnp.zeros((H_NSA, LANES), F32)
    for h in range(H_NSA):
        slab = jnp.broadcast_to(z[:, COL_NQ + (h // 2) * LANES:COL_NQ + (h // 2 + 1) * LANES], (H_NSA, LANES))
        g = h // R_GRP
        if h % 2 != g:
            slab = pltpu.roll(slab, HEAD_DIM, 1)
        keep = (row8 == h) & (left8 if g == 0 else jnp.logical_not(left8))
        qbd = jnp.where(keep, slab, qbd)
    qbd = (qbd * (HEAD_DIM ** -0.5)).astype(BF16)

    ncp = kc_ref.shape[0]
    n_c = lax.broadcasted_iota(I32, (H_NSA, ncp), 1)
    s_c = _dot_nt(qbd, kc_ref[...].astype(BF16)) + bcs_ref[...]
    p_c = _masked_softmax(s_c, (n_c * CMP_STRIDE + (L_CMP - 1) <= past) & (n_c < n_cmp))
    o_c = _dot(p_c.astype(BF16), vc_ref[...].astype(BF16))
    pc0 = jnp.sum(p_c[0:R_GRP], axis=0, keepdims=True)
    pc1 = jnp.sum(p_c[R_GRP:], axis=0, keepdims=True)
    rowc = lax.broadcasted_iota(I32, (H_NSA, ncp), 0)
    pcs = jnp.where(rowc < R_GRP, jnp.broadcast_to(pc0, (H_NSA, ncp)), jnp.broadcast_to(pc1, (H_NSA, ncp)))
    imp = _dot3(pcs, pool_ref[...])
    sel = _select_top(_block_scores(imp, jnp.full(imp.shape, past, I32), n_sel), min(N_SELECT, n_sel))
    picked = _dot(sel.astype(BF16), exp_ref[...])

    c128 = lax.broadcasted_iota(I32, (LANES, KV_CHUNK), 1)

    def new_col(col):
        return jnp.broadcast_to(z[:, col:col + LANES], (KV_CHUNK, LANES)).T

    def first_col(tile):
        return jnp.where(c128 == 0, tile, 0.0).astype(BF16)

    ks = [pg[0:LANES, :].astype(BF16) for pg in pages] + [first_col(new_col(COL_KVS))]
    vs = [pg[LANES:2 * LANES, :].astype(BF16) for pg in pages] + [first_col(new_col(COL_KVS + LANES))]
    s_s = jnp.concatenate([_dot(qbd, k) for k in ks], axis=1) + bss_ref[...]
    key = lax.broadcasted_iota(I32, s_s.shape, 1)
    p_s = _masked_softmax(s_s, (picked > 0.5) & (key <= past)).astype(BF16)
    o_s = _dot_nt(p_s[:, 0:KV_CHUNK], vs[0])
    for c in range(1, len(vs)):
        o_s = o_s + _dot_nt(p_s[:, c * KV_CHUNK:(c + 1) * KV_CHUNK], vs[c])

    kw_new, vw_new = new_col(COL_KVW), new_col(COL_KVW + LANES)
    win = win_ref[...]
    s_w = jnp.concatenate([_dot(qbd, win[0:LANES].astype(BF16)), _dot(qbd, first_col(kw_new))], axis=1) + bws_ref[...]
    colw = lax.broadcasted_iota(I32, s_w.shape, 1)
    p_w = _masked_softmax(s_w, (colw <= wbuf) & (wbuf - colw < WINDOW)).astype(BF16)
    o_w = _dot_nt(p_w[:, 0:wbuf], win[LANES:].astype(BF16)) + _dot_nt(p_w[:, wbuf:], first_col(vw_new))

    shifted = pltpu.roll(win, wbuf - 1, 1)
    new_kv = jnp.concatenate([kw_new, vw_new], axis=0)
    last = jnp.where(lax.broadcasted_iota(I32, (KV_W, LANES), 1) == LANES - 1, new_kv, shifted[:, wbuf - LANES:])
    wo_ref[:, 0:wbuf - LANES] = shifted[:, 0:wbuf - LANES]
    wo_ref[:, wbuf - LANES:] = last

    sig = jnp.broadcast_to(_sigmoid(z[:, COL_NG:COL_NG + LANES]), (H_NSA, LANES))

    def gate_col(c):
        return jnp.sum(jnp.where(lane8 == row8 * 3 + c, sig, 0.0), axis=1, keepdims=True)

    o = gate_col(0) * o_c + gate_col(1) * o_s + gate_col(2) * o_w
    fix = jnp.where((row8 % 2) != (row8 // R_GRP), pltpu.roll(o, HEAD_DIM, 1), o)
    left1 = left8[0:1]
    o_ref[...] = jnp.concatenate(
        [jnp.where(left1, fix[2 * k2:2 * k2 + 1], fix[2 * k2 + 1:2 * k2 + 2]) for k2 in range(H_NSA // 2)], axis=1)


def nsa_step(zs3, kc, vc, sel_cache, win_cache, page_idx, win_row0, bias_cs, bias_ss, bias_ws, past):
    bs = zs3.shape[0]
    n_pages = page_idx.shape[1]
    wbuf = win_cache.shape[2]
    n_sub = (past + 1) // CMP_STRIDE
    n_sel = -(-(past + 1) // SEL_BLOCK)
    n_keys = (n_pages + 1) * PAGE_SIZE
    assert n_sel <= SEL_LANES and PAGE_SIZE == KV_CHUNK and wbuf % KV_CHUNK == 0 and kc.shape[1] == n_sub
    pool, expand = _selection_consts(n_sub, n_keys)
    per = 2 if bs % 2 == 0 and win_row0 % 2 == 0 else 1
    fixed = lambda shape: pl.BlockSpec(shape, lambda b, pt: (0,) * len(shape))
    page = lambda u, p: pl.BlockSpec((None, KV_W, PAGE_SIZE), lambda b, pt, u=u, p=p: (pt[per * b + u, p], 0, 0))
    grid_spec = pltpu.PrefetchScalarGridSpec(
        num_scalar_prefetch=1,
        grid=(bs // per,),
        in_specs=[pl.BlockSpec((per, 1, N_IN_PAD), lambda b, pt: (b, 0, 0)),
                  pl.BlockSpec((per, n_sub, LANES), lambda b, pt: (b, 0, 0)),
                  pl.BlockSpec((per, n_sub, LANES), lambda b, pt: (b, 0, 0))]
        + [page(u, p) for u in range(per) for p in range(n_pages)]
        + [pl.BlockSpec((per, KV_W, wbuf), lambda b, pt: (win_row0 // per + b, 0, 0)),
           fixed(bias_cs.shape), fixed(bias_ss.shape), fixed(bias_ws.shape), fixed(pool.shape), fixed(expand.shape)],
        out_specs=[pl.BlockSpec((per, 1, NSA_QW), lambda b, pt: (b, 0, 0)),
                   pl.BlockSpec((per, KV_W, wbuf), lambda b, pt: (b, 0, 0))],
    )
    return pl.pallas_call(
        functools.partial(_nsa_step_kernel, n_pages=n_pages, per_step=per, past=past, wbuf=wbuf, n_cmp=n_sub - 1,
                          n_sel=n_sel),
        grid_spec=grid_spec,
        out_shape=[jax.ShapeDtypeStruct((bs, 1, NSA_QW), F32), jax.ShapeDtypeStruct((bs, KV_W, wbuf), F32)],
        compiler_params=_cparams(1),
        name="nsa_step",
    )(page_idx, zs3, kc, vc, *([sel_cache] * (per * n_pages)), win_cache, bias_cs, bias_ss, bias_ws, pool, expand)


def _route(logits, tm):
    m = logits.shape[0]
    top_v, top_i = lax.top_k(logits, TOP_K)
    top_w = jax.nn.softmax(top_v, axis=-1)
    slot_e = top_i.reshape(-1)
    onehot = (slot_e[:, None] == jnp.arange(N_EXPERTS)[None, :]).astype(I32)
    before = jnp.cumsum(onehot, axis=0) - onehot
    rank = jnp.sum(before * onehot, axis=1)
    count = jnp.sum(onehot, axis=0)
    padded = -(-count // tm) * tm
    start = jnp.cumsum(padded) - padded
    pos = start[slot_e] + rank
    n_rows = (-(-(m * TOP_K) // tm) + N_EXPERTS) * tm
    row_slot = jnp.full((n_rows,), -1, I32).at[pos].set(jnp.arange(m * TOP_K, dtype=I32))
    slot = jnp.maximum(row_slot, 0)
    row_token = slot // TOP_K
    row_gate = jnp.where(row_slot >= 0, top_w.reshape(-1)[slot], 0.0)
    tile_start = jnp.arange(n_rows // tm, dtype=I32) * tm
    tile_expert = jnp.clip(jnp.searchsorted(jnp.cumsum(padded), tile_start, side='right'), 0, N_EXPERTS - 1)
    tile_valid = (tile_start < jnp.sum(padded)).astype(I32)
    return pos.reshape(m, TOP_K), row_token, row_gate.reshape(n_rows, 1), tile_expert.astype(I32), tile_valid


def _dense_ffn(h, g, w1, w3, w2):
    m = h.shape[0]
    tm = _row_tile(m)
    ones = jnp.ones((m // tm,), I32)
    return grouped_ffn(h, g, jnp.ones((m, 1), F32), jnp.zeros((m // tm,), I32), ones,
                       w1[None], w3[None], w2[None], tm=tm, tf=w1.shape[1] // 2, residual=True)


def kernel(x_prompt, x_sample, p_prompt, p_sample, state_ret, cache_win_kv, cache_cmp_kv, cache_sel_kv, page_table, rel_bias, norm_mix, w_in, ret_gain, cmp_pe, cmp_w1, cmp_w2, w_out, norm_ffn, ffn_w1, ffn_w3, ffn_w2, router, moe_w1, moe_w3, moe_w2, ple_norm, ple_gate, ple_proj, norm_final):
    b_p, t_p, d = x_prompt.shape
    b_s = x_sample.shape[0]
    depth = w_in.shape[0]
    n_pages = page_table.shape[1]
    past = n_pages * PAGE_SIZE
    n_pool = cache_cmp_kv.shape[1]
    w_buf = cache_win_kv.shape[2]
    assert x_sample.shape[1] == 1 and t_p % Q_BLOCK == 0 and t_p >= w_buf and past % CMP_STRIDE == 0
    row = (2, N_KV, HEAD_DIM)
    m_p = b_p * t_p

    ar = lambda n: jnp.arange(n, dtype=I32)
    n_sub_p = t_p // CMP_STRIDE
    bias_ct = bias_lookup(rel_bias, ar(t_p)[None, :] - (ar(n_sub_p)[:, None] * CMP_STRIDE + L_CMP - 1))
    rel_t = (ar(FAR_REL + 1)[:, None, None] * KV_CHUNK + ar(Q_BLOCK)[None, None, :] - ar(KV_CHUNK)[None, :, None])
    bias_tt = bias_lookup(rel_bias, rel_t.reshape((FAR_REL + 1) * KV_CHUNK, Q_BLOCK))
    bias_tt = bias_tt.reshape(H_NSA, FAR_REL + 1, KV_CHUNK, Q_BLOCK).transpose(1, 0, 2, 3)
    n_sub_s = (past + 1) // CMP_STRIDE
    rows8 = jnp.zeros((SUBLANES, 1), I32)
    bias_cs = bias_lookup(rel_bias, rows8 + (past - (ar(n_sub_s)[None, :] * CMP_STRIDE + L_CMP - 1)))[:, 0]
    bias_ss = bias_lookup(rel_bias, rows8 + (past - ar((n_pages + 1) * PAGE_SIZE)[None, :]))[:, 0]
    bias_ws = bias_lookup(rel_bias, rows8 + (w_buf - ar(w_buf + KV_CHUNK)[None, :]))[:, 0]

    def rows_last(x):
        return jnp.moveaxis(x, 2, -1).reshape(x.shape[0] * x.shape[1], KV_W, x.shape[2])

    def rows_first(xt, lead):
        return jnp.moveaxis(xt.reshape((lead,) + row + (xt.shape[-1],)), -1, 1)

    cmp_cache = rows_last(cache_cmp_kv)
    sel_cache = rows_last(cache_sel_kv)
    win_cache = rows_last(cache_win_kv)
    state_t = jnp.moveaxis(state_ret, 1, -1).reshape(depth * H_RET, HEAD_DIM, HEAD_DIM, b_s)

    h_p = x_prompt.reshape(m_p, d)
    h_s = x_sample.reshape(b_s, d)
    rp, rs, wp, ws, cp, cs, sp, ss = [], [], [], [], [], [], [], []
    for i in range(depth):
        w_in_b = jnp.pad(w_in[i], ((0, 0), (0, N_IN_PAD - N_IN))).astype(BF16)
        w_out_b = w_out[i].astype(BF16)
        pe_t, cw1, cw2 = _compress_weights(cmp_pe[i], cmp_w1[i], cmp_w2[i])

        w_kvt = w_in[i][:, COL_KVC:COL_KVC + 3 * KV_W].T.astype(BF16)
        z, kvt = rms_matmul_kvt(h_p, norm_mix[i], w_in_b, w_kvt, b_p)
        z3 = z.reshape(b_p, t_p, N_IN_PAD)
        ret_y, ret_state = retention_prompt(z3, ret_gain[i])
        kc, vc = compress(kvt, ar(b_p)[:, None], pe_t, cw1, cw2)
        nsa_y = nsa_prompt(z3, kvt, kc, jnp.swapaxes(vc, 1, 2), bias_ct, bias_tt)
        h_p = out_proj(h_p, ret_y.reshape(m_p, RET_W), nsa_y.reshape(m_p, NSA_QW), w_out_b)
        rp.append(ret_state)
        cp.append(rows_first(kvt[:, 0:KV_W], b_p))
        sp.append(rows_first(kvt[:, KV_W:2 * KV_W], b_p))
        wp.append(rows_first(kvt[:, 2 * KV_W:, t_p - w_buf:], b_p))

        zs = rms_matmul(h_s, norm_mix[i], w_in_b)
        zst = zs.T
        ret_yst, state_s = retention_step(zst, state_t, i, past, ret_gain[i])
        page_idx = page_table + i * n_pool
        grp = 4 if b_s % 4 == 0 else 1
        kc_s, vc_s = compress(cmp_cache, page_idx.reshape(b_s // grp, grp * n_pages), pe_t, cw1, cw2)
        kc_s, vc_s = kc_s.reshape(b_s, n_sub_s, LANES), vc_s.reshape(b_s, n_sub_s, LANES)
        nsa_ys, win_s = nsa_step(zs.reshape(b_s, 1, N_IN_PAD), kc_s, vc_s, sel_cache, win_cache, page_idx, i * b_s,
                                 bias_cs, bias_ss, bias_ws, past)
        h_s = out_proj(h_s, ret_yst.T, nsa_ys.reshape(b_s, NSA_QW), w_out_b)
        rs.append(jnp.moveaxis(state_s, -1, 0))
        ws.append(rows_first(win_s, b_s))
        cs.append(rows_first(zst[None, COL_KVC:COL_KVC + KV_W], 1).reshape((b_s, 1) + row))
        ss.append(rows_first(zst[None, COL_KVS:COL_KVS + KV_W], 1).reshape((b_s, 1) + row))

        j = i // 2
        if i % 2 == 0:
            w1, w3, w2 = ffn_w1[j].astype(BF16), ffn_w3[j].astype(BF16), ffn_w2[j].astype(BF16)
            h_p = _dense_ffn(h_p, norm_ffn[i], w1, w3, w2)
            h_s = _dense_ffn(h_s, norm_ffn[i], w1, w3, w2)
        else:
            router_w = jnp.pad(router[j], ((0, 0), (0, LANES - N_EXPERTS)))
            tf = moe_w1.shape[3] // 7
            logits_p = router_logits(h_p, norm_ffn[i], router_w)
            tm = 1024
            pos, row_token, row_gate, tile_expert, tile_valid = _route(logits_p[:, :N_EXPERTS], tm)
            y_rows = grouped_ffn(h_p.reshape(m_p, SUBLANES, d // SUBLANES), norm_ffn[i], row_gate, tile_expert,
                                 tile_valid, moe_w1[j], moe_w3[j], moe_w2[j], tm=tm, tf=tf, residual=False,
                                 row_idx=row_token)
            h_p = h_p + y_rows[pos[:, 0]] + y_rows[pos[:, 1]]
            logits_s = router_logits(h_s, norm_ffn[i], router_w)
            top_v, top_i = lax.top_k(logits_s[:, :N_EXPERTS], TOP_K)
            top_w = jax.nn.softmax(top_v, axis=-1)
            gates = jnp.sum(jax.nn.one_hot(top_i, N_EXPERTS, dtype=F32) * top_w[..., None], axis=-2)
            y_e = grouped_ffn(jnp.tile(h_s, (N_EXPERTS, 1)), norm_ffn[i], gates.T.reshape(N_EXPERTS * b_s, 1),
                              jnp.arange(N_EXPERTS, dtype=I32), jnp.ones((N_EXPERTS,), I32),
                              moe_w1[j], moe_w3[j], moe_w2[j], tm=b_s, tf=tf, residual=False)
            h_s = h_s + jnp.sum(y_e.reshape(N_EXPERTS, b_s, d), axis=0)

        wg, wpj = ple_gate[i].astype(BF16), ple_proj[i].astype(BF16)
        h_p = ple(h_p, ple_norm[i], wg, p_prompt[i].reshape(m_p, P_DIM), wpj, norm_final, i == depth - 1)
        h_s = ple(h_s, ple_norm[i], wg, p_sample[i].reshape(b_s, P_DIM), wpj, norm_final, i == depth - 1)

    return (h_p.reshape(b_p, t_p, d), h_s.reshape(b_s, 1, d), jnp.stack(rp), jnp.stack(rs), jnp.stack(wp),
            jnp.stack(ws), jnp.stack(cp), jnp.stack(cs), jnp.stack(sp), jnp.stack(ss))
```

```python
import functools
import math

import numpy as np
import jax
import jax.numpy as jnp
from jax import lax
from jax.experimental import pallas as pl
from jax.experimental.pallas import tpu as pltpu

F32 = jnp.float32
BF16 = jnp.bfloat16
I32 = jnp.int32

LANES = 128
SUBLANES = 8
VMEM_LIMIT_BYTES = 56 * 1024 * 1024

D_MODEL = 1024
HEAD_DIM = 64
H_RET = 8
H_NSA = 8
N_KV = 2
R_GRP = H_NSA // N_KV
RET_W = H_RET * HEAD_DIM
NSA_QW = H_NSA * HEAD_DIM
KV_W = 2 * N_KV * HEAD_DIM
N_IN = 4 * RET_W + NSA_QW + 3 * KV_W + 3 * H_NSA
N_IN_PAD = 3584
COL_RQ, COL_RK, COL_RV, COL_RG = 0, RET_W, 2 * RET_W, 3 * RET_W
COL_NQ = 4 * RET_W
COL_KVC = COL_NQ + NSA_QW
COL_KVS = COL_KVC + KV_W
COL_KVW = COL_KVS + KV_W
COL_NG = COL_KVW + KV_W
RET_CHUNK = 128
CMP_STRIDE = 16
L_CMP = 2 * CMP_STRIDE
CMP_HIDDEN = 2 * HEAD_DIM
SEL_BLOCK = 64
N_SELECT = 16
WINDOW = 512
Q_BLOCK = 128
N_BUCKETS = 32
MAX_DISTANCE = 128
N_EXPERTS = 8
TOP_K = 2
P_DIM = 256
PAGE_SIZE = 128
ROPE_BASE = 10000.0
EPS = 1e-6
FORCED_SCORE = 1e4
NEG = -1e30
TINY = float(np.finfo(np.float32).tiny)
SEL_LANES = 64
FAR_REL = 2
KV_CHUNK = 128


def _cparams(n_grid):
    return pltpu.CompilerParams(dimension_semantics=("arbitrary",) * n_grid,
                                vmem_limit_bytes=VMEM_LIMIT_BYTES)


def _dot(a, b):
    return jnp.dot(a, b, preferred_element_type=F32)


def _dot_nt(a, b):
    return lax.dot_general(a, b, (((1,), (1,)), ((), ())), preferred_element_type=F32)


def _split3(x):
    hi = x.astype(BF16)
    r = x - hi.astype(F32)
    mid = r.astype(BF16)
    lo = (r - mid.astype(F32)).astype(BF16)
    return hi, mid, lo


def _dot3(x, b01):
    hi, mid, lo = _split3(x)
    return _dot(hi, b01) + _dot(mid, b01) + _dot(lo, b01)


def _dot6(x, w):
    xh, xm, xl = _split3(x)
    wh, wm, wl = _split3(w)
    return (_dot(xh, wh) + _dot(xh, wm) + _dot(xm, wh)
            + _dot(xm, wm) + _dot(xh, wl) + _dot(xl, wh))


def _sigmoid(x):
    return 1.0 / (1.0 + jnp.exp(-x))


def _rms(x, g):
    return x * lax.rsqrt(jnp.mean(x * x, axis=-1, keepdims=True) + EPS) * g


def _row_tile(m):
    for t in (512, 256, 128, 64, 32, 16, 8):
        if m % t == 0:
            return t
    raise ValueError(f"row count {m} not a multiple of 8")


def _rms_matmul_kernel(x_ref, g_ref, w_ref, o_ref):
    o_ref[...] = _dot(_rms(x_ref[...], g_ref[...]).astype(BF16), w_ref[...])


def rms_matmul(x, g, w):
    m, d = x.shape
    n = w.shape[1]
    tm = min(_row_tile(m), 256)
    return pl.pallas_call(
        _rms_matmul_kernel,
        grid=(m // tm,),
        in_specs=[pl.BlockSpec((tm, d), lambda i: (i, 0)),
                  pl.BlockSpec((1, d), lambda i: (0, 0)),
                  pl.BlockSpec((d, n), lambda i: (0, 0))],
        out_specs=pl.BlockSpec((tm, n), lambda i: (i, 0)),
        out_shape=jax.ShapeDtypeStruct((m, n), F32),
        compiler_params=_cparams(1),
    )(x, g.reshape(1, d), w)


def _rms_matmul_kvt_kernel(x_ref, g_ref, w_ref, wt_ref, o_ref, ot_ref):
    xn = _rms(x_ref[...], g_ref[...]).astype(BF16)
    o_ref[...] = _dot(xn, w_ref[...])
    ot_ref[...] = _dot_nt(wt_ref[...], xn)


def rms_matmul_kvt(x, g, w, wt, bsz):
    m, d = x.shape
    n = w.shape[1]
    nt = wt.shape[0]
    t = m // bsz
    tm = min(_row_tile(t), 256)
    per = t // tm
    return pl.pallas_call(
        _rms_matmul_kvt_kernel,
        grid=(m // tm,),
        in_specs=[pl.BlockSpec((tm, d), lambda i: (i, 0)),
                  pl.BlockSpec((1, d), lambda i: (0, 0)),
                  pl.BlockSpec((d, n), lambda i: (0, 0)),
                  pl.BlockSpec((nt, d), lambda i: (0, 0))],
        out_specs=[pl.BlockSpec((tm, n), lambda i: (i, 0)),
                   pl.BlockSpec((None, nt, tm), lambda i: (i // per, 0, i % per))],
        out_shape=[jax.ShapeDtypeStruct((m, n), F32), jax.ShapeDtypeStruct((bsz, nt, t), F32)],
        compiler_params=_cparams(1),
        name="rms_matmul_kvt",
    )(x, g.reshape(1, d), w, wt)


def _router_kernel(x_ref, g_ref, w_ref, o_ref):
    o_ref[...] = _dot6(_rms(x_ref[...], g_ref[...]), w_ref[...])


def router_logits(x, g, w):
    m, d = x.shape
    n = w.shape[1]
    tm = min(_row_tile(m), 256)
    return pl.pallas_call(
        _router_kernel,
        grid=(m // tm,),
        in_specs=[pl.BlockSpec((tm, d), lambda i: (i, 0)),
                  pl.BlockSpec((1, d), lambda i: (0, 0)),
                  pl.BlockSpec((d, n), lambda i: (0, 0))],
        out_specs=pl.BlockSpec((tm, n), lambda i: (i, 0)),
        out_shape=jax.ShapeDtypeStruct((m, n), F32),
        compiler_params=_cparams(1),
        name="router",
    )(x, g.reshape(1, d), w)


def _outproj_kernel(h_ref, a_ref, b_ref, w_ref, o_ref):
    ka = a_ref.shape[1]
    o_ref[...] = (h_ref[...] + _dot(a_ref[...].astype(BF16), w_ref[0:ka, :])
                  + _dot(b_ref[...].astype(BF16), w_ref[ka:, :]))


def out_proj(h, a, b, w):
    m, d = h.shape
    ka, kb = a.shape[1], b.shape[1]
    tm = _row_tile(m)
    return pl.pallas_call(
        _outproj_kernel,
        grid=(m // tm,),
        in_specs=[pl.BlockSpec((tm, d), lambda i: (i, 0)),
                  pl.BlockSpec((tm, ka), lambda i: (i, 0)),
                  pl.BlockSpec((tm, kb), lambda i: (i, 0)),
                  pl.BlockSpec((ka + kb, d), lambda i: (0, 0))],
        out_specs=pl.BlockSpec((tm, d), lambda i: (i, 0)),
        out_shape=jax.ShapeDtypeStruct((m, d), F32),
        compiler_params=_cparams(1),
        name="out_proj",
    )(h, a, b, w)


def _ffn_kernel(te_ref, tv_ref, idx_ref, x_ref, g_ref, gate_ref, w1_ref, w3_ref, w2_ref, o_ref,
                xn_ref, acc_ref, *gather_scratch, residual):
    i = pl.program_id(0)
    j = pl.program_id(1)

    if gather_scratch:
        xbuf, sem = gather_scratch
        tm = xbuf.shape[1]

        def issue(tile, slot, lo, hi):
            def start(k, carry):
                pltpu.make_async_copy(x_ref.at[idx_ref[tile * tm + k]], xbuf.at[slot, k], sem.at[slot]).start()
                return carry
            lax.fori_loop(lo, hi, start, 0)

        @pl.when((j == 0) & (i == 0))
        def _():
            issue(0, 0, 0, tm)

        @pl.when(j == 0)
        def _():
            slot = i % 2
            pltpu.make_async_copy(x_ref.at[pl.ds(0, tm)], xbuf.at[slot], sem.at[slot]).wait()
            x = jnp.concatenate([xbuf[slot, :, s, :] for s in range(xbuf.shape[2])], axis=1)
            xn_ref[...] = _rms(x, g_ref[...]).astype(BF16)
            acc_ref[...] = jnp.zeros_like(acc_ref)

        @pl.when(i + 1 < pl.num_programs(0))
        def _():
            share = -(-tm // pl.num_programs(1))
            issue(i + 1, 1 - i % 2, jnp.minimum(j * share, tm), jnp.minimum((j + 1) * share, tm))
    else:
        @pl.when(j == 0)
        def _():
            xn_ref[...] = _rms(x_ref[...], g_ref[...]).astype(BF16)
            acc_ref[...] = jnp.zeros_like(acc_ref)

    @pl.when(tv_ref[i] != 0)
    def _():
        xn = xn_ref[...]
        a = _dot(xn, w1_ref[...].astype(BF16))
        b = _dot(xn, w3_ref[...].astype(BF16))
        hid = (a * _sigmoid(a)) * b
        acc_ref[...] += _dot(hid.astype(BF16), w2_ref[...].astype(BF16))

    @pl.when(j == pl.num_programs(1) - 1)
    def _():
        y = acc_ref[...] * gate_ref[...]
        if residual:
            y = y + x_ref[...]
        o_ref[...] = y


def grouped_ffn(x, g, row_gate, tile_expert, tile_valid, w1, w3, w2, *, tm, tf, residual, row_idx=None):
    gather = row_idx is not None
    m = row_gate.shape[0]
    d = w1.shape[1]
    f = w1.shape[2]
    assert not (residual and gather)
    if gather:
        assert x.shape[1:] == (SUBLANES, d // SUBLANES)
        x_spec = pl.BlockSpec(memory_space=pl.ANY)
        scratch = [pltpu.VMEM((2, tm) + x.shape[1:], F32), pltpu.SemaphoreType.DMA((2,))]
    else:
        row_idx = jnp.zeros((1,), I32)
        x_spec = pl.BlockSpec((tm, d), lambda i, j, *_: (i, 0))
        scratch = []
    grid_spec = pltpu.PrefetchScalarGridSpec(
        num_scalar_prefetch=3,
        grid=(m // tm, f // tf),
        in_specs=[x_spec,
                  pl.BlockSpec((1, d), lambda i, j, *_: (0, 0)),
                  pl.BlockSpec((tm, 1), lambda i, j, *_: (i, 0)),
                  pl.BlockSpec((None, d, tf), lambda i, j, te, *_: (te[i], 0, j)),
                  pl.BlockSpec((None, d, tf), lambda i, j, te, *_: (te[i], 0, j)),
                  pl.BlockSpec((None, tf, d), lambda i, j, te, *_: (te[i], j, 0))],
        out_specs=pl.BlockSpec((tm, d), lambda i, j, *_: (i, 0)),
        scratch_shapes=[pltpu.VMEM((tm, d), BF16), pltpu.VMEM((tm, d), F32)] + scratch,
    )
    return pl.pallas_call(
        functools.partial(_ffn_kernel, residual=residual),
        grid_spec=grid_spec,
        out_shape=jax.ShapeDtypeStruct((m, d), F32),
        compiler_params=_cparams(2),
        name="grouped_ffn",
    )(tile_expert, tile_valid, row_idx, x, g.reshape(1, d), row_gate, w1, w3, w2)


def _ple_kernel(h_ref, g_ref, wg_ref, p_ref, wp_ref, gf_ref, o_ref, *, final):
    h = h_ref[...]
    gate = _sigmoid(_dot(_rms(h, g_ref[...]).astype(BF16), wg_ref[...]))
    proj = _dot(p_ref[...].astype(BF16), wp_ref[...])
    h2 = h + gate * proj
    o_ref[...] = _rms(h2, gf_ref[...]) if final else h2


def ple(h, g, wg, p, wp, gf, final):
    m, d = h.shape
    pd = p.shape[1]
    tm = _row_tile(m)
    row = lambda i: (i, 0)
    fixed = lambda i: (0, 0)
    return pl.pallas_call(
        functools.partial(_ple_kernel, final=final),
        grid=(m // tm,),
        in_specs=[pl.BlockSpec((tm, d), row), pl.BlockSpec((1, d), fixed),
                  pl.BlockSpec((d, d), fixed), pl.BlockSpec((tm, pd), row),
                  pl.BlockSpec((pd, d), fixed), pl.BlockSpec((1, d), fixed)],
        out_specs=pl.BlockSpec((tm, d), row),
        out_shape=jax.ShapeDtypeStruct((m, d), F32),
        compiler_params=_cparams(1),
        name="ple",
    )(h, g.reshape(1, d), wg, p, wp, gf.reshape(1, d))


def _bias_kernel(tab_ref, d_ref, o_ref):
    n = jnp.maximum(d_ref[...], 0)
    max_exact = N_BUCKETS // 2
    nf = jnp.maximum(n, 1).astype(F32)
    large = max_exact + (jnp.log(nf / max_exact) / math.log(MAX_DISTANCE / max_exact)
                         * (N_BUCKETS - max_exact)).astype(I32)
    large = jnp.minimum(large, N_BUCKETS - 1)
    bucket = jnp.where(n < max_exact, n, large)
    for h in range(H_NSA):
        acc = jnp.zeros(n.shape, F32)
        for k in range(N_BUCKETS):
            acc = jnp.where(bucket == k, tab_ref[k, h], acc)
        o_ref[h] = acc


def bias_lookup(table, dist):
    r, c = dist.shape
    tr = min(r, 128)
    tc = 512 if c % 512 == 0 else c
    return pl.pallas_call(
        _bias_kernel,
        grid=(r // tr, c // tc),
        in_specs=[pl.BlockSpec(memory_space=pltpu.SMEM),
                  pl.BlockSpec((tr, tc), lambda i, j: (i, j))],
        out_specs=pl.BlockSpec((H_NSA, tr, tc), lambda i, j: (0, i, j)),
        out_shape=jax.ShapeDtypeStruct((H_NSA, r, c), F32),
        compiler_params=_cparams(2),
        name="bias_lookup",
    )(table, dist)


def _swap_halves(x):
    lane = lax.broadcasted_iota(I32, x.shape, 1)
    return jnp.where(lane % HEAD_DIM < HEAD_DIM // 2,
                     pltpu.roll(x, LANES - HEAD_DIM // 2, 1), pltpu.roll(x, HEAD_DIM // 2, 1))


def _retention_prompt_kernel(q_ref, k_ref, v_ref, g_ref, cos_ref, sin_ref, gain_ref, intra_ref,
                             qdec_ref, kdec_ref, cdec_ref, y_ref, s_out_ref, s_ref):
    c = pl.program_id(1)

    @pl.when(c == 0)
    def _():
        s_ref[...] = jnp.zeros_like(s_ref)

    cos = cos_ref[...]
    sin = sin_ref[...]
    lane = lax.broadcasted_iota(I32, (RET_CHUNK, LANES), 1)
    left = lane < HEAD_DIM
    row = lax.broadcasted_iota(I32, (LANES, LANES), 0)
    col = lax.broadcasted_iota(I32, (LANES, LANES), 1)
    same_head = (row < HEAD_DIM) == (col < HEAD_DIM)
    ones_bd = jnp.where(same_head, 1.0, 0.0).astype(BF16)
    for p in range(H_RET // 2):
        sl = slice(p * LANES, (p + 1) * LANES)
        q = q_ref[:, sl]
        k = k_ref[:, sl]
        v = v_ref[:, sl].astype(BF16)
        qr = q * cos + _swap_halves(q) * sin
        kr = (k * cos + _swap_halves(k) * sin) * (HEAD_DIM ** -0.5)
        qb = qr.astype(BF16)
        q2 = jnp.concatenate([jnp.where(left, qr, 0.0), jnp.where(left, 0.0, qr)], axis=0).astype(BF16)
        inner = _dot_nt(q2, kr.astype(BF16)) * intra_ref[p]
        o2 = _dot(inner.astype(BF16), v)
        s_old = s_ref[p]
        cross = _dot(qb, s_old.astype(BF16)) * qdec_ref[:, sl]
        o = jnp.where(left, o2[:RET_CHUNK], o2[RET_CHUNK:]) + cross
        kd = (kr * kdec_ref[:, sl]).T.astype(BF16)
        s_ref[p] = s_old * cdec_ref[p] + jnp.where(same_head, _dot(kd, v), 0.0)
        ms = _dot3(o * o, ones_bd) * (1.0 / HEAD_DIM)
        gate = g_ref[:, sl]
        y = o * lax.rsqrt(ms + EPS) * gain_ref[:, sl] * (gate * _sigmoid(gate))
        y_ref[:, sl] = y.astype(y_ref.dtype)

    @pl.when(c == pl.num_programs(1) - 1)
    def _():
        for p in range(H_RET // 2):
            s = s_ref[p]
            s_out_ref[2 * p] = s[:HEAD_DIM, :HEAD_DIM]
            s_out_ref[2 * p + 1] = s[HEAD_DIM:, HEAD_DIM:]


def _retention_consts(c):
    log_g = jnp.log1p(-jnp.exp2(-5.0 - jnp.arange(H_RET, dtype=F32)))
    j = jnp.arange(c, dtype=F32)
    diff = j[:, None] - j[None, :]
    intra = jnp.where(diff >= 0, jnp.exp(jnp.maximum(diff, 0.0)[None] * log_g[:, None, None]), 0.0)
    q_decay = jnp.exp((j[:, None] + 1.0) * log_g[None, :])
    k_decay = jnp.exp((c - 1.0 - j[:, None]) * log_g[None, :])
    c_decay = jnp.exp(c * log_g)
    intra2 = intra.reshape(H_RET // 2, 2 * c, c)
    qdec = jnp.repeat(q_decay, HEAD_DIM, axis=1)
    kdec = jnp.repeat(k_decay, HEAD_DIM, axis=1)
    cdec = jnp.repeat(c_decay.reshape(H_RET // 2, 2), HEAD_DIM, axis=1)[:, :, None] * jnp.ones((1, 1, LANES), F32)
    return intra2, qdec, kdec, cdec


def _rotary_tables(pos):
    half = HEAD_DIM // 2
    inv = ROPE_BASE ** (-jnp.arange(half, dtype=F32) / half)
    ang = pos.astype(F32)[:, None] * inv[None, :]
    cos, sin = jnp.cos(ang), jnp.sin(ang)
    cos_t = jnp.concatenate([cos, cos, cos, cos], axis=1)
    sin_t = jnp.concatenate([-sin, sin, -sin, sin], axis=1)
    return cos_t, sin_t


def retention_prompt(z3, gain):
    b, t, _ = z3.shape
    c = RET_CHUNK
    cos_t, sin_t = _rotary_tables(jnp.arange(t))
    intra2, qdec, kdec, cdec = _retention_consts(c)
    blk = lambda col: pl.BlockSpec((None, c, RET_W), lambda bi, ci, col=col: (bi, ci, col))
    fixed2 = lambda shape: pl.BlockSpec(shape, lambda bi, ci: (0,) * len(shape))
    return pl.pallas_call(
        _retention_prompt_kernel,
        grid=(b, t // c),
        in_specs=[blk(COL_RQ // RET_W), blk(COL_RK // RET_W), blk(COL_RV // RET_W), blk(COL_RG // RET_W),
                  pl.BlockSpec((c, LANES), lambda bi, ci: (ci, 0)),
                  pl.BlockSpec((c, LANES), lambda bi, ci: (ci, 0)),
                  fixed2((1, RET_W)), fixed2((H_RET // 2, 2 * c, c)), fixed2((c, RET_W)),
                  fixed2((c, RET_W)), fixed2((H_RET // 2, LANES, LANES))],
        out_specs=[pl.BlockSpec((None, c, RET_W), lambda bi, ci: (bi, ci, 0)),
                   pl.BlockSpec((None, H_RET, HEAD_DIM, HEAD_DIM), lambda bi, ci: (bi, 0, 0, 0))],
        out_shape=[jax.ShapeDtypeStruct((b, t, RET_W), BF16),
                   jax.ShapeDtypeStruct((b, H_RET, HEAD_DIM, HEAD_DIM), F32)],
        scratch_shapes=[pltpu.VMEM((H_RET // 2, LANES, LANES), F32)],
        compiler_params=_cparams(2),
        name="retention_prompt",
    )(z3, z3, z3, z3, cos_t, sin_t, gain.reshape(1, RET_W), intra2, qdec, kdec, cdec)


def _retention_step_kernel(q_ref, k_ref, v_ref, g_ref, s_ref, cos_ref, sin_ref, gamma_ref, gain_ref,
                           y_ref, s_out_ref):
    half = HEAD_DIM // 2
    cos = cos_ref[...]
    sin = sin_ref[...]

    def rot(x):
        x1, x2 = x[:half], x[half:]
        return jnp.concatenate([x1 * cos - x2 * sin, x1 * sin + x2 * cos], axis=0)

    qr = rot(q_ref[...])
    kr = rot(k_ref[...]) * (HEAD_DIM ** -0.5)
    v = v_ref[...]
    gamma = gamma_ref[...]
    o = jnp.zeros_like(v)
    for d in range(HEAD_DIM):
        s_new = s_ref[d] * gamma + kr[d:d + 1] * v
        s_out_ref[d] = s_new
        o = o + qr[d:d + 1] * s_new
    ms = jnp.mean(o * o, axis=0, keepdims=True)
    gate = g_ref[...]
    y_ref[...] = o * lax.rsqrt(ms + EPS) * gain_ref[...] * (gate * _sigmoid(gate))


def retention_step(zt, state, layer, pos, gain):
    bs = zt.shape[1]
    half = HEAD_DIM // 2
    inv = ROPE_BASE ** (-jnp.arange(half, dtype=F32) / half)
    ang = jnp.full((1,), pos).astype(F32)[:, None] * inv[None, :]
    cos = jnp.broadcast_to(jnp.cos(ang).reshape(half, 1), (half, bs))
    sin = jnp.broadcast_to(jnp.sin(ang).reshape(half, 1), (half, bs))
    gamma = 1.0 - jnp.exp2(-5.0 - jnp.arange(H_RET, dtype=F32))
    gamma = jnp.broadcast_to(gamma.reshape(H_RET, 1, 1), (H_RET, 1, bs))
    gain_b = jnp.broadcast_to(gain.reshape(H_RET, HEAD_DIM, 1), (H_RET, HEAD_DIM, bs))
    head = lambda col: pl.BlockSpec((HEAD_DIM, bs), lambda h, col=col: (col // HEAD_DIM + h, 0))
    fixed = pl.BlockSpec((half, bs), lambda h: (0, 0))
    return pl.pallas_call(
        _retention_step_kernel,
        grid=(H_RET,),
        in_specs=[head(COL_RQ), head(COL_RK), head(COL_RV), head(COL_RG),
                  pl.BlockSpec((None, HEAD_DIM, HEAD_DIM, bs), lambda h: (layer * H_RET + h, 0, 0, 0)),
                  fixed, fixed,
                  pl.BlockSpec((None, 1, bs), lambda h: (h, 0, 0)),
                  pl.BlockSpec((None, HEAD_DIM, bs), lambda h: (h, 0, 0))],
        out_specs=[pl.BlockSpec((HEAD_DIM, bs), lambda h: (h, 0)),
                   pl.BlockSpec((None, HEAD_DIM, HEAD_DIM, bs), lambda h: (h, 0, 0, 0))],
        out_shape=[jax.ShapeDtypeStruct((RET_W, bs), F32),
                   jax.ShapeDtypeStruct((H_RET, HEAD_DIM, HEAD_DIM, bs), F32)],
        compiler_params=_cparams(1),
        name="retention_step",
    )(zt, zt, zt, zt, state, cos, sin, gamma, gain_b)


def _gelu_tanh(x):
    return 0.5 * x * (1.0 + jnp.tanh(math.sqrt(2.0 / math.pi) * (x + 0.044715 * (x * x * x))))


def _compress_kernel(*refs, n_pieces):
    x_refs = refs[1:1 + n_pieces]
    pe_ref, w1_ref, w2_ref, kc_ref, vc_ref, rows_ref = refs[1 + n_pieces:]
    width = x_refs[0].shape[1]
    n = n_pieces * width // CMP_STRIDE
    for kv, o_ref in enumerate((kc_ref, vc_ref)):
        for p, x_ref in enumerate(x_refs):
            for c in range(width // LANES):
                rows_ref[pl.ds(p * width + c * LANES, LANES), :] = (
                    x_ref[kv * LANES:(kv + 1) * LANES, c * LANES:(c + 1) * LANES].T)
        xk = jnp.concatenate([rows_ref[pl.ds(j, n, stride=CMP_STRIDE), :] for j in range(CMP_STRIDE)], axis=1)
        top = _dot((xk + pe_ref[kv, 0]).astype(BF16), w1_ref[kv, 0])
        bot = _dot((xk + pe_ref[kv, 1]).astype(BF16), w1_ref[kv, 1])
        pre = top + pltpu.roll(bot, n - 1, 0)
        hid = _gelu_tanh(pre).astype(BF16)
        o_ref[...] = jnp.concatenate(
            [_dot(hid[:, g * CMP_HIDDEN:(g + 1) * CMP_HIDDEN], w2_ref[kv]) for g in range(N_KV)], axis=1)


def _compress_weights(pe, w1, w2):
    pe_r = pe.reshape(2, 2, CMP_STRIDE, 1, HEAD_DIM)
    pe_t = jnp.broadcast_to(pe_r, (2, 2, CMP_STRIDE, N_KV, HEAD_DIM)).reshape(2, 2, 1, CMP_STRIDE * LANES)
    w1r = w1.reshape(2, 2, CMP_STRIDE, HEAD_DIM, CMP_HIDDEN)
    eye = jnp.eye(N_KV, dtype=w1.dtype)
    wbig = jnp.einsum('khjdc,gG->khjgdGc', w1r, eye)
    wbig = wbig.reshape(2, 2, CMP_STRIDE * LANES, N_KV * CMP_HIDDEN)
    return pe_t.astype(F32), wbig.astype(BF16), w2.astype(BF16)


def compress(src, idx, pe_t, w1b, w2b):
    bsz, n_pieces = idx.shape
    width = src.shape[2]
    n = n_pieces * width // CMP_STRIDE
    piece = lambda p: pl.BlockSpec((None, KV_W, width), lambda b, ix, p=p: (ix[b, p], 0, 0))
    fixed = lambda shape: pl.BlockSpec(shape, lambda b, ix: (0,) * len(shape))
    out = pl.BlockSpec((None, n, LANES), lambda b, ix: (b, 0, 0))
    grid_spec = pltpu.PrefetchScalarGridSpec(
        num_scalar_prefetch=1,
        grid=(bsz,),
        in_specs=[piece(p) for p in range(n_pieces)]
        + [fixed(pe_t.shape), fixed(w1b.shape), fixed(w2b.shape)],
        out_specs=[out, out],
        scratch_shapes=[pltpu.VMEM((n_pieces * width, LANES), F32)],
    )
    return pl.pallas_call(
        functools.partial(_compress_kernel, n_pieces=n_pieces),
        grid_spec=grid_spec,
        out_shape=[jax.ShapeDtypeStruct((bsz, n, LANES), F32)] * 2,
        compiler_params=_cparams(1),
        name="compress",
    )(idx, *([src] * n_pieces), pe_t, w1b, w2b)


def _masked_softmax(s, mask):
    s = jnp.where(mask, s, NEG)
    m = jnp.max(s, axis=-1, keepdims=True)
    e = jnp.where(mask, jnp.exp(s - m), 0.0)
    return e / jnp.maximum(jnp.sum(e, axis=-1, keepdims=True), TINY)


def _block_scores(imp, qpos, n_sel, axis=1):
    blk = lax.broadcasted_iota(I32, imp.shape, axis)
    cur = qpos // SEL_BLOCK
    forced = (blk == 0) | (blk == cur) | (blk == cur - 1)
    valid = blk * SEL_BLOCK <= qpos
    score = jnp.where(forced, FORCED_SCORE, jnp.where(valid, imp, -1.0))
    return jnp.where(blk < n_sel, score, -2.0)


def _select_top(score, n_top, axis=1):
    blk = lax.broadcasted_iota(I32, score.shape, axis)
    rank = jnp.zeros(score.shape, F32)
    for i in range(score.shape[axis]):
        one = score[:, i:i + 1] if axis == 1 else score[i:i + 1, :]
        beats = (one > score) | ((one == score) & (blk > i))
        rank = rank + jnp.where(beats, 1.0, 0.0)
    return jnp.where(rank < n_top, 1.0, 0.0)


def _attend_t(s, ok, m_prev, l_prev):
    s = jnp.where(ok, s, NEG)
    m_new = jnp.maximum(m_prev, jnp.max(s, axis=0, keepdims=True))
    alpha = jnp.exp(m_prev - m_new)
    p = jnp.exp(s - m_new)
    l_new = alpha * l_prev + jnp.sum(p, axis=0, keepdims=True)
    return m_new, l_new, alpha, p


def _nsa_prompt_kernel(q_ref, gate_ref, kc_ref, vct_ref, ks_ref, vst_ref, kw_ref, vwt_ref, bc_ref, bt_ref,
                       o_ref, qt_ref, pcs_ref, pick_ref, acc_ref, os_ref, s_ref, p_ref, *, n_cmp, n_sel):
    i = pl.program_id(1)
    qs = i * Q_BLOCK
    key_i = lax.broadcasted_iota(I32, (KV_CHUNK, Q_BLOCK), 0)
    tok_i = lax.broadcasted_iota(I32, (KV_CHUNK, Q_BLOCK), 1)
    row8 = lax.broadcasted_iota(I32, (H_NSA, Q_BLOCK), 0)
    zeros_half = jnp.zeros((HEAD_DIM, Q_BLOCK), F32)

    for k2 in range(H_NSA // 2):
        slab_t = q_ref[:, k2 * LANES:(k2 + 1) * LANES].T * (HEAD_DIM ** -0.5)
        for h in (2 * k2, 2 * k2 + 1):
            qh = slab_t[(h % 2) * HEAD_DIM:(h % 2 + 1) * HEAD_DIM]
            both = [qh, zeros_half] if h // R_GRP == 0 else [zeros_half, qh]
            qt_ref[:, h * Q_BLOCK:(h + 1) * Q_BLOCK] = jnp.concatenate(both, axis=0).astype(BF16)

    sig_t = _sigmoid(gate_ref[...]).T

    ncp = kc_ref.shape[0]
    kc = kc_ref[...].astype(BF16)
    n_c = lax.broadcasted_iota(I32, (ncp, Q_BLOCK), 0)
    t_c = lax.broadcasted_iota(I32, (ncp, Q_BLOCK), 1)
    ok_c = (n_c * CMP_STRIDE + (L_CMP - 1) <= qs + t_c) & (n_c < n_cmp)
    blk_pos = qs + lax.broadcasted_iota(I32, (pick_ref.shape[1], Q_BLOCK), 1)
    o_c = []
    for g in range(N_KV):
        vct = vct_ref[g * HEAD_DIM:(g + 1) * HEAD_DIM, :].astype(BF16)
        pcs = jnp.zeros((ncp, Q_BLOCK), F32)
        for r in range(R_GRP):
            h = g * R_GRP + r
            s = _dot(kc, qt_ref[:, h * Q_BLOCK:(h + 1) * Q_BLOCK]) + bc_ref[h]
            s = jnp.where(ok_c, s, NEG)
            e = jnp.where(ok_c, jnp.exp(s - jnp.max(s, axis=0, keepdims=True)), 0.0)
            p = e * (1.0 / jnp.maximum(jnp.sum(e, axis=0, keepdims=True), TINY))
            o_c.append(_dot(vct, p.astype(BF16)))
            pcs = pcs + p
        pcs_ref[...] = pcs
        per = SEL_BLOCK // CMP_STRIDE
        imp = pcs_ref[pl.ds(0, ncp // per, stride=per), :]
        for c in range(1, per):
            imp = imp + pcs_ref[pl.ds(c, ncp // per, stride=per), :]
        pick_ref[g] = _select_top(_block_scores(imp, blk_pos, n_sel, axis=0), min(N_SELECT, n_sel), axis=0)

    dmat = tok_i - key_i

    def stream(k_ref, vt_ref, n_chunks, chunk_of, mask_of):
        def scores(c, slot):
            off, _ = chunk_of(c)
            k = k_ref[pl.ds(off, KV_CHUNK), :].astype(BF16)
            for h in range(H_NSA):
                s_ref[slot, h] = _dot(k, qt_ref[:, h * Q_BLOCK:(h + 1) * Q_BLOCK])

        def values(c, slot, alpha_all):
            off, _ = chunk_of(c)
            for g in range(N_KV):
                vt = vt_ref[g * HEAD_DIM:(g + 1) * HEAD_DIM, pl.ds(off, KV_CHUNK)].astype(BF16)
                for h in range(g * R_GRP, (g + 1) * R_GRP):
                    acc_ref[h] = acc_ref[h] * alpha_all[h:h + 1] + _dot(vt, p_ref[slot, h])

        def softmax(c, slot, m_all, l_all):
            _, rel = chunk_of(c)
            ok_g = mask_of(c, rel)
            bias_idx = min(max(rel, 0), FAR_REL) if isinstance(rel, int) else jnp.clip(rel, 0, FAR_REL)
            alpha_all = jnp.zeros((H_NSA, Q_BLOCK), F32)
            for h in range(H_NSA):
                s = s_ref[slot, h] + bt_ref[bias_idx, h]
                m_new, l_new, alpha, p = _attend_t(s, ok_g[h // R_GRP], m_all[h:h + 1], l_all[h:h + 1])
                p_ref[slot, h] = p.astype(BF16)
                m_all = jnp.where(row8 == h, m_new, m_all)
                l_all = jnp.where(row8 == h, l_new, l_all)
                alpha_all = jnp.where(row8 == h, alpha, alpha_all)
            return m_all, l_all, alpha_all

        def stage(c, slot, carry, prefetch=True):
            m_all, l_all, alpha_prev = carry
            if prefetch:
                scores(c + 1, 1 - slot)
            values(c - 1, 1 - slot, alpha_prev)
            return softmax(c, slot, m_all, l_all)

        acc_ref[...] = jnp.zeros_like(acc_ref)
        p_ref[1] = jnp.zeros(p_ref.shape[1:], BF16)
        scores(0, 0)
        carry = (jnp.full((H_NSA, Q_BLOCK), NEG, F32), jnp.zeros((H_NSA, Q_BLOCK), F32),
                 jnp.ones((H_NSA, Q_BLOCK), F32))
        if isinstance(n_chunks, int):
            for c in range(n_chunks):
                carry = stage(c, c % 2, carry, prefetch=c + 1 < n_chunks)
            last = n_chunks - 1
        else:
            pairs = (n_chunks + 1) // 2
            carry = lax.fori_loop(0, pairs, lambda t, cr: stage(2 * t + 1, 1, stage(2 * t, 0, cr)), carry)
            last = 2 * pairs - 1
        m_all, l_all, alpha_last = carry
        values(last, last % 2 if isinstance(last, int) else 1, alpha_last)
        return m_all, l_all

    def sel_chunk(c):
        return pl.multiple_of(jnp.clip(c, 0, i) * KV_CHUNK, KV_CHUNK), i - c

    def sel_mask(c, rel):
        causal = rel * KV_CHUNK + dmat >= 0
        per = KV_CHUNK // SEL_BLOCK
        first = jnp.minimum(c, i) * per
        ok_g = []
        for g in range(N_KV):
            picked = jnp.concatenate(
                [jnp.broadcast_to(pick_ref[g, pl.ds(first + b, 1), :], (SEL_BLOCK, Q_BLOCK)) for b in range(per)],
                axis=0)
            ok_g.append(causal & (picked > 0.5))
        return ok_g

    m_s, l_s = stream(ks_ref, vst_ref, i + 1, sel_chunk, sel_mask)
    for h in range(H_NSA):
        os_ref[h] = jnp.where(m_s[h:h + 1] > 0.5 * NEG, acc_ref[h] / jnp.maximum(l_s[h:h + 1], TINY), 0.0)

    n_back = WINDOW // KV_CHUNK

    def win_chunk(c):
        return pl.multiple_of(jnp.clip(i - n_back + c, 0, i) * KV_CHUNK, KV_CHUNK), n_back - c

    def win_mask(c, rel):
        dist = rel * KV_CHUNK + dmat + jnp.where(i - n_back + c < 0, WINDOW, 0)
        ok = (dist >= 0) & (dist < WINDOW)
        return [ok, ok]

    m_w, l_w = stream(kw_ref, vwt_ref, n_back + 1, win_chunk, win_mask)

    for k2 in range(H_NSA // 2):
        pair = []
        for h in (2 * k2, 2 * k2 + 1):
            o_w = jnp.where(m_w[h:h + 1] > 0.5 * NEG, acc_ref[h] / jnp.maximum(l_w[h:h + 1], TINY), 0.0)
            pair.append(sig_t[3 * h:3 * h + 1] * o_c[h] + sig_t[3 * h + 1:3 * h + 2] * os_ref[h]
                        + sig_t[3 * h + 2:3 * h + 3] * o_w)
        o_ref[:, k2 * LANES:(k2 + 1) * LANES] = jnp.concatenate(pair, axis=0).T.astype(o_ref.dtype)


def _selection_consts(n_rows_cmp, n_keys):
    n = np.arange(n_rows_cmp)
    blk = np.arange(SEL_LANES)
    pool = (n[:, None] // (SEL_BLOCK // CMP_STRIDE) == blk[None, :]).astype(np.float32)
    key = np.arange(n_keys)
    expand = (blk[:, None] == key[None, :] // SEL_BLOCK).astype(np.float32)
    return jnp.asarray(pool, BF16), jnp.asarray(expand, BF16)


def nsa_prompt(z3, kvt, kc, vct, bias_ct, bias_tt):
    b, t, _ = z3.shape
    n_sub = t // CMP_STRIDE
    n_sel = -(-t // SEL_BLOCK)
    assert kc.shape[1] == n_sub and n_sub == n_sel * (SEL_BLOCK // CMP_STRIDE) and n_sel % SUBLANES == 0
    fixed = lambda shape: pl.BlockSpec(shape, lambda bi, i: (0,) * len(shape))
    v_rows = lambda branch: (branch * KV_W + N_KV * HEAD_DIM) // LANES
    return pl.pallas_call(
        functools.partial(_nsa_prompt_kernel, n_cmp=n_sub - 1, n_sel=n_sel),
        grid=(b, t // Q_BLOCK),
        in_specs=[pl.BlockSpec((None, Q_BLOCK, NSA_QW), lambda bi, i: (bi, i, COL_NQ // NSA_QW)),
                  pl.BlockSpec((None, Q_BLOCK, LANES), lambda bi, i: (bi, i, COL_NG // LANES)),
                  pl.BlockSpec((None, n_sub, LANES), lambda bi, i: (bi, 0, 0)),
                  pl.BlockSpec((None, LANES, n_sub), lambda bi, i: (bi, 0, 0)),
                  pl.BlockSpec((None, t, LANES), lambda bi, i: (bi, 0, COL_KVS // LANES)),
                  pl.BlockSpec((None, LANES, t), lambda bi, i: (bi, v_rows(1), 0)),
                  pl.BlockSpec((None, t, LANES), lambda bi, i: (bi, 0, COL_KVW // LANES)),
                  pl.BlockSpec((None, LANES, t), lambda bi, i: (bi, v_rows(2), 0)),
                  pl.BlockSpec((H_NSA, n_sub, Q_BLOCK), lambda bi, i: (0, 0, i)),
                  fixed(bias_tt.shape)],
        out_specs=pl.BlockSpec((None, Q_BLOCK, NSA_QW), lambda bi, i: (bi, i, 0)),
        out_shape=jax.ShapeDtypeStruct((b, t, NSA_QW), BF16),
        scratch_shapes=[pltpu.VMEM((LANES, H_NSA * Q_BLOCK), BF16),
                        pltpu.VMEM((n_sub, Q_BLOCK), F32),
                        pltpu.VMEM((N_KV, n_sel, Q_BLOCK), F32),
                        pltpu.VMEM((H_NSA, HEAD_DIM, Q_BLOCK), F32),
                        pltpu.VMEM((H_NSA, HEAD_DIM, Q_BLOCK), F32),
                        pltpu.VMEM((2, H_NSA, KV_CHUNK, Q_BLOCK), F32),
                        pltpu.VMEM((2, H_NSA, KV_CHUNK, Q_BLOCK), BF16)],
        compiler_params=_cparams(2),
        name="nsa_prompt",
    )(z3, z3, kc, vct, z3, kvt, z3, kvt, bias_ct, bias_tt)


def _nsa_step_kernel(pt_ref, z_ref, kc_ref, vc_ref, *rest, n_pages, per_step, **static):
    del pt_ref
    pages = rest[:per_step * n_pages]
    win_ref, bcs_ref, bss_ref, bws_ref, pool_ref, exp_ref, o_ref, wo_ref = rest[per_step * n_pages:]
    for u in range(per_step):
        _nsa_step_one(z_ref.at[u], kc_ref.at[u], vc_ref.at[u], pages[u * n_pages:(u + 1) * n_pages], win_ref.at[u],
                      bcs_ref, bss_ref, bws_ref, pool_ref, exp_ref, o_ref.at[u], wo_ref.at[u], **static)


def _nsa_step_one(z_ref, kc_ref, vc_ref, pages, win_ref, bcs_ref, bss_ref, bws_ref, pool_ref, exp_ref, o_ref, wo_ref,
                  *, past, wbuf, n_cmp, n_sel):
    z = z_ref[...]
    row8 = lax.broadcasted_iota(I32, (H_NSA, LANES), 0)
    lane8 = lax.broadcasted_iota(I32, (H_NSA, LANES), 1)
    left8 = lane8 < HEAD_DIM
    qbd = jnp.zeros((H_NSA, LANES), F32)
    for h in range(H_NSA):
        slab = jnp.broadcast_to(z[:, COL_NQ + (h // 2) * LANES:COL_NQ + (h // 2 + 1) * LANES], (H_NSA, LANES))
        g = h // R_GRP
        if h % 2 != g:
            slab = pltpu.roll(slab, HEAD_DIM, 1)
        keep = (row8 == h) & (left8 if g == 0 else jnp.logical_not(left8))
        qbd = jnp.where(keep, slab, qbd)
    qbd = (qbd * (HEAD_DIM ** -0.5)).astype(BF16)

    ncp = kc_ref.shape[0]
    n_c = lax.broadcasted_iota(I32, (H_NSA, ncp), 1)
    s_c = _dot_nt(qbd, kc_ref[...].astype(BF16)) + bcs_ref[...]
    p_c = _masked_softmax(s_c, (n_c * CMP_STRIDE + (L_CMP - 1) <= past) & (n_c < n_cmp))
    o_c = _dot(p_c.astype(BF16), vc_ref[...].astype(BF16))
    pc0 = jnp.sum(p_c[0:R_GRP], axis=0, keepdims=True)
    pc1 = jnp.sum(p_c[R_GRP:], axis=0, keepdims=True)
    rowc = lax.broadcasted_iota(I32, (H_NSA, ncp), 0)
    pcs = jnp.where(rowc < R_GRP, jnp.broadcast_to(pc0, (H_NSA, ncp)), jnp.broadcast_to(pc1, (H_NSA, ncp)))
    imp = _dot3(pcs, pool_ref[...])
    sel = _select_top(_block_scores(imp, jnp.full(imp.shape, past, I32), n_sel), min(N_SELECT, n_sel))
    picked = _dot(sel.astype(BF16), exp_ref[...])

    c128 = lax.broadcasted_iota(I32, (LANES, KV_CHUNK), 1)

    def new_col(col):
        return jnp.broadcast_to(z[:, col:col + LANES], (KV_CHUNK, LANES)).T

    def first_col(tile):
        return jnp.where(c128 == 0, tile, 0.0).astype(BF16)

    ks = [pg[0:LANES, :].astype(BF16) for pg in pages] + [first_col(new_col(COL_KVS))]
    vs = [pg[LANES:2 * LANES, :].astype(BF16) for pg in pages] + [first_col(new_col(COL_KVS + LANES))]
    s_s = jnp.concatenate([_dot(qbd, k) for k in ks], axis=1) + bss_ref[...]
    key = lax.broadcasted_iota(I32, s_s.shape, 1)
    p_s = _masked_softmax(s_s, (picked > 0.5) & (key <= past)).astype(BF16)
    o_s = _dot_nt(p_s[:, 0:KV_CHUNK], vs[0])
    for c in range(1, len(vs)):
        o_s = o_s + _dot_nt(p_s[:, c * KV_CHUNK:(c + 1) * KV_CHUNK], vs[c])

    kw_new, vw_new = new_col(COL_KVW), new_col(COL_KVW + LANES)
    win = win_ref[...]
    s_w = jnp.concatenate([_dot(qbd, win[0:LANES].astype(BF16)), _dot(qbd, first_col(kw_new))], axis=1) + bws_ref[...]
    colw = lax.broadcasted_iota(I32, s_w.shape, 1)
    p_w = _masked_softmax(s_w, (colw <= wbuf) & (wbuf - colw < WINDOW)).astype(BF16)
    o_w = _dot_nt(p_w[:, 0:wbuf], win[LANES:].astype(BF16)) + _dot_nt(p_w[:, wbuf:], first_col(vw_new))

    shifted = pltpu.roll(win, wbuf - 1, 1)
    new_kv = jnp.concatenate([kw_new, vw_new], axis=0)
    last = jnp.where(lax.broadcasted_iota(I32, (KV_W, LANES), 1) == LANES - 1, new_kv, shifted[:, wbuf - LANES:])
    wo_ref[:, 0:wbuf - LANES] = shifted[:, 0:wbuf - LANES]
    wo_ref[:, wbuf - LANES:] = last

    sig = jnp.broadcast_to(_sigmoid(z[:, COL_NG:COL_NG + LANES]), (H_NSA, LANES))

    def gate_col(c):
        return jnp.sum(jnp.where(lane8 == row8 * 3 + c, sig, 0.0), axis=1, keepdims=True)

    o = gate_col(0) * o_c + gate_col(1) * o_s + gate_col(2) * o_w
    fix = jnp.where((row8 % 2) != (row8 // R_GRP), pltpu.roll(o, HEAD_DIM, 1), o)
    left1 = left8[0:1]
    o_ref[...] = jnp.concatenate(
        [jnp.where(left1, fix[2 * k2:2 * k2 + 1], fix[2 * k2 + 1:2 * k2 + 2]) for k2 in range(H_NSA // 2)], axis=1)


def nsa_step(zs3, kc, vc, sel_cache, win_cache, page_idx, win_row0, bias_cs, bias_ss, bias_ws, past):
    bs = zs3.shape[0]
    n_pages = page_idx.shape[1]
    wbuf = win_cache.shape[2]
    n_sub = (past + 1) // CMP_STRIDE
    n_sel = -(-(past + 1) // SEL_BLOCK)
    n_keys = (n_pages + 1) * PAGE_SIZE
    assert n_sel <= SEL_LANES and PAGE_SIZE == KV_CHUNK and wbuf % KV_CHUNK == 0 and kc.shape[1] == n_sub
    pool, expand = _selection_consts(n_sub, n_keys)
    per = 2 if bs % 2 == 0 and win_row0 % 2 == 0 else 1
    fixed = lambda shape: pl.BlockSpec(shape, lambda b, pt: (0,) * len(shape))
    page = lambda u, p: pl.BlockSpec((None, KV_W, PAGE_SIZE), lambda b, pt, u=u, p=p: (pt[per * b + u, p], 0, 0))
    grid_spec = pltpu.PrefetchScalarGridSpec(
        num_scalar_prefetch=1,
        grid=(bs // per,),
        in_specs=[pl.BlockSpec((per, 1, N_IN_PAD), lambda b, pt: (b, 0, 0)),
                  pl.BlockSpec((per, n_sub, LANES), lambda b, pt: (b, 0, 0)),
                  pl.BlockSpec((per, n_sub, LANES), lambda b, pt: (b, 0, 0))]
        + [page(u, p) for u in range(per) for p in range(n_pages)]
        + [pl.BlockSpec((per, KV_W, wbuf), lambda b, pt: (win_row0 // per + b, 0, 0)),
           fixed(bias_cs.shape), fixed(bias_ss.shape), fixed(bias_ws.shape), fixed(pool.shape), fixed(expand.shape)],
        out_specs=[pl.BlockSpec((per, 1, NSA_QW), lambda b, pt: (b, 0, 0)),
                   pl.BlockSpec((per, KV_W, wbuf), lambda b, pt: (b, 0, 0))],
    )
    return pl.pallas_call(
        functools.partial(_nsa_step_kernel, n_pages=n_pages, per_step=per, past=past, wbuf=wbuf, n_cmp=n_sub - 1,
                          n_sel=n_sel),
        grid_spec=grid_spec,
        out_shape=[jax.ShapeDtypeStruct((bs, 1, NSA_QW), F32), jax.ShapeDtypeStruct((bs, KV_W, wbuf), F32)],
        compiler_params=_cparams(1),
        name="nsa_step",
    )(page_idx, zs3, kc, vc, *([sel_cache] * (per * n_pages)), win_cache, bias_cs, bias_ss, bias_ws, pool, expand)


def _route(logits, tm):
    m = logits.shape[0]
    top_v, top_i = lax.top_k(logits, TOP_K)
    top_w = jax.nn.softmax(top_v, axis=-1)
    slot_e = top_i.reshape(-1)
    onehot = (slot_e[:, None] == jnp.arange(N_EXPERTS)[None, :]).astype(I32)
    before = jnp.cumsum(onehot, axis=0) - onehot
    rank = jnp.sum(before * onehot, axis=1)
    count = jnp.sum(onehot, axis=0)
    padded = -(-count // tm) * tm
    start = jnp.cumsum(padded) - padded
    pos = start[slot_e] + rank
    n_rows = (-(-(m * TOP_K) // tm) + N_EXPERTS) * tm
    row_slot = jnp.full((n_rows,), -1, I32).at[pos].set(jnp.arange(m * TOP_K, dtype=I32))
    slot = jnp.maximum(row_slot, 0)
    row_token = slot // TOP_K
    row_gate = jnp.where(row_slot >= 0, top_w.reshape(-1)[slot], 0.0)
    tile_start = jnp.arange(n_rows // tm, dtype=I32) * tm
    tile_expert = jnp.clip(jnp.searchsorted(jnp.cumsum(padded), tile_start, side='right'), 0, N_EXPERTS - 1)
    tile_valid = (tile_start < jnp.sum(padded)).astype(I32)
    return pos.reshape(m, TOP_K), row_token, row_gate.reshape(n_rows, 1), tile_expert.astype(I32), tile_valid


def _dense_ffn(h, g, w1, w3, w2):
    m = h.shape[0]
    tm = _row_tile(m)
    ones = jnp.ones((m // tm,), I32)
    return grouped_ffn(h, g, jnp.ones((m, 1), F32), jnp.zeros((m // tm,), I32), ones,
                       w1[None], w3[None], w2[None], tm=tm, tf=w1.shape[1] // 2, residual=True)


def kernel(x_prompt, x_sample, p_prompt, p_sample, state_ret, cache_win_kv, cache_cmp_kv, cache_sel_kv, page_table, rel_bias, norm_mix, w_in, ret_gain, cmp_pe, cmp_w1, cmp_w2, w_out, norm_ffn, ffn_w1, ffn_w3, ffn_w2, router, moe_w1, moe_w3, moe_w2, ple_norm, ple_gate, ple_proj, norm_final):
    b_p, t_p, d = x_prompt.shape
    b_s = x_sample.shape[0]
    depth = w_in.shape[0]
    n_pages = page_table.shape[1]
    past = n_pages * PAGE_SIZE
    n_pool = cache_cmp_kv.shape[1]
    w_buf = cache_win_kv.shape[2]
    assert x_sample.shape[1] == 1 and t_p % Q_BLOCK == 0 and t_p >= w_buf and past % CMP_STRIDE == 0
    row = (2, N_KV, HEAD_DIM)
    m_p = b_p * t_p

    ar = lambda n: jnp.arange(n, dtype=I32)
    n_sub_p = t_p // CMP_STRIDE
    bias_ct = bias_lookup(rel_bias, ar(t_p)[None, :] - (ar(n_sub_p)[:, None] * CMP_STRIDE + L_CMP - 1))
    rel_t = (ar(FAR_REL + 1)[:, None, None] * KV_CHUNK + ar(Q_BLOCK)[None, None, :] - ar(KV_CHUNK)[None, :, None])
    bias_tt = bias_lookup(rel_bias, rel_t.reshape((FAR_REL + 1) * KV_CHUNK, Q_BLOCK))
    bias_tt = bias_tt.reshape(H_NSA, FAR_REL + 1, KV_CHUNK, Q_BLOCK).transpose(1, 0, 2, 3)
    n_sub_s = (past + 1) // CMP_STRIDE
    rows8 = jnp.zeros((SUBLANES, 1), I32)
    bias_cs = bias_lookup(rel_bias, rows8 + (past - (ar(n_sub_s)[None, :] * CMP_STRIDE + L_CMP - 1)))[:, 0]
    bias_ss = bias_lookup(rel_bias, rows8 + (past - ar((n_pages + 1) * PAGE_SIZE)[None, :]))[:, 0]
    bias_ws = bias_lookup(rel_bias, rows8 + (w_buf - ar(w_buf + KV_CHUNK)[None, :]))[:, 0]

    def rows_last(x):
        return jnp.moveaxis(x, 2, -1).reshape(x.shape[0] * x.shape[1], KV_W, x.shape[2])

    def rows_first(xt, lead):
        return jnp.moveaxis(xt.reshape((lead,) + row + (xt.shape[-1],)), -1, 1)

    cmp_cache = rows_last(cache_cmp_kv)
    sel_cache = rows_last(cache_sel_kv)
    win_cache = rows_last(cache_win_kv)
    state_t = jnp.moveaxis(state_ret, 1, -1).reshape(depth * H_RET, HEAD_DIM, HEAD_DIM, b_s)

    h_p = x_prompt.reshape(m_p, d)
    h_s = x_sample.reshape(b_s, d)
    rp, rs, wp, ws, cp, cs, sp, ss = [], [], [], [], [], [], [], []
    for i in range(depth):
        w_in_b = jnp.pad(w_in[i], ((0, 0), (0, N_IN_PAD - N_IN))).astype(BF16)
        w_out_b = w_out[i].astype(BF16)
        pe_t, cw1, cw2 = _compress_weights(cmp_pe[i], cmp_w1[i], cmp_w2[i])

        w_kvt = w_in[i][:, COL_KVC:COL_KVC + 3 * KV_W].T.astype(BF16)
        z, kvt = rms_matmul_kvt(h_p, norm_mix[i], w_in_b, w_kvt, b_p)
        z3 = z.reshape(b_p, t_p, N_IN_PAD)
        ret_y, ret_state = retention_prompt(z3, ret_gain[i])
        kc, vc = compress(kvt, ar(b_p)[:, None], pe_t, cw1, cw2)
        nsa_y = nsa_prompt(z3, kvt, kc, jnp.swapaxes(vc, 1, 2), bias_ct, bias_tt)
        h_p = out_proj(h_p, ret_y.reshape(m_p, RET_W), nsa_y.reshape(m_p, NSA_QW), w_out_b)
        rp.append(ret_state)
        cp.append(rows_first(kvt[:, 0:KV_W], b_p))
        sp.append(rows_first(kvt[:, KV_W:2 * KV_W], b_p))
        wp.append(rows_first(kvt[:, 2 * KV_W:, t_p - w_buf:], b_p))

        zs = rms_matmul(h_s, norm_mix[i], w_in_b)
        zst = zs.T
        ret_yst, state_s = retention_step(zst, state_t, i, past, ret_gain[i])
        page_idx = page_table + i * n_pool
        grp = 4 if b_s % 4 == 0 else 1
        kc_s, vc_s = compress(cmp_cache, page_idx.reshape(b_s // grp, grp * n_pages), pe_t, cw1, cw2)
        kc_s, vc_s = kc_s.reshape(b_s, n_sub_s, LANES), vc_s.reshape(b_s, n_sub_s, LANES)
        nsa_ys, win_s = nsa_step(zs.reshape(b_s, 1, N_IN_PAD), kc_s, vc_s, sel_cache, win_cache, page_idx, i * b_s,
                                 bias_cs, bias_ss, bias_ws, past)
        h_s = out_proj(h_s, ret_yst.T, nsa_ys.reshape(b_s, NSA_QW), w_out_b)
        rs.append(jnp.moveaxis(state_s, -1, 0))
        ws.append(rows_first(win_s, b_s))
        cs.append(rows_first(zst[None, COL_KVC:COL_KVC + KV_W], 1).reshape((b_s, 1) + row))
        ss.append(rows_first(zst[None, COL_KVS:COL_KVS + KV_W], 1).reshape((b_s, 1) + row))

        j = i // 2
        if i % 2 == 0:
            w1, w3, w2 = ffn_w1[j].astype(BF16), ffn_w3[j].astype(BF16), ffn_w2[j].astype(BF16)
            h_p = _dense_ffn(h_p, norm_ffn[i], w1, w3, w2)
            h_s = _dense_ffn(h_s, norm_ffn[i], w1, w3, w2)
        else:
            router_w = jnp.pad(router[j], ((0, 0), (0, LANES - N_EXPERTS)))
            tf = moe_w1.shape[3] // 7
            logits_p = router_logits(h_p, norm_ffn[i], router_w)
            tm = 1024
            pos, row_token, row_gate, tile_expert, tile_valid = _route(logits_p[:, :N_EXPERTS], tm)
            y_rows = grouped_ffn(h_p.reshape(m_p, SUBLANES, d // SUBLANES), norm_ffn[i], row_gate, tile_expert,
                                 tile_valid, moe_w1[j], moe_w3[j], moe_w2[j], tm=tm, tf=tf, residual=False,
                                 row_idx=row_token)
            h_p = h_p + y_rows[pos[:, 0]] + y_rows[pos[:, 1]]
            logits_s = router_logits(h_s, norm_ffn[i], router_w)
            top_v, top_i = lax.top_k(logits_s[:, :N_EXPERTS], TOP_K)
            top_w = jax.nn.softmax(top_v, axis=-1)
            gates = jnp.sum(jax.nn.one_hot(top_i, N_EXPERTS, dtype=F32) * top_w[..., None], axis=-2)
            y_e = grouped_ffn(jnp.tile(h_s, (N_EXPERTS, 1)), norm_ffn[i], gates.T.reshape(N_EXPERTS * b_s, 1),
                              jnp.arange(N_EXPERTS, dtype=I32), jnp.ones((N_EXPERTS,), I32),
                              moe_w1[j], moe_w3[j], moe_w2[j], tm=b_s, tf=tf, residual=False)
            h_s = h_s + jnp.sum(y_e.reshape(N_EXPERTS, b_s, d), axis=0)

        wg, wpj = ple_gate[i].astype(BF16), ple_proj[i].astype(BF16)
        h_p = ple(h_p, ple_norm[i], wg, p_prompt[i].reshape(m_p, P_DIM), wpj, norm_final, i == depth - 1)
        h_s = ple(h_s, ple_norm[i], wg, p_sample[i].reshape(b_s, P_DIM), wpj, norm_final, i == depth - 1)

    return (h_p.reshape(b_p, t_p, d), h_s.reshape(b_s, 1, d), jnp.stack(rp), jnp.stack(rs), jnp.stack(wp),
            jnp.stack(ws), jnp.stack(cp), jnp.stack(cs), jnp.stack(sp), jnp.stack(ss))
```

```python
import functools
import math

import numpy as np
import jax
import jax.numpy as jnp
from jax import lax
from jax.experimental import pallas as pl
from jax.experimental.pallas import tpu as pltpu

F32 = jnp.float32
BF16 = jnp.bfloat16
I32 = jnp.int32

LANES = 128
SUBLANES = 8
VMEM_LIMIT_BYTES = 56 * 1024 * 1024

D_MODEL = 1024
HEAD_DIM = 64
H_RET = 8
H_NSA = 8
N_KV = 2
R_GRP = H_NSA // N_KV
RET_W = H_RET * HEAD_DIM
NSA_QW = H_NSA * HEAD_DIM
KV_W = 2 * N_KV * HEAD_DIM
N_IN = 4 * RET_W + NSA_QW + 3 * KV_W + 3 * H_NSA
N_IN_PAD = 3584
COL_RQ, COL_RK, COL_RV, COL_RG = 0, RET_W, 2 * RET_W, 3 * RET_W
COL_NQ = 4 * RET_W
COL_KVC = COL_NQ + NSA_QW
COL_KVS = COL_KVC + KV_W
COL_KVW = COL_KVS + KV_W
COL_NG = COL_KVW + KV_W
RET_CHUNK = 128
CMP_STRIDE = 16
L_CMP = 2 * CMP_STRIDE
CMP_HIDDEN = 2 * HEAD_DIM
SEL_BLOCK = 64
N_SELECT = 16
WINDOW = 512
Q_BLOCK = 128
N_BUCKETS = 32
MAX_DISTANCE = 128
N_EXPERTS = 8
TOP_K = 2
P_DIM = 256
PAGE_SIZE = 128
ROPE_BASE = 10000.0
EPS = 1e-6
FORCED_SCORE = 1e4
NEG = -1e30
TINY = float(np.finfo(np.float32).tiny)
SEL_LANES = 64
FAR_REL = 2
KV_CHUNK = 128


def _cparams(n_grid):
    return pltpu.CompilerParams(dimension_semantics=("arbitrary",) * n_grid,
                                vmem_limit_bytes=VMEM_LIMIT_BYTES)


def _dot(a, b):
    return jnp.dot(a, b, preferred_element_type=F32)


def _dot_nt(a, b):
    return lax.dot_general(a, b, (((1,), (1,)), ((), ())), preferred_element_type=F32)


def _split3(x):
    hi = x.astype(BF16)
    r = x - hi.astype(F32)
    mid = r.astype(BF16)
    lo = (r - mid.astype(F32)).astype(BF16)
    return hi, mid, lo


def _dot3(x, b01):
    hi, mid, lo = _split3(x)
    return _dot(hi, b01) + _dot(mid, b01) + _dot(lo, b01)


def _dot6(x, w):
    xh, xm, xl = _split3(x)
    wh, wm, wl = _split3(w)
    return (_dot(xh, wh) + _dot(xh, wm) + _dot(xm, wh)
            + _dot(xm, wm) + _dot(xh, wl) + _dot(xl, wh))


def _sigmoid(x):
    return 1.0 / (1.0 + jnp.exp(-x))


def _rms(x, g):
    return x * lax.rsqrt(jnp.mean(x * x, axis=-1, keepdims=True) + EPS) * g


def _row_tile(m):
    for t in (512, 256, 128, 64, 32, 16, 8):
        if m % t == 0:
            return t
    raise ValueError(f"row count {m} not a multiple of 8")


def _rms_matmul_kernel(x_ref, g_ref, w_ref, o_ref):
    o_ref[...] = _dot(_rms(x_ref[...], g_ref[...]).astype(BF16), w_ref[...])


def rms_matmul(x, g, w):
    m, d = x.shape
    n = w.shape[1]
    tm = min(_row_tile(m), 256)
    return pl.pallas_call(
        _rms_matmul_kernel,
        grid=(m // tm,),
        in_specs=[pl.BlockSpec((tm, d), lambda i: (i, 0)),
                  pl.BlockSpec((1, d), lambda i: (0, 0)),
                  pl.BlockSpec((d, n), lambda i: (0, 0))],
        out_specs=pl.BlockSpec((tm, n), lambda i: (i, 0)),
        out_shape=jax.ShapeDtypeStruct((m, n), F32),
        compiler_params=_cparams(1),
    )(x, g.reshape(1, d), w)


def _rms_matmul_kvt_kernel(x_ref, g_ref, w_ref, wt_ref, o_ref, ot_ref):
    xn = _rms(x_ref[...], g_ref[...]).astype(BF16)
    o_ref[...] = _dot(xn, w_ref[...])
    ot_ref[...] = _dot_nt(wt_ref[...], xn)


def rms_matmul_kvt(x, g, w, wt, bsz):
    m, d = x.shape
    n = w.shape[1]
    nt = wt.shape[0]
    t = m // bsz
    tm = min(_row_tile(t), 256)
    per = t // tm
    return pl.pallas_call(
        _rms_matmul_kvt_kernel,
        grid=(m // tm,),
        in_specs=[pl.BlockSpec((tm, d), lambda i: (i, 0)),
                  pl.BlockSpec((1, d), lambda i: (0, 0)),
                  pl.BlockSpec((d, n), lambda i: (0, 0)),
                  pl.BlockSpec((nt, d), lambda i: (0, 0))],
        out_specs=[pl.BlockSpec((tm, n), lambda i: (i, 0)),
                   pl.BlockSpec((None, nt, tm), lambda i: (i // per, 0, i % per))],
        out_shape=[jax.ShapeDtypeStruct((m, n), F32), jax.ShapeDtypeStruct((bsz, nt, t), F32)],
        compiler_params=_cparams(1),
        name="rms_matmul_kvt",
    )(x, g.reshape(1, d), w, wt)


def _router_kernel(x_ref, g_ref, w_ref, o_ref):
    o_ref[...] = _dot6(_rms(x_ref[...], g_ref[...]), w_ref[...])


def router_logits(x, g, w):
    m, d = x.shape
    n = w.shape[1]
    tm = min(_row_tile(m), 256)
    return pl.pallas_call(
        _router_kernel,
        grid=(m // tm,),
        in_specs=[pl.BlockSpec((tm, d), lambda i: (i, 0)),
                  pl.BlockSpec((1, d), lambda i: (0, 0)),
                  pl.BlockSpec((d, n), lambda i: (0, 0))],
        out_specs=pl.BlockSpec((tm, n), lambda i: (i, 0)),
        out_shape=jax.ShapeDtypeStruct((m, n), F32),
        compiler_params=_cparams(1),
        name="router",
    )(x, g.reshape(1, d), w)


def _outproj_kernel(h_ref, a_ref, b_ref, w_ref, o_ref):
    ka = a_ref.shape[1]
    o_ref[...] = (h_ref[...] + _dot(a_ref[...].astype(BF16), w_ref[0:ka, :])
                  + _dot(b_ref[...].astype(BF16), w_ref[ka:, :]))


def out_proj(h, a, b, w):
    m, d = h.shape
    ka, kb = a.shape[1], b.shape[1]
    tm = _row_tile(m)
    return pl.pallas_call(
        _outproj_kernel,
        grid=(m // tm,),
        in_specs=[pl.BlockSpec((tm, d), lambda i: (i, 0)),
                  pl.BlockSpec((tm, ka), lambda i: (i, 0)),
                  pl.BlockSpec((tm, kb), lambda i: (i, 0)),
                  pl.BlockSpec((ka + kb, d), lambda i: (0, 0))],
        out_specs=pl.BlockSpec((tm, d), lambda i: (i, 0)),
        out_shape=jax.ShapeDtypeStruct((m, d), F32),
        compiler_params=_cparams(1),
        name="out_proj",
    )(h, a, b, w)


def _ffn_kernel(te_ref, tv_ref, idx_ref, x_ref, g_ref, gate_ref, w1_ref, w3_ref, w2_ref, o_ref,
                xn_ref, acc_ref, *gather_scratch, residual):
    i = pl.program_id(0)
    j = pl.program_id(1)

    if gather_scratch:
        xbuf, sem = gather_scratch
        tm = xbuf.shape[1]

        def issue(tile, slot):
            def start(k2, carry):
                for q in range(2):
                    k = 2 * k2 + q
                    pltpu.make_async_copy(x_ref.at[idx_ref[tile * tm + k]], xbuf.at[slot, k],
                                          sem.at[slot]).start(priority=q)
                return carry
            lax.fori_loop(0, tm // 2, start, 0, unroll=4)

        @pl.when((j == 0) & (i == 0))
        def _():
            issue(0, 0)

        @pl.when(j == 0)
        def _():
            slot = i % 2
            pltpu.make_async_copy(x_ref.at[pl.ds(0, tm)], xbuf.at[slot], sem.at[slot]).wait()

            @pl.when(i + 1 < pl.num_programs(0))
            def _():
                issue(i + 1, 1 - slot)

            x = jnp.concatenate([xbuf[slot, :, s, :] for s in range(xbuf.shape[2])], axis=1)
            xn_ref[...] = _rms(x, g_ref[...]).astype(BF16)
            acc_ref[...] = jnp.zeros_like(acc_ref)
    else:
        @pl.when(j == 0)
        def _():
            xn_ref[...] = _rms(x_ref[...], g_ref[...]).astype(BF16)
            acc_ref[...] = jnp.zeros_like(acc_ref)

    @pl.when(tv_ref[i] != 0)
    def _():
        xn = xn_ref[...]
        a = _dot(xn, w1_ref[...].astype(BF16))
        b = _dot(xn, w3_ref[...].astype(BF16))
        hid = (a * _sigmoid(a)) * b
        acc_ref[...] += _dot(hid.astype(BF16), w2_ref[...].astype(BF16))

    @pl.when(j == pl.num_programs(1) - 1)
    def _():
        y = acc_ref[...] * gate_ref[...]
        if residual:
            y = y + x_ref[...]
        o_ref[...] = y


def grouped_ffn(x, g, row_gate, tile_expert, tile_valid, w1, w3, w2, *, tm, tf, residual, row_idx=None):
    gather = row_idx is not None
    m = row_gate.shape[0]
    d = w1.shape[1]
    f = w1.shape[2]
    assert not (residual and gather)
    if gather:
        assert x.shape[1:] == (SUBLANES, d // SUBLANES)
        x_spec = pl.BlockSpec(memory_space=pl.ANY)
        scratch = [pltpu.VMEM((2, tm) + x.shape[1:], F32), pltpu.SemaphoreType.DMA((2,))]
    else:
        row_idx = jnp.zeros((1,), I32)
        x_spec = pl.BlockSpec((tm, d), lambda i, j, *_: (i, 0))
        scratch = []
    grid_spec = pltpu.PrefetchScalarGridSpec(
        num_scalar_prefetch=3,
        grid=(m // tm, f // tf),
        in_specs=[x_spec,
                  pl.BlockSpec((1, d), lambda i, j, *_: (0, 0)),
                  pl.BlockSpec((tm, 1), lambda i, j, *_: (i, 0)),
                  pl.BlockSpec((None, d, tf), lambda i, j, te, *_: (te[i], 0, j)),
                  pl.BlockSpec((None, d, tf), lambda i, j, te, *_: (te[i], 0, j)),
                  pl.BlockSpec((None, tf, d), lambda i, j, te, *_: (te[i], j, 0))],
        out_specs=pl.BlockSpec((tm, d), lambda i, j, *_: (i, 0)),
        scratch_shapes=[pltpu.VMEM((tm, d), BF16), pltpu.VMEM((tm, d), F32)] + scratch,
    )
    return pl.pallas_call(
        functools.partial(_ffn_kernel, residual=residual),
        grid_spec=grid_spec,
        out_shape=jax.ShapeDtypeStruct((m, d), F32),
        compiler_params=_cparams(2),
        name="grouped_ffn",
    )(tile_expert, tile_valid, row_idx, x, g.reshape(1, d), row_gate, w1, w3, w2)


def _ple_kernel(h_ref, g_ref, wg_ref, p_ref, wp_ref, gf_ref, o_ref, *, final):
    h = h_ref[...]
    gate = _sigmoid(_dot(_rms(h, g_ref[...]).astype(BF16), wg_ref[...]))
    proj = _dot(p_ref[...].astype(BF16), wp_ref[...])
    h2 = h + gate * proj
    o_ref[...] = _rms(h2, gf_ref[...]) if final else h2


def ple(h, g, wg, p, wp, gf, final):
    m, d = h.shape
    pd = p.shape[1]
    tm = _row_tile(m)
    row = lambda i: (i, 0)
    fixed = lambda i: (0, 0)
    return pl.pallas_call(
        functools.partial(_ple_kernel, final=final),
        grid=(m // tm,),
        in_specs=[pl.BlockSpec((tm, d), row), pl.BlockSpec((1, d), fixed),
                  pl.BlockSpec((d, d), fixed), pl.BlockSpec((tm, pd), row),
                  pl.BlockSpec((pd, d), fixed), pl.BlockSpec((1, d), fixed)],
        out_specs=pl.BlockSpec((tm, d), row),
        out_shape=jax.ShapeDtypeStruct((m, d), F32),
        compiler_params=_cparams(1),
        name="ple",
    )(h, g.reshape(1, d), wg, p, wp, gf.reshape(1, d))


def _bias_kernel(tab_ref, d_ref, o_ref):
    n = jnp.maximum(d_ref[...], 0)
    max_exact = N_BUCKETS // 2
    nf = jnp.maximum(n, 1).astype(F32)
    large = max_exact + (jnp.log(nf / max_exact) / math.log(MAX_DISTANCE / max_exact)
                         * (N_BUCKETS - max_exact)).astype(I32)
    large = jnp.minimum(large, N_BUCKETS - 1)
    bucket = jnp.where(n < max_exact, n, large)
    for h in range(H_NSA):
        acc = jnp.zeros(n.shape, F32)
        for k in range(N_BUCKETS):
            acc = jnp.where(bucket == k, tab_ref[k, h], acc)
        o_ref[h] = acc


def bias_lookup(table, dist):
    r, c = dist.shape
    tr = min(r, 128)
    tc = 512 if c % 512 == 0 else c
    return pl.pallas_call(
        _bias_kernel,
        grid=(r // tr, c // tc),
        in_specs=[pl.BlockSpec(memory_space=pltpu.SMEM),
                  pl.BlockSpec((tr, tc), lambda i, j: (i, j))],
        out_specs=pl.BlockSpec((H_NSA, tr, tc), lambda i, j: (0, i, j)),
        out_shape=jax.ShapeDtypeStruct((H_NSA, r, c), F32),
        compiler_params=_cparams(2),
        name="bias_lookup",
    )(table, dist)


def _swap_halves(x):
    lane = lax.broadcasted_iota(I32, x.shape, 1)
    return jnp.where(lane % HEAD_DIM < HEAD_DIM // 2,
                     pltpu.roll(x, LANES - HEAD_DIM // 2, 1), pltpu.roll(x, HEAD_DIM // 2, 1))


def _retention_prompt_kernel(q_ref, k_ref, v_ref, g_ref, cos_ref, sin_ref, gain_ref, intra_ref,
                             qdec_ref, kdec_ref, cdec_ref, y_ref, s_out_ref, s_ref):
    c = pl.program_id(1)

    @pl.when(c == 0)
    def _():
        s_ref[...] = jnp.zeros_like(s_ref)

    cos = cos_ref[...]
    sin = sin_ref[...]
    lane = lax.broadcasted_iota(I32, (RET_CHUNK, LANES), 1)
    left = lane < HEAD_DIM
    row = lax.broadcasted_iota(I32, (LANES, LANES), 0)
    col = lax.broadcasted_iota(I32, (LANES, LANES), 1)
    same_head = (row < HEAD_DIM) == (col < HEAD_DIM)
    ones_bd = jnp.where(same_head, 1.0, 0.0).astype(BF16)
    for p in range(H_RET // 2):
        sl = slice(p * LANES, (p + 1) * LANES)
        q = q_ref[:, sl]
        k = k_ref[:, sl]
        v = v_ref[:, sl].astype(BF16)
        qr = q * cos + _swap_halves(q) * sin
        kr = (k * cos + _swap_halves(k) * sin) * (HEAD_DIM ** -0.5)
        qb = qr.astype(BF16)
        q2 = jnp.concatenate([jnp.where(left, qr, 0.0), jnp.where(left, 0.0, qr)], axis=0).astype(BF16)
        inner = _dot_nt(q2, kr.astype(BF16)) * intra_ref[p]
        o2 = _dot(inner.astype(BF16), v)
        s_old = s_ref[p]
        cross = _dot(qb, s_old.astype(BF16)) * qdec_ref[:, sl]
        o = jnp.where(left, o2[:RET_CHUNK], o2[RET_CHUNK:]) + cross
        kd = (kr * kdec_ref[:, sl]).T.astype(BF16)
        s_ref[p] = s_old * cdec_ref[p] + jnp.where(same_head, _dot(kd, v), 0.0)
        ms = _dot3(o * o, ones_bd) * (1.0 / HEAD_DIM)
        gate = g_ref[:, sl]
        y = o * lax.rsqrt(ms + EPS) * gain_ref[:, sl] * (gate * _sigmoid(gate))
        y_ref[:, sl] = y.astype(y_ref.dtype)

    @pl.when(c == pl.num_programs(1) - 1)
    def _():
        for p in range(H_RET // 2):
            s = s_ref[p]
            s_out_ref[2 * p] = s[:HEAD_DIM, :HEAD_DIM]
            s_out_ref[2 * p + 1] = s[HEAD_DIM:, HEAD_DIM:]


def _retention_consts(c):
    log_g = jnp.log1p(-jnp.exp2(-5.0 - jnp.arange(H_RET, dtype=F32)))
    j = jnp.arange(c, dtype=F32)
    diff = j[:, None] - j[None, :]
    intra = jnp.where(diff >= 0, jnp.exp(jnp.maximum(diff, 0.0)[None] * log_g[:, None, None]), 0.0)
    q_decay = jnp.exp((j[:, None] + 1.0) * log_g[None, :])
    k_decay = jnp.exp((c - 1.0 - j[:, None]) * log_g[None, :])
    c_decay = jnp.exp(c * log_g)
    intra2 = intra.reshape(H_RET // 2, 2 * c, c)
    qdec = jnp.repeat(q_decay, HEAD_DIM, axis=1)
    kdec = jnp.repeat(k_decay, HEAD_DIM, axis=1)
    cdec = jnp.repeat(c_decay.reshape(H_RET // 2, 2), HEAD_DIM, axis=1)[:, :, None] * jnp.ones((1, 1, LANES), F32)
    return intra2, qdec, kdec, cdec


def _rotary_tables(pos):
    half = HEAD_DIM // 2
    inv = ROPE_BASE ** (-jnp.arange(half, dtype=F32) / half)
    ang = pos.astype(F32)[:, None] * inv[None, :]
    cos, sin = jnp.cos(ang), jnp.sin(ang)
    cos_t = jnp.concatenate([cos, cos, cos, cos], axis=1)
    sin_t = jnp.concatenate([-sin, sin, -sin, sin], axis=1)
    return cos_t, sin_t


def retention_prompt(z3, gain):
    b, t, _ = z3.shape
    c = RET_CHUNK
    cos_t, sin_t = _rotary_tables(jnp.arange(t))
    intra2, qdec, kdec, cdec = _retention_consts(c)
    blk = lambda col: pl.BlockSpec((None, c, RET_W), lambda bi, ci, col=col: (bi, ci, col))
    fixed2 = lambda shape: pl.BlockSpec(shape, lambda bi, ci: (0,) * len(shape))
    return pl.pallas_call(
        _retention_prompt_kernel,
        grid=(b, t // c),
        in_specs=[blk(COL_RQ // RET_W), blk(COL_RK // RET_W), blk(COL_RV // RET_W), blk(COL_RG // RET_W),
                  pl.BlockSpec((c, LANES), lambda bi, ci: (ci, 0)),
                  pl.BlockSpec((c, LANES), lambda bi, ci: (ci, 0)),
                  fixed2((1, RET_W)), fixed2((H_RET // 2, 2 * c, c)), fixed2((c, RET_W)),
                  fixed2((c, RET_W)), fixed2((H_RET // 2, LANES, LANES))],
        out_specs=[pl.BlockSpec((None, c, RET_W), lambda bi, ci: (bi, ci, 0)),
                   pl.BlockSpec((None, H_RET, HEAD_DIM, HEAD_DIM), lambda bi, ci: (bi, 0, 0, 0))],
        out_shape=[jax.ShapeDtypeStruct((b, t, RET_W), BF16),
                   jax.ShapeDtypeStruct((b, H_RET, HEAD_DIM, HEAD_DIM), F32)],
        scratch_shapes=[pltpu.VMEM((H_RET // 2, LANES, LANES), F32)],
        compiler_params=_cparams(2),
        name="retention_prompt",
    )(z3, z3, z3, z3, cos_t, sin_t, gain.reshape(1, RET_W), intra2, qdec, kdec, cdec)


def _retention_step_kernel(q_ref, k_ref, v_ref, g_ref, s_ref, cos_ref, sin_ref, gamma_ref, gain_ref,
                           y_ref, s_out_ref):
    half = HEAD_DIM // 2
    cos = cos_ref[...]
    sin = sin_ref[...]

    def rot(x):
        x1, x2 = x[:half], x[half:]
        return jnp.concatenate([x1 * cos - x2 * sin, x1 * sin + x2 * cos], axis=0)

    qr = rot(q_ref[...])
    kr = rot(k_ref[...]) * (HEAD_DIM ** -0.5)
    v = v_ref[...]
    gamma = gamma_ref[...]
    o = jnp.zeros_like(v)
    for d in range(HEAD_DIM):
        s_new = s_ref[d] * gamma + kr[d:d + 1] * v
        s_out_ref[d] = s_new
        o = o + qr[d:d + 1] * s_new
    ms = jnp.mean(o * o, axis=0, keepdims=True)
    gate = g_ref[...]
    y_ref[...] = o * lax.rsqrt(ms + EPS) * gain_ref[...] * (gate * _sigmoid(gate))


def retention_step(zt, state, layer, pos, gain):
    bs = zt.shape[1]
    half = HEAD_DIM // 2
    inv = ROPE_BASE ** (-jnp.arange(half, dtype=F32) / half)
    ang = jnp.full((1,), pos).astype(F32)[:, None] * inv[None, :]
    cos = jnp.broadcast_to(jnp.cos(ang).reshape(half, 1), (half, bs))
    sin = jnp.broadcast_to(jnp.sin(ang).reshape(half, 1), (half, bs))
    gamma = 1.0 - jnp.exp2(-5.0 - jnp.arange(H_RET, dtype=F32))
    gamma = jnp.broadcast_to(gamma.reshape(H_RET, 1, 1), (H_RET, 1, bs))
    gain_b = jnp.broadcast_to(gain.reshape(H_RET, HEAD_DIM, 1), (H_RET, HEAD_DIM, bs))
    head = lambda col: pl.BlockSpec((HEAD_DIM, bs), lambda h, col=col: (col // HEAD_DIM + h, 0))
    fixed = pl.BlockSpec((half, bs), lambda h: (0, 0))
    return pl.pallas_call(
        _retention_step_kernel,
        grid=(H_RET,),
        in_specs=[head(COL_RQ), head(COL_RK), head(COL_RV), head(COL_RG),
                  pl.BlockSpec((None, HEAD_DIM, HEAD_DIM, bs), lambda h: (layer * H_RET + h, 0, 0, 0)),
                  fixed, fixed,
                  pl.BlockSpec((None, 1, bs), lambda h: (h, 0, 0)),
                  pl.BlockSpec((None, HEAD_DIM, bs), lambda h: (h, 0, 0))],
        out_specs=[pl.BlockSpec((HEAD_DIM, bs), lambda h: (h, 0)),
                   pl.BlockSpec((None, HEAD_DIM, HEAD_DIM, bs), lambda h: (h, 0, 0, 0))],
        out_shape=[jax.ShapeDtypeStruct((RET_W, bs), F32),
                   jax.ShapeDtypeStruct((H_RET, HEAD_DIM, HEAD_DIM, bs), F32)],
        compiler_params=_cparams(1),
        name="retention_step",
    )(zt, zt, zt, zt, state, cos, sin, gamma, gain_b)


def _gelu_tanh(x):
    return 0.5 * x * (1.0 + jnp.tanh(math.sqrt(2.0 / math.pi) * (x + 0.044715 * (x * x * x))))


def _compress_kernel(*refs, n_pieces):
    x_refs = refs[1:1 + n_pieces]
    pe_ref, w1_ref, w2_ref, kc_ref, vc_ref, rows_ref = refs[1 + n_pieces:]
    width = x_refs[0].shape[1]
    n = n_pieces * width // CMP_STRIDE
    for kv, o_ref in enumerate((kc_ref, vc_ref)):
        for p, x_ref in enumerate(x_refs):
            for c in range(width // LANES):
                rows_ref[pl.ds(p * width + c * LANES, LANES), :] = (
                    x_ref[kv * LANES:(kv + 1) * LANES, c * LANES:(c + 1) * LANES].T)
        xk = jnp.concatenate([rows_ref[pl.ds(j, n, stride=CMP_STRIDE), :] for j in range(CMP_STRIDE)], axis=1)
        top = _dot((xk + pe_ref[kv, 0]).astype(BF16), w1_ref[kv, 0])
        bot = _dot((xk + pe_ref[kv, 1]).astype(BF16), w1_ref[kv, 1])
        pre = top + pltpu.roll(bot, n - 1, 0)
        hid = _gelu_tanh(pre).astype(BF16)
        o_ref[...] = jnp.concatenate(
            [_dot(hid[:, g * CMP_HIDDEN:(g + 1) * CMP_HIDDEN], w2_ref[kv]) for g in range(N_KV)], axis=1)


def _compress_weights(pe, w1, w2):
    pe_r = pe.reshape(2, 2, CMP_STRIDE, 1, HEAD_DIM)
    pe_t = jnp.broadcast_to(pe_r, (2, 2, CMP_STRIDE, N_KV, HEAD_DIM)).reshape(2, 2, 1, CMP_STRIDE * LANES)
    w1r = w1.reshape(2, 2, CMP_STRIDE, HEAD_DIM, CMP_HIDDEN)
    eye = jnp.eye(N_KV, dtype=w1.dtype)
    wbig = jnp.einsum('khjdc,gG->khjgdGc', w1r, eye)
    wbig = wbig.reshape(2, 2, CMP_STRIDE * LANES, N_KV * CMP_HIDDEN)
    return pe_t.astype(F32), wbig.astype(BF16), w2.astype(BF16)


def compress(src, idx, pe_t, w1b, w2b):
    bsz, n_pieces = idx.shape
    width = src.shape[2]
    n = n_pieces * width // CMP_STRIDE
    piece = lambda p: pl.BlockSpec((None, KV_W, width), lambda b, ix, p=p: (ix[b, p], 0, 0))
    fixed = lambda shape: pl.BlockSpec(shape, lambda b, ix: (0,) * len(shape))
    out = pl.BlockSpec((None, n, LANES), lambda b, ix: (b, 0, 0))
    grid_spec = pltpu.PrefetchScalarGridSpec(
        num_scalar_prefetch=1,
        grid=(bsz,),
        in_specs=[piece(p) for p in range(n_pieces)]
        + [fixed(pe_t.shape), fixed(w1b.shape), fixed(w2b.shape)],
        out_specs=[out, out],
        scratch_shapes=[pltpu.VMEM((n_pieces * width, LANES), F32)],
    )
    return pl.pallas_call(
        functools.partial(_compress_kernel, n_pieces=n_pieces),
        grid_spec=grid_spec,
        out_shape=[jax.ShapeDtypeStruct((bsz, n, LANES), F32)] * 2,
        compiler_params=_cparams(1),
        name="compress",
    )(idx, *([src] * n_pieces), pe_t, w1b, w2b)


def _masked_softmax(s, mask):
    s = jnp.where(mask, s, NEG)
    m = jnp.max(s, axis=-1, keepdims=True)
    e = jnp.where(mask, jnp.exp(s - m), 0.0)
    return e / jnp.maximum(jnp.sum(e, axis=-1, keepdims=True), TINY)


def _block_scores(imp, qpos, n_sel, axis=1):
    blk = lax.broadcasted_iota(I32, imp.shape, axis)
    cur = qpos // SEL_BLOCK
    forced = (blk == 0) | (blk == cur) | (blk == cur - 1)
    valid = blk * SEL_BLOCK <= qpos
    score = jnp.where(forced, FORCED_SCORE, jnp.where(valid, imp, -1.0))
    return jnp.where(blk < n_sel, score, -2.0)


def _select_top(score, n_top, axis=1):
    blk = lax.broadcasted_iota(I32, score.shape, axis)
    rank = jnp.zeros(score.shape, F32)
    for i in range(score.shape[axis]):
        one = score[:, i:i + 1] if axis == 1 else score[i:i + 1, :]
        beats = (one > score) | ((one == score) & (blk > i))
        rank = rank + jnp.where(beats, 1.0, 0.0)
    return jnp.where(rank < n_top, 1.0, 0.0)


def _attend_t(s, ok, m_prev, l_prev):
    s = jnp.where(ok, s, NEG)
    m_new = jnp.maximum(m_prev, jnp.max(s, axis=0, keepdims=True))
    alpha = jnp.exp(m_prev - m_new)
    p = jnp.exp(s - m_new)
    l_new = alpha * l_prev + jnp.sum(p, axis=0, keepdims=True)
    return m_new, l_new, alpha, p


def _nsa_prompt_kernel(q_ref, gate_ref, kc_ref, vct_ref, ks_ref, vst_ref, kw_ref, vwt_ref, bc_ref, bt_ref,
                       o_ref, qt_ref, pcs_ref, pick_ref, acc_ref, os_ref, s_ref, p_ref, *, n_cmp, n_sel):
    i = pl.program_id(1)
    qs = i * Q_BLOCK
    key_i = lax.broadcasted_iota(I32, (KV_CHUNK, Q_BLOCK), 0)
    tok_i = lax.broadcasted_iota(I32, (KV_CHUNK, Q_BLOCK), 1)
    row8 = lax.broadcasted_iota(I32, (H_NSA, Q_BLOCK), 0)
    zeros_half = jnp.zeros((HEAD_DIM, Q_BLOCK), F32)

    for k2 in range(H_NSA // 2):
        slab_t = q_ref[:, k2 * LANES:(k2 + 1) * LANES].T * (HEAD_DIM ** -0.5)
        for h in (2 * k2, 2 * k2 + 1):
            qh = slab_t[(h % 2) * HEAD_DIM:(h % 2 + 1) * HEAD_DIM]
            both = [qh, zeros_half] if h // R_GRP == 0 else [zeros_half, qh]
            qt_ref[:, h * Q_BLOCK:(h + 1) * Q_BLOCK] = jnp.concatenate(both, axis=0).astype(BF16)

    sig_t = _sigmoid(gate_ref[...]).T

    ncp = kc_ref.shape[0]
    kc = kc_ref[...].astype(BF16)
    n_c = lax.broadcasted_iota(I32, (ncp, Q_BLOCK), 0)
    t_c = lax.broadcasted_iota(I32, (ncp, Q_BLOCK), 1)
    ok_c = (n_c * CMP_STRIDE + (L_CMP - 1) <= qs + t_c) & (n_c < n_cmp)
    blk_pos = qs + lax.broadcasted_iota(I32, (pick_ref.shape[1], Q_BLOCK), 1)
    o_c = []
    for g in range(N_KV):
        vct = vct_ref[g * HEAD_DIM:(g + 1) * HEAD_DIM, :].astype(BF16)
        pcs = jnp.zeros((ncp, Q_BLOCK), F32)
        for r in range(R_GRP):
            h = g * R_GRP + r
            s = _dot(kc, qt_ref[:, h * Q_BLOCK:(h + 1) * Q_BLOCK]) + bc_ref[h]
            s = jnp.where(ok_c, s, NEG)
            e = jnp.where(ok_c, jnp.exp(s - jnp.max(s, axis=0, keepdims=True)), 0.0)
            p = e * (1.0 / jnp.maximum(jnp.sum(e, axis=0, keepdims=True), TINY))
            o_c.append(_dot(vct, p.astype(BF16)))
            pcs = pcs + p
        pcs_ref[...] = pcs
        per = SEL_BLOCK // CMP_STRIDE
        imp = pcs_ref[pl.ds(0, ncp // per, stride=per), :]
        for c in range(1, per):
            imp = imp + pcs_ref[pl.ds(c, ncp // per, stride=per), :]
        pick_ref[g] = _select_top(_block_scores(imp, blk_pos, n_sel, axis=0), min(N_SELECT, n_sel), axis=0)

    dmat = tok_i - key_i

    def stream(k_ref, vt_ref, n_chunks, chunk_of, mask_of):
        def scores(c, slot):
            off, _ = chunk_of(c)
            k = k_ref[pl.ds(off, KV_CHUNK), :].astype(BF16)
            for h in range(H_NSA):
                s_ref[slot, h] = _dot(k, qt_ref[:, h * Q_BLOCK:(h + 1) * Q_BLOCK])

        def values(c, slot, alpha_all):
            off, _ = chunk_of(c)
            for g in range(N_KV):
                vt = vt_ref[g * HEAD_DIM:(g + 1) * HEAD_DIM, pl.ds(off, KV_CHUNK)].astype(BF16)
                for h in range(g * R_GRP, (g + 1) * R_GRP):
                    acc_ref[h] = acc_ref[h] * alpha_all[h:h + 1] + _dot(vt, p_ref[slot, h])

        def softmax(c, slot, m_all, l_all):
            _, rel = chunk_of(c)
            ok_g = mask_of(c, rel)
            bias_idx = min(max(rel, 0), FAR_REL) if isinstance(rel, int) else jnp.clip(rel, 0, FAR_REL)
            alpha_all = jnp.zeros((H_NSA, Q_BLOCK), F32)
            for h in range(H_NSA):
                s = s_ref[slot, h] + bt_ref[bias_idx, h]
                m_new, l_new, alpha, p = _attend_t(s, ok_g[h // R_GRP], m_all[h:h + 1], l_all[h:h + 1])
                p_ref[slot, h] = p.astype(BF16)
                m_all = jnp.where(row8 == h, m_new, m_all)
                l_all = jnp.where(row8 == h, l_new, l_all)
                alpha_all = jnp.where(row8 == h, alpha, alpha_all)
            return m_all, l_all, alpha_all

        def stage(c, slot, carry, prefetch=True):
            m_all, l_all, alpha_prev = carry
            if prefetch:
                scores(c + 1, 1 - slot)
            values(c - 1, 1 - slot, alpha_prev)
            return softmax(c, slot, m_all, l_all)

        acc_ref[...] = jnp.zeros_like(acc_ref)
        p_ref[1] = jnp.zeros(p_ref.shape[1:], BF16)
        scores(0, 0)
        carry = (jnp.full((H_NSA, Q_BLOCK), NEG, F32), jnp.zeros((H_NSA, Q_BLOCK), F32),
                 jnp.ones((H_NSA, Q_BLOCK), F32))
        if isinstance(n_chunks, int):
            for c in range(n_chunks):
                carry = stage(c, c % 2, carry, prefetch=c + 1 < n_chunks)
            last = n_chunks - 1
        else:
            pairs = (n_chunks + 1) // 2
            carry = lax.fori_loop(0, pairs, lambda t, cr: stage(2 * t + 1, 1, stage(2 * t, 0, cr)), carry)
            last = 2 * pairs - 1
        m_all, l_all, alpha_last = carry
        values(last, last % 2 if isinstance(last, int) else 1, alpha_last)
        return m_all, l_all

    def sel_chunk(c):
        return pl.multiple_of(jnp.clip(c, 0, i) * KV_CHUNK, KV_CHUNK), i - c

    def sel_mask(c, rel):
        causal = rel * KV_CHUNK + dmat >= 0
        per = KV_CHUNK // SEL_BLOCK
        first = jnp.minimum(c, i) * per
        ok_g = []
        for g in range(N_KV):
            picked = jnp.concatenate(
                [jnp.broadcast_to(pick_ref[g, pl.ds(first + b, 1), :], (SEL_BLOCK, Q_BLOCK)) for b in range(per)],
                axis=0)
            ok_g.append(causal & (picked > 0.5))
        return ok_g

    m_s, l_s = stream(ks_ref, vst_ref, i + 1, sel_chunk, sel_mask)
    for h in range(H_NSA):
        os_ref[h] = jnp.where(m_s[h:h + 1] > 0.5 * NEG, acc_ref[h] / jnp.maximum(l_s[h:h + 1], TINY), 0.0)

    n_back = WINDOW // KV_CHUNK

    def win_chunk(c):
        return pl.multiple_of(jnp.clip(i - n_back + c, 0, i) * KV_CHUNK, KV_CHUNK), n_back - c

    def win_mask(c, rel):
        dist = rel * KV_CHUNK + dmat + jnp.where(i - n_back + c < 0, WINDOW, 0)
        ok = (dist >= 0) & (dist < WINDOW)
        return [ok, ok]

    m_w, l_w = stream(kw_ref, vwt_ref, n_back + 1, win_chunk, win_mask)

    for k2 in range(H_NSA // 2):
        pair = []
        for h in (2 * k2, 2 * k2 + 1):
            o_w = jnp.where(m_w[h:h + 1] > 0.5 * NEG, acc_ref[h] / jnp.maximum(l_w[h:h + 1], TINY), 0.0)
            pair.append(sig_t[3 * h:3 * h + 1] * o_c[h] + sig_t[3 * h + 1:3 * h + 2] * os_ref[h]
                        + sig_t[3 * h + 2:3 * h + 3] * o_w)
        o_ref[:, k2 * LANES:(k2 + 1) * LANES] = jnp.concatenate(pair, axis=0).T.astype(o_ref.dtype)


def _selection_consts(n_rows_cmp, n_keys):
    n = np.arange(n_rows_cmp)
    blk = np.arange(SEL_LANES)
    pool = (n[:, None] // (SEL_BLOCK // CMP_STRIDE) == blk[None, :]).astype(np.float32)
    key = np.arange(n_keys)
    expand = (blk[:, None] == key[None, :] // SEL_BLOCK).astype(np.float32)
    return jnp.asarray(pool, BF16), jnp.asarray(expand, BF16)


def nsa_prompt(z3, kvt, kc, vct, bias_ct, bias_tt):
    b, t, _ = z3.shape
    n_sub = t // CMP_STRIDE
    n_sel = -(-t // SEL_BLOCK)
    assert kc.shape[1] == n_sub and n_sub == n_sel * (SEL_BLOCK // CMP_STRIDE) and n_sel % SUBLANES == 0
    fixed = lambda shape: pl.BlockSpec(shape, lambda bi, i: (0,) * len(shape))
    v_rows = lambda branch: (branch * KV_W + N_KV * HEAD_DIM) // LANES
    return pl.pallas_call(
        functools.partial(_nsa_prompt_kernel, n_cmp=n_sub - 1, n_sel=n_sel),
        grid=(b, t // Q_BLOCK),
        in_specs=[pl.BlockSpec((None, Q_BLOCK, NSA_QW), lambda bi, i: (bi, i, COL_NQ // NSA_QW)),
                  pl.BlockSpec((None, Q_BLOCK, LANES), lambda bi, i: (bi, i, COL_NG // LANES)),
                  pl.BlockSpec((None, n_sub, LANES), lambda bi, i: (bi, 0, 0)),
                  pl.BlockSpec((None, LANES, n_sub), lambda bi, i: (bi, 0, 0)),
                  pl.BlockSpec((None, t, LANES), lambda bi, i: (bi, 0, COL_KVS // LANES)),
                  pl.BlockSpec((None, LANES, t), lambda bi, i: (bi, v_rows(1), 0)),
                  pl.BlockSpec((None, t, LANES), lambda bi, i: (bi, 0, COL_KVW // LANES)),
                  pl.BlockSpec((None, LANES, t), lambda bi, i: (bi, v_rows(2), 0)),
                  pl.BlockSpec((H_NSA, n_sub, Q_BLOCK), lambda bi, i: (0, 0, i)),
                  fixed(bias_tt.shape)],
        out_specs=pl.BlockSpec((None, Q_BLOCK, NSA_QW), lambda bi, i: (bi, i, 0)),
        out_shape=jax.ShapeDtypeStruct((b, t, NSA_QW), BF16),
        scratch_shapes=[pltpu.VMEM((LANES, H_NSA * Q_BLOCK), BF16),
                        pltpu.VMEM((n_sub, Q_BLOCK), F32),
                        pltpu.VMEM((N_KV, n_sel, Q_BLOCK), F32),
                        pltpu.VMEM((H_NSA, HEAD_DIM, Q_BLOCK), F32),
                        pltpu.VMEM((H_NSA, HEAD_DIM, Q_BLOCK), F32),
                        pltpu.VMEM((2, H_NSA, KV_CHUNK, Q_BLOCK), F32),
                        pltpu.VMEM((2, H_NSA, KV_CHUNK, Q_BLOCK), BF16)],
        compiler_params=_cparams(2),
        name="nsa_prompt",
    )(z3, z3, kc, vct, z3, kvt, z3, kvt, bias_ct, bias_tt)


def _nsa_step_kernel(pt_ref, z_ref, kc_ref, vc_ref, *rest, n_pages, per_step, **static):
    del pt_ref
    pages = rest[:per_step * n_pages]
    win_ref, bcs_ref, bss_ref, bws_ref, pool_ref, exp_ref, o_ref, wo_ref = rest[per_step * n_pages:]
    for u in range(per_step):
        _nsa_step_one(z_ref.at[u], kc_ref.at[u], vc_ref.at[u], pages[u * n_pages:(u + 1) * n_pages], win_ref.at[u],
                      bcs_ref, bss_ref, bws_ref, pool_ref, exp_ref, o_ref.at[u], wo_ref.at[u], **static)


def _nsa_step_one(z_ref, kc_ref, vc_ref, pages, win_ref, bcs_ref, bss_ref, bws_ref, pool_ref, exp_ref, o_ref, wo_ref,
                  *, past, wbuf, n_cmp, n_sel):
    z = z_ref[...]
    row8 = lax.broadcasted_iota(I32, (H_NSA, LANES), 0)
    lane8 = lax.broadcasted_iota(I32, (H_NSA, LANES), 1)
    left8 = lane8 < HEAD_DIM
    qbd = jnp.zeros((H_NSA, LANES), F32)
    for h in range(H_NSA):
        slab = jnp.broadcast_to(z[:, COL_NQ + (h // 2) * LANES:COL_NQ + (h // 2 + 1) * LANES], (H_NSA, LANES))
        g = h // R_GRP
        if h % 2 != g:
            slab = pltpu.roll(slab, HEAD_DIM, 1)
        keep = (row8 == h) & (left8 if g == 0 else jnp.logical_not(left8))
        qbd = jnp.where(keep, slab, qbd)
    qbd = (qbd * (HEAD_DIM ** -0.5)).astype(BF16)

    ncp = kc_ref.shape[0]
    n_c = lax.broadcasted_iota(I32, (H_NSA, ncp), 1)
    s_c = _dot_nt(qbd, kc_ref[...].astype(BF16)) + bcs_ref[...]
    p_c = _masked_softmax(s_c, (n_c * CMP_STRIDE + (L_CMP - 1) <= past) & (n_c < n_cmp))
    o_c = _dot(p_c.astype(BF16), vc_ref[...].astype(BF16))
    pc0 = jnp.sum(p_c[0:R_GRP], axis=0, keepdims=True)
    pc1 = jnp.sum(p_c[R_GRP:], axis=0, keepdims=True)
    rowc = lax.broadcasted_iota(I32, (H_NSA, ncp), 0)
    pcs = jnp.where(rowc < R_GRP, jnp.broadcast_to(pc0, (H_NSA, ncp)), jnp.broadcast_to(pc1, (H_NSA, ncp)))
    imp = _dot3(pcs, pool_ref[...])
    sel = _select_top(_block_scores(imp, jnp.full(imp.shape, past, I32), n_sel), min(N_SELECT, n_sel))
    picked = _dot(sel.astype(BF16), exp_ref[...])

    c128 = lax.broadcasted_iota(I32, (LANES, KV_CHUNK), 1)

    def new_col(col):
        return jnp.broadcast_to(z[:, col:col + LANES], (KV_CHUNK, LANES)).T

    def first_col(tile):
        return jnp.where(c128 == 0, tile, 0.0).astype(BF16)

    ks = [pg[0:LANES, :].astype(BF16) for pg in pages] + [first_col(new_col(COL_KVS))]
    vs = [pg[LANES:2 * LANES, :].astype(BF16) for pg in pages] + [first_col(new_col(COL_KVS + LANES))]
    s_s = jnp.concatenate([_dot(qbd, k) for k in ks], axis=1) + bss_ref[...]
    key = lax.broadcasted_iota(I32, s_s.shape, 1)
    p_s = _masked_softmax(s_s, (picked > 0.5) & (key <= past)).astype(BF16)
    o_s = _dot_nt(p_s[:, 0:KV_CHUNK], vs[0])
    for c in range(1, len(vs)):
        o_s = o_s + _dot_nt(p_s[:, c * KV_CHUNK:(c + 1) * KV_CHUNK], vs[c])

    kw_new, vw_new = new_col(COL_KVW), new_col(COL_KVW + LANES)
    win = win_ref[...]
    s_w = jnp.concatenate([_dot(qbd, win[0:LANES].astype(BF16)), _dot(qbd, first_col(kw_new))], axis=1) + bws_ref[...]
    colw = lax.broadcasted_iota(I32, s_w.shape, 1)
    p_w = _masked_softmax(s_w, (colw <= wbuf) & (wbuf - colw < WINDOW)).astype(BF16)
    o_w = _dot_nt(p_w[:, 0:wbuf], win[LANES:].astype(BF16)) + _dot_nt(p_w[:, wbuf:], first_col(vw_new))

    shifted = pltpu.roll(win, wbuf - 1, 1)
    new_kv = jnp.concatenate([kw_new, vw_new], axis=0)
    last = jnp.where(lax.broadcasted_iota(I32, (KV_W, LANES), 1) == LANES - 1, new_kv, shifted[:, wbuf - LANES:])
    wo_ref[:, 0:wbuf - LANES] = shifted[:, 0:wbuf - LANES]
    wo_ref[:, wbuf - LANES:] = last

    sig = jnp.broadcast_to(_sigmoid(z[:, COL_NG:COL_NG + LANES]), (H_NSA, LANES))

    def gate_col(c):
        return jnp.sum(jnp.where(lane8 == row8 * 3 + c, sig, 0.0), axis=1, keepdims=True)

    o = gate_col(0) * o_c + gate_col(1) * o_s + gate_col(2) * o_w
    fix = jnp.where((row8 % 2) != (row8 // R_GRP), pltpu.roll(o, HEAD_DIM, 1), o)
    left1 = left8[0:1]
    o_ref[...] = jnp.concatenate(
        [jnp.where(left1, fix[2 * k2:2 * k2 + 1], fix[2 * k2 + 1:2 * k2 + 2]) for k2 in range(H_NSA // 2)], axis=1)


def nsa_step(zs3, kc, vc, sel_cache, win_cache, page_idx, win_row0, bias_cs, bias_ss, bias_ws, past):
    bs = zs3.shape[0]
    n_pages = page_idx.shape[1]
    wbuf = win_cache.shape[2]
    n_sub = (past + 1) // CMP_STRIDE
    n_sel = -(-(past + 1) // SEL_BLOCK)
    n_keys = (n_pages + 1) * PAGE_SIZE
    assert n_sel <= SEL_LANES and PAGE_SIZE == KV_CHUNK and wbuf % KV_CHUNK == 0 and kc.shape[1] == n_sub
    pool, expand = _selection_consts(n_sub, n_keys)
    per = 2 if bs % 2 == 0 and win_row0 % 2 == 0 else 1
    fixed = lambda shape: pl.BlockSpec(shape, lambda b, pt: (0,) * len(shape))
    page = lambda u, p: pl.BlockSpec((None, KV_W, PAGE_SIZE), lambda b, pt, u=u, p=p: (pt[per * b + u, p], 0, 0))
    grid_spec = pltpu.PrefetchScalarGridSpec(
        num_scalar_prefetch=1,
        grid=(bs // per,),
        in_specs=[pl.BlockSpec((per, 1, N_IN_PAD), lambda b, pt: (b, 0, 0)),
                  pl.BlockSpec((per, n_sub, LANES), lambda b, pt: (b, 0, 0)),
                  pl.BlockSpec((per, n_sub, LANES), lambda b, pt: (b, 0, 0))]
        + [page(u, p) for u in range(per) for p in range(n_pages)]
        + [pl.BlockSpec((per, KV_W, wbuf), lambda b, pt: (win_row0 // per + b, 0, 0)),
           fixed(bias_cs.shape), fixed(bias_ss.shape), fixed(bias_ws.shape), fixed(pool.shape), fixed(expand.shape)],
        out_specs=[pl.BlockSpec((per, 1, NSA_QW), lambda b, pt: (b, 0, 0)),
                   pl.BlockSpec((per, KV_W, wbuf), lambda b, pt: (b, 0, 0))],
    )
    return pl.pallas_call(
        functools.partial(_nsa_step_kernel, n_pages=n_pages, per_step=per, past=past, wbuf=wbuf, n_cmp=n_sub - 1,
                          n_sel=n_sel),
        grid_spec=grid_spec,
        out_shape=[jax.ShapeDtypeStruct((bs, 1, NSA_QW), F32), jax.ShapeDtypeStruct((bs, KV_W, wbuf), F32)],
        compiler_params=_cparams(1),
        name="nsa_step",
    )(page_idx, zs3, kc, vc, *([sel_cache] * (per * n_pages)), win_cache, bias_cs, bias_ss, bias_ws, pool, expand)


def _route(logits, tm):
    m = logits.shape[0]
    top_v, top_i = lax.top_k(logits, TOP_K)
    top_w = jax.nn.softmax(top_v, axis=-1)
    slot_e = top_i.reshape(-1)
    onehot = (slot_e[:, None] == jnp.arange(N_EXPERTS)[None, :]).astype(I32)
    before = jnp.cumsum(onehot, axis=0) - onehot
    rank = jnp.sum(before * onehot, axis=1)
    count = jnp.sum(onehot, axis=0)
    padded = -(-count // tm) * tm
    start = jnp.cumsum(padded) - padded
    pos = start[slot_e] + rank
    n_rows = (-(-(m * TOP_K) // tm) + N_EXPERTS) * tm
    row_slot = jnp.full((n_rows,), -1, I32).at[pos].set(jnp.arange(m * TOP_K, dtype=I32))
    slot = jnp.maximum(row_slot, 0)
    row_token = slot // TOP_K
    row_gate = jnp.where(row_slot >= 0, top_w.reshape(-1)[slot], 0.0)
    tile_start = jnp.arange(n_rows // tm, dtype=I32) * tm
    tile_expert = jnp.clip(jnp.searchsorted(jnp.cumsum(padded), tile_start, side='right'), 0, N_EXPERTS - 1)
    tile_valid = (tile_start < jnp.sum(padded)).astype(I32)
    return pos.reshape(m, TOP_K), row_token, row_gate.reshape(n_rows, 1), tile_expert.astype(I32), tile_valid


def _dense_ffn(h, g, w1, w3, w2):
    m = h.shape[0]
    tm = _row_tile(m)
    ones = jnp.ones((m // tm,), I32)
    return grouped_ffn(h, g, jnp.ones((m, 1), F32), jnp.zeros((m // tm,), I32), ones,
                       w1[None], w3[None], w2[None], tm=tm, tf=w1.shape[1] // 2, residual=True)


def kernel(x_prompt, x_sample, p_prompt, p_sample, state_ret, cache_win_kv, cache_cmp_kv, cache_sel_kv, page_table, rel_bias, norm_mix, w_in, ret_gain, cmp_pe, cmp_w1, cmp_w2, w_out, norm_ffn, ffn_w1, ffn_w3, ffn_w2, router, moe_w1, moe_w3, moe_w2, ple_norm, ple_gate, ple_proj, norm_final):
    b_p, t_p, d = x_prompt.shape
    b_s = x_sample.shape[0]
    depth = w_in.shape[0]
    n_pages = page_table.shape[1]
    past = n_pages * PAGE_SIZE
    n_pool = cache_cmp_kv.shape[1]
    w_buf = cache_win_kv.shape[2]
    assert x_sample.shape[1] == 1 and t_p % Q_BLOCK == 0 and t_p >= w_buf and past % CMP_STRIDE == 0
    row = (2, N_KV, HEAD_DIM)
    m_p = b_p * t_p

    ar = lambda n: jnp.arange(n, dtype=I32)
    n_sub_p = t_p // CMP_STRIDE
    bias_ct = bias_lookup(rel_bias, ar(t_p)[None, :] - (ar(n_sub_p)[:, None] * CMP_STRIDE + L_CMP - 1))
    rel_t = (ar(FAR_REL + 1)[:, None, None] * KV_CHUNK + ar(Q_BLOCK)[None, None, :] - ar(KV_CHUNK)[None, :, None])
    bias_tt = bias_lookup(rel_bias, rel_t.reshape((FAR_REL + 1) * KV_CHUNK, Q_BLOCK))
    bias_tt = bias_tt.reshape(H_NSA, FAR_REL + 1, KV_CHUNK, Q_BLOCK).transpose(1, 0, 2, 3)
    n_sub_s = (past + 1) // CMP_STRIDE
    rows8 = jnp.zeros((SUBLANES, 1), I32)
    bias_cs = bias_lookup(rel_bias, rows8 + (past - (ar(n_sub_s)[None, :] * CMP_STRIDE + L_CMP - 1)))[:, 0]
    bias_ss = bias_lookup(rel_bias, rows8 + (past - ar((n_pages + 1) * PAGE_SIZE)[None, :]))[:, 0]
    bias_ws = bias_lookup(rel_bias, rows8 + (w_buf - ar(w_buf + KV_CHUNK)[None, :]))[:, 0]

    def rows_last(x):
        return jnp.moveaxis(x, 2, -1).reshape(x.shape[0] * x.shape[1], KV_W, x.shape[2])

    def rows_first(xt, lead):
        return jnp.moveaxis(xt.reshape((lead,) + row + (xt.shape[-1],)), -1, 1)

    cmp_cache = rows_last(cache_cmp_kv)
    sel_cache = rows_last(cache_sel_kv)
    win_cache = rows_last(cache_win_kv)
    state_t = jnp.moveaxis(state_ret, 1, -1).reshape(depth * H_RET, HEAD_DIM, HEAD_DIM, b_s)

    h_p = x_prompt.reshape(m_p, d)
    h_s = x_sample.reshape(b_s, d)
    rp, rs, wp, ws, cp, cs, sp, ss = [], [], [], [], [], [], [], []
    for i in range(depth):
        w_in_b = jnp.pad(w_in[i], ((0, 0), (0, N_IN_PAD - N_IN))).astype(BF16)
        w_out_b = w_out[i].astype(BF16)
        pe_t, cw1, cw2 = _compress_weights(cmp_pe[i], cmp_w1[i], cmp_w2[i])

        w_kvt = w_in[i][:, COL_KVC:COL_KVC + 3 * KV_W].T.astype(BF16)
        z, kvt = rms_matmul_kvt(h_p, norm_mix[i], w_in_b, w_kvt, b_p)
        z3 = z.reshape(b_p, t_p, N_IN_PAD)
        ret_y, ret_state = retention_prompt(z3, ret_gain[i])
        kc, vc = compress(kvt, ar(b_p)[:, None], pe_t, cw1, cw2)
        nsa_y = nsa_prompt(z3, kvt, kc, jnp.swapaxes(vc, 1, 2), bias_ct, bias_tt)
        h_p = out_proj(h_p, ret_y.reshape(m_p, RET_W), nsa_y.reshape(m_p, NSA_QW), w_out_b)
        rp.append(ret_state)
        cp.append(rows_first(kvt[:, 0:KV_W], b_p))
        sp.append(rows_first(kvt[:, KV_W:2 * KV_W], b_p))
        wp.append(rows_first(kvt[:, 2 * KV_W:, t_p - w_buf:], b_p))

        zs = rms_matmul(h_s, norm_mix[i], w_in_b)
        zst = zs.T
        ret_yst, state_s = retention_step(zst, state_t, i, past, ret_gain[i])
        page_idx = page_table + i * n_pool
        grp = 4 if b_s % 4 == 0 else 1
        kc_s, vc_s = compress(cmp_cache, page_idx.reshape(b_s // grp, grp * n_pages), pe_t, cw1, cw2)
        kc_s, vc_s = kc_s.reshape(b_s, n_sub_s, LANES), vc_s.reshape(b_s, n_sub_s, LANES)
        nsa_ys, win_s = nsa_step(zs.reshape(b_s, 1, N_IN_PAD), kc_s, vc_s, sel_cache, win_cache, page_idx, i * b_s,
                                 bias_cs, bias_ss, bias_ws, past)
        h_s = out_proj(h_s, ret_yst.T, nsa_ys.reshape(b_s, NSA_QW), w_out_b)
        rs.append(jnp.moveaxis(state_s, -1, 0))
        ws.append(rows_first(win_s, b_s))
        cs.append(rows_first(zst[None, COL_KVC:COL_KVC + KV_W], 1).reshape((b_s, 1) + row))
        ss.append(rows_first(zst[None, COL_KVS:COL_KVS + KV_W], 1).reshape((b_s, 1) + row))

        j = i // 2
        if i % 2 == 0:
            w1, w3, w2 = ffn_w1[j].astype(BF16), ffn_w3[j].astype(BF16), ffn_w2[j].astype(BF16)
            h_p = _dense_ffn(h_p, norm_ffn[i], w1, w3, w2)
            h_s = _dense_ffn(h_s, norm_ffn[i], w1, w3, w2)
        else:
            router_w = jnp.pad(router[j], ((0, 0), (0, LANES - N_EXPERTS)))
            tf = moe_w1.shape[3] // 7
            logits_p = router_logits(h_p, norm_ffn[i], router_w)
            tm = 1024
            pos, row_token, row_gate, tile_expert, tile_valid = _route(logits_p[:, :N_EXPERTS], tm)
            y_rows = grouped_ffn(h_p.reshape(m_p, SUBLANES, d // SUBLANES), norm_ffn[i], row_gate, tile_expert,
                                 tile_valid, moe_w1[j], moe_w3[j], moe_w2[j], tm=tm, tf=tf, residual=False,
                                 row_idx=row_token)
            h_p = h_p + y_rows[pos[:, 0]] + y_rows[pos[:, 1]]
            logits_s = router_logits(h_s, norm_ffn[i], router_w)
            top_v, top_i = lax.top_k(logits_s[:, :N_EXPERTS], TOP_K)
            top_w = jax.nn.softmax(top_v, axis=-1)
            gates = jnp.sum(jax.nn.one_hot(top_i, N_EXPERTS, dtype=F32) * top_w[..., None], axis=-2)
            y_e = grouped_ffn(jnp.tile(h_s, (N_EXPERTS, 1)), norm_ffn[i], gates.T.reshape(N_EXPERTS * b_s, 1),
                              jnp.arange(N_EXPERTS, dtype=I32), jnp.ones((N_EXPERTS,), I32),
                              moe_w1[j], moe_w3[j], moe_w2[j], tm=b_s, tf=tf, residual=False)
            h_s = h_s + jnp.sum(y_e.reshape(N_EXPERTS, b_s, d), axis=0)

        wg, wpj = ple_gate[i].astype(BF16), ple_proj[i].astype(BF16)
        h_p = ple(h_p, ple_norm[i], wg, p_prompt[i].reshape(m_p, P_DIM), wpj, norm_final, i == depth - 1)
        h_s = ple(h_s, ple_norm[i], wg, p_sample[i].reshape(b_s, P_DIM), wpj, norm_final, i == depth - 1)

    return (h_p.reshape(b_p, t_p, d), h_s.reshape(b_s, 1, d), jnp.stack(rp), jnp.stack(rs), jnp.stack(wp),
            jnp.stack(ws), jnp.stack(cp), jnp.stack(cs), jnp.stack(sp), jnp.stack(ss))
```

```python
import functools
import math

import numpy as np
import jax
import jax.numpy as jnp
from jax import lax
from jax.experimental import pallas as pl
from jax.experimental.pallas import tpu as pltpu

F32 = jnp.float32
BF16 = jnp.bfloat16
I32 = jnp.int32

LANES = 128
SUBLANES = 8
VMEM_LIMIT_BYTES = 56 * 1024 * 1024

D_MODEL = 1024
HEAD_DIM = 64
H_RET = 8
H_NSA = 8
N_KV = 2
R_GRP = H_NSA // N_KV
RET_W = H_RET * HEAD_DIM
NSA_QW = H_NSA * HEAD_DIM
KV_W = 2 * N_KV * HEAD_DIM
N_IN = 4 * RET_W + NSA_QW + 3 * KV_W + 3 * H_NSA
N_IN_PAD = 3584
COL_RQ, COL_RK, COL_RV, COL_RG = 0, RET_W, 2 * RET_W, 3 * RET_W
COL_NQ = 4 * RET_W
COL_KVC = COL_NQ + NSA_QW
COL_KVS = COL_KVC + KV_W
COL_KVW = COL_KVS + KV_W
COL_NG = COL_KVW + KV_W
RET_CHUNK = 128
CMP_STRIDE = 16
L_CMP = 2 * CMP_STRIDE
CMP_HIDDEN = 2 * HEAD_DIM
SEL_BLOCK = 64
N_SELECT = 16
WINDOW = 512
Q_BLOCK = 128
N_BUCKETS = 32
MAX_DISTANCE = 128
N_EXPERTS = 8
TOP_K = 2
P_DIM = 256
PAGE_SIZE = 128
ROPE_BASE = 10000.0
EPS = 1e-6
FORCED_SCORE = 1e4
NEG = -1e30
TINY = float(np.finfo(np.float32).tiny)
SEL_LANES = 64
FAR_REL = 2
KV_CHUNK = 128


def _cparams(n_grid):
    return pltpu.CompilerParams(dimension_semantics=("arbitrary",) * n_grid,
                                vmem_limit_bytes=VMEM_LIMIT_BYTES)


def _dot(a, b):
    return jnp.dot(a, b, preferred_element_type=F32)


def _dot_nt(a, b):
    return lax.dot_general(a, b, (((1,), (1,)), ((), ())), preferred_element_type=F32)


def _split3(x):
    hi = x.astype(BF16)
    r = x - hi.astype(F32)
    mid = r.astype(BF16)
    lo = (r - mid.astype(F32)).astype(BF16)
    return hi, mid, lo


def _dot3(x, b01):
    hi, mid, lo = _split3(x)
    return _dot(hi, b01) + _dot(mid, b01) + _dot(lo, b01)


def _dot6(x, w):
    xh, xm, xl = _split3(x)
    wh, wm, wl = _split3(w)
    return (_dot(xh, wh) + _dot(xh, wm) + _dot(xm, wh)
            + _dot(xm, wm) + _dot(xh, wl) + _dot(xl, wh))


def _sigmoid(x):
    return 1.0 / (1.0 + jnp.exp(-x))


def _rms(x, g):
    return x * lax.rsqrt(jnp.mean(x * x, axis=-1, keepdims=True) + EPS) * g


def _row_tile(m):
    for t in (512, 256, 128, 64, 32, 16, 8):
        if m % t == 0:
            return t
    raise ValueError(f"row count {m} not a multiple of 8")


def _rms_matmul_kernel(x_ref, g_ref, w_ref, o_ref):
    o_ref[...] = _dot(_rms(x_ref[...], g_ref[...]).astype(BF16), w_ref[...])


def rms_matmul(x, g, w):
    m, d = x.shape
    n = w.shape[1]
    tm = min(_row_tile(m), 256)
    return pl.pallas_call(
        _rms_matmul_kernel,
        grid=(m // tm,),
        in_specs=[pl.BlockSpec((tm, d), lambda i: (i, 0)),
                  pl.BlockSpec((1, d), lambda i: (0, 0)),
                  pl.BlockSpec((d, n), lambda i: (0, 0))],
        out_specs=pl.BlockSpec((tm, n), lambda i: (i, 0)),
        out_shape=jax.ShapeDtypeStruct((m, n), F32),
        compiler_params=_cparams(1),
    )(x, g.reshape(1, d), w)


def _rms_matmul_kvt_kernel(x_ref, g_ref, w_ref, wt_ref, o_ref, ot_ref):
    xn = _rms(x_ref[...], g_ref[...]).astype(BF16)
    o_ref[...] = _dot(xn, w_ref[...])
    ot_ref[...] = _dot_nt(wt_ref[...], xn)


def rms_matmul_kvt(x, g, w, wt, bsz):
    m, d = x.shape
    n = w.shape[1]
    nt = wt.shape[0]
    t = m // bsz
    tm = _row_tile(t)
    per = t // tm
    return pl.pallas_call(
        _rms_matmul_kvt_kernel,
        grid=(m // tm,),
        in_specs=[pl.BlockSpec((tm, d), lambda i: (i, 0)),
                  pl.BlockSpec((1, d), lambda i: (0, 0)),
                  pl.BlockSpec((d, n), lambda i: (0, 0)),
                  pl.BlockSpec((nt, d), lambda i: (0, 0))],
        out_specs=[pl.BlockSpec((tm, n), lambda i: (i, 0)),
                   pl.BlockSpec((None, nt, tm), lambda i: (i // per, 0, i % per))],
        out_shape=[jax.ShapeDtypeStruct((m, n), F32), jax.ShapeDtypeStruct((bsz, nt, t), F32)],
        compiler_params=_cparams(1),
        name="rms_matmul_kvt",
    )(x, g.reshape(1, d), w, wt)


def _router_kernel(x_ref, g_ref, w_ref, o_ref):
    o_ref[...] = _dot6(_rms(x_ref[...], g_ref[...]), w_ref[...])


def router_logits(x, g, w):
    m, d = x.shape
    n = w.shape[1]
    tm = min(_row_tile(m), 256)
    return pl.pallas_call(
        _router_kernel,
        grid=(m // tm,),
        in_specs=[pl.BlockSpec((tm, d), lambda i: (i, 0)),
                  pl.BlockSpec((1, d), lambda i: (0, 0)),
                  pl.BlockSpec((d, n), lambda i: (0, 0))],
        out_specs=pl.BlockSpec((tm, n), lambda i: (i, 0)),
        out_shape=jax.ShapeDtypeStruct((m, n), F32),
        compiler_params=_cparams(1),
        name="router",
    )(x, g.reshape(1, d), w)


def _outproj_kernel(h_ref, a_ref, b_ref, w_ref, o_ref):
    ka = a_ref.shape[1]
    o_ref[...] = (h_ref[...] + _dot(a_ref[...].astype(BF16), w_ref[0:ka, :])
                  + _dot(b_ref[...].astype(BF16), w_ref[ka:, :]))


def out_proj(h, a, b, w):
    m, d = h.shape
    ka, kb = a.shape[1], b.shape[1]
    tm = _row_tile(m)
    return pl.pallas_call(
        _outproj_kernel,
        grid=(m // tm,),
        in_specs=[pl.BlockSpec((tm, d), lambda i: (i, 0)),
                  pl.BlockSpec((tm, ka), lambda i: (i, 0)),
                  pl.BlockSpec((tm, kb), lambda i: (i, 0)),
                  pl.BlockSpec((ka + kb, d), lambda i: (0, 0))],
        out_specs=pl.BlockSpec((tm, d), lambda i: (i, 0)),
        out_shape=jax.ShapeDtypeStruct((m, d), F32),
        compiler_params=_cparams(1),
        name="out_proj",
    )(h, a, b, w)


def _ffn_kernel(te_ref, tv_ref, idx_ref, x_ref, g_ref, gate_ref, w1_ref, w3_ref, w2_ref, o_ref,
                xn_ref, acc_ref, *gather_scratch, residual):
    i = pl.program_id(0)
    j = pl.program_id(1)

    if gather_scratch:
        xbuf, sem = gather_scratch
        tm = xbuf.shape[1]

        def issue(tile, slot):
            def start(k2, carry):
                for q in range(2):
                    k = 2 * k2 + q
                    pltpu.make_async_copy(x_ref.at[idx_ref[tile * tm + k]], xbuf.at[slot, k],
                                          sem.at[slot]).start(priority=q)
                return carry
            lax.fori_loop(0, tm // 2, start, 0, unroll=4)

        @pl.when((j == 0) & (i == 0))
        def _():
            issue(0, 0)

        @pl.when(j == 0)
        def _():
            slot = i % 2
            pltpu.make_async_copy(x_ref.at[pl.ds(0, tm)], xbuf.at[slot], sem.at[slot]).wait()

            @pl.when(i + 1 < pl.num_programs(0))
            def _():
                issue(i + 1, 1 - slot)

            x = jnp.concatenate([xbuf[slot, :, s, :] for s in range(xbuf.shape[2])], axis=1)
            xn_ref[...] = _rms(x, g_ref[...]).astype(BF16)
            acc_ref[...] = jnp.zeros_like(acc_ref)
    else:
        @pl.when(j == 0)
        def _():
            xn_ref[...] = _rms(x_ref[...], g_ref[...]).astype(BF16)
            acc_ref[...] = jnp.zeros_like(acc_ref)

    @pl.when(tv_ref[i] != 0)
    def _():
        xn = xn_ref[...]
        a = _dot(xn, w1_ref[...].astype(BF16))
        b = _dot(xn, w3_ref[...].astype(BF16))
        hid = (a * _sigmoid(a)) * b
        acc_ref[...] += _dot(hid.astype(BF16), w2_ref[...].astype(BF16))

    @pl.when(j == pl.num_programs(1) - 1)
    def _():
        y = acc_ref[...] * gate_ref[...]
        if residual:
            y = y + x_ref[...]
        o_ref[...] = y


def grouped_ffn(x, g, row_gate, tile_expert, tile_valid, w1, w3, w2, *, tm, tf, residual, row_idx=None):
    gather = row_idx is not None
    m = row_gate.shape[0]
    d = w1.shape[1]
    f = w1.shape[2]
    assert not (residual and gather)
    if gather:
        assert x.shape[1:] == (SUBLANES, d // SUBLANES)
        x_spec = pl.BlockSpec(memory_space=pl.ANY)
        scratch = [pltpu.VMEM((2, tm) + x.shape[1:], F32), pltpu.SemaphoreType.DMA((2,))]
    else:
        row_idx = jnp.zeros((1,), I32)
        x_spec = pl.BlockSpec((tm, d), lambda i, j, *_: (i, 0))
        scratch = []
    grid_spec = pltpu.PrefetchScalarGridSpec(
        num_scalar_prefetch=3,
        grid=(m // tm, f // tf),
        in_specs=[x_spec,
                  pl.BlockSpec((1, d), lambda i, j, *_: (0, 0)),
                  pl.BlockSpec((tm, 1), lambda i, j, *_: (i, 0)),
                  pl.BlockSpec((None, d, tf), lambda i, j, te, *_: (te[i], 0, j)),
                  pl.BlockSpec((None, d, tf), lambda i, j, te, *_: (te[i], 0, j)),
                  pl.BlockSpec((None, tf, d), lambda i, j, te, *_: (te[i], j, 0))],
        out_specs=pl.BlockSpec((tm, d), lambda i, j, *_: (i, 0)),
        scratch_shapes=[pltpu.VMEM((tm, d), BF16), pltpu.VMEM((tm, d), F32)] + scratch,
    )
    return pl.pallas_call(
        functools.partial(_ffn_kernel, residual=residual),
        grid_spec=grid_spec,
        out_shape=jax.ShapeDtypeStruct((m, d), F32),
        compiler_params=_cparams(2),
        name="grouped_ffn",
    )(tile_expert, tile_valid, row_idx, x, g.reshape(1, d), row_gate, w1, w3, w2)


def _ple_kernel(h_ref, g_ref, wg_ref, p_ref, wp_ref, gf_ref, o_ref, *, final):
    h = h_ref[...]
    gate = _sigmoid(_dot(_rms(h, g_ref[...]).astype(BF16), wg_ref[...]))
    proj = _dot(p_ref[...].astype(BF16), wp_ref[...])
    h2 = h + gate * proj
    o_ref[...] = _rms(h2, gf_ref[...]) if final else h2


def ple(h, g, wg, p, wp, gf, final):
    m, d = h.shape
    pd = p.shape[1]
    tm = _row_tile(m)
    row = lambda i: (i, 0)
    fixed = lambda i: (0, 0)
    return pl.pallas_call(
        functools.partial(_ple_kernel, final=final),
        grid=(m // tm,),
        in_specs=[pl.BlockSpec((tm, d), row), pl.BlockSpec((1, d), fixed),
                  pl.BlockSpec((d, d), fixed), pl.BlockSpec((tm, pd), row),
                  pl.BlockSpec((pd, d), fixed), pl.BlockSpec((1, d), fixed)],
        out_specs=pl.BlockSpec((tm, d), row),
        out_shape=jax.ShapeDtypeStruct((m, d), F32),
        compiler_params=_cparams(1),
        name="ple",
    )(h, g.reshape(1, d), wg, p, wp, gf.reshape(1, d))


def _bias_kernel(tab_ref, d_ref, o_ref):
    n = jnp.maximum(d_ref[...], 0)
    max_exact = N_BUCKETS // 2
    nf = jnp.maximum(n, 1).astype(F32)
    large = max_exact + (jnp.log(nf / max_exact) / math.log(MAX_DISTANCE / max_exact)
                         * (N_BUCKETS - max_exact)).astype(I32)
    large = jnp.minimum(large, N_BUCKETS - 1)
    bucket = jnp.where(n < max_exact, n, large)
    for h in range(H_NSA):
        acc = jnp.zeros(n.shape, F32)
        for k in range(N_BUCKETS):
            acc = jnp.where(bucket == k, tab_ref[k, h], acc)
        o_ref[h] = acc


def bias_lookup(table, dist):
    r, c = dist.shape
    tr = min(r, 128)
    tc = 512 if c % 512 == 0 else c
    return pl.pallas_call(
        _bias_kernel,
        grid=(r // tr, c // tc),
        in_specs=[pl.BlockSpec(memory_space=pltpu.SMEM),
                  pl.BlockSpec((tr, tc), lambda i, j: (i, j))],
        out_specs=pl.BlockSpec((H_NSA, tr, tc), lambda i, j: (0, i, j)),
        out_shape=jax.ShapeDtypeStruct((H_NSA, r, c), F32),
        compiler_params=_cparams(2),
        name="bias_lookup",
    )(table, dist)


def _swap_halves(x):
    lane = lax.broadcasted_iota(I32, x.shape, 1)
    return jnp.where(lane % HEAD_DIM < HEAD_DIM // 2,
                     pltpu.roll(x, LANES - HEAD_DIM // 2, 1), pltpu.roll(x, HEAD_DIM // 2, 1))


def _retention_prompt_kernel(q_ref, k_ref, v_ref, g_ref, cos_ref, sin_ref, gain_ref, intra_ref,
                             qdec_ref, kdec_ref, cdec_ref, y_ref, s_out_ref, s_ref):
    c = pl.program_id(1)

    @pl.when(c == 0)
    def _():
        s_ref[...] = jnp.zeros_like(s_ref)

    cos = cos_ref[...]
    sin = sin_ref[...]
    lane = lax.broadcasted_iota(I32, (RET_CHUNK, LANES), 1)
    left = lane < HEAD_DIM
    row = lax.broadcasted_iota(I32, (LANES, LANES), 0)
    col = lax.broadcasted_iota(I32, (LANES, LANES), 1)
    same_head = (row < HEAD_DIM) == (col < HEAD_DIM)
    ones_bd = jnp.where(same_head, 1.0, 0.0).astype(BF16)
    for p in range(H_RET // 2):
        sl = slice(p * LANES, (p + 1) * LANES)
        q = q_ref[:, sl]
        k = k_ref[:, sl]
        v = v_ref[:, sl].astype(BF16)
        qr = q * cos + _swap_halves(q) * sin
        kr = (k * cos + _swap_halves(k) * sin) * (HEAD_DIM ** -0.5)
        qb = qr.astype(BF16)
        q2 = jnp.concatenate([jnp.where(left, qr, 0.0), jnp.where(left, 0.0, qr)], axis=0).astype(BF16)
        inner = _dot_nt(q2, kr.astype(BF16)) * intra_ref[p]
        o2 = _dot(inner.astype(BF16), v)
        s_old = s_ref[p]
        cross = _dot(qb, s_old.astype(BF16)) * qdec_ref[:, sl]
        o = jnp.where(left, o2[:RET_CHUNK], o2[RET_CHUNK:]) + cross
        kd = (kr * kdec_ref[:, sl]).T.astype(BF16)
        s_ref[p] = s_old * cdec_ref[p] + jnp.where(same_head, _dot(kd, v), 0.0)
        ms = _dot3(o * o, ones_bd) * (1.0 / HEAD_DIM)
        gate = g_ref[:, sl]
        y = o * lax.rsqrt(ms + EPS) * gain_ref[:, sl] * (gate * _sigmoid(gate))
        y_ref[:, sl] = y.astype(y_ref.dtype)

    @pl.when(c == pl.num_programs(1) - 1)
    def _():
        for p in range(H_RET // 2):
            s = s_ref[p]
            s_out_ref[2 * p] = s[:HEAD_DIM, :HEAD_DIM]
            s_out_ref[2 * p + 1] = s[HEAD_DIM:, HEAD_DIM:]


def _retention_consts(c):
    log_g = jnp.log1p(-jnp.exp2(-5.0 - jnp.arange(H_RET, dtype=F32)))
    j = jnp.arange(c, dtype=F32)
    diff = j[:, None] - j[None, :]
    intra = jnp.where(diff >= 0, jnp.exp(jnp.maximum(diff, 0.0)[None] * log_g[:, None, None]), 0.0)
    q_decay = jnp.exp((j[:, None] + 1.0) * log_g[None, :])
    k_decay = jnp.exp((c - 1.0 - j[:, None]) * log_g[None, :])
    c_decay = jnp.exp(c * log_g)
    intra2 = intra.reshape(H_RET // 2, 2 * c, c)
    qdec = jnp.repeat(q_decay, HEAD_DIM, axis=1)
    kdec = jnp.repeat(k_decay, HEAD_DIM, axis=1)
    cdec = jnp.repeat(c_decay.reshape(H_RET // 2, 2), HEAD_DIM, axis=1)[:, :, None] * jnp.ones((1, 1, LANES), F32)
    return intra2, qdec, kdec, cdec


def _rotary_tables(pos):
    half = HEAD_DIM // 2
    inv = ROPE_BASE ** (-jnp.arange(half, dtype=F32) / half)
    ang = pos.astype(F32)[:, None] * inv[None, :]
    cos, sin = jnp.cos(ang), jnp.sin(ang)
    cos_t = jnp.concatenate([cos, cos, cos, cos], axis=1)
    sin_t = jnp.concatenate([-sin, sin, -sin, sin], axis=1)
    return cos_t, sin_t


def retention_prompt(z3, gain):
    b, t, _ = z3.shape
    c = RET_CHUNK
    cos_t, sin_t = _rotary_tables(jnp.arange(t))
    intra2, qdec, kdec, cdec = _retention_consts(c)
    blk = lambda col: pl.BlockSpec((None, c, RET_W), lambda bi, ci, col=col: (bi, ci, col))
    fixed2 = lambda shape: pl.BlockSpec(shape, lambda bi, ci: (0,) * len(shape))
    return pl.pallas_call(
        _retention_prompt_kernel,
        grid=(b, t // c),
        in_specs=[blk(COL_RQ // RET_W), blk(COL_RK // RET_W), blk(COL_RV // RET_W), blk(COL_RG // RET_W),
                  pl.BlockSpec((c, LANES), lambda bi, ci: (ci, 0)),
                  pl.BlockSpec((c, LANES), lambda bi, ci: (ci, 0)),
                  fixed2((1, RET_W)), fixed2((H_RET // 2, 2 * c, c)), fixed2((c, RET_W)),
                  fixed2((c, RET_W)), fixed2((H_RET // 2, LANES, LANES))],
        out_specs=[pl.BlockSpec((None, c, RET_W), lambda bi, ci: (bi, ci, 0)),
                   pl.BlockSpec((None, H_RET, HEAD_DIM, HEAD_DIM), lambda bi, ci: (bi, 0, 0, 0))],
        out_shape=[jax.ShapeDtypeStruct((b, t, RET_W), BF16),
                   jax.ShapeDtypeStruct((b, H_RET, HEAD_DIM, HEAD_DIM), F32)],
        scratch_shapes=[pltpu.VMEM((H_RET // 2, LANES, LANES), F32)],
        compiler_params=_cparams(2),
        name="retention_prompt",
    )(z3, z3, z3, z3, cos_t, sin_t, gain.reshape(1, RET_W), intra2, qdec, kdec, cdec)


def _retention_step_kernel(q_ref, k_ref, v_ref, g_ref, s_ref, cos_ref, sin_ref, gamma_ref, gain_ref,
                           y_ref, s_out_ref):
    half = HEAD_DIM // 2
    cos = cos_ref[...]
    sin = sin_ref[...]

    def rot(x):
        x1, x2 = x[:half], x[half:]
        return jnp.concatenate([x1 * cos - x2 * sin, x1 * sin + x2 * cos], axis=0)

    qr = rot(q_ref[...])
    kr = rot(k_ref[...]) * (HEAD_DIM ** -0.5)
    v = v_ref[...]
    gamma = gamma_ref[...]
    o = jnp.zeros_like(v)
    for d in range(HEAD_DIM):
        s_new = s_ref[d] * gamma + kr[d:d + 1] * v
        s_out_ref[d] = s_new
        o = o + qr[d:d + 1] * s_new
    ms = jnp.mean(o * o, axis=0, keepdims=True)
    gate = g_ref[...]
    y_ref[...] = o * lax.rsqrt(ms + EPS) * gain_ref[...] * (gate * _sigmoid(gate))


def retention_step(zt, state, layer, pos, gain):
    bs = zt.shape[1]
    half = HEAD_DIM // 2
    inv = ROPE_BASE ** (-jnp.arange(half, dtype=F32) / half)
    ang = jnp.full((1,), pos).astype(F32)[:, None] * inv[None, :]
    cos = jnp.broadcast_to(jnp.cos(ang).reshape(half, 1), (half, bs))
    sin = jnp.broadcast_to(jnp.sin(ang).reshape(half, 1), (half, bs))
    gamma = 1.0 - jnp.exp2(-5.0 - jnp.arange(H_RET, dtype=F32))
    gamma = jnp.broadcast_to(gamma.reshape(H_RET, 1, 1), (H_RET, 1, bs))
    gain_b = jnp.broadcast_to(gain.reshape(H_RET, HEAD_DIM, 1), (H_RET, HEAD_DIM, bs))
    head = lambda col: pl.BlockSpec((HEAD_DIM, bs), lambda h, col=col: (col // HEAD_DIM + h, 0))
    fixed = pl.BlockSpec((half, bs), lambda h: (0, 0))
    return pl.pallas_call(
        _retention_step_kernel,
        grid=(H_RET,),
        in_specs=[head(COL_RQ), head(COL_RK), head(COL_RV), head(COL_RG),
                  pl.BlockSpec((None, HEAD_DIM, HEAD_DIM, bs), lambda h: (layer * H_RET + h, 0, 0, 0)),
                  fixed, fixed,
                  pl.BlockSpec((None, 1, bs), lambda h: (h, 0, 0)),
                  pl.BlockSpec((None, HEAD_DIM, bs), lambda h: (h, 0, 0))],
        out_specs=[pl.BlockSpec((HEAD_DIM, bs), lambda h: (h, 0)),
                   pl.BlockSpec((None, HEAD_DIM, HEAD_DIM, bs), lambda h: (h, 0, 0, 0))],
        out_shape=[jax.ShapeDtypeStruct((RET_W, bs), F32),
                   jax.ShapeDtypeStruct((H_RET, HEAD_DIM, HEAD_DIM, bs), F32)],
        compiler_params=_cparams(1),
        name="retention_step",
    )(zt, zt, zt, zt, state, cos, sin, gamma, gain_b)


def _gelu_tanh(x):
    return 0.5 * x * (1.0 + jnp.tanh(math.sqrt(2.0 / math.pi) * (x + 0.044715 * (x * x * x))))


def _compress_kernel(*refs, n_pieces):
    x_refs = refs[1:1 + n_pieces]
    pe_ref, w1_ref, w2_ref, kc_ref, vc_ref, rows_ref = refs[1 + n_pieces:]
    width = x_refs[0].shape[1]
    n = n_pieces * width // CMP_STRIDE
    for kv, o_ref in enumerate((kc_ref, vc_ref)):
        for p, x_ref in enumerate(x_refs):
            for c in range(width // LANES):
                rows_ref[pl.ds(p * width + c * LANES, LANES), :] = (
                    x_ref[kv * LANES:(kv + 1) * LANES, c * LANES:(c + 1) * LANES].T)
        xk = jnp.concatenate([rows_ref[pl.ds(j, n, stride=CMP_STRIDE), :] for j in range(CMP_STRIDE)], axis=1)
        top = _dot((xk + pe_ref[kv, 0]).astype(BF16), w1_ref[kv, 0])
        bot = _dot((xk + pe_ref[kv, 1]).astype(BF16), w1_ref[kv, 1])
        pre = top + pltpu.roll(bot, n - 1, 0)
        hid = _gelu_tanh(pre).astype(BF16)
        o_ref[...] = jnp.concatenate(
            [_dot(hid[:, g * CMP_HIDDEN:(g + 1) * CMP_HIDDEN], w2_ref[kv]) for g in range(N_KV)], axis=1)


def _compress_weights(pe, w1, w2):
    pe_r = pe.reshape(2, 2, CMP_STRIDE, 1, HEAD_DIM)
    pe_t = jnp.broadcast_to(pe_r, (2, 2, CMP_STRIDE, N_KV, HEAD_DIM)).reshape(2, 2, 1, CMP_STRIDE * LANES)
    w1r = w1.reshape(2, 2, CMP_STRIDE, HEAD_DIM, CMP_HIDDEN)
    eye = jnp.eye(N_KV, dtype=w1.dtype)
    wbig = jnp.einsum('khjdc,gG->khjgdGc', w1r, eye)
    wbig = wbig.reshape(2, 2, CMP_STRIDE * LANES, N_KV * CMP_HIDDEN)
    return pe_t.astype(F32), wbig.astype(BF16), w2.astype(BF16)


def compress(src, idx, pe_t, w1b, w2b):
    bsz, n_pieces = idx.shape
    width = src.shape[2]
    n = n_pieces * width // CMP_STRIDE
    piece = lambda p: pl.BlockSpec((None, KV_W, width), lambda b, ix, p=p: (ix[b, p], 0, 0))
    fixed = lambda shape: pl.BlockSpec(shape, lambda b, ix: (0,) * len(shape))
    out = pl.BlockSpec((None, n, LANES), lambda b, ix: (b, 0, 0))
    grid_spec = pltpu.PrefetchScalarGridSpec(
        num_scalar_prefetch=1,
        grid=(bsz,),
        in_specs=[piece(p) for p in range(n_pieces)]
        + [fixed(pe_t.shape), fixed(w1b.shape), fixed(w2b.shape)],
        out_specs=[out, out],
        scratch_shapes=[pltpu.VMEM((n_pieces * width, LANES), F32)],
    )
    return pl.pallas_call(
        functools.partial(_compress_kernel, n_pieces=n_pieces),
        grid_spec=grid_spec,
        out_shape=[jax.ShapeDtypeStruct((bsz, n, LANES), F32)] * 2,
        compiler_params=_cparams(1),
        name="compress",
    )(idx, *([src] * n_pieces), pe_t, w1b, w2b)


def _masked_softmax(s, mask):
    s = jnp.where(mask, s, NEG)
    m = jnp.max(s, axis=-1, keepdims=True)
    e = jnp.where(mask, jnp.exp(s - m), 0.0)
    return e / jnp.maximum(jnp.sum(e, axis=-1, keepdims=True), TINY)


def _block_scores(imp, qpos, n_sel, axis=1):
    blk = lax.broadcasted_iota(I32, imp.shape, axis)
    cur = qpos // SEL_BLOCK
    forced = (blk == 0) | (blk == cur) | (blk == cur - 1)
    valid = blk * SEL_BLOCK <= qpos
    score = jnp.where(forced, FORCED_SCORE, jnp.where(valid, imp, -1.0))
    return jnp.where(blk < n_sel, score, -2.0)


def _select_top(score, n_top, axis=1):
    blk = lax.broadcasted_iota(I32, score.shape, axis)
    rank = jnp.zeros(score.shape, F32)
    for i in range(score.shape[axis]):
        one = score[:, i:i + 1] if axis == 1 else score[i:i + 1, :]
        beats = (one > score) | ((one == score) & (blk > i))
        rank = rank + jnp.where(beats, 1.0, 0.0)
    return jnp.where(rank < n_top, 1.0, 0.0)


def _attend_t(s, ok, m_prev, l_prev):
    s = jnp.where(ok, s, NEG)
    m_new = jnp.maximum(m_prev, jnp.max(s, axis=0, keepdims=True))
    alpha = jnp.exp(m_prev - m_new)
    p = jnp.exp(s - m_new)
    l_new = alpha * l_prev + jnp.sum(p, axis=0, keepdims=True)
    return m_new, l_new, alpha, p


def _nsa_prompt_kernel(q_ref, gate_ref, kc_ref, vct_ref, ks_ref, vst_ref, kw_ref, vwt_ref, bc_ref, bt_ref,
                       o_ref, qt_ref, pcs_ref, pick_ref, acc_ref, os_ref, s_ref, p_ref, *, n_cmp, n_sel):
    i = pl.program_id(1)
    qs = i * Q_BLOCK
    key_i = lax.broadcasted_iota(I32, (KV_CHUNK, Q_BLOCK), 0)
    tok_i = lax.broadcasted_iota(I32, (KV_CHUNK, Q_BLOCK), 1)
    row8 = lax.broadcasted_iota(I32, (H_NSA, Q_BLOCK), 0)
    zeros_half = jnp.zeros((HEAD_DIM, Q_BLOCK), F32)

    for k2 in range(H_NSA // 2):
        slab_t = q_ref[:, k2 * LANES:(k2 + 1) * LANES].T * (HEAD_DIM ** -0.5)
        for h in (2 * k2, 2 * k2 + 1):
            qh = slab_t[(h % 2) * HEAD_DIM:(h % 2 + 1) * HEAD_DIM]
            both = [qh, zeros_half] if h // R_GRP == 0 else [zeros_half, qh]
            qt_ref[:, h * Q_BLOCK:(h + 1) * Q_BLOCK] = jnp.concatenate(both, axis=0).astype(BF16)

    sig_t = _sigmoid(gate_ref[...]).T

    ncp = kc_ref.shape[0]
    kc = kc_ref[...].astype(BF16)
    n_c = lax.broadcasted_iota(I32, (ncp, Q_BLOCK), 0)
    t_c = lax.broadcasted_iota(I32, (ncp, Q_BLOCK), 1)
    ok_c = (n_c * CMP_STRIDE + (L_CMP - 1) <= qs + t_c) & (n_c < n_cmp)
    blk_pos = qs + lax.broadcasted_iota(I32, (pick_ref.shape[1], Q_BLOCK), 1)
    o_c = []
    for g in range(N_KV):
        vct = vct_ref[g * HEAD_DIM:(g + 1) * HEAD_DIM, :].astype(BF16)
        pcs = jnp.zeros((ncp, Q_BLOCK), F32)
        for r in range(R_GRP):
            h = g * R_GRP + r
            s = _dot(kc, qt_ref[:, h * Q_BLOCK:(h + 1) * Q_BLOCK]) + bc_ref[h]
            s = jnp.where(ok_c, s, NEG)
            e = jnp.where(ok_c, jnp.exp(s - jnp.max(s, axis=0, keepdims=True)), 0.0)
            p = e * (1.0 / jnp.maximum(jnp.sum(e, axis=0, keepdims=True), TINY))
            o_c.append(_dot(vct, p.astype(BF16)))
            pcs = pcs + p
        pcs_ref[...] = pcs
        per = SEL_BLOCK // CMP_STRIDE
        imp = pcs_ref[pl.ds(0, ncp // per, stride=per), :]
        for c in range(1, per):
            imp = imp + pcs_ref[pl.ds(c, ncp // per, stride=per), :]
        pick_ref[g] = _select_top(_block_scores(imp, blk_pos, n_sel, axis=0), min(N_SELECT, n_sel), axis=0)

    dmat = tok_i - key_i

    def stream(k_ref, vt_ref, n_chunks, chunk_of, mask_of):
        def scores(c, slot):
            off, _ = chunk_of(c)
            k = k_ref[pl.ds(off, KV_CHUNK), :].astype(BF16)
            for h in range(H_NSA):
                s_ref[slot, h] = _dot(k, qt_ref[:, h * Q_BLOCK:(h + 1) * Q_BLOCK])

        def values(c, slot, alpha_all):
            off, _ = chunk_of(c)
            for g in range(N_KV):
                vt = vt_ref[g * HEAD_DIM:(g + 1) * HEAD_DIM, pl.ds(off, KV_CHUNK)].astype(BF16)
                for h in range(g * R_GRP, (g + 1) * R_GRP):
                    acc_ref[h] = acc_ref[h] * alpha_all[h:h + 1] + _dot(vt, p_ref[slot, h])

        def softmax(c, slot, m_all, l_all):
            _, rel = chunk_of(c)
            ok_g = mask_of(c, rel)
            bias_idx = min(max(rel, 0), FAR_REL) if isinstance(rel, int) else jnp.clip(rel, 0, FAR_REL)
            alpha_all = jnp.zeros((H_NSA, Q_BLOCK), F32)
            for h in range(H_NSA):
                s = s_ref[slot, h] + bt_ref[bias_idx, h]
                m_new, l_new, alpha, p = _attend_t(s, ok_g[h // R_GRP], m_all[h:h + 1], l_all[h:h + 1])
                p_ref[slot, h] = p.astype(BF16)
                m_all = jnp.where(row8 == h, m_new, m_all)
                l_all = jnp.where(row8 == h, l_new, l_all)
                alpha_all = jnp.where(row8 == h, alpha, alpha_all)
            return m_all, l_all, alpha_all

        def stage(c, slot, carry, prefetch=True):
            m_all, l_all, alpha_prev = carry
            if prefetch:
                scores(c + 1, 1 - slot)
            values(c - 1, 1 - slot, alpha_prev)
            return softmax(c, slot, m_all, l_all)

        acc_ref[...] = jnp.zeros_like(acc_ref)
        p_ref[1] = jnp.zeros(p_ref.shape[1:], BF16)
        scores(0, 0)
        carry = (jnp.full((H_NSA, Q_BLOCK), NEG, F32), jnp.zeros((H_NSA, Q_BLOCK), F32),
                 jnp.ones((H_NSA, Q_BLOCK), F32))
        if isinstance(n_chunks, int):
            for c in range(n_chunks):
                carry = stage(c, c % 2, carry, prefetch=c + 1 < n_chunks)
            last = n_chunks - 1
        else:
            pairs = (n_chunks + 1) // 2
            carry = lax.fori_loop(0, pairs, lambda t, cr: stage(2 * t + 1, 1, stage(2 * t, 0, cr)), carry)
            last = 2 * pairs - 1
        m_all, l_all, alpha_last = carry
        values(last, last % 2 if isinstance(last, int) else 1, alpha_last)
        return m_all, l_all

    def sel_chunk(c):
        return pl.multiple_of(jnp.clip(c, 0, i) * KV_CHUNK, KV_CHUNK), i - c

    def sel_mask(c, rel):
        causal = rel * KV_CHUNK + dmat >= 0
        per = KV_CHUNK // SEL_BLOCK
        first = jnp.minimum(c, i) * per
        ok_g = []
        for g in range(N_KV):
            picked = jnp.concatenate(
                [jnp.broadcast_to(pick_ref[g, pl.ds(first + b, 1), :], (SEL_BLOCK, Q_BLOCK)) for b in range(per)],
                axis=0)
            ok_g.append(causal & (picked > 0.5))
        return ok_g

    m_s, l_s = stream(ks_ref, vst_ref, i + 1, sel_chunk, sel_mask)
    for h in range(H_NSA):
        os_ref[h] = jnp.where(m_s[h:h + 1] > 0.5 * NEG, acc_ref[h] / jnp.maximum(l_s[h:h + 1], TINY), 0.0)

    n_back = WINDOW // KV_CHUNK

    def win_chunk(c):
        return pl.multiple_of(jnp.clip(i - n_back + c, 0, i) * KV_CHUNK, KV_CHUNK), n_back - c

    def win_mask(c, rel):
        dist = rel * KV_CHUNK + dmat + jnp.where(i - n_back + c < 0, WINDOW, 0)
        ok = (dist >= 0) & (dist < WINDOW)
        return [ok, ok]

    m_w, l_w = stream(kw_ref, vwt_ref, n_back + 1, win_chunk, win_mask)

    for k2 in range(H_NSA // 2):
        pair = []
        for h in (2 * k2, 2 * k2 + 1):
            o_w = jnp.where(m_w[h:h + 1] > 0.5 * NEG, acc_ref[h] / jnp.maximum(l_w[h:h + 1], TINY), 0.0)
            pair.append(sig_t[3 * h:3 * h + 1] * o_c[h] + sig_t[3 * h + 1:3 * h + 2] * os_ref[h]
                        + sig_t[3 * h + 2:3 * h + 3] * o_w)
        o_ref[:, k2 * LANES:(k2 + 1) * LANES] = jnp.concatenate(pair, axis=0).T.astype(o_ref.dtype)


def _selection_consts(n_rows_cmp, n_keys):
    n = np.arange(n_rows_cmp)
    blk = np.arange(SEL_LANES)
    pool = (n[:, None] // (SEL_BLOCK // CMP_STRIDE) == blk[None, :]).astype(np.float32)
    key = np.arange(n_keys)
    expand = (blk[:, None] == key[None, :] // SEL_BLOCK).astype(np.float32)
    return jnp.asarray(pool, BF16), jnp.asarray(expand, BF16)


def nsa_prompt(z3, kvt, kc, vct, bias_ct, bias_tt):
    b, t, _ = z3.shape
    n_sub = t // CMP_STRIDE
    n_sel = -(-t // SEL_BLOCK)
    assert kc.shape[1] == n_sub and n_sub == n_sel * (SEL_BLOCK // CMP_STRIDE) and n_sel % SUBLANES == 0
    fixed = lambda shape: pl.BlockSpec(shape, lambda bi, i: (0,) * len(shape))
    v_rows = lambda branch: (branch * KV_W + N_KV * HEAD_DIM) // LANES
    return pl.pallas_call(
        functools.partial(_nsa_prompt_kernel, n_cmp=n_sub - 1, n_sel=n_sel),
        grid=(b, t // Q_BLOCK),
        in_specs=[pl.BlockSpec((None, Q_BLOCK, NSA_QW), lambda bi, i: (bi, i, COL_NQ // NSA_QW)),
                  pl.BlockSpec((None, Q_BLOCK, LANES), lambda bi, i: (bi, i, COL_NG // LANES)),
                  pl.BlockSpec((None, n_sub, LANES), lambda bi, i: (bi, 0, 0)),
                  pl.BlockSpec((None, LANES, n_sub), lambda bi, i: (bi, 0, 0)),
                  pl.BlockSpec((None, t, LANES), lambda bi, i: (bi, 0, COL_KVS // LANES)),
                  pl.BlockSpec((None, LANES, t), lambda bi, i: (bi, v_rows(1), 0)),
                  pl.BlockSpec((None, t, LANES), lambda bi, i: (bi, 0, COL_KVW // LANES)),
                  pl.BlockSpec((None, LANES, t), lambda bi, i: (bi, v_rows(2), 0)),
                  pl.BlockSpec((H_NSA, n_sub, Q_BLOCK), lambda bi, i: (0, 0, i)),
                  fixed(bias_tt.shape)],
        out_specs=pl.BlockSpec((None, Q_BLOCK, NSA_QW), lambda bi, i: (bi, i, 0)),
        out_shape=jax.ShapeDtypeStruct((b, t, NSA_QW), BF16),
        scratch_shapes=[pltpu.VMEM((LANES, H_NSA * Q_BLOCK), BF16),
                        pltpu.VMEM((n_sub, Q_BLOCK), F32),
                        pltpu.VMEM((N_KV, n_sel, Q_BLOCK), F32),
                        pltpu.VMEM((H_NSA, HEAD_DIM, Q_BLOCK), F32),
                        pltpu.VMEM((H_NSA, HEAD_DIM, Q_BLOCK), F32),
                        pltpu.VMEM((2, H_NSA, KV_CHUNK, Q_BLOCK), F32),
                        pltpu.VMEM((2, H_NSA, KV_CHUNK, Q_BLOCK), BF16)],
        compiler_params=_cparams(2),
        name="nsa_prompt",
    )(z3, z3, kc, vct, z3, kvt, z3, kvt, bias_ct, bias_tt)


def _nsa_step_kernel(pt_ref, z_ref, kc_ref, vc_ref, *rest, n_pages, per_step, **static):
    del pt_ref
    pages = rest[:per_step * n_pages]
    win_ref, bcs_ref, bss_ref, bws_ref, pool_ref, exp_ref, o_ref, wo_ref = rest[per_step * n_pages:]
    for u in range(per_step):
        _nsa_step_one(z_ref.at[u], kc_ref.at[u], vc_ref.at[u], pages[u * n_pages:(u + 1) * n_pages], win_ref.at[u],
                      bcs_ref, bss_ref, bws_ref, pool_ref, exp_ref, o_ref.at[u], wo_ref.at[u], **static)


def _nsa_step_one(z_ref, kc_ref, vc_ref, pages, win_ref, bcs_ref, bss_ref, bws_ref, pool_ref, exp_ref, o_ref, wo_ref,
                  *, past, wbuf, n_cmp, n_sel):
    z = z_ref[...]
    row8 = lax.broadcasted_iota(I32, (H_NSA, LANES), 0)
    lane8 = lax.broadcasted_iota(I32, (H_NSA, LANES), 1)
    left8 = lane8 < HEAD_DIM
    qbd = jnp.zeros((H_NSA, LANES), F32)
    for h in range(H_NSA):
        slab = jnp.broadcast_to(z[:, COL_NQ + (h // 2) * LANES:COL_NQ + (h // 2 + 1) * LANES], (H_NSA, LANES))
        g = h // R_GRP
        if h % 2 != g:
            slab = pltpu.roll(slab, HEAD_DIM, 1)
        keep = (row8 == h) & (left8 if g == 0 else jnp.logical_not(left8))
        qbd = jnp.where(keep, slab, qbd)
    qbd = (qbd * (HEAD_DIM ** -0.5)).astype(BF16)

    ncp = kc_ref.shape[0]
    n_c = lax.broadcasted_iota(I32, (H_NSA, ncp), 1)
    s_c = _dot_nt(qbd, kc_ref[...].astype(BF16)) + bcs_ref[...]
    p_c = _masked_softmax(s_c, (n_c * CMP_STRIDE + (L_CMP - 1) <= past) & (n_c < n_cmp))
    o_c = _dot(p_c.astype(BF16), vc_ref[...].astype(BF16))
    pc0 = jnp.sum(p_c[0:R_GRP], axis=0, keepdims=True)
    pc1 = jnp.sum(p_c[R_GRP:], axis=0, keepdims=True)
    rowc = lax.broadcasted_iota(I32, (H_NSA, ncp), 0)
    pcs = jnp.where(rowc < R_GRP, jnp.broadcast_to(pc0, (H_NSA, ncp)), jnp.broadcast_to(pc1, (H_NSA, ncp)))
    imp = _dot3(pcs, pool_ref[...])
    sel = _select_top(_block_scores(imp, jnp.full(imp.shape, past, I32), n_sel), min(N_SELECT, n_sel))
    picked = _dot(sel.astype(BF16), exp_ref[...])

    c128 = lax.broadcasted_iota(I32, (LANES, KV_CHUNK), 1)

    def new_col(col):
        return jnp.broadcast_to(z[:, col:col + LANES], (KV_CHUNK, LANES)).T

    def first_col(tile):
        return jnp.where(c128 == 0, tile, 0.0).astype(BF16)

    ks = [pg[0:LANES, :].astype(BF16) for pg in pages] + [first_col(new_col(COL_KVS))]
    vs = [pg[LANES:2 * LANES, :].astype(BF16) for pg in pages] + [first_col(new_col(COL_KVS + LANES))]
    s_s = jnp.concatenate([_dot(qbd, k) for k in ks], axis=1) + bss_ref[...]
    key = lax.broadcasted_iota(I32, s_s.shape, 1)
    p_s = _masked_softmax(s_s, (picked > 0.5) & (key <= past)).astype(BF16)
    o_s = _dot_nt(p_s[:, 0:KV_CHUNK], vs[0])
    for c in range(1, len(vs)):
        o_s = o_s + _dot_nt(p_s[:, c * KV_CHUNK:(c + 1) * KV_CHUNK], vs[c])

    kw_new, vw_new = new_col(COL_KVW), new_col(COL_KVW + LANES)
    win = win_ref[...]
    s_w = jnp.concatenate([_dot(qbd, win[0:LANES].astype(BF16)), _dot(qbd, first_col(kw_new))], axis=1) + bws_ref[...]
    colw = lax.broadcasted_iota(I32, s_w.shape, 1)
    p_w = _masked_softmax(s_w, (colw <= wbuf) & (wbuf - colw < WINDOW)).astype(BF16)
    o_w = _dot_nt(p_w[:, 0:wbuf], win[LANES:].astype(BF16)) + _dot_nt(p_w[:, wbuf:], first_col(vw_new))

    shifted = pltpu.roll(win, wbuf - 1, 1)
    new_kv = jnp.concatenate([kw_new, vw_new], axis=0)
    last = jnp.where(lax.broadcasted_iota(I32, (KV_W, LANES), 1) == LANES - 1, new_kv, shifted[:, wbuf - LANES:])
    wo_ref[:, 0:wbuf - LANES] = shifted[:, 0:wbuf - LANES]
    wo_ref[:, wbuf - LANES:] = last

    sig = jnp.broadcast_to(_sigmoid(z[:, COL_NG:COL_NG + LANES]), (H_NSA, LANES))

    def gate_col(c):
        return jnp.sum(jnp.where(lane8 == row8 * 3 + c, sig, 0.0), axis=1, keepdims=True)

    o = gate_col(0) * o_c + gate_col(1) * o_s + gate_col(2) * o_w
    fix = jnp.where((row8 % 2) != (row8 // R_GRP), pltpu.roll(o, HEAD_DIM, 1), o)
    left1 = left8[0:1]
    o_ref[...] = jnp.concatenate(
        [jnp.where(left1, fix[2 * k2:2 * k2 + 1], fix[2 * k2 + 1:2 * k2 + 2]) for k2 in range(H_NSA // 2)], axis=1)


def nsa_step(zs3, kc, vc, sel_cache, win_cache, page_idx, win_row0, bias_cs, bias_ss, bias_ws, past):
    bs = zs3.shape[0]
    n_pages = page_idx.shape[1]
    wbuf = win_cache.shape[2]
    n_sub = (past + 1) // CMP_STRIDE
    n_sel = -(-(past + 1) // SEL_BLOCK)
    n_keys = (n_pages + 1) * PAGE_SIZE
    assert n_sel <= SEL_LANES and PAGE_SIZE == KV_CHUNK and wbuf % KV_CHUNK == 0 and kc.shape[1] == n_sub
    pool, expand = _selection_consts(n_sub, n_keys)
    per = 2 if bs % 2 == 0 and win_row0 % 2 == 0 else 1
    fixed = lambda shape: pl.BlockSpec(shape, lambda b, pt: (0,) * len(shape))
    page = lambda u, p: pl.BlockSpec((None, KV_W, PAGE_SIZE), lambda b, pt, u=u, p=p: (pt[per * b + u, p], 0, 0))
    grid_spec = pltpu.PrefetchScalarGridSpec(
        num_scalar_prefetch=1,
        grid=(bs // per,),
        in_specs=[pl.BlockSpec((per, 1, N_IN_PAD), lambda b, pt: (b, 0, 0)),
                  pl.BlockSpec((per, n_sub, LANES), lambda b, pt: (b, 0, 0)),
                  pl.BlockSpec((per, n_sub, LANES), lambda b, pt: (b, 0, 0))]
        + [page(u, p) for u in range(per) for p in range(n_pages)]
        + [pl.BlockSpec((per, KV_W, wbuf), lambda b, pt: (win_row0 // per + b, 0, 0)),
           fixed(bias_cs.shape), fixed(bias_ss.shape), fixed(bias_ws.shape), fixed(pool.shape), fixed(expand.shape)],
        out_specs=[pl.BlockSpec((per, 1, NSA_QW), lambda b, pt: (b, 0, 0)),
                   pl.BlockSpec((per, KV_W, wbuf), lambda b, pt: (b, 0, 0))],
    )
    return pl.pallas_call(
        functools.partial(_nsa_step_kernel, n_pages=n_pages, per_step=per, past=past, wbuf=wbuf, n_cmp=n_sub - 1,
                          n_sel=n_sel),
        grid_spec=grid_spec,
        out_shape=[jax.ShapeDtypeStruct((bs, 1, NSA_QW), F32), jax.ShapeDtypeStruct((bs, KV_W, wbuf), F32)],
        compiler_params=_cparams(1),
        name="nsa_step",
    )(page_idx, zs3, kc, vc, *([sel_cache] * (per * n_pages)), win_cache, bias_cs, bias_ss, bias_ws, pool, expand)


def _route(logits, tm):
    m = logits.shape[0]
    top_v, top_i = lax.top_k(logits, TOP_K)
    top_w = jax.nn.softmax(top_v, axis=-1)
    slot_e = top_i.reshape(-1)
    onehot = (slot_e[:, None] == jnp.arange(N_EXPERTS)[None, :]).astype(I32)
    before = jnp.cumsum(onehot, axis=0) - onehot
    rank = jnp.sum(before * onehot, axis=1)
    count = jnp.sum(onehot, axis=0)
    padded = -(-count // tm) * tm
    start = jnp.cumsum(padded) - padded
    pos = start[slot_e] + rank
    n_rows = (-(-(m * TOP_K) // tm) + N_EXPERTS) * tm
    row_slot = jnp.full((n_rows,), -1, I32).at[pos].set(jnp.arange(m * TOP_K, dtype=I32))
    slot = jnp.maximum(row_slot, 0)
    row_token = slot // TOP_K
    row_gate = jnp.where(row_slot >= 0, top_w.reshape(-1)[slot], 0.0)
    tile_start = jnp.arange(n_rows // tm, dtype=I32) * tm
    tile_expert = jnp.clip(jnp.searchsorted(jnp.cumsum(padded), tile_start, side='right'), 0, N_EXPERTS - 1)
    tile_valid = (tile_start < jnp.sum(padded)).astype(I32)
    return pos.reshape(m, TOP_K), row_token, row_gate.reshape(n_rows, 1), tile_expert.astype(I32), tile_valid


def _dense_ffn(h, g, w1, w3, w2):
    m = h.shape[0]
    tm = _row_tile(m)
    ones = jnp.ones((m // tm,), I32)
    return grouped_ffn(h, g, jnp.ones((m, 1), F32), jnp.zeros((m // tm,), I32), ones,
                       w1[None], w3[None], w2[None], tm=tm, tf=w1.shape[1] // 2, residual=True)


def kernel(x_prompt, x_sample, p_prompt, p_sample, state_ret, cache_win_kv, cache_cmp_kv, cache_sel_kv, page_table, rel_bias, norm_mix, w_in, ret_gain, cmp_pe, cmp_w1, cmp_w2, w_out, norm_ffn, ffn_w1, ffn_w3, ffn_w2, router, moe_w1, moe_w3, moe_w2, ple_norm, ple_gate, ple_proj, norm_final):
    b_p, t_p, d = x_prompt.shape
    b_s = x_sample.shape[0]
    depth = w_in.shape[0]
    n_pages = page_table.shape[1]
    past = n_pages * PAGE_SIZE
    n_pool = cache_cmp_kv.shape[1]
    w_buf = cache_win_kv.shape[2]
    assert x_sample.shape[1] == 1 and t_p % Q_BLOCK == 0 and t_p >= w_buf and past % CMP_STRIDE == 0
    row = (2, N_KV, HEAD_DIM)
    m_p = b_p * t_p

    ar = lambda n: jnp.arange(n, dtype=I32)
    n_sub_p = t_p // CMP_STRIDE
    bias_ct = bias_lookup(rel_bias, ar(t_p)[None, :] - (ar(n_sub_p)[:, None] * CMP_STRIDE + L_CMP - 1))
    rel_t = (ar(FAR_REL + 1)[:, None, None] * KV_CHUNK + ar(Q_BLOCK)[None, None, :] - ar(KV_CHUNK)[None, :, None])
    bias_tt = bias_lookup(rel_bias, rel_t.reshape((FAR_REL + 1) * KV_CHUNK, Q_BLOCK))
    bias_tt = bias_tt.reshape(H_NSA, FAR_REL + 1, KV_CHUNK, Q_BLOCK).transpose(1, 0, 2, 3)
    n_sub_s = (past + 1) // CMP_STRIDE
    rows8 = jnp.zeros((SUBLANES, 1), I32)
    bias_cs = bias_lookup(rel_bias, rows8 + (past - (ar(n_sub_s)[None, :] * CMP_STRIDE + L_CMP - 1)))[:, 0]
    bias_ss = bias_lookup(rel_bias, rows8 + (past - ar((n_pages + 1) * PAGE_SIZE)[None, :]))[:, 0]
    bias_ws = bias_lookup(rel_bias, rows8 + (w_buf - ar(w_buf + KV_CHUNK)[None, :]))[:, 0]

    def rows_last(x):
        return jnp.moveaxis(x, 2, -1).reshape(x.shape[0] * x.shape[1], KV_W, x.shape[2])

    def rows_first(xt, lead):
        return jnp.moveaxis(xt.reshape((lead,) + row + (xt.shape[-1],)), -1, 1)

    cmp_cache = rows_last(cache_cmp_kv)
    sel_cache = rows_last(cache_sel_kv)
    win_cache = rows_last(cache_win_kv)
    state_t = jnp.moveaxis(state_ret, 1, -1).reshape(depth * H_RET, HEAD_DIM, HEAD_DIM, b_s)

    h_p = x_prompt.reshape(m_p, d)
    h_s = x_sample.reshape(b_s, d)
    rp, rs, wp, ws, cp, cs, sp, ss = [], [], [], [], [], [], [], []
    for i in range(depth):
        w_in_b = jnp.pad(w_in[i], ((0, 0), (0, N_IN_PAD - N_IN))).astype(BF16)
        w_out_b = w_out[i].astype(BF16)
        pe_t, cw1, cw2 = _compress_weights(cmp_pe[i], cmp_w1[i], cmp_w2[i])

        w_kvt = w_in[i][:, COL_KVC:COL_KVC + 3 * KV_W].T.astype(BF16)
        z, kvt = rms_matmul_kvt(h_p, norm_mix[i], w_in_b, w_kvt, b_p)
        z3 = z.reshape(b_p, t_p, N_IN_PAD)
        ret_y, ret_state = retention_prompt(z3, ret_gain[i])
        kc, vc = compress(kvt, ar(b_p)[:, None], pe_t, cw1, cw2)
        nsa_y = nsa_prompt(z3, kvt, kc, jnp.swapaxes(vc, 1, 2), bias_ct, bias_tt)
        h_p = out_proj(h_p, ret_y.reshape(m_p, RET_W), nsa_y.reshape(m_p, NSA_QW), w_out_b)
        rp.append(ret_state)
        cp.append(rows_first(kvt[:, 0:KV_W], b_p))
        sp.append(rows_first(kvt[:, KV_W:2 * KV_W], b_p))
        wp.append(rows_first(kvt[:, 2 * KV_W:, t_p - w_buf:], b_p))

        zs = rms_matmul(h_s, norm_mix[i], w_in_b)
        zst = zs.T
        ret_yst, state_s = retention_step(zst, state_t, i, past, ret_gain[i])
        page_idx = page_table + i * n_pool
        grp = 4 if b_s % 4 == 0 else 1
        kc_s, vc_s = compress(cmp_cache, page_idx.reshape(b_s // grp, grp * n_pages), pe_t, cw1, cw2)
        kc_s, vc_s = kc_s.reshape(b_s, n_sub_s, LANES), vc_s.reshape(b_s, n_sub_s, LANES)
        nsa_ys, win_s = nsa_step(zs.reshape(b_s, 1, N_IN_PAD), kc_s, vc_s, sel_cache, win_cache, page_idx, i * b_s,
                                 bias_cs, bias_ss, bias_ws, past)
        h_s = out_proj(h_s, ret_yst.T, nsa_ys.reshape(b_s, NSA_QW), w_out_b)
        rs.append(jnp.moveaxis(state_s, -1, 0))
        ws.append(rows_first(win_s, b_s))
        cs.append(rows_first(zst[None, COL_KVC:COL_KVC + KV_W], 1).reshape((b_s, 1) + row))
        ss.append(rows_first(zst[None, COL_KVS:COL_KVS + KV_W], 1).reshape((b_s, 1) + row))

        j = i // 2
        if i % 2 == 0:
            w1, w3, w2 = ffn_w1[j].astype(BF16), ffn_w3[j].astype(BF16), ffn_w2[j].astype(BF16)
            h_p = _dense_ffn(h_p, norm_ffn[i], w1, w3, w2)
            h_s = _dense_ffn(h_s, norm_ffn[i], w1, w3, w2)
        else:
            router_w = jnp.pad(router[j], ((0, 0), (0, LANES - N_EXPERTS)))
            tf = moe_w1.shape[3] // 7
            logits_p = router_logits(h_p, norm_ffn[i], router_w)
            tm = 1024
            pos, row_token, row_gate, tile_expert, tile_valid = _route(logits_p[:, :N_EXPERTS], tm)
            y_rows = grouped_ffn(h_p.reshape(m_p, SUBLANES, d // SUBLANES), norm_ffn[i], row_gate, tile_expert,
                                 tile_valid, moe_w1[j], moe_w3[j], moe_w2[j], tm=tm, tf=tf, residual=False,
                                 row_idx=row_token)
            h_p = h_p + y_rows[pos[:, 0]] + y_rows[pos[:, 1]]
            logits_s = router_logits(h_s, norm_ffn[i], router_w)
            top_v, top_i = lax.top_k(logits_s[:, :N_EXPERTS], TOP_K)
            top_w = jax.nn.softmax(top_v, axis=-1)
            gates = jnp.sum(jax.nn.one_hot(top_i, N_EXPERTS, dtype=F32) * top_w[..., None], axis=-2)
            y_e = grouped_ffn(jnp.tile(h_s, (N_EXPERTS, 1)), norm_ffn[i], gates.T.reshape(N_EXPERTS * b_s, 1),
                              jnp.arange(N_EXPERTS, dtype=I32), jnp.ones((N_EXPERTS,), I32),
                              moe_w1[j], moe_w3[j], moe_w2[j], tm=b_s, tf=tf, residual=False)
            h_s = h_s + jnp.sum(y_e.reshape(N_EXPERTS, b_s, d), axis=0)

        wg, wpj = ple_gate[i].astype(BF16), ple_proj[i].astype(BF16)
        h_p = ple(h_p, ple_norm[i], wg, p_prompt[i].reshape(m_p, P_DIM), wpj, norm_final, i == depth - 1)
        h_s = ple(h_s, ple_norm[i], wg, p_sample[i].reshape(b_s, P_DIM), wpj, norm_final, i == depth - 1)

    return (h_p.reshape(b_p, t_p, d), h_s.reshape(b_s, 1, d), jnp.stack(rp), jnp.stack(rs), jnp.stack(wp),
            jnp.stack(ws), jnp.stack(cp), jnp.stack(cs), jnp.stack(sp), jnp.stack(ss))
```

```python
import functools
import math

import numpy as np
import jax
import jax.numpy as jnp
from jax import lax
from jax.experimental import pallas as pl
from jax.experimental.pallas import tpu as pltpu

F32 = jnp.float32
BF16 = jnp.bfloat16
I32 = jnp.int32

LANES = 128
SUBLANES = 8
VMEM_LIMIT_BYTES = 56 * 1024 * 1024

D_MODEL = 1024
HEAD_DIM = 64
H_RET = 8
H_NSA = 8
N_KV = 2
R_GRP = H_NSA // N_KV
RET_W = H_RET * HEAD_DIM
NSA_QW = H_NSA * HEAD_DIM
KV_W = 2 * N_KV * HEAD_DIM
N_IN = 4 * RET_W + NSA_QW + 3 * KV_W + 3 * H_NSA
N_IN_PAD = 3584
COL_RQ, COL_RK, COL_RV, COL_RG = 0, RET_W, 2 * RET_W, 3 * RET_W
COL_NQ = 4 * RET_W
COL_KVC = COL_NQ + NSA_QW
COL_KVS = COL_KVC + KV_W
COL_KVW = COL_KVS + KV_W
COL_NG = COL_KVW + KV_W
RET_CHUNK = 128
CMP_STRIDE = 16
L_CMP = 2 * CMP_STRIDE
CMP_HIDDEN = 2 * HEAD_DIM
SEL_BLOCK = 64
N_SELECT = 16
WINDOW = 512
Q_BLOCK = 128
N_BUCKETS = 32
MAX_DISTANCE = 128
N_EXPERTS = 8
TOP_K = 2
P_DIM = 256
PAGE_SIZE = 128
ROPE_BASE = 10000.0
EPS = 1e-6
FORCED_SCORE = 1e4
NEG = -1e30
TINY = float(np.finfo(np.float32).tiny)
LOG2E = math.log2(math.e)
SEL_LANES = 64
FAR_REL = 2
KV_CHUNK = 128


def _cparams(n_grid):
    return pltpu.CompilerParams(dimension_semantics=("arbitrary",) * n_grid,
                                vmem_limit_bytes=VMEM_LIMIT_BYTES)


def _dot(a, b):
    return jnp.dot(a, b, preferred_element_type=F32)


def _dot_nt(a, b):
    return lax.dot_general(a, b, (((1,), (1,)), ((), ())), preferred_element_type=F32)


def _split3(x):
    hi = x.astype(BF16)
    r = x - hi.astype(F32)
    mid = r.astype(BF16)
    lo = (r - mid.astype(F32)).astype(BF16)
    return hi, mid, lo


def _dot3(x, b01):
    hi, mid, lo = _split3(x)
    return _dot(hi, b01) + _dot(mid, b01) + _dot(lo, b01)


def _dot6(x, w):
    xh, xm, xl = _split3(x)
    wh, wm, wl = _split3(w)
    return (_dot(xh, wh) + _dot(xh, wm) + _dot(xm, wh)
            + _dot(xm, wm) + _dot(xh, wl) + _dot(xl, wh))


def _sigmoid(x):
    return 1.0 / (1.0 + jnp.exp(-x))


def _rms(x, g):
    return x * lax.rsqrt(jnp.mean(x * x, axis=-1, keepdims=True) + EPS) * g


def _row_tile(m):
    for t in (512, 256, 128, 64, 32, 16, 8):
        if m % t == 0:
            return t
    raise ValueError(f"row count {m} not a multiple of 8")


def _rms_matmul_kernel(x_ref, g_ref, w_ref, o_ref):
    o_ref[...] = _dot(_rms(x_ref[...], g_ref[...]).astype(BF16), w_ref[...])


def rms_matmul(x, g, w):
    m, d = x.shape
    n = w.shape[1]
    tm = min(_row_tile(m), 256)
    return pl.pallas_call(
        _rms_matmul_kernel,
        grid=(m // tm,),
        in_specs=[pl.BlockSpec((tm, d), lambda i: (i, 0)),
                  pl.BlockSpec((1, d), lambda i: (0, 0)),
                  pl.BlockSpec((d, n), lambda i: (0, 0))],
        out_specs=pl.BlockSpec((tm, n), lambda i: (i, 0)),
        out_shape=jax.ShapeDtypeStruct((m, n), F32),
        compiler_params=_cparams(1),
    )(x, g.reshape(1, d), w)


def _rms_matmul_kvt_kernel(x_ref, g_ref, w_ref, wt_ref, o_ref, ot_ref):
    xn = _rms(x_ref[...], g_ref[...]).astype(BF16)
    o_ref[...] = _dot(xn, w_ref[...])
    ot_ref[...] = _dot_nt(wt_ref[...], xn)


def rms_matmul_kvt(x, g, w, wt, bsz):
    m, d = x.shape
    n = w.shape[1]
    nt = wt.shape[0]
    t = m // bsz
    tm = _row_tile(t)
    per = t // tm
    return pl.pallas_call(
        _rms_matmul_kvt_kernel,
        grid=(m // tm,),
        in_specs=[pl.BlockSpec((tm, d), lambda i: (i, 0)),
                  pl.BlockSpec((1, d), lambda i: (0, 0)),
                  pl.BlockSpec((d, n), lambda i: (0, 0)),
                  pl.BlockSpec((nt, d), lambda i: (0, 0))],
        out_specs=[pl.BlockSpec((tm, n), lambda i: (i, 0)),
                   pl.BlockSpec((None, nt, tm), lambda i: (i // per, 0, i % per))],
        out_shape=[jax.ShapeDtypeStruct((m, n), F32), jax.ShapeDtypeStruct((bsz, nt, t), F32)],
        compiler_params=_cparams(1),
        name="rms_matmul_kvt",
    )(x, g.reshape(1, d), w, wt)


def _router_kernel(x_ref, g_ref, w_ref, o_ref):
    o_ref[...] = _dot6(_rms(x_ref[...], g_ref[...]), w_ref[...])


def router_logits(x, g, w):
    m, d = x.shape
    n = w.shape[1]
    tm = min(_row_tile(m), 256)
    return pl.pallas_call(
        _router_kernel,
        grid=(m // tm,),
        in_specs=[pl.BlockSpec((tm, d), lambda i: (i, 0)),
                  pl.BlockSpec((1, d), lambda i: (0, 0)),
                  pl.BlockSpec((d, n), lambda i: (0, 0))],
        out_specs=pl.BlockSpec((tm, n), lambda i: (i, 0)),
        out_shape=jax.ShapeDtypeStruct((m, n), F32),
        compiler_params=_cparams(1),
        name="router",
    )(x, g.reshape(1, d), w)


def _outproj_kernel(h_ref, a_ref, b_ref, w_ref, o_ref):
    ka = a_ref.shape[1]
    o_ref[...] = (h_ref[...] + _dot(a_ref[...].astype(BF16), w_ref[0:ka, :])
                  + _dot(b_ref[...].astype(BF16), w_ref[ka:, :]))


def out_proj(h, a, b, w):
    m, d = h.shape
    ka, kb = a.shape[1], b.shape[1]
    tm = _row_tile(m)
    return pl.pallas_call(
        _outproj_kernel,
        grid=(m // tm,),
        in_specs=[pl.BlockSpec((tm, d), lambda i: (i, 0)),
                  pl.BlockSpec((tm, ka), lambda i: (i, 0)),
                  pl.BlockSpec((tm, kb), lambda i: (i, 0)),
                  pl.BlockSpec((ka + kb, d), lambda i: (0, 0))],
        out_specs=pl.BlockSpec((tm, d), lambda i: (i, 0)),
        out_shape=jax.ShapeDtypeStruct((m, d), F32),
        compiler_params=_cparams(1),
        name="out_proj",
    )(h, a, b, w)


def _ffn_kernel(te_ref, tv_ref, idx_ref, x_ref, g_ref, gate_ref, w1_ref, w3_ref, w2_ref, o_ref,
                xn_ref, acc_ref, *gather_scratch, residual):
    i = pl.program_id(0)
    j = pl.program_id(1)

    if gather_scratch:
        xbuf, sem = gather_scratch
        tm = xbuf.shape[1]

        def issue(tile, slot):
            def start(k2, carry):
                for q in range(2):
                    k = 2 * k2 + q
                    pltpu.make_async_copy(x_ref.at[idx_ref[tile * tm + k]], xbuf.at[slot, k],
                                          sem.at[slot]).start(priority=q)
                return carry
            lax.fori_loop(0, tm // 2, start, 0, unroll=4)

        @pl.when((j == 0) & (i == 0))
        def _():
            issue(0, 0)

        @pl.when(j == 0)
        def _():
            slot = i % 2
            pltpu.make_async_copy(x_ref.at[pl.ds(0, tm)], xbuf.at[slot], sem.at[slot]).wait()

            @pl.when(i + 1 < pl.num_programs(0))
            def _():
                issue(i + 1, 1 - slot)

            x = jnp.concatenate([xbuf[slot, :, s, :] for s in range(xbuf.shape[2])], axis=1)
            xn_ref[...] = _rms(x, g_ref[...]).astype(BF16)
            acc_ref[...] = jnp.zeros_like(acc_ref)
    else:
        @pl.when(j == 0)
        def _():
            xn_ref[...] = _rms(x_ref[...], g_ref[...]).astype(BF16)
            acc_ref[...] = jnp.zeros_like(acc_ref)

    @pl.when(tv_ref[i] != 0)
    def _():
        xn = xn_ref[...]
        a = _dot(xn, w1_ref[...].astype(BF16))
        b = _dot(xn, w3_ref[...].astype(BF16))
        hid = (a * _sigmoid(a)) * b
        acc_ref[...] += _dot(hid.astype(BF16), w2_ref[...].astype(BF16))

    @pl.when(j == pl.num_programs(1) - 1)
    def _():
        y = acc_ref[...] * gate_ref[...]
        if residual:
            y = y + x_ref[...]
        o_ref[...] = y


def grouped_ffn(x, g, row_gate, tile_expert, tile_valid, w1, w3, w2, *, tm, tf, residual, row_idx=None):
    gather = row_idx is not None
    m = row_gate.shape[0]
    d = w1.shape[1]
    f = w1.shape[2]
    assert not (residual and gather)
    if gather:
        assert x.shape[1:] == (SUBLANES, d // SUBLANES)
        x_spec = pl.BlockSpec(memory_space=pl.ANY)
        scratch = [pltpu.VMEM((2, tm) + x.shape[1:], F32), pltpu.SemaphoreType.DMA((2,))]
    else:
        row_idx = jnp.zeros((1,), I32)
        x_spec = pl.BlockSpec((tm, d), lambda i, j, *_: (i, 0))
        scratch = []
    grid_spec = pltpu.PrefetchScalarGridSpec(
        num_scalar_prefetch=3,
        grid=(m // tm, f // tf),
        in_specs=[x_spec,
                  pl.BlockSpec((1, d), lambda i, j, *_: (0, 0)),
                  pl.BlockSpec((tm, 1), lambda i, j, *_: (i, 0)),
                  pl.BlockSpec((None, d, tf), lambda i, j, te, *_: (te[i], 0, j)),
                  pl.BlockSpec((None, d, tf), lambda i, j, te, *_: (te[i], 0, j)),
                  pl.BlockSpec((None, tf, d), lambda i, j, te, *_: (te[i], j, 0))],
        out_specs=pl.BlockSpec((tm, d), lambda i, j, *_: (i, 0)),
        scratch_shapes=[pltpu.VMEM((tm, d), BF16), pltpu.VMEM((tm, d), F32)] + scratch,
    )
    return pl.pallas_call(
        functools.partial(_ffn_kernel, residual=residual),
        grid_spec=grid_spec,
        out_shape=jax.ShapeDtypeStruct((m, d), F32),
        compiler_params=_cparams(2),
        name="grouped_ffn",
    )(tile_expert, tile_valid, row_idx, x, g.reshape(1, d), row_gate, w1, w3, w2)


def _ple_kernel(h_ref, g_ref, wg_ref, p_ref, wp_ref, gf_ref, o_ref, *, final):
    h = h_ref[...]
    gate = _sigmoid(_dot(_rms(h, g_ref[...]).astype(BF16), wg_ref[...]))
    proj = _dot(p_ref[...].astype(BF16), wp_ref[...])
    h2 = h + gate * proj
    o_ref[...] = _rms(h2, gf_ref[...]) if final else h2


def ple(h, g, wg, p, wp, gf, final):
    m, d = h.shape
    pd = p.shape[1]
    tm = _row_tile(m)
    row = lambda i: (i, 0)
    fixed = lambda i: (0, 0)
    return pl.pallas_call(
        functools.partial(_ple_kernel, final=final),
        grid=(m // tm,),
        in_specs=[pl.BlockSpec((tm, d), row), pl.BlockSpec((1, d), fixed),
                  pl.BlockSpec((d, d), fixed), pl.BlockSpec((tm, pd), row),
                  pl.BlockSpec((pd, d), fixed), pl.BlockSpec((1, d), fixed)],
        out_specs=pl.BlockSpec((tm, d), row),
        out_shape=jax.ShapeDtypeStruct((m, d), F32),
        compiler_params=_cparams(1),
        name="ple",
    )(h, g.reshape(1, d), wg, p, wp, gf.reshape(1, d))


def _bias_kernel(tab_ref, d_ref, o_ref):
    n = jnp.maximum(d_ref[...], 0)
    max_exact = N_BUCKETS // 2
    nf = jnp.maximum(n, 1).astype(F32)
    large = max_exact + (jnp.log(nf / max_exact) / math.log(MAX_DISTANCE / max_exact)
                         * (N_BUCKETS - max_exact)).astype(I32)
    large = jnp.minimum(large, N_BUCKETS - 1)
    bucket = jnp.where(n < max_exact, n, large)
    for h in range(H_NSA):
        acc = jnp.zeros(n.shape, F32)
        for k in range(N_BUCKETS):
            acc = jnp.where(bucket == k, tab_ref[k, h], acc)
        o_ref[h] = acc


def bias_lookup(table, dist):
    r, c = dist.shape
    tr = min(r, 128)
    tc = 512 if c % 512 == 0 else c
    return pl.pallas_call(
        _bias_kernel,
        grid=(r // tr, c // tc),
        in_specs=[pl.BlockSpec(memory_space=pltpu.SMEM),
                  pl.BlockSpec((tr, tc), lambda i, j: (i, j))],
        out_specs=pl.BlockSpec((H_NSA, tr, tc), lambda i, j: (0, i, j)),
        out_shape=jax.ShapeDtypeStruct((H_NSA, r, c), F32),
        compiler_params=_cparams(2),
        name="bias_lookup",
    )(table, dist)


def _swap_halves(x):
    lane = lax.broadcasted_iota(I32, x.shape, 1)
    return jnp.where(lane % HEAD_DIM < HEAD_DIM // 2,
                     pltpu.roll(x, LANES - HEAD_DIM // 2, 1), pltpu.roll(x, HEAD_DIM // 2, 1))


def _retention_prompt_kernel(q_ref, k_ref, v_ref, g_ref, cos_ref, sin_ref, gain_ref, intra_ref,
                             qdec_ref, kdec_ref, cdec_ref, y_ref, s_out_ref, s_ref):
    c = pl.program_id(1)

    @pl.when(c == 0)
    def _():
        s_ref[...] = jnp.zeros_like(s_ref)

    cos = cos_ref[...]
    sin = sin_ref[...]
    lane = lax.broadcasted_iota(I32, (RET_CHUNK, LANES), 1)
    left = lane < HEAD_DIM
    row = lax.broadcasted_iota(I32, (LANES, LANES), 0)
    col = lax.broadcasted_iota(I32, (LANES, LANES), 1)
    same_head = (row < HEAD_DIM) == (col < HEAD_DIM)
    ones_bd = jnp.where(same_head, 1.0, 0.0).astype(BF16)
    for p in range(H_RET // 2):
        sl = slice(p * LANES, (p + 1) * LANES)
        q = q_ref[:, sl]
        k = k_ref[:, sl]
        v = v_ref[:, sl].astype(BF16)
        qr = q * cos + _swap_halves(q) * sin
        kr = (k * cos + _swap_halves(k) * sin) * (HEAD_DIM ** -0.5)
        qb = qr.astype(BF16)
        q2 = jnp.concatenate([jnp.where(left, qr, 0.0), jnp.where(left, 0.0, qr)], axis=0).astype(BF16)
        inner = _dot_nt(q2, kr.astype(BF16)) * intra_ref[p]
        o2 = _dot(inner.astype(BF16), v)
        s_old = s_ref[p]
        cross = _dot(qb, s_old.astype(BF16)) * qdec_ref[:, sl]
        o = jnp.where(left, o2[:RET_CHUNK], o2[RET_CHUNK:]) + cross
        kd = (kr * kdec_ref[:, sl]).T.astype(BF16)
        s_ref[p] = s_old * cdec_ref[p] + jnp.where(same_head, _dot(kd, v), 0.0)
        ms = _dot3(o * o, ones_bd) * (1.0 / HEAD_DIM)
        gate = g_ref[:, sl]
        y = o * lax.rsqrt(ms + EPS) * gain_ref[:, sl] * (gate * _sigmoid(gate))
        y_ref[:, sl] = y.astype(y_ref.dtype)

    @pl.when(c == pl.num_programs(1) - 1)
    def _():
        for p in range(H_RET // 2):
            s = s_ref[p]
            s_out_ref[2 * p] = s[:HEAD_DIM, :HEAD_DIM]
            s_out_ref[2 * p + 1] = s[HEAD_DIM:, HEAD_DIM:]


def _retention_consts(c):
    log_g = jnp.log1p(-jnp.exp2(-5.0 - jnp.arange(H_RET, dtype=F32)))
    j = jnp.arange(c, dtype=F32)
    diff = j[:, None] - j[None, :]
    intra = jnp.where(diff >= 0, jnp.exp(jnp.maximum(diff, 0.0)[None] * log_g[:, None, None]), 0.0)
    q_decay = jnp.exp((j[:, None] + 1.0) * log_g[None, :])
    k_decay = jnp.exp((c - 1.0 - j[:, None]) * log_g[None, :])
    c_decay = jnp.exp(c * log_g)
    intra2 = intra.reshape(H_RET // 2, 2 * c, c)
    qdec = jnp.repeat(q_decay, HEAD_DIM, axis=1)
    kdec = jnp.repeat(k_decay, HEAD_DIM, axis=1)
    cdec = jnp.repeat(c_decay.reshape(H_RET // 2, 2), HEAD_DIM, axis=1)[:, :, None] * jnp.ones((1, 1, LANES), F32)
    return intra2, qdec, kdec, cdec


def _rotary_tables(pos):
    half = HEAD_DIM // 2
    inv = ROPE_BASE ** (-jnp.arange(half, dtype=F32) / half)
    ang = pos.astype(F32)[:, None] * inv[None, :]
    cos, sin = jnp.cos(ang), jnp.sin(ang)
    cos_t = jnp.concatenate([cos, cos, cos, cos], axis=1)
    sin_t = jnp.concatenate([-sin, sin, -sin, sin], axis=1)
    return cos_t, sin_t


def retention_prompt(z3, gain):
    b, t, _ = z3.shape
    c = RET_CHUNK
    cos_t, sin_t = _rotary_tables(jnp.arange(t))
    intra2, qdec, kdec, cdec = _retention_consts(c)
    blk = lambda col: pl.BlockSpec((None, c, RET_W), lambda bi, ci, col=col: (bi, ci, col))
    fixed2 = lambda shape: pl.BlockSpec(shape, lambda bi, ci: (0,) * len(shape))
    return pl.pallas_call(
        _retention_prompt_kernel,
        grid=(b, t // c),
        in_specs=[blk(COL_RQ // RET_W), blk(COL_RK // RET_W), blk(COL_RV // RET_W), blk(COL_RG // RET_W),
                  pl.BlockSpec((c, LANES), lambda bi, ci: (ci, 0)),
                  pl.BlockSpec((c, LANES), lambda bi, ci: (ci, 0)),
                  fixed2((1, RET_W)), fixed2((H_RET // 2, 2 * c, c)), fixed2((c, RET_W)),
                  fixed2((c, RET_W)), fixed2((H_RET // 2, LANES, LANES))],
        out_specs=[pl.BlockSpec((None, c, RET_W), lambda bi, ci: (bi, ci, 0)),
                   pl.BlockSpec((None, H_RET, HEAD_DIM, HEAD_DIM), lambda bi, ci: (bi, 0, 0, 0))],
        out_shape=[jax.ShapeDtypeStruct((b, t, RET_W), BF16),
                   jax.ShapeDtypeStruct((b, H_RET, HEAD_DIM, HEAD_DIM), F32)],
        scratch_shapes=[pltpu.VMEM((H_RET // 2, LANES, LANES), F32)],
        compiler_params=_cparams(2),
        name="retention_prompt",
    )(z3, z3, z3, z3, cos_t, sin_t, gain.reshape(1, RET_W), intra2, qdec, kdec, cdec)


def _retention_step_kernel(q_ref, k_ref, v_ref, g_ref, s_ref, cos_ref, sin_ref, gamma_ref, gain_ref,
                           y_ref, s_out_ref):
    half = HEAD_DIM // 2
    cos = cos_ref[...]
    sin = sin_ref[...]

    def rot(x):
        x1, x2 = x[:half], x[half:]
        return jnp.concatenate([x1 * cos - x2 * sin, x1 * sin + x2 * cos], axis=0)

    qr = rot(q_ref[...])
    kr = rot(k_ref[...]) * (HEAD_DIM ** -0.5)
    v = v_ref[...]
    gamma = gamma_ref[...]
    o = jnp.zeros_like(v)
    for d in range(HEAD_DIM):
        s_new = s_ref[d] * gamma + kr[d:d + 1] * v
        s_out_ref[d] = s_new
        o = o + qr[d:d + 1] * s_new
    ms = jnp.mean(o * o, axis=0, keepdims=True)
    gate = g_ref[...]
    y_ref[...] = o * lax.rsqrt(ms + EPS) * gain_ref[...] * (gate * _sigmoid(gate))


def retention_step(zt, state, layer, pos, gain):
    bs = zt.shape[1]
    half = HEAD_DIM // 2
    inv = ROPE_BASE ** (-jnp.arange(half, dtype=F32) / half)
    ang = jnp.full((1,), pos).astype(F32)[:, None] * inv[None, :]
    cos = jnp.broadcast_to(jnp.cos(ang).reshape(half, 1), (half, bs))
    sin = jnp.broadcast_to(jnp.sin(ang).reshape(half, 1), (half, bs))
    gamma = 1.0 - jnp.exp2(-5.0 - jnp.arange(H_RET, dtype=F32))
    gamma = jnp.broadcast_to(gamma.reshape(H_RET, 1, 1), (H_RET, 1, bs))
    gain_b = jnp.broadcast_to(gain.reshape(H_RET, HEAD_DIM, 1), (H_RET, HEAD_DIM, bs))
    head = lambda col: pl.BlockSpec((HEAD_DIM, bs), lambda h, col=col: (col // HEAD_DIM + h, 0))
    fixed = pl.BlockSpec((half, bs), lambda h: (0, 0))
    return pl.pallas_call(
        _retention_step_kernel,
        grid=(H_RET,),
        in_specs=[head(COL_RQ), head(COL_RK), head(COL_RV), head(COL_RG),
                  pl.BlockSpec((None, HEAD_DIM, HEAD_DIM, bs), lambda h: (layer * H_RET + h, 0, 0, 0)),
                  fixed, fixed,
                  pl.BlockSpec((None, 1, bs), lambda h: (h, 0, 0)),
                  pl.BlockSpec((None, HEAD_DIM, bs), lambda h: (h, 0, 0))],
        out_specs=[pl.BlockSpec((HEAD_DIM, bs), lambda h: (h, 0)),
                   pl.BlockSpec((None, HEAD_DIM, HEAD_DIM, bs), lambda h: (h, 0, 0, 0))],
        out_shape=[jax.ShapeDtypeStruct((RET_W, bs), F32),
                   jax.ShapeDtypeStruct((H_RET, HEAD_DIM, HEAD_DIM, bs), F32)],
        compiler_params=_cparams(1),
        name="retention_step",
    )(zt, zt, zt, zt, state, cos, sin, gamma, gain_b)


def _gelu_tanh(x):
    return 0.5 * x * (1.0 + jnp.tanh(math.sqrt(2.0 / math.pi) * (x + 0.044715 * (x * x * x))))


def _compress_kernel(*refs, n_pieces):
    x_refs = refs[1:1 + n_pieces]
    pe_ref, w1_ref, w2_ref, kc_ref, vc_ref, rows_ref = refs[1 + n_pieces:]
    width = x_refs[0].shape[1]
    n = n_pieces * width // CMP_STRIDE
    for kv, o_ref in enumerate((kc_ref, vc_ref)):
        for p, x_ref in enumerate(x_refs):
            for c in range(width // LANES):
                rows_ref[pl.ds(p * width + c * LANES, LANES), :] = (
                    x_ref[kv * LANES:(kv + 1) * LANES, c * LANES:(c + 1) * LANES].T)
        xk = jnp.concatenate([rows_ref[pl.ds(j, n, stride=CMP_STRIDE), :] for j in range(CMP_STRIDE)], axis=1)
        top = _dot((xk + pe_ref[kv, 0]).astype(BF16), w1_ref[kv, 0])
        bot = _dot((xk + pe_ref[kv, 1]).astype(BF16), w1_ref[kv, 1])
        pre = top + pltpu.roll(bot, n - 1, 0)
        hid = _gelu_tanh(pre).astype(BF16)
        o_ref[...] = jnp.concatenate(
            [_dot(hid[:, g * CMP_HIDDEN:(g + 1) * CMP_HIDDEN], w2_ref[kv]) for g in range(N_KV)], axis=1)


def _compress_weights(pe, w1, w2):
    pe_r = pe.reshape(2, 2, CMP_STRIDE, 1, HEAD_DIM)
    pe_t = jnp.broadcast_to(pe_r, (2, 2, CMP_STRIDE, N_KV, HEAD_DIM)).reshape(2, 2, 1, CMP_STRIDE * LANES)
    w1r = w1.reshape(2, 2, CMP_STRIDE, HEAD_DIM, CMP_HIDDEN)
    eye = jnp.eye(N_KV, dtype=w1.dtype)
    wbig = jnp.einsum('khjdc,gG->khjgdGc', w1r, eye)
    wbig = wbig.reshape(2, 2, CMP_STRIDE * LANES, N_KV * CMP_HIDDEN)
    return pe_t.astype(F32), wbig.astype(BF16), w2.astype(BF16)


def compress(src, idx, pe_t, w1b, w2b):
    bsz, n_pieces = idx.shape
    width = src.shape[2]
    n = n_pieces * width // CMP_STRIDE
    piece = lambda p: pl.BlockSpec((None, KV_W, width), lambda b, ix, p=p: (ix[b, p], 0, 0))
    fixed = lambda shape: pl.BlockSpec(shape, lambda b, ix: (0,) * len(shape))
    out = pl.BlockSpec((None, n, LANES), lambda b, ix: (b, 0, 0))
    grid_spec = pltpu.PrefetchScalarGridSpec(
        num_scalar_prefetch=1,
        grid=(bsz,),
        in_specs=[piece(p) for p in range(n_pieces)]
        + [fixed(pe_t.shape), fixed(w1b.shape), fixed(w2b.shape)],
        out_specs=[out, out],
        scratch_shapes=[pltpu.VMEM((n_pieces * width, LANES), F32)],
    )
    return pl.pallas_call(
        functools.partial(_compress_kernel, n_pieces=n_pieces),
        grid_spec=grid_spec,
        out_shape=[jax.ShapeDtypeStruct((bsz, n, LANES), F32)] * 2,
        compiler_params=_cparams(1),
        name="compress",
    )(idx, *([src] * n_pieces), pe_t, w1b, w2b)


def _masked_softmax(s, mask):
    s = jnp.where(mask, s, NEG)
    m = jnp.max(s, axis=-1, keepdims=True)
    e = jnp.where(mask, jnp.exp(s - m), 0.0)
    return e / jnp.maximum(jnp.sum(e, axis=-1, keepdims=True), TINY)


def _block_scores(imp, qpos, n_sel, axis=1):
    blk = lax.broadcasted_iota(I32, imp.shape, axis)
    cur = qpos // SEL_BLOCK
    forced = (blk == 0) | (blk == cur) | (blk == cur - 1)
    valid = blk * SEL_BLOCK <= qpos
    score = jnp.where(forced, FORCED_SCORE, jnp.where(valid, imp, -1.0))
    return jnp.where(blk < n_sel, score, -2.0)


def _select_top(score, n_top, axis=1):
    blk = lax.broadcasted_iota(I32, score.shape, axis)
    rank = jnp.zeros(score.shape, F32)
    for i in range(score.shape[axis]):
        one = score[:, i:i + 1] if axis == 1 else score[i:i + 1, :]
        beats = (one > score) | ((one == score) & (blk > i))
        rank = rank + jnp.where(beats, 1.0, 0.0)
    return jnp.where(rank < n_top, 1.0, 0.0)


def _attend_t(s, ok, m_prev, l_prev):
    s = jnp.where(ok, s, NEG)
    m_new = jnp.maximum(m_prev, jnp.max(s, axis=0, keepdims=True))
    alpha = jnp.exp2(m_prev - m_new)
    p = jnp.exp2(s - m_new)
    l_new = alpha * l_prev + jnp.sum(p, axis=0, keepdims=True)
    return m_new, l_new, alpha, p


def _nsa_prompt_kernel(q_ref, gate_ref, kc_ref, vct_ref, ks_ref, vst_ref, kw_ref, vwt_ref, bc_ref, bt_ref,
                       o_ref, qt_ref, pcs_ref, pick_ref, acc_ref, os_ref, s_ref, p_ref, *, n_cmp, n_sel):
    i = pl.program_id(1)
    qs = i * Q_BLOCK
    key_i = lax.broadcasted_iota(I32, (KV_CHUNK, Q_BLOCK), 0)
    tok_i = lax.broadcasted_iota(I32, (KV_CHUNK, Q_BLOCK), 1)
    row8 = lax.broadcasted_iota(I32, (H_NSA, Q_BLOCK), 0)
    zeros_half = jnp.zeros((HEAD_DIM, Q_BLOCK), F32)

    for k2 in range(H_NSA // 2):
        slab_t = q_ref[:, k2 * LANES:(k2 + 1) * LANES].T * (HEAD_DIM ** -0.5 * LOG2E)
        for h in (2 * k2, 2 * k2 + 1):
            qh = slab_t[(h % 2) * HEAD_DIM:(h % 2 + 1) * HEAD_DIM]
            both = [qh, zeros_half] if h // R_GRP == 0 else [zeros_half, qh]
            qt_ref[:, h * Q_BLOCK:(h + 1) * Q_BLOCK] = jnp.concatenate(both, axis=0).astype(BF16)

    sig_t = _sigmoid(gate_ref[...]).T

    ncp = kc_ref.shape[0]
    kc = kc_ref[...].astype(BF16)
    n_c = lax.broadcasted_iota(I32, (ncp, Q_BLOCK), 0)
    t_c = lax.broadcasted_iota(I32, (ncp, Q_BLOCK), 1)
    ok_c = (n_c * CMP_STRIDE + (L_CMP - 1) <= qs + t_c) & (n_c < n_cmp)
    blk_pos = qs + lax.broadcasted_iota(I32, (pick_ref.shape[1], Q_BLOCK), 1)
    o_c = []
    for g in range(N_KV):
        vct = vct_ref[g * HEAD_DIM:(g + 1) * HEAD_DIM, :].astype(BF16)
        pcs = jnp.zeros((ncp, Q_BLOCK), F32)
        for r in range(R_GRP):
            h = g * R_GRP + r
            s = _dot(kc, qt_ref[:, h * Q_BLOCK:(h + 1) * Q_BLOCK]) + bc_ref[h]
            s = jnp.where(ok_c, s, NEG)
            e = jnp.where(ok_c, jnp.exp2(s - jnp.max(s, axis=0, keepdims=True)), 0.0)
            p = e * (1.0 / jnp.maximum(jnp.sum(e, axis=0, keepdims=True), TINY))
            o_c.append(_dot(vct, p.astype(BF16)))
            pcs = pcs + p
        pcs_ref[...] = pcs
        per = SEL_BLOCK // CMP_STRIDE
        imp = pcs_ref[pl.ds(0, ncp // per, stride=per), :]
        for c in range(1, per):
            imp = imp + pcs_ref[pl.ds(c, ncp // per, stride=per), :]
        pick_ref[g] = _select_top(_block_scores(imp, blk_pos, n_sel, axis=0), min(N_SELECT, n_sel), axis=0)

    dmat = tok_i - key_i

    def stream(k_ref, vt_ref, n_chunks, chunk_of, mask_of):
        def scores(c, slot):
            off, _ = chunk_of(c)
            k = k_ref[pl.ds(off, KV_CHUNK), :].astype(BF16)
            for h in range(H_NSA):
                s_ref[slot, h] = _dot(k, qt_ref[:, h * Q_BLOCK:(h + 1) * Q_BLOCK])

        def values(c, slot, alpha_all):
            off, _ = chunk_of(c)
            for g in range(N_KV):
                vt = vt_ref[g * HEAD_DIM:(g + 1) * HEAD_DIM, pl.ds(off, KV_CHUNK)].astype(BF16)
                for h in range(g * R_GRP, (g + 1) * R_GRP):
                    acc_ref[h] = acc_ref[h] * alpha_all[h:h + 1] + _dot(vt, p_ref[slot, h])

        def softmax(c, slot, m_all, l_all):
            _, rel = chunk_of(c)
            ok_g = mask_of(c, rel)
            bias_idx = min(max(rel, 0), FAR_REL) if isinstance(rel, int) else jnp.clip(rel, 0, FAR_REL)
            alpha_all = jnp.zeros((H_NSA, Q_BLOCK), F32)
            for h in range(H_NSA):
                s = s_ref[slot, h] + bt_ref[bias_idx, h]
                m_new, l_new, alpha, p = _attend_t(s, ok_g[h // R_GRP], m_all[h:h + 1], l_all[h:h + 1])
                p_ref[slot, h] = p.astype(BF16)
                m_all = jnp.where(row8 == h, m_new, m_all)
                l_all = jnp.where(row8 == h, l_new, l_all)
                alpha_all = jnp.where(row8 == h, alpha, alpha_all)
            return m_all, l_all, alpha_all

        def stage(c, slot, carry, prefetch=True):
            m_all, l_all, alpha_prev = carry
            if prefetch:
                scores(c + 1, 1 - slot)
            values(c - 1, 1 - slot, alpha_prev)
            return softmax(c, slot, m_all, l_all)

        acc_ref[...] = jnp.zeros_like(acc_ref)
        p_ref[1] = jnp.zeros(p_ref.shape[1:], BF16)
        scores(0, 0)
        carry = (jnp.full((H_NSA, Q_BLOCK), NEG, F32), jnp.zeros((H_NSA, Q_BLOCK), F32),
                 jnp.ones((H_NSA, Q_BLOCK), F32))
        if isinstance(n_chunks, int):
            for c in range(n_chunks):
                carry = stage(c, c % 2, carry, prefetch=c + 1 < n_chunks)
            last = n_chunks - 1
        else:
            pairs = (n_chunks + 1) // 2
            carry = lax.fori_loop(0, pairs, lambda t, cr: stage(2 * t + 1, 1, stage(2 * t, 0, cr)), carry)
            last = 2 * pairs - 1
        m_all, l_all, alpha_last = carry
        values(last, last % 2 if isinstance(last, int) else 1, alpha_last)
        return m_all, l_all

    def sel_chunk(c):
        return pl.multiple_of(jnp.clip(c, 0, i) * KV_CHUNK, KV_CHUNK), i - c

    def sel_mask(c, rel):
        causal = rel * KV_CHUNK + dmat >= 0
        per = KV_CHUNK // SEL_BLOCK
        first = jnp.minimum(c, i) * per
        ok_g = []
        for g in range(N_KV):
            picked = jnp.concatenate(
                [jnp.broadcast_to(pick_ref[g, pl.ds(first + b, 1), :], (SEL_BLOCK, Q_BLOCK)) for b in range(per)],
                axis=0)
            ok_g.append(causal & (picked > 0.5))
        return ok_g

    m_s, l_s = stream(ks_ref, vst_ref, i + 1, sel_chunk, sel_mask)
    for h in range(H_NSA):
        os_ref[h] = jnp.where(m_s[h:h + 1] > 0.5 * NEG, acc_ref[h] / jnp.maximum(l_s[h:h + 1], TINY), 0.0)

    n_back = WINDOW // KV_CHUNK

    def win_chunk(c):
        return pl.multiple_of(jnp.clip(i - n_back + c, 0, i) * KV_CHUNK, KV_CHUNK), n_back - c

    def win_mask(c, rel):
        dist = rel * KV_CHUNK + dmat + jnp.where(i - n_back + c < 0, WINDOW, 0)
        ok = (dist >= 0) & (dist < WINDOW)
        return [ok, ok]

    m_w, l_w = stream(kw_ref, vwt_ref, n_back + 1, win_chunk, win_mask)

    for k2 in range(H_NSA // 2):
        pair = []
        for h in (2 * k2, 2 * k2 + 1):
            o_w = jnp.where(m_w[h:h + 1] > 0.5 * NEG, acc_ref[h] / jnp.maximum(l_w[h:h + 1], TINY), 0.0)
            pair.append(sig_t[3 * h:3 * h + 1] * o_c[h] + sig_t[3 * h + 1:3 * h + 2] * os_ref[h]
                        + sig_t[3 * h + 2:3 * h + 3] * o_w)
        o_ref[:, k2 * LANES:(k2 + 1) * LANES] = jnp.concatenate(pair, axis=0).T.astype(o_ref.dtype)


def _selection_consts(n_rows_cmp, n_keys):
    n = np.arange(n_rows_cmp)
    blk = np.arange(SEL_LANES)
    pool = (n[:, None] // (SEL_BLOCK // CMP_STRIDE) == blk[None, :]).astype(np.float32)
    key = np.arange(n_keys)
    expand = (blk[:, None] == key[None, :] // SEL_BLOCK).astype(np.float32)
    return jnp.asarray(pool, BF16), jnp.asarray(expand, BF16)


def nsa_prompt(z3, kvt, kc, vct, bias_ct, bias_tt):
    b, t, _ = z3.shape
    n_sub = t // CMP_STRIDE
    n_sel = -(-t // SEL_BLOCK)
    assert kc.shape[1] == n_sub and n_sub == n_sel * (SEL_BLOCK // CMP_STRIDE) and n_sel % SUBLANES == 0
    fixed = lambda shape: pl.BlockSpec(shape, lambda bi, i: (0,) * len(shape))
    v_rows = lambda branch: (branch * KV_W + N_KV * HEAD_DIM) // LANES
    return pl.pallas_call(
        functools.partial(_nsa_prompt_kernel, n_cmp=n_sub - 1, n_sel=n_sel),
        grid=(b, t // Q_BLOCK),
        in_specs=[pl.BlockSpec((None, Q_BLOCK, NSA_QW), lambda bi, i: (bi, i, COL_NQ // NSA_QW)),
                  pl.BlockSpec((None, Q_BLOCK, LANES), lambda bi, i: (bi, i, COL_NG // LANES)),
                  pl.BlockSpec((None, n_sub, LANES), lambda bi, i: (bi, 0, 0)),
                  pl.BlockSpec((None, LANES, n_sub), lambda bi, i: (bi, 0, 0)),
                  pl.BlockSpec((None, t, LANES), lambda bi, i: (bi, 0, COL_KVS // LANES)),
                  pl.BlockSpec((None, LANES, t), lambda bi, i: (bi, v_rows(1), 0)),
                  pl.BlockSpec((None, t, LANES), lambda bi, i: (bi, 0, COL_KVW // LANES)),
                  pl.BlockSpec((None, LANES, t), lambda bi, i: (bi, v_rows(2), 0)),
                  pl.BlockSpec((H_NSA, n_sub, Q_BLOCK), lambda bi, i: (0, 0, i)),
                  fixed(bias_tt.shape)],
        out_specs=pl.BlockSpec((None, Q_BLOCK, NSA_QW), lambda bi, i: (bi, i, 0)),
        out_shape=jax.ShapeDtypeStruct((b, t, NSA_QW), BF16),
        scratch_shapes=[pltpu.VMEM((LANES, H_NSA * Q_BLOCK), BF16),
                        pltpu.VMEM((n_sub, Q_BLOCK), F32),
                        pltpu.VMEM((N_KV, n_sel, Q_BLOCK), F32),
                        pltpu.VMEM((H_NSA, HEAD_DIM, Q_BLOCK), F32),
                        pltpu.VMEM((H_NSA, HEAD_DIM, Q_BLOCK), F32),
                        pltpu.VMEM((2, H_NSA, KV_CHUNK, Q_BLOCK), F32),
                        pltpu.VMEM((2, H_NSA, KV_CHUNK, Q_BLOCK), BF16)],
        compiler_params=_cparams(2),
        name="nsa_prompt",
    )(z3, z3, kc, vct, z3, kvt, z3, kvt, bias_ct, bias_tt)


def _nsa_step_kernel(pt_ref, z_ref, kc_ref, vc_ref, *rest, n_pages, per_step, **static):
    del pt_ref
    pages = rest[:per_step * n_pages]
    win_ref, bcs_ref, bss_ref, bws_ref, pool_ref, exp_ref, o_ref, wo_ref = rest[per_step * n_pages:]
    for u in range(per_step):
        _nsa_step_one(z_ref.at[u], kc_ref.at[u], vc_ref.at[u], pages[u * n_pages:(u + 1) * n_pages], win_ref.at[u],
                      bcs_ref, bss_ref, bws_ref, pool_ref, exp_ref, o_ref.at[u], wo_ref.at[u], **static)


def _nsa_step_one(z_ref, kc_ref, vc_ref, pages, win_ref, bcs_ref, bss_ref, bws_ref, pool_ref, exp_ref, o_ref, wo_ref,
                  *, past, wbuf, n_cmp, n_sel):
    z = z_ref[...]
    row8 = lax.broadcasted_iota(I32, (H_NSA, LANES), 0)
    lane8 = lax.broadcasted_iota(I32, (H_NSA, LANES), 1)
    left8 = lane8 < HEAD_DIM
    qbd = jnp.zeros((H_NSA, LANES), F32)
    for h in range(H_NSA):
        slab = jnp.broadcast_to(z[:, COL_NQ + (h // 2) * LANES:COL_NQ + (h // 2 + 1) * LANES], (H_NSA, LANES))
        g = h // R_GRP
        if h % 2 != g:
            slab = pltpu.roll(slab, HEAD_DIM, 1)
        keep = (row8 == h) & (left8 if g == 0 else jnp.logical_not(left8))
        qbd = jnp.where(keep, slab, qbd)
    qbd = (qbd * (HEAD_DIM ** -0.5)).astype(BF16)

    ncp = kc_ref.shape[0]
    n_c = lax.broadcasted_iota(I32, (H_NSA, ncp), 1)
    s_c = _dot_nt(qbd, kc_ref[...].astype(BF16)) + bcs_ref[...]
    p_c = _masked_softmax(s_c, (n_c * CMP_STRIDE + (L_CMP - 1) <= past) & (n_c < n_cmp))
    o_c = _dot(p_c.astype(BF16), vc_ref[...].astype(BF16))
    pc0 = jnp.sum(p_c[0:R_GRP], axis=0, keepdims=True)
    pc1 = jnp.sum(p_c[R_GRP:], axis=0, keepdims=True)
    rowc = lax.broadcasted_iota(I32, (H_NSA, ncp), 0)
    pcs = jnp.where(rowc < R_GRP, jnp.broadcast_to(pc0, (H_NSA, ncp)), jnp.broadcast_to(pc1, (H_NSA, ncp)))
    imp = _dot3(pcs, pool_ref[...])
    sel = _select_top(_block_scores(imp, jnp.full(imp.shape, past, I32), n_sel), min(N_SELECT, n_sel))
    picked = _dot(sel.astype(BF16), exp_ref[...])

    c128 = lax.broadcasted_iota(I32, (LANES, KV_CHUNK), 1)

    def new_col(col):
        return jnp.broadcast_to(z[:, col:col + LANES], (KV_CHUNK, LANES)).T

    def first_col(tile):
        return jnp.where(c128 == 0, tile, 0.0).astype(BF16)

    ks = [pg[0:LANES, :].astype(BF16) for pg in pages] + [first_col(new_col(COL_KVS))]
    vs = [pg[LANES:2 * LANES, :].astype(BF16) for pg in pages] + [first_col(new_col(COL_KVS + LANES))]
    s_s = jnp.concatenate([_dot(qbd, k) for k in ks], axis=1) + bss_ref[...]
    key = lax.broadcasted_iota(I32, s_s.shape, 1)
    p_s = _masked_softmax(s_s, (picked > 0.5) & (key <= past)).astype(BF16)
    o_s = _dot_nt(p_s[:, 0:KV_CHUNK], vs[0])
    for c in range(1, len(vs)):
        o_s = o_s + _dot_nt(p_s[:, c * KV_CHUNK:(c + 1) * KV_CHUNK], vs[c])

    kw_new, vw_new = new_col(COL_KVW), new_col(COL_KVW + LANES)
    win = win_ref[...]
    s_w = jnp.concatenate([_dot(qbd, win[0:LANES].astype(BF16)), _dot(qbd, first_col(kw_new))], axis=1) + bws_ref[...]
    colw = lax.broadcasted_iota(I32, s_w.shape, 1)
    p_w = _masked_softmax(s_w, (colw <= wbuf) & (wbuf - colw < WINDOW)).astype(BF16)
    o_w = _dot_nt(p_w[:, 0:wbuf], win[LANES:].astype(BF16)) + _dot_nt(p_w[:, wbuf:], first_col(vw_new))

    shifted = pltpu.roll(win, wbuf - 1, 1)
    new_kv = jnp.concatenate([kw_new, vw_new], axis=0)
    last = jnp.where(lax.broadcasted_iota(I32, (KV_W, LANES), 1) == LANES - 1, new_kv, shifted[:, wbuf - LANES:])
    wo_ref[:, 0:wbuf - LANES] = shifted[:, 0:wbuf - LANES]
    wo_ref[:, wbuf - LANES:] = last

    sig = jnp.broadcast_to(_sigmoid(z[:, COL_NG:COL_NG + LANES]), (H_NSA, LANES))

    def gate_col(c):
        return jnp.sum(jnp.where(lane8 == row8 * 3 + c, sig, 0.0), axis=1, keepdims=True)

    o = gate_col(0) * o_c + gate_col(1) * o_s + gate_col(2) * o_w
    fix = jnp.where((row8 % 2) != (row8 // R_GRP), pltpu.roll(o, HEAD_DIM, 1), o)
    left1 = left8[0:1]
    o_ref[...] = jnp.concatenate(
        [jnp.where(left1, fix[2 * k2:2 * k2 + 1], fix[2 * k2 + 1:2 * k2 + 2]) for k2 in range(H_NSA // 2)], axis=1)


def nsa_step(zs3, kc, vc, sel_cache, win_cache, page_idx, win_row0, bias_cs, bias_ss, bias_ws, past):
    bs = zs3.shape[0]
    n_pages = page_idx.shape[1]
    wbuf = win_cache.shape[2]
    n_sub = (past + 1) // CMP_STRIDE
    n_sel = -(-(past + 1) // SEL_BLOCK)
    n_keys = (n_pages + 1) * PAGE_SIZE
    assert n_sel <= SEL_LANES and PAGE_SIZE == KV_CHUNK and wbuf % KV_CHUNK == 0 and kc.shape[1] == n_sub
    pool, expand = _selection_consts(n_sub, n_keys)
    per = 2 if bs % 2 == 0 and win_row0 % 2 == 0 else 1
    fixed = lambda shape: pl.BlockSpec(shape, lambda b, pt: (0,) * len(shape))
    page = lambda u, p: pl.BlockSpec((None, KV_W, PAGE_SIZE), lambda b, pt, u=u, p=p: (pt[per * b + u, p], 0, 0))
    grid_spec = pltpu.PrefetchScalarGridSpec(
        num_scalar_prefetch=1,
        grid=(bs // per,),
        in_specs=[pl.BlockSpec((per, 1, N_IN_PAD), lambda b, pt: (b, 0, 0)),
                  pl.BlockSpec((per, n_sub, LANES), lambda b, pt: (b, 0, 0)),
                  pl.BlockSpec((per, n_sub, LANES), lambda b, pt: (b, 0, 0))]
        + [page(u, p) for u in range(per) for p in range(n_pages)]
        + [pl.BlockSpec((per, KV_W, wbuf), lambda b, pt: (win_row0 // per + b, 0, 0)),
           fixed(bias_cs.shape), fixed(bias_ss.shape), fixed(bias_ws.shape), fixed(pool.shape), fixed(expand.shape)],
        out_specs=[pl.BlockSpec((per, 1, NSA_QW), lambda b, pt: (b, 0, 0)),
                   pl.BlockSpec((per, KV_W, wbuf), lambda b, pt: (b, 0, 0))],
    )
    return pl.pallas_call(
        functools.partial(_nsa_step_kernel, n_pages=n_pages, per_step=per, past=past, wbuf=wbuf, n_cmp=n_sub - 1,
                          n_sel=n_sel),
        grid_spec=grid_spec,
        out_shape=[jax.ShapeDtypeStruct((bs, 1, NSA_QW), F32), jax.ShapeDtypeStruct((bs, KV_W, wbuf), F32)],
        compiler_params=_cparams(1),
        name="nsa_step",
    )(page_idx, zs3, kc, vc, *([sel_cache] * (per * n_pages)), win_cache, bias_cs, bias_ss, bias_ws, pool, expand)


def _route(logits, tm):
    m = logits.shape[0]
    top_v, top_i = lax.top_k(logits, TOP_K)
    top_w = jax.nn.softmax(top_v, axis=-1)
    slot_e = top_i.reshape(-1)
    onehot = (slot_e[:, None] == jnp.arange(N_EXPERTS)[None, :]).astype(I32)
    before = jnp.cumsum(onehot, axis=0) - onehot
    rank = jnp.sum(before * onehot, axis=1)
    count = jnp.sum(onehot, axis=0)
    padded = -(-count // tm) * tm
    start = jnp.cumsum(padded) - padded
    pos = start[slot_e] + rank
    n_rows = (-(-(m * TOP_K) // tm) + N_EXPERTS) * tm
    row_slot = jnp.full((n_rows,), -1, I32).at[pos].set(jnp.arange(m * TOP_K, dtype=I32))
    slot = jnp.maximum(row_slot, 0)
    row_token = slot // TOP_K
    row_gate = jnp.where(row_slot >= 0, top_w.reshape(-1)[slot], 0.0)
    tile_start = jnp.arange(n_rows // tm, dtype=I32) * tm
    tile_expert = jnp.clip(jnp.searchsorted(jnp.cumsum(padded), tile_start, side='right'), 0, N_EXPERTS - 1)
    tile_valid = (tile_start < jnp.sum(padded)).astype(I32)
    return pos.reshape(m, TOP_K), row_token, row_gate.reshape(n_rows, 1), tile_expert.astype(I32), tile_valid


def _dense_ffn(h, g, w1, w3, w2):
    m = h.shape[0]
    tm = _row_tile(m)
    ones = jnp.ones((m // tm,), I32)
    return grouped_ffn(h, g, jnp.ones((m, 1), F32), jnp.zeros((m // tm,), I32), ones,
                       w1[None], w3[None], w2[None], tm=tm, tf=w1.shape[1] // 2, residual=True)


def kernel(x_prompt, x_sample, p_prompt, p_sample, state_ret, cache_win_kv, cache_cmp_kv, cache_sel_kv, page_table, rel_bias, norm_mix, w_in, ret_gain, cmp_pe, cmp_w1, cmp_w2, w_out, norm_ffn, ffn_w1, ffn_w3, ffn_w2, router, moe_w1, moe_w3, moe_w2, ple_norm, ple_gate, ple_proj, norm_final):
    b_p, t_p, d = x_prompt.shape
    b_s = x_sample.shape[0]
    depth = w_in.shape[0]
    n_pages = page_table.shape[1]
    past = n_pages * PAGE_SIZE
    n_pool = cache_cmp_kv.shape[1]
    w_buf = cache_win_kv.shape[2]
    assert x_sample.shape[1] == 1 and t_p % Q_BLOCK == 0 and t_p >= w_buf and past % CMP_STRIDE == 0
    row = (2, N_KV, HEAD_DIM)
    m_p = b_p * t_p

    ar = lambda n: jnp.arange(n, dtype=I32)
    n_sub_p = t_p // CMP_STRIDE
    bias_ct = bias_lookup(rel_bias, ar(t_p)[None, :] - (ar(n_sub_p)[:, None] * CMP_STRIDE + L_CMP - 1))
    rel_t = (ar(FAR_REL + 1)[:, None, None] * KV_CHUNK + ar(Q_BLOCK)[None, None, :] - ar(KV_CHUNK)[None, :, None])
    bias_tt = bias_lookup(rel_bias, rel_t.reshape((FAR_REL + 1) * KV_CHUNK, Q_BLOCK))
    bias_tt = bias_tt.reshape(H_NSA, FAR_REL + 1, KV_CHUNK, Q_BLOCK).transpose(1, 0, 2, 3) * LOG2E
    bias_ct = bias_ct * LOG2E
    n_sub_s = (past + 1) // CMP_STRIDE
    rows8 = jnp.zeros((SUBLANES, 1), I32)
    bias_cs = bias_lookup(rel_bias, rows8 + (past - (ar(n_sub_s)[None, :] * CMP_STRIDE + L_CMP - 1)))[:, 0]
    bias_ss = bias_lookup(rel_bias, rows8 + (past - ar((n_pages + 1) * PAGE_SIZE)[None, :]))[:, 0]
    bias_ws = bias_lookup(rel_bias, rows8 + (w_buf - ar(w_buf + KV_CHUNK)[None, :]))[:, 0]

    def rows_last(x):
        return jnp.moveaxis(x, 2, -1).reshape(x.shape[0] * x.shape[1], KV_W, x.shape[2])

    def rows_first(xt, lead):
        return jnp.moveaxis(xt.reshape((lead,) + row + (xt.shape[-1],)), -1, 1)

    cmp_cache = rows_last(cache_cmp_kv)
    sel_cache = rows_last(cache_sel_kv)
    win_cache = rows_last(cache_win_kv)
    state_t = jnp.moveaxis(state_ret, 1, -1).reshape(depth * H_RET, HEAD_DIM, HEAD_DIM, b_s)

    h_p = x_prompt.reshape(m_p, d)
    h_s = x_sample.reshape(b_s, d)
    rp, rs, wp, ws, cp, cs, sp, ss = [], [], [], [], [], [], [], []
    for i in range(depth):
        w_in_b = jnp.pad(w_in[i], ((0, 0), (0, N_IN_PAD - N_IN))).astype(BF16)
        w_out_b = w_out[i].astype(BF16)
        pe_t, cw1, cw2 = _compress_weights(cmp_pe[i], cmp_w1[i], cmp_w2[i])

        w_kvt = w_in[i][:, COL_KVC:COL_KVC + 3 * KV_W].T.astype(BF16)
        z, kvt = rms_matmul_kvt(h_p, norm_mix[i], w_in_b, w_kvt, b_p)
        z3 = z.reshape(b_p, t_p, N_IN_PAD)
        ret_y, ret_state = retention_prompt(z3, ret_gain[i])
        kc, vc = compress(kvt, ar(b_p)[:, None], pe_t, cw1, cw2)
        nsa_y = nsa_prompt(z3, kvt, kc, jnp.swapaxes(vc, 1, 2), bias_ct, bias_tt)
        h_p = out_proj(h_p, ret_y.reshape(m_p, RET_W), nsa_y.reshape(m_p, NSA_QW), w_out_b)
        rp.append(ret_state)
        cp.append(rows_first(kvt[:, 0:KV_W], b_p))
        sp.append(rows_first(kvt[:, KV_W:2 * KV_W], b_p))
        wp.append(rows_first(kvt[:, 2 * KV_W:, t_p - w_buf:], b_p))

        zs = rms_matmul(h_s, norm_mix[i], w_in_b)
        zst = zs.T
        ret_yst, state_s = retention_step(zst, state_t, i, past, ret_gain[i])
        page_idx = page_table + i * n_pool
        grp = 4 if b_s % 4 == 0 else 1
        kc_s, vc_s = compress(cmp_cache, page_idx.reshape(b_s // grp, grp * n_pages), pe_t, cw1, cw2)
        kc_s, vc_s = kc_s.reshape(b_s, n_sub_s, LANES), vc_s.reshape(b_s, n_sub_s, LANES)
        nsa_ys, win_s = nsa_step(zs.reshape(b_s, 1, N_IN_PAD), kc_s, vc_s, sel_cache, win_cache, page_idx, i * b_s,
                                 bias_cs, bias_ss, bias_ws, past)
        h_s = out_proj(h_s, ret_yst.T, nsa_ys.reshape(b_s, NSA_QW), w_out_b)
        rs.append(jnp.moveaxis(state_s, -1, 0))
        ws.append(rows_first(win_s, b_s))
        cs.append(rows_first(zst[None, COL_KVC:COL_KVC + KV_W], 1).reshape((b_s, 1) + row))
        ss.append(rows_first(zst[None, COL_KVS:COL_KVS + KV_W], 1).reshape((b_s, 1) + row))

        j = i // 2
        if i % 2 == 0:
            w1, w3, w2 = ffn_w1[j].astype(BF16), ffn_w3[j].astype(BF16), ffn_w2[j].astype(BF16)
            h_p = _dense_ffn(h_p, norm_ffn[i], w1, w3, w2)
            h_s = _dense_ffn(h_s, norm_ffn[i], w1, w3, w2)
        else:
            router_w = jnp.pad(router[j], ((0, 0), (0, LANES - N_EXPERTS)))
            tf = moe_w1.shape[3] // 7
            logits_p = router_logits(h_p, norm_ffn[i], router_w)
            tm = 1024
            pos, row_token, row_gate, tile_expert, tile_valid = _route(logits_p[:, :N_EXPERTS], tm)
            y_rows = grouped_ffn(h_p.reshape(m_p, SUBLANES, d // SUBLANES), norm_ffn[i], row_gate, tile_expert,
                                 tile_valid, moe_w1[j], moe_w3[j], moe_w2[j], tm=tm, tf=tf, residual=False,
                                 row_idx=row_token)
            h_p = h_p + y_rows[pos[:, 0]] + y_rows[pos[:, 1]]
            logits_s = router_logits(h_s, norm_ffn[i], router_w)
            top_v, top_i = lax.top_k(logits_s[:, :N_EXPERTS], TOP_K)
            top_w = jax.nn.softmax(top_v, axis=-1)
            gates = jnp.sum(jax.nn.one_hot(top_i, N_EXPERTS, dtype=F32) * top_w[..., None], axis=-2)
            y_e = grouped_ffn(jnp.tile(h_s, (N_EXPERTS, 1)), norm_ffn[i], gates.T.reshape(N_EXPERTS * b_s, 1),
                              jnp.arange(N_EXPERTS, dtype=I32), jnp.ones((N_EXPERTS,), I32),
                              moe_w1[j], moe_w3[j], moe_w2[j], tm=b_s, tf=tf, residual=False)
            h_s = h_s + jnp.sum(y_e.reshape(N_EXPERTS, b_s, d), axis=0)

        wg, wpj = ple_gate[i].astype(BF16), ple_proj[i].astype(BF16)
        h_p = ple(h_p, ple_norm[i], wg, p_prompt[i].reshape(m_p, P_DIM), wpj, norm_final, i == depth - 1)
        h_s = ple(h_s, ple_norm[i], wg, p_sample[i].reshape(b_s, P_DIM), wpj, norm_final, i == depth - 1)

    return (h_p.reshape(b_p, t_p, d), h_s.reshape(b_s, 1, d), jnp.stack(rp), jnp.stack(rs), jnp.stack(wp),
            jnp.stack(ws), jnp.stack(cp), jnp.stack(cs), jnp.stack(sp), jnp.stack(ss))
```
